```python
import math
import jax, jax.numpy as jnp
from jax import lax
import numpy as np

D_MODEL = 1024
BATCH = 8
SEQ = 2048
DEPTH = 4

CHUNK = 64
N_A = DEPTH // 2
N_B = DEPTH - N_A
D_FF = 2816
GMLP_WIDTH = 4 * D_MODEL
GMLP_HALF = GMLP_WIDTH // 2
GMLP_WINDOW = 128
GMLP_GROUPS = 8
GMLP_GROUP_DIM = GMLP_HALF // GMLP_GROUPS
N_HEADS = 16
HEAD_DIM = D_MODEL // N_HEADS
LEFT_CHUNKS = 8
BAND = (LEFT_CHUNKS + 1) * CHUNK
LEFT_PAD = LEFT_CHUNKS * CHUNK
MAX_REL = 4 * CHUNK
N_REL = (CHUNK - 1) + MAX_REL + 1
ALPHA = (2.0 * DEPTH) ** 0.25
BETA = (8.0 * DEPTH) ** -0.25
LN_EPS = 1e-5
N_MOD = 9

kernel_name = "hybrid_gmlp_yoco_chunk_attention_encoder"


def layer_norm(x, g, b):
    xf = x.astype(jnp.float32)
    mu = jnp.mean(xf, axis=-1, keepdims=True)
    var = jnp.mean(jnp.square(xf - mu), axis=-1, keepdims=True)
    return ((xf - mu) * lax.rsqrt(var + LN_EPS)).astype(x.dtype) * g + b


def swiglu(h, w_gu, w_down):
    gu = h @ w_gu
    g, u = jnp.split(gu, 2, axis=-1)
    return (jax.nn.silu(g) * u) @ w_down


def gmlp_mixer(h, w_in, b_in, ln_g, ln_b, w_s, b_s, w_out):
    B, S, _ = h.shape
    z = jax.nn.gelu(h @ w_in + b_in, approximate=False)
    u, v = jnp.split(z, 2, axis=-1)
    v = layer_norm(v, ln_g, ln_b)
    v = v.reshape(B, S // GMLP_WINDOW, GMLP_WINDOW, GMLP_GROUPS, GMLP_GROUP_DIM)
    t = np.arange(GMLP_WINDOW)
    mask = ((t[None, :] // CHUNK) <= (t[:, None] // CHUNK)).astype(np.float32)
    ws = w_s * jnp.asarray(mask, dtype=w_s.dtype)[None]
    s = jnp.einsum('gts,bnsgc->bntgc', ws, v) + b_s.T[None, None, :, :, None]
    return (u * s.reshape(B, S, GMLP_HALF)) @ w_out


def chunk_band_attention(h, w_q, rel_bias, w_o, k_pad, v_pad):
    B, S, _ = h.shape
    n_chunks = S // CHUNK
    q = (h @ w_q).reshape(B, n_chunks, CHUNK, N_HEADS, HEAD_DIM)
    q = jnp.transpose(q, (1, 0, 2, 3, 4))
    t = np.arange(CHUNK)
    r = np.arange(BAND)
    dist = t[:, None] + LEFT_PAD - r[None, :]
    idx = np.clip(dist, -(CHUNK - 1), MAX_REL) + (CHUNK - 1)
    bias = rel_bias[:, idx].astype(jnp.float32)
    scale = HEAD_DIM ** -0.5
    r_j = jnp.arange(BAND, dtype=jnp.int32)

    def one_chunk(args):
        n, qn = args
        start = n * CHUNK
        kn = lax.dynamic_slice_in_dim(k_pad, start, BAND, axis=1)
        vn = lax.dynamic_slice_in_dim(v_pad, start, BAND, axis=1)
        valid = (start - LEFT_PAD + r_j) >= 0
        sc = jnp.einsum('bthd,brhd->bhtr', qn, kn).astype(jnp.float32) * scale + bias
        sc = jnp.where(valid[None, None, None, :], sc, -jnp.inf)
        p = jax.nn.softmax(sc, axis=-1).astype(vn.dtype)
        return jnp.einsum('bhtr,brhd->bthd', p, vn)

    out = lax.map(one_chunk, (jnp.arange(n_chunks, dtype=jnp.int32), q))
    out = jnp.transpose(out, (1, 0, 2, 3, 4)).reshape(B, S, D_MODEL)
    return out @ w_o


def _fwd_setup_inputs(seed: int = 0) -> dict:
    key = jax.random.key(seed)
    ks = jax.random.split(key, 24)
    D = D_MODEL
    f32 = jnp.float32
    nrm = lambda k, shape: jax.random.normal(k, shape, dtype=f32)
    v_scale = jnp.concatenate([jnp.ones((D,), f32), jnp.full((D,), BETA, f32)])
    return {
        "x": nrm(ks[0], (BATCH, SEQ, D)),
        "c": nrm(ks[1], (BATCH, D)),
        "w_ada": nrm(ks[2], (DEPTH, D, N_MOD * D)) * (0.1 * D ** -0.5),
        "b_ada": nrm(ks[3], (DEPTH, N_MOD * D)) * 0.01,
        "ln_g": 1.0 + 0.01 * nrm(ks[4], (DEPTH, 3, D)),
        "ln_b": 0.01 * nrm(ks[5], (DEPTH, 3, D)),
        "ffn_gu": nrm(ks[6], (DEPTH, 2, D, 2 * D_FF)) * D ** -0.5,
        "ffn_down": nrm(ks[7], (DEPTH, 2, D_FF, D)) * (BETA * D_FF ** -0.5),
        "gmlp_w_in": nrm(ks[8], (N_A, D, GMLP_WIDTH)) * D ** -0.5,
        "gmlp_b_in": 0.01 * nrm(ks[9], (N_A, GMLP_WIDTH)),
        "gmlp_ln_g": 1.0 + 0.01 * nrm(ks[10], (N_A, GMLP_HALF)),
        "gmlp_ln_b": 0.01 * nrm(ks[11], (N_A, GMLP_HALF)),
        "gmlp_w_s": nrm(ks[12], (N_A, GMLP_GROUPS, GMLP_WINDOW, GMLP_WINDOW)) * (0.5 * GMLP_WINDOW ** -0.5),
        "gmlp_b_s": 1.0 + 0.01 * nrm(ks[13], (N_A, GMLP_GROUPS, GMLP_WINDOW)),
        "gmlp_w_out": nrm(ks[14], (N_A, GMLP_HALF, D)) * (BETA * GMLP_HALF ** -0.5),
        "w_ada_kv": nrm(ks[15], (D, 2 * D)) * (0.1 * D ** -0.5),
        "b_ada_kv": 0.01 * nrm(ks[16], (2 * D,)),
        "w_kv": nrm(ks[17], (D, 2 * D)) * D ** -0.5 * v_scale[None, :],
        "attn_w_q": nrm(ks[18], (N_B, D, D)) * D ** -0.5,
        "attn_rel_bias": 0.5 * nrm(ks[19], (N_B, N_HEADS, N_REL)),
        "attn_w_o": nrm(ks[20], (N_B, D, D)) * (BETA * D ** -0.5),
    }


def _fwd_reference(x, c, w_ada, b_ada, ln_g, ln_b, ffn_gu, ffn_down,
              gmlp_w_in, gmlp_b_in, gmlp_ln_g, gmlp_ln_b, gmlp_w_s, gmlp_b_s, gmlp_w_out,
              w_ada_kv, b_ada_kv, w_kv, attn_w_q, attn_rel_bias, attn_w_o):
    B, S, D = x.shape
    c_act = jax.nn.silu(c)
    k_pad = None
    v_pad = None
    for l in range(DEPTH):
        mod = (c_act @ w_ada[l] + b_ada[l]).reshape(B, 1, N_MOD, D)
        shift = [mod[:, :, 3 * i] for i in range(3)]
        scl = [mod[:, :, 3 * i + 1] for i in range(3)]
        gate = [1.0 + mod[:, :, 3 * i + 2] for i in range(3)]

        h = x * (1.0 + scl[0]) + shift[0]
        y = swiglu(h, ffn_gu[l, 0], ffn_down[l, 0])
        x = layer_norm(ALPHA * x + 0.5 * gate[0] * y, ln_g[l, 0], ln_b[l, 0])

        h = x * (1.0 + scl[1]) + shift[1]
        if l < N_A:
            y = gmlp_mixer(h, gmlp_w_in[l], gmlp_b_in[l], gmlp_ln_g[l], gmlp_ln_b[l],
                           gmlp_w_s[l], gmlp_b_s[l], gmlp_w_out[l])
        else:
            j = l - N_A
            y = chunk_band_attention(h, attn_w_q[j], attn_rel_bias[j], attn_w_o[j], k_pad, v_pad)
        x = layer_norm(ALPHA * x + gate[1] * y, ln_g[l, 1], ln_b[l, 1])

        h = x * (1.0 + scl[2]) + shift[2]
        y = swiglu(h, ffn_gu[l, 1], ffn_down[l, 1])
        x = layer_norm(ALPHA * x + 0.5 * gate[2] * y, ln_g[l, 2], ln_b[l, 2])

        if l == N_A - 1:
            mkv = (c_act @ w_ada_kv + b_ada_kv).reshape(B, 1, 2, D)
            hkv = x * (1.0 + mkv[:, :, 1]) + mkv[:, :, 0]
            kv = hkv @ w_kv
            k = kv[..., :D].reshape(B, S, N_HEADS, HEAD_DIM)
            v = kv[..., D:].reshape(B, S, N_HEADS, HEAD_DIM)
            pad = ((0, 0), (LEFT_PAD, 0), (0, 0), (0, 0))
            k_pad = jnp.pad(k, pad)
            v_pad = jnp.pad(v, pad)
    return x


import jax as _jax
import jax.numpy as _jnp

TWIN_FORMAT = 'train_step'
FWD_PARAMS = ['x', 'c', 'w_ada', 'b_ada', 'ln_g', 'ln_b', 'ffn_gu', 'ffn_down', 'gmlp_w_in', 'gmlp_b_in', 'gmlp_ln_g', 'gmlp_ln_b', 'gmlp_w_s', 'gmlp_b_s', 'gmlp_w_out', 'w_ada_kv', 'b_ada_kv', 'w_kv', 'attn_w_q', 'attn_rel_bias', 'attn_w_o']
TWIN_WEIGHTS = ['w_ada', 'b_ada', 'ln_g', 'ln_b', 'ffn_gu', 'ffn_down', 'gmlp_w_in', 'gmlp_b_in', 'gmlp_ln_g', 'gmlp_ln_b', 'gmlp_w_s', 'gmlp_b_s', 'gmlp_w_out', 'w_ada_kv', 'b_ada_kv', 'w_kv', 'attn_w_q', 'attn_rel_bias', 'attn_w_o']
TWIN_DIFF_INPUT = 'x'
TWIN_INPUTS = ['x', 'c', 'w_ada', 'b_ada', 'ln_g', 'ln_b', 'ffn_gu', 'ffn_down', 'gmlp_w_in', 'gmlp_b_in', 'gmlp_ln_g', 'gmlp_ln_b', 'gmlp_w_s', 'gmlp_b_s', 'gmlp_w_out', 'w_ada_kv', 'b_ada_kv', 'w_kv', 'attn_w_q', 'attn_rel_bias', 'attn_w_o', 'loss_target', 'm_w_ada', 'm_b_ada', 'm_ln_g', 'm_ln_b', 'm_ffn_gu', 'm_ffn_down', 'm_gmlp_w_in', 'm_gmlp_b_in', 'm_gmlp_ln_g', 'm_gmlp_ln_b', 'm_gmlp_w_s', 'm_gmlp_b_s', 'm_gmlp_w_out', 'm_w_ada_kv', 'm_b_ada_kv', 'm_w_kv', 'm_attn_w_q', 'm_attn_rel_bias', 'm_attn_w_o', 'v_w_ada', 'v_b_ada', 'v_ln_g', 'v_ln_b', 'v_ffn_gu', 'v_ffn_down', 'v_gmlp_w_in', 'v_gmlp_b_in', 'v_gmlp_ln_g', 'v_gmlp_ln_b', 'v_gmlp_w_s', 'v_gmlp_b_s', 'v_gmlp_w_out', 'v_w_ada_kv', 'v_b_ada_kv', 'v_w_kv', 'v_attn_w_q', 'v_attn_rel_bias', 'v_attn_w_o']
TWIN_OUTPUTS = ['loss', 'grad_x', 'grad_w_ada', 'grad_b_ada', 'grad_ln_g', 'grad_ln_b', 'grad_ffn_gu', 'grad_ffn_down', 'grad_gmlp_w_in', 'grad_gmlp_b_in', 'grad_gmlp_ln_g', 'grad_gmlp_ln_b', 'grad_gmlp_w_s', 'grad_gmlp_b_s', 'grad_gmlp_w_out', 'grad_w_ada_kv', 'grad_b_ada_kv', 'grad_w_kv', 'grad_attn_w_q', 'grad_attn_rel_bias', 'grad_attn_w_o', 'delta_w_ada', 'delta_b_ada', 'delta_ln_g', 'delta_ln_b', 'delta_ffn_gu', 'delta_ffn_down', 'delta_gmlp_w_in', 'delta_gmlp_b_in', 'delta_gmlp_ln_g', 'delta_gmlp_ln_b', 'delta_gmlp_w_s', 'delta_gmlp_b_s', 'delta_gmlp_w_out', 'delta_w_ada_kv', 'delta_b_ada_kv', 'delta_w_kv', 'delta_attn_w_q', 'delta_attn_rel_bias', 'delta_attn_w_o', 'new_m_w_ada', 'new_m_b_ada', 'new_m_ln_g', 'new_m_ln_b', 'new_m_ffn_gu', 'new_m_ffn_down', 'new_m_gmlp_w_in', 'new_m_gmlp_b_in', 'new_m_gmlp_ln_g', 'new_m_gmlp_ln_b', 'new_m_gmlp_w_s', 'new_m_gmlp_b_s', 'new_m_gmlp_w_out', 'new_m_w_ada_kv', 'new_m_b_ada_kv', 'new_m_w_kv', 'new_m_attn_w_q', 'new_m_attn_rel_bias', 'new_m_attn_w_o', 'new_v_w_ada', 'new_v_b_ada', 'new_v_ln_g', 'new_v_ln_b', 'new_v_ffn_gu', 'new_v_ffn_down', 'new_v_gmlp_w_in', 'new_v_gmlp_b_in', 'new_v_gmlp_ln_g', 'new_v_gmlp_ln_b', 'new_v_gmlp_w_s', 'new_v_gmlp_b_s', 'new_v_gmlp_w_out', 'new_v_w_ada_kv', 'new_v_b_ada_kv', 'new_v_w_kv', 'new_v_attn_w_q', 'new_v_attn_rel_bias', 'new_v_attn_w_o']
TWIN_LEAF_KINDS = {'loss': 'loss', 'grad_x': 'grad_x', 'grad_w_ada': 'grad_w', 'grad_b_ada': 'grad_w', 'grad_ln_g': 'grad_w', 'grad_ln_b': 'grad_w', 'grad_ffn_gu': 'grad_w', 'grad_ffn_down': 'grad_w', 'grad_gmlp_w_in': 'grad_w', 'grad_gmlp_b_in': 'grad_w', 'grad_gmlp_ln_g': 'grad_w', 'grad_gmlp_ln_b': 'grad_w', 'grad_gmlp_w_s': 'grad_w', 'grad_gmlp_b_s': 'grad_w', 'grad_gmlp_w_out': 'grad_w', 'grad_w_ada_kv': 'grad_w', 'grad_b_ada_kv': 'grad_w', 'grad_w_kv': 'grad_w', 'grad_attn_w_q': 'grad_w', 'grad_attn_rel_bias': 'grad_w', 'grad_attn_w_o': 'grad_w', 'delta_w_ada': 'delta_w', 'delta_b_ada': 'delta_w', 'delta_ln_g': 'delta_w', 'delta_ln_b': 'delta_w', 'delta_ffn_gu': 'delta_w', 'delta_ffn_down': 'delta_w', 'delta_gmlp_w_in': 'delta_w', 'delta_gmlp_b_in': 'delta_w', 'delta_gmlp_ln_g': 'delta_w', 'delta_gmlp_ln_b': 'delta_w', 'delta_gmlp_w_s': 'delta_w', 'delta_gmlp_b_s': 'delta_w', 'delta_gmlp_w_out': 'delta_w', 'delta_w_ada_kv': 'delta_w', 'delta_b_ada_kv': 'delta_w', 'delta_w_kv': 'delta_w', 'delta_attn_w_q': 'delta_w', 'delta_attn_rel_bias': 'delta_w', 'delta_attn_w_o': 'delta_w', 'new_m_w_ada': 'new_m', 'new_m_b_ada': 'new_m', 'new_m_ln_g': 'new_m', 'new_m_ln_b': 'new_m', 'new_m_ffn_gu': 'new_m', 'new_m_ffn_down': 'new_m', 'new_m_gmlp_w_in': 'new_m', 'new_m_gmlp_b_in': 'new_m', 'new_m_gmlp_ln_g': 'new_m', 'new_m_gmlp_ln_b': 'new_m', 'new_m_gmlp_w_s': 'new_m', 'new_m_gmlp_b_s': 'new_m', 'new_m_gmlp_w_out': 'new_m', 'new_m_w_ada_kv': 'new_m', 'new_m_b_ada_kv': 'new_m', 'new_m_w_kv': 'new_m', 'new_m_attn_w_q': 'new_m', 'new_m_attn_rel_bias': 'new_m', 'new_m_attn_w_o': 'new_m', 'new_v_w_ada': 'new_v', 'new_v_b_ada': 'new_v', 'new_v_ln_g': 'new_v', 'new_v_ln_b': 'new_v', 'new_v_ffn_gu': 'new_v', 'new_v_ffn_down': 'new_v', 'new_v_gmlp_w_in': 'new_v', 'new_v_gmlp_b_in': 'new_v', 'new_v_gmlp_ln_g': 'new_v', 'new_v_gmlp_ln_b': 'new_v', 'new_v_gmlp_w_s': 'new_v', 'new_v_gmlp_b_s': 'new_v', 'new_v_gmlp_w_out': 'new_v', 'new_v_w_ada_kv': 'new_v', 'new_v_b_ada_kv': 'new_v', 'new_v_w_kv': 'new_v', 'new_v_attn_w_q': 'new_v', 'new_v_attn_rel_bias': 'new_v', 'new_v_attn_w_o': 'new_v'}


def _forward(args):
    return _fwd_reference(*[args[k] for k in FWD_PARAMS])


def _output_shape():
    out = _jax.eval_shape(lambda: _forward(_fwd_setup_inputs(0)))
    return out.shape, out.dtype

N_MICROBATCH = 1
ADAM_LR = 0.001
ADAM_B1 = 0.9
ADAM_B2 = 0.999
ADAM_EPS = 1e-08
ADAM_WD = 0.01
ADAM_STEP = 10
PER_EXAMPLE_BATCH_AXIS = {'x': 0, 'c': 0, 'loss_target': 0}
SHARED_INPUTS = []
_WEIGHT_DTYPES = {'w_ada': _jnp.float32, 'b_ada': _jnp.float32, 'ln_g': _jnp.float32, 'ln_b': _jnp.float32, 'ffn_gu': _jnp.float32, 'ffn_down': _jnp.float32, 'gmlp_w_in': _jnp.float32, 'gmlp_b_in': _jnp.float32, 'gmlp_ln_g': _jnp.float32, 'gmlp_ln_b': _jnp.float32, 'gmlp_w_s': _jnp.float32, 'gmlp_b_s': _jnp.float32, 'gmlp_w_out': _jnp.float32, 'w_ada_kv': _jnp.float32, 'b_ada_kv': _jnp.float32, 'w_kv': _jnp.float32, 'attn_w_q': _jnp.float32, 'attn_rel_bias': _jnp.float32, 'attn_w_o': _jnp.float32}
MOMENT_SCALE = {'w_ada': 8.877020e-03, 'b_ada': 1.512502e-02, 'ln_g': 4.623593e+00, 'ln_b': 4.335406e-01, 'ffn_gu': 5.826452e-03, 'ffn_down': 2.260058e-02, 'gmlp_w_in': 1.271347e-02, 'gmlp_b_in': 1.433987e-02, 'gmlp_ln_g': 6.397724e-03, 'gmlp_ln_b': 6.337674e-03, 'gmlp_w_s': 1.783579e-02, 'gmlp_b_s': 2.054367e-02, 'gmlp_w_out': 5.614030e-02, 'w_ada_kv': 9.350110e-03, 'b_ada_kv': 1.870184e-02, 'w_kv': 6.724445e-03, 'attn_w_q': 2.019518e-03, 'attn_rel_bias': 8.677442e-04, 'attn_w_o': 6.588898e-03}


def _to_microbatches(a, axis):
    t = _jnp.moveaxis(a, axis, 0)
    t = t.reshape((N_MICROBATCH, t.shape[0] // N_MICROBATCH) + t.shape[1:])
    return _jnp.moveaxis(t, 1, axis + 1)


def setup_inputs(seed: int = 0) -> dict:
    inp = _fwd_setup_inputs(seed)
    key = _jax.random.fold_in(_jax.random.key(seed), 7919)
    shape, _ = _output_shape()
    out = dict(inp)
    out["loss_target"] = _jax.random.normal(_jax.random.fold_in(key, 0), shape, _jnp.float32)
    for i, name in enumerate(TWIN_WEIGHTS):
        w = inp[name].astype(_jnp.float32)
        if MOMENT_SCALE is None:
            s = _jnp.sqrt(_jnp.mean(_jnp.square(w)) + 1e-30)
        else:
            s = MOMENT_SCALE[name]
        km, kv = _jax.random.split(_jax.random.fold_in(key, i + 1))
        out[name] = w
        out["m_" + name] = s * _jax.random.normal(km, w.shape, _jnp.float32)
        out["v_" + name] = (s * s) * _jax.random.uniform(kv, w.shape, _jnp.float32, 0.5, 1.5)
    if N_MICROBATCH > 1:
        for name, axis in PER_EXAMPLE_BATCH_AXIS.items():
            out[name] = _to_microbatches(out[name], axis)
    return {'x': out['x'], 'c': out['c'], 'w_ada': out['w_ada'], 'b_ada': out['b_ada'], 'ln_g': out['ln_g'], 'ln_b': out['ln_b'], 'ffn_gu': out['ffn_gu'], 'ffn_down': out['ffn_down'], 'gmlp_w_in': out['gmlp_w_in'], 'gmlp_b_in': out['gmlp_b_in'], 'gmlp_ln_g': out['gmlp_ln_g'], 'gmlp_ln_b': out['gmlp_ln_b'], 'gmlp_w_s': out['gmlp_w_s'], 'gmlp_b_s': out['gmlp_b_s'], 'gmlp_w_out': out['gmlp_w_out'], 'w_ada_kv': out['w_ada_kv'], 'b_ada_kv': out['b_ada_kv'], 'w_kv': out['w_kv'], 'attn_w_q': out['attn_w_q'], 'attn_rel_bias': out['attn_rel_bias'], 'attn_w_o': out['attn_w_o'], 'loss_target': out['loss_target'], 'm_w_ada': out['m_w_ada'], 'm_b_ada': out['m_b_ada'], 'm_ln_g': out['m_ln_g'], 'm_ln_b': out['m_ln_b'], 'm_ffn_gu': out['m_ffn_gu'], 'm_ffn_down': out['m_ffn_down'], 'm_gmlp_w_in': out['m_gmlp_w_in'], 'm_gmlp_b_in': out['m_gmlp_b_in'], 'm_gmlp_ln_g': out['m_gmlp_ln_g'], 'm_gmlp_ln_b': out['m_gmlp_ln_b'], 'm_gmlp_w_s': out['m_gmlp_w_s'], 'm_gmlp_b_s': out['m_gmlp_b_s'], 'm_gmlp_w_out': out['m_gmlp_w_out'], 'm_w_ada_kv': out['m_w_ada_kv'], 'm_b_ada_kv': out['m_b_ada_kv'], 'm_w_kv': out['m_w_kv'], 'm_attn_w_q': out['m_attn_w_q'], 'm_attn_rel_bias': out['m_attn_rel_bias'], 'm_attn_w_o': out['m_attn_w_o'], 'v_w_ada': out['v_w_ada'], 'v_b_ada': out['v_b_ada'], 'v_ln_g': out['v_ln_g'], 'v_ln_b': out['v_ln_b'], 'v_ffn_gu': out['v_ffn_gu'], 'v_ffn_down': out['v_ffn_down'], 'v_gmlp_w_in': out['v_gmlp_w_in'], 'v_gmlp_b_in': out['v_gmlp_b_in'], 'v_gmlp_ln_g': out['v_gmlp_ln_g'], 'v_gmlp_ln_b': out['v_gmlp_ln_b'], 'v_gmlp_w_s': out['v_gmlp_w_s'], 'v_gmlp_b_s': out['v_gmlp_b_s'], 'v_gmlp_w_out': out['v_gmlp_w_out'], 'v_w_ada_kv': out['v_w_ada_kv'], 'v_b_ada_kv': out['v_b_ada_kv'], 'v_w_kv': out['v_w_kv'], 'v_attn_w_q': out['v_attn_w_q'], 'v_attn_rel_bias': out['v_attn_rel_bias'], 'v_attn_w_o': out['v_attn_w_o']}


def _loss(weights, diff, rest, loss_target):
    with _jax.named_scope("forward"):
        args = {**rest, TWIN_DIFF_INPUT: diff, **{k: w.astype(_WEIGHT_DTYPES[k]) for k, w in weights.items()}}
        y = _forward(args)
    with _jax.named_scope("loss_head"):
        err = _jnp.square(y.astype(_jnp.float32) - loss_target)
        return 0.5 * _jnp.sum(_jnp.mean(err, axis=-1)) if err.ndim else 0.5 * err


def _adamw(w, g, m, v):
    m = ADAM_B1 * m + (1.0 - ADAM_B1) * g
    v = ADAM_B2 * v + (1.0 - ADAM_B2) * _jnp.square(g)
    m_hat = m / (1.0 - ADAM_B1 ** ADAM_STEP)
    v_hat = v / (1.0 - ADAM_B2 ** ADAM_STEP)
    delta = -ADAM_LR * (m_hat / (_jnp.sqrt(v_hat) + ADAM_EPS) + ADAM_WD * w)
    return delta, m, v


def reference(x, c, w_ada, b_ada, ln_g, ln_b, ffn_gu, ffn_down, gmlp_w_in, gmlp_b_in, gmlp_ln_g, gmlp_ln_b, gmlp_w_s, gmlp_b_s, gmlp_w_out, w_ada_kv, b_ada_kv, w_kv, attn_w_q, attn_rel_bias, attn_w_o, loss_target, m_w_ada, m_b_ada, m_ln_g, m_ln_b, m_ffn_gu, m_ffn_down, m_gmlp_w_in, m_gmlp_b_in, m_gmlp_ln_g, m_gmlp_ln_b, m_gmlp_w_s, m_gmlp_b_s, m_gmlp_w_out, m_w_ada_kv, m_b_ada_kv, m_w_kv, m_attn_w_q, m_attn_rel_bias, m_attn_w_o, v_w_ada, v_b_ada, v_ln_g, v_ln_b, v_ffn_gu, v_ffn_down, v_gmlp_w_in, v_gmlp_b_in, v_gmlp_ln_g, v_gmlp_ln_b, v_gmlp_w_s, v_gmlp_b_s, v_gmlp_w_out, v_w_ada_kv, v_b_ada_kv, v_w_kv, v_attn_w_q, v_attn_rel_bias, v_attn_w_o):
    given = dict(x=x, c=c, w_ada=w_ada, b_ada=b_ada, ln_g=ln_g, ln_b=ln_b, ffn_gu=ffn_gu, ffn_down=ffn_down, gmlp_w_in=gmlp_w_in, gmlp_b_in=gmlp_b_in, gmlp_ln_g=gmlp_ln_g, gmlp_ln_b=gmlp_ln_b, gmlp_w_s=gmlp_w_s, gmlp_b_s=gmlp_b_s, gmlp_w_out=gmlp_w_out, w_ada_kv=w_ada_kv, b_ada_kv=b_ada_kv, w_kv=w_kv, attn_w_q=attn_w_q, attn_rel_bias=attn_rel_bias, attn_w_o=attn_w_o, loss_target=loss_target, m_w_ada=m_w_ada, m_b_ada=m_b_ada, m_ln_g=m_ln_g, m_ln_b=m_ln_b, m_ffn_gu=m_ffn_gu, m_ffn_down=m_ffn_down, m_gmlp_w_in=m_gmlp_w_in, m_gmlp_b_in=m_gmlp_b_in, m_gmlp_ln_g=m_gmlp_ln_g, m_gmlp_ln_b=m_gmlp_ln_b, m_gmlp_w_s=m_gmlp_w_s, m_gmlp_b_s=m_gmlp_b_s, m_gmlp_w_out=m_gmlp_w_out, m_w_ada_kv=m_w_ada_kv, m_b_ada_kv=m_b_ada_kv, m_w_kv=m_w_kv, m_attn_w_q=m_attn_w_q, m_attn_rel_bias=m_attn_rel_bias, m_attn_w_o=m_attn_w_o, v_w_ada=v_w_ada, v_b_ada=v_b_ada, v_ln_g=v_ln_g, v_ln_b=v_ln_b, v_ffn_gu=v_ffn_gu, v_ffn_down=v_ffn_down, v_gmlp_w_in=v_gmlp_w_in, v_gmlp_b_in=v_gmlp_b_in, v_gmlp_ln_g=v_gmlp_ln_g, v_gmlp_ln_b=v_gmlp_ln_b, v_gmlp_w_s=v_gmlp_w_s, v_gmlp_b_s=v_gmlp_b_s, v_gmlp_w_out=v_gmlp_w_out, v_w_ada_kv=v_w_ada_kv, v_b_ada_kv=v_b_ada_kv, v_w_kv=v_w_kv, v_attn_w_q=v_attn_w_q, v_attn_rel_bias=v_attn_rel_bias, v_attn_w_o=v_attn_w_o)
    weights = {n: given[n] for n in TWIN_WEIGHTS}
    shared = {n: given[n] for n in SHARED_INPUTS}
    per_example = {n: given[n] for n in ['x', 'c']}
    grad_fn = _jax.value_and_grad(_loss, argnums=(0, 1))

    def one_microbatch(ex, loss_target):
        ex = dict(ex)
        diff = ex.pop(TWIN_DIFF_INPUT)
        return grad_fn(weights, diff, {**shared, **ex}, loss_target)

    if N_MICROBATCH == 1:
        loss, (grad_w, grad_x) = one_microbatch(per_example, given["loss_target"])
    else:
        def body(carry, xs):
            loss_sum, grad_sum = carry
            l_k, (gw_k, gx_k) = one_microbatch(xs[0], xs[1])
            with _jax.named_scope("update"):
                return (loss_sum + l_k, _jax.tree.map(_jnp.add, grad_sum, gw_k)), gx_k

        init = (_jnp.zeros((), _jnp.float32), _jax.tree.map(_jnp.zeros_like, weights))
        (loss, grad_w), grad_x = _jax.lax.scan(body, init, (per_example, given["loss_target"]))
    with _jax.named_scope("update"):
        delta_w, new_m, new_v = {}, {}, {}
        for n in TWIN_WEIGHTS:
            delta_w[n], new_m[n], new_v[n] = _adamw(weights[n], grad_w[n], given["m_" + n], given["v_" + n])
    return (loss, grad_x, *[grad_w[n] for n in TWIN_WEIGHTS], *[delta_w[n] for n in TWIN_WEIGHTS],
            *[new_m[n] for n in TWIN_WEIGHTS], *[new_v[n] for n in TWIN_WEIGHTS])
```

```python
import functools

import jax
import jax.numpy as jnp
from jax import lax
from jax.experimental import pallas as pl
from jax.experimental.pallas import tpu as pltpu

F32 = jnp.float32
BF16 = jnp.bfloat16
MESH = pl.DeviceIdType.MESH
HIGHEST = lax.Precision.HIGHEST

CHUNK = 64
GMLP_WINDOW = 128
GMLP_GROUPS = 8
HEAD_DIM = 64
LEFT_CHUNKS = 8
BAND = (LEFT_CHUNKS + 1) * CHUNK
LEFT_PAD = LEFT_CHUNKS * CHUNK
MAX_REL = 4 * CHUNK
N_REL = (CHUNK - 1) + MAX_REL + 1
LN_EPS = 1e-5
N_MOD = 9
N_DEV = 8
N_CHIP = 4

ADAM_LR = 0.001
ADAM_B1 = 0.9
ADAM_B2 = 0.999
ADAM_EPS = 1e-08
ADAM_WD = 0.01
ADAM_STEP = 10

LANES = 128
ROW_TILE = 256
WGRAD_ROWS = 512
ATTN_CHUNKS_PER_STEP = 4
VMEM_LIMIT_MB = 56

NT = (((1,), (1,)), ((), ()))
TN = (((0,), (0,)), ((), ()))

ANY = pl.BlockSpec(memory_space=pl.ANY)
VMEM_SPEC = pl.BlockSpec(memory_space=pltpu.VMEM)


def _params(semantics=None):
    kw = dict(vmem_limit_bytes=VMEM_LIMIT_MB * 1024 * 1024)
    if semantics is not None:
        kw["dimension_semantics"] = semantics
    return pltpu.CompilerParams(**kw)


def _sigmoid(v):
    return 1.0 / (1.0 + jnp.exp(-v))


def _gelu(v):
    return 0.5 * v * (1.0 + lax.erf(v * (2.0 ** -0.5)))


def _gelu_grad(v):
    return 0.5 * (1.0 + lax.erf(v * (2.0 ** -0.5))) + v * jnp.exp(-0.5 * v * v) * ((2.0 * jnp.pi) ** -0.5)


def _row(m):
    return lambda i: (i, 0)


def _fixed2(i):
    return (0, 0)


def _fixed3(i):
    return (0, 0, 0)


def mod_matmul(x, scl, shift, w, bias, out_dtype, name):
    S, D = x.shape
    NS, _, n = w.shape
    tm = min(ROW_TILE, S)
    has_bias = bias is not None

    def body(*refs):
        if has_bias:
            x_ref, scl_ref, sh_ref, w_ref, b_ref, o_ref = refs
        else:
            x_ref, scl_ref, sh_ref, w_ref, o_ref = refs
        h = (x_ref[...] * (1.0 + scl_ref[...]) + sh_ref[...]).astype(BF16)
        for s in range(NS):
            acc = jnp.dot(h, w_ref[s], preferred_element_type=F32)
            if has_bias:
                acc = acc + b_ref[:, s * n:(s + 1) * n]
            o_ref[:, s * n:(s + 1) * n] = acc.astype(out_dtype)

    in_specs = [pl.BlockSpec((tm, D), _row(0)), pl.BlockSpec((1, D), _fixed2), pl.BlockSpec((1, D), _fixed2),
                pl.BlockSpec((NS, D, n), _fixed3)]
    args = [x, scl, shift, w]
    if has_bias:
        in_specs.append(pl.BlockSpec((1, NS * n), _fixed2))
        args.append(bias)
    return pl.pallas_call(
        body, name=name, grid=(S // tm,), in_specs=in_specs,
        out_specs=pl.BlockSpec((tm, NS * n), _row(0)),
        out_shape=jax.ShapeDtypeStruct((S, NS * n), out_dtype),
        compiler_params=_params(("parallel",)),
    )(*args)


def matmul_res_ln(a, w, x, gw, lg, lb, alpha, swiglu, name):
    S, D = x.shape
    K = w.shape[0]
    tm = min(ROW_TILE, S)
    ka = a.shape[1]

    def body(a_ref, w_ref, x_ref, gw_ref, lg_ref, lb_ref, xn_ref, xh_ref, rs_ref, y_ref):
        if swiglu:
            g = a_ref[:, :K].astype(F32)
            u = a_ref[:, K:].astype(F32)
            act = (g * _sigmoid(g) * u).astype(BF16)
        else:
            act = a_ref[...].astype(BF16)
        y = jnp.dot(act, w_ref[...], preferred_element_type=F32)
        z = alpha * x_ref[...] + gw_ref[...] * y
        mu = jnp.mean(z, axis=-1, keepdims=True)
        zc = z - mu
        var = jnp.mean(zc * zc, axis=-1, keepdims=True)
        rstd = lax.rsqrt(var + LN_EPS)
        xhat = zc * rstd
        xn_ref[...] = xhat * lg_ref[...] + lb_ref[...]
        xh_ref[...] = xhat
        rs_ref[...] = rstd
        y_ref[...] = y.astype(BF16)

    vec = pl.BlockSpec((1, D), _fixed2)
    return pl.pallas_call(
        body, name=name, grid=(S // tm,),
        in_specs=[pl.BlockSpec((tm, ka), _row(0)), pl.BlockSpec((K, D), _fixed2), pl.BlockSpec((tm, D), _row(0)),
                  vec, vec, vec],
        out_specs=[pl.BlockSpec((tm, D), _row(0)), pl.BlockSpec((tm, D), _row(0)), pl.BlockSpec((tm, 1), _row(0)),
                   pl.BlockSpec((tm, D), _row(0))],
        out_shape=[jax.ShapeDtypeStruct((S, D), F32), jax.ShapeDtypeStruct((S, D), F32),
                   jax.ShapeDtypeStruct((S, 1), F32), jax.ShapeDtypeStruct((S, D), BF16)],
        compiler_params=_params(("parallel",)),
    )(a, w, x, gw, lg, lb)


def ln_res_bwd(dxn, xhat, rstd, lg, y, gw, wres, alpha, name):
    S, D = dxn.shape
    tm = min(ROW_TILE, S)

    def body(dxn_ref, xh_ref, rs_ref, lg_ref, y_ref, gw_ref, dxa_ref, dy_ref, acc_ref):
        @pl.when(pl.program_id(0) == 0)
        def _():
            acc_ref[...] = jnp.zeros_like(acc_ref)

        d = dxn_ref[...]
        xh = xh_ref[...]
        dxh = d * lg_ref[...]
        m1 = jnp.mean(dxh, axis=-1, keepdims=True)
        m2 = jnp.mean(dxh * xh, axis=-1, keepdims=True)
        dz = rs_ref[...] * (dxh - m1 - xh * m2)
        dxa_ref[...] = alpha * dz
        dy_ref[...] = (gw_ref[...] * dz).astype(BF16)
        acc_ref[0:1, :] += jnp.sum(d * xh, axis=0, keepdims=True)
        acc_ref[1:2, :] += jnp.sum(d, axis=0, keepdims=True)
        acc_ref[2:3, :] += jnp.sum((wres * dz) * y_ref[...].astype(F32), axis=0, keepdims=True)

    vec = pl.BlockSpec((1, D), _fixed2)
    tile = pl.BlockSpec((tm, D), _row(0))
    return pl.pallas_call(
        body, name=name, grid=(S // tm,),
        in_specs=[tile, tile, pl.BlockSpec((tm, 1), _row(0)), vec, tile, vec],
        out_specs=[tile, tile, pl.BlockSpec((8, D), _fixed2)],
        out_shape=[jax.ShapeDtypeStruct((S, D), F32), jax.ShapeDtypeStruct((S, D), BF16),
                   jax.ShapeDtypeStruct((8, D), F32)],
        compiler_params=_params(("arbitrary",)),
    )(dxn, xhat, rstd, lg, y, gw)


def ffn_act_bwd(dy, wd, gu, name):
    S, D = dy.shape
    K = wd.shape[0]
    tm = min(ROW_TILE, S)

    def body(dy_ref, wd_ref, gu_ref, o_ref):
        da = lax.dot_general(dy_ref[...], wd_ref[...], NT, preferred_element_type=F32)
        g = gu_ref[:, :K].astype(F32)
        u = gu_ref[:, K:].astype(F32)
        sg = _sigmoid(g)
        o_ref[:, :K] = (da * u * (sg * (1.0 + g * (1.0 - sg)))).astype(BF16)
        o_ref[:, K:] = (da * (g * sg)).astype(BF16)

    return pl.pallas_call(
        body, name=name, grid=(S // tm,),
        in_specs=[pl.BlockSpec((tm, D), _row(0)), pl.BlockSpec((K, D), _fixed2), pl.BlockSpec((tm, 2 * K), _row(0))],
        out_specs=pl.BlockSpec((tm, 2 * K), _row(0)),
        out_shape=jax.ShapeDtypeStruct((S, 2 * K), BF16),
        compiler_params=_params(("parallel",)),
    )(dy, wd, gu)


def matmul_nt(a, w, name):
    S, D = a.shape
    K = w.shape[0]
    tm = min(ROW_TILE, S)

    def body(a_ref, w_ref, o_ref):
        o_ref[...] = lax.dot_general(a_ref[...], w_ref[...], NT, preferred_element_type=F32).astype(BF16)

    return pl.pallas_call(
        body, name=name, grid=(S // tm,),
        in_specs=[pl.BlockSpec((tm, D), _row(0)), pl.BlockSpec((K, D), _fixed2)],
        out_specs=pl.BlockSpec((tm, K), _row(0)),
        out_shape=jax.ShapeDtypeStruct((S, K), BF16),
        compiler_params=_params(("parallel",)),
    )(a, w)


def dgrad_mod(dpre, w, dxa, xin, scl, name):
    S, D = xin.shape
    NS, _, n = w.shape
    tm = min(ROW_TILE, S)

    def body(dp_ref, w_ref, dxa_ref, xin_ref, scl_ref, dx_ref, acc_ref):
        @pl.when(pl.program_id(0) == 0)
        def _():
            acc_ref[...] = jnp.zeros_like(acc_ref)

        dh = jnp.zeros((tm, D), F32)
        for s in range(NS):
            dh = dh + lax.dot_general(dp_ref[:, s * n:(s + 1) * n].astype(BF16), w_ref[s], NT,
                                      preferred_element_type=F32)
        dx_ref[...] = dxa_ref[...] + dh * (1.0 + scl_ref[...])
        acc_ref[0:1, :] += jnp.sum(dh * xin_ref[...], axis=0, keepdims=True)
        acc_ref[1:2, :] += jnp.sum(dh, axis=0, keepdims=True)

    tile = pl.BlockSpec((tm, D), _row(0))
    return pl.pallas_call(
        body, name=name, grid=(S // tm,),
        in_specs=[pl.BlockSpec((tm, NS * n), _row(0)), pl.BlockSpec((NS, D, n), _fixed3), tile, tile,
                  pl.BlockSpec((1, D), _fixed2)],
        out_specs=[tile, pl.BlockSpec((8, D), _fixed2)],
        out_shape=[jax.ShapeDtypeStruct((S, D), F32), jax.ShapeDtypeStruct((8, D), F32)],
        compiler_params=_params(("arbitrary",)),
    )(dpre, w, dxa, xin, scl)


def wgrad(kind, a_args, b, kb, nb, name):
    S, N = b.shape
    if kind == "swiglu":
        K = a_args[0].shape[1] // 2
    else:
        K = a_args[0].shape[1]
    NBk, KB = N // nb, K // kb
    ts = min(WGRAD_ROWS, S)
    nsteps = S // ts

    def body(*refs):
        acc_ref = refs[-1]
        o_ref = refs[-2]
        b_ref = refs[-3]
        si = pl.program_id(2)

        @pl.when(si == 0)
        def _():
            acc_ref[...] = jnp.zeros_like(acc_ref)

        if kind == "mod":
            x_ref, scl_ref, sh_ref = refs[:3]
            a = (x_ref[...] * (1.0 + scl_ref[...]) + sh_ref[...]).astype(BF16)
        elif kind == "swiglu":
            g = refs[0][...].astype(F32)
            u = refs[1][...].astype(F32)
            a = (g * _sigmoid(g) * u).astype(BF16)
        else:
            a = refs[0][...].astype(BF16)
        acc_ref[...] += lax.dot_general(a, b_ref[...].astype(BF16), TN, preferred_element_type=F32)

        @pl.when(si == nsteps - 1)
        def _():
            o_ref[...] = acc_ref[...].astype(BF16)

    a_tile = pl.BlockSpec((ts, kb), lambda j, k, s: (s, k))
    if kind == "mod":
        vec = pl.BlockSpec((1, kb), lambda j, k, s: (0, k))
        in_specs = [a_tile, vec, vec]
        args = list(a_args)
    elif kind == "swiglu":
        in_specs = [a_tile, pl.BlockSpec((ts, kb), lambda j, k, s: (s, k + KB))]
        args = [a_args[0], a_args[0]]
    else:
        in_specs = [a_tile]
        args = list(a_args)
    in_specs.append(pl.BlockSpec((ts, nb), lambda j, k, s: (s, j)))
    args.append(b)
    return pl.pallas_call(
        body, name=name, grid=(NBk, KB, nsteps), in_specs=in_specs,
        out_specs=pl.BlockSpec((None, kb, nb), lambda j, k, s: (j, k, 0)),
        out_shape=jax.ShapeDtypeStruct((NBk, K, nb), BF16),
        scratch_shapes=[pltpu.VMEM((kb, nb), F32)],
        compiler_params=_params(("parallel", "parallel", "arbitrary")),
    )(*args)


def _window_mask():
    t = lax.broadcasted_iota(jnp.int32, (GMLP_WINDOW, GMLP_WINDOW), 0)
    s = lax.broadcasted_iota(jnp.int32, (GMLP_WINDOW, GMLP_WINDOW), 1)
    return ((s // CHUNK) <= (t // CHUNK)).astype(F32)


def sgu_fwd(pre, glg, glb, ws, bst, name):
    S, H2 = pre.shape
    H = H2 // 2
    W, G = GMLP_WINDOW, GMLP_GROUPS
    gd = H // G
    tm = min(ROW_TILE, S)

    def body(pre_ref, glg_ref, glb_ref, ws_ref, bst_ref, q_ref):
        u = _gelu(pre_ref[:, :H])
        v = _gelu(pre_ref[:, H:])
        mu = jnp.mean(v, axis=-1, keepdims=True)
        vc = v - mu
        var = jnp.mean(vc * vc, axis=-1, keepdims=True)
        vn = ((vc * lax.rsqrt(var + LN_EPS)) * glg_ref[...] + glb_ref[...]).astype(BF16)
        mask = _window_mask()
        for g in range(G):
            wsg = (ws_ref[g] * mask).astype(BF16)
            bcol = bst_ref[:, g:g + 1]
            for wi in range(tm // W):
                rows = slice(wi * W, (wi + 1) * W)
                cols = slice(g * gd, (g + 1) * gd)
                s = jnp.dot(wsg, vn[rows, cols], preferred_element_type=F32) + bcol
                q_ref[rows, cols] = (u[rows, cols] * s).astype(BF16)

    return pl.pallas_call(
        body, name=name, grid=(S // tm,),
        in_specs=[pl.BlockSpec((tm, H2), _row(0)), pl.BlockSpec((1, H), _fixed2), pl.BlockSpec((1, H), _fixed2),
                  pl.BlockSpec((G, W, W), _fixed3), pl.BlockSpec((W, G), _fixed2)],
        out_specs=pl.BlockSpec((tm, H), _row(0)),
        out_shape=jax.ShapeDtypeStruct((S, H), BF16),
        compiler_params=_params(("parallel",)),
    )(pre, glg, glb, ws, bst)


def sgu_bwd(dq, pre, glg, glb, ws, bst, name):
    S, H2 = pre.shape
    H = H2 // 2
    W, G = GMLP_WINDOW, GMLP_GROUPS
    gd = H // G
    tm = min(ROW_TILE, S)

    def body(dq_ref, pre_ref, glg_ref, glb_ref, ws_ref, bst_ref,
             dpre_ref, dws_ref, dss_ref, dgl_ref, dbin_ref, du_s, dvn_s):
        @pl.when(pl.program_id(0) == 0)
        def _():
            dws_ref[...] = jnp.zeros_like(dws_ref)
            dss_ref[...] = jnp.zeros_like(dss_ref)
            dgl_ref[...] = jnp.zeros_like(dgl_ref)
            dbin_ref[...] = jnp.zeros_like(dbin_ref)

        pu = pre_ref[:, :H]
        pv = pre_ref[:, H:]
        u = _gelu(pu)
        v = _gelu(pv)
        mu = jnp.mean(v, axis=-1, keepdims=True)
        vc = v - mu
        var = jnp.mean(vc * vc, axis=-1, keepdims=True)
        rstd = lax.rsqrt(var + LN_EPS)
        vhat = vc * rstd
        vn = (vhat * glg_ref[...] + glb_ref[...]).astype(BF16)
        mask = _window_mask()
        for g in range(G):
            wsg = (ws_ref[g] * mask).astype(BF16)
            bcol = bst_ref[:, g:g + 1]
            cols = slice(g * gd, (g + 1) * gd)
            for wi in range(tm // W):
                rows = slice(wi * W, (wi + 1) * W)
                vblk = vn[rows, cols]
                s = jnp.dot(wsg, vblk, preferred_element_type=F32) + bcol
                dqb = dq_ref[rows, cols].astype(F32)
                du_s[rows, cols] = dqb * s
                ds = dqb * u[rows, cols]
                dss_ref[:, cols] += ds
                dsb = ds.astype(BF16)
                dvn_s[rows, cols] = lax.dot_general(wsg, dsb, TN, preferred_element_type=F32)
                dws_ref[g] += lax.dot_general(dsb, vblk, NT, preferred_element_type=F32) * mask
        dvn = dvn_s[...]
        dgl_ref[0:1, :] += jnp.sum(dvn * vhat, axis=0, keepdims=True)
        dgl_ref[1:2, :] += jnp.sum(dvn, axis=0, keepdims=True)
        dvh = dvn * glg_ref[...]
        m1 = jnp.mean(dvh, axis=-1, keepdims=True)
        m2 = jnp.mean(dvh * vhat, axis=-1, keepdims=True)
        dv = rstd * (dvh - m1 - vhat * m2)
        dpu = du_s[...] * _gelu_grad(pu)
        dpv = dv * _gelu_grad(pv)
        dbin_ref[0:1, :H] += jnp.sum(dpu, axis=0, keepdims=True)
        dbin_ref[0:1, H:] += jnp.sum(dpv, axis=0, keepdims=True)
        dpre_ref[:, :H] = dpu.astype(BF16)
        dpre_ref[:, H:] = dpv.astype(BF16)

    return pl.pallas_call(
        body, name=name, grid=(S // tm,),
        in_specs=[pl.BlockSpec((tm, H), _row(0)), pl.BlockSpec((tm, H2), _row(0)), pl.BlockSpec((1, H), _fixed2),
                  pl.BlockSpec((1, H), _fixed2), pl.BlockSpec((G, W, W), _fixed3), pl.BlockSpec((W, G), _fixed2)],
        out_specs=[pl.BlockSpec((tm, H2), _row(0)), pl.BlockSpec((G, W, W), _fixed3), pl.BlockSpec((W, H), _fixed2),
                   pl.BlockSpec((8, H), _fixed2), pl.BlockSpec((8, H2), _fixed2)],
        out_shape=[jax.ShapeDtypeStruct((S, H2), BF16), jax.ShapeDtypeStruct((G, W, W), F32),
                   jax.ShapeDtypeStruct((W, H), F32), jax.ShapeDtypeStruct((8, H), F32),
                   jax.ShapeDtypeStruct((8, H2), F32)],
        scratch_shapes=[pltpu.VMEM((tm, H), F32), pltpu.VMEM((tm, H), F32)],
        compiler_params=_params(("arbitrary",)),
    )(dq, pre, glg, glb, ws, bst)


def group_lane_sum(dss, name):
    W, H = dss.shape
    gd = H // GMLP_GROUPS

    def body(d_ref, o_ref):
        j = lax.broadcasted_iota(jnp.int32, (H, LANES), 0)
        g = lax.broadcasted_iota(jnp.int32, (H, LANES), 1)
        ind = ((j // gd) == g).astype(F32)
        o_ref[...] = jnp.dot(d_ref[...], ind, preferred_element_type=F32, precision=HIGHEST)

    return pl.pallas_call(
        body, name=name, in_specs=[VMEM_SPEC], out_specs=VMEM_SPEC,
        out_shape=jax.ShapeDtypeStruct((W, LANES), F32), compiler_params=_params(),
    )(dss)


def _attn_scores(q2, k2, bias, sel, valid, scale):
    qm = jnp.where(sel, q2, jnp.zeros_like(q2))
    s = lax.dot_general(qm, k2, NT, preferred_element_type=F32) * scale + bias
    s = jnp.where(valid, s, -jnp.inf)
    m = jnp.max(s, axis=-1, keepdims=True)
    e = jnp.exp(s - m)
    p = e / jnp.sum(e, axis=-1, keepdims=True)
    return qm, p


def attn_fwd(q, kpad, vpad, bias, name):
    S, D = q.shape
    HP = D // LANES
    cps = min(ATTN_CHUNKS_PER_STEP, S // CHUNK)
    tq = cps * CHUNK
    scale = HEAD_DIM ** -0.5

    def body(q_ref, k_ref, v_ref, b_ref, o_ref):
        j = pl.program_id(1)
        sel0 = lax.broadcasted_iota(jnp.int32, (CHUNK, LANES), 1) < HEAD_DIM
        r = lax.broadcasted_iota(jnp.int32, (CHUNK, BAND), 1)
        for cc in range(cps):
            start = pl.multiple_of((j * cps + cc) * CHUNK, CHUNK)
            rows = slice(cc * CHUNK, (cc + 1) * CHUNK)
            q2 = q_ref[rows, :]
            k2 = k_ref[pl.ds(start, BAND), :]
            v2 = v_ref[pl.ds(start, BAND), :]
            valid = (r + start) >= LEFT_PAD
            outs = []
            for sub in range(2):
                sel = sel0 if sub == 0 else jnp.logical_not(sel0)
                _, p = _attn_scores(q2, k2, b_ref[sub], sel, valid, scale)
                outs.append(jnp.dot(p.astype(BF16), v2, preferred_element_type=F32))
            o_ref[rows, :] = jnp.where(sel0, outs[0], outs[1]).astype(BF16)

    kv_spec = pl.BlockSpec((S + LEFT_PAD, LANES), lambda h, j: (0, h))
    return pl.pallas_call(
        body, name=name, grid=(HP, S // tq),
        in_specs=[pl.BlockSpec((tq, LANES), lambda h, j: (j, h)), kv_spec, kv_spec,
                  pl.BlockSpec((2, CHUNK, BAND), lambda h, j: (h, 0, 0))],
        out_specs=pl.BlockSpec((tq, LANES), lambda h, j: (j, h)),
        out_shape=jax.ShapeDtypeStruct((S, D), BF16),
        compiler_params=_params(("parallel", "parallel")),
    )(q, kpad, vpad, bias)


def attn_bwd(q, do, kpad, vpad, bias, dk_in, dv_in, name):
    S, D = q.shape
    HP = D // LANES
    NH = 2 * HP
    cps = min(ATTN_CHUNKS_PER_STEP, S // CHUNK)
    tq = cps * CHUNK
    nj = S // tq
    scale = HEAD_DIM ** -0.5

    def body(q_ref, do_ref, k_ref, v_ref, b_ref, dki_ref, dvi_ref, dq_ref, dk_ref, dv_ref, db_ref, dk_acc, dv_acc):
        j = pl.program_id(1)

        @pl.when(j == 0)
        def _():
            dk_acc[:LEFT_PAD, :] = jnp.zeros((LEFT_PAD, LANES), F32)
            dv_acc[:LEFT_PAD, :] = jnp.zeros((LEFT_PAD, LANES), F32)
            dk_acc[LEFT_PAD:, :] = dki_ref[...]
            dv_acc[LEFT_PAD:, :] = dvi_ref[...]
            db_ref[...] = jnp.zeros_like(db_ref)

        sel0 = lax.broadcasted_iota(jnp.int32, (CHUNK, LANES), 1) < HEAD_DIM
        r = lax.broadcasted_iota(jnp.int32, (CHUNK, BAND), 1)
        for cc in range(cps):
            start = pl.multiple_of((j * cps + cc) * CHUNK, CHUNK)
            rows = slice(cc * CHUNK, (cc + 1) * CHUNK)
            q2 = q_ref[rows, :]
            do2 = do_ref[rows, :]
            k2 = k_ref[pl.ds(start, BAND), :]
            v2 = v_ref[pl.ds(start, BAND), :]
            valid = (r + start) >= LEFT_PAD
            dqs = []
            for sub in range(2):
                sel = sel0 if sub == 0 else jnp.logical_not(sel0)
                qm, p = _attn_scores(q2, k2, b_ref[sub], sel, valid, scale)
                dom = jnp.where(sel, do2, jnp.zeros_like(do2))
                dp = lax.dot_general(dom, v2, NT, preferred_element_type=F32)
                delta = jnp.sum(dp * p, axis=-1, keepdims=True)
                ds = p * (dp - delta)
                db_ref[sub] += ds
                dsb = ds.astype(BF16)
                dqs.append(jnp.dot(dsb, k2, preferred_element_type=F32) * scale)
                dk_acc[pl.ds(start, BAND), :] += lax.dot_general(dsb, qm, TN, preferred_element_type=F32) * scale
                dv_acc[pl.ds(start, BAND), :] += lax.dot_general(p.astype(BF16), dom, TN, preferred_element_type=F32)
            dq_ref[rows, :] = jnp.where(sel0, dqs[0], dqs[1]).astype(BF16)

        @pl.when(j == nj - 1)
        def _():
            dk_ref[...] = dk_acc[LEFT_PAD:, :]
            dv_ref[...] = dv_acc[LEFT_PAD:, :]

    q_spec = pl.BlockSpec((tq, LANES), lambda h, j: (j, h))
    kv_spec = pl.BlockSpec((S + LEFT_PAD, LANES), lambda h, j: (0, h))
    col_spec = pl.BlockSpec((S, LANES), lambda h, j: (0, h))
    b_spec = pl.BlockSpec((2, CHUNK, BAND), lambda h, j: (h, 0, 0))
    return pl.pallas_call(
        body, name=name, grid=(HP, nj),
        in_specs=[q_spec, q_spec, kv_spec, kv_spec, b_spec, col_spec, col_spec],
        out_specs=[q_spec, col_spec, col_spec, b_spec],
        out_shape=[jax.ShapeDtypeStruct((S, D), BF16), jax.ShapeDtypeStruct((S, D), F32),
                   jax.ShapeDtypeStruct((S, D), F32), jax.ShapeDtypeStruct((NH, CHUNK, BAND), F32)],
        scratch_shapes=[pltpu.VMEM((S + LEFT_PAD, LANES), F32), pltpu.VMEM((S + LEFT_PAD, LANES), F32)],
        compiler_params=_params(("parallel", "arbitrary")),
    )(q, do, kpad, vpad, bias, dk_in, dv_in)


def _rel_onehot(t):
    r = lax.broadcasted_iota(jnp.int32, (BAND, N_REL), 0)
    i = lax.broadcasted_iota(jnp.int32, (BAND, N_REL), 1)
    idx = jnp.clip(t + LEFT_PAD - r, -(CHUNK - 1), MAX_REL) + (CHUNK - 1)
    return (idx == i).astype(F32)


def bias_expand(rb, name):
    NH = rb.shape[0]

    def body(rb_ref, o_ref):
        def step(t, carry):
            o_ref[t] = lax.dot_general(rb_ref[...], _rel_onehot(t), NT, preferred_element_type=F32,
                                       precision=HIGHEST)
            return carry

        lax.fori_loop(0, CHUNK, step, 0)

    return pl.pallas_call(
        body, name=name, in_specs=[VMEM_SPEC], out_specs=VMEM_SPEC,
        out_shape=jax.ShapeDtypeStruct((CHUNK, NH, BAND), F32), compiler_params=_params(),
    )(rb)


def bias_grad(dsum, name):
    NH = dsum.shape[1]

    def body(d_ref, o_ref):
        def step(t, acc):
            return acc + jnp.dot(d_ref[t], _rel_onehot(t), preferred_element_type=F32, precision=HIGHEST)

        o_ref[...] = lax.fori_loop(0, CHUNK, step, jnp.zeros((NH, N_REL), F32))

    return pl.pallas_call(
        body, name=name, in_specs=[VMEM_SPEC], out_specs=VMEM_SPEC,
        out_shape=jax.ShapeDtypeStruct((NH, N_REL), F32), compiler_params=_params(),
    )(dsum)


def loss_grad(y, tgt, name):
    S, D = y.shape
    tm = min(ROW_TILE, S)

    def body(y_ref, t_ref, d_ref, acc_ref):
        @pl.when(pl.program_id(0) == 0)
        def _():
            acc_ref[...] = jnp.zeros_like(acc_ref)

        err = y_ref[...] - t_ref[...]
        d_ref[...] = err * (1.0 / D)
        acc_ref[0:1, :] += jnp.sum(err * err, axis=0, keepdims=True)

    tile = pl.BlockSpec((tm, D), _row(0))
    return pl.pallas_call(
        body, name=name, grid=(S // tm,), in_specs=[tile, tile],
        out_specs=[tile, pl.BlockSpec((8, D), _fixed2)],
        out_shape=[jax.ShapeDtypeStruct((S, D), F32), jax.ShapeDtypeStruct((8, D), F32)],
        compiler_params=_params(("arbitrary",)),
    )(y, tgt)


def _col_tile(n):
    for t in (768, 512, 256, 128):
        if n % t == 0:
            return t
    return n


def ada_fwd(c_all, w, b, name):
    L, D, n = w.shape
    tn = _col_tile(n)

    def body(c_ref, w_ref, b_ref, o_ref):
        cv = c_ref[...]
        ca = cv * _sigmoid(cv)
        o_ref[...] = jnp.dot(ca, w_ref[...], preferred_element_type=F32, precision=HIGHEST) + b_ref[...]

    return pl.pallas_call(
        body, name=name, grid=(L, n // tn),
        in_specs=[pl.BlockSpec((N_DEV, D), lambda l, j: (0, 0)), pl.BlockSpec((None, D, tn), lambda l, j: (l, 0, j)),
                  pl.BlockSpec((None, 1, tn), lambda l, j: (l, 0, j))],
        out_specs=pl.BlockSpec((None, N_DEV, tn), lambda l, j: (l, 0, j)),
        out_shape=jax.ShapeDtypeStruct((L, N_DEV, n), F32),
        compiler_params=_params(("parallel", "parallel")),
    )(c_all, w, b)


def ada_wgrad(c_all_t, dmod, name):
    L, _, n = dmod.shape
    D = c_all_t.shape[0]
    tn = _col_tile(n)

    def body(c_ref, d_ref, o_ref):
        cv = c_ref[...]
        ca = cv * _sigmoid(cv)
        o_ref[...] = jnp.dot(ca, d_ref[...], preferred_element_type=F32, precision=HIGHEST)

    return pl.pallas_call(
        body, name=name, grid=(L, n // tn),
        in_specs=[pl.BlockSpec((D, N_DEV), lambda l, j: (0, 0)), pl.BlockSpec((None, N_DEV, tn), lambda l, j: (l, 0, j))],
        out_specs=pl.BlockSpec((None, D, tn), lambda l, j: (l, 0, j)),
        out_shape=jax.ShapeDtypeStruct((L, D, n), F32),
        compiler_params=_params(("parallel", "parallel")),
    )(c_all_t, dmod)


ELEMENTWISE_BLOCK_BYTES = 2 * 1024 * 1024


def _elementwise_rows(rows, row_bytes):
    for t in (1024, 512, 256, 128, 64, 32, 16):
        if rows % t == 0 and t * row_bytes <= ELEMENTWISE_BLOCK_BYTES:
            return t
    return rows


def sum_leading(a, name):
    n, M, N = a.shape
    tr = _elementwise_rows(M, n * N * 4)

    def body(a_ref, o_ref):
        acc = a_ref[0].astype(F32)
        for i in range(1, n):
            acc = acc + a_ref[i].astype(F32)
        o_ref[...] = acc

    return pl.pallas_call(
        body, name=name, grid=(M // tr,),
        in_specs=[pl.BlockSpec((n, tr, N), lambda i: (0, i, 0))],
        out_specs=pl.BlockSpec((tr, N), _row(0)),
        out_shape=jax.ShapeDtypeStruct((M, N), F32),
        compiler_params=_params(("parallel",)),
    )(a)


def add_bf16(a, b, name):
    n, M, N = a.shape
    tr = _elementwise_rows(M, N * 4)

    def body(a_ref, b_ref, o_ref):
        o_ref[...] = (a_ref[...].astype(F32) + b_ref[...].astype(F32)).astype(BF16)

    spec = pl.BlockSpec((None, tr, N), lambda k, i: (k, i, 0))
    return pl.pallas_call(
        body, name=name, grid=(n, M // tr), in_specs=[spec, spec], out_specs=spec,
        out_shape=jax.ShapeDtypeStruct((n, M, N), BF16),
        compiler_params=_params(("parallel", "parallel")),
    )(a, b)


def adamw(w, g, m, v, name):
    M, N = w.shape
    tr = _elementwise_rows(M, N * 4)
    c1 = 1.0 - ADAM_B1 ** ADAM_STEP
    c2 = 1.0 - ADAM_B2 ** ADAM_STEP

    def body(w_ref, g_ref, m_ref, v_ref, d_ref, nm_ref, nv_ref):
        g = g_ref[...]
        nm = ADAM_B1 * m_ref[...] + (1.0 - ADAM_B1) * g
        nv = ADAM_B2 * v_ref[...] + (1.0 - ADAM_B2) * (g * g)
        d_ref[...] = -ADAM_LR * ((nm / c1) / (jnp.sqrt(nv / c2) + ADAM_EPS) + ADAM_WD * w_ref[...])
        nm_ref[...] = nm
        nv_ref[...] = nv

    spec = pl.BlockSpec((tr, N), _row(0))
    shp = jax.ShapeDtypeStruct((M, N), F32)
    return pl.pallas_call(
        body, name=name, grid=(M // tr,), in_specs=[spec] * 4, out_specs=[spec] * 3, out_shape=[shp] * 3,
        compiler_params=_params(("parallel",)),
    )(w, g, m, v)


def _coords():
    return lax.axis_index("x"), lax.axis_index("y"), lax.axis_index("c")


def all_gather8(block, name):
    m_per, n = block.shape

    def body(x_ref, out_ref, send_sems, recv_sems, local_sem):
        x, y, c = _coords()
        me, sibling = (x, y, c), (x, y, 1 - c)
        chips = [(1 - x, y), (x, 1 - y), (1 - x, 1 - y)]

        def rows(px, py, pc):
            return out_ref.at[pl.ds((4 * px + 2 * py + pc) * m_per, m_per), :]

        def copy(k, blk, to, src=None):
            return pltpu.make_async_remote_copy(
                src_ref=rows(*blk) if src is None else src, dst_ref=rows(*blk),
                send_sem=send_sems.at[k], recv_sem=recv_sems.at[k], device_id=to, device_id_type=MESH)

        mine = pltpu.make_async_copy(x_ref, rows(*me), local_sem)
        mine.start()
        first = [copy(0, me, sibling, src=x_ref)]
        first += [copy(1 + j, me, (*chip, c), src=x_ref) for j, chip in enumerate(chips)]
        for cp in first:
            cp.start()
        passed = [copy(4 + j, (*chip, c), sibling) for j, chip in enumerate(chips)]
        for j, chip in enumerate(chips):
            copy(1 + j, (*chip, c), me).wait_recv()
            passed[j].start()
        copy(0, sibling, me).wait_recv()
        for j, chip in enumerate(chips):
            copy(4 + j, (*chip, 1 - c), me).wait_recv()
        for cp in first + passed:
            cp.wait_send()
        mine.wait()

    return pl.pallas_call(
        body, name=name, in_specs=[VMEM_SPEC], out_specs=VMEM_SPEC,
        out_shape=jax.ShapeDtypeStruct((N_DEV * m_per, n), block.dtype),
        scratch_shapes=[pltpu.SemaphoreType.DMA((7,)), pltpu.SemaphoreType.DMA((7,)), pltpu.SemaphoreType.DMA],
        compiler_params=_params(),
    )(block)


def _other_chips(x, y):
    return [(1 - x, y), (x, 1 - y), (1 - x, 1 - y)]


def weight_gather(stacks, units, name):
    n_u = len(units)
    n_s = len(stacks)

    def body(*refs):
        ins, outs = refs[:n_s], refs[n_s:n_s + n_u]
        send_sems, recv_sems, local_sems = refs[n_s + n_u:]
        x, y, c = _coords()
        me_s = 2 * x + y
        chips = _other_chips(x, y)
        sends, locals_ = [], []
        for u, (si, k) in enumerate(units):
            src = ins[si].at[k]
            lc = pltpu.make_async_copy(src, outs[u].at[me_s], local_sems.at[u])
            lc.start()
            locals_.append(lc)
            for j, chip in enumerate(chips):
                cp = pltpu.make_async_remote_copy(
                    src_ref=src, dst_ref=outs[u].at[me_s], send_sem=send_sems.at[3 * u + j],
                    recv_sem=recv_sems.at[3 * u + j], device_id=(*chip, c), device_id_type=MESH)
                cp.start()
                sends.append(cp)
        for u, (si, k) in enumerate(units):
            for j, (px, py) in enumerate(chips):
                pltpu.make_async_remote_copy(
                    src_ref=ins[si].at[k], dst_ref=outs[u].at[2 * px + py], send_sem=send_sems.at[3 * u + j],
                    recv_sem=recv_sems.at[3 * u + j], device_id=(px, py, c), device_id_type=MESH).wait_recv()
        for cp in sends:
            cp.wait_send()
        for lc in locals_:
            lc.wait()

    out_shape = [jax.ShapeDtypeStruct((N_CHIP,) + stacks[si].shape[1:], stacks[si].dtype) for si, _ in units]
    return pl.pallas_call(
        body, name=name, in_specs=[ANY] * n_s, out_specs=[ANY] * n_u, out_shape=out_shape,
        scratch_shapes=[pltpu.SemaphoreType.DMA((3 * n_u,)), pltpu.SemaphoreType.DMA((3 * n_u,)),
                        pltpu.SemaphoreType.DMA((n_u,))],
        compiler_params=_params(),
    )(*stacks)


def grad_pair_exchange(grads, name):
    n_u = len(grads)

    def body(*refs):
        ins, outs = refs[:n_u], refs[n_u:2 * n_u]
        send_sems, recv_sems = refs[2 * n_u:]
        x, y, c = _coords()
        cps = []
        for u in range(n_u):
            half = ins[u].shape[1] // 2
            cp = pltpu.make_async_remote_copy(
                src_ref=ins[u].at[:, pl.ds((1 - c) * half, half), :], dst_ref=outs[u],
                send_sem=send_sems.at[u], recv_sem=recv_sems.at[u], device_id=(x, y, 1 - c), device_id_type=MESH)
            cp.start()
            cps.append(cp)
        for cp in cps:
            cp.wait()

    out_shape = [jax.ShapeDtypeStruct((g.shape[0], g.shape[1] // 2, g.shape[2]), g.dtype) for g in grads]
    return pl.pallas_call(
        body, name=name, in_specs=[ANY] * n_u, out_specs=[ANY] * n_u, out_shape=out_shape,
        scratch_shapes=[pltpu.SemaphoreType.DMA((n_u,)), pltpu.SemaphoreType.DMA((n_u,))],
        compiler_params=_params(),
    )(*grads)


def grad_chip_exchange(psums, name):
    n_u = len(psums)

    def body(*refs):
        ins, outs = refs[:n_u], refs[n_u:2 * n_u]
        send_sems, recv_sems, local_sems = refs[2 * n_u:]
        x, y, c = _coords()
        me_s = 2 * x + y
        chips = _other_chips(x, y)
        sends, locals_ = [], []
        for u in range(n_u):
            lc = pltpu.make_async_copy(ins[u].at[me_s], outs[u].at[me_s], local_sems.at[u])
            lc.start()
            locals_.append(lc)
            for j, (px, py) in enumerate(chips):
                cp = pltpu.make_async_remote_copy(
                    src_ref=ins[u].at[2 * px + py], dst_ref=outs[u].at[me_s], send_sem=send_sems.at[3 * u + j],
                    recv_sem=recv_sems.at[3 * u + j], device_id=(px, py, c), device_id_type=MESH)
                cp.start()
                sends.append(cp)
        for u in range(n_u):
            for j, (px, py) in enumerate(chips):
                pltpu.make_async_remote_copy(
                    src_ref=ins[u].at[me_s], dst_ref=outs[u].at[2 * px + py], send_sem=send_sems.at[3 * u + j],
                    recv_sem=recv_sems.at[3 * u + j], device_id=(px, py, c), device_id_type=MESH).wait_recv()
        for cp in sends:
            cp.wait_send()
        for lc in locals_:
            lc.wait()

    out_shape = [jax.ShapeDtypeStruct(p.shape, p.dtype) for p in psums]
    return pl.pallas_call(
        body, name=name, in_specs=[ANY] * n_u, out_specs=[ANY] * n_u, out_shape=out_shape,
        scratch_shapes=[pltpu.SemaphoreType.DMA((3 * n_u,)), pltpu.SemaphoreType.DMA((3 * n_u,)),
                        pltpu.SemaphoreType.DMA((n_u,))],
        compiler_params=_params(),
    )(*psums)


def grad_half_exchange(halves, units, stack_shapes, name):
    n_u = len(halves)
    n_s = len(stack_shapes)

    def body(*refs):
        ins, outs = refs[:n_u], refs[n_u:n_u + n_s]
        send_sems, recv_sems, local_sems = refs[n_u + n_s:]
        x, y, c = _coords()
        work = []
        for u, (si, k) in enumerate(units):
            half = ins[u].shape[0]
            mine = outs[si].at[k, pl.ds(c * half, half), :]
            lc = pltpu.make_async_copy(ins[u], mine, local_sems.at[u])
            lc.start()
            cp = pltpu.make_async_remote_copy(
                src_ref=ins[u], dst_ref=mine, send_sem=send_sems.at[u], recv_sem=recv_sems.at[u],
                device_id=(x, y, 1 - c), device_id_type=MESH)
            cp.start()
            work.append((lc, cp))
        for u, (si, k) in enumerate(units):
            half = ins[u].shape[0]
            pltpu.make_async_remote_copy(
                src_ref=ins[u], dst_ref=outs[si].at[k, pl.ds((1 - c) * half, half), :], send_sem=send_sems.at[u],
                recv_sem=recv_sems.at[u], device_id=(x, y, 1 - c), device_id_type=MESH).wait_recv()
        for lc, cp in work:
            cp.wait_send()
            lc.wait()

    out_shape = [jax.ShapeDtypeStruct(s, F32) for s in stack_shapes]
    return pl.pallas_call(
        body, name=name, in_specs=[ANY] * n_u, out_specs=[ANY] * n_s, out_shape=out_shape,
        scratch_shapes=[pltpu.SemaphoreType.DMA((n_u,)), pltpu.SemaphoreType.DMA((n_u,)),
                        pltpu.SemaphoreType.DMA((n_u,))],
        compiler_params=_params(),
    )(*halves)


def _pack_rows(parts):
    flat = jnp.concatenate([p.reshape(-1).astype(F32) for p in parts])
    n = flat.shape[0]
    padded = -(-n // (8 * LANES)) * (8 * LANES)
    return jnp.pad(flat, (0, padded - n)).reshape(-1, LANES)


def _unpack_rows(packed, shapes):
    flat = packed.reshape(-1)
    out, off = [], 0
    for s in shapes:
        size = 1
        for d in s:
            size *= d
        out.append(flat[off:off + size].reshape(s))
        off += size
    return out


def _shard_last(full, s_me):
    n = full.shape[-1] // N_CHIP
    return lax.dynamic_slice_in_dim(full, s_me * n, n, axis=full.ndim - 1)


def _unshard_last(g):
    moved = jnp.moveaxis(g, 0, -2)
    return moved.reshape(moved.shape[:-2] + (moved.shape[-2] * moved.shape[-1],))


def kernel(x, c, w_ada, b_ada, ln_g, ln_b, ffn_gu, ffn_down, gmlp_w_in, gmlp_b_in, gmlp_ln_g, gmlp_ln_b, gmlp_w_s, gmlp_b_s, gmlp_w_out, w_ada_kv, b_ada_kv, w_kv, attn_w_q, attn_rel_bias, attn_w_o, loss_target, m_w_ada, m_b_ada, m_ln_g, m_ln_b, m_ffn_gu, m_ffn_down, m_gmlp_w_in, m_gmlp_b_in, m_gmlp_ln_g, m_gmlp_ln_b, m_gmlp_w_s, m_gmlp_b_s, m_gmlp_w_out, m_w_ada_kv, m_b_ada_kv, m_w_kv, m_attn_w_q, m_attn_rel_bias, m_attn_w_o, v_w_ada, v_b_ada, v_ln_g, v_ln_b, v_ffn_gu, v_ffn_down, v_gmlp_w_in, v_gmlp_b_in, v_gmlp_ln_g, v_gmlp_ln_b, v_gmlp_w_s, v_gmlp_b_s, v_gmlp_w_out, v_w_ada_kv, v_b_ada_kv, v_w_kv, v_attn_w_q, v_attn_rel_bias, v_attn_w_o):
    xi, yi, ci = _coords()
    s_me = 2 * xi + yi
    dev = 4 * xi + 2 * yi + ci

    x0 = x[0]
    tgt = loss_target[0]
    S, D = x0.shape
    L = w_ada.shape[0]
    NA = gmlp_w_in.shape[0]
    NB = attn_w_q.shape[0]
    NH = D // HEAD_DIM
    alpha = (2.0 * L) ** 0.25
    n_ada = w_ada.shape[2]
    n_kv = w_ada_kv.shape[1]

    c_all = all_gather8(jnp.broadcast_to(c, (8, D)), "ag_c").reshape(N_DEV, 8, D)[:, 0]
    b_ada_sh = lax.dynamic_slice_in_dim(b_ada, s_me * n_ada, n_ada, axis=1)
    b_kv_sh = lax.dynamic_slice_in_dim(b_ada_kv, s_me * n_kv, n_kv, axis=0)
    mod_part = ada_fwd(c_all, w_ada, b_ada_sh[:, None, :], "ada_fwd")
    mkv_part = ada_fwd(c_all, w_ada_kv[None], b_kv_sh[None, None, :], "ada_kv_fwd")
    part = jnp.concatenate([jnp.transpose(mod_part, (1, 0, 2)).reshape(N_DEV, L * n_ada), mkv_part[0]], axis=1)
    width = part.shape[1]
    pad_w = -(-width // LANES) * LANES - width
    all_part = all_gather8(jnp.pad(part, ((0, 0), (0, pad_w))), "ag_mod").reshape(N_DEV, N_DEV, width + pad_w)
    mine = lax.dynamic_index_in_dim(all_part[0::2], dev, axis=1, keepdims=False)
    mod = jnp.transpose(mine[:, :L * n_ada].reshape(N_CHIP, L, n_ada), (1, 0, 2)).reshape(L, N_MOD, D)
    mkv = mine[:, L * n_ada:width].reshape(2, D)

    def mrow(l, k):
        return mod[l, k][None, :]

    stack_names = ["ffn_gu", "ffn_down", "gmlp_w_in", "gmlp_w_out", "w_kv", "attn_w_q", "attn_w_o"]
    stack_src = dict(ffn_gu=ffn_gu, ffn_down=ffn_down, gmlp_w_in=gmlp_w_in, gmlp_w_out=gmlp_w_out, w_kv=w_kv[None],
                     attn_w_q=attn_w_q, attn_w_o=attn_w_o)
    stacks = [stack_src[nm].astype(BF16).reshape((-1,) + stack_src[nm].shape[-2:]) for nm in stack_names]
    units = [(si, k) for si, st in enumerate(stacks) for k in range(st.shape[0])]
    unit_of = {(stack_names[si], k): u for u, (si, k) in enumerate(units)}

    small_shapes = [ln_g.shape, ln_b.shape, gmlp_b_in.shape, gmlp_ln_g.shape, gmlp_ln_b.shape, attn_rel_bias.shape]
    small_pack = _pack_rows([ln_g, ln_b, gmlp_b_in, gmlp_ln_g, gmlp_ln_b, attn_rel_bias])
    gathered = weight_gather(stacks + [small_pack[None]], units + [(len(stacks), 0)], "weight_gather")
    small_all = gathered[-1]
    wg = gathered[:-1]
    sm = [_unpack_rows(small_all[s], small_shapes) for s in range(N_CHIP)]
    ln_g_f, ln_b_f, b_in_f, gln_g_f, gln_b_f, rel_f = [
        _unshard_last(jnp.stack([sm[s][i] for s in range(N_CHIP)])) for i in range(len(small_shapes))]

    def W(nm, k):
        return wg[unit_of[(nm, k)]]

    def Wrows(nm, k):
        w4 = W(nm, k)
        return w4.reshape(w4.shape[0] * w4.shape[1], w4.shape[2])

    bst = [jnp.transpose(gmlp_b_s[j]) for j in range(NA)]
    biases = [jnp.transpose(bias_expand(rel_f[j], "bias_expand"), (1, 0, 2)) for j in range(NB)]

    saved = []
    xc = x0
    kpad = vpad = xkv = None
    for l in range(L):
        sv = {}
        for i in (0, 2):
            k = 2 * l + i // 2
            gu = mod_matmul(xc, mrow(l, 3 * i + 1), mrow(l, 3 * i), W("ffn_gu", k), None, BF16, "ffn_up")
            gw = 0.5 * (1.0 + mrow(l, 3 * i + 2))
            xn, xh, rs, yv = matmul_res_ln(gu, Wrows("ffn_down", k), xc, gw, ln_g_f[l, i][None], ln_b_f[l, i][None],
                                           alpha, True, "ffn_down")
            sv[i] = dict(x=xc, gu=gu, xh=xh, rs=rs, y=yv, gw=gw)
            xc = xn
            if i == 0:
                gw = 1.0 + mrow(l, 5)
                if l < NA:
                    pre = mod_matmul(xc, mrow(l, 4), mrow(l, 3), W("gmlp_w_in", l), b_in_f[l][None], F32, "gmlp_in")
                    qv = sgu_fwd(pre, gln_g_f[l][None], gln_b_f[l][None], gmlp_w_s[l], bst[l], "sgu_fwd")
                    xn, xh, rs, yv = matmul_res_ln(qv, Wrows("gmlp_w_out", l), xc, gw, ln_g_f[l, 1][None],
                                                   ln_b_f[l, 1][None], alpha, False, "gmlp_out")
                    sv[1] = dict(x=xc, pre=pre, a=qv, xh=xh, rs=rs, y=yv, gw=gw)
                else:
                    j = l - NA
                    qh = mod_matmul(xc, mrow(l, 4), mrow(l, 3), Wrows("attn_w_q", j)[None], None, BF16, "attn_q")
                    ov = attn_fwd(qh, kpad, vpad, biases[j], "attn_fwd")
                    xn, xh, rs, yv = matmul_res_ln(ov, Wrows("attn_w_o", j), xc, gw, ln_g_f[l, 1][None],
                                                   ln_b_f[l, 1][None], alpha, False, "attn_out")
                    sv[1] = dict(x=xc, q=qh, a=ov, xh=xh, rs=rs, y=yv, gw=gw)
                xc = xn
        saved.append(sv)
        if l == NA - 1:
            xkv = xc
            kv = mod_matmul(xc, mkv[1][None], mkv[0][None], W("w_kv", 0), None, BF16, "kv_proj")
            kpad = jnp.pad(kv[:, :D], ((LEFT_PAD, 0), (0, 0)))
            vpad = jnp.pad(kv[:, D:], ((LEFT_PAD, 0), (0, 0)))

    dx, lacc = loss_grad(xc, tgt, "loss_grad")
    loss = lax.psum((0.5 / D) * jnp.sum(lacc[0]), ("x", "y", "c"))

    gfull = [None] * len(units)
    dmod = [[None] * N_MOD for _ in range(L)]
    d_ln_g = [[None] * 3 for _ in range(L)]
    d_ln_b = [[None] * 3 for _ in range(L)]
    d_b_in, d_gln_g, d_gln_b, d_ws, d_bs, d_rel = ([None] * NA, [None] * NA, [None] * NA, [None] * NA, [None] * NA,
                                                  [None] * NB)
    dk = jnp.zeros((S, D), F32)
    dv = jnp.zeros((S, D), F32)
    dmkv = None

    def put(nm, k, g):
        gfull[unit_of[(nm, k)]] = g.reshape((N_CHIP, -1, g.shape[-1]))

    for l in reversed(range(L)):
        if l == NA - 1:
            dkv = jnp.concatenate([dk, dv], axis=1)
            put("w_kv", 0, wgrad("mod", (xkv, mkv[1][None], mkv[0][None]), dkv, D, dkv.shape[1] // N_CHIP, "kv_wgrad"))
            dx, acc = dgrad_mod(dkv, W("w_kv", 0), dx, xkv, mkv[1][None], "kv_dgrad")
            dmkv = jnp.stack([acc[1], acc[0]])
        sv = saved[l]
        for i in (2, 1, 0):
            t = sv[i]
            wres = 1.0 if i == 1 else 0.5
            dxa, dy, acc1 = ln_res_bwd(dx, t["xh"], t["rs"], ln_g_f[l, i][None], t["y"], t["gw"], wres, alpha,
                                       "ln_res_bwd")
            d_ln_g[l][i], d_ln_b[l][i], dmod[l][3 * i + 2] = acc1[0], acc1[1], acc1[2]
            scl, shf = mrow(l, 3 * i + 1), mrow(l, 3 * i)
            if i != 1:
                k = 2 * l + i // 2
                F = t["gu"].shape[1] // 2
                dgu = ffn_act_bwd(dy, Wrows("ffn_down", k), t["gu"], "ffn_act_bwd")
                put("ffn_down", k, wgrad("swiglu", (t["gu"],), dy, F // 2, D, "ffn_down_wgrad"))
                put("ffn_gu", k, wgrad("mod", (t["x"], scl, shf), dgu, D, 2 * F // N_CHIP, "ffn_up_wgrad"))
                dx, acc2 = dgrad_mod(dgu, W("ffn_gu", k), dxa, t["x"], scl, "ffn_up_dgrad")
            elif l < NA:
                dq = matmul_nt(dy, Wrows("gmlp_w_out", l), "gmlp_out_dgrad")
                put("gmlp_w_out", l, wgrad("plain", (t["a"],), dy, t["a"].shape[1], D, "gmlp_out_wgrad"))
                dpre, dws_l, dss, dgl, dbin = sgu_bwd(dq, t["pre"], gln_g_f[l][None], gln_b_f[l][None], gmlp_w_s[l],
                                                      bst[l], "sgu_bwd")
                d_ws[l] = dws_l
                d_bs[l] = jnp.transpose(group_lane_sum(dss, "sgu_bias_grad")[:, :GMLP_GROUPS])
                d_gln_g[l], d_gln_b[l], d_b_in[l] = dgl[0], dgl[1], dbin[0]
                put("gmlp_w_in", l, wgrad("mod", (t["x"], scl, shf), dpre, D, dpre.shape[1] // N_CHIP, "gmlp_in_wgrad"))
                dx, acc2 = dgrad_mod(dpre, W("gmlp_w_in", l), dxa, t["x"], scl, "gmlp_in_dgrad")
            else:
                j = l - NA
                do = matmul_nt(dy, Wrows("attn_w_o", j), "attn_out_dgrad")
                put("attn_w_o", j, wgrad("plain", (t["a"],), dy, D, D, "attn_out_wgrad"))
                dqh, dk, dv, dbias = attn_bwd(t["q"], do, kpad, vpad, biases[j], dk, dv, "attn_bwd")
                d_rel[j] = bias_grad(jnp.transpose(dbias, (1, 0, 2)), "bias_grad")
                put("attn_w_q", j, wgrad("mod", (t["x"], scl, shf), dqh, D, D, "attn_q_wgrad"))
                dx, acc2 = dgrad_mod(dqh, Wrows("attn_w_q", j)[None], dxa, t["x"], scl, "attn_q_dgrad")
            dmod[l][3 * i + 1], dmod[l][3 * i] = acc2[0], acc2[1]
    grad_x = dx[None]

    dvec = _pack_rows([jnp.stack([jnp.stack(r) for r in dmod]), dmkv])
    n_dvec = L * N_MOD * D + 2 * D
    dall = all_gather8(dvec, "ag_dmod").reshape(N_DEV, -1, LANES)
    db_all = sum_leading(dall, "ada_bias_grad").reshape(-1)[:n_dvec]
    g_b_ada = db_all[:L * N_MOD * D].reshape(L, N_MOD * D)
    g_b_ada_kv = db_all[L * N_MOD * D:]
    dall2 = dall.reshape(N_DEV, -1)[:, :n_dvec]
    dmod_all = dall2[:, :L * N_MOD * D].reshape(N_DEV, L, N_MOD * D)
    dmod_sh = jnp.transpose(lax.dynamic_slice_in_dim(dmod_all, s_me * n_ada, n_ada, axis=2), (1, 0, 2))
    dmkv_sh = lax.dynamic_slice_in_dim(dall2[:, L * N_MOD * D:], s_me * n_kv, n_kv, axis=1)[None]
    c_all_t = jnp.transpose(c_all)
    g_w_ada = ada_wgrad(c_all_t, dmod_sh, "ada_wgrad")
    g_w_ada_kv = ada_wgrad(c_all_t, dmkv_sh, "ada_kv_wgrad")[0]

    small_g = [jnp.stack([jnp.stack(r) for r in d_ln_g]), jnp.stack([jnp.stack(r) for r in d_ln_b]),
               jnp.stack(d_b_in), jnp.stack(d_gln_g), jnp.stack(d_gln_b), jnp.stack(d_rel), jnp.stack(d_ws),
               jnp.stack(d_bs)]
    sg_shapes = [a.shape for a in small_g]
    sg_pack = _pack_rows(small_g)
    sg_all = all_gather8(sg_pack, "ag_small_grads").reshape(N_DEV, -1, LANES)
    sg_sum = _unpack_rows(sum_leading(sg_all, "small_grad_sum"), sg_shapes)
    g_ln_g, g_ln_b, g_b_in, g_gln_g, g_gln_b, g_rel = [_shard_last(a, s_me) for a in sg_sum[:6]]
    g_ws, g_bs = sg_sum[6], sg_sum[7]

    from_sibling = grad_pair_exchange(gfull, "grad_pair_exchange")
    psums = []
    for u in range(len(units)):
        half = gfull[u].shape[1] // 2
        own = lax.dynamic_slice_in_dim(gfull[u], ci * half, half, axis=1)
        psums.append(add_bf16(own, from_sibling[u], "grad_pair_sum"))
    from_chips = grad_chip_exchange(psums, "grad_chip_exchange")
    halves = [sum_leading(q4, "grad_chip_sum") for q4 in from_chips]
    stack_shapes = [stack_src[nm].reshape((-1,) + stack_src[nm].shape[-2:]).shape for nm in stack_names]
    gstacks = grad_half_exchange(halves, units, stack_shapes, "grad_half_exchange")
    big_g = {nm: gstacks[si].reshape(stack_src[nm].shape) for si, nm in enumerate(stack_names)}
    big_g["w_kv"] = big_g["w_kv"][0]

    grads = dict(w_ada=g_w_ada, b_ada=g_b_ada, ln_g=g_ln_g, ln_b=g_ln_b, ffn_gu=big_g["ffn_gu"],
                 ffn_down=big_g["ffn_down"], gmlp_w_in=big_g["gmlp_w_in"], gmlp_b_in=g_b_in, gmlp_ln_g=g_gln_g,
                 gmlp_ln_b=g_gln_b, gmlp_w_s=g_ws, gmlp_b_s=g_bs, gmlp_w_out=big_g["gmlp_w_out"],
                 w_ada_kv=g_w_ada_kv, b_ada_kv=g_b_ada_kv, w_kv=big_g["w_kv"], attn_w_q=big_g["attn_w_q"],
                 attn_rel_bias=g_rel, attn_w_o=big_g["attn_w_o"])
    weights = dict(w_ada=w_ada, b_ada=b_ada, ln_g=ln_g, ln_b=ln_b, ffn_gu=ffn_gu, ffn_down=ffn_down,
                   gmlp_w_in=gmlp_w_in, gmlp_b_in=gmlp_b_in, gmlp_ln_g=gmlp_ln_g, gmlp_ln_b=gmlp_ln_b,
                   gmlp_w_s=gmlp_w_s, gmlp_b_s=gmlp_b_s, gmlp_w_out=gmlp_w_out, w_ada_kv=w_ada_kv,
                   b_ada_kv=b_ada_kv, w_kv=w_kv, attn_w_q=attn_w_q, attn_rel_bias=attn_rel_bias, attn_w_o=attn_w_o)
    ms = dict(w_ada=m_w_ada, b_ada=m_b_ada, ln_g=m_ln_g, ln_b=m_ln_b, ffn_gu=m_ffn_gu, ffn_down=m_ffn_down,
              gmlp_w_in=m_gmlp_w_in, gmlp_b_in=m_gmlp_b_in, gmlp_ln_g=m_gmlp_ln_g, gmlp_ln_b=m_gmlp_ln_b,
              gmlp_w_s=m_gmlp_w_s, gmlp_b_s=m_gmlp_b_s, gmlp_w_out=m_gmlp_w_out, w_ada_kv=m_w_ada_kv,
              b_ada_kv=m_b_ada_kv, w_kv=m_w_kv, attn_w_q=m_attn_w_q, attn_rel_bias=m_attn_rel_bias,
              attn_w_o=m_attn_w_o)
    vs = dict(w_ada=v_w_ada, b_ada=v_b_ada, ln_g=v_ln_g, ln_b=v_ln_b, ffn_gu=v_ffn_gu, ffn_down=v_ffn_down,
              gmlp_w_in=v_gmlp_w_in, gmlp_b_in=v_gmlp_b_in, gmlp_ln_g=v_gmlp_ln_g, gmlp_ln_b=v_gmlp_ln_b,
              gmlp_w_s=v_gmlp_w_s, gmlp_b_s=v_gmlp_b_s, gmlp_w_out=v_gmlp_w_out, w_ada_kv=v_w_ada_kv,
              b_ada_kv=v_b_ada_kv, w_kv=v_w_kv, attn_w_q=v_attn_w_q, attn_rel_bias=v_attn_rel_bias,
              attn_w_o=v_attn_w_o)
    order = ["w_ada", "b_ada", "ln_g", "ln_b", "ffn_gu", "ffn_down", "gmlp_w_in", "gmlp_b_in", "gmlp_ln_g",
             "gmlp_ln_b", "gmlp_w_s", "gmlp_b_s", "gmlp_w_out", "w_ada_kv", "b_ada_kv", "w_kv", "attn_w_q",
             "attn_rel_bias", "attn_w_o"]
    big_names = ["w_ada", "ffn_gu", "ffn_down", "gmlp_w_in", "gmlp_w_out", "w_ada_kv", "w_kv", "attn_w_q", "attn_w_o"]
    small_names = [nm for nm in order if nm not in big_names]
    delta, new_m, new_v = {}, {}, {}
    for nm in big_names:
        shp = weights[nm].shape
        two_d = (-1, shp[-1])
        d, a, b = adamw(weights[nm].reshape(two_d), grads[nm].reshape(two_d), ms[nm].reshape(two_d),
                        vs[nm].reshape(two_d), "adamw")
        delta[nm], new_m[nm], new_v[nm] = d.reshape(shp), a.reshape(shp), b.reshape(shp)
    shapes = [weights[nm].shape for nm in small_names]
    d, a, b = adamw(_pack_rows([weights[nm] for nm in small_names]), _pack_rows([grads[nm] for nm in small_names]),
                    _pack_rows([ms[nm] for nm in small_names]), _pack_rows([vs[nm] for nm in small_names]),
                    "adamw_small")
    for nm, dd, aa, bb in zip(small_names, _unpack_rows(d, shapes), _unpack_rows(a, shapes), _unpack_rows(b, shapes)):
        delta[nm], new_m[nm], new_v[nm] = dd, aa, bb

    return (loss, grad_x, *[grads[nm] for nm in order], *[delta[nm] for nm in order],
            *[new_m[nm] for nm in order], *[new_v[nm] for nm in order])
```

```python
import functools

import jax
import jax.numpy as jnp
from jax import lax
from jax.experimental import pallas as pl
from jax.experimental.pallas import tpu as pltpu

F32 = jnp.float32
BF16 = jnp.bfloat16
MESH = pl.DeviceIdType.MESH
HIGHEST = lax.Precision.HIGHEST

CHUNK = 64
GMLP_WINDOW = 128
GMLP_GROUPS = 8
HEAD_DIM = 64
LEFT_CHUNKS = 8
BAND = (LEFT_CHUNKS + 1) * CHUNK
LEFT_PAD = LEFT_CHUNKS * CHUNK
MAX_REL = 4 * CHUNK
N_REL = (CHUNK - 1) + MAX_REL + 1
LN_EPS = 1e-5
N_MOD = 9
N_DEV = 8
N_CHIP = 4

ADAM_LR = 0.001
ADAM_B1 = 0.9
ADAM_B2 = 0.999
ADAM_EPS = 1e-08
ADAM_WD = 0.01
ADAM_STEP = 10

LANES = 128
ROW_TILE = 256
WGRAD_ROWS = 512
ATTN_CHUNKS_PER_STEP = 4
VMEM_LIMIT_MB = 56

NT = (((1,), (1,)), ((), ()))
TN = (((0,), (0,)), ((), ()))

ANY = pl.BlockSpec(memory_space=pl.ANY)
VMEM_SPEC = pl.BlockSpec(memory_space=pltpu.VMEM)


def _params(semantics=None):
    kw = dict(vmem_limit_bytes=VMEM_LIMIT_MB * 1024 * 1024)
    if semantics is not None:
        kw["dimension_semantics"] = semantics
    return pltpu.CompilerParams(**kw)


def _sigmoid(v):
    return 1.0 / (1.0 + jnp.exp(-v))


def _gelu(v):
    return 0.5 * v * (1.0 + lax.erf(v * (2.0 ** -0.5)))


def _gelu_grad(v):
    return 0.5 * (1.0 + lax.erf(v * (2.0 ** -0.5))) + v * jnp.exp(-0.5 * v * v) * ((2.0 * jnp.pi) ** -0.5)


def _row(m):
    return lambda i: (i, 0)


def _fixed2(i):
    return (0, 0)


def _fixed3(i):
    return (0, 0, 0)


def mod_matmul(x, scl, shift, w, bias, out_dtype, name):
    S, D = x.shape
    NS, _, n = w.shape
    tm = min(ROW_TILE, S)
    has_bias = bias is not None

    def body(*refs):
        if has_bias:
            x_ref, scl_ref, sh_ref, w_ref, b_ref, o_ref = refs
        else:
            x_ref, scl_ref, sh_ref, w_ref, o_ref = refs
        h = (x_ref[...] * (1.0 + scl_ref[...]) + sh_ref[...]).astype(BF16)
        for s in range(NS):
            acc = jnp.dot(h, w_ref[s], preferred_element_type=F32)
            if has_bias:
                acc = acc + b_ref[:, s * n:(s + 1) * n]
            o_ref[:, s * n:(s + 1) * n] = acc.astype(out_dtype)

    in_specs = [pl.BlockSpec((tm, D), _row(0)), pl.BlockSpec((1, D), _fixed2), pl.BlockSpec((1, D), _fixed2),
                pl.BlockSpec((NS, D, n), _fixed3)]
    args = [x, scl, shift, w]
    if has_bias:
        in_specs.append(pl.BlockSpec((1, NS * n), _fixed2))
        args.append(bias)
    return pl.pallas_call(
        body, name=name, grid=(S // tm,), in_specs=in_specs,
        out_specs=pl.BlockSpec((tm, NS * n), _row(0)),
        out_shape=jax.ShapeDtypeStruct((S, NS * n), out_dtype),
        compiler_params=_params(("parallel",)),
    )(*args)


def matmul_res_ln(a, w, x, gw, lg, lb, alpha, swiglu, name):
    S, D = x.shape
    K = w.shape[0]
    tm = min(ROW_TILE, S)
    ka = a.shape[1]

    def body(a_ref, w_ref, x_ref, gw_ref, lg_ref, lb_ref, xn_ref, xh_ref, rs_ref, y_ref):
        if swiglu:
            g = a_ref[:, :K].astype(F32)
            u = a_ref[:, K:].astype(F32)
            act = (g * _sigmoid(g) * u).astype(BF16)
        else:
            act = a_ref[...].astype(BF16)
        y = jnp.dot(act, w_ref[...], preferred_element_type=F32)
        z = alpha * x_ref[...] + gw_ref[...] * y
        mu = jnp.mean(z, axis=-1, keepdims=True)
        zc = z - mu
        var = jnp.mean(zc * zc, axis=-1, keepdims=True)
        rstd = lax.rsqrt(var + LN_EPS)
        xhat = zc * rstd
        xn_ref[...] = xhat * lg_ref[...] + lb_ref[...]
        xh_ref[...] = xhat
        rs_ref[...] = rstd
        y_ref[...] = y.astype(BF16)

    vec = pl.BlockSpec((1, D), _fixed2)
    return pl.pallas_call(
        body, name=name, grid=(S // tm,),
        in_specs=[pl.BlockSpec((tm, ka), _row(0)), pl.BlockSpec((K, D), _fixed2), pl.BlockSpec((tm, D), _row(0)),
                  vec, vec, vec],
        out_specs=[pl.BlockSpec((tm, D), _row(0)), pl.BlockSpec((tm, D), _row(0)), pl.BlockSpec((tm, 1), _row(0)),
                   pl.BlockSpec((tm, D), _row(0))],
        out_shape=[jax.ShapeDtypeStruct((S, D), F32), jax.ShapeDtypeStruct((S, D), F32),
                   jax.ShapeDtypeStruct((S, 1), F32), jax.ShapeDtypeStruct((S, D), BF16)],
        compiler_params=_params(("parallel",)),
    )(a, w, x, gw, lg, lb)


def ln_res_bwd(dxn, xhat, rstd, lg, y, gw, wres, alpha, name):
    S, D = dxn.shape
    tm = min(ROW_TILE, S)

    def body(dxn_ref, xh_ref, rs_ref, lg_ref, y_ref, gw_ref, dxa_ref, dy_ref, acc_ref):
        @pl.when(pl.program_id(0) == 0)
        def _():
            acc_ref[...] = jnp.zeros_like(acc_ref)

        d = dxn_ref[...]
        xh = xh_ref[...]
        dxh = d * lg_ref[...]
        m1 = jnp.mean(dxh, axis=-1, keepdims=True)
        m2 = jnp.mean(dxh * xh, axis=-1, keepdims=True)
        dz = rs_ref[...] * (dxh - m1 - xh * m2)
        dxa_ref[...] = alpha * dz
        dy_ref[...] = (gw_ref[...] * dz).astype(BF16)
        acc_ref[0:1, :] += jnp.sum(d * xh, axis=0, keepdims=True)
        acc_ref[1:2, :] += jnp.sum(d, axis=0, keepdims=True)
        acc_ref[2:3, :] += jnp.sum((wres * dz) * y_ref[...].astype(F32), axis=0, keepdims=True)

    vec = pl.BlockSpec((1, D), _fixed2)
    tile = pl.BlockSpec((tm, D), _row(0))
    return pl.pallas_call(
        body, name=name, grid=(S // tm,),
        in_specs=[tile, tile, pl.BlockSpec((tm, 1), _row(0)), vec, tile, vec],
        out_specs=[tile, tile, pl.BlockSpec((8, D), _fixed2)],
        out_shape=[jax.ShapeDtypeStruct((S, D), F32), jax.ShapeDtypeStruct((S, D), BF16),
                   jax.ShapeDtypeStruct((8, D), F32)],
        compiler_params=_params(("arbitrary",)),
    )(dxn, xhat, rstd, lg, y, gw)


def ffn_act_bwd(dy, wd, gu, name):
    S, D = dy.shape
    K = wd.shape[0]
    tm = min(ROW_TILE, S)

    def body(dy_ref, wd_ref, gu_ref, o_ref):
        da = lax.dot_general(dy_ref[...], wd_ref[...], NT, preferred_element_type=F32)
        g = gu_ref[:, :K].astype(F32)
        u = gu_ref[:, K:].astype(F32)
        sg = _sigmoid(g)
        o_ref[:, :K] = (da * u * (sg * (1.0 + g * (1.0 - sg)))).astype(BF16)
        o_ref[:, K:] = (da * (g * sg)).astype(BF16)

    return pl.pallas_call(
        body, name=name, grid=(S // tm,),
        in_specs=[pl.BlockSpec((tm, D), _row(0)), pl.BlockSpec((K, D), _fixed2), pl.BlockSpec((tm, 2 * K), _row(0))],
        out_specs=pl.BlockSpec((tm, 2 * K), _row(0)),
        out_shape=jax.ShapeDtypeStruct((S, 2 * K), BF16),
        compiler_params=_params(("parallel",)),
    )(dy, wd, gu)


def matmul_nt(a, w, name):
    S, D = a.shape
    K = w.shape[0]
    tm = min(ROW_TILE, S)

    def body(a_ref, w_ref, o_ref):
        o_ref[...] = lax.dot_general(a_ref[...], w_ref[...], NT, preferred_element_type=F32).astype(BF16)

    return pl.pallas_call(
        body, name=name, grid=(S // tm,),
        in_specs=[pl.BlockSpec((tm, D), _row(0)), pl.BlockSpec((K, D), _fixed2)],
        out_specs=pl.BlockSpec((tm, K), _row(0)),
        out_shape=jax.ShapeDtypeStruct((S, K), BF16),
        compiler_params=_params(("parallel",)),
    )(a, w)


def dgrad_mod(dpre, w, dxa, xin, scl, name):
    S, D = xin.shape
    NS, _, n = w.shape
    tm = min(ROW_TILE, S)

    def body(dp_ref, w_ref, dxa_ref, xin_ref, scl_ref, dx_ref, acc_ref):
        @pl.when(pl.program_id(0) == 0)
        def _():
            acc_ref[...] = jnp.zeros_like(acc_ref)

        dh = jnp.zeros((tm, D), F32)
        for s in range(NS):
            dh = dh + lax.dot_general(dp_ref[:, s * n:(s + 1) * n].astype(BF16), w_ref[s], NT,
                                      preferred_element_type=F32)
        dx_ref[...] = dxa_ref[...] + dh * (1.0 + scl_ref[...])
        acc_ref[0:1, :] += jnp.sum(dh * xin_ref[...], axis=0, keepdims=True)
        acc_ref[1:2, :] += jnp.sum(dh, axis=0, keepdims=True)

    tile = pl.BlockSpec((tm, D), _row(0))
    return pl.pallas_call(
        body, name=name, grid=(S // tm,),
        in_specs=[pl.BlockSpec((tm, NS * n), _row(0)), pl.BlockSpec((NS, D, n), _fixed3), tile, tile,
                  pl.BlockSpec((1, D), _fixed2)],
        out_specs=[tile, pl.BlockSpec((8, D), _fixed2)],
        out_shape=[jax.ShapeDtypeStruct((S, D), F32), jax.ShapeDtypeStruct((8, D), F32)],
        compiler_params=_params(("arbitrary",)),
    )(dpre, w, dxa, xin, scl)


def wgrad(kind, a_args, b, kb, nb, name):
    S, N = b.shape
    if kind == "swiglu":
        K = a_args[0].shape[1] // 2
    else:
        K = a_args[0].shape[1]
    NBk, KB = N // nb, K // kb
    ts = min(WGRAD_ROWS, S)
    nsteps = S // ts

    def body(*refs):
        acc_ref = refs[-1]
        o_ref = refs[-2]
        b_ref = refs[-3]
        si = pl.program_id(2)

        @pl.when(si == 0)
        def _():
            acc_ref[...] = jnp.zeros_like(acc_ref)

        if kind == "mod":
            x_ref, scl_ref, sh_ref = refs[:3]
            a = (x_ref[...] * (1.0 + scl_ref[...]) + sh_ref[...]).astype(BF16)
        elif kind == "swiglu":
            g = refs[0][...].astype(F32)
            u = refs[1][...].astype(F32)
            a = (g * _sigmoid(g) * u).astype(BF16)
        else:
            a = refs[0][...].astype(BF16)
        acc_ref[...] += lax.dot_general(a, b_ref[...].astype(BF16), TN, preferred_element_type=F32)

        @pl.when(si == nsteps - 1)
        def _():
            o_ref[...] = acc_ref[...].astype(BF16)

    a_tile = pl.BlockSpec((ts, kb), lambda j, k, s: (s, k))
    if kind == "mod":
        vec = pl.BlockSpec((1, kb), lambda j, k, s: (0, k))
        in_specs = [a_tile, vec, vec]
        args = list(a_args)
    elif kind == "swiglu":
        in_specs = [a_tile, pl.BlockSpec((ts, kb), lambda j, k, s: (s, k + KB))]
        args = [a_args[0], a_args[0]]
    else:
        in_specs = [a_tile]
        args = list(a_args)
    in_specs.append(pl.BlockSpec((ts, nb), lambda j, k, s: (s, j)))
    args.append(b)
    return pl.pallas_call(
        body, name=name, grid=(NBk, KB, nsteps), in_specs=in_specs,
        out_specs=pl.BlockSpec((None, kb, nb), lambda j, k, s: (j, k, 0)),
        out_shape=jax.ShapeDtypeStruct((NBk, K, nb), BF16),
        scratch_shapes=[pltpu.VMEM((kb, nb), F32)],
        compiler_params=_params(("parallel", "parallel", "arbitrary")),
    )(*args)


def _window_mask():
    t = lax.broadcasted_iota(jnp.int32, (GMLP_WINDOW, GMLP_WINDOW), 0)
    s = lax.broadcasted_iota(jnp.int32, (GMLP_WINDOW, GMLP_WINDOW), 1)
    return ((s // CHUNK) <= (t // CHUNK)).astype(F32)


def sgu_fwd(pre, glg, glb, ws, bst, name):
    S, H2 = pre.shape
    H = H2 // 2
    W, G = GMLP_WINDOW, GMLP_GROUPS
    gd = H // G
    tm = min(ROW_TILE, S)

    def body(pre_ref, glg_ref, glb_ref, ws_ref, bst_ref, q_ref):
        u = _gelu(pre_ref[:, :H])
        v = _gelu(pre_ref[:, H:])
        mu = jnp.mean(v, axis=-1, keepdims=True)
        vc = v - mu
        var = jnp.mean(vc * vc, axis=-1, keepdims=True)
        vn = ((vc * lax.rsqrt(var + LN_EPS)) * glg_ref[...] + glb_ref[...]).astype(BF16)
        mask = _window_mask()
        for g in range(G):
            wsg = (ws_ref[g] * mask).astype(BF16)
            bcol = bst_ref[:, g:g + 1]
            for wi in range(tm // W):
                rows = slice(wi * W, (wi + 1) * W)
                cols = slice(g * gd, (g + 1) * gd)
                s = jnp.dot(wsg, vn[rows, cols], preferred_element_type=F32) + bcol
                q_ref[rows, cols] = (u[rows, cols] * s).astype(BF16)

    return pl.pallas_call(
        body, name=name, grid=(S // tm,),
        in_specs=[pl.BlockSpec((tm, H2), _row(0)), pl.BlockSpec((1, H), _fixed2), pl.BlockSpec((1, H), _fixed2),
                  pl.BlockSpec((G, W, W), _fixed3), pl.BlockSpec((W, G), _fixed2)],
        out_specs=pl.BlockSpec((tm, H), _row(0)),
        out_shape=jax.ShapeDtypeStruct((S, H), BF16),
        compiler_params=_params(("parallel",)),
    )(pre, glg, glb, ws, bst)


def sgu_bwd(dq, pre, glg, glb, ws, bst, name):
    S, H2 = pre.shape
    H = H2 // 2
    W, G = GMLP_WINDOW, GMLP_GROUPS
    gd = H // G
    tm = min(ROW_TILE, S)

    def body(dq_ref, pre_ref, glg_ref, glb_ref, ws_ref, bst_ref,
             dpre_ref, dws_ref, dss_ref, dgl_ref, dbin_ref, du_s, dvn_s):
        @pl.when(pl.program_id(0) == 0)
        def _():
            dws_ref[...] = jnp.zeros_like(dws_ref)
            dss_ref[...] = jnp.zeros_like(dss_ref)
            dgl_ref[...] = jnp.zeros_like(dgl_ref)
            dbin_ref[...] = jnp.zeros_like(dbin_ref)

        pu = pre_ref[:, :H]
        pv = pre_ref[:, H:]
        u = _gelu(pu)
        v = _gelu(pv)
        mu = jnp.mean(v, axis=-1, keepdims=True)
        vc = v - mu
        var = jnp.mean(vc * vc, axis=-1, keepdims=True)
        rstd = lax.rsqrt(var + LN_EPS)
        vhat = vc * rstd
        vn = (vhat * glg_ref[...] + glb_ref[...]).astype(BF16)
        mask = _window_mask()
        for g in range(G):
            wsg = (ws_ref[g] * mask).astype(BF16)
            bcol = bst_ref[:, g:g + 1]
            cols = slice(g * gd, (g + 1) * gd)
            for wi in range(tm // W):
                rows = slice(wi * W, (wi + 1) * W)
                vblk = vn[rows, cols]
                s = jnp.dot(wsg, vblk, preferred_element_type=F32) + bcol
                dqb = dq_ref[rows, cols].astype(F32)
                du_s[rows, cols] = dqb * s
                ds = dqb * u[rows, cols]
                dss_ref[:, cols] += ds
                dsb = ds.astype(BF16)
                dvn_s[rows, cols] = lax.dot_general(wsg, dsb, TN, preferred_element_type=F32)
                dws_ref[g] += lax.dot_general(dsb, vblk, NT, preferred_element_type=F32) * mask
        dvn = dvn_s[...]
        dgl_ref[0:1, :] += jnp.sum(dvn * vhat, axis=0, keepdims=True)
        dgl_ref[1:2, :] += jnp.sum(dvn, axis=0, keepdims=True)
        dvh = dvn * glg_ref[...]
        m1 = jnp.mean(dvh, axis=-1, keepdims=True)
        m2 = jnp.mean(dvh * vhat, axis=-1, keepdims=True)
        dv = rstd * (dvh - m1 - vhat * m2)
        dpu = du_s[...] * _gelu_grad(pu)
        dpv = dv * _gelu_grad(pv)
        dbin_ref[0:1, :H] += jnp.sum(dpu, axis=0, keepdims=True)
        dbin_ref[0:1, H:] += jnp.sum(dpv, axis=0, keepdims=True)
        dpre_ref[:, :H] = dpu.astype(BF16)
        dpre_ref[:, H:] = dpv.astype(BF16)

    return pl.pallas_call(
        body, name=name, grid=(S // tm,),
        in_specs=[pl.BlockSpec((tm, H), _row(0)), pl.BlockSpec((tm, H2), _row(0)), pl.BlockSpec((1, H), _fixed2),
                  pl.BlockSpec((1, H), _fixed2), pl.BlockSpec((G, W, W), _fixed3), pl.BlockSpec((W, G), _fixed2)],
        out_specs=[pl.BlockSpec((tm, H2), _row(0)), pl.BlockSpec((G, W, W), _fixed3), pl.BlockSpec((W, H), _fixed2),
                   pl.BlockSpec((8, H), _fixed2), pl.BlockSpec((8, H2), _fixed2)],
        out_shape=[jax.ShapeDtypeStruct((S, H2), BF16), jax.ShapeDtypeStruct((G, W, W), F32),
                   jax.ShapeDtypeStruct((W, H), F32), jax.ShapeDtypeStruct((8, H), F32),
                   jax.ShapeDtypeStruct((8, H2), F32)],
        scratch_shapes=[pltpu.VMEM((tm, H), F32), pltpu.VMEM((tm, H), F32)],
        compiler_params=_params(("arbitrary",)),
    )(dq, pre, glg, glb, ws, bst)


def group_lane_sum(dss, name):
    W, H = dss.shape
    gd = H // GMLP_GROUPS

    def body(d_ref, o_ref):
        j = lax.broadcasted_iota(jnp.int32, (H, LANES), 0)
        g = lax.broadcasted_iota(jnp.int32, (H, LANES), 1)
        ind = ((j // gd) == g).astype(F32)
        o_ref[...] = jnp.dot(d_ref[...], ind, preferred_element_type=F32, precision=HIGHEST)

    return pl.pallas_call(
        body, name=name, in_specs=[VMEM_SPEC], out_specs=VMEM_SPEC,
        out_shape=jax.ShapeDtypeStruct((W, LANES), F32), compiler_params=_params(),
    )(dss)


def _attn_scores(q2, k2, bias, sel, valid, scale):
    qm = jnp.where(sel, q2, jnp.zeros_like(q2))
    s = lax.dot_general(qm, k2, NT, preferred_element_type=F32) * scale + bias
    s = jnp.where(valid, s, -jnp.inf)
    m = jnp.max(s, axis=-1, keepdims=True)
    e = jnp.exp(s - m)
    p = e / jnp.sum(e, axis=-1, keepdims=True)
    return qm, p


def attn_fwd(q, kpad, vpad, bias, name):
    S, D = q.shape
    HP = D // LANES
    cps = min(ATTN_CHUNKS_PER_STEP, S // CHUNK)
    tq = cps * CHUNK
    scale = HEAD_DIM ** -0.5

    def body(q_ref, k_ref, v_ref, b_ref, o_ref):
        j = pl.program_id(1)
        sel0 = lax.broadcasted_iota(jnp.int32, (CHUNK, LANES), 1) < HEAD_DIM
        r = lax.broadcasted_iota(jnp.int32, (CHUNK, BAND), 1)
        for cc in range(cps):
            start = pl.multiple_of((j * cps + cc) * CHUNK, CHUNK)
            rows = slice(cc * CHUNK, (cc + 1) * CHUNK)
            q2 = q_ref[rows, :]
            k2 = k_ref[pl.ds(start, BAND), :]
            v2 = v_ref[pl.ds(start, BAND), :]
            valid = (r + start) >= LEFT_PAD
            outs = []
            for sub in range(2):
                sel = sel0 if sub == 0 else jnp.logical_not(sel0)
                _, p = _attn_scores(q2, k2, b_ref[sub], sel, valid, scale)
                outs.append(jnp.dot(p.astype(BF16), v2, preferred_element_type=F32))
            o_ref[rows, :] = jnp.where(sel0, outs[0], outs[1]).astype(BF16)

    kv_spec = pl.BlockSpec((S + LEFT_PAD, LANES), lambda h, j: (0, h))
    return pl.pallas_call(
        body, name=name, grid=(HP, S // tq),
        in_specs=[pl.BlockSpec((tq, LANES), lambda h, j: (j, h)), kv_spec, kv_spec,
                  pl.BlockSpec((2, CHUNK, BAND), lambda h, j: (h, 0, 0))],
        out_specs=pl.BlockSpec((tq, LANES), lambda h, j: (j, h)),
        out_shape=jax.ShapeDtypeStruct((S, D), BF16),
        compiler_params=_params(("parallel", "parallel")),
    )(q, kpad, vpad, bias)


def attn_bwd(q, do, kpad, vpad, bias, dk_in, dv_in, name):
    S, D = q.shape
    HP = D // LANES
    NH = 2 * HP
    cps = min(ATTN_CHUNKS_PER_STEP, S // CHUNK)
    tq = cps * CHUNK
    nj = S // tq
    scale = HEAD_DIM ** -0.5

    def body(q_ref, do_ref, k_ref, v_ref, b_ref, dki_ref, dvi_ref, dq_ref, dk_ref, dv_ref, db_ref, dk_acc, dv_acc):
        j = pl.program_id(1)

        @pl.when(j == 0)
        def _():
            dk_acc[:LEFT_PAD, :] = jnp.zeros((LEFT_PAD, LANES), F32)
            dv_acc[:LEFT_PAD, :] = jnp.zeros((LEFT_PAD, LANES), F32)
            dk_acc[LEFT_PAD:, :] = dki_ref[...]
            dv_acc[LEFT_PAD:, :] = dvi_ref[...]
            db_ref[...] = jnp.zeros_like(db_ref)

        sel0 = lax.broadcasted_iota(jnp.int32, (CHUNK, LANES), 1) < HEAD_DIM
        r = lax.broadcasted_iota(jnp.int32, (CHUNK, BAND), 1)
        for cc in range(cps):
            start = pl.multiple_of((j * cps + cc) * CHUNK, CHUNK)
            rows = slice(cc * CHUNK, (cc + 1) * CHUNK)
            q2 = q_ref[rows, :]
            do2 = do_ref[rows, :]
            k2 = k_ref[pl.ds(start, BAND), :]
            v2 = v_ref[pl.ds(start, BAND), :]
            valid = (r + start) >= LEFT_PAD
            dqs = []
            for sub in range(2):
                sel = sel0 if sub == 0 else jnp.logical_not(sel0)
                qm, p = _attn_scores(q2, k2, b_ref[sub], sel, valid, scale)
                dom = jnp.where(sel, do2, jnp.zeros_like(do2))
                dp = lax.dot_general(dom, v2, NT, preferred_element_type=F32)
                delta = jnp.sum(dp * p, axis=-1, keepdims=True)
                ds = p * (dp - delta)
                db_ref[sub] += ds
                dsb = ds.astype(BF16)
                dqs.append(jnp.dot(dsb, k2, preferred_element_type=F32) * scale)
                dk_acc[pl.ds(start, BAND), :] += lax.dot_general(dsb, qm, TN, preferred_element_type=F32) * scale
                dv_acc[pl.ds(start, BAND), :] += lax.dot_general(p.astype(BF16), dom, TN, preferred_element_type=F32)
            dq_ref[rows, :] = jnp.where(sel0, dqs[0], dqs[1]).astype(BF16)

        @pl.when(j == nj - 1)
        def _():
            dk_ref[...] = dk_acc[LEFT_PAD:, :]
            dv_ref[...] = dv_acc[LEFT_PAD:, :]

    q_spec = pl.BlockSpec((tq, LANES), lambda h, j: (j, h))
    kv_spec = pl.BlockSpec((S + LEFT_PAD, LANES), lambda h, j: (0, h))
    col_spec = pl.BlockSpec((S, LANES), lambda h, j: (0, h))
    b_spec = pl.BlockSpec((2, CHUNK, BAND), lambda h, j: (h, 0, 0))
    return pl.pallas_call(
        body, name=name, grid=(HP, nj),
        in_specs=[q_spec, q_spec, kv_spec, kv_spec, b_spec, col_spec, col_spec],
        out_specs=[q_spec, col_spec, col_spec, b_spec],
        out_shape=[jax.ShapeDtypeStruct((S, D), BF16), jax.ShapeDtypeStruct((S, D), F32),
                   jax.ShapeDtypeStruct((S, D), F32), jax.ShapeDtypeStruct((NH, CHUNK, BAND), F32)],
        scratch_shapes=[pltpu.VMEM((S + LEFT_PAD, LANES), F32), pltpu.VMEM((S + LEFT_PAD, LANES), F32)],
        compiler_params=_params(("parallel", "arbitrary")),
    )(q, do, kpad, vpad, bias, dk_in, dv_in)


def _rel_onehot(t):
    r = lax.broadcasted_iota(jnp.int32, (BAND, N_REL), 0)
    i = lax.broadcasted_iota(jnp.int32, (BAND, N_REL), 1)
    idx = jnp.clip(t + LEFT_PAD - r, -(CHUNK - 1), MAX_REL) + (CHUNK - 1)
    return (idx == i).astype(F32)


def bias_expand(rb, name):
    NH = rb.shape[0]

    def body(rb_ref, o_ref):
        def step(t, carry):
            o_ref[t] = lax.dot_general(rb_ref[...], _rel_onehot(t), NT, preferred_element_type=F32,
                                       precision=HIGHEST)
            return carry

        lax.fori_loop(0, CHUNK, step, 0)

    return pl.pallas_call(
        body, name=name, in_specs=[VMEM_SPEC], out_specs=VMEM_SPEC,
        out_shape=jax.ShapeDtypeStruct((CHUNK, NH, BAND), F32), compiler_params=_params(),
    )(rb)


def bias_grad(dsum, name):
    NH = dsum.shape[1]

    def body(d_ref, o_ref):
        def step(t, acc):
            return acc + jnp.dot(d_ref[t], _rel_onehot(t), preferred_element_type=F32, precision=HIGHEST)

        o_ref[...] = lax.fori_loop(0, CHUNK, step, jnp.zeros((NH, N_REL), F32))

    return pl.pallas_call(
        body, name=name, in_specs=[VMEM_SPEC], out_specs=VMEM_SPEC,
        out_shape=jax.ShapeDtypeStruct((NH, N_REL), F32), compiler_params=_params(),
    )(dsum)


def loss_grad(y, tgt, name):
    S, D = y.shape
    tm = min(ROW_TILE, S)

    def body(y_ref, t_ref, d_ref, acc_ref):
        @pl.when(pl.program_id(0) == 0)
        def _():
            acc_ref[...] = jnp.zeros_like(acc_ref)

        err = y_ref[...] - t_ref[...]
        d_ref[...] = err * (1.0 / D)
        acc_ref[0:1, :] += jnp.sum(err * err, axis=0, keepdims=True)

    tile = pl.BlockSpec((tm, D), _row(0))
    return pl.pallas_call(
        body, name=name, grid=(S // tm,), in_specs=[tile, tile],
        out_specs=[tile, pl.BlockSpec((8, D), _fixed2)],
        out_shape=[jax.ShapeDtypeStruct((S, D), F32), jax.ShapeDtypeStruct((8, D), F32)],
        compiler_params=_params(("arbitrary",)),
    )(y, tgt)


def _col_tile(n):
    for t in (768, 512, 256, 128):
        if n % t == 0:
            return t
    return n


def ada_fwd(c_all, w, b, name):
    L, D, n = w.shape
    tn = _col_tile(n)

    def body(c_ref, w_ref, b_ref, o_ref):
        cv = c_ref[...]
        ca = cv * _sigmoid(cv)
        o_ref[...] = jnp.dot(ca, w_ref[...], preferred_element_type=F32, precision=HIGHEST) + b_ref[...]

    return pl.pallas_call(
        body, name=name, grid=(L, n // tn),
        in_specs=[pl.BlockSpec((N_DEV, D), lambda l, j: (0, 0)), pl.BlockSpec((None, D, tn), lambda l, j: (l, 0, j)),
                  pl.BlockSpec((None, 1, tn), lambda l, j: (l, 0, j))],
        out_specs=pl.BlockSpec((None, N_DEV, tn), lambda l, j: (l, 0, j)),
        out_shape=jax.ShapeDtypeStruct((L, N_DEV, n), F32),
        compiler_params=_params(("parallel", "parallel")),
    )(c_all, w, b)


def ada_wgrad(c_all_t, dmod, name):
    L, _, n = dmod.shape
    D = c_all_t.shape[0]
    tn = _col_tile(n)

    def body(c_ref, d_ref, o_ref):
        cv = c_ref[...]
        ca = cv * _sigmoid(cv)
        o_ref[...] = jnp.dot(ca, d_ref[...], preferred_element_type=F32, precision=HIGHEST)

    return pl.pallas_call(
        body, name=name, grid=(L, n // tn),
        in_specs=[pl.BlockSpec((D, N_DEV), lambda l, j: (0, 0)), pl.BlockSpec((None, N_DEV, tn), lambda l, j: (l, 0, j))],
        out_specs=pl.BlockSpec((None, D, tn), lambda l, j: (l, 0, j)),
        out_shape=jax.ShapeDtypeStruct((L, D, n), F32),
        compiler_params=_params(("parallel", "parallel")),
    )(c_all_t, dmod)


ELEMENTWISE_BLOCK_BYTES = 2 * 1024 * 1024


def _elementwise_rows(rows, row_bytes):
    for t in (1024, 512, 256, 128, 64, 32, 16):
        if rows % t == 0 and t * row_bytes <= ELEMENTWISE_BLOCK_BYTES:
            return t
    return rows


def sum_leading(a, name, first=None):
    n, M, N = a.shape
    tr = _elementwise_rows(M, n * N * 4)

    def body(*refs):
        a_ref, o_ref = refs[0], refs[-1]
        if first is None:
            acc = a_ref[0].astype(F32)
        else:
            acc = refs[1][...].astype(F32) + a_ref[0].astype(F32)
        for i in range(1, n):
            acc = acc + a_ref[i].astype(F32)
        o_ref[...] = acc

    in_specs = [pl.BlockSpec((n, tr, N), lambda i: (0, i, 0))]
    args = [a]
    if first is not None:
        in_specs.append(pl.BlockSpec((tr, N), _row(0)))
        args.append(first)
    return pl.pallas_call(
        body, name=name, grid=(M // tr,), in_specs=in_specs,
        out_specs=pl.BlockSpec((tr, N), _row(0)),
        out_shape=jax.ShapeDtypeStruct((M, N), F32),
        compiler_params=_params(("parallel",)),
    )(*args)


def add_bf16(a, b, name):
    n, M, N = a.shape
    tr = _elementwise_rows(M, N * 4)

    def body(a_ref, b_ref, o_ref):
        o_ref[...] = (a_ref[...].astype(F32) + b_ref[...].astype(F32)).astype(BF16)

    spec = pl.BlockSpec((None, tr, N), lambda k, i: (k, i, 0))
    return pl.pallas_call(
        body, name=name, grid=(n, M // tr), in_specs=[spec, spec], out_specs=spec,
        out_shape=jax.ShapeDtypeStruct((n, M, N), BF16),
        compiler_params=_params(("parallel", "parallel")),
    )(a, b)


def adamw(w, g, m, v, name):
    M, N = w.shape
    tr = _elementwise_rows(M, N * 4)
    c1 = 1.0 - ADAM_B1 ** ADAM_STEP
    c2 = 1.0 - ADAM_B2 ** ADAM_STEP

    def body(w_ref, g_ref, m_ref, v_ref, d_ref, nm_ref, nv_ref):
        g = g_ref[...]
        nm = ADAM_B1 * m_ref[...] + (1.0 - ADAM_B1) * g
        nv = ADAM_B2 * v_ref[...] + (1.0 - ADAM_B2) * (g * g)
        d_ref[...] = -ADAM_LR * ((nm / c1) / (jnp.sqrt(nv / c2) + ADAM_EPS) + ADAM_WD * w_ref[...])
        nm_ref[...] = nm
        nv_ref[...] = nv

    spec = pl.BlockSpec((tr, N), _row(0))
    shp = jax.ShapeDtypeStruct((M, N), F32)
    return pl.pallas_call(
        body, name=name, grid=(M // tr,), in_specs=[spec] * 4, out_specs=[spec] * 3, out_shape=[shp] * 3,
        compiler_params=_params(("parallel",)),
    )(w, g, m, v)


def _coords():
    return lax.axis_index("x"), lax.axis_index("y"), lax.axis_index("c")


def all_gather8(block, name):
    m_per, n = block.shape

    def body(x_ref, out_ref, send_sems, recv_sems, local_sem):
        x, y, c = _coords()
        me, sibling = (x, y, c), (x, y, 1 - c)
        chips = [(1 - x, y), (x, 1 - y), (1 - x, 1 - y)]

        def rows(px, py, pc):
            return out_ref.at[pl.ds((4 * px + 2 * py + pc) * m_per, m_per), :]

        def copy(k, blk, to, src=None):
            return pltpu.make_async_remote_copy(
                src_ref=rows(*blk) if src is None else src, dst_ref=rows(*blk),
                send_sem=send_sems.at[k], recv_sem=recv_sems.at[k], device_id=to, device_id_type=MESH)

        mine = pltpu.make_async_copy(x_ref, rows(*me), local_sem)
        mine.start()
        first = [copy(0, me, sibling, src=x_ref)]
        first += [copy(1 + j, me, (*chip, c), src=x_ref) for j, chip in enumerate(chips)]
        for cp in first:
            cp.start()
        passed = [copy(4 + j, (*chip, c), sibling) for j, chip in enumerate(chips)]
        for j, chip in enumerate(chips):
            copy(1 + j, (*chip, c), me).wait_recv()
            passed[j].start()
        copy(0, sibling, me).wait_recv()
        for j, chip in enumerate(chips):
            copy(4 + j, (*chip, 1 - c), me).wait_recv()
        for cp in first + passed:
            cp.wait_send()
        mine.wait()

    return pl.pallas_call(
        body, name=name, in_specs=[VMEM_SPEC], out_specs=VMEM_SPEC,
        out_shape=jax.ShapeDtypeStruct((N_DEV * m_per, n), block.dtype),
        scratch_shapes=[pltpu.SemaphoreType.DMA((7,)), pltpu.SemaphoreType.DMA((7,)), pltpu.SemaphoreType.DMA],
        compiler_params=_params(),
    )(block)


def _other_chips(x, y):
    return [(1 - x, y), (x, 1 - y), (1 - x, 1 - y)]


HBM_SPEC = pl.BlockSpec(memory_space=pltpu.HBM)
SEM_SPEC = pl.BlockSpec(memory_space=pltpu.SEMAPHORE)
DATAFLOW = pltpu.SideEffectType.DATAFLOW_SIDE_EFFECTING


def _weight_desc(stack_ref, k, land_ref, px, py, c, me_s):
    h = land_ref.shape[1] // 2
    rows = pl.ds(c * h, h)
    return stack_ref.at[k, rows, :], land_ref.at[me_s, rows, :], land_ref.at[2 * px + py, rows, :]


def _grad_desc(psum_ref, k, land_ref, px, py, c, me_s):
    return psum_ref.at[2 * px + py], land_ref.at[me_s], land_ref.at[2 * px + py]


def exchange_start(srcs, lands, units, groups, desc, name):
    n_s, n_l, n_g = len(srcs), len(lands), len(groups)

    def body(*refs):
        s_refs, l_refs = refs[:n_s], refs[n_s:n_s + n_l]
        outs = refs[n_s + n_l:]
        sems, token = outs[:2 * n_g], outs[-1]
        x, y, c = _coords()
        me_s = 2 * x + y
        for g, ids in enumerate(groups):
            for i, u in enumerate(ids):
                si, k = units[u]
                for j, (px, py) in enumerate(_other_chips(x, y)):
                    src, dst, _ = desc(s_refs[si], k, l_refs[u], px, py, c, me_s)
                    pltpu.make_async_remote_copy(
                        src_ref=src, dst_ref=dst, send_sem=sems[2 * g].at[3 * i + j],
                        recv_sem=sems[2 * g + 1].at[3 * i + j], device_id=(px, py, c), device_id_type=MESH).start()
        token[...] = jnp.zeros_like(token)

    arrs = list(srcs) + list(lands)
    sem_shapes = [pltpu.SemaphoreType.DMA((3 * len(ids),)) for ids in groups for _ in range(2)]
    outs = pl.pallas_call(
        body, name=name,
        in_specs=[HBM_SPEC] * len(arrs),
        out_specs=[SEM_SPEC] * (2 * n_g) + [HBM_SPEC] * len(arrs) + [VMEM_SPEC],
        out_shape=sem_shapes + [pltpu.HBM(a.shape, a.dtype) for a in arrs] + [jax.ShapeDtypeStruct((8, LANES), F32)],
        input_output_aliases={i: 2 * n_g + i for i in range(len(arrs))},
        compiler_params=pltpu.CompilerParams(has_side_effects=DATAFLOW),
    )(*[pltpu.with_memory_space_constraint(a, pltpu.HBM) for a in arrs])
    sems = outs[:2 * n_g]
    thru = outs[2 * n_g:2 * n_g + len(arrs)]
    return sems, list(thru[:n_s]), list(thru[n_s:]), outs[-1]


def exchange_wait(srcs, lands, units, send_sem, recv_sem, desc, after, name):
    n_s, n_l = len(srcs), len(lands)

    def body(*refs):
        s_refs, l_refs = refs[:n_s], refs[n_s:n_s + n_l]
        send_sems, recv_sems = refs[n_s + n_l], refs[n_s + n_l + 1]
        x, y, c = _coords()
        me_s = 2 * x + y
        for i, (si, k) in enumerate(units):
            for j, (px, py) in enumerate(_other_chips(x, y)):
                src, _, mine = desc(s_refs[si], k, l_refs[i], px, py, c, me_s)
                cp = pltpu.make_async_remote_copy(
                    src_ref=src, dst_ref=mine, send_sem=send_sems.at[3 * i + j], recv_sem=recv_sems.at[3 * i + j],
                    device_id=(px, py, c), device_id_type=MESH)
                cp.wait_send()
                cp.wait_recv()

    arrs = list(srcs) + list(lands)
    outs = pl.pallas_call(
        body, name=name,
        in_specs=[HBM_SPEC] * len(arrs) + [SEM_SPEC, SEM_SPEC, ANY],
        out_specs=[HBM_SPEC] * len(arrs),
        out_shape=[pltpu.HBM(a.shape, a.dtype) for a in arrs],
        input_output_aliases={i: i for i in range(len(arrs))},
        compiler_params=pltpu.CompilerParams(has_side_effects=DATAFLOW),
    )(*arrs, send_sem, recv_sem, after)
    return list(outs[:n_s]), list(outs[n_s:])


def sibling_fill(lands, name):
    n_u = len(lands)

    def body(*refs):
        ins, outs = refs[:n_u], refs[n_u:2 * n_u]
        send_sems, recv_sems = refs[2 * n_u:]
        x, y, c = _coords()
        sends = []
        for u in range(n_u):
            h = ins[u].shape[1] // 2
            for j, (px, py) in enumerate(_other_chips(x, y)):
                part = (2 * px + py, pl.ds(c * h, h), slice(None))
                cp = pltpu.make_async_remote_copy(
                    src_ref=ins[u].at[part], dst_ref=outs[u].at[part], send_sem=send_sems.at[3 * u + j],
                    recv_sem=recv_sems.at[3 * u + j], device_id=(x, y, 1 - c), device_id_type=MESH)
                cp.start()
                sends.append(cp)
        for u in range(n_u):
            h = ins[u].shape[1] // 2
            for j, (px, py) in enumerate(_other_chips(x, y)):
                theirs = (2 * px + py, pl.ds((1 - c) * h, h), slice(None))
                pltpu.make_async_remote_copy(
                    src_ref=ins[u].at[theirs], dst_ref=outs[u].at[theirs], send_sem=send_sems.at[3 * u + j],
                    recv_sem=recv_sems.at[3 * u + j], device_id=(x, y, 1 - c), device_id_type=MESH).wait_recv()
        for cp in sends:
            cp.wait_send()

    return pl.pallas_call(
        body, name=name, in_specs=[ANY] * n_u, out_specs=[ANY] * n_u,
        out_shape=[jax.ShapeDtypeStruct(a.shape, a.dtype) for a in lands],
        input_output_aliases={i: i for i in range(n_u)},
        scratch_shapes=[pltpu.SemaphoreType.DMA((3 * n_u,)), pltpu.SemaphoreType.DMA((3 * n_u,))],
        compiler_params=_params(),
    )(*lands)


def grad_pair_exchange(grads, name):
    n_u = len(grads)

    def body(*refs):
        ins, outs = refs[:n_u], refs[n_u:2 * n_u]
        send_sems, recv_sems = refs[2 * n_u:]
        x, y, c = _coords()
        cps = []
        for u in range(n_u):
            half = ins[u].shape[1] // 2
            cp = pltpu.make_async_remote_copy(
                src_ref=ins[u].at[:, pl.ds((1 - c) * half, half), :], dst_ref=outs[u],
                send_sem=send_sems.at[u], recv_sem=recv_sems.at[u], device_id=(x, y, 1 - c), device_id_type=MESH)
            cp.start()
            cps.append(cp)
        for cp in cps:
            cp.wait()

    out_shape = [jax.ShapeDtypeStruct((g.shape[0], g.shape[1] // 2, g.shape[2]), g.dtype) for g in grads]
    return pl.pallas_call(
        body, name=name, in_specs=[ANY] * n_u, out_specs=[ANY] * n_u, out_shape=out_shape,
        scratch_shapes=[pltpu.SemaphoreType.DMA((n_u,)), pltpu.SemaphoreType.DMA((n_u,))],
        compiler_params=_params(),
    )(*grads)


def grad_half_swap(halves, name):
    n_u = len(halves)

    def body(*refs):
        ins, outs = refs[:n_u], refs[n_u:2 * n_u]
        send_sems, recv_sems = refs[2 * n_u:]
        x, y, c = _coords()
        cps = []
        for u in range(n_u):
            cp = pltpu.make_async_remote_copy(
                src_ref=ins[u], dst_ref=outs[u], send_sem=send_sems.at[u], recv_sem=recv_sems.at[u],
                device_id=(x, y, 1 - c), device_id_type=MESH)
            cp.start()
            cps.append(cp)
        for cp in cps:
            cp.wait()

    return pl.pallas_call(
        body, name=name, in_specs=[ANY] * n_u, out_specs=[ANY] * n_u,
        out_shape=[jax.ShapeDtypeStruct(h.shape, h.dtype) for h in halves],
        scratch_shapes=[pltpu.SemaphoreType.DMA((n_u,)), pltpu.SemaphoreType.DMA((n_u,))],
        compiler_params=_params(),
    )(*halves)


def _pack_rows(parts):
    flat = jnp.concatenate([p.reshape(-1).astype(F32) for p in parts])
    n = flat.shape[0]
    padded = -(-n // (8 * LANES)) * (8 * LANES)
    return jnp.pad(flat, (0, padded - n)).reshape(-1, LANES)


def _unpack_rows(packed, shapes):
    flat = packed.reshape(-1)
    out, off = [], 0
    for s in shapes:
        size = 1
        for d in s:
            size *= d
        out.append(flat[off:off + size].reshape(s))
        off += size
    return out


def _shard_last(full, s_me):
    n = full.shape[-1] // N_CHIP
    return lax.dynamic_slice_in_dim(full, s_me * n, n, axis=full.ndim - 1)


def _unshard_last(g):
    moved = jnp.moveaxis(g, 0, -2)
    return moved.reshape(moved.shape[:-2] + (moved.shape[-2] * moved.shape[-1],))


def kernel(x, c, w_ada, b_ada, ln_g, ln_b, ffn_gu, ffn_down, gmlp_w_in, gmlp_b_in, gmlp_ln_g, gmlp_ln_b, gmlp_w_s, gmlp_b_s, gmlp_w_out, w_ada_kv, b_ada_kv, w_kv, attn_w_q, attn_rel_bias, attn_w_o, loss_target, m_w_ada, m_b_ada, m_ln_g, m_ln_b, m_ffn_gu, m_ffn_down, m_gmlp_w_in, m_gmlp_b_in, m_gmlp_ln_g, m_gmlp_ln_b, m_gmlp_w_s, m_gmlp_b_s, m_gmlp_w_out, m_w_ada_kv, m_b_ada_kv, m_w_kv, m_attn_w_q, m_attn_rel_bias, m_attn_w_o, v_w_ada, v_b_ada, v_ln_g, v_ln_b, v_ffn_gu, v_ffn_down, v_gmlp_w_in, v_gmlp_b_in, v_gmlp_ln_g, v_gmlp_ln_b, v_gmlp_w_s, v_gmlp_b_s, v_gmlp_w_out, v_w_ada_kv, v_b_ada_kv, v_w_kv, v_attn_w_q, v_attn_rel_bias, v_attn_w_o):
    xi, yi, ci = _coords()
    s_me = 2 * xi + yi
    dev = 4 * xi + 2 * yi + ci

    x0 = x[0]
    tgt = loss_target[0]
    S, D = x0.shape
    L = w_ada.shape[0]
    NA = gmlp_w_in.shape[0]
    NB = attn_w_q.shape[0]
    NH = D // HEAD_DIM
    alpha = (2.0 * L) ** 0.25
    n_ada = w_ada.shape[2]
    n_kv = w_ada_kv.shape[1]

    stack_names = ["ffn_gu", "ffn_down", "gmlp_w_in", "gmlp_w_out", "w_kv", "attn_w_q", "attn_w_o"]
    stack_src = dict(ffn_gu=ffn_gu, ffn_down=ffn_down, gmlp_w_in=gmlp_w_in, gmlp_w_out=gmlp_w_out, w_kv=w_kv[None],
                     attn_w_q=attn_w_q, attn_w_o=attn_w_o)
    stacks = [stack_src[nm].astype(BF16).reshape((-1,) + stack_src[nm].shape[-2:]) for nm in stack_names]
    units = [(si, k) for si, st in enumerate(stacks) for k in range(st.shape[0])]
    unit_of = {(stack_names[si], k): u for u, (si, k) in enumerate(units)}
    layer_units = []
    for l in range(L):
        names = [("w_kv", 0)] if l == NA else []
        names += [("ffn_gu", 2 * l), ("ffn_down", 2 * l)]
        names += [("gmlp_w_in", l), ("gmlp_w_out", l)] if l < NA else [("attn_w_q", l - NA), ("attn_w_o", l - NA)]
        names += [("ffn_gu", 2 * l + 1), ("ffn_down", 2 * l + 1)]
        layer_units.append([unit_of[n] for n in names])
    lands0 = [lax.dynamic_update_slice(jnp.zeros((N_CHIP,) + stacks[si].shape[1:], BF16), stacks[si][k][None],
                                       (s_me, 0, 0)) for si, k in units]
    w_sems, stacks_t, lands_t, w_token = exchange_start(stacks, lands0, units, layer_units, _weight_desc,
                                                        "weight_send_start")
    c = c + w_token[0, 0]
    wg = {}

    def fetch_weights(l, stacks_now, after):
        ids = layer_units[l]
        stacks_next, got = exchange_wait(stacks_now, [lands_t[u] for u in ids], [units[u] for u in ids],
                                         w_sems[2 * l], w_sems[2 * l + 1], _weight_desc, after,
                                         "weight_send_wait_%d" % l)
        for u, a in zip(ids, sibling_fill(got, "weight_sibling_fill")):
            wg[u] = a
        return stacks_next

    c_all = all_gather8(jnp.broadcast_to(c, (8, D)), "ag_c").reshape(N_DEV, 8, D)[:, 0]
    b_ada_sh = lax.dynamic_slice_in_dim(b_ada, s_me * n_ada, n_ada, axis=1)
    b_kv_sh = lax.dynamic_slice_in_dim(b_ada_kv, s_me * n_kv, n_kv, axis=0)
    mod_part = ada_fwd(c_all, w_ada, b_ada_sh[:, None, :], "ada_fwd")
    mkv_part = ada_fwd(c_all, w_ada_kv[None], b_kv_sh[None, None, :], "ada_kv_fwd")
    part = jnp.concatenate([jnp.transpose(mod_part, (1, 0, 2)).reshape(N_DEV, L * n_ada), mkv_part[0]], axis=1)
    width = part.shape[1]
    pad_w = -(-width // LANES) * LANES - width
    all_part = all_gather8(jnp.pad(part, ((0, 0), (0, pad_w))), "ag_mod").reshape(N_DEV, N_DEV, width + pad_w)
    mine = lax.dynamic_index_in_dim(all_part[0::2], dev, axis=1, keepdims=False)
    mod = jnp.transpose(mine[:, :L * n_ada].reshape(N_CHIP, L, n_ada), (1, 0, 2)).reshape(L, N_MOD, D)
    mkv = mine[:, L * n_ada:width].reshape(2, D)

    def mrow(l, k):
        return mod[l, k][None, :]

    small_shapes = [ln_g.shape, ln_b.shape, gmlp_b_in.shape, gmlp_ln_g.shape, gmlp_ln_b.shape, attn_rel_bias.shape]
    small_pack = _pack_rows([ln_g, ln_b, gmlp_b_in, gmlp_ln_g, gmlp_ln_b, attn_rel_bias])
    small_all = all_gather8(small_pack, "ag_small_params").reshape((N_DEV,) + small_pack.shape)[0::2]
    sm = [_unpack_rows(small_all[s], small_shapes) for s in range(N_CHIP)]
    ln_g_f, ln_b_f, b_in_f, gln_g_f, gln_b_f, rel_f = [
        _unshard_last(jnp.stack([sm[s][i] for s in range(N_CHIP)])) for i in range(len(small_shapes))]

    def W(nm, k):
        return wg[unit_of[(nm, k)]]

    def Wrows(nm, k):
        w4 = W(nm, k)
        return w4.reshape(w4.shape[0] * w4.shape[1], w4.shape[2])

    bst = [jnp.transpose(gmlp_b_s[j]) for j in range(NA)]
    biases = [jnp.transpose(bias_expand(rel_f[j], "bias_expand"), (1, 0, 2)) for j in range(NB)]

    saved = []
    xc = x0
    kpad = vpad = xkv = None
    for l in range(L):
        stacks_t = fetch_weights(l, stacks_t, mod if l == 0 else xc)
        if l == NA:
            xkv = xc
            kv = mod_matmul(xc, mkv[1][None], mkv[0][None], W("w_kv", 0), None, BF16, "kv_proj")
            kpad = jnp.pad(kv[:, :D], ((LEFT_PAD, 0), (0, 0)))
            vpad = jnp.pad(kv[:, D:], ((LEFT_PAD, 0), (0, 0)))
        sv = {}
        for i in (0, 2):
            k = 2 * l + i // 2
            gu = mod_matmul(xc, mrow(l, 3 * i + 1), mrow(l, 3 * i), W("ffn_gu", k), None, BF16, "ffn_up")
            gw = 0.5 * (1.0 + mrow(l, 3 * i + 2))
            xn, xh, rs, yv = matmul_res_ln(gu, Wrows("ffn_down", k), xc, gw, ln_g_f[l, i][None], ln_b_f[l, i][None],
                                           alpha, True, "ffn_down")
            sv[i] = dict(x=xc, gu=gu, xh=xh, rs=rs, y=yv, gw=gw)
            xc = xn
            if i == 0:
                gw = 1.0 + mrow(l, 5)
                if l < NA:
                    pre = mod_matmul(xc, mrow(l, 4), mrow(l, 3), W("gmlp_w_in", l), b_in_f[l][None], F32, "gmlp_in")
                    qv = sgu_fwd(pre, gln_g_f[l][None], gln_b_f[l][None], gmlp_w_s[l], bst[l], "sgu_fwd")
                    xn, xh, rs, yv = matmul_res_ln(qv, Wrows("gmlp_w_out", l), xc, gw, ln_g_f[l, 1][None],
                                                   ln_b_f[l, 1][None], alpha, False, "gmlp_out")
                    sv[1] = dict(x=xc, pre=pre, a=qv, xh=xh, rs=rs, y=yv, gw=gw)
                else:
                    j = l - NA
                    qh = mod_matmul(xc, mrow(l, 4), mrow(l, 3), Wrows("attn_w_q", j)[None], None, BF16, "attn_q")
                    ov = attn_fwd(qh, kpad, vpad, biases[j], "attn_fwd")
                    xn, xh, rs, yv = matmul_res_ln(ov, Wrows("attn_w_o", j), xc, gw, ln_g_f[l, 1][None],
                                                   ln_b_f[l, 1][None], alpha, False, "attn_out")
                    sv[1] = dict(x=xc, q=qh, a=ov, xh=xh, rs=rs, y=yv, gw=gw)
                xc = xn
        saved.append(sv)

    dx, lacc = loss_grad(xc, tgt, "loss_grad")
    loss = lax.psum((0.5 / D) * jnp.sum(lacc[0]), ("x", "y", "c"))

    gfull = [None] * len(units)
    dmod = [[None] * N_MOD for _ in range(L)]
    d_ln_g = [[None] * 3 for _ in range(L)]
    d_ln_b = [[None] * 3 for _ in range(L)]
    d_b_in, d_gln_g, d_gln_b, d_ws, d_bs, d_rel = ([None] * NA, [None] * NA, [None] * NA, [None] * NA, [None] * NA,
                                                  [None] * NB)
    dk = jnp.zeros((S, D), F32)
    dv = jnp.zeros((S, D), F32)
    dmkv = None

    made = []

    def put(nm, k, g):
        gfull[unit_of[(nm, k)]] = g.reshape((N_CHIP, -1, g.shape[-1]))
        made.append(unit_of[(nm, k)])

    own_half, sib_half = {}, {}

    def start_grad_exchange(ids, tag):
        from_sibling = grad_pair_exchange([gfull[u] for u in ids], "grad_pair_exchange")
        psums = []
        for u, fs in zip(ids, from_sibling):
            half = gfull[u].shape[1] // 2
            mine = lax.dynamic_slice_in_dim(gfull[u], ci * half, half, axis=1)
            psums.append(add_bf16(mine, fs, "grad_pair_sum"))
        n = len(ids)
        sems, ps_t, q_t, token = exchange_start(psums, [jnp.zeros_like(p) for p in psums], [(i, 0) for i in range(n)],
                                                [list(range(n))], _grad_desc, "grad_send_start_%d" % tag)
        return dict(ids=ids, tag=tag, sems=sems, ps=ps_t, q=q_t), token

    def finish_grad_exchange(pend, after):
        n = len(pend["ids"])
        ps_t, q = exchange_wait(pend["ps"], pend["q"], [(i, 0) for i in range(n)], pend["sems"][0], pend["sems"][1],
                                _grad_desc, after, "grad_send_wait_%d" % pend["tag"])
        halves = [sum_leading(q[i], "grad_chip_sum", lax.dynamic_index_in_dim(ps_t[i], s_me, 0, keepdims=False))
                  for i in range(n)]
        for u, h, s in zip(pend["ids"], halves, grad_half_swap(halves, "grad_half_swap")):
            own_half[u], sib_half[u] = h, s

    pending = None
    for l in reversed(range(L)):
        if l == NA - 1:
            dkv = jnp.concatenate([dk, dv], axis=1)
            put("w_kv", 0, wgrad("mod", (xkv, mkv[1][None], mkv[0][None]), dkv, D, dkv.shape[1] // N_CHIP, "kv_wgrad"))
            dx, acc = dgrad_mod(dkv, W("w_kv", 0), dx, xkv, mkv[1][None], "kv_dgrad")
            dmkv = jnp.stack([acc[1], acc[0]])
        sv = saved[l]
        for i in (2, 1, 0):
            t = sv[i]
            wres = 1.0 if i == 1 else 0.5
            dxa, dy, acc1 = ln_res_bwd(dx, t["xh"], t["rs"], ln_g_f[l, i][None], t["y"], t["gw"], wres, alpha,
                                       "ln_res_bwd")
            d_ln_g[l][i], d_ln_b[l][i], dmod[l][3 * i + 2] = acc1[0], acc1[1], acc1[2]
            scl, shf = mrow(l, 3 * i + 1), mrow(l, 3 * i)
            if i != 1:
                k = 2 * l + i // 2
                F = t["gu"].shape[1] // 2
                dgu = ffn_act_bwd(dy, Wrows("ffn_down", k), t["gu"], "ffn_act_bwd")
                put("ffn_down", k, wgrad("swiglu", (t["gu"],), dy, F // 2, D, "ffn_down_wgrad"))
                put("ffn_gu", k, wgrad("mod", (t["x"], scl, shf), dgu, D, 2 * F // N_CHIP, "ffn_up_wgrad"))
                dx, acc2 = dgrad_mod(dgu, W("ffn_gu", k), dxa, t["x"], scl, "ffn_up_dgrad")
            elif l < NA:
                dq = matmul_nt(dy, Wrows("gmlp_w_out", l), "gmlp_out_dgrad")
                put("gmlp_w_out", l, wgrad("plain", (t["a"],), dy, t["a"].shape[1], D, "gmlp_out_wgrad"))
                dpre, dws_l, dss, dgl, dbin = sgu_bwd(dq, t["pre"], gln_g_f[l][None], gln_b_f[l][None], gmlp_w_s[l],
                                                      bst[l], "sgu_bwd")
                d_ws[l] = dws_l
                d_bs[l] = jnp.transpose(group_lane_sum(dss, "sgu_bias_grad")[:, :GMLP_GROUPS])
                d_gln_g[l], d_gln_b[l], d_b_in[l] = dgl[0], dgl[1], dbin[0]
                put("gmlp_w_in", l, wgrad("mod", (t["x"], scl, shf), dpre, D, dpre.shape[1] // N_CHIP, "gmlp_in_wgrad"))
                dx, acc2 = dgrad_mod(dpre, W("gmlp_w_in", l), dxa, t["x"], scl, "gmlp_in_dgrad")
            else:
                j = l - NA
                do = matmul_nt(dy, Wrows("attn_w_o", j), "attn_out_dgrad")
                put("attn_w_o", j, wgrad("plain", (t["a"],), dy, D, D, "attn_out_wgrad"))
                dqh, dk, dv, dbias = attn_bwd(t["q"], do, kpad, vpad, biases[j], dk, dv, "attn_bwd")
                d_rel[j] = bias_grad(jnp.transpose(dbias, (1, 0, 2)), "bias_grad")
                put("attn_w_q", j, wgrad("mod", (t["x"], scl, shf), dqh, D, D, "attn_q_wgrad"))
                dx, acc2 = dgrad_mod(dqh, Wrows("attn_w_q", j)[None], dxa, t["x"], scl, "attn_q_dgrad")
            dmod[l][3 * i + 1], dmod[l][3 * i] = acc2[0], acc2[1]
        started, token = start_grad_exchange(list(made), l)
        made.clear()
        if pending is not None:
            finish_grad_exchange(pending, dx)
        pending = started
        dx = dx + token[0, 0]
    grad_x = dx[None]

    dvec = _pack_rows([jnp.stack([jnp.stack(r) for r in dmod]), dmkv])
    n_dvec = L * N_MOD * D + 2 * D
    dall = all_gather8(dvec, "ag_dmod").reshape(N_DEV, -1, LANES)
    db_all = sum_leading(dall, "ada_bias_grad").reshape(-1)[:n_dvec]
    g_b_ada = db_all[:L * N_MOD * D].reshape(L, N_MOD * D)
    g_b_ada_kv = db_all[L * N_MOD * D:]
    dall2 = dall.reshape(N_DEV, -1)[:, :n_dvec]
    dmod_all = dall2[:, :L * N_MOD * D].reshape(N_DEV, L, N_MOD * D)
    dmod_sh = jnp.transpose(lax.dynamic_slice_in_dim(dmod_all, s_me * n_ada, n_ada, axis=2), (1, 0, 2))
    dmkv_sh = lax.dynamic_slice_in_dim(dall2[:, L * N_MOD * D:], s_me * n_kv, n_kv, axis=1)[None]
    c_all_t = jnp.transpose(c_all)
    g_w_ada = ada_wgrad(c_all_t, dmod_sh, "ada_wgrad")
    g_w_ada_kv = ada_wgrad(c_all_t, dmkv_sh, "ada_kv_wgrad")[0]

    small_g = [jnp.stack([jnp.stack(r) for r in d_ln_g]), jnp.stack([jnp.stack(r) for r in d_ln_b]),
               jnp.stack(d_b_in), jnp.stack(d_gln_g), jnp.stack(d_gln_b), jnp.stack(d_rel), jnp.stack(d_ws),
               jnp.stack(d_bs)]
    sg_shapes = [a.shape for a in small_g]
    sg_pack = _pack_rows(small_g)
    sg_all = all_gather8(sg_pack, "ag_small_grads").reshape(N_DEV, -1, LANES)
    sg_sum = _unpack_rows(sum_leading(sg_all, "small_grad_sum"), sg_shapes)
    g_ln_g, g_ln_b, g_b_in, g_gln_g, g_gln_b, g_rel = [_shard_last(a, s_me) for a in sg_sum[:6]]
    g_ws, g_bs = sg_sum[6], sg_sum[7]

    finish_grad_exchange(pending, sg_sum[0])

    def full_grad(u):
        lo = jnp.where(ci == 0, own_half[u], sib_half[u])
        hi = jnp.where(ci == 0, sib_half[u], own_half[u])
        return jnp.concatenate([lo, hi], axis=0)

    big_g = {nm: jnp.stack([full_grad(unit_of[(nm, k)]) for k in range(stacks[si].shape[0])]).reshape(
        stack_src[nm].shape) for si, nm in enumerate(stack_names)}
    big_g["w_kv"] = big_g["w_kv"][0]

    grads = dict(w_ada=g_w_ada, b_ada=g_b_ada, ln_g=g_ln_g, ln_b=g_ln_b, ffn_gu=big_g["ffn_gu"],
                 ffn_down=big_g["ffn_down"], gmlp_w_in=big_g["gmlp_w_in"], gmlp_b_in=g_b_in, gmlp_ln_g=g_gln_g,
                 gmlp_ln_b=g_gln_b, gmlp_w_s=g_ws, gmlp_b_s=g_bs, gmlp_w_out=big_g["gmlp_w_out"],
                 w_ada_kv=g_w_ada_kv, b_ada_kv=g_b_ada_kv, w_kv=big_g["w_kv"], attn_w_q=big_g["attn_w_q"],
                 attn_rel_bias=g_rel, attn_w_o=big_g["attn_w_o"])
    weights = dict(w_ada=w_ada, b_ada=b_ada, ln_g=ln_g, ln_b=ln_b, ffn_gu=ffn_gu, ffn_down=ffn_down,
                   gmlp_w_in=gmlp_w_in, gmlp_b_in=gmlp_b_in, gmlp_ln_g=gmlp_ln_g, gmlp_ln_b=gmlp_ln_b,
                   gmlp_w_s=gmlp_w_s, gmlp_b_s=gmlp_b_s, gmlp_w_out=gmlp_w_out, w_ada_kv=w_ada_kv,
                   b_ada_kv=b_ada_kv, w_kv=w_kv, attn_w_q=attn_w_q, attn_rel_bias=attn_rel_bias, attn_w_o=attn_w_o)
    ms = dict(w_ada=m_w_ada, b_ada=m_b_ada, ln_g=m_ln_g, ln_b=m_ln_b, ffn_gu=m_ffn_gu, ffn_down=m_ffn_down,
              gmlp_w_in=m_gmlp_w_in, gmlp_b_in=m_gmlp_b_in, gmlp_ln_g=m_gmlp_ln_g, gmlp_ln_b=m_gmlp_ln_b,
              gmlp_w_s=m_gmlp_w_s, gmlp_b_s=m_gmlp_b_s, gmlp_w_out=m_gmlp_w_out, w_ada_kv=m_w_ada_kv,
              b_ada_kv=m_b_ada_kv, w_kv=m_w_kv, attn_w_q=m_attn_w_q, attn_rel_bias=m_attn_rel_bias,
              attn_w_o=m_attn_w_o)
    vs = dict(w_ada=v_w_ada, b_ada=v_b_ada, ln_g=v_ln_g, ln_b=v_ln_b, ffn_gu=v_ffn_gu, ffn_down=v_ffn_down,
              gmlp_w_in=v_gmlp_w_in, gmlp_b_in=v_gmlp_b_in, gmlp_ln_g=v_gmlp_ln_g, gmlp_ln_b=v_gmlp_ln_b,
              gmlp_w_s=v_gmlp_w_s, gmlp_b_s=v_gmlp_b_s, gmlp_w_out=v_gmlp_w_out, w_ada_kv=v_w_ada_kv,
              b_ada_kv=v_b_ada_kv, w_kv=v_w_kv, attn_w_q=v_attn_w_q, attn_rel_bias=v_attn_rel_bias,
              attn_w_o=v_attn_w_o)
    order = ["w_ada", "b_ada", "ln_g", "ln_b", "ffn_gu", "ffn_down", "gmlp_w_in", "gmlp_b_in", "gmlp_ln_g",
             "gmlp_ln_b", "gmlp_w_s", "gmlp_b_s", "gmlp_w_out", "w_ada_kv", "b_ada_kv", "w_kv", "attn_w_q",
             "attn_rel_bias", "attn_w_o"]
    big_names = ["w_ada", "ffn_gu", "ffn_down", "gmlp_w_in", "gmlp_w_out", "w_ada_kv", "w_kv", "attn_w_q", "attn_w_o"]
    small_names = [nm for nm in order if nm not in big_names]
    delta, new_m, new_v = {}, {}, {}
    for nm in big_names:
        shp = weights[nm].shape
        two_d = (-1, shp[-1])
        d, a, b = adamw(weights[nm].reshape(two_d), grads[nm].reshape(two_d), ms[nm].reshape(two_d),
                        vs[nm].reshape(two_d), "adamw")
        delta[nm], new_m[nm], new_v[nm] = d.reshape(shp), a.reshape(shp), b.reshape(shp)
    shapes = [weights[nm].shape for nm in small_names]
    d, a, b = adamw(_pack_rows([weights[nm] for nm in small_names]), _pack_rows([grads[nm] for nm in small_names]),
                    _pack_rows([ms[nm] for nm in small_names]), _pack_rows([vs[nm] for nm in small_names]),
                    "adamw_small")
    for nm, dd, aa, bb in zip(small_names, _unpack_rows(d, shapes), _unpack_rows(a, shapes), _unpack_rows(b, shapes)):
        delta[nm], new_m[nm], new_v[nm] = dd, aa, bb

    return (loss, grad_x, *[grads[nm] for nm in order], *[delta[nm] for nm in order],
            *[new_m[nm] for nm in order], *[new_v[nm] for nm in order])
```

```python
import functools

import jax
import jax.numpy as jnp
from jax import lax
from jax.experimental import pallas as pl
from jax.experimental.pallas import tpu as pltpu

F32 = jnp.float32
BF16 = jnp.bfloat16
MESH = pl.DeviceIdType.MESH
HIGHEST = lax.Precision.HIGHEST

CHUNK = 64
GMLP_WINDOW = 128
GMLP_GROUPS = 8
HEAD_DIM = 64
LEFT_CHUNKS = 8
BAND = (LEFT_CHUNKS + 1) * CHUNK
LEFT_PAD = LEFT_CHUNKS * CHUNK
MAX_REL = 4 * CHUNK
N_REL = (CHUNK - 1) + MAX_REL + 1
LN_EPS = 1e-5
N_MOD = 9
N_DEV = 8
N_CHIP = 4

ADAM_LR = 0.001
ADAM_B1 = 0.9
ADAM_B2 = 0.999
ADAM_EPS = 1e-08
ADAM_WD = 0.01
ADAM_STEP = 10

LANES = 128
ROW_TILE = 256
WGRAD_ROWS = 512
ATTN_CHUNKS_PER_STEP = 4
VMEM_LIMIT_MB = 56

NT = (((1,), (1,)), ((), ()))
TN = (((0,), (0,)), ((), ()))

ANY = pl.BlockSpec(memory_space=pl.ANY)
VMEM_SPEC = pl.BlockSpec(memory_space=pltpu.VMEM)


def _params(semantics=None):
    kw = dict(vmem_limit_bytes=VMEM_LIMIT_MB * 1024 * 1024)
    if semantics is not None:
        kw["dimension_semantics"] = semantics
    return pltpu.CompilerParams(**kw)


def _sigmoid(v):
    return 1.0 / (1.0 + jnp.exp(-v))


def _gelu(v):
    return 0.5 * v * (1.0 + lax.erf(v * (2.0 ** -0.5)))


def _gelu_grad(v):
    return 0.5 * (1.0 + lax.erf(v * (2.0 ** -0.5))) + v * jnp.exp(-0.5 * v * v) * ((2.0 * jnp.pi) ** -0.5)


def _row(m):
    return lambda i: (i, 0)


def _fixed2(i):
    return (0, 0)


def _fixed3(i):
    return (0, 0, 0)


def mod_matmul(x, scl, shift, w, bias, out_dtype, name):
    S, D = x.shape
    NS, _, n = w.shape
    tm = min(ROW_TILE, S)
    has_bias = bias is not None

    def body(*refs):
        if has_bias:
            x_ref, scl_ref, sh_ref, w_ref, b_ref, o_ref = refs
        else:
            x_ref, scl_ref, sh_ref, w_ref, o_ref = refs
        h = (x_ref[...] * (1.0 + scl_ref[...]) + sh_ref[...]).astype(BF16)
        for s in range(NS):
            acc = jnp.dot(h, w_ref[s], preferred_element_type=F32)
            if has_bias:
                acc = acc + b_ref[:, s * n:(s + 1) * n]
            o_ref[:, s * n:(s + 1) * n] = acc.astype(out_dtype)

    in_specs = [pl.BlockSpec((tm, D), _row(0)), pl.BlockSpec((1, D), _fixed2), pl.BlockSpec((1, D), _fixed2),
                pl.BlockSpec((NS, D, n), _fixed3)]
    args = [x, scl, shift, w]
    if has_bias:
        in_specs.append(pl.BlockSpec((1, NS * n), _fixed2))
        args.append(bias)
    return pl.pallas_call(
        body, name=name, grid=(S // tm,), in_specs=in_specs,
        out_specs=pl.BlockSpec((tm, NS * n), _row(0)),
        out_shape=jax.ShapeDtypeStruct((S, NS * n), out_dtype),
        compiler_params=_params(("parallel",)),
    )(*args)


def matmul_res_ln(a, w, x, gw, lg, lb, alpha, swiglu, name):
    S, D = x.shape
    K = w.shape[0]
    tm = min(ROW_TILE, S)
    ka = a.shape[1]

    def body(a_ref, w_ref, x_ref, gw_ref, lg_ref, lb_ref, xn_ref, xh_ref, rs_ref, y_ref):
        if swiglu:
            g = a_ref[:, :K].astype(F32)
            u = a_ref[:, K:].astype(F32)
            act = (g * _sigmoid(g) * u).astype(BF16)
        else:
            act = a_ref[...].astype(BF16)
        y = jnp.dot(act, w_ref[...], preferred_element_type=F32)
        z = alpha * x_ref[...] + gw_ref[...] * y
        mu = jnp.mean(z, axis=-1, keepdims=True)
        zc = z - mu
        var = jnp.mean(zc * zc, axis=-1, keepdims=True)
        rstd = lax.rsqrt(var + LN_EPS)
        xhat = zc * rstd
        xn_ref[...] = xhat * lg_ref[...] + lb_ref[...]
        xh_ref[...] = xhat
        rs_ref[...] = rstd
        y_ref[...] = y.astype(BF16)

    vec = pl.BlockSpec((1, D), _fixed2)
    return pl.pallas_call(
        body, name=name, grid=(S // tm,),
        in_specs=[pl.BlockSpec((tm, ka), _row(0)), pl.BlockSpec((K, D), _fixed2), pl.BlockSpec((tm, D), _row(0)),
                  vec, vec, vec],
        out_specs=[pl.BlockSpec((tm, D), _row(0)), pl.BlockSpec((tm, D), _row(0)), pl.BlockSpec((tm, 1), _row(0)),
                   pl.BlockSpec((tm, D), _row(0))],
        out_shape=[jax.ShapeDtypeStruct((S, D), F32), jax.ShapeDtypeStruct((S, D), F32),
                   jax.ShapeDtypeStruct((S, 1), F32), jax.ShapeDtypeStruct((S, D), BF16)],
        compiler_params=_params(("parallel",)),
    )(a, w, x, gw, lg, lb)


def ln_res_bwd(dxn, xhat, rstd, lg, y, gw, wres, alpha, name):
    S, D = dxn.shape
    tm = min(ROW_TILE, S)

    def body(dxn_ref, xh_ref, rs_ref, lg_ref, y_ref, gw_ref, dxa_ref, dy_ref, acc_ref):
        @pl.when(pl.program_id(0) == 0)
        def _():
            acc_ref[...] = jnp.zeros_like(acc_ref)

        d = dxn_ref[...]
        xh = xh_ref[...]
        dxh = d * lg_ref[...]
        m1 = jnp.mean(dxh, axis=-1, keepdims=True)
        m2 = jnp.mean(dxh * xh, axis=-1, keepdims=True)
        dz = rs_ref[...] * (dxh - m1 - xh * m2)
        dxa_ref[...] = alpha * dz
        dy_ref[...] = (gw_ref[...] * dz).astype(BF16)
        acc_ref[0:1, :] += jnp.sum(d * xh, axis=0, keepdims=True)
        acc_ref[1:2, :] += jnp.sum(d, axis=0, keepdims=True)
        acc_ref[2:3, :] += jnp.sum((wres * dz) * y_ref[...].astype(F32), axis=0, keepdims=True)

    vec = pl.BlockSpec((1, D), _fixed2)
    tile = pl.BlockSpec((tm, D), _row(0))
    return pl.pallas_call(
        body, name=name, grid=(S // tm,),
        in_specs=[tile, tile, pl.BlockSpec((tm, 1), _row(0)), vec, tile, vec],
        out_specs=[tile, tile, pl.BlockSpec((8, D), _fixed2)],
        out_shape=[jax.ShapeDtypeStruct((S, D), F32), jax.ShapeDtypeStruct((S, D), BF16),
                   jax.ShapeDtypeStruct((8, D), F32)],
        compiler_params=_params(("arbitrary",)),
    )(dxn, xhat, rstd, lg, y, gw)


def ffn_act_bwd(dy, wd, gu, name):
    S, D = dy.shape
    K = wd.shape[0]
    tm = min(ROW_TILE, S)

    def body(dy_ref, wd_ref, gu_ref, o_ref):
        da = lax.dot_general(dy_ref[...], wd_ref[...], NT, preferred_element_type=F32)
        g = gu_ref[:, :K].astype(F32)
        u = gu_ref[:, K:].astype(F32)
        sg = _sigmoid(g)
        o_ref[:, :K] = (da * u * (sg * (1.0 + g * (1.0 - sg)))).astype(BF16)
        o_ref[:, K:] = (da * (g * sg)).astype(BF16)

    return pl.pallas_call(
        body, name=name, grid=(S // tm,),
        in_specs=[pl.BlockSpec((tm, D), _row(0)), pl.BlockSpec((K, D), _fixed2), pl.BlockSpec((tm, 2 * K), _row(0))],
        out_specs=pl.BlockSpec((tm, 2 * K), _row(0)),
        out_shape=jax.ShapeDtypeStruct((S, 2 * K), BF16),
        compiler_params=_params(("parallel",)),
    )(dy, wd, gu)


def matmul_nt(a, w, name):
    S, D = a.shape
    K = w.shape[0]
    tm = min(ROW_TILE, S)

    def body(a_ref, w_ref, o_ref):
        o_ref[...] = lax.dot_general(a_ref[...], w_ref[...], NT, preferred_element_type=F32).astype(BF16)

    return pl.pallas_call(
        body, name=name, grid=(S // tm,),
        in_specs=[pl.BlockSpec((tm, D), _row(0)), pl.BlockSpec((K, D), _fixed2)],
        out_specs=pl.BlockSpec((tm, K), _row(0)),
        out_shape=jax.ShapeDtypeStruct((S, K), BF16),
        compiler_params=_params(("parallel",)),
    )(a, w)


def dgrad_mod(dpre, w, dxa, xin, scl, name):
    S, D = xin.shape
    NS, _, n = w.shape
    tm = min(ROW_TILE, S)

    def body(dp_ref, w_ref, dxa_ref, xin_ref, scl_ref, dx_ref, acc_ref):
        @pl.when(pl.program_id(0) == 0)
        def _():
            acc_ref[...] = jnp.zeros_like(acc_ref)

        dh = jnp.zeros((tm, D), F32)
        for s in range(NS):
            dh = dh + lax.dot_general(dp_ref[:, s * n:(s + 1) * n].astype(BF16), w_ref[s], NT,
                                      preferred_element_type=F32)
        dx_ref[...] = dxa_ref[...] + dh * (1.0 + scl_ref[...])
        acc_ref[0:1, :] += jnp.sum(dh * xin_ref[...], axis=0, keepdims=True)
        acc_ref[1:2, :] += jnp.sum(dh, axis=0, keepdims=True)

    tile = pl.BlockSpec((tm, D), _row(0))
    return pl.pallas_call(
        body, name=name, grid=(S // tm,),
        in_specs=[pl.BlockSpec((tm, NS * n), _row(0)), pl.BlockSpec((NS, D, n), _fixed3), tile, tile,
                  pl.BlockSpec((1, D), _fixed2)],
        out_specs=[tile, pl.BlockSpec((8, D), _fixed2)],
        out_shape=[jax.ShapeDtypeStruct((S, D), F32), jax.ShapeDtypeStruct((8, D), F32)],
        compiler_params=_params(("arbitrary",)),
    )(dpre, w, dxa, xin, scl)


def wgrad(kind, a_args, b, kb, nb, name):
    S, N = b.shape
    if kind == "swiglu":
        K = a_args[0].shape[1] // 2
    else:
        K = a_args[0].shape[1]
    NBk, KB = N // nb, K // kb
    ts = min(WGRAD_ROWS, S)
    nsteps = S // ts

    def body(*refs):
        acc_ref = refs[-1]
        o_ref = refs[-2]
        b_ref = refs[-3]
        si = pl.program_id(2)

        @pl.when(si == 0)
        def _():
            acc_ref[...] = jnp.zeros_like(acc_ref)

        if kind == "mod":
            x_ref, scl_ref, sh_ref = refs[:3]
            a = (x_ref[...] * (1.0 + scl_ref[...]) + sh_ref[...]).astype(BF16)
        elif kind == "swiglu":
            g = refs[0][...].astype(F32)
            u = refs[1][...].astype(F32)
            a = (g * _sigmoid(g) * u).astype(BF16)
        else:
            a = refs[0][...].astype(BF16)
        acc_ref[...] += lax.dot_general(a, b_ref[...].astype(BF16), TN, preferred_element_type=F32)

        @pl.when(si == nsteps - 1)
        def _():
            o_ref[...] = acc_ref[...].astype(BF16)

    a_tile = pl.BlockSpec((ts, kb), lambda j, k, s: (s, k))
    if kind == "mod":
        vec = pl.BlockSpec((1, kb), lambda j, k, s: (0, k))
        in_specs = [a_tile, vec, vec]
        args = list(a_args)
    elif kind == "swiglu":
        in_specs = [a_tile, pl.BlockSpec((ts, kb), lambda j, k, s: (s, k + KB))]
        args = [a_args[0], a_args[0]]
    else:
        in_specs = [a_tile]
        args = list(a_args)
    in_specs.append(pl.BlockSpec((ts, nb), lambda j, k, s: (s, j)))
    args.append(b)
    return pl.pallas_call(
        body, name=name, grid=(NBk, KB, nsteps), in_specs=in_specs,
        out_specs=pl.BlockSpec((None, kb, nb), lambda j, k, s: (j, k, 0)),
        out_shape=jax.ShapeDtypeStruct((NBk, K, nb), BF16),
        scratch_shapes=[pltpu.VMEM((kb, nb), F32)],
        compiler_params=_params(("parallel", "parallel", "arbitrary")),
    )(*args)


def _window_mask():
    t = lax.broadcasted_iota(jnp.int32, (GMLP_WINDOW, GMLP_WINDOW), 0)
    s = lax.broadcasted_iota(jnp.int32, (GMLP_WINDOW, GMLP_WINDOW), 1)
    return ((s // CHUNK) <= (t // CHUNK)).astype(F32)


def sgu_fwd(pre, glg, glb, ws, bst, name):
    S, H2 = pre.shape
    H = H2 // 2
    W, G = GMLP_WINDOW, GMLP_GROUPS
    gd = H // G
    tm = min(ROW_TILE, S)

    def body(pre_ref, glg_ref, glb_ref, ws_ref, bst_ref, q_ref):
        u = _gelu(pre_ref[:, :H])
        v = _gelu(pre_ref[:, H:])
        mu = jnp.mean(v, axis=-1, keepdims=True)
        vc = v - mu
        var = jnp.mean(vc * vc, axis=-1, keepdims=True)
        vn = ((vc * lax.rsqrt(var + LN_EPS)) * glg_ref[...] + glb_ref[...]).astype(BF16)
        mask = _window_mask()
        for g in range(G):
            wsg = (ws_ref[g] * mask).astype(BF16)
            bcol = bst_ref[:, g:g + 1]
            for wi in range(tm // W):
                rows = slice(wi * W, (wi + 1) * W)
                cols = slice(g * gd, (g + 1) * gd)
                s = jnp.dot(wsg, vn[rows, cols], preferred_element_type=F32) + bcol
                q_ref[rows, cols] = (u[rows, cols] * s).astype(BF16)

    return pl.pallas_call(
        body, name=name, grid=(S // tm,),
        in_specs=[pl.BlockSpec((tm, H2), _row(0)), pl.BlockSpec((1, H), _fixed2), pl.BlockSpec((1, H), _fixed2),
                  pl.BlockSpec((G, W, W), _fixed3), pl.BlockSpec((W, G), _fixed2)],
        out_specs=pl.BlockSpec((tm, H), _row(0)),
        out_shape=jax.ShapeDtypeStruct((S, H), BF16),
        compiler_params=_params(("parallel",)),
    )(pre, glg, glb, ws, bst)


def sgu_bwd(dq, pre, glg, glb, ws, bst, name):
    S, H2 = pre.shape
    H = H2 // 2
    W, G = GMLP_WINDOW, GMLP_GROUPS
    gd = H // G
    tm = min(ROW_TILE, S)

    def body(dq_ref, pre_ref, glg_ref, glb_ref, ws_ref, bst_ref,
             dpre_ref, dws_ref, dss_ref, dgl_ref, dbin_ref, du_s, dvn_s):
        @pl.when(pl.program_id(0) == 0)
        def _():
            dws_ref[...] = jnp.zeros_like(dws_ref)
            dss_ref[...] = jnp.zeros_like(dss_ref)
            dgl_ref[...] = jnp.zeros_like(dgl_ref)
            dbin_ref[...] = jnp.zeros_like(dbin_ref)

        pu = pre_ref[:, :H]
        pv = pre_ref[:, H:]
        u = _gelu(pu)
        v = _gelu(pv)
        mu = jnp.mean(v, axis=-1, keepdims=True)
        vc = v - mu
        var = jnp.mean(vc * vc, axis=-1, keepdims=True)
        rstd = lax.rsqrt(var + LN_EPS)
        vhat = vc * rstd
        vn = (vhat * glg_ref[...] + glb_ref[...]).astype(BF16)
        mask = _window_mask()
        for g in range(G):
            wsg = (ws_ref[g] * mask).astype(BF16)
            bcol = bst_ref[:, g:g + 1]
            cols = slice(g * gd, (g + 1) * gd)
            for wi in range(tm // W):
                rows = slice(wi * W, (wi + 1) * W)
                vblk = vn[rows, cols]
                s = jnp.dot(wsg, vblk, preferred_element_type=F32) + bcol
                dqb = dq_ref[rows, cols].astype(F32)
                du_s[rows, cols] = dqb * s
                ds = dqb * u[rows, cols]
                dss_ref[:, cols] += ds
                dsb = ds.astype(BF16)
                dvn_s[rows, cols] = lax.dot_general(wsg, dsb, TN, preferred_element_type=F32)
                dws_ref[g] += lax.dot_general(dsb, vblk, NT, preferred_element_type=F32) * mask
        dvn = dvn_s[...]
        dgl_ref[0:1, :] += jnp.sum(dvn * vhat, axis=0, keepdims=True)
        dgl_ref[1:2, :] += jnp.sum(dvn, axis=0, keepdims=True)
        dvh = dvn * glg_ref[...]
        m1 = jnp.mean(dvh, axis=-1, keepdims=True)
        m2 = jnp.mean(dvh * vhat, axis=-1, keepdims=True)
        dv = rstd * (dvh - m1 - vhat * m2)
        dpu = du_s[...] * _gelu_grad(pu)
        dpv = dv * _gelu_grad(pv)
        dbin_ref[0:1, :H] += jnp.sum(dpu, axis=0, keepdims=True)
        dbin_ref[0:1, H:] += jnp.sum(dpv, axis=0, keepdims=True)
        dpre_ref[:, :H] = dpu.astype(BF16)
        dpre_ref[:, H:] = dpv.astype(BF16)

    return pl.pallas_call(
        body, name=name, grid=(S // tm,),
        in_specs=[pl.BlockSpec((tm, H), _row(0)), pl.BlockSpec((tm, H2), _row(0)), pl.BlockSpec((1, H), _fixed2),
                  pl.BlockSpec((1, H), _fixed2), pl.BlockSpec((G, W, W), _fixed3), pl.BlockSpec((W, G), _fixed2)],
        out_specs=[pl.BlockSpec((tm, H2), _row(0)), pl.BlockSpec((G, W, W), _fixed3), pl.BlockSpec((W, H), _fixed2),
                   pl.BlockSpec((8, H), _fixed2), pl.BlockSpec((8, H2), _fixed2)],
        out_shape=[jax.ShapeDtypeStruct((S, H2), BF16), jax.ShapeDtypeStruct((G, W, W), F32),
                   jax.ShapeDtypeStruct((W, H), F32), jax.ShapeDtypeStruct((8, H), F32),
                   jax.ShapeDtypeStruct((8, H2), F32)],
        scratch_shapes=[pltpu.VMEM((tm, H), F32), pltpu.VMEM((tm, H), F32)],
        compiler_params=_params(("arbitrary",)),
    )(dq, pre, glg, glb, ws, bst)


def group_lane_sum(dss, name):
    W, H = dss.shape
    gd = H // GMLP_GROUPS

    def body(d_ref, o_ref):
        j = lax.broadcasted_iota(jnp.int32, (H, LANES), 0)
        g = lax.broadcasted_iota(jnp.int32, (H, LANES), 1)
        ind = ((j // gd) == g).astype(F32)
        o_ref[...] = jnp.dot(d_ref[...], ind, preferred_element_type=F32, precision=HIGHEST)

    return pl.pallas_call(
        body, name=name, in_specs=[VMEM_SPEC], out_specs=VMEM_SPEC,
        out_shape=jax.ShapeDtypeStruct((W, LANES), F32), compiler_params=_params(),
    )(dss)


def _attn_scores(q2, k2, bias, sel, valid, scale):
    qm = jnp.where(sel, q2, jnp.zeros_like(q2))
    s = lax.dot_general(qm, k2, NT, preferred_element_type=F32) * scale + bias
    s = jnp.where(valid, s, -jnp.inf)
    m = jnp.max(s, axis=-1, keepdims=True)
    e = jnp.exp(s - m)
    p = e / jnp.sum(e, axis=-1, keepdims=True)
    return qm, p


def attn_fwd(q, kpad, vpad, bias, name):
    S, D = q.shape
    HP = D // LANES
    cps = min(ATTN_CHUNKS_PER_STEP, S // CHUNK)
    tq = cps * CHUNK
    scale = HEAD_DIM ** -0.5

    def body(q_ref, k_ref, v_ref, b_ref, o_ref):
        j = pl.program_id(1)
        sel0 = lax.broadcasted_iota(jnp.int32, (CHUNK, LANES), 1) < HEAD_DIM
        r = lax.broadcasted_iota(jnp.int32, (CHUNK, BAND), 1)
        for cc in range(cps):
            start = pl.multiple_of((j * cps + cc) * CHUNK, CHUNK)
            rows = slice(cc * CHUNK, (cc + 1) * CHUNK)
            q2 = q_ref[rows, :]
            k2 = k_ref[pl.ds(start, BAND), :]
            v2 = v_ref[pl.ds(start, BAND), :]
            valid = (r + start) >= LEFT_PAD
            outs = []
            for sub in range(2):
                sel = sel0 if sub == 0 else jnp.logical_not(sel0)
                _, p = _attn_scores(q2, k2, b_ref[sub], sel, valid, scale)
                outs.append(jnp.dot(p.astype(BF16), v2, preferred_element_type=F32))
            o_ref[rows, :] = jnp.where(sel0, outs[0], outs[1]).astype(BF16)

    kv_spec = pl.BlockSpec((S + LEFT_PAD, LANES), lambda h, j: (0, h))
    return pl.pallas_call(
        body, name=name, grid=(HP, S // tq),
        in_specs=[pl.BlockSpec((tq, LANES), lambda h, j: (j, h)), kv_spec, kv_spec,
                  pl.BlockSpec((2, CHUNK, BAND), lambda h, j: (h, 0, 0))],
        out_specs=pl.BlockSpec((tq, LANES), lambda h, j: (j, h)),
        out_shape=jax.ShapeDtypeStruct((S, D), BF16),
        compiler_params=_params(("parallel", "parallel")),
    )(q, kpad, vpad, bias)


def attn_bwd(q, do, kpad, vpad, bias, dk_in, dv_in, name):
    S, D = q.shape
    HP = D // LANES
    NH = 2 * HP
    cps = min(ATTN_CHUNKS_PER_STEP, S // CHUNK)
    tq = cps * CHUNK
    nj = S // tq
    scale = HEAD_DIM ** -0.5

    def body(q_ref, do_ref, k_ref, v_ref, b_ref, dki_ref, dvi_ref, dq_ref, dk_ref, dv_ref, db_ref, dk_acc, dv_acc):
        j = pl.program_id(1)

        @pl.when(j == 0)
        def _():
            dk_acc[:LEFT_PAD, :] = jnp.zeros((LEFT_PAD, LANES), F32)
            dv_acc[:LEFT_PAD, :] = jnp.zeros((LEFT_PAD, LANES), F32)
            dk_acc[LEFT_PAD:, :] = dki_ref[...]
            dv_acc[LEFT_PAD:, :] = dvi_ref[...]
            db_ref[...] = jnp.zeros_like(db_ref)

        sel0 = lax.broadcasted_iota(jnp.int32, (CHUNK, LANES), 1) < HEAD_DIM
        r = lax.broadcasted_iota(jnp.int32, (CHUNK, BAND), 1)
        for cc in range(cps):
            start = pl.multiple_of((j * cps + cc) * CHUNK, CHUNK)
            rows = slice(cc * CHUNK, (cc + 1) * CHUNK)
            q2 = q_ref[rows, :]
            do2 = do_ref[rows, :]
            k2 = k_ref[pl.ds(start, BAND), :]
            v2 = v_ref[pl.ds(start, BAND), :]
            valid = (r + start) >= LEFT_PAD
            dqs = []
            for sub in range(2):
                sel = sel0 if sub == 0 else jnp.logical_not(sel0)
                qm, p = _attn_scores(q2, k2, b_ref[sub], sel, valid, scale)
                dom = jnp.where(sel, do2, jnp.zeros_like(do2))
                dp = lax.dot_general(dom, v2, NT, preferred_element_type=F32)
                delta = jnp.sum(dp * p, axis=-1, keepdims=True)
                ds = p * (dp - delta)
                db_ref[sub] += ds
                dsb = ds.astype(BF16)
                dqs.append(jnp.dot(dsb, k2, preferred_element_type=F32) * scale)
                dk_acc[pl.ds(start, BAND), :] += lax.dot_general(dsb, qm, TN, preferred_element_type=F32) * scale
                dv_acc[pl.ds(start, BAND), :] += lax.dot_general(p.astype(BF16), dom, TN, preferred_element_type=F32)
            dq_ref[rows, :] = jnp.where(sel0, dqs[0], dqs[1]).astype(BF16)

        @pl.when(j == nj - 1)
        def _():
            dk_ref[...] = dk_acc[LEFT_PAD:, :]
            dv_ref[...] = dv_acc[LEFT_PAD:, :]

    q_spec = pl.BlockSpec((tq, LANES), lambda h, j: (j, h))
    kv_spec = pl.BlockSpec((S + LEFT_PAD, LANES), lambda h, j: (0, h))
    col_spec = pl.BlockSpec((S, LANES), lambda h, j: (0, h))
    b_spec = pl.BlockSpec((2, CHUNK, BAND), lambda h, j: (h, 0, 0))
    return pl.pallas_call(
        body, name=name, grid=(HP, nj),
        in_specs=[q_spec, q_spec, kv_spec, kv_spec, b_spec, col_spec, col_spec],
        out_specs=[q_spec, col_spec, col_spec, b_spec],
        out_shape=[jax.ShapeDtypeStruct((S, D), BF16), jax.ShapeDtypeStruct((S, D), F32),
                   jax.ShapeDtypeStruct((S, D), F32), jax.ShapeDtypeStruct((NH, CHUNK, BAND), F32)],
        scratch_shapes=[pltpu.VMEM((S + LEFT_PAD, LANES), F32), pltpu.VMEM((S + LEFT_PAD, LANES), F32)],
        compiler_params=_params(("parallel", "arbitrary")),
    )(q, do, kpad, vpad, bias, dk_in, dv_in)


def _rel_onehot(t):
    r = lax.broadcasted_iota(jnp.int32, (BAND, N_REL), 0)
    i = lax.broadcasted_iota(jnp.int32, (BAND, N_REL), 1)
    idx = jnp.clip(t + LEFT_PAD - r, -(CHUNK - 1), MAX_REL) + (CHUNK - 1)
    return (idx == i).astype(F32)


def bias_expand(rb, name):
    NH = rb.shape[0]

    def body(rb_ref, o_ref):
        def step(t, carry):
            o_ref[t] = lax.dot_general(rb_ref[...], _rel_onehot(t), NT, preferred_element_type=F32,
                                       precision=HIGHEST)
            return carry

        lax.fori_loop(0, CHUNK, step, 0)

    return pl.pallas_call(
        body, name=name, in_specs=[VMEM_SPEC], out_specs=VMEM_SPEC,
        out_shape=jax.ShapeDtypeStruct((CHUNK, NH, BAND), F32), compiler_params=_params(),
    )(rb)


def bias_grad(dsum, name):
    NH = dsum.shape[1]

    def body(d_ref, o_ref):
        def step(t, acc):
            return acc + jnp.dot(d_ref[t], _rel_onehot(t), preferred_element_type=F32, precision=HIGHEST)

        o_ref[...] = lax.fori_loop(0, CHUNK, step, jnp.zeros((NH, N_REL), F32))

    return pl.pallas_call(
        body, name=name, in_specs=[VMEM_SPEC], out_specs=VMEM_SPEC,
        out_shape=jax.ShapeDtypeStruct((NH, N_REL), F32), compiler_params=_params(),
    )(dsum)


def loss_grad(y, tgt, name):
    S, D = y.shape
    tm = min(ROW_TILE, S)

    def body(y_ref, t_ref, d_ref, acc_ref):
        @pl.when(pl.program_id(0) == 0)
        def _():
            acc_ref[...] = jnp.zeros_like(acc_ref)

        err = y_ref[...] - t_ref[...]
        d_ref[...] = err * (1.0 / D)
        acc_ref[0:1, :] += jnp.sum(err * err, axis=0, keepdims=True)

    tile = pl.BlockSpec((tm, D), _row(0))
    return pl.pallas_call(
        body, name=name, grid=(S // tm,), in_specs=[tile, tile],
        out_specs=[tile, pl.BlockSpec((8, D), _fixed2)],
        out_shape=[jax.ShapeDtypeStruct((S, D), F32), jax.ShapeDtypeStruct((8, D), F32)],
        compiler_params=_params(("arbitrary",)),
    )(y, tgt)


def _col_tile(n):
    for t in (768, 512, 256, 128):
        if n % t == 0:
            return t
    return n


def ada_fwd(c_all, w, b, name):
    L, D, n = w.shape
    tn = _col_tile(n)

    def body(c_ref, w_ref, b_ref, o_ref):
        cv = c_ref[...]
        ca = cv * _sigmoid(cv)
        o_ref[...] = jnp.dot(ca, w_ref[...], preferred_element_type=F32, precision=HIGHEST) + b_ref[...]

    return pl.pallas_call(
        body, name=name, grid=(L, n // tn),
        in_specs=[pl.BlockSpec((N_DEV, D), lambda l, j: (0, 0)), pl.BlockSpec((None, D, tn), lambda l, j: (l, 0, j)),
                  pl.BlockSpec((None, 1, tn), lambda l, j: (l, 0, j))],
        out_specs=pl.BlockSpec((None, N_DEV, tn), lambda l, j: (l, 0, j)),
        out_shape=jax.ShapeDtypeStruct((L, N_DEV, n), F32),
        compiler_params=_params(("parallel", "parallel")),
    )(c_all, w, b)


def ada_wgrad(c_all_t, dmod, name):
    L, _, n = dmod.shape
    D = c_all_t.shape[0]
    tn = _col_tile(n)

    def body(c_ref, d_ref, o_ref):
        cv = c_ref[...]
        ca = cv * _sigmoid(cv)
        o_ref[...] = jnp.dot(ca, d_ref[...], preferred_element_type=F32, precision=HIGHEST)

    return pl.pallas_call(
        body, name=name, grid=(L, n // tn),
        in_specs=[pl.BlockSpec((D, N_DEV), lambda l, j: (0, 0)), pl.BlockSpec((None, N_DEV, tn), lambda l, j: (l, 0, j))],
        out_specs=pl.BlockSpec((None, D, tn), lambda l, j: (l, 0, j)),
        out_shape=jax.ShapeDtypeStruct((L, D, n), F32),
        compiler_params=_params(("parallel", "parallel")),
    )(c_all_t, dmod)


ELEMENTWISE_BLOCK_BYTES = 1024 * 1024


def _elementwise_rows(rows, row_bytes):
    for t in (1024, 512, 256, 128, 64, 32, 16):
        if rows % t == 0 and t * row_bytes <= ELEMENTWISE_BLOCK_BYTES:
            return t
    return rows


def sum_leading(a, name):
    n, M, N = a.shape
    tr = _elementwise_rows(M, n * N * 4)

    def body(a_ref, o_ref):
        acc = a_ref[0]
        for i in range(1, n):
            acc = acc + a_ref[i]
        o_ref[...] = acc

    return pl.pallas_call(
        body, name=name, grid=(M // tr,),
        in_specs=[pl.BlockSpec((n, tr, N), lambda i: (0, i, 0))],
        out_specs=pl.BlockSpec((tr, N), _row(0)),
        out_shape=jax.ShapeDtypeStruct((M, N), F32),
        compiler_params=_params(("parallel",)),
    )(a)


def pair_sum(full, recv, half_idx, name):
    n, M, N = recv.shape
    tr = _elementwise_rows(M, N * 4)
    nblk = M // tr

    def body(h_ref, a_ref, b_ref, o_ref):
        o_ref[...] = (a_ref[...].astype(F32) + b_ref[...].astype(F32)).astype(BF16)

    spec = pl.BlockSpec((None, tr, N), lambda k, i, h: (k, i, 0))
    return pl.pallas_call(
        body, name=name,
        grid_spec=pltpu.PrefetchScalarGridSpec(
            num_scalar_prefetch=1, grid=(n, nblk),
            in_specs=[pl.BlockSpec((None, tr, N), lambda k, i, h: (k, h[0] * nblk + i, 0)), spec], out_specs=spec),
        out_shape=jax.ShapeDtypeStruct((n, M, N), BF16),
        compiler_params=_params(("parallel", "parallel")),
    )(half_idx, full, recv)


def chip_sum(psum, land, chip_idx, name):
    n, M, N = psum.shape
    tr = _elementwise_rows(M, n * N * 4)

    def body(s_ref, p_ref, a_ref, b_ref, c_ref, o_ref):
        o_ref[...] = ((p_ref[...].astype(F32) + a_ref[...].astype(F32)) + b_ref[...].astype(F32)) + c_ref[...].astype(F32)

    def entry(j):
        return pl.BlockSpec((None, tr, N), lambda i, s: ((s[0] + j) % n, i, 0))

    return pl.pallas_call(
        body, name=name,
        grid_spec=pltpu.PrefetchScalarGridSpec(
            num_scalar_prefetch=1, grid=(M // tr,),
            in_specs=[entry(0), entry(1), entry(2), entry(3)],
            out_specs=pl.BlockSpec((tr, N), lambda i, s: (i, 0))),
        out_shape=jax.ShapeDtypeStruct((M, N), F32),
        compiler_params=_params(("parallel",)),
    )(chip_idx, psum, land, land, land)


def adamw(w, g, m, v, name):
    M, N = w.shape
    tr = _elementwise_rows(M, N * 4)
    c1 = 1.0 - ADAM_B1 ** ADAM_STEP
    c2 = 1.0 - ADAM_B2 ** ADAM_STEP

    def body(w_ref, g_ref, m_ref, v_ref, d_ref, nm_ref, nv_ref):
        g = g_ref[...]
        nm = ADAM_B1 * m_ref[...] + (1.0 - ADAM_B1) * g
        nv = ADAM_B2 * v_ref[...] + (1.0 - ADAM_B2) * (g * g)
        d_ref[...] = -ADAM_LR * ((nm / c1) / (jnp.sqrt(nv / c2) + ADAM_EPS) + ADAM_WD * w_ref[...])
        nm_ref[...] = nm
        nv_ref[...] = nv

    spec = pl.BlockSpec((tr, N), _row(0))
    shp = jax.ShapeDtypeStruct((M, N), F32)
    return pl.pallas_call(
        body, name=name, grid=(M // tr,), in_specs=[spec] * 4, out_specs=[spec] * 3, out_shape=[shp] * 3,
        compiler_params=_params(("parallel",)),
    )(w, g, m, v)


def _coords():
    return lax.axis_index("x"), lax.axis_index("y"), lax.axis_index("c")


def all_gather8(block, name):
    m_per, n = block.shape

    def body(x_ref, out_ref, send_sems, recv_sems, local_sem):
        x, y, c = _coords()
        me, sibling = (x, y, c), (x, y, 1 - c)
        chips = [(1 - x, y), (x, 1 - y), (1 - x, 1 - y)]

        def rows(px, py, pc):
            return out_ref.at[pl.ds((4 * px + 2 * py + pc) * m_per, m_per), :]

        def copy(k, blk, to, src=None):
            return pltpu.make_async_remote_copy(
                src_ref=rows(*blk) if src is None else src, dst_ref=rows(*blk),
                send_sem=send_sems.at[k], recv_sem=recv_sems.at[k], device_id=to, device_id_type=MESH)

        mine = pltpu.make_async_copy(x_ref, rows(*me), local_sem)
        mine.start()
        first = [copy(0, me, sibling, src=x_ref)]
        first += [copy(1 + j, me, (*chip, c), src=x_ref) for j, chip in enumerate(chips)]
        for cp in first:
            cp.start()
        passed = [copy(4 + j, (*chip, c), sibling) for j, chip in enumerate(chips)]
        for j, chip in enumerate(chips):
            copy(1 + j, (*chip, c), me).wait_recv()
            passed[j].start()
        copy(0, sibling, me).wait_recv()
        for j, chip in enumerate(chips):
            copy(4 + j, (*chip, 1 - c), me).wait_recv()
        for cp in first + passed:
            cp.wait_send()
        mine.wait()

    return pl.pallas_call(
        body, name=name, in_specs=[VMEM_SPEC], out_specs=VMEM_SPEC,
        out_shape=jax.ShapeDtypeStruct((N_DEV * m_per, n), block.dtype),
        scratch_shapes=[pltpu.SemaphoreType.DMA((7,)), pltpu.SemaphoreType.DMA((7,)), pltpu.SemaphoreType.DMA],
        compiler_params=_params(),
    )(block)


def _other_chips(x, y):
    return [(1 - x, y), (x, 1 - y), (1 - x, 1 - y)]


HBM_SPEC = pl.BlockSpec(memory_space=pltpu.HBM)
SEM_SPEC = pl.BlockSpec(memory_space=pltpu.SEMAPHORE)
DATAFLOW = pltpu.SideEffectType.DATAFLOW_SIDE_EFFECTING


def _weight_desc(stack_ref, k, land_ref, px, py, c, me_s):
    h = land_ref.shape[1] // 2
    rows = pl.ds(c * h, h)
    return stack_ref.at[k, rows, :], land_ref.at[me_s, rows, :], land_ref.at[2 * px + py, rows, :]


def _grad_desc(psum_ref, k, land_ref, px, py, c, me_s):
    return psum_ref.at[2 * px + py], land_ref.at[me_s], land_ref.at[2 * px + py]


def exchange_start(srcs, lands, units, groups, desc, after, name):
    n_s, n_l, n_g = len(srcs), len(lands), len(groups)

    def body(*refs):
        s_refs, l_refs = refs[:n_s], refs[n_s:n_s + n_l]
        outs = refs[n_s + n_l + 1:]
        sems, token = outs[:2 * n_g], outs[-1]
        x, y, c = _coords()
        me_s = 2 * x + y
        for g, ids in enumerate(groups):
            for i, u in enumerate(ids):
                si, k = units[u]
                for j, (px, py) in enumerate(_other_chips(x, y)):
                    src, dst, _ = desc(s_refs[si], k, l_refs[u], px, py, c, me_s)
                    pltpu.make_async_remote_copy(
                        src_ref=src, dst_ref=dst, send_sem=sems[2 * g].at[3 * i + j],
                        recv_sem=sems[2 * g + 1].at[3 * i + j], device_id=(px, py, c), device_id_type=MESH).start()
        token[...] = jnp.zeros_like(token)

    arrs = list(srcs) + list(lands)
    sem_shapes = [pltpu.SemaphoreType.DMA((3 * len(ids),)) for ids in groups for _ in range(2)]
    outs = pl.pallas_call(
        body, name=name,
        in_specs=[HBM_SPEC] * len(arrs) + [ANY],
        out_specs=[SEM_SPEC] * (2 * n_g) + [HBM_SPEC] * len(arrs) + [VMEM_SPEC],
        out_shape=sem_shapes + [pltpu.HBM(a.shape, a.dtype) for a in arrs] + [jax.ShapeDtypeStruct((8, LANES), F32)],
        input_output_aliases={i: 2 * n_g + i for i in range(len(arrs))},
        compiler_params=pltpu.CompilerParams(has_side_effects=DATAFLOW),
    )(*[pltpu.with_memory_space_constraint(a, pltpu.HBM) for a in arrs], after)
    sems = outs[:2 * n_g]
    thru = outs[2 * n_g:2 * n_g + len(arrs)]
    return sems, list(thru[:n_s]), list(thru[n_s:]), outs[-1]


def exchange_wait(srcs, lands, units, send_sem, recv_sem, desc, after, name):
    n_s, n_l = len(srcs), len(lands)

    def body(*refs):
        s_refs, l_refs = refs[:n_s], refs[n_s:n_s + n_l]
        send_sems, recv_sems = refs[n_s + n_l], refs[n_s + n_l + 1]
        x, y, c = _coords()
        me_s = 2 * x + y
        for i, (si, k) in enumerate(units):
            for j, (px, py) in enumerate(_other_chips(x, y)):
                src, _, mine = desc(s_refs[si], k, l_refs[i], px, py, c, me_s)
                cp = pltpu.make_async_remote_copy(
                    src_ref=src, dst_ref=mine, send_sem=send_sems.at[3 * i + j], recv_sem=recv_sems.at[3 * i + j],
                    device_id=(px, py, c), device_id_type=MESH)
                cp.wait_send()
                cp.wait_recv()

    arrs = list(srcs) + list(lands)
    outs = pl.pallas_call(
        body, name=name,
        in_specs=[HBM_SPEC] * len(arrs) + [SEM_SPEC, SEM_SPEC, ANY],
        out_specs=[HBM_SPEC] * len(arrs),
        out_shape=[pltpu.HBM(a.shape, a.dtype) for a in arrs],
        input_output_aliases={i: i for i in range(len(arrs))},
        compiler_params=pltpu.CompilerParams(has_side_effects=DATAFLOW),
    )(*arrs, send_sem, recv_sem, after)
    return list(outs[:n_s]), list(outs[n_s:])


def sibling_fill(lands, name):
    n_u = len(lands)

    def body(*refs):
        ins, outs = refs[:n_u], refs[n_u:2 * n_u]
        send_sems, recv_sems = refs[2 * n_u:]
        x, y, c = _coords()
        sends = []
        for u in range(n_u):
            h = ins[u].shape[1] // 2
            for j, (px, py) in enumerate(_other_chips(x, y)):
                part = (2 * px + py, pl.ds(c * h, h), slice(None))
                cp = pltpu.make_async_remote_copy(
                    src_ref=ins[u].at[part], dst_ref=outs[u].at[part], send_sem=send_sems.at[3 * u + j],
                    recv_sem=recv_sems.at[3 * u + j], device_id=(x, y, 1 - c), device_id_type=MESH)
                cp.start()
                sends.append(cp)
        for u in range(n_u):
            h = ins[u].shape[1] // 2
            for j, (px, py) in enumerate(_other_chips(x, y)):
                theirs = (2 * px + py, pl.ds((1 - c) * h, h), slice(None))
                pltpu.make_async_remote_copy(
                    src_ref=ins[u].at[theirs], dst_ref=outs[u].at[theirs], send_sem=send_sems.at[3 * u + j],
                    recv_sem=recv_sems.at[3 * u + j], device_id=(x, y, 1 - c), device_id_type=MESH).wait_recv()
        for cp in sends:
            cp.wait_send()

    return pl.pallas_call(
        body, name=name, in_specs=[ANY] * n_u, out_specs=[ANY] * n_u,
        out_shape=[jax.ShapeDtypeStruct(a.shape, a.dtype) for a in lands],
        input_output_aliases={i: i for i in range(n_u)},
        scratch_shapes=[pltpu.SemaphoreType.DMA((3 * n_u,)), pltpu.SemaphoreType.DMA((3 * n_u,))],
        compiler_params=_params(),
    )(*lands)


def grad_pair_exchange(grads, name):
    n_u = len(grads)

    def body(*refs):
        ins, outs = refs[:n_u], refs[n_u:2 * n_u]
        send_sems, recv_sems = refs[2 * n_u:]
        x, y, c = _coords()
        cps = []
        for u in range(n_u):
            half = ins[u].shape[1] // 2
            cp = pltpu.make_async_remote_copy(
                src_ref=ins[u].at[:, pl.ds((1 - c) * half, half), :], dst_ref=outs[u],
                send_sem=send_sems.at[u], recv_sem=recv_sems.at[u], device_id=(x, y, 1 - c), device_id_type=MESH)
            cp.start()
            cps.append(cp)
        for cp in cps:
            cp.wait()

    out_shape = [jax.ShapeDtypeStruct((g.shape[0], g.shape[1] // 2, g.shape[2]), g.dtype) for g in grads]
    return pl.pallas_call(
        body, name=name, in_specs=[ANY] * n_u, out_specs=[ANY] * n_u, out_shape=out_shape,
        scratch_shapes=[pltpu.SemaphoreType.DMA((n_u,)), pltpu.SemaphoreType.DMA((n_u,))],
        compiler_params=_params(),
    )(*grads)


def grad_half_swap(halves, name):
    n_u = len(halves)

    def body(*refs):
        ins, outs = refs[:n_u], refs[n_u:2 * n_u]
        send_sems, recv_sems = refs[2 * n_u:]
        x, y, c = _coords()
        cps = []
        for u in range(n_u):
            cp = pltpu.make_async_remote_copy(
                src_ref=ins[u], dst_ref=outs[u], send_sem=send_sems.at[u], recv_sem=recv_sems.at[u],
                device_id=(x, y, 1 - c), device_id_type=MESH)
            cp.start()
            cps.append(cp)
        for cp in cps:
            cp.wait()

    return pl.pallas_call(
        body, name=name, in_specs=[ANY] * n_u, out_specs=[ANY] * n_u,
        out_shape=[jax.ShapeDtypeStruct(h.shape, h.dtype) for h in halves],
        scratch_shapes=[pltpu.SemaphoreType.DMA((n_u,)), pltpu.SemaphoreType.DMA((n_u,))],
        compiler_params=_params(),
    )(*halves)


def _pack_rows(parts):
    flat = jnp.concatenate([p.reshape(-1).astype(F32) for p in parts])
    n = flat.shape[0]
    padded = -(-n // (8 * LANES)) * (8 * LANES)
    return jnp.pad(flat, (0, padded - n)).reshape(-1, LANES)


def _unpack_rows(packed, shapes):
    flat = packed.reshape(-1)
    out, off = [], 0
    for s in shapes:
        size = 1
        for d in s:
            size *= d
        out.append(flat[off:off + size].reshape(s))
        off += size
    return out


def _shard_last(full, s_me):
    n = full.shape[-1] // N_CHIP
    return lax.dynamic_slice_in_dim(full, s_me * n, n, axis=full.ndim - 1)


def _unshard_last(g):
    moved = jnp.moveaxis(g, 0, -2)
    return moved.reshape(moved.shape[:-2] + (moved.shape[-2] * moved.shape[-1],))


def kernel(x, c, w_ada, b_ada, ln_g, ln_b, ffn_gu, ffn_down, gmlp_w_in, gmlp_b_in, gmlp_ln_g, gmlp_ln_b, gmlp_w_s, gmlp_b_s, gmlp_w_out, w_ada_kv, b_ada_kv, w_kv, attn_w_q, attn_rel_bias, attn_w_o, loss_target, m_w_ada, m_b_ada, m_ln_g, m_ln_b, m_ffn_gu, m_ffn_down, m_gmlp_w_in, m_gmlp_b_in, m_gmlp_ln_g, m_gmlp_ln_b, m_gmlp_w_s, m_gmlp_b_s, m_gmlp_w_out, m_w_ada_kv, m_b_ada_kv, m_w_kv, m_attn_w_q, m_attn_rel_bias, m_attn_w_o, v_w_ada, v_b_ada, v_ln_g, v_ln_b, v_ffn_gu, v_ffn_down, v_gmlp_w_in, v_gmlp_b_in, v_gmlp_ln_g, v_gmlp_ln_b, v_gmlp_w_s, v_gmlp_b_s, v_gmlp_w_out, v_w_ada_kv, v_b_ada_kv, v_w_kv, v_attn_w_q, v_attn_rel_bias, v_attn_w_o):
    xi, yi, ci = _coords()
    s_me = 2 * xi + yi
    dev = 4 * xi + 2 * yi + ci

    x0 = x[0]
    tgt = loss_target[0]
    S, D = x0.shape
    L = w_ada.shape[0]
    NA = gmlp_w_in.shape[0]
    NB = attn_w_q.shape[0]
    NH = D // HEAD_DIM
    alpha = (2.0 * L) ** 0.25
    n_ada = w_ada.shape[2]
    n_kv = w_ada_kv.shape[1]

    stack_names = ["ffn_gu", "ffn_down", "gmlp_w_in", "gmlp_w_out", "w_kv", "attn_w_q", "attn_w_o"]
    stack_src = dict(ffn_gu=ffn_gu, ffn_down=ffn_down, gmlp_w_in=gmlp_w_in, gmlp_w_out=gmlp_w_out, w_kv=w_kv[None],
                     attn_w_q=attn_w_q, attn_w_o=attn_w_o)
    stacks = [stack_src[nm].astype(BF16).reshape((-1,) + stack_src[nm].shape[-2:]) for nm in stack_names]
    units = [(si, k) for si, st in enumerate(stacks) for k in range(st.shape[0])]
    unit_of = {(stack_names[si], k): u for u, (si, k) in enumerate(units)}
    weight_groups = [[("ffn_gu", 0), ("ffn_down", 0)],
                     ([("gmlp_w_in", 0), ("gmlp_w_out", 0)] if NA > 0 else [("attn_w_q", 0), ("attn_w_o", 0)])
                     + [("ffn_gu", 1), ("ffn_down", 1)] + ([("w_kv", 0)] if NA == 0 else [])]
    for l in range(1, L):
        names = [("w_kv", 0)] if l == NA else []
        names += [("ffn_gu", 2 * l), ("ffn_down", 2 * l)]
        names += [("gmlp_w_in", l), ("gmlp_w_out", l)] if l < NA else [("attn_w_q", l - NA), ("attn_w_o", l - NA)]
        names += [("ffn_gu", 2 * l + 1), ("ffn_down", 2 * l + 1)]
        weight_groups.append(names)
    weight_groups = [[unit_of[n] for n in names] for names in weight_groups]

    c_all = all_gather8(jnp.broadcast_to(c, (8, D)), "ag_c").reshape(N_DEV, 8, D)[:, 0]
    b_ada_sh = lax.dynamic_slice_in_dim(b_ada, s_me * n_ada, n_ada, axis=1)
    b_kv_sh = lax.dynamic_slice_in_dim(b_ada_kv, s_me * n_kv, n_kv, axis=0)
    mod_part = ada_fwd(c_all, w_ada, b_ada_sh[:, None, :], "ada_fwd")
    mkv_part = ada_fwd(c_all, w_ada_kv[None], b_kv_sh[None, None, :], "ada_kv_fwd")
    part = jnp.concatenate([jnp.transpose(mod_part, (1, 0, 2)).reshape(N_DEV, L * n_ada), mkv_part[0]], axis=1)
    width = part.shape[1]
    pad_w = -(-width // LANES) * LANES - width
    all_part = all_gather8(jnp.pad(part, ((0, 0), (0, pad_w))), "ag_mod").reshape(N_DEV, N_DEV, width + pad_w)
    mine = lax.dynamic_index_in_dim(all_part[0::2], dev, axis=1, keepdims=False)
    mod = jnp.transpose(mine[:, :L * n_ada].reshape(N_CHIP, L, n_ada), (1, 0, 2)).reshape(L, N_MOD, D)
    mkv = mine[:, L * n_ada:width].reshape(2, D)

    def mrow(l, k):
        return mod[l, k][None, :]

    small_shapes = [ln_g.shape, ln_b.shape, gmlp_b_in.shape, gmlp_ln_g.shape, gmlp_ln_b.shape, attn_rel_bias.shape]
    small_pack = _pack_rows([ln_g, ln_b, gmlp_b_in, gmlp_ln_g, gmlp_ln_b, attn_rel_bias])
    small_all = all_gather8(small_pack, "ag_small_params").reshape((N_DEV,) + small_pack.shape)[0::2]
    sm = [_unpack_rows(small_all[s], small_shapes) for s in range(N_CHIP)]
    ln_g_f, ln_b_f, b_in_f, gln_g_f, gln_b_f, rel_f = [
        _unshard_last(jnp.stack([sm[s][i] for s in range(N_CHIP)])) for i in range(len(small_shapes))]

    lands0 = [lax.dynamic_update_slice(lax.empty((N_CHIP,) + stacks[si].shape[1:], BF16), stacks[si][k][None],
                                       (s_me, 0, 0)) for si, k in units]
    gathers_done = jnp.concatenate([mod.reshape(-1)[:LANES], small_all.reshape(-1)[:LANES]])
    w_sems, stacks_t, lands_t, _ = exchange_start(stacks, lands0, units, weight_groups, _weight_desc, gathers_done,
                                                  "weight_send_start")
    wg = {}

    def fetch_weights(g, stacks_now, after):
        ids = weight_groups[g]
        stacks_next, got = exchange_wait(stacks_now, [lands_t[u] for u in ids], [units[u] for u in ids],
                                         w_sems[2 * g], w_sems[2 * g + 1], _weight_desc, after,
                                         "weight_send_wait_%d" % g)
        for u, a in zip(ids, sibling_fill(got, "weight_sibling_fill")):
            wg[u] = a
        return stacks_next

    def W(nm, k):
        return wg[unit_of[(nm, k)]]

    def Wrows(nm, k):
        w4 = W(nm, k)
        return w4.reshape(w4.shape[0] * w4.shape[1], w4.shape[2])

    bst = [jnp.transpose(gmlp_b_s[j]) for j in range(NA)]
    biases = [jnp.transpose(bias_expand(rel_f[j], "bias_expand"), (1, 0, 2)) for j in range(NB)]

    saved = []
    xc = x0
    kpad = vpad = xkv = None
    for l in range(L):
        stacks_t = fetch_weights(0 if l == 0 else l + 1, stacks_t, mod if l == 0 else xc)
        if l == NA:
            xkv = xc
            kv = mod_matmul(xc, mkv[1][None], mkv[0][None], W("w_kv", 0), None, BF16, "kv_proj")
            kpad = jnp.pad(kv[:, :D], ((LEFT_PAD, 0), (0, 0)))
            vpad = jnp.pad(kv[:, D:], ((LEFT_PAD, 0), (0, 0)))
        sv = {}
        for i in (0, 2):
            k = 2 * l + i // 2
            gu = mod_matmul(xc, mrow(l, 3 * i + 1), mrow(l, 3 * i), W("ffn_gu", k), None, BF16, "ffn_up")
            gw = 0.5 * (1.0 + mrow(l, 3 * i + 2))
            xn, xh, rs, yv = matmul_res_ln(gu, Wrows("ffn_down", k), xc, gw, ln_g_f[l, i][None], ln_b_f[l, i][None],
                                           alpha, True, "ffn_down")
            sv[i] = dict(x=xc, gu=gu, xh=xh, rs=rs, y=yv, gw=gw)
            xc = xn
            if i == 0:
                if l == 0:
                    stacks_t = fetch_weights(1, stacks_t, xc)
                gw = 1.0 + mrow(l, 5)
                if l < NA:
                    pre = mod_matmul(xc, mrow(l, 4), mrow(l, 3), W("gmlp_w_in", l), b_in_f[l][None], F32, "gmlp_in")
                    qv = sgu_fwd(pre, gln_g_f[l][None], gln_b_f[l][None], gmlp_w_s[l], bst[l], "sgu_fwd")
                    xn, xh, rs, yv = matmul_res_ln(qv, Wrows("gmlp_w_out", l), xc, gw, ln_g_f[l, 1][None],
                                                   ln_b_f[l, 1][None], alpha, False, "gmlp_out")
                    sv[1] = dict(x=xc, pre=pre, a=qv, xh=xh, rs=rs, y=yv, gw=gw)
                else:
                    j = l - NA
                    qh = mod_matmul(xc, mrow(l, 4), mrow(l, 3), Wrows("attn_w_q", j)[None], None, BF16, "attn_q")
                    ov = attn_fwd(qh, kpad, vpad, biases[j], "attn_fwd")
                    xn, xh, rs, yv = matmul_res_ln(ov, Wrows("attn_w_o", j), xc, gw, ln_g_f[l, 1][None],
                                                   ln_b_f[l, 1][None], alpha, False, "attn_out")
                    sv[1] = dict(x=xc, q=qh, a=ov, xh=xh, rs=rs, y=yv, gw=gw)
                xc = xn
        saved.append(sv)

    dx, lacc = loss_grad(xc, tgt, "loss_grad")
    loss = lax.psum((0.5 / D) * jnp.sum(lacc[0]), ("x", "y", "c"))

    gfull = [None] * len(units)
    dmod = [[None] * N_MOD for _ in range(L)]
    d_ln_g = [[None] * 3 for _ in range(L)]
    d_ln_b = [[None] * 3 for _ in range(L)]
    d_b_in, d_gln_g, d_gln_b, d_ws, d_bs, d_rel = ([None] * NA, [None] * NA, [None] * NA, [None] * NA, [None] * NA,
                                                  [None] * NB)
    dk = jnp.zeros((S, D), F32)
    dv = jnp.zeros((S, D), F32)
    dmkv = None

    made = []

    def put(nm, k, g):
        gfull[unit_of[(nm, k)]] = g.reshape((N_CHIP, -1, g.shape[-1]))
        made.append(unit_of[(nm, k)])

    own_half, sib_half = {}, {}

    core_idx = ci.astype(jnp.int32).reshape(1)
    chip_idx = s_me.astype(jnp.int32).reshape(1)
    n_started = [0]

    def start_grad_exchange(ids, after):
        from_sibling = grad_pair_exchange([gfull[u] for u in ids], "grad_pair_exchange")
        psums = [pair_sum(gfull[u], fs, core_idx, "grad_pair_sum") for u, fs in zip(ids, from_sibling)]
        n = len(ids)
        tag = n_started[0]
        n_started[0] += 1
        sems, ps_t, q_t, token = exchange_start(psums, [lax.empty(p.shape, p.dtype) for p in psums],
                                                [(i, 0) for i in range(n)], [list(range(n))], _grad_desc, after,
                                                "grad_send_start_%d" % tag)
        return dict(ids=ids, tag=tag, sems=sems, ps=ps_t, q=q_t), token

    def finish_grad_exchange(pend, after):
        n = len(pend["ids"])
        ps_t, q = exchange_wait(pend["ps"], pend["q"], [(i, 0) for i in range(n)], pend["sems"][0], pend["sems"][1],
                                _grad_desc, after, "grad_send_wait_%d" % pend["tag"])
        halves = [chip_sum(ps_t[i], q[i], chip_idx, "grad_chip_sum") for i in range(n)]
        for u, h, s in zip(pend["ids"], halves, grad_half_swap(halves, "grad_half_swap")):
            own_half[u], sib_half[u] = h, s

    pending = None
    order_token = jnp.zeros((), F32)
    for l in reversed(range(L)):
        if l == NA - 1:
            dkv = jnp.concatenate([dk, dv], axis=1)
            put("w_kv", 0, wgrad("mod", (xkv, mkv[1][None], mkv[0][None]), dkv, D, dkv.shape[1] // N_CHIP, "kv_wgrad"))
            dx, acc = dgrad_mod(dkv, W("w_kv", 0), dx, xkv, mkv[1][None], "kv_dgrad")
            dmkv = jnp.stack([acc[1], acc[0]])
        sv = saved[l]
        for i in (2, 1, 0):
            t = sv[i]
            wres = 1.0 if i == 1 else 0.5
            dxa, dy, acc1 = ln_res_bwd(dx, t["xh"], t["rs"], ln_g_f[l, i][None] + order_token, t["y"], t["gw"], wres,
                                       alpha, "ln_res_bwd")
            d_ln_g[l][i], d_ln_b[l][i], dmod[l][3 * i + 2] = acc1[0], acc1[1], acc1[2]
            scl, shf = mrow(l, 3 * i + 1), mrow(l, 3 * i)
            if i != 1:
                k = 2 * l + i // 2
                F = t["gu"].shape[1] // 2
                dgu = ffn_act_bwd(dy, Wrows("ffn_down", k), t["gu"], "ffn_act_bwd")
                put("ffn_down", k, wgrad("swiglu", (t["gu"],), dy, F // 2, D, "ffn_down_wgrad"))
                put("ffn_gu", k, wgrad("mod", (t["x"], scl, shf), dgu, D, 2 * F // N_CHIP, "ffn_up_wgrad"))
                dx, acc2 = dgrad_mod(dgu, W("ffn_gu", k), dxa, t["x"], scl, "ffn_up_dgrad")
            elif l < NA:
                dq = matmul_nt(dy, Wrows("gmlp_w_out", l), "gmlp_out_dgrad")
                put("gmlp_w_out", l, wgrad("plain", (t["a"],), dy, t["a"].shape[1], D, "gmlp_out_wgrad"))
                dpre, dws_l, dss, dgl, dbin = sgu_bwd(dq, t["pre"], gln_g_f[l][None], gln_b_f[l][None], gmlp_w_s[l],
                                                      bst[l], "sgu_bwd")
                d_ws[l] = dws_l
                d_bs[l] = jnp.transpose(group_lane_sum(dss, "sgu_bias_grad")[:, :GMLP_GROUPS])
                d_gln_g[l], d_gln_b[l], d_b_in[l] = dgl[0], dgl[1], dbin[0]
                put("gmlp_w_in", l, wgrad("mod", (t["x"], scl, shf), dpre, D, dpre.shape[1] // N_CHIP, "gmlp_in_wgrad"))
                dx, acc2 = dgrad_mod(dpre, W("gmlp_w_in", l), dxa, t["x"], scl, "gmlp_in_dgrad")
            else:
                j = l - NA
                do = matmul_nt(dy, Wrows("attn_w_o", j), "attn_out_dgrad")
                put("attn_w_o", j, wgrad("plain", (t["a"],), dy, D, D, "attn_out_wgrad"))
                dqh, dk, dv, dbias = attn_bwd(t["q"], do, kpad, vpad, biases[j], dk, dv, "attn_bwd")
                d_rel[j] = bias_grad(jnp.transpose(dbias, (1, 0, 2)), "bias_grad")
                put("attn_w_q", j, wgrad("mod", (t["x"], scl, shf), dqh, D, D, "attn_q_wgrad"))
                dx, acc2 = dgrad_mod(dqh, Wrows("attn_w_q", j)[None], dxa, t["x"], scl, "attn_q_dgrad")
            dmod[l][3 * i + 1], dmod[l][3 * i] = acc2[0], acc2[1]
            if (i == 0 and l > 0) or (i == 1 and l == 0):
                started, token = start_grad_exchange(list(made), dx)
                made.clear()
                if pending is not None:
                    finish_grad_exchange(pending, dx)
                pending = started
                order_token = token[0, 0]
    grad_x = dx[None]

    dvec = _pack_rows([jnp.stack([jnp.stack(r) for r in dmod]), dmkv])
    n_dvec = L * N_MOD * D + 2 * D
    dall = all_gather8(dvec, "ag_dmod").reshape(N_DEV, -1, LANES)
    db_all = sum_leading(dall, "ada_bias_grad").reshape(-1)[:n_dvec]
    g_b_ada = db_all[:L * N_MOD * D].reshape(L, N_MOD * D)
    g_b_ada_kv = db_all[L * N_MOD * D:]
    dall2 = dall.reshape(N_DEV, -1)[:, :n_dvec]
    dmod_all = dall2[:, :L * N_MOD * D].reshape(N_DEV, L, N_MOD * D)
    dmod_sh = jnp.transpose(lax.dynamic_slice_in_dim(dmod_all, s_me * n_ada, n_ada, axis=2), (1, 0, 2))
    dmkv_sh = lax.dynamic_slice_in_dim(dall2[:, L * N_MOD * D:], s_me * n_kv, n_kv, axis=1)[None]
    c_all_t = jnp.transpose(c_all)
    g_w_ada = ada_wgrad(c_all_t, dmod_sh, "ada_wgrad")
    g_w_ada_kv = ada_wgrad(c_all_t, dmkv_sh, "ada_kv_wgrad")[0]

    small_g = [jnp.stack([jnp.stack(r) for r in d_ln_g]), jnp.stack([jnp.stack(r) for r in d_ln_b]),
               jnp.stack(d_b_in), jnp.stack(d_gln_g), jnp.stack(d_gln_b), jnp.stack(d_rel), jnp.stack(d_ws),
               jnp.stack(d_bs)]
    sg_shapes = [a.shape for a in small_g]
    sg_pack = _pack_rows(small_g)
    sg_all = all_gather8(sg_pack, "ag_small_grads").reshape(N_DEV, -1, LANES)
    sg_sum = _unpack_rows(sum_leading(sg_all, "small_grad_sum"), sg_shapes)
    g_ln_g, g_ln_b, g_b_in, g_gln_g, g_gln_b, g_rel = [_shard_last(a, s_me) for a in sg_sum[:6]]
    g_ws, g_bs = sg_sum[6], sg_sum[7]

    last, _ = start_grad_exchange(list(made), sg_all)
    finish_grad_exchange(pending, sg_all)

    grads = dict(w_ada=g_w_ada, b_ada=g_b_ada, ln_g=g_ln_g, ln_b=g_ln_b, gmlp_b_in=g_b_in, gmlp_ln_g=g_gln_g,
                 gmlp_ln_b=g_gln_b, gmlp_w_s=g_ws, gmlp_b_s=g_bs, w_ada_kv=g_w_ada_kv, b_ada_kv=g_b_ada_kv,
                 attn_rel_bias=g_rel)
    weights = dict(w_ada=w_ada, b_ada=b_ada, ln_g=ln_g, ln_b=ln_b, ffn_gu=ffn_gu, ffn_down=ffn_down,
                   gmlp_w_in=gmlp_w_in, gmlp_b_in=gmlp_b_in, gmlp_ln_g=gmlp_ln_g, gmlp_ln_b=gmlp_ln_b,
                   gmlp_w_s=gmlp_w_s, gmlp_b_s=gmlp_b_s, gmlp_w_out=gmlp_w_out, w_ada_kv=w_ada_kv,
                   b_ada_kv=b_ada_kv, w_kv=w_kv, attn_w_q=attn_w_q, attn_rel_bias=attn_rel_bias, attn_w_o=attn_w_o)
    ms = dict(w_ada=m_w_ada, b_ada=m_b_ada, ln_g=m_ln_g, ln_b=m_ln_b, ffn_gu=m_ffn_gu, ffn_down=m_ffn_down,
              gmlp_w_in=m_gmlp_w_in, gmlp_b_in=m_gmlp_b_in, gmlp_ln_g=m_gmlp_ln_g, gmlp_ln_b=m_gmlp_ln_b,
              gmlp_w_s=m_gmlp_w_s, gmlp_b_s=m_gmlp_b_s, gmlp_w_out=m_gmlp_w_out, w_ada_kv=m_w_ada_kv,
              b_ada_kv=m_b_ada_kv, w_kv=m_w_kv, attn_w_q=m_attn_w_q, attn_rel_bias=m_attn_rel_bias,
              attn_w_o=m_attn_w_o)
    vs = dict(w_ada=v_w_ada, b_ada=v_b_ada, ln_g=v_ln_g, ln_b=v_ln_b, ffn_gu=v_ffn_gu, ffn_down=v_ffn_down,
              gmlp_w_in=v_gmlp_w_in, gmlp_b_in=v_gmlp_b_in, gmlp_ln_g=v_gmlp_ln_g, gmlp_ln_b=v_gmlp_ln_b,
              gmlp_w_s=v_gmlp_w_s, gmlp_b_s=v_gmlp_b_s, gmlp_w_out=v_gmlp_w_out, w_ada_kv=v_w_ada_kv,
              b_ada_kv=v_b_ada_kv, w_kv=v_w_kv, attn_w_q=v_attn_w_q, attn_rel_bias=v_attn_rel_bias,
              attn_w_o=v_attn_w_o)
    order = ["w_ada", "b_ada", "ln_g", "ln_b", "ffn_gu", "ffn_down", "gmlp_w_in", "gmlp_b_in", "gmlp_ln_g",
             "gmlp_ln_b", "gmlp_w_s", "gmlp_b_s", "gmlp_w_out", "w_ada_kv", "b_ada_kv", "w_kv", "attn_w_q",
             "attn_rel_bias", "attn_w_o"]
    big_names = ["w_ada", "w_ada_kv"] + stack_names
    small_names = [nm for nm in order if nm not in big_names]
    delta, new_m, new_v = {}, {}, {}

    def adamw_big(nm):
        shp = weights[nm].shape
        two_d = (-1, shp[-1])
        d, a, b = adamw(weights[nm].reshape(two_d), grads[nm].reshape(two_d), ms[nm].reshape(two_d),
                        vs[nm].reshape(two_d), "adamw")
        delta[nm], new_m[nm], new_v[nm] = d.reshape(shp), a.reshape(shp), b.reshape(shp)

    adamw_big("w_ada")
    adamw_big("w_ada_kv")
    shapes = [weights[nm].shape for nm in small_names]
    d, a, b = adamw(_pack_rows([weights[nm] for nm in small_names]), _pack_rows([grads[nm] for nm in small_names]),
                    _pack_rows([ms[nm] for nm in small_names]), _pack_rows([vs[nm] for nm in small_names]),
                    "adamw_small")
    for nm, dd, aa, bb in zip(small_names, _unpack_rows(d, shapes), _unpack_rows(a, shapes), _unpack_rows(b, shapes)):
        delta[nm], new_m[nm], new_v[nm] = dd, aa, bb

    def full_grad(u):
        lo = jnp.where(ci == 0, own_half[u], sib_half[u])
        hi = jnp.where(ci == 0, sib_half[u], own_half[u])
        return jnp.concatenate([lo, hi], axis=0)

    def adamw_stack(nm):
        si = stack_names.index(nm)
        g = jnp.stack([full_grad(unit_of[(nm, k)]) for k in range(stacks[si].shape[0])])
        grads[nm] = g.reshape(weights[nm].shape)
        adamw_big(nm)

    late = [stack_names[units[u][0]] for u in last["ids"]]
    early = [nm for nm in stack_names if nm not in late]
    for nm in early:
        adamw_stack(nm)
    finish_grad_exchange(last, delta[early[-1]])
    for nm in stack_names:
        if nm in late:
            adamw_stack(nm)

    return (loss, grad_x, *[grads[nm] for nm in order], *[delta[nm] for nm in order],
            *[new_m[nm] for nm in order], *[new_v[nm] for nm in order])
```

```python
import functools

import jax
import jax.numpy as jnp
from jax import lax
from jax.experimental import pallas as pl
from jax.experimental.pallas import tpu as pltpu

F32 = jnp.float32
BF16 = jnp.bfloat16
MESH = pl.DeviceIdType.MESH
HIGHEST = lax.Precision.HIGHEST

CHUNK = 64
GMLP_WINDOW = 128
GMLP_GROUPS = 8
HEAD_DIM = 64
LEFT_CHUNKS = 8
BAND = (LEFT_CHUNKS + 1) * CHUNK
LEFT_PAD = LEFT_CHUNKS * CHUNK
MAX_REL = 4 * CHUNK
N_REL = (CHUNK - 1) + MAX_REL + 1
LN_EPS = 1e-5
N_MOD = 9
N_DEV = 8
N_CHIP = 4

ADAM_LR = 0.001
ADAM_B1 = 0.9
ADAM_B2 = 0.999
ADAM_EPS = 1e-08
ADAM_WD = 0.01
ADAM_STEP = 10

LANES = 128
ROW_TILE = 256
WGRAD_ROWS = 512
ATTN_CHUNKS_PER_STEP = 4
VMEM_LIMIT_MB = 56

NT = (((1,), (1,)), ((), ()))
TN = (((0,), (0,)), ((), ()))

ANY = pl.BlockSpec(memory_space=pl.ANY)
VMEM_SPEC = pl.BlockSpec(memory_space=pltpu.VMEM)


def _params(semantics=None):
    kw = dict(vmem_limit_bytes=VMEM_LIMIT_MB * 1024 * 1024)
    if semantics is not None:
        kw["dimension_semantics"] = semantics
    return pltpu.CompilerParams(**kw)


def _sigmoid(v):
    return 0.5 * (1.0 + jnp.tanh(0.5 * v))


def _gelu(v):
    return 0.5 * v * (1.0 + lax.erf(v * (2.0 ** -0.5)))


def _gelu_grad(v):
    return 0.5 * (1.0 + lax.erf(v * (2.0 ** -0.5))) + v * jnp.exp(-0.5 * v * v) * ((2.0 * jnp.pi) ** -0.5)


def _row(m):
    return lambda i: (i, 0)


def _fixed2(i):
    return (0, 0)


def _fixed3(i):
    return (0, 0, 0)


def mod_matmul(x, scl, shift, w, bias, out_dtype, name):
    S, D = x.shape
    NS, _, n = w.shape
    tm = min(ROW_TILE, S)
    has_bias = bias is not None

    def body(*refs):
        if has_bias:
            x_ref, scl_ref, sh_ref, w_ref, b_ref, o_ref = refs
        else:
            x_ref, scl_ref, sh_ref, w_ref, o_ref = refs
        h = (x_ref[...] * (1.0 + scl_ref[...]) + sh_ref[...]).astype(BF16)
        for s in range(NS):
            acc = jnp.dot(h, w_ref[s], preferred_element_type=F32)
            if has_bias:
                acc = acc + b_ref[:, s * n:(s + 1) * n]
            o_ref[:, s * n:(s + 1) * n] = acc.astype(out_dtype)

    in_specs = [pl.BlockSpec((tm, D), _row(0)), pl.BlockSpec((1, D), _fixed2), pl.BlockSpec((1, D), _fixed2),
                pl.BlockSpec((NS, D, n), _fixed3)]
    args = [x, scl, shift, w]
    if has_bias:
        in_specs.append(pl.BlockSpec((1, NS * n), _fixed2))
        args.append(bias)
    return pl.pallas_call(
        body, name=name, grid=(S // tm,), in_specs=in_specs,
        out_specs=pl.BlockSpec((tm, NS * n), _row(0)),
        out_shape=jax.ShapeDtypeStruct((S, NS * n), out_dtype),
        compiler_params=_params(("parallel",)),
    )(*args)


def matmul_res_ln(a, w, x, gw, lg, lb, alpha, swiglu, name):
    S, D = x.shape
    K = w.shape[0]
    tm = min(ROW_TILE, S)
    ka = a.shape[1]

    def body(a_ref, w_ref, x_ref, gw_ref, lg_ref, lb_ref, xn_ref, xh_ref, rs_ref, y_ref):
        if swiglu:
            g = a_ref[:, :K].astype(F32)
            u = a_ref[:, K:].astype(F32)
            act = (g * _sigmoid(g) * u).astype(BF16)
        else:
            act = a_ref[...].astype(BF16)
        y = jnp.dot(act, w_ref[...], preferred_element_type=F32)
        z = alpha * x_ref[...] + gw_ref[...] * y
        mu = jnp.mean(z, axis=-1, keepdims=True)
        zc = z - mu
        var = jnp.mean(zc * zc, axis=-1, keepdims=True)
        rstd = lax.rsqrt(var + LN_EPS)
        xhat = zc * rstd
        xn_ref[...] = xhat * lg_ref[...] + lb_ref[...]
        xh_ref[...] = xhat
        rs_ref[...] = rstd
        y_ref[...] = y.astype(BF16)

    vec = pl.BlockSpec((1, D), _fixed2)
    return pl.pallas_call(
        body, name=name, grid=(S // tm,),
        in_specs=[pl.BlockSpec((tm, ka), _row(0)), pl.BlockSpec((K, D), _fixed2), pl.BlockSpec((tm, D), _row(0)),
                  vec, vec, vec],
        out_specs=[pl.BlockSpec((tm, D), _row(0)), pl.BlockSpec((tm, D), _row(0)), pl.BlockSpec((tm, 1), _row(0)),
                   pl.BlockSpec((tm, D), _row(0))],
        out_shape=[jax.ShapeDtypeStruct((S, D), F32), jax.ShapeDtypeStruct((S, D), F32),
                   jax.ShapeDtypeStruct((S, 1), F32), jax.ShapeDtypeStruct((S, D), BF16)],
        compiler_params=_params(("parallel",)),
    )(a, w, x, gw, lg, lb)


def ln_res_bwd(dxn, xhat, rstd, lg, y, gw, wres, alpha, name):
    S, D = dxn.shape
    tm = min(ROW_TILE, S)

    def body(dxn_ref, xh_ref, rs_ref, lg_ref, y_ref, gw_ref, dxa_ref, dy_ref, acc_ref):
        @pl.when(pl.program_id(0) == 0)
        def _():
            acc_ref[...] = jnp.zeros_like(acc_ref)

        d = dxn_ref[...]
        xh = xh_ref[...]
        dxh = d * lg_ref[...]
        m1 = jnp.mean(dxh, axis=-1, keepdims=True)
        m2 = jnp.mean(dxh * xh, axis=-1, keepdims=True)
        dz = rs_ref[...] * (dxh - m1 - xh * m2)
        dxa_ref[...] = alpha * dz
        dy_ref[...] = (gw_ref[...] * dz).astype(BF16)
        acc_ref[0:1, :] += jnp.sum(d * xh, axis=0, keepdims=True)
        acc_ref[1:2, :] += jnp.sum(d, axis=0, keepdims=True)
        acc_ref[2:3, :] += jnp.sum((wres * dz) * y_ref[...].astype(F32), axis=0, keepdims=True)

    vec = pl.BlockSpec((1, D), _fixed2)
    tile = pl.BlockSpec((tm, D), _row(0))
    return pl.pallas_call(
        body, name=name, grid=(S // tm,),
        in_specs=[tile, tile, pl.BlockSpec((tm, 1), _row(0)), vec, tile, vec],
        out_specs=[tile, tile, pl.BlockSpec((8, D), _fixed2)],
        out_shape=[jax.ShapeDtypeStruct((S, D), F32), jax.ShapeDtypeStruct((S, D), BF16),
                   jax.ShapeDtypeStruct((8, D), F32)],
        compiler_params=_params(("arbitrary",)),
    )(dxn, xhat, rstd, lg, y, gw)


def ffn_act_bwd(dy, wd, gu, name):
    S, D = dy.shape
    K = wd.shape[0]
    tm = min(ROW_TILE, S)

    def body(dy_ref, wd_ref, gu_ref, o_ref):
        da = lax.dot_general(dy_ref[...], wd_ref[...], NT, preferred_element_type=F32)
        g = gu_ref[:, :K].astype(F32)
        u = gu_ref[:, K:].astype(F32)
        sg = _sigmoid(g)
        o_ref[:, :K] = (da * u * (sg * (1.0 + g * (1.0 - sg)))).astype(BF16)
        o_ref[:, K:] = (da * (g * sg)).astype(BF16)

    return pl.pallas_call(
        body, name=name, grid=(S // tm,),
        in_specs=[pl.BlockSpec((tm, D), _row(0)), pl.BlockSpec((K, D), _fixed2), pl.BlockSpec((tm, 2 * K), _row(0))],
        out_specs=pl.BlockSpec((tm, 2 * K), _row(0)),
        out_shape=jax.ShapeDtypeStruct((S, 2 * K), BF16),
        compiler_params=_params(("parallel",)),
    )(dy, wd, gu)


def matmul_nt(a, w, name):
    S, D = a.shape
    K = w.shape[0]
    tm = min(ROW_TILE, S)

    def body(a_ref, w_ref, o_ref):
        o_ref[...] = lax.dot_general(a_ref[...], w_ref[...], NT, preferred_element_type=F32).astype(BF16)

    return pl.pallas_call(
        body, name=name, grid=(S // tm,),
        in_specs=[pl.BlockSpec((tm, D), _row(0)), pl.BlockSpec((K, D), _fixed2)],
        out_specs=pl.BlockSpec((tm, K), _row(0)),
        out_shape=jax.ShapeDtypeStruct((S, K), BF16),
        compiler_params=_params(("parallel",)),
    )(a, w)


def dgrad_mod(dpre, w, dxa, xin, scl, name):
    S, D = xin.shape
    NS, _, n = w.shape
    tm = min(ROW_TILE, S)

    def body(dp_ref, w_ref, dxa_ref, xin_ref, scl_ref, dx_ref, acc_ref):
        @pl.when(pl.program_id(0) == 0)
        def _():
            acc_ref[...] = jnp.zeros_like(acc_ref)

        dh = jnp.zeros((tm, D), F32)
        for s in range(NS):
            dh = dh + lax.dot_general(dp_ref[:, s * n:(s + 1) * n].astype(BF16), w_ref[s], NT,
                                      preferred_element_type=F32)
        dx_ref[...] = dxa_ref[...] + dh * (1.0 + scl_ref[...])
        acc_ref[0:1, :] += jnp.sum(dh * xin_ref[...], axis=0, keepdims=True)
        acc_ref[1:2, :] += jnp.sum(dh, axis=0, keepdims=True)

    tile = pl.BlockSpec((tm, D), _row(0))
    return pl.pallas_call(
        body, name=name, grid=(S // tm,),
        in_specs=[pl.BlockSpec((tm, NS * n), _row(0)), pl.BlockSpec((NS, D, n), _fixed3), tile, tile,
                  pl.BlockSpec((1, D), _fixed2)],
        out_specs=[tile, pl.BlockSpec((8, D), _fixed2)],
        out_shape=[jax.ShapeDtypeStruct((S, D), F32), jax.ShapeDtypeStruct((8, D), F32)],
        compiler_params=_params(("arbitrary",)),
    )(dpre, w, dxa, xin, scl)


def wgrad(kind, a_args, b, kb, nb, name):
    S, N = b.shape
    if kind == "swiglu":
        K = a_args[0].shape[1] // 2
    else:
        K = a_args[0].shape[1]
    NBk, KB = N // nb, K // kb
    ts = min(WGRAD_ROWS, S)
    nsteps = S // ts

    def body(*refs):
        acc_ref = refs[-1]
        o_ref = refs[-2]
        b_ref = refs[-3]
        si = pl.program_id(2)

        @pl.when(si == 0)
        def _():
            acc_ref[...] = jnp.zeros_like(acc_ref)

        if kind == "mod":
            x_ref, scl_ref, sh_ref = refs[:3]
            a = (x_ref[...] * (1.0 + scl_ref[...]) + sh_ref[...]).astype(BF16)
        elif kind == "swiglu":
            g = refs[0][...].astype(F32)
            u = refs[1][...].astype(F32)
            a = (g * _sigmoid(g) * u).astype(BF16)
        else:
            a = refs[0][...].astype(BF16)
        acc_ref[...] += lax.dot_general(a, b_ref[...].astype(BF16), TN, preferred_element_type=F32)

        @pl.when(si == nsteps - 1)
        def _():
            o_ref[...] = acc_ref[...].astype(BF16)

    a_tile = pl.BlockSpec((ts, kb), lambda j, k, s: (s, k))
    if kind == "mod":
        vec = pl.BlockSpec((1, kb), lambda j, k, s: (0, k))
        in_specs = [a_tile, vec, vec]
        args = list(a_args)
    elif kind == "swiglu":
        in_specs = [a_tile, pl.BlockSpec((ts, kb), lambda j, k, s: (s, k + KB))]
        args = [a_args[0], a_args[0]]
    else:
        in_specs = [a_tile]
        args = list(a_args)
    in_specs.append(pl.BlockSpec((ts, nb), lambda j, k, s: (s, j)))
    args.append(b)
    return pl.pallas_call(
        body, name=name, grid=(NBk, KB, nsteps), in_specs=in_specs,
        out_specs=pl.BlockSpec((None, kb, nb), lambda j, k, s: (j, k, 0)),
        out_shape=jax.ShapeDtypeStruct((NBk, K, nb), BF16),
        scratch_shapes=[pltpu.VMEM((kb, nb), F32)],
        compiler_params=_params(("parallel", "parallel", "arbitrary")),
    )(*args)


def _window_mask():
    t = lax.broadcasted_iota(jnp.int32, (GMLP_WINDOW, GMLP_WINDOW), 0)
    s = lax.broadcasted_iota(jnp.int32, (GMLP_WINDOW, GMLP_WINDOW), 1)
    return ((s // CHUNK) <= (t // CHUNK)).astype(F32)


def sgu_fwd(pre, glg, glb, ws, bst, name):
    S, H2 = pre.shape
    H = H2 // 2
    W, G = GMLP_WINDOW, GMLP_GROUPS
    gd = H // G
    tm = min(ROW_TILE, S)

    def body(pre_ref, glg_ref, glb_ref, ws_ref, bst_ref, q_ref):
        u = _gelu(pre_ref[:, :H])
        v = _gelu(pre_ref[:, H:])
        mu = jnp.mean(v, axis=-1, keepdims=True)
        vc = v - mu
        var = jnp.mean(vc * vc, axis=-1, keepdims=True)
        vn = ((vc * lax.rsqrt(var + LN_EPS)) * glg_ref[...] + glb_ref[...]).astype(BF16)
        mask = _window_mask()
        for g in range(G):
            wsg = (ws_ref[g] * mask).astype(BF16)
            bcol = bst_ref[:, g:g + 1]
            for wi in range(tm // W):
                rows = slice(wi * W, (wi + 1) * W)
                cols = slice(g * gd, (g + 1) * gd)
                s = jnp.dot(wsg, vn[rows, cols], preferred_element_type=F32) + bcol
                q_ref[rows, cols] = (u[rows, cols] * s).astype(BF16)

    return pl.pallas_call(
        body, name=name, grid=(S // tm,),
        in_specs=[pl.BlockSpec((tm, H2), _row(0)), pl.BlockSpec((1, H), _fixed2), pl.BlockSpec((1, H), _fixed2),
                  pl.BlockSpec((G, W, W), _fixed3), pl.BlockSpec((W, G), _fixed2)],
        out_specs=pl.BlockSpec((tm, H), _row(0)),
        out_shape=jax.ShapeDtypeStruct((S, H), BF16),
        compiler_params=_params(("parallel",)),
    )(pre, glg, glb, ws, bst)


def sgu_bwd(dq, pre, glg, glb, ws, bst, name):
    S, H2 = pre.shape
    H = H2 // 2
    W, G = GMLP_WINDOW, GMLP_GROUPS
    gd = H // G
    tm = min(ROW_TILE, S)

    def body(dq_ref, pre_ref, glg_ref, glb_ref, ws_ref, bst_ref,
             dpre_ref, dws_ref, dss_ref, dgl_ref, dbin_ref, du_s, dvn_s):
        @pl.when(pl.program_id(0) == 0)
        def _():
            dws_ref[...] = jnp.zeros_like(dws_ref)
            dss_ref[...] = jnp.zeros_like(dss_ref)
            dgl_ref[...] = jnp.zeros_like(dgl_ref)
            dbin_ref[...] = jnp.zeros_like(dbin_ref)

        pu = pre_ref[:, :H]
        pv = pre_ref[:, H:]
        u = _gelu(pu)
        v = _gelu(pv)
        mu = jnp.mean(v, axis=-1, keepdims=True)
        vc = v - mu
        var = jnp.mean(vc * vc, axis=-1, keepdims=True)
        rstd = lax.rsqrt(var + LN_EPS)
        vhat = vc * rstd
        vn = (vhat * glg_ref[...] + glb_ref[...]).astype(BF16)
        mask = _window_mask()
        for g in range(G):
            wsg = (ws_ref[g] * mask).astype(BF16)
            bcol = bst_ref[:, g:g + 1]
            cols = slice(g * gd, (g + 1) * gd)
            for wi in range(tm // W):
                rows = slice(wi * W, (wi + 1) * W)
                vblk = vn[rows, cols]
                s = jnp.dot(wsg, vblk, preferred_element_type=F32) + bcol
                dqb = dq_ref[rows, cols].astype(F32)
                du_s[rows, cols] = dqb * s
                ds = dqb * u[rows, cols]
                dss_ref[:, cols] += ds
                dsb = ds.astype(BF16)
                dvn_s[rows, cols] = lax.dot_general(wsg, dsb, TN, preferred_element_type=F32)
                dws_ref[g] += lax.dot_general(dsb, vblk, NT, preferred_element_type=F32) * mask
        dvn = dvn_s[...]
        dgl_ref[0:1, :] += jnp.sum(dvn * vhat, axis=0, keepdims=True)
        dgl_ref[1:2, :] += jnp.sum(dvn, axis=0, keepdims=True)
        dvh = dvn * glg_ref[...]
        m1 = jnp.mean(dvh, axis=-1, keepdims=True)
        m2 = jnp.mean(dvh * vhat, axis=-1, keepdims=True)
        dv = rstd * (dvh - m1 - vhat * m2)
        dpu = du_s[...] * _gelu_grad(pu)
        dpv = dv * _gelu_grad(pv)
        dbin_ref[0:1, :H] += jnp.sum(dpu, axis=0, keepdims=True)
        dbin_ref[0:1, H:] += jnp.sum(dpv, axis=0, keepdims=True)
        dpre_ref[:, :H] = dpu.astype(BF16)
        dpre_ref[:, H:] = dpv.astype(BF16)

    return pl.pallas_call(
        body, name=name, grid=(S // tm,),
        in_specs=[pl.BlockSpec((tm, H), _row(0)), pl.BlockSpec((tm, H2), _row(0)), pl.BlockSpec((1, H), _fixed2),
                  pl.BlockSpec((1, H), _fixed2), pl.BlockSpec((G, W, W), _fixed3), pl.BlockSpec((W, G), _fixed2)],
        out_specs=[pl.BlockSpec((tm, H2), _row(0)), pl.BlockSpec((G, W, W), _fixed3), pl.BlockSpec((W, H), _fixed2),
                   pl.BlockSpec((8, H), _fixed2), pl.BlockSpec((8, H2), _fixed2)],
        out_shape=[jax.ShapeDtypeStruct((S, H2), BF16), jax.ShapeDtypeStruct((G, W, W), F32),
                   jax.ShapeDtypeStruct((W, H), F32), jax.ShapeDtypeStruct((8, H), F32),
                   jax.ShapeDtypeStruct((8, H2), F32)],
        scratch_shapes=[pltpu.VMEM((tm, H), F32), pltpu.VMEM((tm, H), F32)],
        compiler_params=_params(("arbitrary",)),
    )(dq, pre, glg, glb, ws, bst)


def group_lane_sum(dss, name):
    W, H = dss.shape
    gd = H // GMLP_GROUPS

    def body(d_ref, o_ref):
        j = lax.broadcasted_iota(jnp.int32, (H, LANES), 0)
        g = lax.broadcasted_iota(jnp.int32, (H, LANES), 1)
        ind = ((j // gd) == g).astype(F32)
        o_ref[...] = jnp.dot(d_ref[...], ind, preferred_element_type=F32, precision=HIGHEST)

    return pl.pallas_call(
        body, name=name, in_specs=[VMEM_SPEC], out_specs=VMEM_SPEC,
        out_shape=jax.ShapeDtypeStruct((W, LANES), F32), compiler_params=_params(),
    )(dss)


def _attn_load(j, cps, q_ref, k_ref, v_ref):
    r = lax.broadcasted_iota(jnp.int32, (CHUNK, BAND), 1)
    chunks = []
    for cc in range(cps):
        start = pl.multiple_of((j * cps + cc) * CHUNK, CHUNK)
        chunks.append((q_ref[cc * CHUNK:(cc + 1) * CHUNK, :], k_ref[pl.ds(start, BAND), :],
                       v_ref[pl.ds(start, BAND), :], (r + start) >= LEFT_PAD))
    return chunks


def _attn_probs(chunks, b_ref, sels, scale):
    qms = [[jnp.where(sel, q2, jnp.zeros_like(q2)) for sel in sels] for q2, _, _, _ in chunks]
    raw = [[lax.dot_general(qm, k2, NT, preferred_element_type=F32) for qm in qms[cc]]
           for cc, (_, k2, _, _) in enumerate(chunks)]
    probs = []
    for cc, (_, _, _, valid) in enumerate(chunks):
        row = []
        for sub in range(2):
            s = jnp.where(valid, raw[cc][sub] * scale + b_ref[sub], -jnp.inf)
            e = jnp.exp(s - jnp.max(s, axis=-1, keepdims=True))
            row.append(e / jnp.sum(e, axis=-1, keepdims=True))
        probs.append(row)
    return qms, probs


def attn_fwd(q, kpad, vpad, bias, name):
    S, D = q.shape
    HP = D // LANES
    cps = min(ATTN_CHUNKS_PER_STEP, S // CHUNK)
    tq = cps * CHUNK
    scale = HEAD_DIM ** -0.5

    def body(q_ref, k_ref, v_ref, b_ref, o_ref):
        sel0 = lax.broadcasted_iota(jnp.int32, (CHUNK, LANES), 1) < HEAD_DIM
        chunks = _attn_load(pl.program_id(1), cps, q_ref, k_ref, v_ref)
        _, probs = _attn_probs(chunks, b_ref, (sel0, jnp.logical_not(sel0)), scale)
        outs = [[jnp.dot(probs[cc][sub].astype(BF16), v2, preferred_element_type=F32) for sub in range(2)]
                for cc, (_, _, v2, _) in enumerate(chunks)]
        o_ref[...] = jnp.concatenate([jnp.where(sel0, o[0], o[1]) for o in outs], axis=0).astype(BF16)

    kv_spec = pl.BlockSpec((S + LEFT_PAD, LANES), lambda h, j: (0, h))
    return pl.pallas_call(
        body, name=name, grid=(HP, S // tq),
        in_specs=[pl.BlockSpec((tq, LANES), lambda h, j: (j, h)), kv_spec, kv_spec,
                  pl.BlockSpec((2, CHUNK, BAND), lambda h, j: (h, 0, 0))],
        out_specs=pl.BlockSpec((tq, LANES), lambda h, j: (j, h)),
        out_shape=jax.ShapeDtypeStruct((S, D), BF16),
        compiler_params=_params(("parallel", "parallel")),
    )(q, kpad, vpad, bias)


def attn_bwd(q, do, kpad, vpad, bias, dk_in, dv_in, name):
    S, D = q.shape
    HP = D // LANES
    NH = 2 * HP
    cps = min(ATTN_CHUNKS_PER_STEP, S // CHUNK)
    tq = cps * CHUNK
    nj = S // tq
    scale = HEAD_DIM ** -0.5

    def body(q_ref, do_ref, k_ref, v_ref, b_ref, dki_ref, dvi_ref, dq_ref, dk_ref, dv_ref, db_ref, dk_acc, dv_acc):
        j = pl.program_id(1)

        @pl.when(j == 0)
        def _():
            dk_acc[:LEFT_PAD, :] = jnp.zeros((LEFT_PAD, LANES), F32)
            dv_acc[:LEFT_PAD, :] = jnp.zeros((LEFT_PAD, LANES), F32)
            dk_acc[LEFT_PAD:, :] = dki_ref[...]
            dv_acc[LEFT_PAD:, :] = dvi_ref[...]
            db_ref[...] = jnp.zeros_like(db_ref)

        sel0 = lax.broadcasted_iota(jnp.int32, (CHUNK, LANES), 1) < HEAD_DIM
        sels = (sel0, jnp.logical_not(sel0))
        chunks = _attn_load(j, cps, q_ref, k_ref, v_ref)
        pairs = [(cc, sub) for cc in range(cps) for sub in range(2)]
        qms, probs = _attn_probs(chunks, b_ref, sels, scale)
        doms = [[jnp.where(sel, do_ref[cc * CHUNK:(cc + 1) * CHUNK, :], jnp.zeros((CHUNK, LANES), BF16))
                 for sel in sels] for cc in range(cps)]
        dps = {(cc, sub): lax.dot_general(doms[cc][sub], chunks[cc][2], NT, preferred_element_type=F32)
               for cc, sub in pairs}
        dss = {}
        for cc, sub in pairs:
            p = probs[cc][sub]
            dss[cc, sub] = p * (dps[cc, sub] - jnp.sum(dps[cc, sub] * p, axis=-1, keepdims=True))
        dsb = {key: ds.astype(BF16) for key, ds in dss.items()}
        dqs = {(cc, sub): jnp.dot(dsb[cc, sub], chunks[cc][1], preferred_element_type=F32) * scale
               for cc, sub in pairs}
        dks = {(cc, sub): lax.dot_general(dsb[cc, sub], qms[cc][sub], TN, preferred_element_type=F32) * scale
               for cc, sub in pairs}
        dvs = {(cc, sub): lax.dot_general(probs[cc][sub].astype(BF16), doms[cc][sub], TN,
                                          preferred_element_type=F32) for cc, sub in pairs}
        dq_ref[...] = jnp.concatenate([jnp.where(sel0, dqs[cc, 0], dqs[cc, 1]) for cc in range(cps)],
                                      axis=0).astype(BF16)
        for sub in range(2):
            total = dss[0, sub]
            for cc in range(1, cps):
                total = total + dss[cc, sub]
            db_ref[sub] += total
        dk_parts = [dks[cc, 0] + dks[cc, 1] for cc in range(cps)]
        dv_parts = [dvs[cc, 0] + dvs[cc, 1] for cc in range(cps)]

        def window(parts):
            blocks = []
            for rb in range(cps - 1 + BAND // CHUNK):
                acc = None
                for cc in range(cps):
                    b = rb - cc
                    if 0 <= b < BAND // CHUNK:
                        piece = parts[cc][b * CHUNK:(b + 1) * CHUNK, :]
                        acc = piece if acc is None else acc + piece
                blocks.append(acc)
            return jnp.concatenate(blocks, axis=0)

        span = pl.ds(pl.multiple_of(j * cps * CHUNK, CHUNK), (cps - 1) * CHUNK + BAND)
        dk_acc[span, :] += window(dk_parts)
        dv_acc[span, :] += window(dv_parts)

        @pl.when(j == nj - 1)
        def _():
            dk_ref[...] = dk_acc[LEFT_PAD:, :]
            dv_ref[...] = dv_acc[LEFT_PAD:, :]

    q_spec = pl.BlockSpec((tq, LANES), lambda h, j: (j, h))
    kv_spec = pl.BlockSpec((S + LEFT_PAD, LANES), lambda h, j: (0, h))
    col_spec = pl.BlockSpec((S, LANES), lambda h, j: (0, h))
    b_spec = pl.BlockSpec((2, CHUNK, BAND), lambda h, j: (h, 0, 0))
    return pl.pallas_call(
        body, name=name, grid=(HP, nj),
        in_specs=[q_spec, q_spec, kv_spec, kv_spec, b_spec, col_spec, col_spec],
        out_specs=[q_spec, col_spec, col_spec, b_spec],
        out_shape=[jax.ShapeDtypeStruct((S, D), BF16), jax.ShapeDtypeStruct((S, D), F32),
                   jax.ShapeDtypeStruct((S, D), F32), jax.ShapeDtypeStruct((NH, CHUNK, BAND), F32)],
        scratch_shapes=[pltpu.VMEM((S + LEFT_PAD, LANES), F32), pltpu.VMEM((S + LEFT_PAD, LANES), F32)],
        compiler_params=_params(("parallel", "arbitrary")),
    )(q, do, kpad, vpad, bias, dk_in, dv_in)


def _rel_onehot(t):
    r = lax.broadcasted_iota(jnp.int32, (BAND, N_REL), 0)
    i = lax.broadcasted_iota(jnp.int32, (BAND, N_REL), 1)
    idx = jnp.clip(t + LEFT_PAD - r, -(CHUNK - 1), MAX_REL) + (CHUNK - 1)
    return (idx == i).astype(F32)


def bias_expand(rb, name):
    NH = rb.shape[0]

    def body(rb_ref, o_ref):
        def step(t, carry):
            o_ref[t] = lax.dot_general(rb_ref[...], _rel_onehot(t), NT, preferred_element_type=F32,
                                       precision=HIGHEST)
            return carry

        lax.fori_loop(0, CHUNK, step, 0)

    return pl.pallas_call(
        body, name=name, in_specs=[VMEM_SPEC], out_specs=VMEM_SPEC,
        out_shape=jax.ShapeDtypeStruct((CHUNK, NH, BAND), F32), compiler_params=_params(),
    )(rb)


def bias_grad(dsum, name):
    NH = dsum.shape[1]

    def body(d_ref, o_ref):
        def step(t, acc):
            return acc + jnp.dot(d_ref[t], _rel_onehot(t), preferred_element_type=F32, precision=HIGHEST)

        o_ref[...] = lax.fori_loop(0, CHUNK, step, jnp.zeros((NH, N_REL), F32))

    return pl.pallas_call(
        body, name=name, in_specs=[VMEM_SPEC], out_specs=VMEM_SPEC,
        out_shape=jax.ShapeDtypeStruct((NH, N_REL), F32), compiler_params=_params(),
    )(dsum)


def loss_grad(y, tgt, name):
    S, D = y.shape
    tm = min(ROW_TILE, S)

    def body(y_ref, t_ref, d_ref, acc_ref):
        @pl.when(pl.program_id(0) == 0)
        def _():
            acc_ref[...] = jnp.zeros_like(acc_ref)

        err = y_ref[...] - t_ref[...]
        d_ref[...] = err * (1.0 / D)
        acc_ref[0:1, :] += jnp.sum(err * err, axis=0, keepdims=True)

    tile = pl.BlockSpec((tm, D), _row(0))
    return pl.pallas_call(
        body, name=name, grid=(S // tm,), in_specs=[tile, tile],
        out_specs=[tile, pl.BlockSpec((8, D), _fixed2)],
        out_shape=[jax.ShapeDtypeStruct((S, D), F32), jax.ShapeDtypeStruct((8, D), F32)],
        compiler_params=_params(("arbitrary",)),
    )(y, tgt)


def _col_tile(n):
    for t in (768, 512, 256, 128):
        if n % t == 0:
            return t
    return n


def ada_fwd(c_all, w, b, name):
    L, D, n = w.shape
    tn = _col_tile(n)

    def body(c_ref, w_ref, b_ref, o_ref):
        cv = c_ref[...]
        ca = cv * _sigmoid(cv)
        o_ref[...] = jnp.dot(ca, w_ref[...], preferred_element_type=F32, precision=HIGHEST) + b_ref[...]

    return pl.pallas_call(
        body, name=name, grid=(L, n // tn),
        in_specs=[pl.BlockSpec((N_DEV, D), lambda l, j: (0, 0)), pl.BlockSpec((None, D, tn), lambda l, j: (l, 0, j)),
                  pl.BlockSpec((None, 1, tn), lambda l, j: (l, 0, j))],
        out_specs=pl.BlockSpec((None, N_DEV, tn), lambda l, j: (l, 0, j)),
        out_shape=jax.ShapeDtypeStruct((L, N_DEV, n), F32),
        compiler_params=_params(("parallel", "parallel")),
    )(c_all, w, b)


def ada_wgrad(c_all_t, dmod, name):
    L, _, n = dmod.shape
    D = c_all_t.shape[0]
    tn = _col_tile(n)

    def body(c_ref, d_ref, o_ref):
        cv = c_ref[...]
        ca = cv * _sigmoid(cv)
        o_ref[...] = jnp.dot(ca, d_ref[...], preferred_element_type=F32, precision=HIGHEST)

    return pl.pallas_call(
        body, name=name, grid=(L, n // tn),
        in_specs=[pl.BlockSpec((D, N_DEV), lambda l, j: (0, 0)), pl.BlockSpec((None, N_DEV, tn), lambda l, j: (l, 0, j))],
        out_specs=pl.BlockSpec((None, D, tn), lambda l, j: (l, 0, j)),
        out_shape=jax.ShapeDtypeStruct((L, D, n), F32),
        compiler_params=_params(("parallel", "parallel")),
    )(c_all_t, dmod)


ELEMENTWISE_BLOCK_BYTES = 3 * 1024 * 1024


def _elementwise_rows(rows, row_bytes):
    for t in (4096, 2048, 1024, 512, 256, 128, 64, 32, 16):
        if rows % t == 0 and t * row_bytes <= ELEMENTWISE_BLOCK_BYTES:
            return t
    return rows


def sum_leading(a, name):
    n, M, N = a.shape
    tr = _elementwise_rows(M, n * N * 4)

    def body(a_ref, o_ref):
        acc = a_ref[0]
        for i in range(1, n):
            acc = acc + a_ref[i]
        o_ref[...] = acc

    return pl.pallas_call(
        body, name=name, grid=(M // tr,),
        in_specs=[pl.BlockSpec((n, tr, N), lambda i: (0, i, 0))],
        out_specs=pl.BlockSpec((tr, N), _row(0)),
        out_shape=jax.ShapeDtypeStruct((M, N), F32),
        compiler_params=_params(("parallel",)),
    )(a)


def pair_sum(full, recv, half_idx, name):
    n, M, N = recv.shape
    kb = max(k for k in (1, 2, 4) if n % k == 0 and (k == 1 or k * M * N * 2 <= ELEMENTWISE_BLOCK_BYTES))

    def body(h_ref, a_ref, b_ref, o_ref):
        o_ref[...] = (a_ref[...].astype(F32) + b_ref[...].astype(F32)).astype(BF16)

    spec = pl.BlockSpec((kb, M, N), lambda k, h: (k, 0, 0))
    return pl.pallas_call(
        body, name=name,
        grid_spec=pltpu.PrefetchScalarGridSpec(
            num_scalar_prefetch=1, grid=(n // kb,),
            in_specs=[pl.BlockSpec((kb, M, N), lambda k, h: (k, h[0], 0)), spec], out_specs=spec),
        out_shape=jax.ShapeDtypeStruct((n, M, N), BF16),
        compiler_params=_params(("parallel",)),
    )(half_idx, full, recv)


def chip_sum(psum, land, chip_idx, name):
    n, M, N = psum.shape
    tr = _elementwise_rows(M, N * 8)

    def body(s_ref, p_ref, a_ref, b_ref, c_ref, o_ref):
        o_ref[...] = ((p_ref[...].astype(F32) + a_ref[...].astype(F32)) + b_ref[...].astype(F32)) + c_ref[...].astype(F32)

    def entry(j):
        return pl.BlockSpec((None, tr, N), lambda i, s: ((s[0] + j) % n, i, 0))

    return pl.pallas_call(
        body, name=name,
        grid_spec=pltpu.PrefetchScalarGridSpec(
            num_scalar_prefetch=1, grid=(M // tr,),
            in_specs=[entry(0), entry(1), entry(2), entry(3)],
            out_specs=pl.BlockSpec((tr, N), lambda i, s: (i, 0))),
        out_shape=jax.ShapeDtypeStruct((M, N), F32),
        compiler_params=_params(("parallel",)),
    )(chip_idx, psum, land, land, land)


def adamw(w, g, m, v, name):
    M, N = w.shape
    tr = _elementwise_rows(M, N * 4)
    c1 = 1.0 - ADAM_B1 ** ADAM_STEP
    c2 = 1.0 - ADAM_B2 ** ADAM_STEP

    def body(w_ref, g_ref, m_ref, v_ref, d_ref, nm_ref, nv_ref):
        g = g_ref[...]
        nm = ADAM_B1 * m_ref[...] + (1.0 - ADAM_B1) * g
        nv = ADAM_B2 * v_ref[...] + (1.0 - ADAM_B2) * (g * g)
        d_ref[...] = -ADAM_LR * ((nm / c1) / (jnp.sqrt(nv / c2) + ADAM_EPS) + ADAM_WD * w_ref[...])
        nm_ref[...] = nm
        nv_ref[...] = nv

    spec = pl.BlockSpec((tr, N), _row(0))
    shp = jax.ShapeDtypeStruct((M, N), F32)
    return pl.pallas_call(
        body, name=name, grid=(M // tr,), in_specs=[spec] * 4, out_specs=[spec] * 3, out_shape=[shp] * 3,
        compiler_params=_params(("parallel",)),
    )(w, g, m, v)


def _coords():
    return lax.axis_index("x"), lax.axis_index("y"), lax.axis_index("c")


def all_gather8(block, name):
    m_per, n = block.shape

    def body(x_ref, out_ref, send_sems, recv_sems, local_sem):
        x, y, c = _coords()
        me, sibling = (x, y, c), (x, y, 1 - c)
        chips = [(1 - x, y), (x, 1 - y), (1 - x, 1 - y)]

        def rows(px, py, pc):
            return out_ref.at[pl.ds((4 * px + 2 * py + pc) * m_per, m_per), :]

        def copy(k, blk, to, src=None):
            return pltpu.make_async_remote_copy(
                src_ref=rows(*blk) if src is None else src, dst_ref=rows(*blk),
                send_sem=send_sems.at[k], recv_sem=recv_sems.at[k], device_id=to, device_id_type=MESH)

        mine = pltpu.make_async_copy(x_ref, rows(*me), local_sem)
        mine.start()
        first = [copy(0, me, sibling, src=x_ref)]
        first += [copy(1 + j, me, (*chip, c), src=x_ref) for j, chip in enumerate(chips)]
        for cp in first:
            cp.start()
        passed = [copy(4 + j, (*chip, c), sibling) for j, chip in enumerate(chips)]
        for j, chip in enumerate(chips):
            copy(1 + j, (*chip, c), me).wait_recv()
            passed[j].start()
        copy(0, sibling, me).wait_recv()
        for j, chip in enumerate(chips):
            copy(4 + j, (*chip, 1 - c), me).wait_recv()
        for cp in first + passed:
            cp.wait_send()
        mine.wait()

    return pl.pallas_call(
        body, name=name, in_specs=[VMEM_SPEC], out_specs=VMEM_SPEC,
        out_shape=jax.ShapeDtypeStruct((N_DEV * m_per, n), block.dtype),
        scratch_shapes=[pltpu.SemaphoreType.DMA((7,)), pltpu.SemaphoreType.DMA((7,)), pltpu.SemaphoreType.DMA],
        compiler_params=_params(),
    )(block)


def _other_chips(x, y):
    return [(1 - x, y), (x, 1 - y), (1 - x, 1 - y)]


HBM_SPEC = pl.BlockSpec(memory_space=pltpu.HBM)
SEM_SPEC = pl.BlockSpec(memory_space=pltpu.SEMAPHORE)
DATAFLOW = pltpu.SideEffectType.DATAFLOW_SIDE_EFFECTING


def _weight_desc(stack_ref, k, land_ref, px, py, c, me_s):
    h = land_ref.shape[1] // 2
    rows = pl.ds(c * h, h)
    return stack_ref.at[k, rows, :], land_ref.at[me_s, rows, :], land_ref.at[2 * px + py, rows, :]


def _grad_desc(psum_ref, k, land_ref, px, py, c, me_s):
    return psum_ref.at[2 * px + py], land_ref.at[me_s], land_ref.at[2 * px + py]


def exchange_start(srcs, lands, units, groups, desc, after, name):
    n_s, n_l, n_g = len(srcs), len(lands), len(groups)

    def body(*refs):
        s_refs, l_refs = refs[:n_s], refs[n_s:n_s + n_l]
        outs = refs[n_s + n_l + 1:]
        sems, token = outs[:2 * n_g], outs[-1]
        x, y, c = _coords()
        me_s = 2 * x + y
        for g, ids in enumerate(groups):
            for i, u in enumerate(ids):
                si, k = units[u]
                for j, (px, py) in enumerate(_other_chips(x, y)):
                    src, dst, _ = desc(s_refs[si], k, l_refs[u], px, py, c, me_s)
                    pltpu.make_async_remote_copy(
                        src_ref=src, dst_ref=dst, send_sem=sems[2 * g].at[3 * i + j],
                        recv_sem=sems[2 * g + 1].at[3 * i + j], device_id=(px, py, c), device_id_type=MESH).start()
        token[...] = jnp.zeros_like(token)

    arrs = list(srcs) + list(lands)
    sem_shapes = [pltpu.SemaphoreType.DMA((3 * len(ids),)) for ids in groups for _ in range(2)]
    outs = pl.pallas_call(
        body, name=name,
        in_specs=[HBM_SPEC] * len(arrs) + [ANY],
        out_specs=[SEM_SPEC] * (2 * n_g) + [HBM_SPEC] * len(arrs) + [VMEM_SPEC],
        out_shape=sem_shapes + [pltpu.HBM(a.shape, a.dtype) for a in arrs] + [jax.ShapeDtypeStruct((8, LANES), F32)],
        input_output_aliases={i: 2 * n_g + i for i in range(len(arrs))},
        compiler_params=pltpu.CompilerParams(has_side_effects=DATAFLOW),
    )(*[pltpu.with_memory_space_constraint(a, pltpu.HBM) for a in arrs], after)
    sems = outs[:2 * n_g]
    thru = outs[2 * n_g:2 * n_g + len(arrs)]
    return sems, list(thru[:n_s]), list(thru[n_s:]), outs[-1]


def exchange_wait(srcs, lands, units, send_sem, recv_sem, desc, after, name):
    n_s, n_l = len(srcs), len(lands)

    def body(*refs):
        s_refs, l_refs = refs[:n_s], refs[n_s:n_s + n_l]
        send_sems, recv_sems = refs[n_s + n_l], refs[n_s + n_l + 1]
        x, y, c = _coords()
        me_s = 2 * x + y
        for i, (si, k) in enumerate(units):
            for j, (px, py) in enumerate(_other_chips(x, y)):
                src, _, mine = desc(s_refs[si], k, l_refs[i], px, py, c, me_s)
                cp = pltpu.make_async_remote_copy(
                    src_ref=src, dst_ref=mine, send_sem=send_sems.at[3 * i + j], recv_sem=recv_sems.at[3 * i + j],
                    device_id=(px, py, c), device_id_type=MESH)
                cp.wait_send()
                cp.wait_recv()

    arrs = list(srcs) + list(lands)
    outs = pl.pallas_call(
        body, name=name,
        in_specs=[HBM_SPEC] * len(arrs) + [SEM_SPEC, SEM_SPEC, ANY],
        out_specs=[HBM_SPEC] * len(arrs),
        out_shape=[pltpu.HBM(a.shape, a.dtype) for a in arrs],
        input_output_aliases={i: i for i in range(len(arrs))},
        compiler_params=pltpu.CompilerParams(has_side_effects=DATAFLOW),
    )(*arrs, send_sem, recv_sem, after)
    return list(outs[:n_s]), list(outs[n_s:])


def sibling_fill(lands, name):
    n_u = len(lands)

    def body(*refs):
        ins, outs = refs[:n_u], refs[n_u:2 * n_u]
        send_sems, recv_sems = refs[2 * n_u:]
        x, y, c = _coords()
        sends = []
        for u in range(n_u):
            h = ins[u].shape[1] // 2
            for j, (px, py) in enumerate(_other_chips(x, y)):
                part = (2 * px + py, pl.ds(c * h, h), slice(None))
                cp = pltpu.make_async_remote_copy(
                    src_ref=ins[u].at[part], dst_ref=outs[u].at[part], send_sem=send_sems.at[3 * u + j],
                    recv_sem=recv_sems.at[3 * u + j], device_id=(x, y, 1 - c), device_id_type=MESH)
                cp.start()
                sends.append(cp)
        for u in range(n_u):
            h = ins[u].shape[1] // 2
            for j, (px, py) in enumerate(_other_chips(x, y)):
                theirs = (2 * px + py, pl.ds((1 - c) * h, h), slice(None))
                pltpu.make_async_remote_copy(
                    src_ref=ins[u].at[theirs], dst_ref=outs[u].at[theirs], send_sem=send_sems.at[3 * u + j],
                    recv_sem=recv_sems.at[3 * u + j], device_id=(x, y, 1 - c), device_id_type=MESH).wait_recv()
        for cp in sends:
            cp.wait_send()

    return pl.pallas_call(
        body, name=name, in_specs=[ANY] * n_u, out_specs=[ANY] * n_u,
        out_shape=[jax.ShapeDtypeStruct(a.shape, a.dtype) for a in lands],
        input_output_aliases={i: i for i in range(n_u)},
        scratch_shapes=[pltpu.SemaphoreType.DMA((3 * n_u,)), pltpu.SemaphoreType.DMA((3 * n_u,))],
        compiler_params=_params(),
    )(*lands)


def grad_pair_exchange(grads, name):
    n_u = len(grads)

    def body(*refs):
        ins, outs = refs[:n_u], refs[n_u:2 * n_u]
        send_sems, recv_sems = refs[2 * n_u:]
        x, y, c = _coords()
        cps = []
        for u in range(n_u):
            half = ins[u].shape[1] // 2
            cp = pltpu.make_async_remote_copy(
                src_ref=ins[u].at[:, pl.ds((1 - c) * half, half), :], dst_ref=outs[u],
                send_sem=send_sems.at[u], recv_sem=recv_sems.at[u], device_id=(x, y, 1 - c), device_id_type=MESH)
            cp.start()
            cps.append(cp)
        for cp in cps:
            cp.wait()

    out_shape = [jax.ShapeDtypeStruct((g.shape[0], g.shape[1] // 2, g.shape[2]), g.dtype) for g in grads]
    return pl.pallas_call(
        body, name=name, in_specs=[ANY] * n_u, out_specs=[ANY] * n_u, out_shape=out_shape,
        scratch_shapes=[pltpu.SemaphoreType.DMA((n_u,)), pltpu.SemaphoreType.DMA((n_u,))],
        compiler_params=_params(),
    )(*grads)


def grad_half_swap(halves, name):
    n_u = len(halves)

    def body(*refs):
        ins, outs = refs[:n_u], refs[n_u:2 * n_u]
        send_sems, recv_sems = refs[2 * n_u:]
        x, y, c = _coords()
        cps = []
        for u in range(n_u):
            cp = pltpu.make_async_remote_copy(
                src_ref=ins[u], dst_ref=outs[u], send_sem=send_sems.at[u], recv_sem=recv_sems.at[u],
                device_id=(x, y, 1 - c), device_id_type=MESH)
            cp.start()
            cps.append(cp)
        for cp in cps:
            cp.wait()

    return pl.pallas_call(
        body, name=name, in_specs=[ANY] * n_u, out_specs=[ANY] * n_u,
        out_shape=[jax.ShapeDtypeStruct(h.shape, h.dtype) for h in halves],
        scratch_shapes=[pltpu.SemaphoreType.DMA((n_u,)), pltpu.SemaphoreType.DMA((n_u,))],
        compiler_params=_params(),
    )(*halves)


def _pack_rows(parts):
    flat = jnp.concatenate([p.reshape(-1).astype(F32) for p in parts])
    n = flat.shape[0]
    padded = -(-n // (8 * LANES)) * (8 * LANES)
    return jnp.pad(flat, (0, padded - n)).reshape(-1, LANES)


def _unpack_rows(packed, shapes):
    flat = packed.reshape(-1)
    out, off = [], 0
    for s in shapes:
        size = 1
        for d in s:
            size *= d
        out.append(flat[off:off + size].reshape(s))
        off += size
    return out


def _shard_last(full, s_me):
    n = full.shape[-1] // N_CHIP
    return lax.dynamic_slice_in_dim(full, s_me * n, n, axis=full.ndim - 1)


def _unshard_last(g):
    moved = jnp.moveaxis(g, 0, -2)
    return moved.reshape(moved.shape[:-2] + (moved.shape[-2] * moved.shape[-1],))


def kernel(x, c, w_ada, b_ada, ln_g, ln_b, ffn_gu, ffn_down, gmlp_w_in, gmlp_b_in, gmlp_ln_g, gmlp_ln_b, gmlp_w_s, gmlp_b_s, gmlp_w_out, w_ada_kv, b_ada_kv, w_kv, attn_w_q, attn_rel_bias, attn_w_o, loss_target, m_w_ada, m_b_ada, m_ln_g, m_ln_b, m_ffn_gu, m_ffn_down, m_gmlp_w_in, m_gmlp_b_in, m_gmlp_ln_g, m_gmlp_ln_b, m_gmlp_w_s, m_gmlp_b_s, m_gmlp_w_out, m_w_ada_kv, m_b_ada_kv, m_w_kv, m_attn_w_q, m_attn_rel_bias, m_attn_w_o, v_w_ada, v_b_ada, v_ln_g, v_ln_b, v_ffn_gu, v_ffn_down, v_gmlp_w_in, v_gmlp_b_in, v_gmlp_ln_g, v_gmlp_ln_b, v_gmlp_w_s, v_gmlp_b_s, v_gmlp_w_out, v_w_ada_kv, v_b_ada_kv, v_w_kv, v_attn_w_q, v_attn_rel_bias, v_attn_w_o):
    xi, yi, ci = _coords()
    s_me = 2 * xi + yi
    dev = 4 * xi + 2 * yi + ci

    x0 = x[0]
    tgt = loss_target[0]
    S, D = x0.shape
    L = w_ada.shape[0]
    NA = gmlp_w_in.shape[0]
    NB = attn_w_q.shape[0]
    NH = D // HEAD_DIM
    alpha = (2.0 * L) ** 0.25
    n_ada = w_ada.shape[2]
    n_kv = w_ada_kv.shape[1]

    stack_names = ["ffn_gu", "ffn_down", "gmlp_w_in", "gmlp_w_out", "w_kv", "attn_w_q", "attn_w_o"]
    stack_src = dict(ffn_gu=ffn_gu, ffn_down=ffn_down, gmlp_w_in=gmlp_w_in, gmlp_w_out=gmlp_w_out, w_kv=w_kv[None],
                     attn_w_q=attn_w_q, attn_w_o=attn_w_o)
    stacks = [stack_src[nm].astype(BF16).reshape((-1,) + stack_src[nm].shape[-2:]) for nm in stack_names]
    units = [(si, k) for si, st in enumerate(stacks) for k in range(st.shape[0])]
    unit_of = {(stack_names[si], k): u for u, (si, k) in enumerate(units)}
    weight_groups = [[("ffn_gu", 0), ("ffn_down", 0)],
                     ([("gmlp_w_in", 0), ("gmlp_w_out", 0)] if NA > 0 else [("attn_w_q", 0), ("attn_w_o", 0)])
                     + [("ffn_gu", 1), ("ffn_down", 1)] + ([("w_kv", 0)] if NA == 0 else [])]
    for l in range(1, L):
        names = [("w_kv", 0)] if l == NA else []
        names += [("ffn_gu", 2 * l), ("ffn_down", 2 * l)]
        names += [("gmlp_w_in", l), ("gmlp_w_out", l)] if l < NA else [("attn_w_q", l - NA), ("attn_w_o", l - NA)]
        names += [("ffn_gu", 2 * l + 1), ("ffn_down", 2 * l + 1)]
        weight_groups.append(names)
    weight_groups = [[unit_of[n] for n in names] for names in weight_groups]

    c_all = all_gather8(jnp.broadcast_to(c, (8, D)), "ag_c").reshape(N_DEV, 8, D)[:, 0]
    b_ada_sh = lax.dynamic_slice_in_dim(b_ada, s_me * n_ada, n_ada, axis=1)
    b_kv_sh = lax.dynamic_slice_in_dim(b_ada_kv, s_me * n_kv, n_kv, axis=0)
    mod_part = ada_fwd(c_all, w_ada, b_ada_sh[:, None, :], "ada_fwd")
    mkv_part = ada_fwd(c_all, w_ada_kv[None], b_kv_sh[None, None, :], "ada_kv_fwd")
    part = jnp.concatenate([jnp.transpose(mod_part, (1, 0, 2)).reshape(N_DEV, L * n_ada), mkv_part[0]], axis=1)
    width = part.shape[1]
    pad_w = -(-width // LANES) * LANES - width
    all_part = all_gather8(jnp.pad(part, ((0, 0), (0, pad_w))), "ag_mod").reshape(N_DEV, N_DEV, width + pad_w)
    mine = lax.dynamic_index_in_dim(all_part[0::2], dev, axis=1, keepdims=False)
    mod = jnp.transpose(mine[:, :L * n_ada].reshape(N_CHIP, L, n_ada), (1, 0, 2)).reshape(L, N_MOD, D)
    mkv = mine[:, L * n_ada:width].reshape(2, D)

    def mrow(l, k):
        return mod[l, k][None, :]

    small_shapes = [ln_g.shape, ln_b.shape, gmlp_b_in.shape, gmlp_ln_g.shape, gmlp_ln_b.shape, attn_rel_bias.shape]
    small_pack = _pack_rows([ln_g, ln_b, gmlp_b_in, gmlp_ln_g, gmlp_ln_b, attn_rel_bias])
    small_all = all_gather8(small_pack, "ag_small_params").reshape((N_DEV,) + small_pack.shape)[0::2]
    sm = [_unpack_rows(small_all[s], small_shapes) for s in range(N_CHIP)]
    ln_g_f, ln_b_f, b_in_f, gln_g_f, gln_b_f, rel_f = [
        _unshard_last(jnp.stack([sm[s][i] for s in range(N_CHIP)])) for i in range(len(small_shapes))]

    lands0 = [lax.dynamic_update_slice(lax.empty((N_CHIP,) + stacks[si].shape[1:], BF16), stacks[si][k][None],
                                       (s_me, 0, 0)) for si, k in units]
    gathers_done = jnp.concatenate([mod.reshape(-1)[:LANES], small_all.reshape(-1)[:LANES]])
    w_sems, stacks_t, lands_t, _ = exchange_start(stacks, lands0, units, weight_groups, _weight_desc, gathers_done,
                                                  "weight_send_start")
    wg = {}

    def fetch_weights(g, stacks_now, after):
        ids = weight_groups[g]
        stacks_next, got = exchange_wait(stacks_now, [lands_t[u] for u in ids], [units[u] for u in ids],
                                         w_sems[2 * g], w_sems[2 * g + 1], _weight_desc, after,
                                         "weight_send_wait_%d" % g)
        for u, a in zip(ids, sibling_fill(got, "weight_sibling_fill")):
            wg[u] = a
        return stacks_next

    def W(nm, k):
        return wg[unit_of[(nm, k)]]

    def Wrows(nm, k):
        w4 = W(nm, k)
        return w4.reshape(w4.shape[0] * w4.shape[1], w4.shape[2])

    bst = [jnp.transpose(gmlp_b_s[j]) for j in range(NA)]
    biases = {}

    def make_bias(j, dep):
        rel, _ = lax.optimization_barrier((rel_f[j], dep))
        biases[j] = jnp.transpose(bias_expand(rel, "bias_expand"), (1, 0, 2))
        return biases[j]

    saved = []
    xc = x0
    kpad = vpad = xkv = None
    for l in range(L):
        after = mod if l == 0 else xc
        if l == 1 and NB > 1:
            after = make_bias(1, xc)
        stacks_t = fetch_weights(0 if l == 0 else l + 1, stacks_t, after)
        if l == NA:
            xkv = xc
            kv = mod_matmul(xc, mkv[1][None], mkv[0][None], W("w_kv", 0), None, BF16, "kv_proj")
            kpad = jnp.pad(kv[:, :D], ((LEFT_PAD, 0), (0, 0)))
            vpad = jnp.pad(kv[:, D:], ((LEFT_PAD, 0), (0, 0)))
        sv = {}
        for i in (0, 2):
            k = 2 * l + i // 2
            gu = mod_matmul(xc, mrow(l, 3 * i + 1), mrow(l, 3 * i), W("ffn_gu", k), None, BF16, "ffn_up")
            gw = 0.5 * (1.0 + mrow(l, 3 * i + 2))
            xn, xh, rs, yv = matmul_res_ln(gu, Wrows("ffn_down", k), xc, gw, ln_g_f[l, i][None], ln_b_f[l, i][None],
                                           alpha, True, "ffn_down")
            sv[i] = dict(x=xc, gu=gu, xh=xh, rs=rs, y=yv, gw=gw)
            xc = xn
            if i == 0:
                if l == 0:
                    stacks_t = fetch_weights(1, stacks_t, make_bias(0, xc) if NB > 0 else xc)
                gw = 1.0 + mrow(l, 5)
                if l < NA:
                    pre = mod_matmul(xc, mrow(l, 4), mrow(l, 3), W("gmlp_w_in", l), b_in_f[l][None], F32, "gmlp_in")
                    qv = sgu_fwd(pre, gln_g_f[l][None], gln_b_f[l][None], gmlp_w_s[l], bst[l], "sgu_fwd")
                    xn, xh, rs, yv = matmul_res_ln(qv, Wrows("gmlp_w_out", l), xc, gw, ln_g_f[l, 1][None],
                                                   ln_b_f[l, 1][None], alpha, False, "gmlp_out")
                    sv[1] = dict(x=xc, pre=pre, a=qv, xh=xh, rs=rs, y=yv, gw=gw)
                else:
                    j = l - NA
                    if j not in biases:
                        make_bias(j, xc)
                    qh = mod_matmul(xc, mrow(l, 4), mrow(l, 3), Wrows("attn_w_q", j)[None], None, BF16, "attn_q")
                    ov = attn_fwd(qh, kpad, vpad, biases[j], "attn_fwd")
                    xn, xh, rs, yv = matmul_res_ln(ov, Wrows("attn_w_o", j), xc, gw, ln_g_f[l, 1][None],
                                                   ln_b_f[l, 1][None], alpha, False, "attn_out")
                    sv[1] = dict(x=xc, q=qh, a=ov, xh=xh, rs=rs, y=yv, gw=gw)
                xc = xn
        saved.append(sv)

    dx, lacc = loss_grad(xc, tgt, "loss_grad")
    loss = lax.psum((0.5 / D) * jnp.sum(lacc[0]), ("x", "y", "c"))

    gfull = [None] * len(units)
    dmod = [[None] * N_MOD for _ in range(L)]
    d_ln_g = [[None] * 3 for _ in range(L)]
    d_ln_b = [[None] * 3 for _ in range(L)]
    d_b_in, d_gln_g, d_gln_b, d_ws, d_bs, d_rel = ([None] * NA, [None] * NA, [None] * NA, [None] * NA, [None] * NA,
                                                  [None] * NB)
    dk = jnp.zeros((S, D), F32)
    dv = jnp.zeros((S, D), F32)
    dmkv = None

    made = []

    def put(nm, k, g):
        gfull[unit_of[(nm, k)]] = g.reshape((N_CHIP, -1, g.shape[-1]))
        made.append(unit_of[(nm, k)])

    own_half, sib_half = {}, {}

    core_idx = ci.astype(jnp.int32).reshape(1)
    chip_idx = s_me.astype(jnp.int32).reshape(1)
    n_started = [0]

    def start_grad_exchange(ids, after):
        from_sibling = grad_pair_exchange([gfull[u] for u in ids], "grad_pair_exchange")
        psums = [pair_sum(gfull[u], fs, core_idx, "grad_pair_sum") for u, fs in zip(ids, from_sibling)]
        n = len(ids)
        tag = n_started[0]
        n_started[0] += 1
        sems, ps_t, q_t, token = exchange_start(psums, [lax.empty(p.shape, p.dtype) for p in psums],
                                                [(i, 0) for i in range(n)], [list(range(n))], _grad_desc, after,
                                                "grad_send_start_%d" % tag)
        return dict(ids=ids, tag=tag, sems=sems, ps=ps_t, q=q_t), token

    def finish_grad_exchange(pend, after):
        n = len(pend["ids"])
        ps_t, q = exchange_wait(pend["ps"], pend["q"], [(i, 0) for i in range(n)], pend["sems"][0], pend["sems"][1],
                                _grad_desc, after, "grad_send_wait_%d" % pend["tag"])
        halves = [chip_sum(ps_t[i], q[i], chip_idx, "grad_chip_sum") for i in range(n)]
        for u, h, s in zip(pend["ids"], halves, grad_half_swap(halves, "grad_half_swap")):
            own_half[u], sib_half[u] = h, s

    pending = None
    order_token = jnp.zeros((), F32)
    for l in reversed(range(L)):
        if l == NA - 1:
            dkv = jnp.concatenate([dk, dv], axis=1)
            put("w_kv", 0, wgrad("mod", (xkv, mkv[1][None], mkv[0][None]), dkv, D, dkv.shape[1] // N_CHIP, "kv_wgrad"))
            dx, acc = dgrad_mod(dkv, W("w_kv", 0), dx, xkv, mkv[1][None], "kv_dgrad")
            dmkv = jnp.stack([acc[1], acc[0]])
        sv = saved[l]
        for i in (2, 1, 0):
            t = sv[i]
            wres = 1.0 if i == 1 else 0.5
            dxa, dy, acc1 = ln_res_bwd(dx, t["xh"], t["rs"], ln_g_f[l, i][None] + order_token, t["y"], t["gw"], wres,
                                       alpha, "ln_res_bwd")
            d_ln_g[l][i], d_ln_b[l][i], dmod[l][3 * i + 2] = acc1[0], acc1[1], acc1[2]
            scl, shf = mrow(l, 3 * i + 1), mrow(l, 3 * i)
            if i != 1:
                k = 2 * l + i // 2
                F = t["gu"].shape[1] // 2
                dgu = ffn_act_bwd(dy, Wrows("ffn_down", k), t["gu"], "ffn_act_bwd")
                put("ffn_down", k, wgrad("swiglu", (t["gu"],), dy, F // 2, D, "ffn_down_wgrad"))
                put("ffn_gu", k, wgrad("mod", (t["x"], scl, shf), dgu, D, 2 * F // N_CHIP, "ffn_up_wgrad"))
                dx, acc2 = dgrad_mod(dgu, W("ffn_gu", k), dxa, t["x"], scl, "ffn_up_dgrad")
            elif l < NA:
                dq = matmul_nt(dy, Wrows("gmlp_w_out", l), "gmlp_out_dgrad")
                put("gmlp_w_out", l, wgrad("plain", (t["a"],), dy, t["a"].shape[1], D, "gmlp_out_wgrad"))
                dpre, dws_l, dss, dgl, dbin = sgu_bwd(dq, t["pre"], gln_g_f[l][None], gln_b_f[l][None], gmlp_w_s[l],
                                                      bst[l], "sgu_bwd")
                d_ws[l] = dws_l
                d_bs[l] = jnp.transpose(group_lane_sum(dss, "sgu_bias_grad")[:, :GMLP_GROUPS])
                d_gln_g[l], d_gln_b[l], d_b_in[l] = dgl[0], dgl[1], dbin[0]
                put("gmlp_w_in", l, wgrad("mod", (t["x"], scl, shf), dpre, D, dpre.shape[1] // N_CHIP, "gmlp_in_wgrad"))
                dx, acc2 = dgrad_mod(dpre, W("gmlp_w_in", l), dxa, t["x"], scl, "gmlp_in_dgrad")
            else:
                j = l - NA
                do = matmul_nt(dy, Wrows("attn_w_o", j), "attn_out_dgrad")
                put("attn_w_o", j, wgrad("plain", (t["a"],), dy, D, D, "attn_out_wgrad"))
                dqh, dk, dv, dbias = attn_bwd(t["q"], do, kpad, vpad, biases[j], dk, dv, "attn_bwd")
                d_rel[j] = bias_grad(jnp.transpose(dbias, (1, 0, 2)), "bias_grad")
                put("attn_w_q", j, wgrad("mod", (t["x"], scl, shf), dqh, D, D, "attn_q_wgrad"))
                dx, acc2 = dgrad_mod(dqh, Wrows("attn_w_q", j)[None], dxa, t["x"], scl, "attn_q_dgrad")
            dmod[l][3 * i + 1], dmod[l][3 * i] = acc2[0], acc2[1]
            if (i == 0 and l > 0) or (i == 1 and l == 0):
                started, token = start_grad_exchange(list(made), dx)
                made.clear()
                if pending is not None:
                    finish_grad_exchange(pending, dx)
                pending = started
                order_token = token[0, 0]
    grad_x = dx[None]

    dvec = _pack_rows([jnp.stack([jnp.stack(r) for r in dmod]), dmkv])
    n_dvec = L * N_MOD * D + 2 * D
    dall = all_gather8(dvec, "ag_dmod").reshape(N_DEV, -1, LANES)
    db_all = sum_leading(dall, "ada_bias_grad").reshape(-1)[:n_dvec]
    g_b_ada = db_all[:L * N_MOD * D].reshape(L, N_MOD * D)
    g_b_ada_kv = db_all[L * N_MOD * D:]
    dall2 = dall.reshape(N_DEV, -1)[:, :n_dvec]
    dmod_all = dall2[:, :L * N_MOD * D].reshape(N_DEV, L, N_MOD * D)
    dmod_sh = jnp.transpose(lax.dynamic_slice_in_dim(dmod_all, s_me * n_ada, n_ada, axis=2), (1, 0, 2))
    dmkv_sh = lax.dynamic_slice_in_dim(dall2[:, L * N_MOD * D:], s_me * n_kv, n_kv, axis=1)[None]
    c_all_t = jnp.transpose(c_all)
    g_w_ada = ada_wgrad(c_all_t, dmod_sh, "ada_wgrad")
    g_w_ada_kv = ada_wgrad(c_all_t, dmkv_sh, "ada_kv_wgrad")[0]

    small_g = [jnp.stack([jnp.stack(r) for r in d_ln_g]), jnp.stack([jnp.stack(r) for r in d_ln_b]),
               jnp.stack(d_b_in), jnp.stack(d_gln_g), jnp.stack(d_gln_b), jnp.stack(d_rel), jnp.stack(d_ws),
               jnp.stack(d_bs)]
    sg_shapes = [a.shape for a in small_g]
    sg_pack = _pack_rows(small_g)
    sg_all = all_gather8(sg_pack, "ag_small_grads").reshape(N_DEV, -1, LANES)
    sg_sum = _unpack_rows(sum_leading(sg_all, "small_grad_sum"), sg_shapes)
    g_ln_g, g_ln_b, g_b_in, g_gln_g, g_gln_b, g_rel = [_shard_last(a, s_me) for a in sg_sum[:6]]
    g_ws, g_bs = sg_sum[6], sg_sum[7]

    last, _ = start_grad_exchange(list(made), sg_all)
    finish_grad_exchange(pending, sg_all)

    grads = dict(w_ada=g_w_ada, b_ada=g_b_ada, ln_g=g_ln_g, ln_b=g_ln_b, gmlp_b_in=g_b_in, gmlp_ln_g=g_gln_g,
                 gmlp_ln_b=g_gln_b, gmlp_w_s=g_ws, gmlp_b_s=g_bs, w_ada_kv=g_w_ada_kv, b_ada_kv=g_b_ada_kv,
                 attn_rel_bias=g_rel)
    weights = dict(w_ada=w_ada, b_ada=b_ada, ln_g=ln_g, ln_b=ln_b, ffn_gu=ffn_gu, ffn_down=ffn_down,
                   gmlp_w_in=gmlp_w_in, gmlp_b_in=gmlp_b_in, gmlp_ln_g=gmlp_ln_g, gmlp_ln_b=gmlp_ln_b,
                   gmlp_w_s=gmlp_w_s, gmlp_b_s=gmlp_b_s, gmlp_w_out=gmlp_w_out, w_ada_kv=w_ada_kv,
                   b_ada_kv=b_ada_kv, w_kv=w_kv, attn_w_q=attn_w_q, attn_rel_bias=attn_rel_bias, attn_w_o=attn_w_o)
    ms = dict(w_ada=m_w_ada, b_ada=m_b_ada, ln_g=m_ln_g, ln_b=m_ln_b, ffn_gu=m_ffn_gu, ffn_down=m_ffn_down,
              gmlp_w_in=m_gmlp_w_in, gmlp_b_in=m_gmlp_b_in, gmlp_ln_g=m_gmlp_ln_g, gmlp_ln_b=m_gmlp_ln_b,
              gmlp_w_s=m_gmlp_w_s, gmlp_b_s=m_gmlp_b_s, gmlp_w_out=m_gmlp_w_out, w_ada_kv=m_w_ada_kv,
              b_ada_kv=m_b_ada_kv, w_kv=m_w_kv, attn_w_q=m_attn_w_q, attn_rel_bias=m_attn_rel_bias,
              attn_w_o=m_attn_w_o)
    vs = dict(w_ada=v_w_ada, b_ada=v_b_ada, ln_g=v_ln_g, ln_b=v_ln_b, ffn_gu=v_ffn_gu, ffn_down=v_ffn_down,
              gmlp_w_in=v_gmlp_w_in, gmlp_b_in=v_gmlp_b_in, gmlp_ln_g=v_gmlp_ln_g, gmlp_ln_b=v_gmlp_ln_b,
              gmlp_w_s=v_gmlp_w_s, gmlp_b_s=v_gmlp_b_s, gmlp_w_out=v_gmlp_w_out, w_ada_kv=v_w_ada_kv,
              b_ada_kv=v_b_ada_kv, w_kv=v_w_kv, attn_w_q=v_attn_w_q, attn_rel_bias=v_attn_rel_bias,
              attn_w_o=v_attn_w_o)
    order = ["w_ada", "b_ada", "ln_g", "ln_b", "ffn_gu", "ffn_down", "gmlp_w_in", "gmlp_b_in", "gmlp_ln_g",
             "gmlp_ln_b", "gmlp_w_s", "gmlp_b_s", "gmlp_w_out", "w_ada_kv", "b_ada_kv", "w_kv", "attn_w_q",
             "attn_rel_bias", "attn_w_o"]
    big_names = ["w_ada", "w_ada_kv"] + stack_names
    small_names = [nm for nm in order if nm not in big_names]
    delta, new_m, new_v = {}, {}, {}

    def adamw_big(nm):
        shp = weights[nm].shape
        two_d = (-1, shp[-1])
        d, a, b = adamw(weights[nm].reshape(two_d), grads[nm].reshape(two_d), ms[nm].reshape(two_d),
                        vs[nm].reshape(two_d), "adamw")
        delta[nm], new_m[nm], new_v[nm] = d.reshape(shp), a.reshape(shp), b.reshape(shp)

    adamw_big("w_ada")
    adamw_big("w_ada_kv")
    shapes = [weights[nm].shape for nm in small_names]
    d, a, b = adamw(_pack_rows([weights[nm] for nm in small_names]), _pack_rows([grads[nm] for nm in small_names]),
                    _pack_rows([ms[nm] for nm in small_names]), _pack_rows([vs[nm] for nm in small_names]),
                    "adamw_small")
    for nm, dd, aa, bb in zip(small_names, _unpack_rows(d, shapes), _unpack_rows(a, shapes), _unpack_rows(b, shapes)):
        delta[nm], new_m[nm], new_v[nm] = dd, aa, bb

    def full_grad(u):
        lo = jnp.where(ci == 0, own_half[u], sib_half[u])
        hi = jnp.where(ci == 0, sib_half[u], own_half[u])
        return jnp.concatenate([lo, hi], axis=0)

    def adamw_stack(nm):
        si = stack_names.index(nm)
        g = jnp.stack([full_grad(unit_of[(nm, k)]) for k in range(stacks[si].shape[0])])
        grads[nm] = g.reshape(weights[nm].shape)
        adamw_big(nm)

    late = [stack_names[units[u][0]] for u in last["ids"]]
    early = [nm for nm in stack_names if nm not in late]
    for nm in early:
        adamw_stack(nm)
    finish_grad_exchange(last, delta[early[-1]])
    for nm in stack_names:
        if nm in late:
            adamw_stack(nm)

    return (loss, grad_x, *[grads[nm] for nm in order], *[delta[nm] for nm in order],
            *[new_m[nm] for nm in order], *[new_v[nm] for nm in order])
```

```python
import functools

import jax
import jax.numpy as jnp
from jax import lax
from jax.experimental import pallas as pl
from jax.experimental.pallas import tpu as pltpu

F32 = jnp.float32
BF16 = jnp.bfloat16
MESH = pl.DeviceIdType.MESH
HIGHEST = lax.Precision.HIGHEST

CHUNK = 64
GMLP_WINDOW = 128
GMLP_GROUPS = 8
HEAD_DIM = 64
LEFT_CHUNKS = 8
BAND = (LEFT_CHUNKS + 1) * CHUNK
LEFT_PAD = LEFT_CHUNKS * CHUNK
MAX_REL = 4 * CHUNK
N_REL = (CHUNK - 1) + MAX_REL + 1
LN_EPS = 1e-5
N_MOD = 9
N_DEV = 8
N_CHIP = 4

ADAM_LR = 0.001
ADAM_B1 = 0.9
ADAM_B2 = 0.999
ADAM_EPS = 1e-08
ADAM_WD = 0.01
ADAM_STEP = 10

LANES = 128
ROW_TILE = 256
MATMUL_ROW_TILE = 512
WGRAD_ROWS = 1024
ATTN_CHUNKS_PER_STEP = 4
VMEM_LIMIT_MB = 56

NT = (((1,), (1,)), ((), ()))
TN = (((0,), (0,)), ((), ()))

ANY = pl.BlockSpec(memory_space=pl.ANY)
VMEM_SPEC = pl.BlockSpec(memory_space=pltpu.VMEM)


def _params(semantics=None):
    kw = dict(vmem_limit_bytes=VMEM_LIMIT_MB * 1024 * 1024)
    if semantics is not None:
        kw["dimension_semantics"] = semantics
    return pltpu.CompilerParams(**kw)


def _sigmoid(v):
    return 0.5 * (1.0 + jnp.tanh(0.5 * v))


def _gelu(v):
    return 0.5 * v * (1.0 + lax.erf(v * (2.0 ** -0.5)))


def _gelu_grad(v):
    return 0.5 * (1.0 + lax.erf(v * (2.0 ** -0.5))) + v * jnp.exp(-0.5 * v * v) * ((2.0 * jnp.pi) ** -0.5)


def _row(m):
    return lambda i: (i, 0)


def _fixed2(i):
    return (0, 0)


def _fixed3(i):
    return (0, 0, 0)


def _resident(shape):
    return pl.BlockSpec(shape, _fixed2 if len(shape) == 2 else _fixed3, pipeline_mode=pl.Buffered(1))


def mod_matmul(x, scl, shift, w, bias, out_dtype, name):
    S, D = x.shape
    NS, _, n = w.shape
    tm = min(MATMUL_ROW_TILE, S)
    has_bias = bias is not None

    def body(*refs):
        if has_bias:
            x_ref, scl_ref, sh_ref, w_ref, b_ref, o_ref, h_ref = refs
        else:
            x_ref, scl_ref, sh_ref, w_ref, o_ref, h_ref = refs
        h = (x_ref[...] * (1.0 + scl_ref[...]) + sh_ref[...]).astype(BF16)
        h_ref[...] = h
        for s in range(NS):
            acc = jnp.dot(h, w_ref[s], preferred_element_type=F32)
            if has_bias:
                acc = acc + b_ref[:, s * n:(s + 1) * n]
            o_ref[:, s * n:(s + 1) * n] = acc.astype(out_dtype)

    in_specs = [pl.BlockSpec((tm, D), _row(0)), pl.BlockSpec((1, D), _fixed2), pl.BlockSpec((1, D), _fixed2),
                _resident((NS, D, n))]
    args = [x, scl, shift, w]
    if has_bias:
        in_specs.append(pl.BlockSpec((1, NS * n), _fixed2))
        args.append(bias)
    return pl.pallas_call(
        body, name=name, grid=(S // tm,), in_specs=in_specs,
        out_specs=[pl.BlockSpec((tm, NS * n), _row(0)), pl.BlockSpec((tm, D), _row(0))],
        out_shape=[jax.ShapeDtypeStruct((S, NS * n), out_dtype), jax.ShapeDtypeStruct((S, D), BF16)],
        compiler_params=_params(("parallel",)),
    )(*args)


def matmul_res_ln(a, w, x, gw, lg, lb, alpha, swiglu, name):
    S, D = x.shape
    K = w.shape[0]
    tm = min(MATMUL_ROW_TILE, S)
    ka = a.shape[1]

    def body(a_ref, w_ref, x_ref, gw_ref, lg_ref, lb_ref, xn_ref, xh_ref, rs_ref, y_ref, *act_ref):
        if swiglu:
            g = a_ref[:, :K].astype(F32)
            u = a_ref[:, K:].astype(F32)
            act = (g * _sigmoid(g) * u).astype(BF16)
            act_ref[0][...] = act
        else:
            act = a_ref[...].astype(BF16)
        y = jnp.dot(act, w_ref[...], preferred_element_type=F32)
        z = alpha * x_ref[...] + gw_ref[...] * y
        mu = jnp.mean(z, axis=-1, keepdims=True)
        zc = z - mu
        var = jnp.mean(zc * zc, axis=-1, keepdims=True)
        rstd = lax.rsqrt(var + LN_EPS)
        xhat = zc * rstd
        xn_ref[...] = xhat * lg_ref[...] + lb_ref[...]
        xh_ref[...] = xhat
        rs_ref[...] = rstd
        y_ref[...] = y.astype(BF16)

    vec = pl.BlockSpec((1, D), _fixed2)
    out_specs = [pl.BlockSpec((tm, D), _row(0)), pl.BlockSpec((tm, D), _row(0)), pl.BlockSpec((tm, 1), _row(0)),
                 pl.BlockSpec((tm, D), _row(0))]
    out_shape = [jax.ShapeDtypeStruct((S, D), F32), jax.ShapeDtypeStruct((S, D), F32),
                 jax.ShapeDtypeStruct((S, 1), F32), jax.ShapeDtypeStruct((S, D), BF16)]
    if swiglu:
        out_specs.append(pl.BlockSpec((tm, K), _row(0)))
        out_shape.append(jax.ShapeDtypeStruct((S, K), BF16))
    return pl.pallas_call(
        body, name=name, grid=(S // tm,),
        in_specs=[pl.BlockSpec((tm, ka), _row(0)), _resident((K, D)), pl.BlockSpec((tm, D), _row(0)),
                  vec, vec, vec],
        out_specs=out_specs, out_shape=out_shape,
        compiler_params=_params(("parallel",)),
    )(a, w, x, gw, lg, lb)


def ln_res_bwd(dxn, xhat, rstd, lg, y, gw, wres, alpha, name):
    S, D = dxn.shape
    tm = min(MATMUL_ROW_TILE, S)

    def body(dxn_ref, xh_ref, rs_ref, lg_ref, y_ref, gw_ref, dxa_ref, dy_ref, acc_ref):
        @pl.when(pl.program_id(0) == 0)
        def _():
            acc_ref[...] = jnp.zeros_like(acc_ref)

        d = dxn_ref[...]
        xh = xh_ref[...]
        dxh = d * lg_ref[...]
        m1 = jnp.mean(dxh, axis=-1, keepdims=True)
        m2 = jnp.mean(dxh * xh, axis=-1, keepdims=True)
        dz = rs_ref[...] * (dxh - m1 - xh * m2)
        dxa_ref[...] = alpha * dz
        dy_ref[...] = (gw_ref[...] * dz).astype(BF16)
        acc_ref[0:1, :] += jnp.sum(d * xh, axis=0, keepdims=True)
        acc_ref[1:2, :] += jnp.sum(d, axis=0, keepdims=True)
        acc_ref[2:3, :] += jnp.sum((wres * dz) * y_ref[...].astype(F32), axis=0, keepdims=True)

    vec = pl.BlockSpec((1, D), _fixed2)
    tile = pl.BlockSpec((tm, D), _row(0))
    return pl.pallas_call(
        body, name=name, grid=(S // tm,),
        in_specs=[tile, tile, pl.BlockSpec((tm, 1), _row(0)), vec, tile, vec],
        out_specs=[tile, tile, pl.BlockSpec((8, D), _fixed2)],
        out_shape=[jax.ShapeDtypeStruct((S, D), F32), jax.ShapeDtypeStruct((S, D), BF16),
                   jax.ShapeDtypeStruct((8, D), F32)],
        compiler_params=_params(("arbitrary",)),
    )(dxn, xhat, rstd, lg, y, gw)


def ffn_act_bwd(dy, wd, gu, name):
    S, D = dy.shape
    K = wd.shape[0]
    tm = min(MATMUL_ROW_TILE, S)

    def body(dy_ref, wd_ref, gu_ref, o_ref):
        da = lax.dot_general(dy_ref[...], wd_ref[...], NT, preferred_element_type=F32)
        g = gu_ref[:, :K].astype(F32)
        u = gu_ref[:, K:].astype(F32)
        sg = _sigmoid(g)
        o_ref[:, :K] = (da * u * (sg * (1.0 + g * (1.0 - sg)))).astype(BF16)
        o_ref[:, K:] = (da * (g * sg)).astype(BF16)

    return pl.pallas_call(
        body, name=name, grid=(S // tm,),
        in_specs=[pl.BlockSpec((tm, D), _row(0)), _resident((K, D)), pl.BlockSpec((tm, 2 * K), _row(0))],
        out_specs=pl.BlockSpec((tm, 2 * K), _row(0)),
        out_shape=jax.ShapeDtypeStruct((S, 2 * K), BF16),
        compiler_params=_params(("parallel",)),
    )(dy, wd, gu)


def matmul_nt(a, w, name):
    S, D = a.shape
    K = w.shape[0]
    tm = min(MATMUL_ROW_TILE, S)

    def body(a_ref, w_ref, o_ref):
        o_ref[...] = lax.dot_general(a_ref[...], w_ref[...], NT, preferred_element_type=F32).astype(BF16)

    return pl.pallas_call(
        body, name=name, grid=(S // tm,),
        in_specs=[pl.BlockSpec((tm, D), _row(0)), _resident((K, D))],
        out_specs=pl.BlockSpec((tm, K), _row(0)),
        out_shape=jax.ShapeDtypeStruct((S, K), BF16),
        compiler_params=_params(("parallel",)),
    )(a, w)


def dgrad_mod(dpre, w, dxa, xin, scl, name):
    S, D = xin.shape
    NS, _, n = w.shape
    tm = min(MATMUL_ROW_TILE, S)

    def body(dp_ref, w_ref, dxa_ref, xin_ref, scl_ref, dx_ref, acc_ref):
        @pl.when(pl.program_id(0) == 0)
        def _():
            acc_ref[...] = jnp.zeros_like(acc_ref)

        dh = jnp.zeros((tm, D), F32)
        for s in range(NS):
            dh = dh + lax.dot_general(dp_ref[:, s * n:(s + 1) * n].astype(BF16), w_ref[s], NT,
                                      preferred_element_type=F32)
        dx_ref[...] = dxa_ref[...] + dh * (1.0 + scl_ref[...])
        acc_ref[0:1, :] += jnp.sum(dh * xin_ref[...], axis=0, keepdims=True)
        acc_ref[1:2, :] += jnp.sum(dh, axis=0, keepdims=True)

    tile = pl.BlockSpec((tm, D), _row(0))
    return pl.pallas_call(
        body, name=name, grid=(S // tm,),
        in_specs=[pl.BlockSpec((tm, NS * n), _row(0)), _resident((NS, D, n)), tile, tile,
                  pl.BlockSpec((1, D), _fixed2)],
        out_specs=[tile, pl.BlockSpec((8, D), _fixed2)],
        out_shape=[jax.ShapeDtypeStruct((S, D), F32), jax.ShapeDtypeStruct((8, D), F32)],
        compiler_params=_params(("arbitrary",)),
    )(dpre, w, dxa, xin, scl)


def wgrad(a, b, kb, nb, name):
    S, N = b.shape
    K = a.shape[1]
    NBk, KB = N // nb, K // kb
    ts = min(WGRAD_ROWS, S)
    nsteps = S // ts

    def body(a_ref, b_ref, o_ref, acc_ref):
        si = pl.program_id(2)

        @pl.when(si == 0)
        def _():
            acc_ref[...] = jnp.zeros_like(acc_ref)

        acc_ref[...] += lax.dot_general(a_ref[...], b_ref[...].astype(BF16), TN, preferred_element_type=F32)

        @pl.when(si == nsteps - 1)
        def _():
            o_ref[...] = acc_ref[...].astype(BF16)

    return pl.pallas_call(
        body, name=name, grid=(NBk, KB, nsteps),
        in_specs=[pl.BlockSpec((ts, kb), lambda j, k, s: (s, k)), pl.BlockSpec((ts, nb), lambda j, k, s: (s, j))],
        out_specs=pl.BlockSpec((None, kb, nb), lambda j, k, s: (j, k, 0)),
        out_shape=jax.ShapeDtypeStruct((NBk, K, nb), BF16),
        scratch_shapes=[pltpu.VMEM((kb, nb), F32)],
        compiler_params=_params(("parallel", "parallel", "arbitrary")),
    )(a, b)


def _window_mask():
    t = lax.broadcasted_iota(jnp.int32, (GMLP_WINDOW, GMLP_WINDOW), 0)
    s = lax.broadcasted_iota(jnp.int32, (GMLP_WINDOW, GMLP_WINDOW), 1)
    return ((s // CHUNK) <= (t // CHUNK)).astype(F32)


def sgu_fwd(pre, glg, glb, ws, bst, name):
    S, H2 = pre.shape
    H = H2 // 2
    W, G = GMLP_WINDOW, GMLP_GROUPS
    gd = H // G
    tm = min(ROW_TILE, S)

    def body(pre_ref, glg_ref, glb_ref, ws_ref, bst_ref, q_ref):
        u = _gelu(pre_ref[:, :H])
        v = _gelu(pre_ref[:, H:])
        mu = jnp.mean(v, axis=-1, keepdims=True)
        vc = v - mu
        var = jnp.mean(vc * vc, axis=-1, keepdims=True)
        vn = ((vc * lax.rsqrt(var + LN_EPS)) * glg_ref[...] + glb_ref[...]).astype(BF16)
        mask = _window_mask()
        for g in range(G):
            wsg = (ws_ref[g] * mask).astype(BF16)
            bcol = bst_ref[:, g:g + 1]
            for wi in range(tm // W):
                rows = slice(wi * W, (wi + 1) * W)
                cols = slice(g * gd, (g + 1) * gd)
                s = jnp.dot(wsg, vn[rows, cols], preferred_element_type=F32) + bcol
                q_ref[rows, cols] = (u[rows, cols] * s).astype(BF16)

    return pl.pallas_call(
        body, name=name, grid=(S // tm,),
        in_specs=[pl.BlockSpec((tm, H2), _row(0)), pl.BlockSpec((1, H), _fixed2), pl.BlockSpec((1, H), _fixed2),
                  pl.BlockSpec((G, W, W), _fixed3), pl.BlockSpec((W, G), _fixed2)],
        out_specs=pl.BlockSpec((tm, H), _row(0)),
        out_shape=jax.ShapeDtypeStruct((S, H), BF16),
        compiler_params=_params(("parallel",)),
    )(pre, glg, glb, ws, bst)


def sgu_bwd(dq, pre, glg, glb, ws, bst, name):
    S, H2 = pre.shape
    H = H2 // 2
    W, G = GMLP_WINDOW, GMLP_GROUPS
    gd = H // G
    tm = min(ROW_TILE, S)

    def body(dq_ref, pre_ref, glg_ref, glb_ref, ws_ref, bst_ref,
             dpre_ref, dws_ref, dss_ref, dgl_ref, dbin_ref, du_s, dvn_s):
        @pl.when(pl.program_id(0) == 0)
        def _():
            dws_ref[...] = jnp.zeros_like(dws_ref)
            dss_ref[...] = jnp.zeros_like(dss_ref)
            dgl_ref[...] = jnp.zeros_like(dgl_ref)
            dbin_ref[...] = jnp.zeros_like(dbin_ref)

        pu = pre_ref[:, :H]
        pv = pre_ref[:, H:]
        u = _gelu(pu)
        v = _gelu(pv)
        mu = jnp.mean(v, axis=-1, keepdims=True)
        vc = v - mu
        var = jnp.mean(vc * vc, axis=-1, keepdims=True)
        rstd = lax.rsqrt(var + LN_EPS)
        vhat = vc * rstd
        vn = (vhat * glg_ref[...] + glb_ref[...]).astype(BF16)
        mask = _window_mask()
        for g in range(G):
            wsg = (ws_ref[g] * mask).astype(BF16)
            bcol = bst_ref[:, g:g + 1]
            cols = slice(g * gd, (g + 1) * gd)
            for wi in range(tm // W):
                rows = slice(wi * W, (wi + 1) * W)
                vblk = vn[rows, cols]
                s = jnp.dot(wsg, vblk, preferred_element_type=F32) + bcol
                dqb = dq_ref[rows, cols].astype(F32)
                du_s[rows, cols] = dqb * s
                ds = dqb * u[rows, cols]
                dss_ref[:, cols] += ds
                dsb = ds.astype(BF16)
                dvn_s[rows, cols] = lax.dot_general(wsg, dsb, TN, preferred_element_type=F32)
                dws_ref[g] += lax.dot_general(dsb, vblk, NT, preferred_element_type=F32) * mask
        dvn = dvn_s[...]
        dgl_ref[0:1, :] += jnp.sum(dvn * vhat, axis=0, keepdims=True)
        dgl_ref[1:2, :] += jnp.sum(dvn, axis=0, keepdims=True)
        dvh = dvn * glg_ref[...]
        m1 = jnp.mean(dvh, axis=-1, keepdims=True)
        m2 = jnp.mean(dvh * vhat, axis=-1, keepdims=True)
        dv = rstd * (dvh - m1 - vhat * m2)
        dpu = du_s[...] * _gelu_grad(pu)
        dpv = dv * _gelu_grad(pv)
        dbin_ref[0:1, :H] += jnp.sum(dpu, axis=0, keepdims=True)
        dbin_ref[0:1, H:] += jnp.sum(dpv, axis=0, keepdims=True)
        dpre_ref[:, :H] = dpu.astype(BF16)
        dpre_ref[:, H:] = dpv.astype(BF16)

    return pl.pallas_call(
        body, name=name, grid=(S // tm,),
        in_specs=[pl.BlockSpec((tm, H), _row(0)), pl.BlockSpec((tm, H2), _row(0)), pl.BlockSpec((1, H), _fixed2),
                  pl.BlockSpec((1, H), _fixed2), pl.BlockSpec((G, W, W), _fixed3), pl.BlockSpec((W, G), _fixed2)],
        out_specs=[pl.BlockSpec((tm, H2), _row(0)), pl.BlockSpec((G, W, W), _fixed3), pl.BlockSpec((W, H), _fixed2),
                   pl.BlockSpec((8, H), _fixed2), pl.BlockSpec((8, H2), _fixed2)],
        out_shape=[jax.ShapeDtypeStruct((S, H2), BF16), jax.ShapeDtypeStruct((G, W, W), F32),
                   jax.ShapeDtypeStruct((W, H), F32), jax.ShapeDtypeStruct((8, H), F32),
                   jax.ShapeDtypeStruct((8, H2), F32)],
        scratch_shapes=[pltpu.VMEM((tm, H), F32), pltpu.VMEM((tm, H), F32)],
        compiler_params=_params(("arbitrary",)),
    )(dq, pre, glg, glb, ws, bst)


def group_lane_sum(dss, name):
    W, H = dss.shape
    gd = H // GMLP_GROUPS

    def body(d_ref, o_ref):
        j = lax.broadcasted_iota(jnp.int32, (H, LANES), 0)
        g = lax.broadcasted_iota(jnp.int32, (H, LANES), 1)
        ind = ((j // gd) == g).astype(F32)
        o_ref[...] = jnp.dot(d_ref[...], ind, preferred_element_type=F32, precision=HIGHEST)

    return pl.pallas_call(
        body, name=name, in_specs=[VMEM_SPEC], out_specs=VMEM_SPEC,
        out_shape=jax.ShapeDtypeStruct((W, LANES), F32), compiler_params=_params(),
    )(dss)


def _attn_load(j, cps, q_ref, k_ref, v_ref):
    r = lax.broadcasted_iota(jnp.int32, (CHUNK, BAND), 1)
    chunks = []
    for cc in range(cps):
        start = pl.multiple_of((j * cps + cc) * CHUNK, CHUNK)
        chunks.append((q_ref[cc * CHUNK:(cc + 1) * CHUNK, :], k_ref[pl.ds(start, BAND), :],
                       v_ref[pl.ds(start, BAND), :], (r + start) >= LEFT_PAD))
    return chunks


def _attn_probs(chunks, b_ref, sels, scale):
    qms = [[jnp.where(sel, q2, jnp.zeros_like(q2)) for sel in sels] for q2, _, _, _ in chunks]
    raw = [[lax.dot_general(qm, k2, NT, preferred_element_type=F32) for qm in qms[cc]]
           for cc, (_, k2, _, _) in enumerate(chunks)]
    probs = []
    for cc, (_, _, _, valid) in enumerate(chunks):
        row = []
        for sub in range(2):
            s = jnp.where(valid, raw[cc][sub] * scale + b_ref[sub], -jnp.inf)
            e = jnp.exp(s - jnp.max(s, axis=-1, keepdims=True))
            row.append(e / jnp.sum(e, axis=-1, keepdims=True))
        probs.append(row)
    return qms, probs


def attn_fwd(q, kpad, vpad, bias, name):
    S, D = q.shape
    HP = D // LANES
    cps = min(ATTN_CHUNKS_PER_STEP, S // CHUNK)
    tq = cps * CHUNK
    scale = HEAD_DIM ** -0.5

    def body(q_ref, k_ref, v_ref, b_ref, o_ref):
        sel0 = lax.broadcasted_iota(jnp.int32, (CHUNK, LANES), 1) < HEAD_DIM
        chunks = _attn_load(pl.program_id(1), cps, q_ref, k_ref, v_ref)
        _, probs = _attn_probs(chunks, b_ref, (sel0, jnp.logical_not(sel0)), scale)
        outs = [[jnp.dot(probs[cc][sub].astype(BF16), v2, preferred_element_type=F32) for sub in range(2)]
                for cc, (_, _, v2, _) in enumerate(chunks)]
        o_ref[...] = jnp.concatenate([jnp.where(sel0, o[0], o[1]) for o in outs], axis=0).astype(BF16)

    kv_spec = pl.BlockSpec((S + LEFT_PAD, LANES), lambda h, j: (0, h))
    return pl.pallas_call(
        body, name=name, grid=(HP, S // tq),
        in_specs=[pl.BlockSpec((tq, LANES), lambda h, j: (j, h)), kv_spec, kv_spec,
                  pl.BlockSpec((2, CHUNK, BAND), lambda h, j: (h, 0, 0))],
        out_specs=pl.BlockSpec((tq, LANES), lambda h, j: (j, h)),
        out_shape=jax.ShapeDtypeStruct((S, D), BF16),
        compiler_params=_params(("parallel", "parallel")),
    )(q, kpad, vpad, bias)


def attn_bwd(q, do, kpad, vpad, bias, dk_in, dv_in, name):
    S, D = q.shape
    HP = D // LANES
    NH = 2 * HP
    cps = min(ATTN_CHUNKS_PER_STEP, S // CHUNK)
    tq = cps * CHUNK
    nj = S // tq
    scale = HEAD_DIM ** -0.5

    def body(q_ref, do_ref, k_ref, v_ref, b_ref, dki_ref, dvi_ref, dq_ref, dk_ref, dv_ref, db_ref, dk_acc, dv_acc):
        j = pl.program_id(1)

        @pl.when(j == 0)
        def _():
            dk_acc[:LEFT_PAD, :] = jnp.zeros((LEFT_PAD, LANES), F32)
            dv_acc[:LEFT_PAD, :] = jnp.zeros((LEFT_PAD, LANES), F32)
            dk_acc[LEFT_PAD:, :] = dki_ref[...]
            dv_acc[LEFT_PAD:, :] = dvi_ref[...]
            db_ref[...] = jnp.zeros_like(db_ref)

        sel0 = lax.broadcasted_iota(jnp.int32, (CHUNK, LANES), 1) < HEAD_DIM
        sels = (sel0, jnp.logical_not(sel0))
        chunks = _attn_load(j, cps, q_ref, k_ref, v_ref)
        pairs = [(cc, sub) for cc in range(cps) for sub in range(2)]
        qms, probs = _attn_probs(chunks, b_ref, sels, scale)
        doms = [[jnp.where(sel, do_ref[cc * CHUNK:(cc + 1) * CHUNK, :], jnp.zeros((CHUNK, LANES), BF16))
                 for sel in sels] for cc in range(cps)]
        dps = {(cc, sub): lax.dot_general(doms[cc][sub], chunks[cc][2], NT, preferred_element_type=F32)
               for cc, sub in pairs}
        dss = {}
        for cc, sub in pairs:
            p = probs[cc][sub]
            dss[cc, sub] = p * (dps[cc, sub] - jnp.sum(dps[cc, sub] * p, axis=-1, keepdims=True))
        dsb = {key: ds.astype(BF16) for key, ds in dss.items()}
        dqs = {(cc, sub): jnp.dot(dsb[cc, sub], chunks[cc][1], preferred_element_type=F32) * scale
               for cc, sub in pairs}
        dks = {(cc, sub): lax.dot_general(dsb[cc, sub], qms[cc][sub], TN, preferred_element_type=F32) * scale
               for cc, sub in pairs}
        dvs = {(cc, sub): lax.dot_general(probs[cc][sub].astype(BF16), doms[cc][sub], TN,
                                          preferred_element_type=F32) for cc, sub in pairs}
        dq_ref[...] = jnp.concatenate([jnp.where(sel0, dqs[cc, 0], dqs[cc, 1]) for cc in range(cps)],
                                      axis=0).astype(BF16)
        for sub in range(2):
            total = dss[0, sub]
            for cc in range(1, cps):
                total = total + dss[cc, sub]
            db_ref[sub] += total
        dk_parts = [dks[cc, 0] + dks[cc, 1] for cc in range(cps)]
        dv_parts = [dvs[cc, 0] + dvs[cc, 1] for cc in range(cps)]

        def window(parts):
            blocks = []
            for rb in range(cps - 1 + BAND // CHUNK):
                acc = None
                for cc in range(cps):
                    b = rb - cc
                    if 0 <= b < BAND // CHUNK:
                        piece = parts[cc][b * CHUNK:(b + 1) * CHUNK, :]
                        acc = piece if acc is None else acc + piece
                blocks.append(acc)
            return jnp.concatenate(blocks, axis=0)

        span = pl.ds(pl.multiple_of(j * cps * CHUNK, CHUNK), (cps - 1) * CHUNK + BAND)
        dk_acc[span, :] += window(dk_parts)
        dv_acc[span, :] += window(dv_parts)

        @pl.when(j == nj - 1)
        def _():
            dk_ref[...] = dk_acc[LEFT_PAD:, :]
            dv_ref[...] = dv_acc[LEFT_PAD:, :]

    q_spec = pl.BlockSpec((tq, LANES), lambda h, j: (j, h))
    kv_spec = pl.BlockSpec((S + LEFT_PAD, LANES), lambda h, j: (0, h))
    col_spec = pl.BlockSpec((S, LANES), lambda h, j: (0, h))
    b_spec = pl.BlockSpec((2, CHUNK, BAND), lambda h, j: (h, 0, 0))
    return pl.pallas_call(
        body, name=name, grid=(HP, nj),
        in_specs=[q_spec, q_spec, kv_spec, kv_spec, b_spec, col_spec, col_spec],
        out_specs=[q_spec, col_spec, col_spec, b_spec],
        out_shape=[jax.ShapeDtypeStruct((S, D), BF16), jax.ShapeDtypeStruct((S, D), F32),
                   jax.ShapeDtypeStruct((S, D), F32), jax.ShapeDtypeStruct((NH, CHUNK, BAND), F32)],
        scratch_shapes=[pltpu.VMEM((S + LEFT_PAD, LANES), F32), pltpu.VMEM((S + LEFT_PAD, LANES), F32)],
        compiler_params=_params(("parallel", "arbitrary")),
    )(q, do, kpad, vpad, bias, dk_in, dv_in)


def _rel_onehot(t):
    r = lax.broadcasted_iota(jnp.int32, (BAND, N_REL), 0)
    i = lax.broadcasted_iota(jnp.int32, (BAND, N_REL), 1)
    idx = jnp.clip(t + LEFT_PAD - r, -(CHUNK - 1), MAX_REL) + (CHUNK - 1)
    return (idx == i).astype(F32)


def bias_expand(rb, name):
    NH = rb.shape[0]

    def body(rb_ref, o_ref):
        def step(t, carry):
            o_ref[t] = lax.dot_general(rb_ref[...], _rel_onehot(t), NT, preferred_element_type=F32,
                                       precision=HIGHEST)
            return carry

        lax.fori_loop(0, CHUNK, step, 0)

    return pl.pallas_call(
        body, name=name, in_specs=[VMEM_SPEC], out_specs=VMEM_SPEC,
        out_shape=jax.ShapeDtypeStruct((CHUNK, NH, BAND), F32), compiler_params=_params(),
    )(rb)


def bias_grad(dsum, name):
    NH = dsum.shape[1]

    def body(d_ref, o_ref):
        def step(t, acc):
            return acc + jnp.dot(d_ref[t], _rel_onehot(t), preferred_element_type=F32, precision=HIGHEST)

        o_ref[...] = lax.fori_loop(0, CHUNK, step, jnp.zeros((NH, N_REL), F32))

    return pl.pallas_call(
        body, name=name, in_specs=[VMEM_SPEC], out_specs=VMEM_SPEC,
        out_shape=jax.ShapeDtypeStruct((NH, N_REL), F32), compiler_params=_params(),
    )(dsum)


def loss_grad(y, tgt, name):
    S, D = y.shape
    tm = min(MATMUL_ROW_TILE, S)

    def body(y_ref, t_ref, d_ref, acc_ref):
        @pl.when(pl.program_id(0) == 0)
        def _():
            acc_ref[...] = jnp.zeros_like(acc_ref)

        err = y_ref[...] - t_ref[...]
        d_ref[...] = err * (1.0 / D)
        acc_ref[0:1, :] += jnp.sum(err * err, axis=0, keepdims=True)

    tile = pl.BlockSpec((tm, D), _row(0))
    return pl.pallas_call(
        body, name=name, grid=(S // tm,), in_specs=[tile, tile],
        out_specs=[tile, pl.BlockSpec((8, D), _fixed2)],
        out_shape=[jax.ShapeDtypeStruct((S, D), F32), jax.ShapeDtypeStruct((8, D), F32)],
        compiler_params=_params(("arbitrary",)),
    )(y, tgt)


def _col_tile(n):
    for t in (768, 512, 256, 128):
        if n % t == 0:
            return t
    return n


def ada_fwd(c_all, w, b, name):
    L, D, n = w.shape
    tn = _col_tile(n)

    def body(c_ref, w_ref, b_ref, o_ref):
        cv = c_ref[...]
        ca = cv * _sigmoid(cv)
        o_ref[...] = jnp.dot(ca, w_ref[...], preferred_element_type=F32, precision=HIGHEST) + b_ref[...]

    return pl.pallas_call(
        body, name=name, grid=(L, n // tn),
        in_specs=[pl.BlockSpec((N_DEV, D), lambda l, j: (0, 0)), pl.BlockSpec((None, D, tn), lambda l, j: (l, 0, j)),
                  pl.BlockSpec((None, 1, tn), lambda l, j: (l, 0, j))],
        out_specs=pl.BlockSpec((None, N_DEV, tn), lambda l, j: (l, 0, j)),
        out_shape=jax.ShapeDtypeStruct((L, N_DEV, n), F32),
        compiler_params=_params(("parallel", "parallel")),
    )(c_all, w, b)


def ada_wgrad(c_all_t, dmod, name):
    L, _, n = dmod.shape
    D = c_all_t.shape[0]
    tn = _col_tile(n)

    def body(c_ref, d_ref, o_ref):
        cv = c_ref[...]
        ca = cv * _sigmoid(cv)
        o_ref[...] = jnp.dot(ca, d_ref[...], preferred_element_type=F32, precision=HIGHEST)

    return pl.pallas_call(
        body, name=name, grid=(L, n // tn),
        in_specs=[pl.BlockSpec((D, N_DEV), lambda l, j: (0, 0)), pl.BlockSpec((None, N_DEV, tn), lambda l, j: (l, 0, j))],
        out_specs=pl.BlockSpec((None, D, tn), lambda l, j: (l, 0, j)),
        out_shape=jax.ShapeDtypeStruct((L, D, n), F32),
        compiler_params=_params(("parallel", "parallel")),
    )(c_all_t, dmod)


ELEMENTWISE_BLOCK_BYTES = 3 * 1024 * 1024


def _elementwise_rows(rows, row_bytes):
    for t in (4096, 2048, 1024, 512, 256, 128, 64, 32, 16):
        if rows % t == 0 and t * row_bytes <= ELEMENTWISE_BLOCK_BYTES:
            return t
    return rows


def sum_leading(a, name):
    n, M, N = a.shape
    tr = _elementwise_rows(M, n * N * 4)

    def body(a_ref, o_ref):
        acc = a_ref[0]
        for i in range(1, n):
            acc = acc + a_ref[i]
        o_ref[...] = acc

    return pl.pallas_call(
        body, name=name, grid=(M // tr,),
        in_specs=[pl.BlockSpec((n, tr, N), lambda i: (0, i, 0))],
        out_specs=pl.BlockSpec((tr, N), _row(0)),
        out_shape=jax.ShapeDtypeStruct((M, N), F32),
        compiler_params=_params(("parallel",)),
    )(a)


def pair_sum(full, recv, half_idx, name):
    n, M, N = recv.shape
    kb = max(k for k in (1, 2, 4) if n % k == 0 and (k == 1 or k * M * N * 2 <= ELEMENTWISE_BLOCK_BYTES))

    def body(h_ref, a_ref, b_ref, o_ref):
        o_ref[...] = (a_ref[...].astype(F32) + b_ref[...].astype(F32)).astype(BF16)

    spec = pl.BlockSpec((kb, M, N), lambda k, h: (k, 0, 0))
    return pl.pallas_call(
        body, name=name,
        grid_spec=pltpu.PrefetchScalarGridSpec(
            num_scalar_prefetch=1, grid=(n // kb,),
            in_specs=[pl.BlockSpec((kb, M, N), lambda k, h: (k, h[0], 0)), spec], out_specs=spec),
        out_shape=jax.ShapeDtypeStruct((n, M, N), BF16),
        compiler_params=_params(("parallel",)),
    )(half_idx, full, recv)


def chip_sum(psum, land, chip_idx, name):
    n, M, N = psum.shape
    tr = _elementwise_rows(M, N * 8)

    def body(s_ref, p_ref, a_ref, b_ref, c_ref, o_ref):
        o_ref[...] = ((p_ref[...].astype(F32) + a_ref[...].astype(F32)) + b_ref[...].astype(F32)) + c_ref[...].astype(F32)

    def entry(j):
        return pl.BlockSpec((None, tr, N), lambda i, s: ((s[0] + j) % n, i, 0))

    return pl.pallas_call(
        body, name=name,
        grid_spec=pltpu.PrefetchScalarGridSpec(
            num_scalar_prefetch=1, grid=(M // tr,),
            in_specs=[entry(0), entry(1), entry(2), entry(3)],
            out_specs=pl.BlockSpec((tr, N), lambda i, s: (i, 0))),
        out_shape=jax.ShapeDtypeStruct((M, N), F32),
        compiler_params=_params(("parallel",)),
    )(chip_idx, psum, land, land, land)


def adamw(w, g, m, v, name):
    M, N = w.shape
    tr = _elementwise_rows(M, N * 4)
    c1 = 1.0 - ADAM_B1 ** ADAM_STEP
    c2 = 1.0 - ADAM_B2 ** ADAM_STEP

    def body(w_ref, g_ref, m_ref, v_ref, d_ref, nm_ref, nv_ref):
        g = g_ref[...]
        nm = ADAM_B1 * m_ref[...] + (1.0 - ADAM_B1) * g
        nv = ADAM_B2 * v_ref[...] + (1.0 - ADAM_B2) * (g * g)
        d_ref[...] = -ADAM_LR * ((nm / c1) / (jnp.sqrt(nv / c2) + ADAM_EPS) + ADAM_WD * w_ref[...])
        nm_ref[...] = nm
        nv_ref[...] = nv

    spec = pl.BlockSpec((tr, N), _row(0))
    shp = jax.ShapeDtypeStruct((M, N), F32)
    return pl.pallas_call(
        body, name=name, grid=(M // tr,), in_specs=[spec] * 4, out_specs=[spec] * 3, out_shape=[shp] * 3,
        compiler_params=_params(("parallel",)),
    )(w, g, m, v)


def _coords():
    return lax.axis_index("x"), lax.axis_index("y"), lax.axis_index("c")


def all_gather8(block, name):
    m_per, n = block.shape

    def body(x_ref, out_ref, send_sems, recv_sems, local_sem):
        x, y, c = _coords()
        me, sibling = (x, y, c), (x, y, 1 - c)
        chips = [(1 - x, y), (x, 1 - y), (1 - x, 1 - y)]

        def rows(px, py, pc):
            return out_ref.at[pl.ds((4 * px + 2 * py + pc) * m_per, m_per), :]

        def copy(k, blk, to, src=None):
            return pltpu.make_async_remote_copy(
                src_ref=rows(*blk) if src is None else src, dst_ref=rows(*blk),
                send_sem=send_sems.at[k], recv_sem=recv_sems.at[k], device_id=to, device_id_type=MESH)

        mine = pltpu.make_async_copy(x_ref, rows(*me), local_sem)
        mine.start()
        first = [copy(0, me, sibling, src=x_ref)]
        first += [copy(1 + j, me, (*chip, c), src=x_ref) for j, chip in enumerate(chips)]
        for cp in first:
            cp.start()
        passed = [copy(4 + j, (*chip, c), sibling) for j, chip in enumerate(chips)]
        for j, chip in enumerate(chips):
            copy(1 + j, (*chip, c), me).wait_recv()
            passed[j].start()
        copy(0, sibling, me).wait_recv()
        for j, chip in enumerate(chips):
            copy(4 + j, (*chip, 1 - c), me).wait_recv()
        for cp in first + passed:
            cp.wait_send()
        mine.wait()

    return pl.pallas_call(
        body, name=name, in_specs=[VMEM_SPEC], out_specs=VMEM_SPEC,
        out_shape=jax.ShapeDtypeStruct((N_DEV * m_per, n), block.dtype),
        scratch_shapes=[pltpu.SemaphoreType.DMA((7,)), pltpu.SemaphoreType.DMA((7,)), pltpu.SemaphoreType.DMA],
        compiler_params=_params(),
    )(block)


def _other_chips(x, y):
    return [(1 - x, y), (x, 1 - y), (1 - x, 1 - y)]


HBM_SPEC = pl.BlockSpec(memory_space=pltpu.HBM)
SEM_SPEC = pl.BlockSpec(memory_space=pltpu.SEMAPHORE)
DATAFLOW = pltpu.SideEffectType.DATAFLOW_SIDE_EFFECTING


def _weight_desc(stack_ref, k, land_ref, px, py, c, me_s):
    h = land_ref.shape[1] // 2
    rows = pl.ds(c * h, h)
    return stack_ref.at[k, rows, :], land_ref.at[me_s, rows, :], land_ref.at[2 * px + py, rows, :]


def _grad_desc(psum_ref, k, land_ref, px, py, c, me_s):
    return psum_ref.at[2 * px + py], land_ref.at[me_s], land_ref.at[2 * px + py]


def exchange_start(srcs, lands, units, groups, desc, after, name):
    n_s, n_l, n_g = len(srcs), len(lands), len(groups)

    def body(*refs):
        s_refs, l_refs = refs[:n_s], refs[n_s:n_s + n_l]
        outs = refs[n_s + n_l + 1:]
        sems, token = outs[:2 * n_g], outs[-1]
        x, y, c = _coords()
        me_s = 2 * x + y
        for g, ids in enumerate(groups):
            for i, u in enumerate(ids):
                si, k = units[u]
                for j, (px, py) in enumerate(_other_chips(x, y)):
                    src, dst, _ = desc(s_refs[si], k, l_refs[u], px, py, c, me_s)
                    pltpu.make_async_remote_copy(
                        src_ref=src, dst_ref=dst, send_sem=sems[2 * g].at[3 * i + j],
                        recv_sem=sems[2 * g + 1].at[3 * i + j], device_id=(px, py, c), device_id_type=MESH).start()
        token[...] = jnp.zeros_like(token)

    arrs = list(srcs) + list(lands)
    sem_shapes = [pltpu.SemaphoreType.DMA((3 * len(ids),)) for ids in groups for _ in range(2)]
    outs = pl.pallas_call(
        body, name=name,
        in_specs=[HBM_SPEC] * len(arrs) + [ANY],
        out_specs=[SEM_SPEC] * (2 * n_g) + [HBM_SPEC] * len(arrs) + [VMEM_SPEC],
        out_shape=sem_shapes + [pltpu.HBM(a.shape, a.dtype) for a in arrs] + [jax.ShapeDtypeStruct((8, LANES), F32)],
        input_output_aliases={i: 2 * n_g + i for i in range(len(arrs))},
        compiler_params=pltpu.CompilerParams(has_side_effects=DATAFLOW),
    )(*[pltpu.with_memory_space_constraint(a, pltpu.HBM) for a in arrs], after)
    sems = outs[:2 * n_g]
    thru = outs[2 * n_g:2 * n_g + len(arrs)]
    return sems, list(thru[:n_s]), list(thru[n_s:]), outs[-1]


def exchange_wait(srcs, lands, units, send_sem, recv_sem, desc, after, name):
    n_s, n_l = len(srcs), len(lands)

    def body(*refs):
        s_refs, l_refs = refs[:n_s], refs[n_s:n_s + n_l]
        send_sems, recv_sems = refs[n_s + n_l], refs[n_s + n_l + 1]
        x, y, c = _coords()
        me_s = 2 * x + y
        for i, (si, k) in enumerate(units):
            for j, (px, py) in enumerate(_other_chips(x, y)):
                src, _, mine = desc(s_refs[si], k, l_refs[i], px, py, c, me_s)
                cp = pltpu.make_async_remote_copy(
                    src_ref=src, dst_ref=mine, send_sem=send_sems.at[3 * i + j], recv_sem=recv_sems.at[3 * i + j],
                    device_id=(px, py, c), device_id_type=MESH)
                cp.wait_send()
                cp.wait_recv()

    arrs = list(srcs) + list(lands)
    outs = pl.pallas_call(
        body, name=name,
        in_specs=[HBM_SPEC] * len(arrs) + [SEM_SPEC, SEM_SPEC, ANY],
        out_specs=[HBM_SPEC] * len(arrs),
        out_shape=[pltpu.HBM(a.shape, a.dtype) for a in arrs],
        input_output_aliases={i: i for i in range(len(arrs))},
        compiler_params=pltpu.CompilerParams(has_side_effects=DATAFLOW),
    )(*arrs, send_sem, recv_sem, after)
    return list(outs[:n_s]), list(outs[n_s:])


def sibling_fill(lands, name):
    n_u = len(lands)

    def body(*refs):
        ins, outs = refs[:n_u], refs[n_u:2 * n_u]
        send_sems, recv_sems = refs[2 * n_u:]
        x, y, c = _coords()
        sends = []
        for u in range(n_u):
            h = ins[u].shape[1] // 2
            for j, (px, py) in enumerate(_other_chips(x, y)):
                part = (2 * px + py, pl.ds(c * h, h), slice(None))
                cp = pltpu.make_async_remote_copy(
                    src_ref=ins[u].at[part], dst_ref=outs[u].at[part], send_sem=send_sems.at[3 * u + j],
                    recv_sem=recv_sems.at[3 * u + j], device_id=(x, y, 1 - c), device_id_type=MESH)
                cp.start()
                sends.append(cp)
        for u in range(n_u):
            h = ins[u].shape[1] // 2
            for j, (px, py) in enumerate(_other_chips(x, y)):
                theirs = (2 * px + py, pl.ds((1 - c) * h, h), slice(None))
                pltpu.make_async_remote_copy(
                    src_ref=ins[u].at[theirs], dst_ref=outs[u].at[theirs], send_sem=send_sems.at[3 * u + j],
                    recv_sem=recv_sems.at[3 * u + j], device_id=(x, y, 1 - c), device_id_type=MESH).wait_recv()
        for cp in sends:
            cp.wait_send()

    return pl.pallas_call(
        body, name=name, in_specs=[ANY] * n_u, out_specs=[ANY] * n_u,
        out_shape=[jax.ShapeDtypeStruct(a.shape, a.dtype) for a in lands],
        input_output_aliases={i: i for i in range(n_u)},
        scratch_shapes=[pltpu.SemaphoreType.DMA((3 * n_u,)), pltpu.SemaphoreType.DMA((3 * n_u,))],
        compiler_params=_params(),
    )(*lands)


def grad_pair_exchange(grads, name):
    n_u = len(grads)

    def body(*refs):
        ins, outs = refs[:n_u], refs[n_u:2 * n_u]
        send_sems, recv_sems = refs[2 * n_u:]
        x, y, c = _coords()
        cps = []
        for u in range(n_u):
            half = ins[u].shape[1] // 2
            cp = pltpu.make_async_remote_copy(
                src_ref=ins[u].at[:, pl.ds((1 - c) * half, half), :], dst_ref=outs[u],
                send_sem=send_sems.at[u], recv_sem=recv_sems.at[u], device_id=(x, y, 1 - c), device_id_type=MESH)
            cp.start()
            cps.append(cp)
        for cp in cps:
            cp.wait()

    out_shape = [jax.ShapeDtypeStruct((g.shape[0], g.shape[1] // 2, g.shape[2]), g.dtype) for g in grads]
    return pl.pallas_call(
        body, name=name, in_specs=[ANY] * n_u, out_specs=[ANY] * n_u, out_shape=out_shape,
        scratch_shapes=[pltpu.SemaphoreType.DMA((n_u,)), pltpu.SemaphoreType.DMA((n_u,))],
        compiler_params=_params(),
    )(*grads)


def grad_half_swap(halves, name):
    n_u = len(halves)

    def body(*refs):
        ins, outs = refs[:n_u], refs[n_u:2 * n_u]
        send_sems, recv_sems = refs[2 * n_u:]
        x, y, c = _coords()
        cps = []
        for u in range(n_u):
            cp = pltpu.make_async_remote_copy(
                src_ref=ins[u], dst_ref=outs[u], send_sem=send_sems.at[u], recv_sem=recv_sems.at[u],
                device_id=(x, y, 1 - c), device_id_type=MESH)
            cp.start()
            cps.append(cp)
        for cp in cps:
            cp.wait()

    return pl.pallas_call(
        body, name=name, in_specs=[ANY] * n_u, out_specs=[ANY] * n_u,
        out_shape=[jax.ShapeDtypeStruct(h.shape, h.dtype) for h in halves],
        scratch_shapes=[pltpu.SemaphoreType.DMA((n_u,)), pltpu.SemaphoreType.DMA((n_u,))],
        compiler_params=_params(),
    )(*halves)


def _pack_rows(parts):
    flat = jnp.concatenate([p.reshape(-1).astype(F32) for p in parts])
    n = flat.shape[0]
    padded = -(-n // (8 * LANES)) * (8 * LANES)
    return jnp.pad(flat, (0, padded - n)).reshape(-1, LANES)


def _unpack_rows(packed, shapes):
    flat = packed.reshape(-1)
    out, off = [], 0
    for s in shapes:
        size = 1
        for d in s:
            size *= d
        out.append(flat[off:off + size].reshape(s))
        off += size
    return out


def _shard_last(full, s_me):
    n = full.shape[-1] // N_CHIP
    return lax.dynamic_slice_in_dim(full, s_me * n, n, axis=full.ndim - 1)


def _unshard_last(g):
    moved = jnp.moveaxis(g, 0, -2)
    return moved.reshape(moved.shape[:-2] + (moved.shape[-2] * moved.shape[-1],))


def kernel(x, c, w_ada, b_ada, ln_g, ln_b, ffn_gu, ffn_down, gmlp_w_in, gmlp_b_in, gmlp_ln_g, gmlp_ln_b, gmlp_w_s, gmlp_b_s, gmlp_w_out, w_ada_kv, b_ada_kv, w_kv, attn_w_q, attn_rel_bias, attn_w_o, loss_target, m_w_ada, m_b_ada, m_ln_g, m_ln_b, m_ffn_gu, m_ffn_down, m_gmlp_w_in, m_gmlp_b_in, m_gmlp_ln_g, m_gmlp_ln_b, m_gmlp_w_s, m_gmlp_b_s, m_gmlp_w_out, m_w_ada_kv, m_b_ada_kv, m_w_kv, m_attn_w_q, m_attn_rel_bias, m_attn_w_o, v_w_ada, v_b_ada, v_ln_g, v_ln_b, v_ffn_gu, v_ffn_down, v_gmlp_w_in, v_gmlp_b_in, v_gmlp_ln_g, v_gmlp_ln_b, v_gmlp_w_s, v_gmlp_b_s, v_gmlp_w_out, v_w_ada_kv, v_b_ada_kv, v_w_kv, v_attn_w_q, v_attn_rel_bias, v_attn_w_o):
    xi, yi, ci = _coords()
    s_me = 2 * xi + yi
    dev = 4 * xi + 2 * yi + ci

    x0 = x[0]
    tgt = loss_target[0]
    S, D = x0.shape
    L = w_ada.shape[0]
    NA = gmlp_w_in.shape[0]
    NB = attn_w_q.shape[0]
    NH = D // HEAD_DIM
    alpha = (2.0 * L) ** 0.25
    n_ada = w_ada.shape[2]
    n_kv = w_ada_kv.shape[1]

    stack_names = ["ffn_gu", "ffn_down", "gmlp_w_in", "gmlp_w_out", "w_kv", "attn_w_q", "attn_w_o"]
    stack_src = dict(ffn_gu=ffn_gu, ffn_down=ffn_down, gmlp_w_in=gmlp_w_in, gmlp_w_out=gmlp_w_out, w_kv=w_kv[None],
                     attn_w_q=attn_w_q, attn_w_o=attn_w_o)
    stacks = [stack_src[nm].astype(BF16).reshape((-1,) + stack_src[nm].shape[-2:]) for nm in stack_names]
    units = [(si, k) for si, st in enumerate(stacks) for k in range(st.shape[0])]
    unit_of = {(stack_names[si], k): u for u, (si, k) in enumerate(units)}
    weight_groups = [[("ffn_gu", 0), ("ffn_down", 0)],
                     ([("gmlp_w_in", 0), ("gmlp_w_out", 0)] if NA > 0 else [("attn_w_q", 0), ("attn_w_o", 0)])
                     + [("ffn_gu", 1), ("ffn_down", 1)] + ([("w_kv", 0)] if NA == 0 else [])]
    for l in range(1, L):
        names = [("w_kv", 0)] if l == NA else []
        names += [("ffn_gu", 2 * l), ("ffn_down", 2 * l)]
        names += [("gmlp_w_in", l), ("gmlp_w_out", l)] if l < NA else [("attn_w_q", l - NA), ("attn_w_o", l - NA)]
        names += [("ffn_gu", 2 * l + 1), ("ffn_down", 2 * l + 1)]
        weight_groups.append(names)
    weight_groups = [[unit_of[n] for n in names] for names in weight_groups]

    c_all = all_gather8(jnp.broadcast_to(c, (8, D)), "ag_c").reshape(N_DEV, 8, D)[:, 0]
    b_ada_sh = lax.dynamic_slice_in_dim(b_ada, s_me * n_ada, n_ada, axis=1)
    b_kv_sh = lax.dynamic_slice_in_dim(b_ada_kv, s_me * n_kv, n_kv, axis=0)
    mod_part = ada_fwd(c_all, w_ada, b_ada_sh[:, None, :], "ada_fwd")
    mkv_part = ada_fwd(c_all, w_ada_kv[None], b_kv_sh[None, None, :], "ada_kv_fwd")
    part = jnp.concatenate([jnp.transpose(mod_part, (1, 0, 2)).reshape(N_DEV, L * n_ada), mkv_part[0]], axis=1)
    width = part.shape[1]
    pad_w = -(-width // LANES) * LANES - width
    all_part = all_gather8(jnp.pad(part, ((0, 0), (0, pad_w))), "ag_mod").reshape(N_DEV, N_DEV, width + pad_w)
    mine = lax.dynamic_index_in_dim(all_part[0::2], dev, axis=1, keepdims=False)
    mod = jnp.transpose(mine[:, :L * n_ada].reshape(N_CHIP, L, n_ada), (1, 0, 2)).reshape(L, N_MOD, D)
    mkv = mine[:, L * n_ada:width].reshape(2, D)

    def mrow(l, k):
        return mod[l, k][None, :]

    small_shapes = [ln_g.shape, ln_b.shape, gmlp_b_in.shape, gmlp_ln_g.shape, gmlp_ln_b.shape, attn_rel_bias.shape]
    small_pack = _pack_rows([ln_g, ln_b, gmlp_b_in, gmlp_ln_g, gmlp_ln_b, attn_rel_bias])
    small_all = all_gather8(small_pack, "ag_small_params").reshape((N_DEV,) + small_pack.shape)[0::2]
    sm = [_unpack_rows(small_all[s], small_shapes) for s in range(N_CHIP)]
    ln_g_f, ln_b_f, b_in_f, gln_g_f, gln_b_f, rel_f = [
        _unshard_last(jnp.stack([sm[s][i] for s in range(N_CHIP)])) for i in range(len(small_shapes))]

    lands0 = [lax.dynamic_update_slice(lax.empty((N_CHIP,) + stacks[si].shape[1:], BF16), stacks[si][k][None],
                                       (s_me, 0, 0)) for si, k in units]
    gathers_done = jnp.concatenate([mod.reshape(-1)[:LANES], small_all.reshape(-1)[:LANES]])
    w_sems, stacks_t, lands_t, _ = exchange_start(stacks, lands0, units, weight_groups, _weight_desc, gathers_done,
                                                  "weight_send_start")
    wg = {}

    def fetch_weights(g, stacks_now, after):
        ids = weight_groups[g]
        stacks_next, got = exchange_wait(stacks_now, [lands_t[u] for u in ids], [units[u] for u in ids],
                                         w_sems[2 * g], w_sems[2 * g + 1], _weight_desc, after,
                                         "weight_send_wait_%d" % g)
        for u, a in zip(ids, sibling_fill(got, "weight_sibling_fill")):
            wg[u] = a
        return stacks_next

    def W(nm, k):
        return wg[unit_of[(nm, k)]]

    def Wrows(nm, k):
        w4 = W(nm, k)
        return w4.reshape(w4.shape[0] * w4.shape[1], w4.shape[2])

    bst = [jnp.transpose(gmlp_b_s[j]) for j in range(NA)]
    biases = {}

    def make_bias(j, dep):
        rel, _ = lax.optimization_barrier((rel_f[j], dep))
        biases[j] = jnp.transpose(bias_expand(rel, "bias_expand"), (1, 0, 2))
        return biases[j]

    saved = []
    xc = x0
    kpad = vpad = xkv = None
    for l in range(L):
        after = mod if l == 0 else xc
        if l == 1 and NB > 1:
            after = make_bias(1, xc)
        stacks_t = fetch_weights(0 if l == 0 else l + 1, stacks_t, after)
        if l == NA:
            xkv = xc
            kv, hkv = mod_matmul(xc, mkv[1][None], mkv[0][None], W("w_kv", 0), None, BF16, "kv_proj")
            kpad = jnp.pad(kv[:, :D], ((LEFT_PAD, 0), (0, 0)))
            vpad = jnp.pad(kv[:, D:], ((LEFT_PAD, 0), (0, 0)))
        sv = {}
        for i in (0, 2):
            k = 2 * l + i // 2
            gu, hv = mod_matmul(xc, mrow(l, 3 * i + 1), mrow(l, 3 * i), W("ffn_gu", k), None, BF16, "ffn_up")
            gw = 0.5 * (1.0 + mrow(l, 3 * i + 2))
            xn, xh, rs, yv, av = matmul_res_ln(gu, Wrows("ffn_down", k), xc, gw, ln_g_f[l, i][None],
                                               ln_b_f[l, i][None], alpha, True, "ffn_down")
            sv[i] = dict(x=xc, h=hv, gu=gu, a=av, xh=xh, rs=rs, y=yv, gw=gw)
            xc = xn
            if i == 0:
                if l == 0:
                    stacks_t = fetch_weights(1, stacks_t, make_bias(0, xc) if NB > 0 else xc)
                gw = 1.0 + mrow(l, 5)
                if l < NA:
                    pre, hv = mod_matmul(xc, mrow(l, 4), mrow(l, 3), W("gmlp_w_in", l), b_in_f[l][None], F32,
                                         "gmlp_in")
                    qv = sgu_fwd(pre, gln_g_f[l][None], gln_b_f[l][None], gmlp_w_s[l], bst[l], "sgu_fwd")
                    xn, xh, rs, yv = matmul_res_ln(qv, Wrows("gmlp_w_out", l), xc, gw, ln_g_f[l, 1][None],
                                                   ln_b_f[l, 1][None], alpha, False, "gmlp_out")
                    sv[1] = dict(x=xc, h=hv, pre=pre, a=qv, xh=xh, rs=rs, y=yv, gw=gw)
                else:
                    j = l - NA
                    if j not in biases:
                        make_bias(j, xc)
                    qh, hv = mod_matmul(xc, mrow(l, 4), mrow(l, 3), Wrows("attn_w_q", j)[None], None, BF16, "attn_q")
                    ov = attn_fwd(qh, kpad, vpad, biases[j], "attn_fwd")
                    xn, xh, rs, yv = matmul_res_ln(ov, Wrows("attn_w_o", j), xc, gw, ln_g_f[l, 1][None],
                                                   ln_b_f[l, 1][None], alpha, False, "attn_out")
                    sv[1] = dict(x=xc, h=hv, q=qh, a=ov, xh=xh, rs=rs, y=yv, gw=gw)
                xc = xn
        saved.append(sv)

    dx, lacc = loss_grad(xc, tgt, "loss_grad")
    loss = lax.psum((0.5 / D) * jnp.sum(lacc[0]), ("x", "y", "c"))

    gfull = [None] * len(units)
    dmod = [[None] * N_MOD for _ in range(L)]
    d_ln_g = [[None] * 3 for _ in range(L)]
    d_ln_b = [[None] * 3 for _ in range(L)]
    d_b_in, d_gln_g, d_gln_b, d_ws, d_bs, d_rel = ([None] * NA, [None] * NA, [None] * NA, [None] * NA, [None] * NA,
                                                  [None] * NB)
    dk = jnp.zeros((S, D), F32)
    dv = jnp.zeros((S, D), F32)
    dmkv = None

    made = []

    def put(nm, k, g):
        gfull[unit_of[(nm, k)]] = g.reshape((N_CHIP, -1, g.shape[-1]))
        made.append(unit_of[(nm, k)])

    own_half, sib_half = {}, {}

    core_idx = ci.astype(jnp.int32).reshape(1)
    chip_idx = s_me.astype(jnp.int32).reshape(1)
    n_started = [0]

    def start_grad_exchange(ids, after):
        from_sibling = grad_pair_exchange([gfull[u] for u in ids], "grad_pair_exchange")
        psums = [pair_sum(gfull[u], fs, core_idx, "grad_pair_sum") for u, fs in zip(ids, from_sibling)]
        n = len(ids)
        tag = n_started[0]
        n_started[0] += 1
        sems, ps_t, q_t, token = exchange_start(psums, [lax.empty(p.shape, p.dtype) for p in psums],
                                                [(i, 0) for i in range(n)], [list(range(n))], _grad_desc, after,
                                                "grad_send_start_%d" % tag)
        return dict(ids=ids, tag=tag, sems=sems, ps=ps_t, q=q_t), token

    def finish_grad_exchange(pend, after):
        n = len(pend["ids"])
        ps_t, q = exchange_wait(pend["ps"], pend["q"], [(i, 0) for i in range(n)], pend["sems"][0], pend["sems"][1],
                                _grad_desc, after, "grad_send_wait_%d" % pend["tag"])
        halves = [chip_sum(ps_t[i], q[i], chip_idx, "grad_chip_sum") for i in range(n)]
        for u, h, s in zip(pend["ids"], halves, grad_half_swap(halves, "grad_half_swap")):
            own_half[u], sib_half[u] = h, s

    pending = None
    order_token = jnp.zeros((), F32)
    for l in reversed(range(L)):
        if l == NA - 1:
            dkv = jnp.concatenate([dk, dv], axis=1)
            put("w_kv", 0, wgrad(hkv, dkv, D, dkv.shape[1] // N_CHIP, "kv_wgrad"))
            dx, acc = dgrad_mod(dkv, W("w_kv", 0), dx, xkv, mkv[1][None], "kv_dgrad")
            dmkv = jnp.stack([acc[1], acc[0]])
        sv = saved[l]
        for i in (2, 1, 0):
            t = sv[i]
            wres = 1.0 if i == 1 else 0.5
            dxa, dy, acc1 = ln_res_bwd(dx, t["xh"], t["rs"], ln_g_f[l, i][None] + order_token, t["y"], t["gw"], wres,
                                       alpha, "ln_res_bwd")
            d_ln_g[l][i], d_ln_b[l][i], dmod[l][3 * i + 2] = acc1[0], acc1[1], acc1[2]
            scl, shf = mrow(l, 3 * i + 1), mrow(l, 3 * i)
            if i != 1:
                k = 2 * l + i // 2
                F = t["gu"].shape[1] // 2
                dgu = ffn_act_bwd(dy, Wrows("ffn_down", k), t["gu"], "ffn_act_bwd")
                put("ffn_down", k, wgrad(t["a"], dy, F // 2, D, "ffn_down_wgrad"))
                put("ffn_gu", k, wgrad(t["h"], dgu, D, 2 * F // N_CHIP, "ffn_up_wgrad"))
                dx, acc2 = dgrad_mod(dgu, W("ffn_gu", k), dxa, t["x"], scl, "ffn_up_dgrad")
            elif l < NA:
                dq = matmul_nt(dy, Wrows("gmlp_w_out", l), "gmlp_out_dgrad")
                put("gmlp_w_out", l, wgrad(t["a"], dy, t["a"].shape[1], D, "gmlp_out_wgrad"))
                dpre, dws_l, dss, dgl, dbin = sgu_bwd(dq, t["pre"], gln_g_f[l][None], gln_b_f[l][None], gmlp_w_s[l],
                                                      bst[l], "sgu_bwd")
                d_ws[l] = dws_l
                d_bs[l] = jnp.transpose(group_lane_sum(dss, "sgu_bias_grad")[:, :GMLP_GROUPS])
                d_gln_g[l], d_gln_b[l], d_b_in[l] = dgl[0], dgl[1], dbin[0]
                put("gmlp_w_in", l, wgrad(t["h"], dpre, D, dpre.shape[1] // N_CHIP, "gmlp_in_wgrad"))
                dx, acc2 = dgrad_mod(dpre, W("gmlp_w_in", l), dxa, t["x"], scl, "gmlp_in_dgrad")
            else:
                j = l - NA
                do = matmul_nt(dy, Wrows("attn_w_o", j), "attn_out_dgrad")
                put("attn_w_o", j, wgrad(t["a"], dy, D, D, "attn_out_wgrad"))
                dqh, dk, dv, dbias = attn_bwd(t["q"], do, kpad, vpad, biases[j], dk, dv, "attn_bwd")
                d_rel[j] = bias_grad(jnp.transpose(dbias, (1, 0, 2)), "bias_grad")
                put("attn_w_q", j, wgrad(t["h"], dqh, D, D, "attn_q_wgrad"))
                dx, acc2 = dgrad_mod(dqh, Wrows("attn_w_q", j)[None], dxa, t["x"], scl, "attn_q_dgrad")
            dmod[l][3 * i + 1], dmod[l][3 * i] = acc2[0], acc2[1]
            if (i == 0 and l > 0) or (i == 1 and l == 0):
                started, token = start_grad_exchange(list(made), dx)
                made.clear()
                if pending is not None:
                    finish_grad_exchange(pending, dx)
                pending = started
                order_token = token[0, 0]
    grad_x = dx[None]

    dvec = _pack_rows([jnp.stack([jnp.stack(r) for r in dmod]), dmkv])
    n_dvec = L * N_MOD * D + 2 * D
    dall = all_gather8(dvec, "ag_dmod").reshape(N_DEV, -1, LANES)
    db_all = sum_leading(dall, "ada_bias_grad").reshape(-1)[:n_dvec]
    g_b_ada = db_all[:L * N_MOD * D].reshape(L, N_MOD * D)
    g_b_ada_kv = db_all[L * N_MOD * D:]
    dall2 = dall.reshape(N_DEV, -1)[:, :n_dvec]
    dmod_all = dall2[:, :L * N_MOD * D].reshape(N_DEV, L, N_MOD * D)
    dmod_sh = jnp.transpose(lax.dynamic_slice_in_dim(dmod_all, s_me * n_ada, n_ada, axis=2), (1, 0, 2))
    dmkv_sh = lax.dynamic_slice_in_dim(dall2[:, L * N_MOD * D:], s_me * n_kv, n_kv, axis=1)[None]
    c_all_t = jnp.transpose(c_all)
    g_w_ada = ada_wgrad(c_all_t, dmod_sh, "ada_wgrad")
    g_w_ada_kv = ada_wgrad(c_all_t, dmkv_sh, "ada_kv_wgrad")[0]

    small_g = [jnp.stack([jnp.stack(r) for r in d_ln_g]), jnp.stack([jnp.stack(r) for r in d_ln_b]),
               jnp.stack(d_b_in), jnp.stack(d_gln_g), jnp.stack(d_gln_b), jnp.stack(d_rel), jnp.stack(d_ws),
               jnp.stack(d_bs)]
    sg_shapes = [a.shape for a in small_g]
    sg_pack = _pack_rows(small_g)
    sg_all = all_gather8(sg_pack, "ag_small_grads").reshape(N_DEV, -1, LANES)
    sg_sum = _unpack_rows(sum_leading(sg_all, "small_grad_sum"), sg_shapes)
    g_ln_g, g_ln_b, g_b_in, g_gln_g, g_gln_b, g_rel = [_shard_last(a, s_me) for a in sg_sum[:6]]
    g_ws, g_bs = sg_sum[6], sg_sum[7]

    last, _ = start_grad_exchange(list(made), sg_all)
    finish_grad_exchange(pending, sg_all)

    grads = dict(w_ada=g_w_ada, b_ada=g_b_ada, ln_g=g_ln_g, ln_b=g_ln_b, gmlp_b_in=g_b_in, gmlp_ln_g=g_gln_g,
                 gmlp_ln_b=g_gln_b, gmlp_w_s=g_ws, gmlp_b_s=g_bs, w_ada_kv=g_w_ada_kv, b_ada_kv=g_b_ada_kv,
                 attn_rel_bias=g_rel)
    weights = dict(w_ada=w_ada, b_ada=b_ada, ln_g=ln_g, ln_b=ln_b, ffn_gu=ffn_gu, ffn_down=ffn_down,
                   gmlp_w_in=gmlp_w_in, gmlp_b_in=gmlp_b_in, gmlp_ln_g=gmlp_ln_g, gmlp_ln_b=gmlp_ln_b,
                   gmlp_w_s=gmlp_w_s, gmlp_b_s=gmlp_b_s, gmlp_w_out=gmlp_w_out, w_ada_kv=w_ada_kv,
                   b_ada_kv=b_ada_kv, w_kv=w_kv, attn_w_q=attn_w_q, attn_rel_bias=attn_rel_bias, attn_w_o=attn_w_o)
    ms = dict(w_ada=m_w_ada, b_ada=m_b_ada, ln_g=m_ln_g, ln_b=m_ln_b, ffn_gu=m_ffn_gu, ffn_down=m_ffn_down,
              gmlp_w_in=m_gmlp_w_in, gmlp_b_in=m_gmlp_b_in, gmlp_ln_g=m_gmlp_ln_g, gmlp_ln_b=m_gmlp_ln_b,
              gmlp_w_s=m_gmlp_w_s, gmlp_b_s=m_gmlp_b_s, gmlp_w_out=m_gmlp_w_out, w_ada_kv=m_w_ada_kv,
              b_ada_kv=m_b_ada_kv, w_kv=m_w_kv, attn_w_q=m_attn_w_q, attn_rel_bias=m_attn_rel_bias,
              attn_w_o=m_attn_w_o)
    vs = dict(w_ada=v_w_ada, b_ada=v_b_ada, ln_g=v_ln_g, ln_b=v_ln_b, ffn_gu=v_ffn_gu, ffn_down=v_ffn_down,
              gmlp_w_in=v_gmlp_w_in, gmlp_b_in=v_gmlp_b_in, gmlp_ln_g=v_gmlp_ln_g, gmlp_ln_b=v_gmlp_ln_b,
              gmlp_w_s=v_gmlp_w_s, gmlp_b_s=v_gmlp_b_s, gmlp_w_out=v_gmlp_w_out, w_ada_kv=v_w_ada_kv,
              b_ada_kv=v_b_ada_kv, w_kv=v_w_kv, attn_w_q=v_attn_w_q, attn_rel_bias=v_attn_rel_bias,
              attn_w_o=v_attn_w_o)
    order = ["w_ada", "b_ada", "ln_g", "ln_b", "ffn_gu", "ffn_down", "gmlp_w_in", "gmlp_b_in", "gmlp_ln_g",
             "gmlp_ln_b", "gmlp_w_s", "gmlp_b_s", "gmlp_w_out", "w_ada_kv", "b_ada_kv", "w_kv", "attn_w_q",
             "attn_rel_bias", "attn_w_o"]
    big_names = ["w_ada", "w_ada_kv"] + stack_names
    small_names = [nm for nm in order if nm not in big_names]
    delta, new_m, new_v = {}, {}, {}

    def adamw_big(nm):
        shp = weights[nm].shape
        two_d = (-1, shp[-1])
        d, a, b = adamw(weights[nm].reshape(two_d), grads[nm].reshape(two_d), ms[nm].reshape(two_d),
                        vs[nm].reshape(two_d), "adamw")
        delta[nm], new_m[nm], new_v[nm] = d.reshape(shp), a.reshape(shp), b.reshape(shp)

    adamw_big("w_ada")
    adamw_big("w_ada_kv")
    shapes = [weights[nm].shape for nm in small_names]
    d, a, b = adamw(_pack_rows([weights[nm] for nm in small_names]), _pack_rows([grads[nm] for nm in small_names]),
                    _pack_rows([ms[nm] for nm in small_names]), _pack_rows([vs[nm] for nm in small_names]),
                    "adamw_small")
    for nm, dd, aa, bb in zip(small_names, _unpack_rows(d, shapes), _unpack_rows(a, shapes), _unpack_rows(b, shapes)):
        delta[nm], new_m[nm], new_v[nm] = dd, aa, bb

    def full_grad(u):
        lo = jnp.where(ci == 0, own_half[u], sib_half[u])
        hi = jnp.where(ci == 0, sib_half[u], own_half[u])
        return jnp.concatenate([lo, hi], axis=0)

    def adamw_stack(nm):
        si = stack_names.index(nm)
        g = jnp.stack([full_grad(unit_of[(nm, k)]) for k in range(stacks[si].shape[0])])
        grads[nm] = g.reshape(weights[nm].shape)
        adamw_big(nm)

    late = [stack_names[units[u][0]] for u in last["ids"]]
    early = [nm for nm in stack_names if nm not in late]
    for nm in early:
        adamw_stack(nm)
    finish_grad_exchange(last, delta[early[-1]])
    for nm in stack_names:
        if nm in late:
            adamw_stack(nm)

    return (loss, grad_x, *[grads[nm] for nm in order], *[delta[nm] for nm in order],
            *[new_m[nm] for nm in order], *[new_v[nm] for nm in order])
```

```python
import functools

import jax
import jax.numpy as jnp
from jax import lax
from jax.experimental import pallas as pl
from jax.experimental.pallas import tpu as pltpu

F32 = jnp.float32
BF16 = jnp.bfloat16
MESH = pl.DeviceIdType.MESH
HIGHEST = lax.Precision.HIGHEST

CHUNK = 64
GMLP_WINDOW = 128
GMLP_GROUPS = 8
HEAD_DIM = 64
LEFT_CHUNKS = 8
BAND = (LEFT_CHUNKS + 1) * CHUNK
LEFT_PAD = LEFT_CHUNKS * CHUNK
MAX_REL = 4 * CHUNK
N_REL = (CHUNK - 1) + MAX_REL + 1
LN_EPS = 1e-5
N_MOD = 9
N_DEV = 8
N_CHIP = 4

ADAM_LR = 0.001
ADAM_B1 = 0.9
ADAM_B2 = 0.999
ADAM_EPS = 1e-08
ADAM_WD = 0.01
ADAM_STEP = 10

LANES = 128
ROW_TILE = 256
MATMUL_ROW_TILE = 512
WGRAD_ROWS = 1024
ATTN_CHUNKS_PER_STEP = 4
VMEM_LIMIT_MB = 56

NT = (((1,), (1,)), ((), ()))
TN = (((0,), (0,)), ((), ()))

ANY = pl.BlockSpec(memory_space=pl.ANY)
VMEM_SPEC = pl.BlockSpec(memory_space=pltpu.VMEM)


def _params(semantics=None):
    kw = dict(vmem_limit_bytes=VMEM_LIMIT_MB * 1024 * 1024)
    if semantics is not None:
        kw["dimension_semantics"] = semantics
    return pltpu.CompilerParams(**kw)


def _sigmoid(v):
    return 0.5 * (1.0 + jnp.tanh(0.5 * v))


def _gelu(v):
    return 0.5 * v * (1.0 + lax.erf(v * (2.0 ** -0.5)))


def _gelu_grad(v):
    return 0.5 * (1.0 + lax.erf(v * (2.0 ** -0.5))) + v * jnp.exp(-0.5 * v * v) * ((2.0 * jnp.pi) ** -0.5)


def _row(m):
    return lambda i: (i, 0)


def _fixed2(i):
    return (0, 0)


def _fixed3(i):
    return (0, 0, 0)


def _resident(shape):
    return pl.BlockSpec(shape, _fixed2 if len(shape) == 2 else _fixed3, pipeline_mode=pl.Buffered(1))


def mod_matmul(x, scl, shift, w, bias, out_dtype, name):
    S, D = x.shape
    NS, _, n = w.shape
    tm = min(MATMUL_ROW_TILE, S)
    has_bias = bias is not None

    def body(*refs):
        if has_bias:
            x_ref, scl_ref, sh_ref, w_ref, b_ref, o_ref, h_ref = refs
        else:
            x_ref, scl_ref, sh_ref, w_ref, o_ref, h_ref = refs
        h = (x_ref[...] * (1.0 + scl_ref[...]) + sh_ref[...]).astype(BF16)
        h_ref[...] = h
        for s in range(NS):
            acc = jnp.dot(h, w_ref[s], preferred_element_type=F32)
            if has_bias:
                acc = acc + b_ref[:, s * n:(s + 1) * n]
            o_ref[:, s * n:(s + 1) * n] = acc.astype(out_dtype)

    in_specs = [pl.BlockSpec((tm, D), _row(0)), pl.BlockSpec((1, D), _fixed2), pl.BlockSpec((1, D), _fixed2),
                _resident((NS, D, n))]
    args = [x, scl, shift, w]
    if has_bias:
        in_specs.append(pl.BlockSpec((1, NS * n), _fixed2))
        args.append(bias)
    return pl.pallas_call(
        body, name=name, grid=(S // tm,), in_specs=in_specs,
        out_specs=[pl.BlockSpec((tm, NS * n), _row(0)), pl.BlockSpec((tm, D), _row(0))],
        out_shape=[jax.ShapeDtypeStruct((S, NS * n), out_dtype), jax.ShapeDtypeStruct((S, D), BF16)],
        compiler_params=_params(("parallel",)),
    )(*args)


def matmul_res_ln(a, w, x, gw, lg, lb, alpha, swiglu, name):
    S, D = x.shape
    K = w.shape[0]
    tm = min(ROW_TILE, S)
    ka = a.shape[1]

    def body(a_ref, w_ref, x_ref, gw_ref, lg_ref, lb_ref, xn_ref, xh_ref, rs_ref, y_ref, *act_ref):
        if swiglu:
            g = a_ref[:, :K].astype(F32)
            u = a_ref[:, K:].astype(F32)
            act = (g * _sigmoid(g) * u).astype(BF16)
            act_ref[0][...] = act
        else:
            act = a_ref[...].astype(BF16)
        y = jnp.dot(act, w_ref[...], preferred_element_type=F32)
        z = alpha * x_ref[...] + gw_ref[...] * y
        mu = jnp.mean(z, axis=-1, keepdims=True)
        zc = z - mu
        var = jnp.mean(zc * zc, axis=-1, keepdims=True)
        rstd = lax.rsqrt(var + LN_EPS)
        xhat = zc * rstd
        xn_ref[...] = xhat * lg_ref[...] + lb_ref[...]
        xh_ref[...] = xhat
        rs_ref[...] = rstd
        y_ref[...] = y.astype(BF16)

    vec = pl.BlockSpec((1, D), _fixed2)
    out_specs = [pl.BlockSpec((tm, D), _row(0)), pl.BlockSpec((tm, D), _row(0)), pl.BlockSpec((tm, 1), _row(0)),
                 pl.BlockSpec((tm, D), _row(0))]
    out_shape = [jax.ShapeDtypeStruct((S, D), F32), jax.ShapeDtypeStruct((S, D), F32),
                 jax.ShapeDtypeStruct((S, 1), F32), jax.ShapeDtypeStruct((S, D), BF16)]
    if swiglu:
        out_specs.append(pl.BlockSpec((tm, K), _row(0)))
        out_shape.append(jax.ShapeDtypeStruct((S, K), BF16))
    return pl.pallas_call(
        body, name=name, grid=(S // tm,),
        in_specs=[pl.BlockSpec((tm, ka), _row(0)), _resident((K, D)), pl.BlockSpec((tm, D), _row(0)),
                  vec, vec, vec],
        out_specs=out_specs, out_shape=out_shape,
        compiler_params=_params(("parallel",)),
    )(a, w, x, gw, lg, lb)


def _ln_res_bwd_tile(d, xh_ref, rs_ref, lg_ref, y_ref, gw_ref, wres, alpha, dxa_ref, dy_ref, acc_ref, row0):
    xh = xh_ref[...]
    dxh = d * lg_ref[...]
    m1 = jnp.mean(dxh, axis=-1, keepdims=True)
    m2 = jnp.mean(dxh * xh, axis=-1, keepdims=True)
    dz = rs_ref[...] * (dxh - m1 - xh * m2)
    dxa_ref[...] = alpha * dz
    dy_ref[...] = (gw_ref[...] * dz).astype(BF16)
    acc_ref[row0:row0 + 1, :] += jnp.sum(d * xh, axis=0, keepdims=True)
    acc_ref[row0 + 1:row0 + 2, :] += jnp.sum(d, axis=0, keepdims=True)
    acc_ref[row0 + 2:row0 + 3, :] += jnp.sum((wres * dz) * y_ref[...].astype(F32), axis=0, keepdims=True)


def ln_res_bwd(dxn, xhat, rstd, lg, y, gw, wres, alpha, name):
    S, D = dxn.shape
    tm = min(MATMUL_ROW_TILE, S)

    def body(dxn_ref, xh_ref, rs_ref, lg_ref, y_ref, gw_ref, dxa_ref, dy_ref, acc_ref):
        @pl.when(pl.program_id(0) == 0)
        def _():
            acc_ref[...] = jnp.zeros_like(acc_ref)

        _ln_res_bwd_tile(dxn_ref[...], xh_ref, rs_ref, lg_ref, y_ref, gw_ref, wres, alpha, dxa_ref, dy_ref, acc_ref, 0)

    vec = pl.BlockSpec((1, D), _fixed2)
    tile = pl.BlockSpec((tm, D), _row(0))
    return pl.pallas_call(
        body, name=name, grid=(S // tm,),
        in_specs=[tile, tile, pl.BlockSpec((tm, 1), _row(0)), vec, tile, vec],
        out_specs=[tile, tile, pl.BlockSpec((8, D), _fixed2)],
        out_shape=[jax.ShapeDtypeStruct((S, D), F32), jax.ShapeDtypeStruct((S, D), BF16),
                   jax.ShapeDtypeStruct((8, D), F32)],
        compiler_params=_params(("arbitrary",)),
    )(dxn, xhat, rstd, lg, y, gw)


def ffn_act_bwd(dy, wd, gu, name):
    S, D = dy.shape
    K = wd.shape[0]
    tm = min(ROW_TILE, S)

    def body(dy_ref, wd_ref, gu_ref, o_ref):
        da = lax.dot_general(dy_ref[...], wd_ref[...], NT, preferred_element_type=F32)
        g = gu_ref[:, :K].astype(F32)
        u = gu_ref[:, K:].astype(F32)
        sg = _sigmoid(g)
        o_ref[:, :K] = (da * u * (sg * (1.0 + g * (1.0 - sg)))).astype(BF16)
        o_ref[:, K:] = (da * (g * sg)).astype(BF16)

    return pl.pallas_call(
        body, name=name, grid=(S // tm,),
        in_specs=[pl.BlockSpec((tm, D), _row(0)), _resident((K, D)), pl.BlockSpec((tm, 2 * K), _row(0))],
        out_specs=pl.BlockSpec((tm, 2 * K), _row(0)),
        out_shape=jax.ShapeDtypeStruct((S, 2 * K), BF16),
        compiler_params=_params(("parallel",)),
    )(dy, wd, gu)


def matmul_nt(a, w, name):
    S, D = a.shape
    K = w.shape[0]
    tm = min(MATMUL_ROW_TILE, S)

    def body(a_ref, w_ref, o_ref):
        o_ref[...] = lax.dot_general(a_ref[...], w_ref[...], NT, preferred_element_type=F32).astype(BF16)

    return pl.pallas_call(
        body, name=name, grid=(S // tm,),
        in_specs=[pl.BlockSpec((tm, D), _row(0)), _resident((K, D))],
        out_specs=pl.BlockSpec((tm, K), _row(0)),
        out_shape=jax.ShapeDtypeStruct((S, K), BF16),
        compiler_params=_params(("parallel",)),
    )(a, w)


def dgrad_mod(dpre, w, dxa, xin, scl, prev, alpha, name):
    S, D = xin.shape
    NS, _, n = w.shape
    tm = min(ROW_TILE, S)
    wres = prev[5] if prev is not None else None

    def body(*refs):
        dp_ref, w_ref, dxa_ref, xin_ref, scl_ref = refs[:5]
        acc_ref = refs[-1]

        @pl.when(pl.program_id(0) == 0)
        def _():
            acc_ref[...] = jnp.zeros_like(acc_ref)

        dh = jnp.zeros((tm, D), F32)
        for s in range(NS):
            dh = dh + lax.dot_general(dp_ref[:, s * n:(s + 1) * n].astype(BF16), w_ref[s], NT,
                                      preferred_element_type=F32)
        dx = dxa_ref[...] + dh * (1.0 + scl_ref[...])
        acc_ref[0:1, :] += jnp.sum(dh * xin_ref[...], axis=0, keepdims=True)
        acc_ref[1:2, :] += jnp.sum(dh, axis=0, keepdims=True)
        if prev is None:
            refs[5][...] = dx
        else:
            xh_ref, rs_ref, lg_ref, y_ref, gw_ref, pdxa_ref, pdy_ref = refs[5:12]
            _ln_res_bwd_tile(dx, xh_ref, rs_ref, lg_ref, y_ref, gw_ref, wres, alpha, pdxa_ref, pdy_ref, acc_ref, 2)

    tile = pl.BlockSpec((tm, D), _row(0))
    vec = pl.BlockSpec((1, D), _fixed2)
    in_specs = [pl.BlockSpec((tm, NS * n), _row(0)), _resident((NS, D, n)), tile, tile, vec]
    args = [dpre, w, dxa, xin, scl]
    if prev is None:
        out_specs = [tile]
        out_shape = [jax.ShapeDtypeStruct((S, D), F32)]
    else:
        in_specs += [tile, pl.BlockSpec((tm, 1), _row(0)), vec, tile, vec]
        args += list(prev[:5])
        out_specs = [tile, tile]
        out_shape = [jax.ShapeDtypeStruct((S, D), F32), jax.ShapeDtypeStruct((S, D), BF16)]
    return pl.pallas_call(
        body, name=name, grid=(S // tm,), in_specs=in_specs,
        out_specs=out_specs + [pl.BlockSpec((8, D), _fixed2)],
        out_shape=out_shape + [jax.ShapeDtypeStruct((8, D), F32)],
        compiler_params=_params(("arbitrary",)),
    )(*args)


def wgrad_pair(a, b, J, kb, nb, a_block, b_block, half_idx, name):
    S = a.shape[0]
    ts = min(WGRAD_ROWS, S)
    nsteps = S // ts

    def body(h_ref, a_ref, b_ref, o_ref, acc_ref, send_buf, recv_buf, send_sems, recv_sems):
        jj, si = pl.program_id(0), pl.program_id(1)
        x, y, c = _coords()
        j = lax.rem(jj, J)
        last = si == nsteps - 1

        def copy(blk):
            return pltpu.make_async_remote_copy(
                src_ref=send_buf.at[blk], dst_ref=recv_buf.at[blk], send_sem=send_sems.at[blk],
                recv_sem=recv_sems.at[blk], device_id=(x, y, 1 - c), device_id_type=MESH)

        @pl.when(si == 0)
        def _():
            acc_ref[...] = jnp.zeros_like(acc_ref)

        acc_ref[...] += lax.dot_general(a_ref[...], b_ref[...].astype(BF16), TN, preferred_element_type=F32)

        @pl.when(jnp.logical_and(last, jj < J))
        def _():
            send_buf[j] = acc_ref[...].astype(BF16)
            copy(j).start()

        @pl.when(jnp.logical_and(last, jj >= J))
        def _():
            copy(j).wait_recv()
            o_ref[...] = (acc_ref[...] + recv_buf[j].astype(F32)).astype(BF16)

        @pl.when(jnp.logical_and(last, jj == 2 * J - 1))
        def _():
            for blk in range(J):
                copy(blk).wait_send()

    def half(jj, h):
        return jnp.where(jj < J, 1 - h[0], h[0])

    return pl.pallas_call(
        body, name=name,
        grid_spec=pltpu.PrefetchScalarGridSpec(
            num_scalar_prefetch=1, grid=(2 * J, nsteps),
            in_specs=[pl.BlockSpec((ts, kb), lambda jj, s, h: (s, a_block(lax.rem(jj, J), half(jj, h)))),
                      pl.BlockSpec((ts, nb), lambda jj, s, h: (s, b_block(lax.rem(jj, J), half(jj, h))))],
            out_specs=pl.BlockSpec((None, kb, nb), lambda jj, s, h: (jnp.maximum(jj - J, 0), 0, 0)),
            scratch_shapes=[pltpu.VMEM((kb, nb), F32), pltpu.VMEM((J, kb, nb), BF16), pltpu.VMEM((J, kb, nb), BF16),
                            pltpu.SemaphoreType.DMA((J,)), pltpu.SemaphoreType.DMA((J,))]),
        out_shape=jax.ShapeDtypeStruct((J, kb, nb), BF16),
        compiler_params=_params(("arbitrary", "arbitrary")),
    )(half_idx, a, b)


def _window_mask():
    t = lax.broadcasted_iota(jnp.int32, (GMLP_WINDOW, GMLP_WINDOW), 0)
    s = lax.broadcasted_iota(jnp.int32, (GMLP_WINDOW, GMLP_WINDOW), 1)
    return ((s // CHUNK) <= (t // CHUNK)).astype(F32)


def sgu_fwd(pre, glg, glb, ws, bst, name):
    S, H2 = pre.shape
    H = H2 // 2
    W, G = GMLP_WINDOW, GMLP_GROUPS
    gd = H // G
    tm = min(ROW_TILE, S)

    def body(pre_ref, glg_ref, glb_ref, ws_ref, bst_ref, q_ref):
        u = _gelu(pre_ref[:, :H])
        v = _gelu(pre_ref[:, H:])
        mu = jnp.mean(v, axis=-1, keepdims=True)
        vc = v - mu
        var = jnp.mean(vc * vc, axis=-1, keepdims=True)
        vn = ((vc * lax.rsqrt(var + LN_EPS)) * glg_ref[...] + glb_ref[...]).astype(BF16)
        mask = _window_mask()
        for g in range(G):
            wsg = (ws_ref[g] * mask).astype(BF16)
            bcol = bst_ref[:, g:g + 1]
            for wi in range(tm // W):
                rows = slice(wi * W, (wi + 1) * W)
                cols = slice(g * gd, (g + 1) * gd)
                s = jnp.dot(wsg, vn[rows, cols], preferred_element_type=F32) + bcol
                q_ref[rows, cols] = (u[rows, cols] * s).astype(BF16)

    return pl.pallas_call(
        body, name=name, grid=(S // tm,),
        in_specs=[pl.BlockSpec((tm, H2), _row(0)), pl.BlockSpec((1, H), _fixed2), pl.BlockSpec((1, H), _fixed2),
                  pl.BlockSpec((G, W, W), _fixed3), pl.BlockSpec((W, G), _fixed2)],
        out_specs=pl.BlockSpec((tm, H), _row(0)),
        out_shape=jax.ShapeDtypeStruct((S, H), BF16),
        compiler_params=_params(("parallel",)),
    )(pre, glg, glb, ws, bst)


def sgu_bwd(dq, pre, glg, glb, ws, bst, name):
    S, H2 = pre.shape
    H = H2 // 2
    W, G = GMLP_WINDOW, GMLP_GROUPS
    gd = H // G
    tm = min(ROW_TILE, S)

    def body(dq_ref, pre_ref, glg_ref, glb_ref, ws_ref, bst_ref,
             dpre_ref, dws_ref, dss_ref, dgl_ref, dbin_ref, du_s, dvn_s):
        @pl.when(pl.program_id(0) == 0)
        def _():
            dws_ref[...] = jnp.zeros_like(dws_ref)
            dss_ref[...] = jnp.zeros_like(dss_ref)
            dgl_ref[...] = jnp.zeros_like(dgl_ref)
            dbin_ref[...] = jnp.zeros_like(dbin_ref)

        pu = pre_ref[:, :H]
        pv = pre_ref[:, H:]
        u = _gelu(pu)
        v = _gelu(pv)
        mu = jnp.mean(v, axis=-1, keepdims=True)
        vc = v - mu
        var = jnp.mean(vc * vc, axis=-1, keepdims=True)
        rstd = lax.rsqrt(var + LN_EPS)
        vhat = vc * rstd
        vn = (vhat * glg_ref[...] + glb_ref[...]).astype(BF16)
        mask = _window_mask()
        for g in range(G):
            wsg = (ws_ref[g] * mask).astype(BF16)
            bcol = bst_ref[:, g:g + 1]
            cols = slice(g * gd, (g + 1) * gd)
            for wi in range(tm // W):
                rows = slice(wi * W, (wi + 1) * W)
                vblk = vn[rows, cols]
                s = jnp.dot(wsg, vblk, preferred_element_type=F32) + bcol
                dqb = dq_ref[rows, cols].astype(F32)
                du_s[rows, cols] = dqb * s
                ds = dqb * u[rows, cols]
                dss_ref[:, cols] += ds
                dsb = ds.astype(BF16)
                dvn_s[rows, cols] = lax.dot_general(wsg, dsb, TN, preferred_element_type=F32)
                dws_ref[g] += lax.dot_general(dsb, vblk, NT, preferred_element_type=F32) * mask
        dvn = dvn_s[...]
        dgl_ref[0:1, :] += jnp.sum(dvn * vhat, axis=0, keepdims=True)
        dgl_ref[1:2, :] += jnp.sum(dvn, axis=0, keepdims=True)
        dvh = dvn * glg_ref[...]
        m1 = jnp.mean(dvh, axis=-1, keepdims=True)
        m2 = jnp.mean(dvh * vhat, axis=-1, keepdims=True)
        dv = rstd * (dvh - m1 - vhat * m2)
        dpu = du_s[...] * _gelu_grad(pu)
        dpv = dv * _gelu_grad(pv)
        dbin_ref[0:1, :H] += jnp.sum(dpu, axis=0, keepdims=True)
        dbin_ref[0:1, H:] += jnp.sum(dpv, axis=0, keepdims=True)
        dpre_ref[:, :H] = dpu.astype(BF16)
        dpre_ref[:, H:] = dpv.astype(BF16)

    return pl.pallas_call(
        body, name=name, grid=(S // tm,),
        in_specs=[pl.BlockSpec((tm, H), _row(0)), pl.BlockSpec((tm, H2), _row(0)), pl.BlockSpec((1, H), _fixed2),
                  pl.BlockSpec((1, H), _fixed2), pl.BlockSpec((G, W, W), _fixed3), pl.BlockSpec((W, G), _fixed2)],
        out_specs=[pl.BlockSpec((tm, H2), _row(0)), pl.BlockSpec((G, W, W), _fixed3), pl.BlockSpec((W, H), _fixed2),
                   pl.BlockSpec((8, H), _fixed2), pl.BlockSpec((8, H2), _fixed2)],
        out_shape=[jax.ShapeDtypeStruct((S, H2), BF16), jax.ShapeDtypeStruct((G, W, W), F32),
                   jax.ShapeDtypeStruct((W, H), F32), jax.ShapeDtypeStruct((8, H), F32),
                   jax.ShapeDtypeStruct((8, H2), F32)],
        scratch_shapes=[pltpu.VMEM((tm, H), F32), pltpu.VMEM((tm, H), F32)],
        compiler_params=_params(("arbitrary",)),
    )(dq, pre, glg, glb, ws, bst)


def group_lane_sum(dss, name):
    W, H = dss.shape
    gd = H // GMLP_GROUPS

    def body(d_ref, o_ref):
        j = lax.broadcasted_iota(jnp.int32, (H, LANES), 0)
        g = lax.broadcasted_iota(jnp.int32, (H, LANES), 1)
        ind = ((j // gd) == g).astype(F32)
        o_ref[...] = jnp.dot(d_ref[...], ind, preferred_element_type=F32, precision=HIGHEST)

    return pl.pallas_call(
        body, name=name, in_specs=[VMEM_SPEC], out_specs=VMEM_SPEC,
        out_shape=jax.ShapeDtypeStruct((W, LANES), F32), compiler_params=_params(),
    )(dss)


def _attn_load(j, cps, q_ref, k_ref, v_ref):
    r = lax.broadcasted_iota(jnp.int32, (CHUNK, BAND), 1)
    chunks = []
    for cc in range(cps):
        start = pl.multiple_of((j * cps + cc) * CHUNK, CHUNK)
        chunks.append((q_ref[cc * CHUNK:(cc + 1) * CHUNK, :], k_ref[pl.ds(start, BAND), :],
                       v_ref[pl.ds(start, BAND), :], (r + start) >= LEFT_PAD))
    return chunks


def _attn_probs(chunks, b_ref, sels, scale):
    qms = [[jnp.where(sel, q2, jnp.zeros_like(q2)) for sel in sels] for q2, _, _, _ in chunks]
    raw = [[lax.dot_general(qm, k2, NT, preferred_element_type=F32) for qm in qms[cc]]
           for cc, (_, k2, _, _) in enumerate(chunks)]
    probs = []
    for cc, (_, _, _, valid) in enumerate(chunks):
        row = []
        for sub in range(2):
            s = jnp.where(valid, raw[cc][sub] * scale + b_ref[sub], -jnp.inf)
            e = jnp.exp(s - jnp.max(s, axis=-1, keepdims=True))
            row.append(e / jnp.sum(e, axis=-1, keepdims=True))
        probs.append(row)
    return qms, probs


def attn_fwd(q, kpad, vpad, bias, name):
    S, D = q.shape
    HP = D // LANES
    cps = min(ATTN_CHUNKS_PER_STEP, S // CHUNK)
    tq = cps * CHUNK
    scale = HEAD_DIM ** -0.5

    def body(q_ref, k_ref, v_ref, b_ref, o_ref):
        sel0 = lax.broadcasted_iota(jnp.int32, (CHUNK, LANES), 1) < HEAD_DIM
        chunks = _attn_load(pl.program_id(1), cps, q_ref, k_ref, v_ref)
        _, probs = _attn_probs(chunks, b_ref, (sel0, jnp.logical_not(sel0)), scale)
        outs = [[jnp.dot(probs[cc][sub].astype(BF16), v2, preferred_element_type=F32) for sub in range(2)]
                for cc, (_, _, v2, _) in enumerate(chunks)]
        o_ref[...] = jnp.concatenate([jnp.where(sel0, o[0], o[1]) for o in outs], axis=0).astype(BF16)

    kv_spec = pl.BlockSpec((S + LEFT_PAD, LANES), lambda h, j: (0, h))
    return pl.pallas_call(
        body, name=name, grid=(HP, S // tq),
        in_specs=[pl.BlockSpec((tq, LANES), lambda h, j: (j, h)), kv_spec, kv_spec,
                  pl.BlockSpec((2, CHUNK, BAND), lambda h, j: (h, 0, 0))],
        out_specs=pl.BlockSpec((tq, LANES), lambda h, j: (j, h)),
        out_shape=jax.ShapeDtypeStruct((S, D), BF16),
        compiler_params=_params(("parallel", "parallel")),
    )(q, kpad, vpad, bias)


def attn_bwd(q, do, kpad, vpad, bias, dk_in, dv_in, name):
    S, D = q.shape
    HP = D // LANES
    NH = 2 * HP
    cps = min(ATTN_CHUNKS_PER_STEP, S // CHUNK)
    tq = cps * CHUNK
    nj = S // tq
    scale = HEAD_DIM ** -0.5

    def body(q_ref, do_ref, k_ref, v_ref, b_ref, dki_ref, dvi_ref, dq_ref, dk_ref, dv_ref, db_ref, dk_acc, dv_acc):
        j = pl.program_id(1)

        @pl.when(j == 0)
        def _():
            dk_acc[:LEFT_PAD, :] = jnp.zeros((LEFT_PAD, LANES), F32)
            dv_acc[:LEFT_PAD, :] = jnp.zeros((LEFT_PAD, LANES), F32)
            dk_acc[LEFT_PAD:, :] = dki_ref[...]
            dv_acc[LEFT_PAD:, :] = dvi_ref[...]
            db_ref[...] = jnp.zeros_like(db_ref)

        sel0 = lax.broadcasted_iota(jnp.int32, (CHUNK, LANES), 1) < HEAD_DIM
        sels = (sel0, jnp.logical_not(sel0))
        chunks = _attn_load(j, cps, q_ref, k_ref, v_ref)
        pairs = [(cc, sub) for cc in range(cps) for sub in range(2)]
        qms, probs = _attn_probs(chunks, b_ref, sels, scale)
        doms = [[jnp.where(sel, do_ref[cc * CHUNK:(cc + 1) * CHUNK, :], jnp.zeros((CHUNK, LANES), BF16))
                 for sel in sels] for cc in range(cps)]
        dps = {(cc, sub): lax.dot_general(doms[cc][sub], chunks[cc][2], NT, preferred_element_type=F32)
               for cc, sub in pairs}
        dss = {}
        for cc, sub in pairs:
            p = probs[cc][sub]
            dss[cc, sub] = p * (dps[cc, sub] - jnp.sum(dps[cc, sub] * p, axis=-1, keepdims=True))
        dsb = {key: ds.astype(BF16) for key, ds in dss.items()}
        dqs = {(cc, sub): jnp.dot(dsb[cc, sub], chunks[cc][1], preferred_element_type=F32) * scale
               for cc, sub in pairs}
        dks = {(cc, sub): lax.dot_general(dsb[cc, sub], qms[cc][sub], TN, preferred_element_type=F32) * scale
               for cc, sub in pairs}
        dvs = {(cc, sub): lax.dot_general(probs[cc][sub].astype(BF16), doms[cc][sub], TN,
                                          preferred_element_type=F32) for cc, sub in pairs}
        dq_ref[...] = jnp.concatenate([jnp.where(sel0, dqs[cc, 0], dqs[cc, 1]) for cc in range(cps)],
                                      axis=0).astype(BF16)
        for sub in range(2):
            total = dss[0, sub]
            for cc in range(1, cps):
                total = total + dss[cc, sub]
            db_ref[sub] += total
        dk_parts = [dks[cc, 0] + dks[cc, 1] for cc in range(cps)]
        dv_parts = [dvs[cc, 0] + dvs[cc, 1] for cc in range(cps)]

        def window(parts):
            blocks = []
            for rb in range(cps - 1 + BAND // CHUNK):
                acc = None
                for cc in range(cps):
                    b = rb - cc
                    if 0 <= b < BAND // CHUNK:
                        piece = parts[cc][b * CHUNK:(b + 1) * CHUNK, :]
                        acc = piece if acc is None else acc + piece
                blocks.append(acc)
            return jnp.concatenate(blocks, axis=0)

        span = pl.ds(pl.multiple_of(j * cps * CHUNK, CHUNK), (cps - 1) * CHUNK + BAND)
        dk_acc[span, :] += window(dk_parts)
        dv_acc[span, :] += window(dv_parts)

        @pl.when(j == nj - 1)
        def _():
            dk_ref[...] = dk_acc[LEFT_PAD:, :]
            dv_ref[...] = dv_acc[LEFT_PAD:, :]

    q_spec = pl.BlockSpec((tq, LANES), lambda h, j: (j, h))
    kv_spec = pl.BlockSpec((S + LEFT_PAD, LANES), lambda h, j: (0, h))
    col_spec = pl.BlockSpec((S, LANES), lambda h, j: (0, h))
    b_spec = pl.BlockSpec((2, CHUNK, BAND), lambda h, j: (h, 0, 0))
    return pl.pallas_call(
        body, name=name, grid=(HP, nj),
        in_specs=[q_spec, q_spec, kv_spec, kv_spec, b_spec, col_spec, col_spec],
        out_specs=[q_spec, col_spec, col_spec, b_spec],
        out_shape=[jax.ShapeDtypeStruct((S, D), BF16), jax.ShapeDtypeStruct((S, D), F32),
                   jax.ShapeDtypeStruct((S, D), F32), jax.ShapeDtypeStruct((NH, CHUNK, BAND), F32)],
        scratch_shapes=[pltpu.VMEM((S + LEFT_PAD, LANES), F32), pltpu.VMEM((S + LEFT_PAD, LANES), F32)],
        compiler_params=_params(("parallel", "arbitrary")),
    )(q, do, kpad, vpad, bias, dk_in, dv_in)


def _rel_onehot(t):
    r = lax.broadcasted_iota(jnp.int32, (BAND, N_REL), 0)
    i = lax.broadcasted_iota(jnp.int32, (BAND, N_REL), 1)
    idx = jnp.clip(t + LEFT_PAD - r, -(CHUNK - 1), MAX_REL) + (CHUNK - 1)
    return (idx == i).astype(F32)


def bias_expand(rb, name):
    NH = rb.shape[0]

    def body(rb_ref, o_ref):
        def step(t, carry):
            o_ref[t] = lax.dot_general(rb_ref[...], _rel_onehot(t), NT, preferred_element_type=F32,
                                       precision=HIGHEST)
            return carry

        lax.fori_loop(0, CHUNK, step, 0)

    return pl.pallas_call(
        body, name=name, in_specs=[VMEM_SPEC], out_specs=VMEM_SPEC,
        out_shape=jax.ShapeDtypeStruct((CHUNK, NH, BAND), F32), compiler_params=_params(),
    )(rb)


def bias_grad(dsum, name):
    NH = dsum.shape[1]

    def body(d_ref, o_ref):
        def step(t, acc):
            return acc + jnp.dot(d_ref[t], _rel_onehot(t), preferred_element_type=F32, precision=HIGHEST)

        o_ref[...] = lax.fori_loop(0, CHUNK, step, jnp.zeros((NH, N_REL), F32))

    return pl.pallas_call(
        body, name=name, in_specs=[VMEM_SPEC], out_specs=VMEM_SPEC,
        out_shape=jax.ShapeDtypeStruct((NH, N_REL), F32), compiler_params=_params(),
    )(dsum)


def loss_grad(y, tgt, name):
    S, D = y.shape
    tm = min(MATMUL_ROW_TILE, S)

    def body(y_ref, t_ref, d_ref, acc_ref):
        @pl.when(pl.program_id(0) == 0)
        def _():
            acc_ref[...] = jnp.zeros_like(acc_ref)

        err = y_ref[...] - t_ref[...]
        d_ref[...] = err * (1.0 / D)
        acc_ref[0:1, :] += jnp.sum(err * err, axis=0, keepdims=True)

    tile = pl.BlockSpec((tm, D), _row(0))
    return pl.pallas_call(
        body, name=name, grid=(S // tm,), in_specs=[tile, tile],
        out_specs=[tile, pl.BlockSpec((8, D), _fixed2)],
        out_shape=[jax.ShapeDtypeStruct((S, D), F32), jax.ShapeDtypeStruct((8, D), F32)],
        compiler_params=_params(("arbitrary",)),
    )(y, tgt)


def _col_tile(n):
    for t in (768, 512, 256, 128):
        if n % t == 0:
            return t
    return n


def ada_fwd(c_all, w, b, name):
    L, D, n = w.shape
    tn = _col_tile(n)

    def body(c_ref, w_ref, b_ref, o_ref):
        cv = c_ref[...]
        ca = cv * _sigmoid(cv)
        o_ref[...] = jnp.dot(ca, w_ref[...], preferred_element_type=F32, precision=HIGHEST) + b_ref[...]

    return pl.pallas_call(
        body, name=name, grid=(L, n // tn),
        in_specs=[pl.BlockSpec((N_DEV, D), lambda l, j: (0, 0)), pl.BlockSpec((None, D, tn), lambda l, j: (l, 0, j)),
                  pl.BlockSpec((None, 1, tn), lambda l, j: (l, 0, j))],
        out_specs=pl.BlockSpec((None, N_DEV, tn), lambda l, j: (l, 0, j)),
        out_shape=jax.ShapeDtypeStruct((L, N_DEV, n), F32),
        compiler_params=_params(("parallel", "parallel")),
    )(c_all, w, b)


def ada_wgrad(c_all_t, dmod, name):
    L, _, n = dmod.shape
    D = c_all_t.shape[0]
    tn = _col_tile(n)

    def body(c_ref, d_ref, o_ref):
        cv = c_ref[...]
        ca = cv * _sigmoid(cv)
        o_ref[...] = jnp.dot(ca, d_ref[...], preferred_element_type=F32, precision=HIGHEST)

    return pl.pallas_call(
        body, name=name, grid=(L, n // tn),
        in_specs=[pl.BlockSpec((D, N_DEV), lambda l, j: (0, 0)), pl.BlockSpec((None, N_DEV, tn), lambda l, j: (l, 0, j))],
        out_specs=pl.BlockSpec((None, D, tn), lambda l, j: (l, 0, j)),
        out_shape=jax.ShapeDtypeStruct((L, D, n), F32),
        compiler_params=_params(("parallel", "parallel")),
    )(c_all_t, dmod)


ELEMENTWISE_BLOCK_BYTES = 3 * 1024 * 1024


def _elementwise_rows(rows, row_bytes):
    for t in (4096, 2048, 1024, 512, 256, 128, 64, 32, 16):
        if rows % t == 0 and t * row_bytes <= ELEMENTWISE_BLOCK_BYTES:
            return t
    return rows


def sum_leading(a, name):
    n, M, N = a.shape
    tr = _elementwise_rows(M, n * N * 4)

    def body(a_ref, o_ref):
        acc = a_ref[0]
        for i in range(1, n):
            acc = acc + a_ref[i]
        o_ref[...] = acc

    return pl.pallas_call(
        body, name=name, grid=(M // tr,),
        in_specs=[pl.BlockSpec((n, tr, N), lambda i: (0, i, 0))],
        out_specs=pl.BlockSpec((tr, N), _row(0)),
        out_shape=jax.ShapeDtypeStruct((M, N), F32),
        compiler_params=_params(("parallel",)),
    )(a)


def chip_sum(psum, land, chip_idx, name):
    n, M, N = psum.shape
    tr = _elementwise_rows(M, N * 8)

    def body(s_ref, p_ref, a_ref, b_ref, c_ref, o_ref):
        o_ref[...] = ((p_ref[...].astype(F32) + a_ref[...].astype(F32)) + b_ref[...].astype(F32)) + c_ref[...].astype(F32)

    def entry(j):
        return pl.BlockSpec((None, tr, N), lambda i, s: ((s[0] + j) % n, i, 0))

    return pl.pallas_call(
        body, name=name,
        grid_spec=pltpu.PrefetchScalarGridSpec(
            num_scalar_prefetch=1, grid=(M // tr,),
            in_specs=[entry(0), entry(1), entry(2), entry(3)],
            out_specs=pl.BlockSpec((tr, N), lambda i, s: (i, 0))),
        out_shape=jax.ShapeDtypeStruct((M, N), F32),
        compiler_params=_params(("parallel",)),
    )(chip_idx, psum, land, land, land)


def adamw(w, g, m, v, name):
    M, N = w.shape
    tr = _elementwise_rows(M, N * 4)
    c1 = 1.0 - ADAM_B1 ** ADAM_STEP
    c2 = 1.0 - ADAM_B2 ** ADAM_STEP

    def body(w_ref, g_ref, m_ref, v_ref, d_ref, nm_ref, nv_ref):
        g = g_ref[...]
        nm = ADAM_B1 * m_ref[...] + (1.0 - ADAM_B1) * g
        nv = ADAM_B2 * v_ref[...] + (1.0 - ADAM_B2) * (g * g)
        d_ref[...] = -ADAM_LR * ((nm / c1) / (jnp.sqrt(nv / c2) + ADAM_EPS) + ADAM_WD * w_ref[...])
        nm_ref[...] = nm
        nv_ref[...] = nv

    spec = pl.BlockSpec((tr, N), _row(0))
    shp = jax.ShapeDtypeStruct((M, N), F32)
    return pl.pallas_call(
        body, name=name, grid=(M // tr,), in_specs=[spec] * 4, out_specs=[spec] * 3, out_shape=[shp] * 3,
        compiler_params=_params(("parallel",)),
    )(w, g, m, v)


def _coords():
    return lax.axis_index("x"), lax.axis_index("y"), lax.axis_index("c")


def all_gather8(block, name):
    m_per, n = block.shape

    def body(x_ref, out_ref, send_sems, recv_sems, local_sem):
        x, y, c = _coords()
        me, sibling = (x, y, c), (x, y, 1 - c)
        chips = [(1 - x, y), (x, 1 - y), (1 - x, 1 - y)]

        def rows(px, py, pc):
            return out_ref.at[pl.ds((4 * px + 2 * py + pc) * m_per, m_per), :]

        def copy(k, blk, to, src=None):
            return pltpu.make_async_remote_copy(
                src_ref=rows(*blk) if src is None else src, dst_ref=rows(*blk),
                send_sem=send_sems.at[k], recv_sem=recv_sems.at[k], device_id=to, device_id_type=MESH)

        mine = pltpu.make_async_copy(x_ref, rows(*me), local_sem)
        mine.start()
        first = [copy(0, me, sibling, src=x_ref)]
        first += [copy(1 + j, me, (*chip, c), src=x_ref) for j, chip in enumerate(chips)]
        for cp in first:
            cp.start()
        passed = [copy(4 + j, (*chip, c), sibling) for j, chip in enumerate(chips)]
        for j, chip in enumerate(chips):
            copy(1 + j, (*chip, c), me).wait_recv()
            passed[j].start()
        copy(0, sibling, me).wait_recv()
        for j, chip in enumerate(chips):
            copy(4 + j, (*chip, 1 - c), me).wait_recv()
        for cp in first + passed:
            cp.wait_send()
        mine.wait()

    return pl.pallas_call(
        body, name=name, in_specs=[VMEM_SPEC], out_specs=VMEM_SPEC,
        out_shape=jax.ShapeDtypeStruct((N_DEV * m_per, n), block.dtype),
        scratch_shapes=[pltpu.SemaphoreType.DMA((7,)), pltpu.SemaphoreType.DMA((7,)), pltpu.SemaphoreType.DMA],
        compiler_params=_params(),
    )(block)


def _other_chips(x, y):
    return [(1 - x, y), (x, 1 - y), (1 - x, 1 - y)]


HBM_SPEC = pl.BlockSpec(memory_space=pltpu.HBM)
SEM_SPEC = pl.BlockSpec(memory_space=pltpu.SEMAPHORE)
DATAFLOW = pltpu.SideEffectType.DATAFLOW_SIDE_EFFECTING


def _chip_peers(x, y, c):
    return [(px, py, c) for px, py in _other_chips(x, y)]


def _sibling_peer(x, y, c):
    return [(x, y, 1 - c)]


def _weight_desc(stack_ref, k, land_ref, peer, me):
    h = land_ref.shape[1] // 2
    rows = pl.ds(me[2] * h, h)
    return (stack_ref.at[k, rows, :], land_ref.at[2 * me[0] + me[1], rows, :],
            land_ref.at[2 * peer[0] + peer[1], rows, :])


def _grad_desc(psum_ref, k, land_ref, peer, me):
    return psum_ref.at[2 * peer[0] + peer[1]], land_ref.at[2 * me[0] + me[1]], land_ref.at[2 * peer[0] + peer[1]]


def _pair_desc(grad_ref, k, land_ref, peer, me):
    h = land_ref.shape[1]
    return grad_ref.at[:, pl.ds(peer[2] * h, h), :], land_ref, land_ref


def _whole_desc(src_ref, k, land_ref, peer, me):
    return src_ref, land_ref, land_ref


def exchange_start(srcs, lands, units, groups, desc, peers, after, name):
    n_s, n_l, n_g = len(srcs), len(lands), len(groups)
    n_p = len(peers(0, 0, 0))

    def body(*refs):
        s_refs, l_refs = refs[:n_s], refs[n_s:n_s + n_l]
        outs = refs[n_s + n_l + 1:]
        sems, token = outs[:2 * n_g], outs[-1]
        me = _coords()
        for g, ids in enumerate(groups):
            for i, u in enumerate(ids):
                si, k = units[u]
                for j, peer in enumerate(peers(*me)):
                    src, dst, _ = desc(s_refs[si], k, l_refs[u], peer, me)
                    pltpu.make_async_remote_copy(
                        src_ref=src, dst_ref=dst, send_sem=sems[2 * g].at[n_p * i + j],
                        recv_sem=sems[2 * g + 1].at[n_p * i + j], device_id=peer, device_id_type=MESH).start()
        token[...] = jnp.zeros_like(token)

    arrs = list(srcs) + list(lands)
    sem_shapes = [pltpu.SemaphoreType.DMA((n_p * len(ids),)) for ids in groups for _ in range(2)]
    outs = pl.pallas_call(
        body, name=name,
        in_specs=[HBM_SPEC] * len(arrs) + [ANY],
        out_specs=[SEM_SPEC] * (2 * n_g) + [HBM_SPEC] * len(arrs) + [VMEM_SPEC],
        out_shape=sem_shapes + [pltpu.HBM(a.shape, a.dtype) for a in arrs] + [jax.ShapeDtypeStruct((8, LANES), F32)],
        input_output_aliases={i: 2 * n_g + i for i in range(len(arrs))},
        compiler_params=pltpu.CompilerParams(has_side_effects=DATAFLOW),
    )(*[pltpu.with_memory_space_constraint(a, pltpu.HBM) for a in arrs], after)
    sems = outs[:2 * n_g]
    thru = outs[2 * n_g:2 * n_g + len(arrs)]
    return sems, list(thru[:n_s]), list(thru[n_s:]), outs[-1]


def exchange_wait(srcs, lands, units, send_sem, recv_sem, desc, peers, after, name):
    n_s, n_l = len(srcs), len(lands)
    n_p = len(peers(0, 0, 0))

    def body(*refs):
        s_refs, l_refs = refs[:n_s], refs[n_s:n_s + n_l]
        send_sems, recv_sems = refs[n_s + n_l], refs[n_s + n_l + 1]
        me = _coords()
        for i, (si, k) in enumerate(units):
            for j, peer in enumerate(peers(*me)):
                src, _, mine = desc(s_refs[si], k, l_refs[i], peer, me)
                cp = pltpu.make_async_remote_copy(
                    src_ref=src, dst_ref=mine, send_sem=send_sems.at[n_p * i + j], recv_sem=recv_sems.at[n_p * i + j],
                    device_id=peer, device_id_type=MESH)
                cp.wait_send()
                cp.wait_recv()

    arrs = list(srcs) + list(lands)
    outs = pl.pallas_call(
        body, name=name,
        in_specs=[HBM_SPEC] * len(arrs) + [SEM_SPEC, SEM_SPEC, ANY],
        out_specs=[HBM_SPEC] * len(arrs),
        out_shape=[pltpu.HBM(a.shape, a.dtype) for a in arrs],
        input_output_aliases={i: i for i in range(len(arrs))},
        compiler_params=pltpu.CompilerParams(has_side_effects=DATAFLOW),
    )(*arrs, send_sem, recv_sem, after)
    return list(outs[:n_s]), list(outs[n_s:])


def sibling_fill(lands, name):
    n_u = len(lands)

    def body(*refs):
        ins, outs = refs[:n_u], refs[n_u:2 * n_u]
        send_sems, recv_sems = refs[2 * n_u:]
        x, y, c = _coords()
        sends = []
        for u in range(n_u):
            h = ins[u].shape[1] // 2
            for j, (px, py) in enumerate(_other_chips(x, y)):
                part = (2 * px + py, pl.ds(c * h, h), slice(None))
                cp = pltpu.make_async_remote_copy(
                    src_ref=ins[u].at[part], dst_ref=outs[u].at[part], send_sem=send_sems.at[3 * u + j],
                    recv_sem=recv_sems.at[3 * u + j], device_id=(x, y, 1 - c), device_id_type=MESH)
                cp.start()
                sends.append(cp)
        for u in range(n_u):
            h = ins[u].shape[1] // 2
            for j, (px, py) in enumerate(_other_chips(x, y)):
                theirs = (2 * px + py, pl.ds((1 - c) * h, h), slice(None))
                pltpu.make_async_remote_copy(
                    src_ref=ins[u].at[theirs], dst_ref=outs[u].at[theirs], send_sem=send_sems.at[3 * u + j],
                    recv_sem=recv_sems.at[3 * u + j], device_id=(x, y, 1 - c), device_id_type=MESH).wait_recv()
        for cp in sends:
            cp.wait_send()

    return pl.pallas_call(
        body, name=name, in_specs=[ANY] * n_u, out_specs=[ANY] * n_u,
        out_shape=[jax.ShapeDtypeStruct(a.shape, a.dtype) for a in lands],
        input_output_aliases={i: i for i in range(n_u)},
        scratch_shapes=[pltpu.SemaphoreType.DMA((3 * n_u,)), pltpu.SemaphoreType.DMA((3 * n_u,))],
        compiler_params=_params(),
    )(*lands)


def _pack_rows(parts):
    flat = jnp.concatenate([p.reshape(-1).astype(F32) for p in parts])
    n = flat.shape[0]
    padded = -(-n // (8 * LANES)) * (8 * LANES)
    return jnp.pad(flat, (0, padded - n)).reshape(-1, LANES)


def _unpack_rows(packed, shapes):
    flat = packed.reshape(-1)
    out, off = [], 0
    for s in shapes:
        size = 1
        for d in s:
            size *= d
        out.append(flat[off:off + size].reshape(s))
        off += size
    return out


def _shard_last(full, s_me):
    n = full.shape[-1] // N_CHIP
    return lax.dynamic_slice_in_dim(full, s_me * n, n, axis=full.ndim - 1)


def _unshard_last(g):
    moved = jnp.moveaxis(g, 0, -2)
    return moved.reshape(moved.shape[:-2] + (moved.shape[-2] * moved.shape[-1],))


def kernel(x, c, w_ada, b_ada, ln_g, ln_b, ffn_gu, ffn_down, gmlp_w_in, gmlp_b_in, gmlp_ln_g, gmlp_ln_b, gmlp_w_s, gmlp_b_s, gmlp_w_out, w_ada_kv, b_ada_kv, w_kv, attn_w_q, attn_rel_bias, attn_w_o, loss_target, m_w_ada, m_b_ada, m_ln_g, m_ln_b, m_ffn_gu, m_ffn_down, m_gmlp_w_in, m_gmlp_b_in, m_gmlp_ln_g, m_gmlp_ln_b, m_gmlp_w_s, m_gmlp_b_s, m_gmlp_w_out, m_w_ada_kv, m_b_ada_kv, m_w_kv, m_attn_w_q, m_attn_rel_bias, m_attn_w_o, v_w_ada, v_b_ada, v_ln_g, v_ln_b, v_ffn_gu, v_ffn_down, v_gmlp_w_in, v_gmlp_b_in, v_gmlp_ln_g, v_gmlp_ln_b, v_gmlp_w_s, v_gmlp_b_s, v_gmlp_w_out, v_w_ada_kv, v_b_ada_kv, v_w_kv, v_attn_w_q, v_attn_rel_bias, v_attn_w_o):
    xi, yi, ci = _coords()
    s_me = 2 * xi + yi
    dev = 4 * xi + 2 * yi + ci

    x0 = x[0]
    tgt = loss_target[0]
    S, D = x0.shape
    L = w_ada.shape[0]
    NA = gmlp_w_in.shape[0]
    NB = attn_w_q.shape[0]
    NH = D // HEAD_DIM
    alpha = (2.0 * L) ** 0.25
    n_ada = w_ada.shape[2]
    n_kv = w_ada_kv.shape[1]

    stack_names = ["ffn_gu", "ffn_down", "gmlp_w_in", "gmlp_w_out", "w_kv", "attn_w_q", "attn_w_o"]
    stack_src = dict(ffn_gu=ffn_gu, ffn_down=ffn_down, gmlp_w_in=gmlp_w_in, gmlp_w_out=gmlp_w_out, w_kv=w_kv[None],
                     attn_w_q=attn_w_q, attn_w_o=attn_w_o)
    stacks = [stack_src[nm].astype(BF16).reshape((-1,) + stack_src[nm].shape[-2:]) for nm in stack_names]
    units = [(si, k) for si, st in enumerate(stacks) for k in range(st.shape[0])]
    unit_of = {(stack_names[si], k): u for u, (si, k) in enumerate(units)}
    weight_groups = [[("ffn_gu", 0), ("ffn_down", 0)],
                     ([("gmlp_w_in", 0), ("gmlp_w_out", 0)] if NA > 0 else [("attn_w_q", 0), ("attn_w_o", 0)])
                     + [("ffn_gu", 1), ("ffn_down", 1)] + ([("w_kv", 0)] if NA == 0 else [])]
    for l in range(1, L):
        names = [("w_kv", 0)] if l == NA else []
        names += [("ffn_gu", 2 * l), ("ffn_down", 2 * l)]
        names += [("gmlp_w_in", l), ("gmlp_w_out", l)] if l < NA else [("attn_w_q", l - NA), ("attn_w_o", l - NA)]
        names += [("ffn_gu", 2 * l + 1), ("ffn_down", 2 * l + 1)]
        weight_groups.append(names)
    weight_groups = [[unit_of[n] for n in names] for names in weight_groups]

    c_all = all_gather8(jnp.broadcast_to(c, (8, D)), "ag_c").reshape(N_DEV, 8, D)[:, 0]
    b_ada_sh = lax.dynamic_slice_in_dim(b_ada, s_me * n_ada, n_ada, axis=1)
    b_kv_sh = lax.dynamic_slice_in_dim(b_ada_kv, s_me * n_kv, n_kv, axis=0)
    mod_part = ada_fwd(c_all, w_ada, b_ada_sh[:, None, :], "ada_fwd")
    mkv_part = ada_fwd(c_all, w_ada_kv[None], b_kv_sh[None, None, :], "ada_kv_fwd")
    part = jnp.concatenate([jnp.transpose(mod_part, (1, 0, 2)).reshape(N_DEV, L * n_ada), mkv_part[0]], axis=1)
    width = part.shape[1]
    pad_w = -(-width // LANES) * LANES - width
    all_part = all_gather8(jnp.pad(part, ((0, 0), (0, pad_w))), "ag_mod").reshape(N_DEV, N_DEV, width + pad_w)
    mine = lax.dynamic_index_in_dim(all_part[0::2], dev, axis=1, keepdims=False)
    mod = jnp.transpose(mine[:, :L * n_ada].reshape(N_CHIP, L, n_ada), (1, 0, 2)).reshape(L, N_MOD, D)
    mkv = mine[:, L * n_ada:width].reshape(2, D)

    def mrow(l, k):
        return mod[l, k][None, :]

    small_shapes = [ln_g.shape, ln_b.shape, gmlp_b_in.shape, gmlp_ln_g.shape, gmlp_ln_b.shape, attn_rel_bias.shape]
    small_pack = _pack_rows([ln_g, ln_b, gmlp_b_in, gmlp_ln_g, gmlp_ln_b, attn_rel_bias])
    small_all = all_gather8(small_pack, "ag_small_params").reshape((N_DEV,) + small_pack.shape)[0::2]
    sm = [_unpack_rows(small_all[s], small_shapes) for s in range(N_CHIP)]
    ln_g_f, ln_b_f, b_in_f, gln_g_f, gln_b_f, rel_f = [
        _unshard_last(jnp.stack([sm[s][i] for s in range(N_CHIP)])) for i in range(len(small_shapes))]

    lands0 = [lax.dynamic_update_slice(lax.empty((N_CHIP,) + stacks[si].shape[1:], BF16), stacks[si][k][None],
                                       (s_me, 0, 0)) for si, k in units]
    gathers_done = jnp.concatenate([mod.reshape(-1)[:LANES], small_all.reshape(-1)[:LANES]])
    w_sems, stacks_t, lands_t, _ = exchange_start(stacks, lands0, units, weight_groups, _weight_desc, _chip_peers, gathers_done,
                                                  "weight_send_start")
    wg = {}

    def fetch_weights(g, stacks_now, after):
        ids = weight_groups[g]
        stacks_next, got = exchange_wait(stacks_now, [lands_t[u] for u in ids], [units[u] for u in ids],
                                         w_sems[2 * g], w_sems[2 * g + 1], _weight_desc, _chip_peers, after,
                                         "weight_send_wait_%d" % g)
        for u, a in zip(ids, sibling_fill(got, "weight_sibling_fill")):
            wg[u] = a
        return stacks_next

    def W(nm, k):
        return wg[unit_of[(nm, k)]]

    def Wrows(nm, k):
        w4 = W(nm, k)
        return w4.reshape(w4.shape[0] * w4.shape[1], w4.shape[2])

    bst = [jnp.transpose(gmlp_b_s[j]) for j in range(NA)]
    biases = {}

    def make_bias(j, dep):
        rel, _ = lax.optimization_barrier((rel_f[j], dep))
        biases[j] = jnp.transpose(bias_expand(rel, "bias_expand"), (1, 0, 2))
        return biases[j]

    saved = []
    xc = x0
    kpad = vpad = xkv = None
    for l in range(L):
        after = mod if l == 0 else xc
        if l == 1 and NB > 1:
            after = make_bias(1, xc)
        stacks_t = fetch_weights(0 if l == 0 else l + 1, stacks_t, after)
        if l == NA:
            xkv = xc
            kv, hkv = mod_matmul(xc, mkv[1][None], mkv[0][None], W("w_kv", 0), None, BF16, "kv_proj")
            kpad = jnp.pad(kv[:, :D], ((LEFT_PAD, 0), (0, 0)))
            vpad = jnp.pad(kv[:, D:], ((LEFT_PAD, 0), (0, 0)))
        sv = {}
        for i in (0, 2):
            k = 2 * l + i // 2
            gu, hv = mod_matmul(xc, mrow(l, 3 * i + 1), mrow(l, 3 * i), W("ffn_gu", k), None, BF16, "ffn_up")
            gw = 0.5 * (1.0 + mrow(l, 3 * i + 2))
            xn, xh, rs, yv, av = matmul_res_ln(gu, Wrows("ffn_down", k), xc, gw, ln_g_f[l, i][None],
                                               ln_b_f[l, i][None], alpha, True, "ffn_down")
            sv[i] = dict(x=xc, h=hv, gu=gu, a=av, xh=xh, rs=rs, y=yv, gw=gw)
            xc = xn
            if i == 0:
                if l == 0:
                    stacks_t = fetch_weights(1, stacks_t, make_bias(0, xc) if NB > 0 else xc)
                gw = 1.0 + mrow(l, 5)
                if l < NA:
                    pre, hv = mod_matmul(xc, mrow(l, 4), mrow(l, 3), W("gmlp_w_in", l), b_in_f[l][None], F32,
                                         "gmlp_in")
                    qv = sgu_fwd(pre, gln_g_f[l][None], gln_b_f[l][None], gmlp_w_s[l], bst[l], "sgu_fwd")
                    xn, xh, rs, yv = matmul_res_ln(qv, Wrows("gmlp_w_out", l), xc, gw, ln_g_f[l, 1][None],
                                                   ln_b_f[l, 1][None], alpha, False, "gmlp_out")
                    sv[1] = dict(x=xc, h=hv, pre=pre, a=qv, xh=xh, rs=rs, y=yv, gw=gw)
                else:
                    j = l - NA
                    if j not in biases:
                        make_bias(j, xc)
                    qh, hv = mod_matmul(xc, mrow(l, 4), mrow(l, 3), Wrows("attn_w_q", j)[None], None, BF16, "attn_q")
                    ov = attn_fwd(qh, kpad, vpad, biases[j], "attn_fwd")
                    xn, xh, rs, yv = matmul_res_ln(ov, Wrows("attn_w_o", j), xc, gw, ln_g_f[l, 1][None],
                                                   ln_b_f[l, 1][None], alpha, False, "attn_out")
                    sv[1] = dict(x=xc, h=hv, q=qh, a=ov, xh=xh, rs=rs, y=yv, gw=gw)
                xc = xn
        saved.append(sv)

    dx, lacc = loss_grad(xc, tgt, "loss_grad")
    loss = lax.psum((0.5 / D) * jnp.sum(lacc[0]), ("x", "y", "c"))

    gpair = [None] * len(units)
    col_split = {unit_of[("ffn_down", k)] for k in range(2 * L)}
    dmod = [[None] * N_MOD for _ in range(L)]
    d_ln_g = [[None] * 3 for _ in range(L)]
    d_ln_b = [[None] * 3 for _ in range(L)]
    d_b_in, d_gln_g, d_gln_b, d_ws, d_bs, d_rel = ([None] * NA, [None] * NA, [None] * NA, [None] * NA, [None] * NA,
                                                  [None] * NB)
    dk = jnp.zeros((S, D), F32)
    dv = jnp.zeros((S, D), F32)
    dmkv = None

    made = []

    core_idx = ci.astype(jnp.int32).reshape(1)
    chip_idx = s_me.astype(jnp.int32).reshape(1)

    def put(nm, k, a, b, name):
        u = unit_of[(nm, k)]
        rows, cols = stacks[units[u][0]].shape[1:]
        if u in col_split:
            g = wgrad_pair(a, b, N_CHIP // 2, 2 * rows, cols // 2, lambda j, p: j, lambda j, p: p, core_idx, name)
        elif nm in ("ffn_gu", "gmlp_w_in", "w_kv"):
            g = wgrad_pair(a, b, N_CHIP, rows // 2, cols, lambda j, p: p, lambda j, p: j, core_idx, name)
        else:
            g = wgrad_pair(a, b, N_CHIP, rows // 2, cols, lambda j, p: 2 * j + p, lambda j, p: 0, core_idx, name)
        gpair[u] = g.reshape((N_CHIP, -1, g.shape[-1]))
        made.append(u)

    own_half, sib_half = {}, {}
    n_started = [0]

    def start_grad_exchange(ids, after):
        psums = [gpair[u] for u in ids]
        n = len(ids)
        tag = n_started[0]
        n_started[0] += 1
        sems, ps_t, q_t, token = exchange_start(psums, [lax.empty(p.shape, p.dtype) for p in psums],
                                                [(i, 0) for i in range(n)], [list(range(n))], _grad_desc, _chip_peers,
                                                after, "grad_send_start_%d" % tag)
        return dict(ids=ids, tag=tag, sems=sems, ps=ps_t, q=q_t), token

    def finish_grad_exchange(pend, after):
        n = len(pend["ids"])
        ps_t, q = exchange_wait(pend["ps"], pend["q"], [(i, 0) for i in range(n)], pend["sems"][0], pend["sems"][1],
                                _grad_desc, _chip_peers, after, "grad_send_wait_%d" % pend["tag"])
        halves = [chip_sum(ps_t[i], q[i], chip_idx, "grad_chip_sum") for i in range(n)]
        sems, h_t, land_t, _ = exchange_start(halves, [lax.empty(h.shape, h.dtype) for h in halves],
                                              [(i, 0) for i in range(n)], [list(range(n))], _whole_desc,
                                              _sibling_peer, halves[0], "half_send_start_%d" % pend["tag"])
        swaps.append(dict(ids=pend["ids"], tag=pend["tag"], sems=sems, h=h_t, land=land_t))

    def collect_halves(after):
        for sw in swaps:
            n = len(sw["ids"])
            h, land = exchange_wait(sw["h"], sw["land"], [(i, 0) for i in range(n)], sw["sems"][0], sw["sems"][1],
                                    _whole_desc, _sibling_peer, after, "half_send_wait_%d" % sw["tag"])
            for u, mine, theirs in zip(sw["ids"], h, land):
                own_half[u], sib_half[u] = mine, theirs
        swaps.clear()

    swaps = []
    pending = None
    order_token = jnp.zeros((), F32)

    def ln_inputs(l, i):
        t = saved[l][i]
        return (t["xh"], t["rs"], ln_g_f[l, i][None] + order_token, t["y"], t["gw"], 1.0 if i == 1 else 0.5)

    def record_ln(l, i, acc, row0):
        d_ln_g[l][i], d_ln_b[l][i], dmod[l][3 * i + 2] = acc[row0], acc[row0 + 1], acc[row0 + 2]

    ln_done = None
    for l in reversed(range(L)):
        if l == NA - 1:
            dkv = jnp.concatenate([dk, dv], axis=1)
            put("w_kv", 0, hkv, dkv, "kv_wgrad")
            pdxa, pdy, acc = dgrad_mod(dkv, W("w_kv", 0), dx, xkv, mkv[1][None], ln_inputs(l, 2), alpha, "kv_dgrad")
            dmkv = jnp.stack([acc[1], acc[0]])
            record_ln(l, 2, acc, 2)
            ln_done = (pdxa, pdy)
        sv = saved[l]
        for i in (2, 1, 0):
            t = sv[i]
            if ln_done is None:
                dxa, dy, acc1 = ln_res_bwd(dx, *ln_inputs(l, i), alpha, "ln_res_bwd")
                record_ln(l, i, acc1, 0)
            else:
                dxa, dy = ln_done
                ln_done = None
            before = (l, i - 1) if i > 0 else ((l - 1, 2) if l > 0 and l != NA else None)
            prev = ln_inputs(*before) if before is not None else None
            scl = mrow(l, 3 * i + 1)
            if i != 1:
                k = 2 * l + i // 2
                F = t["gu"].shape[1] // 2
                dgu = ffn_act_bwd(dy, Wrows("ffn_down", k), t["gu"], "ffn_act_bwd")
                put("ffn_down", k, t["a"], dy, "ffn_down_wgrad")
                put("ffn_gu", k, t["h"], dgu, "ffn_up_wgrad")
                res = dgrad_mod(dgu, W("ffn_gu", k), dxa, t["x"], scl, prev, alpha, "ffn_up_dgrad")
            elif l < NA:
                dq = matmul_nt(dy, Wrows("gmlp_w_out", l), "gmlp_out_dgrad")
                put("gmlp_w_out", l, t["a"], dy, "gmlp_out_wgrad")
                dpre, dws_l, dss, dgl, dbin = sgu_bwd(dq, t["pre"], gln_g_f[l][None], gln_b_f[l][None], gmlp_w_s[l],
                                                      bst[l], "sgu_bwd")
                d_ws[l] = dws_l
                d_bs[l] = jnp.transpose(group_lane_sum(dss, "sgu_bias_grad")[:, :GMLP_GROUPS])
                d_gln_g[l], d_gln_b[l], d_b_in[l] = dgl[0], dgl[1], dbin[0]
                put("gmlp_w_in", l, t["h"], dpre, "gmlp_in_wgrad")
                res = dgrad_mod(dpre, W("gmlp_w_in", l), dxa, t["x"], scl, prev, alpha, "gmlp_in_dgrad")
            else:
                j = l - NA
                do = matmul_nt(dy, Wrows("attn_w_o", j), "attn_out_dgrad")
                put("attn_w_o", j, t["a"], dy, "attn_out_wgrad")
                dqh, dk, dv, dbias = attn_bwd(t["q"], do, kpad, vpad, biases[j], dk, dv, "attn_bwd")
                d_rel[j] = bias_grad(jnp.transpose(dbias, (1, 0, 2)), "bias_grad")
                put("attn_w_q", j, t["h"], dqh, "attn_q_wgrad")
                res = dgrad_mod(dqh, Wrows("attn_w_q", j)[None], dxa, t["x"], scl, prev, alpha, "attn_q_dgrad")
            acc2 = res[-1]
            dmod[l][3 * i + 1], dmod[l][3 * i] = acc2[0], acc2[1]
            if before is None:
                dx = res[0]
            else:
                record_ln(*before, acc2, 2)
                ln_done = (res[0], res[1])
            if (i == 0 and l > 0) or (i == 1 and l == 0):
                started, token = start_grad_exchange(list(made), acc2)
                made.clear()
                if pending is not None:
                    finish_grad_exchange(pending, acc2)
                pending = started
                order_token = token[0, 0]
    grad_x = dx[None]

    dvec = _pack_rows([jnp.stack([jnp.stack(r) for r in dmod]), dmkv])
    n_dvec = L * N_MOD * D + 2 * D
    dall = all_gather8(dvec, "ag_dmod").reshape(N_DEV, -1, LANES)
    db_all = sum_leading(dall, "ada_bias_grad").reshape(-1)[:n_dvec]
    g_b_ada = db_all[:L * N_MOD * D].reshape(L, N_MOD * D)
    g_b_ada_kv = db_all[L * N_MOD * D:]
    dall2 = dall.reshape(N_DEV, -1)[:, :n_dvec]
    dmod_all = dall2[:, :L * N_MOD * D].reshape(N_DEV, L, N_MOD * D)
    dmod_sh = jnp.transpose(lax.dynamic_slice_in_dim(dmod_all, s_me * n_ada, n_ada, axis=2), (1, 0, 2))
    dmkv_sh = lax.dynamic_slice_in_dim(dall2[:, L * N_MOD * D:], s_me * n_kv, n_kv, axis=1)[None]
    c_all_t = jnp.transpose(c_all)
    g_w_ada = ada_wgrad(c_all_t, dmod_sh, "ada_wgrad")
    g_w_ada_kv = ada_wgrad(c_all_t, dmkv_sh, "ada_kv_wgrad")[0]

    small_g = [jnp.stack([jnp.stack(r) for r in d_ln_g]), jnp.stack([jnp.stack(r) for r in d_ln_b]),
               jnp.stack(d_b_in), jnp.stack(d_gln_g), jnp.stack(d_gln_b), jnp.stack(d_rel), jnp.stack(d_ws),
               jnp.stack(d_bs)]
    sg_shapes = [a.shape for a in small_g]
    sg_pack = _pack_rows(small_g)
    sg_all = all_gather8(sg_pack, "ag_small_grads").reshape(N_DEV, -1, LANES)
    sg_sum = _unpack_rows(sum_leading(sg_all, "small_grad_sum"), sg_shapes)
    g_ln_g, g_ln_b, g_b_in, g_gln_g, g_gln_b, g_rel = [_shard_last(a, s_me) for a in sg_sum[:6]]
    g_ws, g_bs = sg_sum[6], sg_sum[7]

    last, _ = start_grad_exchange(list(made), sg_all)
    finish_grad_exchange(pending, sg_all)

    grads = dict(w_ada=g_w_ada, b_ada=g_b_ada, ln_g=g_ln_g, ln_b=g_ln_b, gmlp_b_in=g_b_in, gmlp_ln_g=g_gln_g,
                 gmlp_ln_b=g_gln_b, gmlp_w_s=g_ws, gmlp_b_s=g_bs, w_ada_kv=g_w_ada_kv, b_ada_kv=g_b_ada_kv,
                 attn_rel_bias=g_rel)
    weights = dict(w_ada=w_ada, b_ada=b_ada, ln_g=ln_g, ln_b=ln_b, ffn_gu=ffn_gu, ffn_down=ffn_down,
                   gmlp_w_in=gmlp_w_in, gmlp_b_in=gmlp_b_in, gmlp_ln_g=gmlp_ln_g, gmlp_ln_b=gmlp_ln_b,
                   gmlp_w_s=gmlp_w_s, gmlp_b_s=gmlp_b_s, gmlp_w_out=gmlp_w_out, w_ada_kv=w_ada_kv,
                   b_ada_kv=b_ada_kv, w_kv=w_kv, attn_w_q=attn_w_q, attn_rel_bias=attn_rel_bias, attn_w_o=attn_w_o)
    ms = dict(w_ada=m_w_ada, b_ada=m_b_ada, ln_g=m_ln_g, ln_b=m_ln_b, ffn_gu=m_ffn_gu, ffn_down=m_ffn_down,
              gmlp_w_in=m_gmlp_w_in, gmlp_b_in=m_gmlp_b_in, gmlp_ln_g=m_gmlp_ln_g, gmlp_ln_b=m_gmlp_ln_b,
              gmlp_w_s=m_gmlp_w_s, gmlp_b_s=m_gmlp_b_s, gmlp_w_out=m_gmlp_w_out, w_ada_kv=m_w_ada_kv,
              b_ada_kv=m_b_ada_kv, w_kv=m_w_kv, attn_w_q=m_attn_w_q, attn_rel_bias=m_attn_rel_bias,
              attn_w_o=m_attn_w_o)
    vs = dict(w_ada=v_w_ada, b_ada=v_b_ada, ln_g=v_ln_g, ln_b=v_ln_b, ffn_gu=v_ffn_gu, ffn_down=v_ffn_down,
              gmlp_w_in=v_gmlp_w_in, gmlp_b_in=v_gmlp_b_in, gmlp_ln_g=v_gmlp_ln_g, gmlp_ln_b=v_gmlp_ln_b,
              gmlp_w_s=v_gmlp_w_s, gmlp_b_s=v_gmlp_b_s, gmlp_w_out=v_gmlp_w_out, w_ada_kv=v_w_ada_kv,
              b_ada_kv=v_b_ada_kv, w_kv=v_w_kv, attn_w_q=v_attn_w_q, attn_rel_bias=v_attn_rel_bias,
              attn_w_o=v_attn_w_o)
    order = ["w_ada", "b_ada", "ln_g", "ln_b", "ffn_gu", "ffn_down", "gmlp_w_in", "gmlp_b_in", "gmlp_ln_g",
             "gmlp_ln_b", "gmlp_w_s", "gmlp_b_s", "gmlp_w_out", "w_ada_kv", "b_ada_kv", "w_kv", "attn_w_q",
             "attn_rel_bias", "attn_w_o"]
    big_names = ["w_ada", "w_ada_kv"] + stack_names
    small_names = [nm for nm in order if nm not in big_names]
    delta, new_m, new_v = {}, {}, {}

    def adamw_big(nm):
        shp = weights[nm].shape
        two_d = (-1, shp[-1])
        d, a, b = adamw(weights[nm].reshape(two_d), grads[nm].reshape(two_d), ms[nm].reshape(two_d),
                        vs[nm].reshape(two_d), "adamw")
        delta[nm], new_m[nm], new_v[nm] = d.reshape(shp), a.reshape(shp), b.reshape(shp)

    adamw_big("w_ada")
    adamw_big("w_ada_kv")
    shapes = [weights[nm].shape for nm in small_names]
    d, a, b = adamw(_pack_rows([weights[nm] for nm in small_names]), _pack_rows([grads[nm] for nm in small_names]),
                    _pack_rows([ms[nm] for nm in small_names]), _pack_rows([vs[nm] for nm in small_names]),
                    "adamw_small")
    for nm, dd, aa, bb in zip(small_names, _unpack_rows(d, shapes), _unpack_rows(a, shapes), _unpack_rows(b, shapes)):
        delta[nm], new_m[nm], new_v[nm] = dd, aa, bb

    def full_grad(u):
        lo = jnp.where(ci == 0, own_half[u], sib_half[u])
        hi = jnp.where(ci == 0, sib_half[u], own_half[u])
        return jnp.concatenate([lo, hi], axis=1 if u in col_split else 0)

    def adamw_stack(nm):
        si = stack_names.index(nm)
        g = jnp.stack([full_grad(unit_of[(nm, k)]) for k in range(stacks[si].shape[0])])
        grads[nm] = g.reshape(weights[nm].shape)
        adamw_big(nm)

    late = [stack_names[units[u][0]] for u in last["ids"]]
    early = [nm for nm in stack_names if nm not in late]
    collect_halves(delta["w_ada"])
    for nm in early:
        adamw_stack(nm)
    finish_grad_exchange(last, delta[early[-1]])
    collect_halves(delta[early[-1]])
    for nm in stack_names:
        if nm in late:
            adamw_stack(nm)

    return (loss, grad_x, *[grads[nm] for nm in order], *[delta[nm] for nm in order],
            *[new_m[nm] for nm in order], *[new_v[nm] for nm in order])
```

```python
import functools

import jax
import jax.numpy as jnp
from jax import lax
from jax.experimental import pallas as pl
from jax.experimental.pallas import tpu as pltpu

F32 = jnp.float32
BF16 = jnp.bfloat16
MESH = pl.DeviceIdType.MESH
HIGHEST = lax.Precision.HIGHEST

CHUNK = 64
GMLP_WINDOW = 128
GMLP_GROUPS = 8
HEAD_DIM = 64
LEFT_CHUNKS = 8
BAND = (LEFT_CHUNKS + 1) * CHUNK
LEFT_PAD = LEFT_CHUNKS * CHUNK
MAX_REL = 4 * CHUNK
N_REL = (CHUNK - 1) + MAX_REL + 1
LN_EPS = 1e-5
N_MOD = 9
N_DEV = 8
N_CHIP = 4

ADAM_LR = 0.001
ADAM_B1 = 0.9
ADAM_B2 = 0.999
ADAM_EPS = 1e-08
ADAM_WD = 0.01
ADAM_STEP = 10

LANES = 128
ROW_TILE = 256
MATMUL_ROW_TILE = 512
WGRAD_ROWS = 1024
ATTN_CHUNKS_PER_STEP = 4
VMEM_LIMIT_MB = 56

NT = (((1,), (1,)), ((), ()))
TN = (((0,), (0,)), ((), ()))

ANY = pl.BlockSpec(memory_space=pl.ANY)
VMEM_SPEC = pl.BlockSpec(memory_space=pltpu.VMEM)


def _params(semantics=None):
    kw = dict(vmem_limit_bytes=VMEM_LIMIT_MB * 1024 * 1024)
    if semantics is not None:
        kw["dimension_semantics"] = semantics
    return pltpu.CompilerParams(**kw)


def _sigmoid(v):
    return 0.5 * (1.0 + jnp.tanh(0.5 * v))


def _gelu(v):
    return 0.5 * v * (1.0 + lax.erf(v * (2.0 ** -0.5)))


def _gelu_grad(v):
    return 0.5 * (1.0 + lax.erf(v * (2.0 ** -0.5))) + v * jnp.exp(-0.5 * v * v) * ((2.0 * jnp.pi) ** -0.5)


def _row(m):
    return lambda i: (i, 0)


def _fixed2(i):
    return (0, 0)


def _fixed3(i):
    return (0, 0, 0)


def _resident(shape):
    return pl.BlockSpec(shape, _fixed2 if len(shape) == 2 else _fixed3, pipeline_mode=pl.Buffered(1))


def mod_matmul(x, scl, shift, w, bias, out_dtype, name):
    S, D = x.shape
    NS, _, n = w.shape
    tm = min(MATMUL_ROW_TILE, S)
    has_bias = bias is not None

    def body(*refs):
        if has_bias:
            x_ref, scl_ref, sh_ref, w_ref, b_ref, o_ref, h_ref = refs
        else:
            x_ref, scl_ref, sh_ref, w_ref, o_ref, h_ref = refs
        h = (x_ref[...] * (1.0 + scl_ref[...]) + sh_ref[...]).astype(BF16)
        h_ref[...] = h
        for s in range(NS):
            acc = jnp.dot(h, w_ref[s], preferred_element_type=F32)
            if has_bias:
                acc = acc + b_ref[:, s * n:(s + 1) * n]
            o_ref[:, s * n:(s + 1) * n] = acc.astype(out_dtype)

    in_specs = [pl.BlockSpec((tm, D), _row(0)), pl.BlockSpec((1, D), _fixed2), pl.BlockSpec((1, D), _fixed2),
                _resident((NS, D, n))]
    args = [x, scl, shift, w]
    if has_bias:
        in_specs.append(pl.BlockSpec((1, NS * n), _fixed2))
        args.append(bias)
    return pl.pallas_call(
        body, name=name, grid=(S // tm,), in_specs=in_specs,
        out_specs=[pl.BlockSpec((tm, NS * n), _row(0)), pl.BlockSpec((tm, D), _row(0))],
        out_shape=[jax.ShapeDtypeStruct((S, NS * n), out_dtype), jax.ShapeDtypeStruct((S, D), BF16)],
        compiler_params=_params(("parallel",)),
    )(*args)


def matmul_res_ln(a, w, x, gw, lg, lb, alpha, swiglu, name):
    S, D = x.shape
    K = w.shape[0]
    tm = min(ROW_TILE, S)
    ka = a.shape[1]

    def body(a_ref, w_ref, x_ref, gw_ref, lg_ref, lb_ref, xn_ref, xh_ref, rs_ref, y_ref, *act_ref):
        if swiglu:
            g = a_ref[:, :K].astype(F32)
            u = a_ref[:, K:].astype(F32)
            act = (g * _sigmoid(g) * u).astype(BF16)
            act_ref[0][...] = act
        else:
            act = a_ref[...].astype(BF16)
        y = jnp.dot(act, w_ref[...], preferred_element_type=F32)
        z = alpha * x_ref[...] + gw_ref[...] * y
        mu = jnp.mean(z, axis=-1, keepdims=True)
        zc = z - mu
        var = jnp.mean(zc * zc, axis=-1, keepdims=True)
        rstd = lax.rsqrt(var + LN_EPS)
        xhat = zc * rstd
        xn_ref[...] = xhat * lg_ref[...] + lb_ref[...]
        xh_ref[...] = xhat
        rs_ref[...] = rstd
        y_ref[...] = y.astype(BF16)

    vec = pl.BlockSpec((1, D), _fixed2)
    out_specs = [pl.BlockSpec((tm, D), _row(0)), pl.BlockSpec((tm, D), _row(0)), pl.BlockSpec((tm, 1), _row(0)),
                 pl.BlockSpec((tm, D), _row(0))]
    out_shape = [jax.ShapeDtypeStruct((S, D), F32), jax.ShapeDtypeStruct((S, D), F32),
                 jax.ShapeDtypeStruct((S, 1), F32), jax.ShapeDtypeStruct((S, D), BF16)]
    if swiglu:
        out_specs.append(pl.BlockSpec((tm, K), _row(0)))
        out_shape.append(jax.ShapeDtypeStruct((S, K), BF16))
    return pl.pallas_call(
        body, name=name, grid=(S // tm,),
        in_specs=[pl.BlockSpec((tm, ka), _row(0)), _resident((K, D)), pl.BlockSpec((tm, D), _row(0)),
                  vec, vec, vec],
        out_specs=out_specs, out_shape=out_shape,
        compiler_params=_params(("parallel",)),
    )(a, w, x, gw, lg, lb)


def _ln_res_bwd_tile(d, xh_ref, rs_ref, lg_ref, y_ref, gw_ref, wres, alpha, dxa_ref, dy_ref, acc_ref, row0):
    xh = xh_ref[...]
    dxh = d * lg_ref[...]
    m1 = jnp.mean(dxh, axis=-1, keepdims=True)
    m2 = jnp.mean(dxh * xh, axis=-1, keepdims=True)
    dz = rs_ref[...] * (dxh - m1 - xh * m2)
    dxa_ref[...] = alpha * dz
    dy_ref[...] = (gw_ref[...] * dz).astype(BF16)
    acc_ref[row0:row0 + 1, :] += jnp.sum(d * xh, axis=0, keepdims=True)
    acc_ref[row0 + 1:row0 + 2, :] += jnp.sum(d, axis=0, keepdims=True)
    acc_ref[row0 + 2:row0 + 3, :] += jnp.sum((wres * dz) * y_ref[...].astype(F32), axis=0, keepdims=True)


def ln_res_bwd(dxn, xhat, rstd, lg, y, gw, wres, alpha, name):
    S, D = dxn.shape
    tm = min(MATMUL_ROW_TILE, S)

    def body(dxn_ref, xh_ref, rs_ref, lg_ref, y_ref, gw_ref, dxa_ref, dy_ref, acc_ref):
        @pl.when(pl.program_id(0) == 0)
        def _():
            acc_ref[...] = jnp.zeros_like(acc_ref)

        _ln_res_bwd_tile(dxn_ref[...], xh_ref, rs_ref, lg_ref, y_ref, gw_ref, wres, alpha, dxa_ref, dy_ref, acc_ref, 0)

    vec = pl.BlockSpec((1, D), _fixed2)
    tile = pl.BlockSpec((tm, D), _row(0))
    return pl.pallas_call(
        body, name=name, grid=(S // tm,),
        in_specs=[tile, tile, pl.BlockSpec((tm, 1), _row(0)), vec, tile, vec],
        out_specs=[tile, tile, pl.BlockSpec((8, D), _fixed2)],
        out_shape=[jax.ShapeDtypeStruct((S, D), F32), jax.ShapeDtypeStruct((S, D), BF16),
                   jax.ShapeDtypeStruct((8, D), F32)],
        compiler_params=_params(("arbitrary",)),
    )(dxn, xhat, rstd, lg, y, gw)


def ffn_act_bwd(dy, wd, gu, name):
    S, D = dy.shape
    K = wd.shape[0]
    tm = min(ROW_TILE, S)

    def body(dy_ref, wd_ref, gu_ref, o_ref):
        da = lax.dot_general(dy_ref[...], wd_ref[...], NT, preferred_element_type=F32)
        g = gu_ref[:, :K].astype(F32)
        u = gu_ref[:, K:].astype(F32)
        sg = _sigmoid(g)
        o_ref[:, :K] = (da * u * (sg * (1.0 + g * (1.0 - sg)))).astype(BF16)
        o_ref[:, K:] = (da * (g * sg)).astype(BF16)

    return pl.pallas_call(
        body, name=name, grid=(S // tm,),
        in_specs=[pl.BlockSpec((tm, D), _row(0)), _resident((K, D)), pl.BlockSpec((tm, 2 * K), _row(0))],
        out_specs=pl.BlockSpec((tm, 2 * K), _row(0)),
        out_shape=jax.ShapeDtypeStruct((S, 2 * K), BF16),
        compiler_params=_params(("parallel",)),
    )(dy, wd, gu)


def matmul_nt(a, w, name):
    S, D = a.shape
    K = w.shape[0]
    tm = min(MATMUL_ROW_TILE, S)

    def body(a_ref, w_ref, o_ref):
        o_ref[...] = lax.dot_general(a_ref[...], w_ref[...], NT, preferred_element_type=F32).astype(BF16)

    return pl.pallas_call(
        body, name=name, grid=(S // tm,),
        in_specs=[pl.BlockSpec((tm, D), _row(0)), _resident((K, D))],
        out_specs=pl.BlockSpec((tm, K), _row(0)),
        out_shape=jax.ShapeDtypeStruct((S, K), BF16),
        compiler_params=_params(("parallel",)),
    )(a, w)


def dgrad_mod(dpre, w, dxa, xin, scl, prev, alpha, name):
    S, D = xin.shape
    NS, _, n = w.shape
    tm = min(ROW_TILE, S)
    wres = prev[5] if prev is not None else None

    def body(*refs):
        dp_ref, w_ref, dxa_ref, xin_ref, scl_ref = refs[:5]
        acc_ref = refs[-1]

        @pl.when(pl.program_id(0) == 0)
        def _():
            acc_ref[...] = jnp.zeros_like(acc_ref)

        dh = jnp.zeros((tm, D), F32)
        for s in range(NS):
            dh = dh + lax.dot_general(dp_ref[:, s * n:(s + 1) * n].astype(BF16), w_ref[s], NT,
                                      preferred_element_type=F32)
        dx = dxa_ref[...] + dh * (1.0 + scl_ref[...])
        acc_ref[0:1, :] += jnp.sum(dh * xin_ref[...], axis=0, keepdims=True)
        acc_ref[1:2, :] += jnp.sum(dh, axis=0, keepdims=True)
        if prev is None:
            refs[5][...] = dx
        else:
            xh_ref, rs_ref, lg_ref, y_ref, gw_ref, pdxa_ref, pdy_ref = refs[5:12]
            _ln_res_bwd_tile(dx, xh_ref, rs_ref, lg_ref, y_ref, gw_ref, wres, alpha, pdxa_ref, pdy_ref, acc_ref, 2)

    tile = pl.BlockSpec((tm, D), _row(0))
    vec = pl.BlockSpec((1, D), _fixed2)
    in_specs = [pl.BlockSpec((tm, NS * n), _row(0)), _resident((NS, D, n)), tile, tile, vec]
    args = [dpre, w, dxa, xin, scl]
    if prev is None:
        out_specs = [tile]
        out_shape = [jax.ShapeDtypeStruct((S, D), F32)]
    else:
        in_specs += [tile, pl.BlockSpec((tm, 1), _row(0)), vec, tile, vec]
        args += list(prev[:5])
        out_specs = [tile, tile]
        out_shape = [jax.ShapeDtypeStruct((S, D), F32), jax.ShapeDtypeStruct((S, D), BF16)]
    return pl.pallas_call(
        body, name=name, grid=(S // tm,), in_specs=in_specs,
        out_specs=out_specs + [pl.BlockSpec((8, D), _fixed2)],
        out_shape=out_shape + [jax.ShapeDtypeStruct((8, D), F32)],
        compiler_params=_params(("arbitrary",)),
    )(*args)


def wgrad_pair(a, b, J, kb, nb, a_block, b_block, half_idx, name):
    S = a.shape[0]
    ts = min(WGRAD_ROWS, S)
    nsteps = S // ts

    def body(h_ref, a_ref, b_ref, o_ref, acc_ref, send_buf, recv_buf, send_sems, recv_sems):
        jj, si = pl.program_id(0), pl.program_id(1)
        x, y, c = _coords()
        j = lax.rem(jj, J)
        last = si == nsteps - 1

        def copy(blk):
            return pltpu.make_async_remote_copy(
                src_ref=send_buf.at[blk], dst_ref=recv_buf.at[blk], send_sem=send_sems.at[blk],
                recv_sem=recv_sems.at[blk], device_id=(x, y, 1 - c), device_id_type=MESH)

        @pl.when(si == 0)
        def _():
            acc_ref[...] = jnp.zeros_like(acc_ref)

        acc_ref[...] += lax.dot_general(a_ref[...], b_ref[...].astype(BF16), TN, preferred_element_type=F32)

        @pl.when(jnp.logical_and(last, jj < J))
        def _():
            send_buf[j] = acc_ref[...].astype(BF16)
            copy(j).start()

        @pl.when(jnp.logical_and(last, jj >= J))
        def _():
            copy(j).wait_recv()
            o_ref[...] = (acc_ref[...] + recv_buf[j].astype(F32)).astype(BF16)

        @pl.when(jnp.logical_and(last, jj == 2 * J - 1))
        def _():
            for blk in range(J):
                copy(blk).wait_send()

    def half(jj, h):
        return jnp.where(jj < J, 1 - h[0], h[0])

    return pl.pallas_call(
        body, name=name,
        grid_spec=pltpu.PrefetchScalarGridSpec(
            num_scalar_prefetch=1, grid=(2 * J, nsteps),
            in_specs=[pl.BlockSpec((ts, kb), lambda jj, s, h: (s, a_block(lax.rem(jj, J), half(jj, h)))),
                      pl.BlockSpec((ts, nb), lambda jj, s, h: (s, b_block(lax.rem(jj, J), half(jj, h))))],
            out_specs=pl.BlockSpec((None, kb, nb), lambda jj, s, h: (jnp.maximum(jj - J, 0), 0, 0)),
            scratch_shapes=[pltpu.VMEM((kb, nb), F32), pltpu.VMEM((J, kb, nb), BF16), pltpu.VMEM((J, kb, nb), BF16),
                            pltpu.SemaphoreType.DMA((J,)), pltpu.SemaphoreType.DMA((J,))]),
        out_shape=jax.ShapeDtypeStruct((J, kb, nb), BF16),
        compiler_params=_params(("arbitrary", "arbitrary")),
    )(half_idx, a, b)


def _window_mask():
    t = lax.broadcasted_iota(jnp.int32, (GMLP_WINDOW, GMLP_WINDOW), 0)
    s = lax.broadcasted_iota(jnp.int32, (GMLP_WINDOW, GMLP_WINDOW), 1)
    return ((s // CHUNK) <= (t // CHUNK)).astype(F32)


def sgu_fwd(pre, glg, glb, ws, bst, name):
    S, H2 = pre.shape
    H = H2 // 2
    W, G = GMLP_WINDOW, GMLP_GROUPS
    gd = H // G
    tm = min(ROW_TILE, S)

    def body(pre_ref, glg_ref, glb_ref, ws_ref, bst_ref, q_ref):
        u = _gelu(pre_ref[:, :H])
        v = _gelu(pre_ref[:, H:])
        mu = jnp.mean(v, axis=-1, keepdims=True)
        vc = v - mu
        var = jnp.mean(vc * vc, axis=-1, keepdims=True)
        vn = ((vc * lax.rsqrt(var + LN_EPS)) * glg_ref[...] + glb_ref[...]).astype(BF16)
        mask = _window_mask()
        for g in range(G):
            wsg = (ws_ref[g] * mask).astype(BF16)
            bcol = bst_ref[:, g:g + 1]
            for wi in range(tm // W):
                rows = slice(wi * W, (wi + 1) * W)
                cols = slice(g * gd, (g + 1) * gd)
                s = jnp.dot(wsg, vn[rows, cols], preferred_element_type=F32) + bcol
                q_ref[rows, cols] = (u[rows, cols] * s).astype(BF16)

    return pl.pallas_call(
        body, name=name, grid=(S // tm,),
        in_specs=[pl.BlockSpec((tm, H2), _row(0)), pl.BlockSpec((1, H), _fixed2), pl.BlockSpec((1, H), _fixed2),
                  pl.BlockSpec((G, W, W), _fixed3), pl.BlockSpec((W, G), _fixed2)],
        out_specs=pl.BlockSpec((tm, H), _row(0)),
        out_shape=jax.ShapeDtypeStruct((S, H), BF16),
        compiler_params=_params(("parallel",)),
    )(pre, glg, glb, ws, bst)


def sgu_bwd(dq, pre, glg, glb, ws, bst, name):
    S, H2 = pre.shape
    H = H2 // 2
    W, G = GMLP_WINDOW, GMLP_GROUPS
    gd = H // G
    tm = min(ROW_TILE, S)

    def body(dq_ref, pre_ref, glg_ref, glb_ref, ws_ref, bst_ref,
             dpre_ref, dws_ref, dss_ref, dgl_ref, dbin_ref, du_s, dvn_s):
        @pl.when(pl.program_id(0) == 0)
        def _():
            dws_ref[...] = jnp.zeros_like(dws_ref)
            dss_ref[...] = jnp.zeros_like(dss_ref)
            dgl_ref[...] = jnp.zeros_like(dgl_ref)
            dbin_ref[...] = jnp.zeros_like(dbin_ref)

        pu = pre_ref[:, :H]
        pv = pre_ref[:, H:]
        u = _gelu(pu)
        v = _gelu(pv)
        mu = jnp.mean(v, axis=-1, keepdims=True)
        vc = v - mu
        var = jnp.mean(vc * vc, axis=-1, keepdims=True)
        rstd = lax.rsqrt(var + LN_EPS)
        vhat = vc * rstd
        vn = (vhat * glg_ref[...] + glb_ref[...]).astype(BF16)
        mask = _window_mask()
        for g in range(G):
            wsg = (ws_ref[g] * mask).astype(BF16)
            bcol = bst_ref[:, g:g + 1]
            cols = slice(g * gd, (g + 1) * gd)
            for wi in range(tm // W):
                rows = slice(wi * W, (wi + 1) * W)
                vblk = vn[rows, cols]
                s = jnp.dot(wsg, vblk, preferred_element_type=F32) + bcol
                dqb = dq_ref[rows, cols].astype(F32)
                du_s[rows, cols] = dqb * s
                ds = dqb * u[rows, cols]
                dss_ref[:, cols] += ds
                dsb = ds.astype(BF16)
                dvn_s[rows, cols] = lax.dot_general(wsg, dsb, TN, preferred_element_type=F32)
                dws_ref[g] += lax.dot_general(dsb, vblk, NT, preferred_element_type=F32) * mask
        dvn = dvn_s[...]
        dgl_ref[0:1, :] += jnp.sum(dvn * vhat, axis=0, keepdims=True)
        dgl_ref[1:2, :] += jnp.sum(dvn, axis=0, keepdims=True)
        dvh = dvn * glg_ref[...]
        m1 = jnp.mean(dvh, axis=-1, keepdims=True)
        m2 = jnp.mean(dvh * vhat, axis=-1, keepdims=True)
        dv = rstd * (dvh - m1 - vhat * m2)
        dpu = du_s[...] * _gelu_grad(pu)
        dpv = dv * _gelu_grad(pv)
        dbin_ref[0:1, :H] += jnp.sum(dpu, axis=0, keepdims=True)
        dbin_ref[0:1, H:] += jnp.sum(dpv, axis=0, keepdims=True)
        dpre_ref[:, :H] = dpu.astype(BF16)
        dpre_ref[:, H:] = dpv.astype(BF16)

    return pl.pallas_call(
        body, name=name, grid=(S // tm,),
        in_specs=[pl.BlockSpec((tm, H), _row(0)), pl.BlockSpec((tm, H2), _row(0)), pl.BlockSpec((1, H), _fixed2),
                  pl.BlockSpec((1, H), _fixed2), pl.BlockSpec((G, W, W), _fixed3), pl.BlockSpec((W, G), _fixed2)],
        out_specs=[pl.BlockSpec((tm, H2), _row(0)), pl.BlockSpec((G, W, W), _fixed3), pl.BlockSpec((W, H), _fixed2),
                   pl.BlockSpec((8, H), _fixed2), pl.BlockSpec((8, H2), _fixed2)],
        out_shape=[jax.ShapeDtypeStruct((S, H2), BF16), jax.ShapeDtypeStruct((G, W, W), F32),
                   jax.ShapeDtypeStruct((W, H), F32), jax.ShapeDtypeStruct((8, H), F32),
                   jax.ShapeDtypeStruct((8, H2), F32)],
        scratch_shapes=[pltpu.VMEM((tm, H), F32), pltpu.VMEM((tm, H), F32)],
        compiler_params=_params(("arbitrary",)),
    )(dq, pre, glg, glb, ws, bst)


def group_lane_sum(dss, name):
    W, H = dss.shape
    gd = H // GMLP_GROUPS

    def body(d_ref, o_ref):
        j = lax.broadcasted_iota(jnp.int32, (H, LANES), 0)
        g = lax.broadcasted_iota(jnp.int32, (H, LANES), 1)
        ind = ((j // gd) == g).astype(F32)
        o_ref[...] = jnp.dot(d_ref[...], ind, preferred_element_type=F32, precision=HIGHEST)

    return pl.pallas_call(
        body, name=name, in_specs=[VMEM_SPEC], out_specs=VMEM_SPEC,
        out_shape=jax.ShapeDtypeStruct((W, LANES), F32), compiler_params=_params(),
    )(dss)


def _attn_load(j, cps, q_ref, k_ref, v_ref):
    r = lax.broadcasted_iota(jnp.int32, (CHUNK, BAND), 1)
    chunks = []
    for cc in range(cps):
        start = pl.multiple_of((j * cps + cc) * CHUNK, CHUNK)
        chunks.append((q_ref[cc * CHUNK:(cc + 1) * CHUNK, :], k_ref[pl.ds(start, BAND), :],
                       v_ref[pl.ds(start, BAND), :], (r + start) >= LEFT_PAD))
    return chunks


def _attn_probs(chunks, b_ref, sels, scale):
    qms = [[jnp.where(sel, q2, jnp.zeros_like(q2)) for sel in sels] for q2, _, _, _ in chunks]
    raw = [[lax.dot_general(qm, k2, NT, preferred_element_type=F32) for qm in qms[cc]]
           for cc, (_, k2, _, _) in enumerate(chunks)]
    probs = []
    for cc, (_, _, _, valid) in enumerate(chunks):
        row = []
        for sub in range(2):
            s = jnp.where(valid, raw[cc][sub] * scale + b_ref[sub], -jnp.inf)
            e = jnp.exp(s - jnp.max(s, axis=-1, keepdims=True))
            row.append(e / jnp.sum(e, axis=-1, keepdims=True))
        probs.append(row)
    return qms, probs


def attn_fwd(q, kpad, vpad, bias, name):
    S, D = q.shape
    HP = D // LANES
    cps = min(ATTN_CHUNKS_PER_STEP, S // CHUNK)
    tq = cps * CHUNK
    scale = HEAD_DIM ** -0.5

    def body(q_ref, k_ref, v_ref, b_ref, o_ref):
        sel0 = lax.broadcasted_iota(jnp.int32, (CHUNK, LANES), 1) < HEAD_DIM
        chunks = _attn_load(pl.program_id(1), cps, q_ref, k_ref, v_ref)
        _, probs = _attn_probs(chunks, b_ref, (sel0, jnp.logical_not(sel0)), scale)
        outs = [[jnp.dot(probs[cc][sub].astype(BF16), v2, preferred_element_type=F32) for sub in range(2)]
                for cc, (_, _, v2, _) in enumerate(chunks)]
        o_ref[...] = jnp.concatenate([jnp.where(sel0, o[0], o[1]) for o in outs], axis=0).astype(BF16)

    kv_spec = pl.BlockSpec((S + LEFT_PAD, LANES), lambda h, j: (0, h))
    return pl.pallas_call(
        body, name=name, grid=(HP, S // tq),
        in_specs=[pl.BlockSpec((tq, LANES), lambda h, j: (j, h)), kv_spec, kv_spec,
                  pl.BlockSpec((2, CHUNK, BAND), lambda h, j: (h, 0, 0))],
        out_specs=pl.BlockSpec((tq, LANES), lambda h, j: (j, h)),
        out_shape=jax.ShapeDtypeStruct((S, D), BF16),
        compiler_params=_params(("parallel", "parallel")),
    )(q, kpad, vpad, bias)


def attn_bwd(q, do, kpad, vpad, bias, dk_in, dv_in, name):
    S, D = q.shape
    HP = D // LANES
    NH = 2 * HP
    cps = min(ATTN_CHUNKS_PER_STEP, S // CHUNK)
    tq = cps * CHUNK
    nj = S // tq
    scale = HEAD_DIM ** -0.5

    def body(q_ref, do_ref, k_ref, v_ref, b_ref, dki_ref, dvi_ref, dq_ref, dk_ref, dv_ref, db_ref, dk_acc, dv_acc):
        j = pl.program_id(1)

        @pl.when(j == 0)
        def _():
            dk_acc[:LEFT_PAD, :] = jnp.zeros((LEFT_PAD, LANES), F32)
            dv_acc[:LEFT_PAD, :] = jnp.zeros((LEFT_PAD, LANES), F32)
            dk_acc[LEFT_PAD:, :] = dki_ref[...]
            dv_acc[LEFT_PAD:, :] = dvi_ref[...]
            db_ref[...] = jnp.zeros_like(db_ref)

        sel0 = lax.broadcasted_iota(jnp.int32, (CHUNK, LANES), 1) < HEAD_DIM
        sels = (sel0, jnp.logical_not(sel0))
        chunks = _attn_load(j, cps, q_ref, k_ref, v_ref)
        pairs = [(cc, sub) for cc in range(cps) for sub in range(2)]
        qms, probs = _attn_probs(chunks, b_ref, sels, scale)
        doms = [[jnp.where(sel, do_ref[cc * CHUNK:(cc + 1) * CHUNK, :], jnp.zeros((CHUNK, LANES), BF16))
                 for sel in sels] for cc in range(cps)]
        dps = {(cc, sub): lax.dot_general(doms[cc][sub], chunks[cc][2], NT, preferred_element_type=F32)
               for cc, sub in pairs}
        dss = {}
        for cc, sub in pairs:
            p = probs[cc][sub]
            dss[cc, sub] = p * (dps[cc, sub] - jnp.sum(dps[cc, sub] * p, axis=-1, keepdims=True))
        dsb = {key: ds.astype(BF16) for key, ds in dss.items()}
        dqs = {(cc, sub): jnp.dot(dsb[cc, sub], chunks[cc][1], preferred_element_type=F32) * scale
               for cc, sub in pairs}
        dks = {(cc, sub): lax.dot_general(dsb[cc, sub], qms[cc][sub], TN, preferred_element_type=F32) * scale
               for cc, sub in pairs}
        dvs = {(cc, sub): lax.dot_general(probs[cc][sub].astype(BF16), doms[cc][sub], TN,
                                          preferred_element_type=F32) for cc, sub in pairs}
        dq_ref[...] = jnp.concatenate([jnp.where(sel0, dqs[cc, 0], dqs[cc, 1]) for cc in range(cps)],
                                      axis=0).astype(BF16)
        for sub in range(2):
            total = dss[0, sub]
            for cc in range(1, cps):
                total = total + dss[cc, sub]
            db_ref[sub] += total
        dk_parts = [dks[cc, 0] + dks[cc, 1] for cc in range(cps)]
        dv_parts = [dvs[cc, 0] + dvs[cc, 1] for cc in range(cps)]

        def window(parts):
            blocks = []
            for rb in range(cps - 1 + BAND // CHUNK):
                acc = None
                for cc in range(cps):
                    b = rb - cc
                    if 0 <= b < BAND // CHUNK:
                        piece = parts[cc][b * CHUNK:(b + 1) * CHUNK, :]
                        acc = piece if acc is None else acc + piece
                blocks.append(acc)
            return jnp.concatenate(blocks, axis=0)

        span = pl.ds(pl.multiple_of(j * cps * CHUNK, CHUNK), (cps - 1) * CHUNK + BAND)
        dk_acc[span, :] += window(dk_parts)
        dv_acc[span, :] += window(dv_parts)

        @pl.when(j == nj - 1)
        def _():
            dk_ref[...] = dk_acc[LEFT_PAD:, :]
            dv_ref[...] = dv_acc[LEFT_PAD:, :]

    q_spec = pl.BlockSpec((tq, LANES), lambda h, j: (j, h))
    kv_spec = pl.BlockSpec((S + LEFT_PAD, LANES), lambda h, j: (0, h))
    col_spec = pl.BlockSpec((S, LANES), lambda h, j: (0, h))
    b_spec = pl.BlockSpec((2, CHUNK, BAND), lambda h, j: (h, 0, 0))
    return pl.pallas_call(
        body, name=name, grid=(HP, nj),
        in_specs=[q_spec, q_spec, kv_spec, kv_spec, b_spec, col_spec, col_spec],
        out_specs=[q_spec, col_spec, col_spec, b_spec],
        out_shape=[jax.ShapeDtypeStruct((S, D), BF16), jax.ShapeDtypeStruct((S, D), F32),
                   jax.ShapeDtypeStruct((S, D), F32), jax.ShapeDtypeStruct((NH, CHUNK, BAND), F32)],
        scratch_shapes=[pltpu.VMEM((S + LEFT_PAD, LANES), F32), pltpu.VMEM((S + LEFT_PAD, LANES), F32)],
        compiler_params=_params(("parallel", "arbitrary")),
    )(q, do, kpad, vpad, bias, dk_in, dv_in)


def _rel_onehot(t):
    r = lax.broadcasted_iota(jnp.int32, (BAND, N_REL), 0)
    i = lax.broadcasted_iota(jnp.int32, (BAND, N_REL), 1)
    idx = jnp.clip(t + LEFT_PAD - r, -(CHUNK - 1), MAX_REL) + (CHUNK - 1)
    return (idx == i).astype(F32)


def bias_expand(rb, name):
    NH = rb.shape[0]

    def body(rb_ref, o_ref):
        def step(t, carry):
            o_ref[t] = lax.dot_general(rb_ref[...], _rel_onehot(t), NT, preferred_element_type=F32,
                                       precision=HIGHEST)
            return carry

        lax.fori_loop(0, CHUNK, step, 0)

    return pl.pallas_call(
        body, name=name, in_specs=[VMEM_SPEC], out_specs=VMEM_SPEC,
        out_shape=jax.ShapeDtypeStruct((CHUNK, NH, BAND), F32), compiler_params=_params(),
    )(rb)


def bias_grad(dsum, name):
    NH = dsum.shape[1]

    def body(d_ref, o_ref):
        def step(t, acc):
            return acc + jnp.dot(d_ref[t], _rel_onehot(t), preferred_element_type=F32, precision=HIGHEST)

        o_ref[...] = lax.fori_loop(0, CHUNK, step, jnp.zeros((NH, N_REL), F32))

    return pl.pallas_call(
        body, name=name, in_specs=[VMEM_SPEC], out_specs=VMEM_SPEC,
        out_shape=jax.ShapeDtypeStruct((NH, N_REL), F32), compiler_params=_params(),
    )(dsum)


def loss_grad(y, tgt, name):
    S, D = y.shape
    tm = min(MATMUL_ROW_TILE, S)

    def body(y_ref, t_ref, d_ref, acc_ref):
        @pl.when(pl.program_id(0) == 0)
        def _():
            acc_ref[...] = jnp.zeros_like(acc_ref)

        err = y_ref[...] - t_ref[...]
        d_ref[...] = err * (1.0 / D)
        acc_ref[0:1, :] += jnp.sum(err * err, axis=0, keepdims=True)

    tile = pl.BlockSpec((tm, D), _row(0))
    return pl.pallas_call(
        body, name=name, grid=(S // tm,), in_specs=[tile, tile],
        out_specs=[tile, pl.BlockSpec((8, D), _fixed2)],
        out_shape=[jax.ShapeDtypeStruct((S, D), F32), jax.ShapeDtypeStruct((8, D), F32)],
        compiler_params=_params(("arbitrary",)),
    )(y, tgt)


def _col_tile(n):
    for t in (768, 512, 256, 128):
        if n % t == 0:
            return t
    return n


def ada_fwd(c_all, w, b, name):
    L, D, n = w.shape
    tn = _col_tile(n)

    def body(c_ref, w_ref, b_ref, o_ref):
        cv = c_ref[...]
        ca = cv * _sigmoid(cv)
        o_ref[...] = jnp.dot(ca, w_ref[...], preferred_element_type=F32, precision=HIGHEST) + b_ref[...]

    return pl.pallas_call(
        body, name=name, grid=(L, n // tn),
        in_specs=[pl.BlockSpec((N_DEV, D), lambda l, j: (0, 0)), pl.BlockSpec((None, D, tn), lambda l, j: (l, 0, j)),
                  pl.BlockSpec((None, 1, tn), lambda l, j: (l, 0, j))],
        out_specs=pl.BlockSpec((None, N_DEV, tn), lambda l, j: (l, 0, j)),
        out_shape=jax.ShapeDtypeStruct((L, N_DEV, n), F32),
        compiler_params=_params(("parallel", "parallel")),
    )(c_all, w, b)


def ada_wgrad(c_all_t, dmod, name):
    L, _, n = dmod.shape
    D = c_all_t.shape[0]
    tn = _col_tile(n)

    def body(c_ref, d_ref, o_ref):
        cv = c_ref[...]
        ca = cv * _sigmoid(cv)
        o_ref[...] = jnp.dot(ca, d_ref[...], preferred_element_type=F32, precision=HIGHEST)

    return pl.pallas_call(
        body, name=name, grid=(L, n // tn),
        in_specs=[pl.BlockSpec((D, N_DEV), lambda l, j: (0, 0)), pl.BlockSpec((None, N_DEV, tn), lambda l, j: (l, 0, j))],
        out_specs=pl.BlockSpec((None, D, tn), lambda l, j: (l, 0, j)),
        out_shape=jax.ShapeDtypeStruct((L, D, n), F32),
        compiler_params=_params(("parallel", "parallel")),
    )(c_all_t, dmod)


ELEMENTWISE_BLOCK_BYTES = 3 * 1024 * 1024


def _elementwise_rows(rows, row_bytes):
    for t in (4096, 2048, 1024, 512, 256, 128, 64, 32, 16):
        if rows % t == 0 and t * row_bytes <= ELEMENTWISE_BLOCK_BYTES:
            return t
    return rows


def sum_leading(a, name):
    n, M, N = a.shape
    tr = _elementwise_rows(M, n * N * 4)

    def body(a_ref, o_ref):
        acc = a_ref[0]
        for i in range(1, n):
            acc = acc + a_ref[i]
        o_ref[...] = acc

    return pl.pallas_call(
        body, name=name, grid=(M // tr,),
        in_specs=[pl.BlockSpec((n, tr, N), lambda i: (0, i, 0))],
        out_specs=pl.BlockSpec((tr, N), _row(0)),
        out_shape=jax.ShapeDtypeStruct((M, N), F32),
        compiler_params=_params(("parallel",)),
    )(a)


def chip_sum(psum, land, chip_idx, name):
    n, M, N = psum.shape
    tr = _elementwise_rows(M, N * 8)

    def body(s_ref, p_ref, a_ref, b_ref, c_ref, o_ref):
        o_ref[...] = ((p_ref[...].astype(F32) + a_ref[...].astype(F32)) + b_ref[...].astype(F32)) + c_ref[...].astype(F32)

    def entry(j):
        return pl.BlockSpec((None, tr, N), lambda i, s: ((s[0] + j) % n, i, 0))

    return pl.pallas_call(
        body, name=name,
        grid_spec=pltpu.PrefetchScalarGridSpec(
            num_scalar_prefetch=1, grid=(M // tr,),
            in_specs=[entry(0), entry(1), entry(2), entry(3)],
            out_specs=pl.BlockSpec((tr, N), lambda i, s: (i, 0))),
        out_shape=jax.ShapeDtypeStruct((M, N), F32),
        compiler_params=_params(("parallel",)),
    )(chip_idx, psum, land, land, land)


def adamw(w, g, m, v, name):
    M, N = w.shape
    tr = _elementwise_rows(M, N * 4)
    c1 = 1.0 - ADAM_B1 ** ADAM_STEP
    c2 = 1.0 - ADAM_B2 ** ADAM_STEP

    def body(w_ref, g_ref, m_ref, v_ref, d_ref, nm_ref, nv_ref):
        g = g_ref[...]
        nm = ADAM_B1 * m_ref[...] + (1.0 - ADAM_B1) * g
        nv = ADAM_B2 * v_ref[...] + (1.0 - ADAM_B2) * (g * g)
        d_ref[...] = -ADAM_LR * ((nm / c1) / (jnp.sqrt(nv / c2) + ADAM_EPS) + ADAM_WD * w_ref[...])
        nm_ref[...] = nm
        nv_ref[...] = nv

    spec = pl.BlockSpec((tr, N), _row(0))
    shp = jax.ShapeDtypeStruct((M, N), F32)
    return pl.pallas_call(
        body, name=name, grid=(M // tr,), in_specs=[spec] * 4, out_specs=[spec] * 3, out_shape=[shp] * 3,
        compiler_params=_params(("parallel",)),
    )(w, g, m, v)


def _coords():
    return lax.axis_index("x"), lax.axis_index("y"), lax.axis_index("c")


def all_gather8(block, name):
    m_per, n = block.shape

    def body(x_ref, out_ref, send_sems, recv_sems, local_sem):
        x, y, c = _coords()
        me, sibling = (x, y, c), (x, y, 1 - c)
        chips = [(1 - x, y), (x, 1 - y), (1 - x, 1 - y)]

        def rows(px, py, pc):
            return out_ref.at[pl.ds((4 * px + 2 * py + pc) * m_per, m_per), :]

        def copy(k, blk, to, src=None):
            return pltpu.make_async_remote_copy(
                src_ref=rows(*blk) if src is None else src, dst_ref=rows(*blk),
                send_sem=send_sems.at[k], recv_sem=recv_sems.at[k], device_id=to, device_id_type=MESH)

        mine = pltpu.make_async_copy(x_ref, rows(*me), local_sem)
        mine.start()
        first = [copy(0, me, sibling, src=x_ref)]
        first += [copy(1 + j, me, (*chip, c), src=x_ref) for j, chip in enumerate(chips)]
        for cp in first:
            cp.start()
        passed = [copy(4 + j, (*chip, c), sibling) for j, chip in enumerate(chips)]
        for j, chip in enumerate(chips):
            copy(1 + j, (*chip, c), me).wait_recv()
            passed[j].start()
        copy(0, sibling, me).wait_recv()
        for j, chip in enumerate(chips):
            copy(4 + j, (*chip, 1 - c), me).wait_recv()
        for cp in first + passed:
            cp.wait_send()
        mine.wait()

    return pl.pallas_call(
        body, name=name, in_specs=[VMEM_SPEC], out_specs=VMEM_SPEC,
        out_shape=jax.ShapeDtypeStruct((N_DEV * m_per, n), block.dtype),
        scratch_shapes=[pltpu.SemaphoreType.DMA((7,)), pltpu.SemaphoreType.DMA((7,)), pltpu.SemaphoreType.DMA],
        compiler_params=_params(),
    )(block)


def _other_chips(x, y):
    return [(1 - x, y), (x, 1 - y), (1 - x, 1 - y)]


HBM_SPEC = pl.BlockSpec(memory_space=pltpu.HBM)
SEM_SPEC = pl.BlockSpec(memory_space=pltpu.SEMAPHORE)
DATAFLOW = pltpu.SideEffectType.DATAFLOW_SIDE_EFFECTING


def _chip_peers(x, y, c):
    return [(px, py, c) for px, py in _other_chips(x, y)]


def _sibling_peer(x, y, c):
    return [(x, y, 1 - c)]


def _weight_desc(stack_ref, k, land_ref, peer, me):
    h = land_ref.shape[1] // 2
    rows = pl.ds(me[2] * h, h)
    return (stack_ref.at[k, rows, :], land_ref.at[2 * me[0] + me[1], rows, :],
            land_ref.at[2 * peer[0] + peer[1], rows, :])


def _grad_desc(psum_ref, k, land_ref, peer, me):
    return psum_ref.at[2 * peer[0] + peer[1]], land_ref.at[2 * me[0] + me[1]], land_ref.at[2 * peer[0] + peer[1]]


def _pair_desc(grad_ref, k, land_ref, peer, me):
    h = land_ref.shape[1]
    return grad_ref.at[:, pl.ds(peer[2] * h, h), :], land_ref, land_ref


def _whole_desc(src_ref, k, land_ref, peer, me):
    return src_ref, land_ref, land_ref


def exchange_start(srcs, lands, units, groups, desc, peers, after, name):
    n_s, n_l, n_g = len(srcs), len(lands), len(groups)
    n_p = len(peers(0, 0, 0))

    def body(*refs):
        s_refs, l_refs = refs[:n_s], refs[n_s:n_s + n_l]
        outs = refs[n_s + n_l + 1:]
        sems, token = outs[:2 * n_g], outs[-1]
        me = _coords()
        for g, ids in enumerate(groups):
            for i, u in enumerate(ids):
                si, k = units[u]
                for j, peer in enumerate(peers(*me)):
                    src, dst, _ = desc(s_refs[si], k, l_refs[u], peer, me)
                    pltpu.make_async_remote_copy(
                        src_ref=src, dst_ref=dst, send_sem=sems[2 * g].at[n_p * i + j],
                        recv_sem=sems[2 * g + 1].at[n_p * i + j], device_id=peer, device_id_type=MESH).start()
        token[...] = jnp.zeros_like(token)

    arrs = list(srcs) + list(lands)
    sem_shapes = [pltpu.SemaphoreType.DMA((n_p * len(ids),)) for ids in groups for _ in range(2)]
    outs = pl.pallas_call(
        body, name=name,
        in_specs=[HBM_SPEC] * len(arrs) + [ANY],
        out_specs=[SEM_SPEC] * (2 * n_g) + [HBM_SPEC] * len(arrs) + [VMEM_SPEC],
        out_shape=sem_shapes + [pltpu.HBM(a.shape, a.dtype) for a in arrs] + [jax.ShapeDtypeStruct((8, LANES), F32)],
        input_output_aliases={i: 2 * n_g + i for i in range(len(arrs))},
        compiler_params=pltpu.CompilerParams(has_side_effects=DATAFLOW),
    )(*[pltpu.with_memory_space_constraint(a, pltpu.HBM) for a in arrs], after)
    sems = outs[:2 * n_g]
    thru = outs[2 * n_g:2 * n_g + len(arrs)]
    return sems, list(thru[:n_s]), list(thru[n_s:]), outs[-1]


def exchange_wait(srcs, lands, units, send_sem, recv_sem, desc, peers, after, name):
    n_s, n_l = len(srcs), len(lands)
    n_p = len(peers(0, 0, 0))

    def body(*refs):
        s_refs, l_refs = refs[:n_s], refs[n_s:n_s + n_l]
        send_sems, recv_sems = refs[n_s + n_l], refs[n_s + n_l + 1]
        me = _coords()
        for i, (si, k) in enumerate(units):
            for j, peer in enumerate(peers(*me)):
                src, _, mine = desc(s_refs[si], k, l_refs[i], peer, me)
                cp = pltpu.make_async_remote_copy(
                    src_ref=src, dst_ref=mine, send_sem=send_sems.at[n_p * i + j], recv_sem=recv_sems.at[n_p * i + j],
                    device_id=peer, device_id_type=MESH)
                cp.wait_send()
                cp.wait_recv()

    arrs = list(srcs) + list(lands)
    outs = pl.pallas_call(
        body, name=name,
        in_specs=[HBM_SPEC] * len(arrs) + [SEM_SPEC, SEM_SPEC, ANY],
        out_specs=[HBM_SPEC] * len(arrs),
        out_shape=[pltpu.HBM(a.shape, a.dtype) for a in arrs],
        input_output_aliases={i: i for i in range(len(arrs))},
        compiler_params=pltpu.CompilerParams(has_side_effects=DATAFLOW),
    )(*arrs, send_sem, recv_sem, after)
    return list(outs[:n_s]), list(outs[n_s:])


def sibling_fill(lands, name):
    n_u = len(lands)

    def body(*refs):
        ins, outs = refs[:n_u], refs[n_u:2 * n_u]
        send_sems, recv_sems = refs[2 * n_u:]
        x, y, c = _coords()
        sends = []
        for u in range(n_u):
            h = ins[u].shape[1] // 2
            for j, (px, py) in enumerate(_other_chips(x, y)):
                part = (2 * px + py, pl.ds(c * h, h), slice(None))
                cp = pltpu.make_async_remote_copy(
                    src_ref=ins[u].at[part], dst_ref=outs[u].at[part], send_sem=send_sems.at[3 * u + j],
                    recv_sem=recv_sems.at[3 * u + j], device_id=(x, y, 1 - c), device_id_type=MESH)
                cp.start()
                sends.append(cp)
        for u in range(n_u):
            h = ins[u].shape[1] // 2
            for j, (px, py) in enumerate(_other_chips(x, y)):
                theirs = (2 * px + py, pl.ds((1 - c) * h, h), slice(None))
                pltpu.make_async_remote_copy(
                    src_ref=ins[u].at[theirs], dst_ref=outs[u].at[theirs], send_sem=send_sems.at[3 * u + j],
                    recv_sem=recv_sems.at[3 * u + j], device_id=(x, y, 1 - c), device_id_type=MESH).wait_recv()
        for cp in sends:
            cp.wait_send()

    return pl.pallas_call(
        body, name=name, in_specs=[ANY] * n_u, out_specs=[ANY] * n_u,
        out_shape=[jax.ShapeDtypeStruct(a.shape, a.dtype) for a in lands],
        input_output_aliases={i: i for i in range(n_u)},
        scratch_shapes=[pltpu.SemaphoreType.DMA((3 * n_u,)), pltpu.SemaphoreType.DMA((3 * n_u,))],
        compiler_params=_params(),
    )(*lands)


def _pack_rows(parts):
    flat = jnp.concatenate([p.reshape(-1).astype(F32) for p in parts])
    n = flat.shape[0]
    padded = -(-n // (8 * LANES)) * (8 * LANES)
    return jnp.pad(flat, (0, padded - n)).reshape(-1, LANES)


def _unpack_rows(packed, shapes):
    flat = packed.reshape(-1)
    out, off = [], 0
    for s in shapes:
        size = 1
        for d in s:
            size *= d
        out.append(flat[off:off + size].reshape(s))
        off += size
    return out


def _shard_last(full, s_me):
    n = full.shape[-1] // N_CHIP
    return lax.dynamic_slice_in_dim(full, s_me * n, n, axis=full.ndim - 1)


def _unshard_last(g):
    moved = jnp.moveaxis(g, 0, -2)
    return moved.reshape(moved.shape[:-2] + (moved.shape[-2] * moved.shape[-1],))


def kernel(x, c, w_ada, b_ada, ln_g, ln_b, ffn_gu, ffn_down, gmlp_w_in, gmlp_b_in, gmlp_ln_g, gmlp_ln_b, gmlp_w_s, gmlp_b_s, gmlp_w_out, w_ada_kv, b_ada_kv, w_kv, attn_w_q, attn_rel_bias, attn_w_o, loss_target, m_w_ada, m_b_ada, m_ln_g, m_ln_b, m_ffn_gu, m_ffn_down, m_gmlp_w_in, m_gmlp_b_in, m_gmlp_ln_g, m_gmlp_ln_b, m_gmlp_w_s, m_gmlp_b_s, m_gmlp_w_out, m_w_ada_kv, m_b_ada_kv, m_w_kv, m_attn_w_q, m_attn_rel_bias, m_attn_w_o, v_w_ada, v_b_ada, v_ln_g, v_ln_b, v_ffn_gu, v_ffn_down, v_gmlp_w_in, v_gmlp_b_in, v_gmlp_ln_g, v_gmlp_ln_b, v_gmlp_w_s, v_gmlp_b_s, v_gmlp_w_out, v_w_ada_kv, v_b_ada_kv, v_w_kv, v_attn_w_q, v_attn_rel_bias, v_attn_w_o):
    xi, yi, ci = _coords()
    s_me = 2 * xi + yi
    dev = 4 * xi + 2 * yi + ci

    x0 = x[0]
    tgt = loss_target[0]
    S, D = x0.shape
    L = w_ada.shape[0]
    NA = gmlp_w_in.shape[0]
    NB = attn_w_q.shape[0]
    NH = D // HEAD_DIM
    alpha = (2.0 * L) ** 0.25
    n_ada = w_ada.shape[2]
    n_kv = w_ada_kv.shape[1]

    stack_names = ["ffn_gu", "ffn_down", "gmlp_w_in", "gmlp_w_out", "w_kv", "attn_w_q", "attn_w_o"]
    stack_src = dict(ffn_gu=ffn_gu, ffn_down=ffn_down, gmlp_w_in=gmlp_w_in, gmlp_w_out=gmlp_w_out, w_kv=w_kv[None],
                     attn_w_q=attn_w_q, attn_w_o=attn_w_o)
    stacks = [stack_src[nm].astype(BF16).reshape((-1,) + stack_src[nm].shape[-2:]) for nm in stack_names]
    units = [(si, k) for si, st in enumerate(stacks) for k in range(st.shape[0])]
    unit_of = {(stack_names[si], k): u for u, (si, k) in enumerate(units)}
    weight_groups = [[("ffn_gu", 0), ("ffn_down", 0)],
                     ([("gmlp_w_in", 0), ("gmlp_w_out", 0)] if NA > 0 else [("attn_w_q", 0), ("attn_w_o", 0)])
                     + [("ffn_gu", 1), ("ffn_down", 1)] + ([("w_kv", 0)] if NA == 0 else [])]
    for l in range(1, L):
        names = [("w_kv", 0)] if l == NA else []
        names += [("ffn_gu", 2 * l), ("ffn_down", 2 * l)]
        names += [("gmlp_w_in", l), ("gmlp_w_out", l)] if l < NA else [("attn_w_q", l - NA), ("attn_w_o", l - NA)]
        names += [("ffn_gu", 2 * l + 1), ("ffn_down", 2 * l + 1)]
        weight_groups.append(names)
    weight_groups = [[unit_of[n] for n in names] for names in weight_groups]

    c_all = all_gather8(jnp.broadcast_to(c, (8, D)), "ag_c").reshape(N_DEV, 8, D)[:, 0]
    b_ada_sh = lax.dynamic_slice_in_dim(b_ada, s_me * n_ada, n_ada, axis=1)
    b_kv_sh = lax.dynamic_slice_in_dim(b_ada_kv, s_me * n_kv, n_kv, axis=0)
    mod_part = ada_fwd(c_all, w_ada, b_ada_sh[:, None, :], "ada_fwd")
    mkv_part = ada_fwd(c_all, w_ada_kv[None], b_kv_sh[None, None, :], "ada_kv_fwd")
    part = jnp.concatenate([jnp.transpose(mod_part, (1, 0, 2)).reshape(N_DEV, L * n_ada), mkv_part[0]], axis=1)
    width = part.shape[1]
    pad_w = -(-width // LANES) * LANES - width
    all_part = all_gather8(jnp.pad(part, ((0, 0), (0, pad_w))), "ag_mod").reshape(N_DEV, N_DEV, width + pad_w)
    mine = lax.dynamic_index_in_dim(all_part[0::2], dev, axis=1, keepdims=False)
    mod = jnp.transpose(mine[:, :L * n_ada].reshape(N_CHIP, L, n_ada), (1, 0, 2)).reshape(L, N_MOD, D)
    mkv = mine[:, L * n_ada:width].reshape(2, D)

    def mrow(l, k):
        return mod[l, k][None, :]

    small_shapes = [ln_g.shape, ln_b.shape, gmlp_b_in.shape, gmlp_ln_g.shape, gmlp_ln_b.shape, attn_rel_bias.shape]
    small_pack = _pack_rows([ln_g, ln_b, gmlp_b_in, gmlp_ln_g, gmlp_ln_b, attn_rel_bias])
    small_all = all_gather8(small_pack, "ag_small_params").reshape((N_DEV,) + small_pack.shape)[0::2]
    sm = [_unpack_rows(small_all[s], small_shapes) for s in range(N_CHIP)]
    ln_g_f, ln_b_f, b_in_f, gln_g_f, gln_b_f, rel_f = [
        _unshard_last(jnp.stack([sm[s][i] for s in range(N_CHIP)])) for i in range(len(small_shapes))]

    lands0 = [lax.dynamic_update_slice(lax.empty((N_CHIP,) + stacks[si].shape[1:], BF16), stacks[si][k][None],
                                       (s_me, 0, 0)) for si, k in units]
    gathers_done = jnp.concatenate([mod.reshape(-1)[:LANES], small_all.reshape(-1)[:LANES]])
    w_sems, stacks_t, lands_t, _ = exchange_start(stacks, lands0, units, weight_groups, _weight_desc, _chip_peers, gathers_done,
                                                  "weight_send_start")
    wg = {}

    def fetch_weights(g, stacks_now, after):
        ids = weight_groups[g]
        stacks_next, got = exchange_wait(stacks_now, [lands_t[u] for u in ids], [units[u] for u in ids],
                                         w_sems[2 * g], w_sems[2 * g + 1], _weight_desc, _chip_peers, after,
                                         "weight_send_wait_%d" % g)
        for u, a in zip(ids, sibling_fill(got, "weight_sibling_fill")):
            wg[u] = a
        return stacks_next

    def W(nm, k):
        return wg[unit_of[(nm, k)]]

    def Wrows(nm, k):
        w4 = W(nm, k)
        return w4.reshape(w4.shape[0] * w4.shape[1], w4.shape[2])

    bst = [jnp.transpose(gmlp_b_s[j]) for j in range(NA)]
    biases = {}

    def make_bias(j, dep):
        rel, _ = lax.optimization_barrier((rel_f[j], dep))
        biases[j] = jnp.transpose(bias_expand(rel, "bias_expand"), (1, 0, 2))
        return biases[j]

    saved = []
    xc = x0
    kpad = vpad = xkv = None
    for l in range(L):
        after = mod if l == 0 else xc
        if l == 1 and NB > 1:
            after = make_bias(1, xc)
        stacks_t = fetch_weights(0 if l == 0 else l + 1, stacks_t, after)
        if l == NA:
            xkv = xc
            kv, hkv = mod_matmul(xc, mkv[1][None], mkv[0][None], W("w_kv", 0), None, BF16, "kv_proj")
            kpad = jnp.pad(kv[:, :D], ((LEFT_PAD, 0), (0, 0)))
            vpad = jnp.pad(kv[:, D:], ((LEFT_PAD, 0), (0, 0)))
        sv = {}
        for i in (0, 2):
            k = 2 * l + i // 2
            gu, hv = mod_matmul(xc, mrow(l, 3 * i + 1), mrow(l, 3 * i), W("ffn_gu", k), None, BF16, "ffn_up")
            gw = 0.5 * (1.0 + mrow(l, 3 * i + 2))
            xn, xh, rs, yv, av = matmul_res_ln(gu, Wrows("ffn_down", k), xc, gw, ln_g_f[l, i][None],
                                               ln_b_f[l, i][None], alpha, True, "ffn_down")
            sv[i] = dict(x=xc, h=hv, gu=gu, a=av, xh=xh, rs=rs, y=yv, gw=gw)
            xc = xn
            if i == 0:
                if l == 0:
                    stacks_t = fetch_weights(1, stacks_t, make_bias(0, xc) if NB > 0 else xc)
                gw = 1.0 + mrow(l, 5)
                if l < NA:
                    pre, hv = mod_matmul(xc, mrow(l, 4), mrow(l, 3), W("gmlp_w_in", l), b_in_f[l][None], F32,
                                         "gmlp_in")
                    qv = sgu_fwd(pre, gln_g_f[l][None], gln_b_f[l][None], gmlp_w_s[l], bst[l], "sgu_fwd")
                    xn, xh, rs, yv = matmul_res_ln(qv, Wrows("gmlp_w_out", l), xc, gw, ln_g_f[l, 1][None],
                                                   ln_b_f[l, 1][None], alpha, False, "gmlp_out")
                    sv[1] = dict(x=xc, h=hv, pre=pre, a=qv, xh=xh, rs=rs, y=yv, gw=gw)
                else:
                    j = l - NA
                    if j not in biases:
                        make_bias(j, xc)
                    qh, hv = mod_matmul(xc, mrow(l, 4), mrow(l, 3), Wrows("attn_w_q", j)[None], None, BF16, "attn_q")
                    ov = attn_fwd(qh, kpad, vpad, biases[j], "attn_fwd")
                    xn, xh, rs, yv = matmul_res_ln(ov, Wrows("attn_w_o", j), xc, gw, ln_g_f[l, 1][None],
                                                   ln_b_f[l, 1][None], alpha, False, "attn_out")
                    sv[1] = dict(x=xc, h=hv, q=qh, a=ov, xh=xh, rs=rs, y=yv, gw=gw)
                xc = xn
        saved.append(sv)

    dx, lacc = loss_grad(xc, tgt, "loss_grad")
    loss = lax.psum((0.5 / D) * jnp.sum(lacc[0]), ("x", "y", "c"))

    gpair = [None] * len(units)
    col_split = {u for u, (si, _) in enumerate(units) if stack_names[si] != "ffn_gu"}
    dmod = [[None] * N_MOD for _ in range(L)]
    d_ln_g = [[None] * 3 for _ in range(L)]
    d_ln_b = [[None] * 3 for _ in range(L)]
    d_b_in, d_gln_g, d_gln_b, d_ws, d_bs, d_rel = ([None] * NA, [None] * NA, [None] * NA, [None] * NA, [None] * NA,
                                                  [None] * NB)
    dk = jnp.zeros((S, D), F32)
    dv = jnp.zeros((S, D), F32)
    dmkv = None

    made = []

    core_idx = ci.astype(jnp.int32).reshape(1)
    chip_idx = s_me.astype(jnp.int32).reshape(1)

    def put(nm, k, a, b, name):
        u = unit_of[(nm, k)]
        rows, cols = stacks[units[u][0]].shape[1:]
        if nm == "ffn_gu":
            g = wgrad_pair(a, b, N_CHIP, rows // 2, cols, lambda j, p: p, lambda j, p: j, core_idx, name)
        elif nm in ("gmlp_w_in", "w_kv"):
            g = wgrad_pair(a, b, N_CHIP, rows, cols // 2, lambda j, p: 0, lambda j, p: 2 * j + p, core_idx, name)
        else:
            g = wgrad_pair(a, b, 1, N_CHIP * rows, cols // 2, lambda j, p: 0, lambda j, p: p, core_idx, name)
        gpair[u] = g.reshape((N_CHIP, -1, g.shape[-1]))
        made.append(u)

    own_half, sib_half = {}, {}
    n_started = [0]

    def start_grad_exchange(ids, after):
        psums = [gpair[u] for u in ids]
        n = len(ids)
        tag = n_started[0]
        n_started[0] += 1
        sems, ps_t, q_t, token = exchange_start(psums, [lax.empty(p.shape, p.dtype) for p in psums],
                                                [(i, 0) for i in range(n)], [list(range(n))], _grad_desc, _chip_peers,
                                                after, "grad_send_start_%d" % tag)
        return dict(ids=ids, tag=tag, sems=sems, ps=ps_t, q=q_t), token

    def finish_grad_exchange(pend, after):
        n = len(pend["ids"])
        ps_t, q = exchange_wait(pend["ps"], pend["q"], [(i, 0) for i in range(n)], pend["sems"][0], pend["sems"][1],
                                _grad_desc, _chip_peers, after, "grad_send_wait_%d" % pend["tag"])
        halves = [chip_sum(ps_t[i], q[i], chip_idx, "grad_chip_sum") for i in range(n)]
        sems, h_t, land_t, _ = exchange_start(halves, [lax.empty(h.shape, h.dtype) for h in halves],
                                              [(i, 0) for i in range(n)], [list(range(n))], _whole_desc,
                                              _sibling_peer, halves[0], "half_send_start_%d" % pend["tag"])
        swaps.append(dict(ids=pend["ids"], tag=pend["tag"], sems=sems, h=h_t, land=land_t))

    def collect_halves(after):
        for sw in swaps:
            n = len(sw["ids"])
            h, land = exchange_wait(sw["h"], sw["land"], [(i, 0) for i in range(n)], sw["sems"][0], sw["sems"][1],
                                    _whole_desc, _sibling_peer, after, "half_send_wait_%d" % sw["tag"])
            for u, mine, theirs in zip(sw["ids"], h, land):
                own_half[u], sib_half[u] = mine, theirs
        swaps.clear()

    swaps = []
    pending = None
    order_token = jnp.zeros((), F32)

    def ln_inputs(l, i):
        t = saved[l][i]
        return (t["xh"], t["rs"], ln_g_f[l, i][None] + order_token, t["y"], t["gw"], 1.0 if i == 1 else 0.5)

    def record_ln(l, i, acc, row0):
        d_ln_g[l][i], d_ln_b[l][i], dmod[l][3 * i + 2] = acc[row0], acc[row0 + 1], acc[row0 + 2]

    ln_done = None
    for l in reversed(range(L)):
        if l == NA - 1:
            dkv = jnp.concatenate([dk, dv], axis=1)
            put("w_kv", 0, hkv, dkv, "kv_wgrad")
            pdxa, pdy, acc = dgrad_mod(dkv, W("w_kv", 0), dx, xkv, mkv[1][None], ln_inputs(l, 2), alpha, "kv_dgrad")
            dmkv = jnp.stack([acc[1], acc[0]])
            record_ln(l, 2, acc, 2)
            ln_done = (pdxa, pdy)
        sv = saved[l]
        for i in (2, 1, 0):
            t = sv[i]
            if ln_done is None:
                dxa, dy, acc1 = ln_res_bwd(dx, *ln_inputs(l, i), alpha, "ln_res_bwd")
                record_ln(l, i, acc1, 0)
            else:
                dxa, dy = ln_done
                ln_done = None
            before = (l, i - 1) if i > 0 else ((l - 1, 2) if l > 0 and l != NA else None)
            prev = ln_inputs(*before) if before is not None else None
            scl = mrow(l, 3 * i + 1)
            if i != 1:
                k = 2 * l + i // 2
                F = t["gu"].shape[1] // 2
                dgu = ffn_act_bwd(dy, Wrows("ffn_down", k), t["gu"], "ffn_act_bwd")
                put("ffn_down", k, t["a"], dy, "ffn_down_wgrad")
                put("ffn_gu", k, t["h"], dgu, "ffn_up_wgrad")
                res = dgrad_mod(dgu, W("ffn_gu", k), dxa, t["x"], scl, prev, alpha, "ffn_up_dgrad")
            elif l < NA:
                dq = matmul_nt(dy, Wrows("gmlp_w_out", l), "gmlp_out_dgrad")
                put("gmlp_w_out", l, t["a"], dy, "gmlp_out_wgrad")
                dpre, dws_l, dss, dgl, dbin = sgu_bwd(dq, t["pre"], gln_g_f[l][None], gln_b_f[l][None], gmlp_w_s[l],
                                                      bst[l], "sgu_bwd")
                d_ws[l] = dws_l
                d_bs[l] = jnp.transpose(group_lane_sum(dss, "sgu_bias_grad")[:, :GMLP_GROUPS])
                d_gln_g[l], d_gln_b[l], d_b_in[l] = dgl[0], dgl[1], dbin[0]
                put("gmlp_w_in", l, t["h"], dpre, "gmlp_in_wgrad")
                res = dgrad_mod(dpre, W("gmlp_w_in", l), dxa, t["x"], scl, prev, alpha, "gmlp_in_dgrad")
            else:
                j = l - NA
                do = matmul_nt(dy, Wrows("attn_w_o", j), "attn_out_dgrad")
                put("attn_w_o", j, t["a"], dy, "attn_out_wgrad")
                dqh, dk, dv, dbias = attn_bwd(t["q"], do, kpad, vpad, biases[j], dk, dv, "attn_bwd")
                d_rel[j] = bias_grad(jnp.transpose(dbias, (1, 0, 2)), "bias_grad")
                put("attn_w_q", j, t["h"], dqh, "attn_q_wgrad")
                res = dgrad_mod(dqh, Wrows("attn_w_q", j)[None], dxa, t["x"], scl, prev, alpha, "attn_q_dgrad")
            acc2 = res[-1]
            dmod[l][3 * i + 1], dmod[l][3 * i] = acc2[0], acc2[1]
            if before is None:
                dx = res[0]
            else:
                record_ln(*before, acc2, 2)
                ln_done = (res[0], res[1])
            if (i == 0 and l > 0) or (i == 1 and l == 0):
                started, token = start_grad_exchange(list(made), acc2)
                made.clear()
                if pending is not None:
                    finish_grad_exchange(pending, acc2)
                pending = started
                order_token = token[0, 0]
    grad_x = dx[None]

    dvec = _pack_rows([jnp.stack([jnp.stack(r) for r in dmod]), dmkv])
    n_dvec = L * N_MOD * D + 2 * D
    dall = all_gather8(dvec, "ag_dmod").reshape(N_DEV, -1, LANES)
    db_all = sum_leading(dall, "ada_bias_grad").reshape(-1)[:n_dvec]
    g_b_ada = db_all[:L * N_MOD * D].reshape(L, N_MOD * D)
    g_b_ada_kv = db_all[L * N_MOD * D:]
    dall2 = dall.reshape(N_DEV, -1)[:, :n_dvec]
    dmod_all = dall2[:, :L * N_MOD * D].reshape(N_DEV, L, N_MOD * D)
    dmod_sh = jnp.transpose(lax.dynamic_slice_in_dim(dmod_all, s_me * n_ada, n_ada, axis=2), (1, 0, 2))
    dmkv_sh = lax.dynamic_slice_in_dim(dall2[:, L * N_MOD * D:], s_me * n_kv, n_kv, axis=1)[None]
    c_all_t = jnp.transpose(c_all)
    g_w_ada = ada_wgrad(c_all_t, dmod_sh, "ada_wgrad")
    g_w_ada_kv = ada_wgrad(c_all_t, dmkv_sh, "ada_kv_wgrad")[0]

    small_g = [jnp.stack([jnp.stack(r) for r in d_ln_g]), jnp.stack([jnp.stack(r) for r in d_ln_b]),
               jnp.stack(d_b_in), jnp.stack(d_gln_g), jnp.stack(d_gln_b), jnp.stack(d_rel), jnp.stack(d_ws),
               jnp.stack(d_bs)]
    sg_shapes = [a.shape for a in small_g]
    sg_pack = _pack_rows(small_g)
    sg_all = all_gather8(sg_pack, "ag_small_grads").reshape(N_DEV, -1, LANES)
    sg_sum = _unpack_rows(sum_leading(sg_all, "small_grad_sum"), sg_shapes)
    g_ln_g, g_ln_b, g_b_in, g_gln_g, g_gln_b, g_rel = [_shard_last(a, s_me) for a in sg_sum[:6]]
    g_ws, g_bs = sg_sum[6], sg_sum[7]

    last, _ = start_grad_exchange(list(made), sg_all)

    grads = dict(w_ada=g_w_ada, b_ada=g_b_ada, ln_g=g_ln_g, ln_b=g_ln_b, gmlp_b_in=g_b_in, gmlp_ln_g=g_gln_g,
                 gmlp_ln_b=g_gln_b, gmlp_w_s=g_ws, gmlp_b_s=g_bs, w_ada_kv=g_w_ada_kv, b_ada_kv=g_b_ada_kv,
                 attn_rel_bias=g_rel)
    weights = dict(w_ada=w_ada, b_ada=b_ada, ln_g=ln_g, ln_b=ln_b, ffn_gu=ffn_gu, ffn_down=ffn_down,
                   gmlp_w_in=gmlp_w_in, gmlp_b_in=gmlp_b_in, gmlp_ln_g=gmlp_ln_g, gmlp_ln_b=gmlp_ln_b,
                   gmlp_w_s=gmlp_w_s, gmlp_b_s=gmlp_b_s, gmlp_w_out=gmlp_w_out, w_ada_kv=w_ada_kv,
                   b_ada_kv=b_ada_kv, w_kv=w_kv, attn_w_q=attn_w_q, attn_rel_bias=attn_rel_bias, attn_w_o=attn_w_o)
    ms = dict(w_ada=m_w_ada, b_ada=m_b_ada, ln_g=m_ln_g, ln_b=m_ln_b, ffn_gu=m_ffn_gu, ffn_down=m_ffn_down,
              gmlp_w_in=m_gmlp_w_in, gmlp_b_in=m_gmlp_b_in, gmlp_ln_g=m_gmlp_ln_g, gmlp_ln_b=m_gmlp_ln_b,
              gmlp_w_s=m_gmlp_w_s, gmlp_b_s=m_gmlp_b_s, gmlp_w_out=m_gmlp_w_out, w_ada_kv=m_w_ada_kv,
              b_ada_kv=m_b_ada_kv, w_kv=m_w_kv, attn_w_q=m_attn_w_q, attn_rel_bias=m_attn_rel_bias,
              attn_w_o=m_attn_w_o)
    vs = dict(w_ada=v_w_ada, b_ada=v_b_ada, ln_g=v_ln_g, ln_b=v_ln_b, ffn_gu=v_ffn_gu, ffn_down=v_ffn_down,
              gmlp_w_in=v_gmlp_w_in, gmlp_b_in=v_gmlp_b_in, gmlp_ln_g=v_gmlp_ln_g, gmlp_ln_b=v_gmlp_ln_b,
              gmlp_w_s=v_gmlp_w_s, gmlp_b_s=v_gmlp_b_s, gmlp_w_out=v_gmlp_w_out, w_ada_kv=v_w_ada_kv,
              b_ada_kv=v_b_ada_kv, w_kv=v_w_kv, attn_w_q=v_attn_w_q, attn_rel_bias=v_attn_rel_bias,
              attn_w_o=v_attn_w_o)
    order = ["w_ada", "b_ada", "ln_g", "ln_b", "ffn_gu", "ffn_down", "gmlp_w_in", "gmlp_b_in", "gmlp_ln_g",
             "gmlp_ln_b", "gmlp_w_s", "gmlp_b_s", "gmlp_w_out", "w_ada_kv", "b_ada_kv", "w_kv", "attn_w_q",
             "attn_rel_bias", "attn_w_o"]
    big_names = ["w_ada", "w_ada_kv"] + stack_names
    small_names = [nm for nm in order if nm not in big_names]
    delta, new_m, new_v = {}, {}, {}

    def adamw_big(nm):
        shp = weights[nm].shape
        two_d = (-1, shp[-1])
        d, a, b = adamw(weights[nm].reshape(two_d), grads[nm].reshape(two_d), ms[nm].reshape(two_d),
                        vs[nm].reshape(two_d), "adamw")
        delta[nm], new_m[nm], new_v[nm] = d.reshape(shp), a.reshape(shp), b.reshape(shp)

    adamw_big("w_ada")
    adamw_big("w_ada_kv")
    shapes = [weights[nm].shape for nm in small_names]
    d, a, b = adamw(_pack_rows([weights[nm] for nm in small_names]), _pack_rows([grads[nm] for nm in small_names]),
                    _pack_rows([ms[nm] for nm in small_names]), _pack_rows([vs[nm] for nm in small_names]),
                    "adamw_small")
    for nm, dd, aa, bb in zip(small_names, _unpack_rows(d, shapes), _unpack_rows(a, shapes), _unpack_rows(b, shapes)):
        delta[nm], new_m[nm], new_v[nm] = dd, aa, bb

    def full_grad(u):
        lo = jnp.where(ci == 0, own_half[u], sib_half[u])
        hi = jnp.where(ci == 0, sib_half[u], own_half[u])
        return jnp.concatenate([lo, hi], axis=1 if u in col_split else 0)

    def adamw_stack(nm):
        si = stack_names.index(nm)
        g = jnp.stack([full_grad(unit_of[(nm, k)]) for k in range(stacks[si].shape[0])])
        grads[nm] = g.reshape(weights[nm].shape)
        adamw_big(nm)

    late = [stack_names[units[u][0]] for u in last["ids"]]
    early = [nm for nm in stack_names if nm not in late]
    finish_grad_exchange(pending, delta["w_ada"])
    collect_halves(delta["w_ada"])
    for nm in early:
        adamw_stack(nm)
    finish_grad_exchange(last, delta[early[-1]])
    collect_halves(delta[early[-1]])
    for nm in stack_names:
        if nm in late:
            adamw_stack(nm)

    return (loss, grad_x, *[grads[nm] for nm in order], *[delta[nm] for nm in order],
            *[new_m[nm] for nm in order], *[new_v[nm] for nm in order])
```

```python
import functools

import jax
import jax.numpy as jnp
from jax import lax
from jax.experimental import pallas as pl
from jax.experimental.pallas import tpu as pltpu

F32 = jnp.float32
BF16 = jnp.bfloat16
MESH = pl.DeviceIdType.MESH
HIGHEST = lax.Precision.HIGHEST

CHUNK = 64
GMLP_WINDOW = 128
GMLP_GROUPS = 8
HEAD_DIM = 64
LEFT_CHUNKS = 8
BAND = (LEFT_CHUNKS + 1) * CHUNK
LEFT_PAD = LEFT_CHUNKS * CHUNK
MAX_REL = 4 * CHUNK
N_REL = (CHUNK - 1) + MAX_REL + 1
LN_EPS = 1e-5
N_MOD = 9
N_DEV = 8
N_CHIP = 4

ADAM_LR = 0.001
ADAM_B1 = 0.9
ADAM_B2 = 0.999
ADAM_EPS = 1e-08
ADAM_WD = 0.01
ADAM_STEP = 10

LANES = 128
ROW_TILE = 256
MATMUL_ROW_TILE = 512
WGRAD_ROWS = 1024
ATTN_CHUNKS_PER_STEP = 4
VMEM_LIMIT_MB = 56

NT = (((1,), (1,)), ((), ()))
TN = (((0,), (0,)), ((), ()))

ANY = pl.BlockSpec(memory_space=pl.ANY)
VMEM_SPEC = pl.BlockSpec(memory_space=pltpu.VMEM)


def _params(semantics=None):
    kw = dict(vmem_limit_bytes=VMEM_LIMIT_MB * 1024 * 1024)
    if semantics is not None:
        kw["dimension_semantics"] = semantics
    return pltpu.CompilerParams(**kw)


def _sigmoid(v):
    return 0.5 * (1.0 + jnp.tanh(0.5 * v))


def _gelu(v):
    return 0.5 * v * (1.0 + lax.erf(v * (2.0 ** -0.5)))


def _gelu_grad(v):
    return 0.5 * (1.0 + lax.erf(v * (2.0 ** -0.5))) + v * jnp.exp(-0.5 * v * v) * ((2.0 * jnp.pi) ** -0.5)


def _row(m):
    return lambda i: (i, 0)


def _fixed2(i):
    return (0, 0)


def _fixed3(i):
    return (0, 0, 0)


def _resident(shape):
    return pl.BlockSpec(shape, _fixed2 if len(shape) == 2 else _fixed3, pipeline_mode=pl.Buffered(1))


def mod_matmul(x, scl, shift, w, bias, out_dtype, name):
    S, D = x.shape
    NS, _, n = w.shape
    tm = min(MATMUL_ROW_TILE, S)
    has_bias = bias is not None

    def body(*refs):
        if has_bias:
            x_ref, scl_ref, sh_ref, w_ref, b_ref, o_ref, h_ref = refs
        else:
            x_ref, scl_ref, sh_ref, w_ref, o_ref, h_ref = refs
        h = (x_ref[...] * (1.0 + scl_ref[...]) + sh_ref[...]).astype(BF16)
        h_ref[...] = h
        for s in range(NS):
            acc = jnp.dot(h, w_ref[s], preferred_element_type=F32)
            if has_bias:
                acc = acc + b_ref[:, s * n:(s + 1) * n]
            o_ref[:, s * n:(s + 1) * n] = acc.astype(out_dtype)

    in_specs = [pl.BlockSpec((tm, D), _row(0)), pl.BlockSpec((1, D), _fixed2), pl.BlockSpec((1, D), _fixed2),
                _resident((NS, D, n))]
    args = [x, scl, shift, w]
    if has_bias:
        in_specs.append(pl.BlockSpec((1, NS * n), _fixed2))
        args.append(bias)
    return pl.pallas_call(
        body, name=name, grid=(S // tm,), in_specs=in_specs,
        out_specs=[pl.BlockSpec((tm, NS * n), _row(0)), pl.BlockSpec((tm, D), _row(0))],
        out_shape=[jax.ShapeDtypeStruct((S, NS * n), out_dtype), jax.ShapeDtypeStruct((S, D), BF16)],
        compiler_params=_params(("parallel",)),
    )(*args)


def matmul_res_ln(a, w, x, gw, lg, lb, alpha, swiglu, name):
    S, D = x.shape
    K = w.shape[0]
    tm = min(ROW_TILE, S)
    ka = a.shape[1]

    def body(a_ref, w_ref, x_ref, gw_ref, lg_ref, lb_ref, xn_ref, xh_ref, rs_ref, y_ref, *act_ref):
        if swiglu:
            g = a_ref[:, :K].astype(F32)
            u = a_ref[:, K:].astype(F32)
            act = (g * _sigmoid(g) * u).astype(BF16)
            act_ref[0][...] = act
        else:
            act = a_ref[...].astype(BF16)
        y = jnp.dot(act, w_ref[...], preferred_element_type=F32)
        z = alpha * x_ref[...] + gw_ref[...] * y
        mu = jnp.mean(z, axis=-1, keepdims=True)
        zc = z - mu
        var = jnp.mean(zc * zc, axis=-1, keepdims=True)
        rstd = lax.rsqrt(var + LN_EPS)
        xhat = zc * rstd
        xn_ref[...] = xhat * lg_ref[...] + lb_ref[...]
        xh_ref[...] = xhat
        rs_ref[...] = rstd
        y_ref[...] = y.astype(BF16)

    vec = pl.BlockSpec((1, D), _fixed2)
    out_specs = [pl.BlockSpec((tm, D), _row(0)), pl.BlockSpec((tm, D), _row(0)), pl.BlockSpec((tm, 1), _row(0)),
                 pl.BlockSpec((tm, D), _row(0))]
    out_shape = [jax.ShapeDtypeStruct((S, D), F32), jax.ShapeDtypeStruct((S, D), F32),
                 jax.ShapeDtypeStruct((S, 1), F32), jax.ShapeDtypeStruct((S, D), BF16)]
    if swiglu:
        out_specs.append(pl.BlockSpec((tm, K), _row(0)))
        out_shape.append(jax.ShapeDtypeStruct((S, K), BF16))
    return pl.pallas_call(
        body, name=name, grid=(S // tm,),
        in_specs=[pl.BlockSpec((tm, ka), _row(0)), _resident((K, D)), pl.BlockSpec((tm, D), _row(0)),
                  vec, vec, vec],
        out_specs=out_specs, out_shape=out_shape,
        compiler_params=_params(("parallel",)),
    )(a, w, x, gw, lg, lb)


def _ln_res_bwd_tile(d, xh_ref, rs_ref, lg_ref, y_ref, gw_ref, wres, alpha, dxa_ref, dy_ref, acc_ref, row0):
    xh = xh_ref[...]
    dxh = d * lg_ref[...]
    m1 = jnp.mean(dxh, axis=-1, keepdims=True)
    m2 = jnp.mean(dxh * xh, axis=-1, keepdims=True)
    dz = rs_ref[...] * (dxh - m1 - xh * m2)
    dxa_ref[...] = alpha * dz
    dy_ref[...] = (gw_ref[...] * dz).astype(BF16)
    acc_ref[row0:row0 + 1, :] += jnp.sum(d * xh, axis=0, keepdims=True)
    acc_ref[row0 + 1:row0 + 2, :] += jnp.sum(d, axis=0, keepdims=True)
    acc_ref[row0 + 2:row0 + 3, :] += jnp.sum((wres * dz) * y_ref[...].astype(F32), axis=0, keepdims=True)


def ln_res_bwd(dxn, xhat, rstd, lg, y, gw, wres, alpha, name):
    S, D = dxn.shape
    tm = min(MATMUL_ROW_TILE, S)

    def body(dxn_ref, xh_ref, rs_ref, lg_ref, y_ref, gw_ref, dxa_ref, dy_ref, acc_ref):
        @pl.when(pl.program_id(0) == 0)
        def _():
            acc_ref[...] = jnp.zeros_like(acc_ref)

        _ln_res_bwd_tile(dxn_ref[...], xh_ref, rs_ref, lg_ref, y_ref, gw_ref, wres, alpha, dxa_ref, dy_ref, acc_ref, 0)

    vec = pl.BlockSpec((1, D), _fixed2)
    tile = pl.BlockSpec((tm, D), _row(0))
    return pl.pallas_call(
        body, name=name, grid=(S // tm,),
        in_specs=[tile, tile, pl.BlockSpec((tm, 1), _row(0)), vec, tile, vec],
        out_specs=[tile, tile, pl.BlockSpec((8, D), _fixed2)],
        out_shape=[jax.ShapeDtypeStruct((S, D), F32), jax.ShapeDtypeStruct((S, D), BF16),
                   jax.ShapeDtypeStruct((8, D), F32)],
        compiler_params=_params(("arbitrary",)),
    )(dxn, xhat, rstd, lg, y, gw)


def ffn_act_bwd(dy, wd, gu, after, name):
    S, D = dy.shape
    K = wd.shape[0]
    tm = min(ROW_TILE, S)

    def body(dy_ref, wd_ref, gu_ref, after_ref, o_ref):
        da = lax.dot_general(dy_ref[...], wd_ref[...], NT, preferred_element_type=F32)
        g = gu_ref[:, :K].astype(F32)
        u = gu_ref[:, K:].astype(F32)
        sg = _sigmoid(g)
        o_ref[:, :K] = (da * u * (sg * (1.0 + g * (1.0 - sg)))).astype(BF16)
        o_ref[:, K:] = (da * (g * sg)).astype(BF16)

    return pl.pallas_call(
        body, name=name, grid=(S // tm,),
        in_specs=[pl.BlockSpec((tm, D), _row(0)), _resident((K, D)), pl.BlockSpec((tm, 2 * K), _row(0)), ANY],
        out_specs=pl.BlockSpec((tm, 2 * K), _row(0)),
        out_shape=jax.ShapeDtypeStruct((S, 2 * K), BF16),
        compiler_params=_params(("parallel",)),
    )(dy, wd, gu, after)


def matmul_nt(a, w, after, name):
    S, D = a.shape
    K = w.shape[0]
    tm = min(MATMUL_ROW_TILE, S)

    def body(a_ref, w_ref, after_ref, o_ref):
        o_ref[...] = lax.dot_general(a_ref[...], w_ref[...], NT, preferred_element_type=F32).astype(BF16)

    return pl.pallas_call(
        body, name=name, grid=(S // tm,),
        in_specs=[pl.BlockSpec((tm, D), _row(0)), _resident((K, D)), ANY],
        out_specs=pl.BlockSpec((tm, K), _row(0)),
        out_shape=jax.ShapeDtypeStruct((S, K), BF16),
        compiler_params=_params(("parallel",)),
    )(a, w, after)


def dgrad_mod(dpre, w, dxa, xin, scl, prev, alpha, name):
    S, D = xin.shape
    NS, _, n = w.shape
    tm = min(ROW_TILE, S)
    wres = prev[5] if prev is not None else None

    def body(*refs):
        dp_ref, w_ref, dxa_ref, xin_ref, scl_ref = refs[:5]
        acc_ref = refs[-1]

        @pl.when(pl.program_id(0) == 0)
        def _():
            acc_ref[...] = jnp.zeros_like(acc_ref)

        dh = jnp.zeros((tm, D), F32)
        for s in range(NS):
            dh = dh + lax.dot_general(dp_ref[:, s * n:(s + 1) * n].astype(BF16), w_ref[s], NT,
                                      preferred_element_type=F32)
        dx = dxa_ref[...] + dh * (1.0 + scl_ref[...])
        acc_ref[0:1, :] += jnp.sum(dh * xin_ref[...], axis=0, keepdims=True)
        acc_ref[1:2, :] += jnp.sum(dh, axis=0, keepdims=True)
        if prev is None:
            refs[5][...] = dx
        else:
            xh_ref, rs_ref, lg_ref, y_ref, gw_ref, pdxa_ref, pdy_ref = refs[5:12]
            _ln_res_bwd_tile(dx, xh_ref, rs_ref, lg_ref, y_ref, gw_ref, wres, alpha, pdxa_ref, pdy_ref, acc_ref, 2)

    tile = pl.BlockSpec((tm, D), _row(0))
    vec = pl.BlockSpec((1, D), _fixed2)
    in_specs = [pl.BlockSpec((tm, NS * n), _row(0)), _resident((NS, D, n)), tile, tile, vec]
    args = [dpre, w, dxa, xin, scl]
    if prev is None:
        out_specs = [tile]
        out_shape = [jax.ShapeDtypeStruct((S, D), F32)]
    else:
        in_specs += [tile, pl.BlockSpec((tm, 1), _row(0)), vec, tile, vec]
        args += list(prev[:5])
        out_specs = [tile, tile]
        out_shape = [jax.ShapeDtypeStruct((S, D), F32), jax.ShapeDtypeStruct((S, D), BF16)]
    return pl.pallas_call(
        body, name=name, grid=(S // tm,), in_specs=in_specs,
        out_specs=out_specs + [pl.BlockSpec((8, D), _fixed2)],
        out_shape=out_shape + [jax.ShapeDtypeStruct((8, D), F32)],
        compiler_params=_params(("arbitrary",)),
    )(*args)


def wgrad_pair(a, b, J, kb, nb, a_block, b_block, half_idx, name):
    S = a.shape[0]
    ts = min(WGRAD_ROWS, S)
    nsteps = S // ts

    def body(h_ref, a_ref, b_ref, o_ref, acc_ref, send_buf, recv_buf, send_sems, recv_sems):
        jj, si = pl.program_id(0), pl.program_id(1)
        x, y, c = _coords()
        j = lax.rem(jj, J)
        last = si == nsteps - 1

        def copy(blk):
            return pltpu.make_async_remote_copy(
                src_ref=send_buf.at[blk], dst_ref=recv_buf.at[blk], send_sem=send_sems.at[blk],
                recv_sem=recv_sems.at[blk], device_id=(x, y, 1 - c), device_id_type=MESH)

        @pl.when(si == 0)
        def _():
            acc_ref[...] = jnp.zeros_like(acc_ref)

        acc_ref[...] += lax.dot_general(a_ref[...], b_ref[...].astype(BF16), TN, preferred_element_type=F32)

        @pl.when(jnp.logical_and(last, jj < J))
        def _():
            send_buf[j] = acc_ref[...].astype(BF16)
            copy(j).start()

        @pl.when(jnp.logical_and(last, jj >= J))
        def _():
            copy(j).wait_recv()
            o_ref[...] = (acc_ref[...] + recv_buf[j].astype(F32)).astype(BF16)

        @pl.when(jnp.logical_and(last, jj == 2 * J - 1))
        def _():
            for blk in range(J):
                copy(blk).wait_send()

    def half(jj, h):
        return jnp.where(jj < J, 1 - h[0], h[0])

    return pl.pallas_call(
        body, name=name,
        grid_spec=pltpu.PrefetchScalarGridSpec(
            num_scalar_prefetch=1, grid=(2 * J, nsteps),
            in_specs=[pl.BlockSpec((ts, kb), lambda jj, s, h: (s, a_block(lax.rem(jj, J), half(jj, h)))),
                      pl.BlockSpec((ts, nb), lambda jj, s, h: (s, b_block(lax.rem(jj, J), half(jj, h))))],
            out_specs=pl.BlockSpec((None, kb, nb), lambda jj, s, h: (jnp.maximum(jj - J, 0), 0, 0)),
            scratch_shapes=[pltpu.VMEM((kb, nb), F32), pltpu.VMEM((J, kb, nb), BF16), pltpu.VMEM((J, kb, nb), BF16),
                            pltpu.SemaphoreType.DMA((J,)), pltpu.SemaphoreType.DMA((J,))]),
        out_shape=jax.ShapeDtypeStruct((J, kb, nb), BF16),
        compiler_params=_params(("arbitrary", "arbitrary")),
    )(half_idx, a, b)


def _window_mask():
    t = lax.broadcasted_iota(jnp.int32, (GMLP_WINDOW, GMLP_WINDOW), 0)
    s = lax.broadcasted_iota(jnp.int32, (GMLP_WINDOW, GMLP_WINDOW), 1)
    return ((s // CHUNK) <= (t // CHUNK)).astype(F32)


def sgu_fwd(pre, glg, glb, ws, bst, name):
    S, H2 = pre.shape
    H = H2 // 2
    W, G = GMLP_WINDOW, GMLP_GROUPS
    gd = H // G
    tm = min(ROW_TILE, S)

    def body(pre_ref, glg_ref, glb_ref, ws_ref, bst_ref, q_ref):
        u = _gelu(pre_ref[:, :H])
        v = _gelu(pre_ref[:, H:])
        mu = jnp.mean(v, axis=-1, keepdims=True)
        vc = v - mu
        var = jnp.mean(vc * vc, axis=-1, keepdims=True)
        vn = ((vc * lax.rsqrt(var + LN_EPS)) * glg_ref[...] + glb_ref[...]).astype(BF16)
        mask = _window_mask()
        for g in range(G):
            wsg = (ws_ref[g] * mask).astype(BF16)
            bcol = bst_ref[:, g:g + 1]
            for wi in range(tm // W):
                rows = slice(wi * W, (wi + 1) * W)
                cols = slice(g * gd, (g + 1) * gd)
                s = jnp.dot(wsg, vn[rows, cols], preferred_element_type=F32) + bcol
                q_ref[rows, cols] = (u[rows, cols] * s).astype(BF16)

    return pl.pallas_call(
        body, name=name, grid=(S // tm,),
        in_specs=[pl.BlockSpec((tm, H2), _row(0)), pl.BlockSpec((1, H), _fixed2), pl.BlockSpec((1, H), _fixed2),
                  pl.BlockSpec((G, W, W), _fixed3), pl.BlockSpec((W, G), _fixed2)],
        out_specs=pl.BlockSpec((tm, H), _row(0)),
        out_shape=jax.ShapeDtypeStruct((S, H), BF16),
        compiler_params=_params(("parallel",)),
    )(pre, glg, glb, ws, bst)


def sgu_bwd(dq, pre, glg, glb, ws, bst, name):
    S, H2 = pre.shape
    H = H2 // 2
    W, G = GMLP_WINDOW, GMLP_GROUPS
    gd = H // G
    tm = min(ROW_TILE, S)

    def body(dq_ref, pre_ref, glg_ref, glb_ref, ws_ref, bst_ref,
             dpre_ref, dws_ref, dss_ref, dgl_ref, dbin_ref, du_s, dvn_s):
        @pl.when(pl.program_id(0) == 0)
        def _():
            dws_ref[...] = jnp.zeros_like(dws_ref)
            dss_ref[...] = jnp.zeros_like(dss_ref)
            dgl_ref[...] = jnp.zeros_like(dgl_ref)
            dbin_ref[...] = jnp.zeros_like(dbin_ref)

        pu = pre_ref[:, :H]
        pv = pre_ref[:, H:]
        u = _gelu(pu)
        v = _gelu(pv)
        mu = jnp.mean(v, axis=-1, keepdims=True)
        vc = v - mu
        var = jnp.mean(vc * vc, axis=-1, keepdims=True)
        rstd = lax.rsqrt(var + LN_EPS)
        vhat = vc * rstd
        vn = (vhat * glg_ref[...] + glb_ref[...]).astype(BF16)
        mask = _window_mask()
        for g in range(G):
            wsg = (ws_ref[g] * mask).astype(BF16)
            bcol = bst_ref[:, g:g + 1]
            cols = slice(g * gd, (g + 1) * gd)
            for wi in range(tm // W):
                rows = slice(wi * W, (wi + 1) * W)
                vblk = vn[rows, cols]
                s = jnp.dot(wsg, vblk, preferred_element_type=F32) + bcol
                dqb = dq_ref[rows, cols].astype(F32)
                du_s[rows, cols] = dqb * s
                ds = dqb * u[rows, cols]
                dss_ref[:, cols] += ds
                dsb = ds.astype(BF16)
                dvn_s[rows, cols] = lax.dot_general(wsg, dsb, TN, preferred_element_type=F32)
                dws_ref[g] += lax.dot_general(dsb, vblk, NT, preferred_element_type=F32) * mask
        dvn = dvn_s[...]
        dgl_ref[0:1, :] += jnp.sum(dvn * vhat, axis=0, keepdims=True)
        dgl_ref[1:2, :] += jnp.sum(dvn, axis=0, keepdims=True)
        dvh = dvn * glg_ref[...]
        m1 = jnp.mean(dvh, axis=-1, keepdims=True)
        m2 = jnp.mean(dvh * vhat, axis=-1, keepdims=True)
        dv = rstd * (dvh - m1 - vhat * m2)
        dpu = du_s[...] * _gelu_grad(pu)
        dpv = dv * _gelu_grad(pv)
        dbin_ref[0:1, :H] += jnp.sum(dpu, axis=0, keepdims=True)
        dbin_ref[0:1, H:] += jnp.sum(dpv, axis=0, keepdims=True)
        dpre_ref[:, :H] = dpu.astype(BF16)
        dpre_ref[:, H:] = dpv.astype(BF16)

    return pl.pallas_call(
        body, name=name, grid=(S // tm,),
        in_specs=[pl.BlockSpec((tm, H), _row(0)), pl.BlockSpec((tm, H2), _row(0)), pl.BlockSpec((1, H), _fixed2),
                  pl.BlockSpec((1, H), _fixed2), pl.BlockSpec((G, W, W), _fixed3), pl.BlockSpec((W, G), _fixed2)],
        out_specs=[pl.BlockSpec((tm, H2), _row(0)), pl.BlockSpec((G, W, W), _fixed3), pl.BlockSpec((W, H), _fixed2),
                   pl.BlockSpec((8, H), _fixed2), pl.BlockSpec((8, H2), _fixed2)],
        out_shape=[jax.ShapeDtypeStruct((S, H2), BF16), jax.ShapeDtypeStruct((G, W, W), F32),
                   jax.ShapeDtypeStruct((W, H), F32), jax.ShapeDtypeStruct((8, H), F32),
                   jax.ShapeDtypeStruct((8, H2), F32)],
        scratch_shapes=[pltpu.VMEM((tm, H), F32), pltpu.VMEM((tm, H), F32)],
        compiler_params=_params(("arbitrary",)),
    )(dq, pre, glg, glb, ws, bst)


def group_lane_sum(dss, name):
    W, H = dss.shape
    gd = H // GMLP_GROUPS

    def body(d_ref, o_ref):
        j = lax.broadcasted_iota(jnp.int32, (H, LANES), 0)
        g = lax.broadcasted_iota(jnp.int32, (H, LANES), 1)
        ind = ((j // gd) == g).astype(F32)
        o_ref[...] = jnp.dot(d_ref[...], ind, preferred_element_type=F32, precision=HIGHEST)

    return pl.pallas_call(
        body, name=name, in_specs=[VMEM_SPEC], out_specs=VMEM_SPEC,
        out_shape=jax.ShapeDtypeStruct((W, LANES), F32), compiler_params=_params(),
    )(dss)


def _attn_load(j, cps, q_ref, k_ref, v_ref):
    r = lax.broadcasted_iota(jnp.int32, (CHUNK, BAND), 1)
    chunks = []
    for cc in range(cps):
        start = pl.multiple_of((j * cps + cc) * CHUNK, CHUNK)
        chunks.append((q_ref[cc * CHUNK:(cc + 1) * CHUNK, :], k_ref[pl.ds(start, BAND), :],
                       v_ref[pl.ds(start, BAND), :], (r + start) >= LEFT_PAD))
    return chunks


def _attn_probs(chunks, b_ref, sels, scale):
    qms = [[jnp.where(sel, q2, jnp.zeros_like(q2)) for sel in sels] for q2, _, _, _ in chunks]
    raw = [[lax.dot_general(qm, k2, NT, preferred_element_type=F32) for qm in qms[cc]]
           for cc, (_, k2, _, _) in enumerate(chunks)]
    probs = []
    for cc, (_, _, _, valid) in enumerate(chunks):
        row = []
        for sub in range(2):
            s = jnp.where(valid, raw[cc][sub] * scale + b_ref[sub], -jnp.inf)
            e = jnp.exp(s - jnp.max(s, axis=-1, keepdims=True))
            row.append(e / jnp.sum(e, axis=-1, keepdims=True))
        probs.append(row)
    return qms, probs


def attn_fwd(q, kpad, vpad, bias, name):
    S, D = q.shape
    HP = D // LANES
    cps = min(ATTN_CHUNKS_PER_STEP, S // CHUNK)
    tq = cps * CHUNK
    scale = HEAD_DIM ** -0.5

    def body(q_ref, k_ref, v_ref, b_ref, o_ref):
        sel0 = lax.broadcasted_iota(jnp.int32, (CHUNK, LANES), 1) < HEAD_DIM
        chunks = _attn_load(pl.program_id(1), cps, q_ref, k_ref, v_ref)
        _, probs = _attn_probs(chunks, b_ref, (sel0, jnp.logical_not(sel0)), scale)
        outs = [[jnp.dot(probs[cc][sub].astype(BF16), v2, preferred_element_type=F32) for sub in range(2)]
                for cc, (_, _, v2, _) in enumerate(chunks)]
        o_ref[...] = jnp.concatenate([jnp.where(sel0, o[0], o[1]) for o in outs], axis=0).astype(BF16)

    kv_spec = pl.BlockSpec((S + LEFT_PAD, LANES), lambda h, j: (0, h))
    return pl.pallas_call(
        body, name=name, grid=(HP, S // tq),
        in_specs=[pl.BlockSpec((tq, LANES), lambda h, j: (j, h)), kv_spec, kv_spec,
                  pl.BlockSpec((2, CHUNK, BAND), lambda h, j: (h, 0, 0))],
        out_specs=pl.BlockSpec((tq, LANES), lambda h, j: (j, h)),
        out_shape=jax.ShapeDtypeStruct((S, D), BF16),
        compiler_params=_params(("parallel", "parallel")),
    )(q, kpad, vpad, bias)


def attn_bwd(q, do, kpad, vpad, bias, dk_in, dv_in, name):
    S, D = q.shape
    HP = D // LANES
    NH = 2 * HP
    cps = min(ATTN_CHUNKS_PER_STEP, S // CHUNK)
    tq = cps * CHUNK
    nj = S // tq
    scale = HEAD_DIM ** -0.5

    def body(q_ref, do_ref, k_ref, v_ref, b_ref, dki_ref, dvi_ref, dq_ref, dk_ref, dv_ref, db_ref, dk_acc, dv_acc):
        j = pl.program_id(1)

        @pl.when(j == 0)
        def _():
            dk_acc[:LEFT_PAD, :] = jnp.zeros((LEFT_PAD, LANES), F32)
            dv_acc[:LEFT_PAD, :] = jnp.zeros((LEFT_PAD, LANES), F32)
            dk_acc[LEFT_PAD:, :] = dki_ref[...]
            dv_acc[LEFT_PAD:, :] = dvi_ref[...]
            db_ref[...] = jnp.zeros_like(db_ref)

        sel0 = lax.broadcasted_iota(jnp.int32, (CHUNK, LANES), 1) < HEAD_DIM
        sels = (sel0, jnp.logical_not(sel0))
        chunks = _attn_load(j, cps, q_ref, k_ref, v_ref)
        pairs = [(cc, sub) for cc in range(cps) for sub in range(2)]
        qms, probs = _attn_probs(chunks, b_ref, sels, scale)
        doms = [[jnp.where(sel, do_ref[cc * CHUNK:(cc + 1) * CHUNK, :], jnp.zeros((CHUNK, LANES), BF16))
                 for sel in sels] for cc in range(cps)]
        dps = {(cc, sub): lax.dot_general(doms[cc][sub], chunks[cc][2], NT, preferred_element_type=F32)
               for cc, sub in pairs}
        dss = {}
        for cc, sub in pairs:
            p = probs[cc][sub]
            dss[cc, sub] = p * (dps[cc, sub] - jnp.sum(dps[cc, sub] * p, axis=-1, keepdims=True))
        dsb = {key: ds.astype(BF16) for key, ds in dss.items()}
        dqs = {(cc, sub): jnp.dot(dsb[cc, sub], chunks[cc][1], preferred_element_type=F32) * scale
               for cc, sub in pairs}
        dks = {(cc, sub): lax.dot_general(dsb[cc, sub], qms[cc][sub], TN, preferred_element_type=F32) * scale
               for cc, sub in pairs}
        dvs = {(cc, sub): lax.dot_general(probs[cc][sub].astype(BF16), doms[cc][sub], TN,
                                          preferred_element_type=F32) for cc, sub in pairs}
        dq_ref[...] = jnp.concatenate([jnp.where(sel0, dqs[cc, 0], dqs[cc, 1]) for cc in range(cps)],
                                      axis=0).astype(BF16)
        for sub in range(2):
            total = dss[0, sub]
            for cc in range(1, cps):
                total = total + dss[cc, sub]
            db_ref[sub] += total
        dk_parts = [dks[cc, 0] + dks[cc, 1] for cc in range(cps)]
        dv_parts = [dvs[cc, 0] + dvs[cc, 1] for cc in range(cps)]

        def window(parts):
            blocks = []
            for rb in range(cps - 1 + BAND // CHUNK):
                acc = None
                for cc in range(cps):
                    b = rb - cc
                    if 0 <= b < BAND // CHUNK:
                        piece = parts[cc][b * CHUNK:(b + 1) * CHUNK, :]
                        acc = piece if acc is None else acc + piece
                blocks.append(acc)
            return jnp.concatenate(blocks, axis=0)

        span = pl.ds(pl.multiple_of(j * cps * CHUNK, CHUNK), (cps - 1) * CHUNK + BAND)
        dk_acc[span, :] += window(dk_parts)
        dv_acc[span, :] += window(dv_parts)

        @pl.when(j == nj - 1)
        def _():
            dk_ref[...] = dk_acc[LEFT_PAD:, :]
            dv_ref[...] = dv_acc[LEFT_PAD:, :]

    q_spec = pl.BlockSpec((tq, LANES), lambda h, j: (j, h))
    kv_spec = pl.BlockSpec((S + LEFT_PAD, LANES), lambda h, j: (0, h))
    col_spec = pl.BlockSpec((S, LANES), lambda h, j: (0, h))
    b_spec = pl.BlockSpec((2, CHUNK, BAND), lambda h, j: (h, 0, 0))
    return pl.pallas_call(
        body, name=name, grid=(HP, nj),
        in_specs=[q_spec, q_spec, kv_spec, kv_spec, b_spec, col_spec, col_spec],
        out_specs=[q_spec, col_spec, col_spec, b_spec],
        out_shape=[jax.ShapeDtypeStruct((S, D), BF16), jax.ShapeDtypeStruct((S, D), F32),
                   jax.ShapeDtypeStruct((S, D), F32), jax.ShapeDtypeStruct((NH, CHUNK, BAND), F32)],
        scratch_shapes=[pltpu.VMEM((S + LEFT_PAD, LANES), F32), pltpu.VMEM((S + LEFT_PAD, LANES), F32)],
        compiler_params=_params(("parallel", "arbitrary")),
    )(q, do, kpad, vpad, bias, dk_in, dv_in)


def _rel_onehot(t):
    r = lax.broadcasted_iota(jnp.int32, (BAND, N_REL), 0)
    i = lax.broadcasted_iota(jnp.int32, (BAND, N_REL), 1)
    idx = jnp.clip(t + LEFT_PAD - r, -(CHUNK - 1), MAX_REL) + (CHUNK - 1)
    return (idx == i).astype(F32)


def bias_expand(rb, name):
    NH = rb.shape[0]

    def body(rb_ref, o_ref):
        def step(t, carry):
            o_ref[t] = lax.dot_general(rb_ref[...], _rel_onehot(t), NT, preferred_element_type=F32,
                                       precision=HIGHEST)
            return carry

        lax.fori_loop(0, CHUNK, step, 0)

    return pl.pallas_call(
        body, name=name, in_specs=[VMEM_SPEC], out_specs=VMEM_SPEC,
        out_shape=jax.ShapeDtypeStruct((CHUNK, NH, BAND), F32), compiler_params=_params(),
    )(rb)


def bias_grad(dsum, name):
    NH = dsum.shape[1]

    def body(d_ref, o_ref):
        def step(t, acc):
            return acc + jnp.dot(d_ref[t], _rel_onehot(t), preferred_element_type=F32, precision=HIGHEST)

        o_ref[...] = lax.fori_loop(0, CHUNK, step, jnp.zeros((NH, N_REL), F32))

    return pl.pallas_call(
        body, name=name, in_specs=[VMEM_SPEC], out_specs=VMEM_SPEC,
        out_shape=jax.ShapeDtypeStruct((NH, N_REL), F32), compiler_params=_params(),
    )(dsum)


def loss_grad(y, tgt, name):
    S, D = y.shape
    tm = min(MATMUL_ROW_TILE, S)

    def body(y_ref, t_ref, d_ref, acc_ref):
        @pl.when(pl.program_id(0) == 0)
        def _():
            acc_ref[...] = jnp.zeros_like(acc_ref)

        err = y_ref[...] - t_ref[...]
        d_ref[...] = err * (1.0 / D)
        acc_ref[0:1, :] += jnp.sum(err * err, axis=0, keepdims=True)

    tile = pl.BlockSpec((tm, D), _row(0))
    return pl.pallas_call(
        body, name=name, grid=(S // tm,), in_specs=[tile, tile],
        out_specs=[tile, pl.BlockSpec((8, D), _fixed2)],
        out_shape=[jax.ShapeDtypeStruct((S, D), F32), jax.ShapeDtypeStruct((8, D), F32)],
        compiler_params=_params(("arbitrary",)),
    )(y, tgt)


def _col_tile(n):
    for t in (768, 512, 256, 128):
        if n % t == 0:
            return t
    return n


def ada_fwd(c_all, w, b, name):
    L, D, n = w.shape
    tn = _col_tile(n)

    def body(c_ref, w_ref, b_ref, o_ref):
        cv = c_ref[...]
        ca = cv * _sigmoid(cv)
        o_ref[...] = jnp.dot(ca, w_ref[...], preferred_element_type=F32, precision=HIGHEST) + b_ref[...]

    return pl.pallas_call(
        body, name=name, grid=(L, n // tn),
        in_specs=[pl.BlockSpec((N_DEV, D), lambda l, j: (0, 0)), pl.BlockSpec((None, D, tn), lambda l, j: (l, 0, j)),
                  pl.BlockSpec((None, 1, tn), lambda l, j: (l, 0, j))],
        out_specs=pl.BlockSpec((None, N_DEV, tn), lambda l, j: (l, 0, j)),
        out_shape=jax.ShapeDtypeStruct((L, N_DEV, n), F32),
        compiler_params=_params(("parallel", "parallel")),
    )(c_all, w, b)


def ada_wgrad(c_all_t, dmod, name):
    L, _, n = dmod.shape
    D = c_all_t.shape[0]
    tn = _col_tile(n)

    def body(c_ref, d_ref, o_ref):
        cv = c_ref[...]
        ca = cv * _sigmoid(cv)
        o_ref[...] = jnp.dot(ca, d_ref[...], preferred_element_type=F32, precision=HIGHEST)

    return pl.pallas_call(
        body, name=name, grid=(L, n // tn),
        in_specs=[pl.BlockSpec((D, N_DEV), lambda l, j: (0, 0)), pl.BlockSpec((None, N_DEV, tn), lambda l, j: (l, 0, j))],
        out_specs=pl.BlockSpec((None, D, tn), lambda l, j: (l, 0, j)),
        out_shape=jax.ShapeDtypeStruct((L, D, n), F32),
        compiler_params=_params(("parallel", "parallel")),
    )(c_all_t, dmod)


ELEMENTWISE_BLOCK_BYTES = 3 * 1024 * 1024


def _elementwise_rows(rows, row_bytes):
    for t in (4096, 2048, 1024, 512, 256, 128, 64, 32, 16):
        if rows % t == 0 and t * row_bytes <= ELEMENTWISE_BLOCK_BYTES:
            return t
    return rows


def sum_leading(a, name):
    n, M, N = a.shape
    tr = _elementwise_rows(M, n * N * 4)

    def body(a_ref, o_ref):
        acc = a_ref[0]
        for i in range(1, n):
            acc = acc + a_ref[i]
        o_ref[...] = acc

    return pl.pallas_call(
        body, name=name, grid=(M // tr,),
        in_specs=[pl.BlockSpec((n, tr, N), lambda i: (0, i, 0))],
        out_specs=pl.BlockSpec((tr, N), _row(0)),
        out_shape=jax.ShapeDtypeStruct((M, N), F32),
        compiler_params=_params(("parallel",)),
    )(a)


def chip_sum(psum, land, chip_idx, name):
    n, M, N = psum.shape
    tr = _elementwise_rows(M, N * 8)

    def body(s_ref, p_ref, a_ref, b_ref, c_ref, o_ref):
        o_ref[...] = ((p_ref[...].astype(F32) + a_ref[...].astype(F32)) + b_ref[...].astype(F32)) + c_ref[...].astype(F32)

    def entry(j):
        return pl.BlockSpec((None, tr, N), lambda i, s: ((s[0] + j) % n, i, 0))

    return pl.pallas_call(
        body, name=name,
        grid_spec=pltpu.PrefetchScalarGridSpec(
            num_scalar_prefetch=1, grid=(M // tr,),
            in_specs=[entry(0), entry(1), entry(2), entry(3)],
            out_specs=pl.BlockSpec((tr, N), lambda i, s: (i, 0))),
        out_shape=jax.ShapeDtypeStruct((M, N), F32),
        compiler_params=_params(("parallel",)),
    )(chip_idx, psum, land, land, land)


def adamw(w, g, m, v, name):
    M, N = w.shape
    tr = _elementwise_rows(M, N * 4)
    c1 = 1.0 - ADAM_B1 ** ADAM_STEP
    c2 = 1.0 - ADAM_B2 ** ADAM_STEP

    def body(w_ref, g_ref, m_ref, v_ref, d_ref, nm_ref, nv_ref):
        g = g_ref[...]
        nm = ADAM_B1 * m_ref[...] + (1.0 - ADAM_B1) * g
        nv = ADAM_B2 * v_ref[...] + (1.0 - ADAM_B2) * (g * g)
        d_ref[...] = -ADAM_LR * ((nm / c1) / (jnp.sqrt(nv / c2) + ADAM_EPS) + ADAM_WD * w_ref[...])
        nm_ref[...] = nm
        nv_ref[...] = nv

    spec = pl.BlockSpec((tr, N), _row(0))
    shp = jax.ShapeDtypeStruct((M, N), F32)
    return pl.pallas_call(
        body, name=name, grid=(M // tr,), in_specs=[spec] * 4, out_specs=[spec] * 3, out_shape=[shp] * 3,
        compiler_params=_params(("parallel",)),
    )(w, g, m, v)


def _coords():
    return lax.axis_index("x"), lax.axis_index("y"), lax.axis_index("c")


def all_gather8(block, name):
    m_per, n = block.shape

    def body(x_ref, out_ref, send_sems, recv_sems, local_sem):
        x, y, c = _coords()
        me, sibling = (x, y, c), (x, y, 1 - c)
        chips = [(1 - x, y), (x, 1 - y), (1 - x, 1 - y)]

        def rows(px, py, pc):
            return out_ref.at[pl.ds((4 * px + 2 * py + pc) * m_per, m_per), :]

        def copy(k, blk, to, src=None):
            return pltpu.make_async_remote_copy(
                src_ref=rows(*blk) if src is None else src, dst_ref=rows(*blk),
                send_sem=send_sems.at[k], recv_sem=recv_sems.at[k], device_id=to, device_id_type=MESH)

        mine = pltpu.make_async_copy(x_ref, rows(*me), local_sem)
        mine.start()
        first = [copy(0, me, sibling, src=x_ref)]
        first += [copy(1 + j, me, (*chip, c), src=x_ref) for j, chip in enumerate(chips)]
        for cp in first:
            cp.start()
        passed = [copy(4 + j, (*chip, c), sibling) for j, chip in enumerate(chips)]
        for j, chip in enumerate(chips):
            copy(1 + j, (*chip, c), me).wait_recv()
            passed[j].start()
        copy(0, sibling, me).wait_recv()
        for j, chip in enumerate(chips):
            copy(4 + j, (*chip, 1 - c), me).wait_recv()
        for cp in first + passed:
            cp.wait_send()
        mine.wait()

    return pl.pallas_call(
        body, name=name, in_specs=[VMEM_SPEC], out_specs=VMEM_SPEC,
        out_shape=jax.ShapeDtypeStruct((N_DEV * m_per, n), block.dtype),
        scratch_shapes=[pltpu.SemaphoreType.DMA((7,)), pltpu.SemaphoreType.DMA((7,)), pltpu.SemaphoreType.DMA],
        compiler_params=_params(),
    )(block)


def _other_chips(x, y):
    return [(1 - x, y), (x, 1 - y), (1 - x, 1 - y)]


HBM_SPEC = pl.BlockSpec(memory_space=pltpu.HBM)
SEM_SPEC = pl.BlockSpec(memory_space=pltpu.SEMAPHORE)
DATAFLOW = pltpu.SideEffectType.DATAFLOW_SIDE_EFFECTING


def _chip_peers(x, y, c):
    return [(px, py, c) for px, py in _other_chips(x, y)]


def _sibling_peer(x, y, c):
    return [(x, y, 1 - c)]


def _weight_desc(stack_ref, k, land_ref, peer, me):
    h = land_ref.shape[1] // 2
    rows = pl.ds(me[2] * h, h)
    return (stack_ref.at[k, rows, :], land_ref.at[2 * me[0] + me[1], rows, :],
            land_ref.at[2 * peer[0] + peer[1], rows, :])


def _grad_desc(psum_ref, k, land_ref, peer, me):
    return psum_ref.at[2 * peer[0] + peer[1]], land_ref.at[2 * me[0] + me[1]], land_ref.at[2 * peer[0] + peer[1]]


def _pair_desc(grad_ref, k, land_ref, peer, me):
    h = land_ref.shape[1]
    return grad_ref.at[:, pl.ds(peer[2] * h, h), :], land_ref, land_ref


def _whole_desc(src_ref, k, land_ref, peer, me):
    return src_ref, land_ref, land_ref


def exchange_start(srcs, lands, units, groups, desc, peers, after, name):
    n_s, n_l, n_g = len(srcs), len(lands), len(groups)
    n_p = len(peers(0, 0, 0))

    def body(*refs):
        s_refs, l_refs = refs[:n_s], refs[n_s:n_s + n_l]
        outs = refs[n_s + n_l + 1:]
        sems, token = outs[:2 * n_g], outs[-1]
        me = _coords()
        for g, ids in enumerate(groups):
            for i, u in enumerate(ids):
                si, k = units[u]
                for j, peer in enumerate(peers(*me)):
                    src, dst, _ = desc(s_refs[si], k, l_refs[u], peer, me)
                    pltpu.make_async_remote_copy(
                        src_ref=src, dst_ref=dst, send_sem=sems[2 * g].at[n_p * i + j],
                        recv_sem=sems[2 * g + 1].at[n_p * i + j], device_id=peer, device_id_type=MESH).start()
        token[...] = jnp.zeros_like(token)

    arrs = list(srcs) + list(lands)
    sem_shapes = [pltpu.SemaphoreType.DMA((n_p * len(ids),)) for ids in groups for _ in range(2)]
    outs = pl.pallas_call(
        body, name=name,
        in_specs=[HBM_SPEC] * len(arrs) + [ANY],
        out_specs=[SEM_SPEC] * (2 * n_g) + [HBM_SPEC] * len(arrs) + [VMEM_SPEC],
        out_shape=sem_shapes + [pltpu.HBM(a.shape, a.dtype) for a in arrs] + [jax.ShapeDtypeStruct((8, LANES), F32)],
        input_output_aliases={i: 2 * n_g + i for i in range(len(arrs))},
        compiler_params=pltpu.CompilerParams(has_side_effects=DATAFLOW),
    )(*[pltpu.with_memory_space_constraint(a, pltpu.HBM) for a in arrs], after)
    sems = outs[:2 * n_g]
    thru = outs[2 * n_g:2 * n_g + len(arrs)]
    return sems, list(thru[:n_s]), list(thru[n_s:]), outs[-1]


def exchange_wait(srcs, lands, units, send_sem, recv_sem, desc, peers, after, name):
    n_s, n_l = len(srcs), len(lands)
    n_p = len(peers(0, 0, 0))

    def body(*refs):
        s_refs, l_refs = refs[:n_s], refs[n_s:n_s + n_l]
        send_sems, recv_sems = refs[n_s + n_l], refs[n_s + n_l + 1]
        me = _coords()
        for i, (si, k) in enumerate(units):
            for j, peer in enumerate(peers(*me)):
                src, _, mine = desc(s_refs[si], k, l_refs[i], peer, me)
                cp = pltpu.make_async_remote_copy(
                    src_ref=src, dst_ref=mine, send_sem=send_sems.at[n_p * i + j], recv_sem=recv_sems.at[n_p * i + j],
                    device_id=peer, device_id_type=MESH)
                cp.wait_send()
                cp.wait_recv()

    arrs = list(srcs) + list(lands)
    outs = pl.pallas_call(
        body, name=name,
        in_specs=[HBM_SPEC] * len(arrs) + [SEM_SPEC, SEM_SPEC, ANY],
        out_specs=[HBM_SPEC] * len(arrs),
        out_shape=[pltpu.HBM(a.shape, a.dtype) for a in arrs],
        input_output_aliases={i: i for i in range(len(arrs))},
        compiler_params=pltpu.CompilerParams(has_side_effects=DATAFLOW),
    )(*arrs, send_sem, recv_sem, after)
    return list(outs[:n_s]), list(outs[n_s:])


def sibling_fill(lands, name):
    n_u = len(lands)

    def body(*refs):
        ins, outs = refs[:n_u], refs[n_u:2 * n_u]
        send_sems, recv_sems = refs[2 * n_u:]
        x, y, c = _coords()
        sends = []
        for u in range(n_u):
            h = ins[u].shape[1] // 2
            for j, (px, py) in enumerate(_other_chips(x, y)):
                part = (2 * px + py, pl.ds(c * h, h), slice(None))
                cp = pltpu.make_async_remote_copy(
                    src_ref=ins[u].at[part], dst_ref=outs[u].at[part], send_sem=send_sems.at[3 * u + j],
                    recv_sem=recv_sems.at[3 * u + j], device_id=(x, y, 1 - c), device_id_type=MESH)
                cp.start()
                sends.append(cp)
        for u in range(n_u):
            h = ins[u].shape[1] // 2
            for j, (px, py) in enumerate(_other_chips(x, y)):
                theirs = (2 * px + py, pl.ds((1 - c) * h, h), slice(None))
                pltpu.make_async_remote_copy(
                    src_ref=ins[u].at[theirs], dst_ref=outs[u].at[theirs], send_sem=send_sems.at[3 * u + j],
                    recv_sem=recv_sems.at[3 * u + j], device_id=(x, y, 1 - c), device_id_type=MESH).wait_recv()
        for cp in sends:
            cp.wait_send()

    return pl.pallas_call(
        body, name=name, in_specs=[ANY] * n_u, out_specs=[ANY] * n_u,
        out_shape=[jax.ShapeDtypeStruct(a.shape, a.dtype) for a in lands],
        input_output_aliases={i: i for i in range(n_u)},
        scratch_shapes=[pltpu.SemaphoreType.DMA((3 * n_u,)), pltpu.SemaphoreType.DMA((3 * n_u,))],
        compiler_params=_params(),
    )(*lands)


def _pack_rows(parts):
    flat = jnp.concatenate([p.reshape(-1).astype(F32) for p in parts])
    n = flat.shape[0]
    padded = -(-n // (8 * LANES)) * (8 * LANES)
    return jnp.pad(flat, (0, padded - n)).reshape(-1, LANES)


def _unpack_rows(packed, shapes):
    flat = packed.reshape(-1)
    out, off = [], 0
    for s in shapes:
        size = 1
        for d in s:
            size *= d
        out.append(flat[off:off + size].reshape(s))
        off += size
    return out


def _shard_last(full, s_me):
    n = full.shape[-1] // N_CHIP
    return lax.dynamic_slice_in_dim(full, s_me * n, n, axis=full.ndim - 1)


def _unshard_last(g):
    moved = jnp.moveaxis(g, 0, -2)
    return moved.reshape(moved.shape[:-2] + (moved.shape[-2] * moved.shape[-1],))


def kernel(x, c, w_ada, b_ada, ln_g, ln_b, ffn_gu, ffn_down, gmlp_w_in, gmlp_b_in, gmlp_ln_g, gmlp_ln_b, gmlp_w_s, gmlp_b_s, gmlp_w_out, w_ada_kv, b_ada_kv, w_kv, attn_w_q, attn_rel_bias, attn_w_o, loss_target, m_w_ada, m_b_ada, m_ln_g, m_ln_b, m_ffn_gu, m_ffn_down, m_gmlp_w_in, m_gmlp_b_in, m_gmlp_ln_g, m_gmlp_ln_b, m_gmlp_w_s, m_gmlp_b_s, m_gmlp_w_out, m_w_ada_kv, m_b_ada_kv, m_w_kv, m_attn_w_q, m_attn_rel_bias, m_attn_w_o, v_w_ada, v_b_ada, v_ln_g, v_ln_b, v_ffn_gu, v_ffn_down, v_gmlp_w_in, v_gmlp_b_in, v_gmlp_ln_g, v_gmlp_ln_b, v_gmlp_w_s, v_gmlp_b_s, v_gmlp_w_out, v_w_ada_kv, v_b_ada_kv, v_w_kv, v_attn_w_q, v_attn_rel_bias, v_attn_w_o):
    xi, yi, ci = _coords()
    s_me = 2 * xi + yi
    dev = 4 * xi + 2 * yi + ci

    x0 = x[0]
    tgt = loss_target[0]
    S, D = x0.shape
    L = w_ada.shape[0]
    NA = gmlp_w_in.shape[0]
    NB = attn_w_q.shape[0]
    NH = D // HEAD_DIM
    alpha = (2.0 * L) ** 0.25
    n_ada = w_ada.shape[2]
    n_kv = w_ada_kv.shape[1]

    stack_names = ["ffn_gu", "ffn_down", "gmlp_w_in", "gmlp_w_out", "w_kv", "attn_w_q", "attn_w_o"]
    stack_src = dict(ffn_gu=ffn_gu, ffn_down=ffn_down, gmlp_w_in=gmlp_w_in, gmlp_w_out=gmlp_w_out, w_kv=w_kv[None],
                     attn_w_q=attn_w_q, attn_w_o=attn_w_o)
    stacks = [stack_src[nm].astype(BF16).reshape((-1,) + stack_src[nm].shape[-2:]) for nm in stack_names]
    units = [(si, k) for si, st in enumerate(stacks) for k in range(st.shape[0])]
    unit_of = {(stack_names[si], k): u for u, (si, k) in enumerate(units)}
    weight_groups = [[("ffn_gu", 0), ("ffn_down", 0)],
                     ([("gmlp_w_in", 0), ("gmlp_w_out", 0)] if NA > 0 else [("attn_w_q", 0), ("attn_w_o", 0)])
                     + [("ffn_gu", 1), ("ffn_down", 1)] + ([("w_kv", 0)] if NA == 0 else [])]
    for l in range(1, L):
        names = [("w_kv", 0)] if l == NA else []
        names += [("ffn_gu", 2 * l), ("ffn_down", 2 * l)]
        names += [("gmlp_w_in", l), ("gmlp_w_out", l)] if l < NA else [("attn_w_q", l - NA), ("attn_w_o", l - NA)]
        names += [("ffn_gu", 2 * l + 1), ("ffn_down", 2 * l + 1)]
        weight_groups.append(names)
    weight_groups = [[unit_of[n] for n in names] for names in weight_groups]

    c_all = all_gather8(jnp.broadcast_to(c, (8, D)), "ag_c").reshape(N_DEV, 8, D)[:, 0]
    b_ada_sh = lax.dynamic_slice_in_dim(b_ada, s_me * n_ada, n_ada, axis=1)
    b_kv_sh = lax.dynamic_slice_in_dim(b_ada_kv, s_me * n_kv, n_kv, axis=0)
    mod_part = ada_fwd(c_all, w_ada, b_ada_sh[:, None, :], "ada_fwd")
    mkv_part = ada_fwd(c_all, w_ada_kv[None], b_kv_sh[None, None, :], "ada_kv_fwd")
    part = jnp.concatenate([jnp.transpose(mod_part, (1, 0, 2)).reshape(N_DEV, L * n_ada), mkv_part[0]], axis=1)
    width = part.shape[1]
    pad_w = -(-width // LANES) * LANES - width
    all_part = all_gather8(jnp.pad(part, ((0, 0), (0, pad_w))), "ag_mod").reshape(N_DEV, N_DEV, width + pad_w)
    mine = lax.dynamic_index_in_dim(all_part[0::2], dev, axis=1, keepdims=False)
    mod = jnp.transpose(mine[:, :L * n_ada].reshape(N_CHIP, L, n_ada), (1, 0, 2)).reshape(L, N_MOD, D)
    mkv = mine[:, L * n_ada:width].reshape(2, D)

    def mrow(l, k):
        return mod[l, k][None, :]

    small_shapes = [ln_g.shape, ln_b.shape, gmlp_b_in.shape, gmlp_ln_g.shape, gmlp_ln_b.shape, attn_rel_bias.shape]
    small_pack = _pack_rows([ln_g, ln_b, gmlp_b_in, gmlp_ln_g, gmlp_ln_b, attn_rel_bias])
    small_all = all_gather8(small_pack, "ag_small_params").reshape((N_DEV,) + small_pack.shape)[0::2]
    sm = [_unpack_rows(small_all[s], small_shapes) for s in range(N_CHIP)]
    ln_g_f, ln_b_f, b_in_f, gln_g_f, gln_b_f, rel_f = [
        _unshard_last(jnp.stack([sm[s][i] for s in range(N_CHIP)])) for i in range(len(small_shapes))]

    lands0 = [lax.dynamic_update_slice(lax.empty((N_CHIP,) + stacks[si].shape[1:], BF16), stacks[si][k][None],
                                       (s_me, 0, 0)) for si, k in units]
    gathers_done = jnp.concatenate([mod.reshape(-1)[:LANES], small_all.reshape(-1)[:LANES]])
    w_sems, stacks_t, lands_t, _ = exchange_start(stacks, lands0, units, weight_groups, _weight_desc, _chip_peers, gathers_done,
                                                  "weight_send_start")
    wg = {}

    def fetch_weights(g, stacks_now, after):
        ids = weight_groups[g]
        stacks_next, got = exchange_wait(stacks_now, [lands_t[u] for u in ids], [units[u] for u in ids],
                                         w_sems[2 * g], w_sems[2 * g + 1], _weight_desc, _chip_peers, after,
                                         "weight_send_wait_%d" % g)
        for u, a in zip(ids, sibling_fill(got, "weight_sibling_fill")):
            wg[u] = a
        return stacks_next

    def W(nm, k):
        return wg[unit_of[(nm, k)]]

    def Wrows(nm, k):
        w4 = W(nm, k)
        return w4.reshape(w4.shape[0] * w4.shape[1], w4.shape[2])

    bst = [jnp.transpose(gmlp_b_s[j]) for j in range(NA)]
    biases = {}

    def make_bias(j, dep):
        rel, _ = lax.optimization_barrier((rel_f[j], dep))
        biases[j] = jnp.transpose(bias_expand(rel, "bias_expand"), (1, 0, 2))
        return biases[j]

    saved = []
    xc = x0
    kpad = vpad = xkv = None
    for l in range(L):
        after = mod if l == 0 else xc
        if l == 1 and NB > 1:
            after = make_bias(1, xc)
        stacks_t = fetch_weights(0 if l == 0 else l + 1, stacks_t, after)
        if l == NA:
            xkv = xc
            kv, hkv = mod_matmul(xc, mkv[1][None], mkv[0][None], W("w_kv", 0), None, BF16, "kv_proj")
            kpad = jnp.pad(kv[:, :D], ((LEFT_PAD, 0), (0, 0)))
            vpad = jnp.pad(kv[:, D:], ((LEFT_PAD, 0), (0, 0)))
        sv = {}
        for i in (0, 2):
            k = 2 * l + i // 2
            gu, hv = mod_matmul(xc, mrow(l, 3 * i + 1), mrow(l, 3 * i), W("ffn_gu", k), None, BF16, "ffn_up")
            gw = 0.5 * (1.0 + mrow(l, 3 * i + 2))
            xn, xh, rs, yv, av = matmul_res_ln(gu, Wrows("ffn_down", k), xc, gw, ln_g_f[l, i][None],
                                               ln_b_f[l, i][None], alpha, True, "ffn_down")
            sv[i] = dict(x=xc, h=hv, gu=gu, a=av, xh=xh, rs=rs, y=yv, gw=gw)
            xc = xn
            if i == 0:
                if l == 0:
                    stacks_t = fetch_weights(1, stacks_t, make_bias(0, xc) if NB > 0 else xc)
                gw = 1.0 + mrow(l, 5)
                if l < NA:
                    pre, hv = mod_matmul(xc, mrow(l, 4), mrow(l, 3), W("gmlp_w_in", l), b_in_f[l][None], F32,
                                         "gmlp_in")
                    qv = sgu_fwd(pre, gln_g_f[l][None], gln_b_f[l][None], gmlp_w_s[l], bst[l], "sgu_fwd")
                    xn, xh, rs, yv = matmul_res_ln(qv, Wrows("gmlp_w_out", l), xc, gw, ln_g_f[l, 1][None],
                                                   ln_b_f[l, 1][None], alpha, False, "gmlp_out")
                    sv[1] = dict(x=xc, h=hv, pre=pre, a=qv, xh=xh, rs=rs, y=yv, gw=gw)
                else:
                    j = l - NA
                    if j not in biases:
                        make_bias(j, xc)
                    qh, hv = mod_matmul(xc, mrow(l, 4), mrow(l, 3), Wrows("attn_w_q", j)[None], None, BF16, "attn_q")
                    ov = attn_fwd(qh, kpad, vpad, biases[j], "attn_fwd")
                    xn, xh, rs, yv = matmul_res_ln(ov, Wrows("attn_w_o", j), xc, gw, ln_g_f[l, 1][None],
                                                   ln_b_f[l, 1][None], alpha, False, "attn_out")
                    sv[1] = dict(x=xc, h=hv, q=qh, a=ov, xh=xh, rs=rs, y=yv, gw=gw)
                xc = xn
        saved.append(sv)

    dx, lacc = loss_grad(xc, tgt, "loss_grad")
    loss = lax.psum((0.5 / D) * jnp.sum(lacc[0]), ("x", "y", "c"))

    gpair = [None] * len(units)
    col_split = {u for u, (si, _) in enumerate(units) if stack_names[si] != "ffn_gu"}
    dmod = [[None] * N_MOD for _ in range(L)]
    d_ln_g = [[None] * 3 for _ in range(L)]
    d_ln_b = [[None] * 3 for _ in range(L)]
    d_b_in, d_gln_g, d_gln_b, d_ws, d_bs, d_rel = ([None] * NA, [None] * NA, [None] * NA, [None] * NA, [None] * NA,
                                                  [None] * NB)
    dk = jnp.zeros((S, D), F32)
    dv = jnp.zeros((S, D), F32)
    dmkv = None

    made = []

    core_idx = ci.astype(jnp.int32).reshape(1)
    chip_idx = s_me.astype(jnp.int32).reshape(1)

    def put(nm, k, a, b, name):
        u = unit_of[(nm, k)]
        rows, cols = stacks[units[u][0]].shape[1:]
        if nm == "ffn_gu":
            g = wgrad_pair(a, b, N_CHIP, rows // 2, cols, lambda j, p: p, lambda j, p: j, core_idx, name)
        elif nm in ("gmlp_w_in", "w_kv"):
            g = wgrad_pair(a, b, N_CHIP, rows, cols // 2, lambda j, p: 0, lambda j, p: 2 * j + p, core_idx, name)
        else:
            g = wgrad_pair(a, b, 1, N_CHIP * rows, cols // 2, lambda j, p: 0, lambda j, p: p, core_idx, name)
        gpair[u] = g.reshape((N_CHIP, -1, g.shape[-1]))
        made.append(u)

    own_half, sib_half = {}, {}
    n_started = [0]

    def start_grad_exchange(ids, after):
        psums = [gpair[u] for u in ids]
        n = len(ids)
        tag = n_started[0]
        n_started[0] += 1
        sems, ps_t, q_t, token = exchange_start(psums, [lax.empty(p.shape, p.dtype) for p in psums],
                                                [(i, 0) for i in range(n)], [list(range(n))], _grad_desc, _chip_peers,
                                                after, "grad_send_start_%d" % tag)
        return dict(ids=ids, tag=tag, sems=sems, ps=ps_t, q=q_t), token

    def finish_grad_exchange(pend, after):
        n = len(pend["ids"])
        ps_t, q = exchange_wait(pend["ps"], pend["q"], [(i, 0) for i in range(n)], pend["sems"][0], pend["sems"][1],
                                _grad_desc, _chip_peers, after, "grad_send_wait_%d" % pend["tag"])
        halves = [chip_sum(ps_t[i], q[i], chip_idx, "grad_chip_sum") for i in range(n)]
        sems, h_t, land_t, token = exchange_start(halves, [lax.empty(h.shape, h.dtype) for h in halves],
                                              [(i, 0) for i in range(n)], [list(range(n))], _whole_desc,
                                              _sibling_peer, halves[0], "half_send_start_%d" % pend["tag"])
        swaps.append(dict(ids=pend["ids"], tag=pend["tag"], sems=sems, h=h_t, land=land_t))
        return token

    def collect_halves(after):
        for sw in swaps:
            n = len(sw["ids"])
            h, land = exchange_wait(sw["h"], sw["land"], [(i, 0) for i in range(n)], sw["sems"][0], sw["sems"][1],
                                    _whole_desc, _sibling_peer, after, "half_send_wait_%d" % sw["tag"])
            for u, mine, theirs in zip(sw["ids"], h, land):
                own_half[u], sib_half[u] = mine, theirs
        swaps.clear()

    swaps = []
    pending = None
    started_before = jnp.zeros((8, LANES), F32)

    def ln_inputs(l, i):
        t = saved[l][i]
        return (t["xh"], t["rs"], ln_g_f[l, i][None], t["y"], t["gw"], 1.0 if i == 1 else 0.5)

    def record_ln(l, i, acc, row0):
        d_ln_g[l][i], d_ln_b[l][i], dmod[l][3 * i + 2] = acc[row0], acc[row0 + 1], acc[row0 + 2]

    ln_done = None
    for l in reversed(range(L)):
        if l == NA - 1:
            dkv = jnp.concatenate([dk, dv], axis=1)
            put("w_kv", 0, hkv, dkv, "kv_wgrad")
            pdxa, pdy, acc = dgrad_mod(dkv, W("w_kv", 0), dx, xkv, mkv[1][None], ln_inputs(l, 2), alpha, "kv_dgrad")
            dmkv = jnp.stack([acc[1], acc[0]])
            record_ln(l, 2, acc, 2)
            ln_done = (pdxa, pdy)
        sv = saved[l]
        for i in (2, 1, 0):
            t = sv[i]
            if ln_done is None:
                dxa, dy, acc1 = ln_res_bwd(dx, *ln_inputs(l, i), alpha, "ln_res_bwd")
                record_ln(l, i, acc1, 0)
            else:
                dxa, dy = ln_done
                ln_done = None
            before = (l, i - 1) if i > 0 else ((l - 1, 2) if l > 0 and l != NA else None)
            prev = ln_inputs(*before) if before is not None else None
            scl = mrow(l, 3 * i + 1)
            if i != 1:
                k = 2 * l + i // 2
                F = t["gu"].shape[1] // 2
                dgu = ffn_act_bwd(dy, Wrows("ffn_down", k), t["gu"], started_before, "ffn_act_bwd")
                put("ffn_down", k, t["a"], dy, "ffn_down_wgrad")
                put("ffn_gu", k, t["h"], dgu, "ffn_up_wgrad")
                res = dgrad_mod(dgu, W("ffn_gu", k), dxa, t["x"], scl, prev, alpha, "ffn_up_dgrad")
            elif l < NA:
                dq = matmul_nt(dy, Wrows("gmlp_w_out", l), started_before, "gmlp_out_dgrad")
                put("gmlp_w_out", l, t["a"], dy, "gmlp_out_wgrad")
                dpre, dws_l, dss, dgl, dbin = sgu_bwd(dq, t["pre"], gln_g_f[l][None], gln_b_f[l][None], gmlp_w_s[l],
                                                      bst[l], "sgu_bwd")
                d_ws[l] = dws_l
                d_bs[l] = jnp.transpose(group_lane_sum(dss, "sgu_bias_grad")[:, :GMLP_GROUPS])
                d_gln_g[l], d_gln_b[l], d_b_in[l] = dgl[0], dgl[1], dbin[0]
                put("gmlp_w_in", l, t["h"], dpre, "gmlp_in_wgrad")
                res = dgrad_mod(dpre, W("gmlp_w_in", l), dxa, t["x"], scl, prev, alpha, "gmlp_in_dgrad")
            else:
                j = l - NA
                do = matmul_nt(dy, Wrows("attn_w_o", j), started_before, "attn_out_dgrad")
                put("attn_w_o", j, t["a"], dy, "attn_out_wgrad")
                dqh, dk, dv, dbias = attn_bwd(t["q"], do, kpad, vpad, biases[j], dk, dv, "attn_bwd")
                d_rel[j] = bias_grad(jnp.transpose(dbias, (1, 0, 2)), "bias_grad")
                put("attn_w_q", j, t["h"], dqh, "attn_q_wgrad")
                res = dgrad_mod(dqh, Wrows("attn_w_q", j)[None], dxa, t["x"], scl, prev, alpha, "attn_q_dgrad")
            acc2 = res[-1]
            dmod[l][3 * i + 1], dmod[l][3 * i] = acc2[0], acc2[1]
            if before is None:
                dx = res[0]
            else:
                record_ln(*before, acc2, 2)
                ln_done = (res[0], res[1])
            if (i == 0 and l > 0) or (i == 1 and l == 0):
                started, started_before = start_grad_exchange(list(made), acc2)
                made.clear()
                if pending is not None:
                    started_before = started_before + finish_grad_exchange(pending, acc2)
                pending = started
    grad_x = dx[None]

    dvec = _pack_rows([jnp.stack([jnp.stack(r) for r in dmod]), dmkv])
    dvec = lax.optimization_barrier((dvec, [gpair[u] for u in made]))[0]
    n_dvec = L * N_MOD * D + 2 * D
    dall = all_gather8(dvec, "ag_dmod").reshape(N_DEV, -1, LANES)
    db_all = sum_leading(dall, "ada_bias_grad").reshape(-1)[:n_dvec]
    g_b_ada = db_all[:L * N_MOD * D].reshape(L, N_MOD * D)
    g_b_ada_kv = db_all[L * N_MOD * D:]
    dall2 = dall.reshape(N_DEV, -1)[:, :n_dvec]
    dmod_all = dall2[:, :L * N_MOD * D].reshape(N_DEV, L, N_MOD * D)
    dmod_sh = jnp.transpose(lax.dynamic_slice_in_dim(dmod_all, s_me * n_ada, n_ada, axis=2), (1, 0, 2))
    dmkv_sh = lax.dynamic_slice_in_dim(dall2[:, L * N_MOD * D:], s_me * n_kv, n_kv, axis=1)[None]
    c_all_t = jnp.transpose(c_all)
    g_w_ada = ada_wgrad(c_all_t, dmod_sh, "ada_wgrad")
    g_w_ada_kv = ada_wgrad(c_all_t, dmkv_sh, "ada_kv_wgrad")[0]

    small_g = [jnp.stack([jnp.stack(r) for r in d_ln_g]), jnp.stack([jnp.stack(r) for r in d_ln_b]),
               jnp.stack(d_b_in), jnp.stack(d_gln_g), jnp.stack(d_gln_b), jnp.stack(d_rel), jnp.stack(d_ws),
               jnp.stack(d_bs)]
    sg_shapes = [a.shape for a in small_g]
    sg_pack = _pack_rows(small_g)
    sg_all = all_gather8(sg_pack, "ag_small_grads").reshape(N_DEV, -1, LANES)
    sg_sum = _unpack_rows(sum_leading(sg_all, "small_grad_sum"), sg_shapes)
    g_ln_g, g_ln_b, g_b_in, g_gln_g, g_gln_b, g_rel = [_shard_last(a, s_me) for a in sg_sum[:6]]
    g_ws, g_bs = sg_sum[6], sg_sum[7]

    last, _ = start_grad_exchange(list(made), sg_all)

    grads = dict(w_ada=g_w_ada, b_ada=g_b_ada, ln_g=g_ln_g, ln_b=g_ln_b, gmlp_b_in=g_b_in, gmlp_ln_g=g_gln_g,
                 gmlp_ln_b=g_gln_b, gmlp_w_s=g_ws, gmlp_b_s=g_bs, w_ada_kv=g_w_ada_kv, b_ada_kv=g_b_ada_kv,
                 attn_rel_bias=g_rel)
    weights = dict(w_ada=w_ada, b_ada=b_ada, ln_g=ln_g, ln_b=ln_b, ffn_gu=ffn_gu, ffn_down=ffn_down,
                   gmlp_w_in=gmlp_w_in, gmlp_b_in=gmlp_b_in, gmlp_ln_g=gmlp_ln_g, gmlp_ln_b=gmlp_ln_b,
                   gmlp_w_s=gmlp_w_s, gmlp_b_s=gmlp_b_s, gmlp_w_out=gmlp_w_out, w_ada_kv=w_ada_kv,
                   b_ada_kv=b_ada_kv, w_kv=w_kv, attn_w_q=attn_w_q, attn_rel_bias=attn_rel_bias, attn_w_o=attn_w_o)
    ms = dict(w_ada=m_w_ada, b_ada=m_b_ada, ln_g=m_ln_g, ln_b=m_ln_b, ffn_gu=m_ffn_gu, ffn_down=m_ffn_down,
              gmlp_w_in=m_gmlp_w_in, gmlp_b_in=m_gmlp_b_in, gmlp_ln_g=m_gmlp_ln_g, gmlp_ln_b=m_gmlp_ln_b,
              gmlp_w_s=m_gmlp_w_s, gmlp_b_s=m_gmlp_b_s, gmlp_w_out=m_gmlp_w_out, w_ada_kv=m_w_ada_kv,
              b_ada_kv=m_b_ada_kv, w_kv=m_w_kv, attn_w_q=m_attn_w_q, attn_rel_bias=m_attn_rel_bias,
              attn_w_o=m_attn_w_o)
    vs = dict(w_ada=v_w_ada, b_ada=v_b_ada, ln_g=v_ln_g, ln_b=v_ln_b, ffn_gu=v_ffn_gu, ffn_down=v_ffn_down,
              gmlp_w_in=v_gmlp_w_in, gmlp_b_in=v_gmlp_b_in, gmlp_ln_g=v_gmlp_ln_g, gmlp_ln_b=v_gmlp_ln_b,
              gmlp_w_s=v_gmlp_w_s, gmlp_b_s=v_gmlp_b_s, gmlp_w_out=v_gmlp_w_out, w_ada_kv=v_w_ada_kv,
              b_ada_kv=v_b_ada_kv, w_kv=v_w_kv, attn_w_q=v_attn_w_q, attn_rel_bias=v_attn_rel_bias,
              attn_w_o=v_attn_w_o)
    order = ["w_ada", "b_ada", "ln_g", "ln_b", "ffn_gu", "ffn_down", "gmlp_w_in", "gmlp_b_in", "gmlp_ln_g",
             "gmlp_ln_b", "gmlp_w_s", "gmlp_b_s", "gmlp_w_out", "w_ada_kv", "b_ada_kv", "w_kv", "attn_w_q",
             "attn_rel_bias", "attn_w_o"]
    big_names = ["w_ada", "w_ada_kv"] + stack_names
    small_names = [nm for nm in order if nm not in big_names]
    delta, new_m, new_v = {}, {}, {}

    def adamw_big(nm):
        shp = weights[nm].shape
        two_d = (-1, shp[-1])
        d, a, b = adamw(weights[nm].reshape(two_d), grads[nm].reshape(two_d), ms[nm].reshape(two_d),
                        vs[nm].reshape(two_d), "adamw")
        delta[nm], new_m[nm], new_v[nm] = d.reshape(shp), a.reshape(shp), b.reshape(shp)

    adamw_big("w_ada")
    adamw_big("w_ada_kv")
    shapes = [weights[nm].shape for nm in small_names]
    d, a, b = adamw(_pack_rows([weights[nm] for nm in small_names]), _pack_rows([grads[nm] for nm in small_names]),
                    _pack_rows([ms[nm] for nm in small_names]), _pack_rows([vs[nm] for nm in small_names]),
                    "adamw_small")
    for nm, dd, aa, bb in zip(small_names, _unpack_rows(d, shapes), _unpack_rows(a, shapes), _unpack_rows(b, shapes)):
        delta[nm], new_m[nm], new_v[nm] = dd, aa, bb

    def full_grad(u):
        lo = jnp.where(ci == 0, own_half[u], sib_half[u])
        hi = jnp.where(ci == 0, sib_half[u], own_half[u])
        return jnp.concatenate([lo, hi], axis=1 if u in col_split else 0)

    def adamw_stack(nm):
        si = stack_names.index(nm)
        g = jnp.stack([full_grad(unit_of[(nm, k)]) for k in range(stacks[si].shape[0])])
        grads[nm] = g.reshape(weights[nm].shape)
        adamw_big(nm)

    late = [stack_names[units[u][0]] for u in last["ids"]]
    early = [nm for nm in stack_names if nm not in late]
    finish_grad_exchange(pending, delta["w_ada"])
    collect_halves(delta["w_ada"])
    for nm in early:
        adamw_stack(nm)
    finish_grad_exchange(last, delta[early[-1]])
    collect_halves(delta[early[-1]])
    for nm in stack_names:
        if nm in late:
            adamw_stack(nm)

    return (loss, grad_x, *[grads[nm] for nm in order], *[delta[nm] for nm in order],
            *[new_m[nm] for nm in order], *[new_v[nm] for nm in order])
```

```python
import functools

import jax
import jax.numpy as jnp
from jax import lax
from jax.experimental import pallas as pl
from jax.experimental.pallas import tpu as pltpu

F32 = jnp.float32
BF16 = jnp.bfloat16
MESH = pl.DeviceIdType.MESH
HIGHEST = lax.Precision.HIGHEST

CHUNK = 64
GMLP_WINDOW = 128
GMLP_GROUPS = 8
HEAD_DIM = 64
LEFT_CHUNKS = 8
BAND = (LEFT_CHUNKS + 1) * CHUNK
LEFT_PAD = LEFT_CHUNKS * CHUNK
MAX_REL = 4 * CHUNK
N_REL = (CHUNK - 1) + MAX_REL + 1
LN_EPS = 1e-5
N_MOD = 9
N_DEV = 8
N_CHIP = 4

ADAM_LR = 0.001
ADAM_B1 = 0.9
ADAM_B2 = 0.999
ADAM_EPS = 1e-08
ADAM_WD = 0.01
ADAM_STEP = 10

LANES = 128
ROW_TILE = 256
MATMUL_ROW_TILE = 512
WGRAD_ROWS = 1024
ATTN_CHUNKS_PER_STEP = 4
VMEM_LIMIT_MB = 56

NT = (((1,), (1,)), ((), ()))
TN = (((0,), (0,)), ((), ()))

ANY = pl.BlockSpec(memory_space=pl.ANY)
VMEM_SPEC = pl.BlockSpec(memory_space=pltpu.VMEM)


def _params(semantics=None):
    kw = dict(vmem_limit_bytes=VMEM_LIMIT_MB * 1024 * 1024)
    if semantics is not None:
        kw["dimension_semantics"] = semantics
    return pltpu.CompilerParams(**kw)


def _sigmoid(v):
    return 0.5 * (1.0 + jnp.tanh(0.5 * v))


def _gelu(v):
    return 0.5 * v * (1.0 + lax.erf(v * (2.0 ** -0.5)))


def _gelu_grad(v):
    return 0.5 * (1.0 + lax.erf(v * (2.0 ** -0.5))) + v * jnp.exp(-0.5 * v * v) * ((2.0 * jnp.pi) ** -0.5)


def _row(m):
    return lambda i: (i, 0)


def _fixed2(i):
    return (0, 0)


def _fixed3(i):
    return (0, 0, 0)


def _resident(shape):
    return pl.BlockSpec(shape, _fixed2 if len(shape) == 2 else _fixed3, pipeline_mode=pl.Buffered(1))


def mod_matmul(x, scl, shift, w, bias, out_dtype, name):
    S, D = x.shape
    NS, _, n = w.shape
    tm = min(MATMUL_ROW_TILE, S)
    has_bias = bias is not None

    nm = S // tm

    def body(*refs):
        if has_bias:
            x_ref, scl_ref, sh_ref, w_ref, b_ref, o_ref, h_ref = refs
        else:
            x_ref, scl_ref, sh_ref, w_ref, o_ref, h_ref = refs
        h = (x_ref[...] * (1.0 + scl_ref[...]) + sh_ref[...]).astype(BF16)

        @pl.when(pl.program_id(0) == 0)
        def _():
            h_ref[...] = h

        acc = jnp.dot(h, w_ref[...], preferred_element_type=F32)
        if has_bias:
            acc = acc + b_ref[...]
        o_ref[...] = acc.astype(out_dtype)

    vec = pl.BlockSpec((1, D), lambda s, m: (0, 0))
    in_specs = [pl.BlockSpec((tm, D), lambda s, m: (m, 0)), vec, vec, pl.BlockSpec((None, D, n), lambda s, m: (s, 0, 0))]
    args = [x, scl, shift, w]
    if has_bias:
        in_specs.append(pl.BlockSpec((1, n), lambda s, m: (0, s)))
        args.append(bias)
    return pl.pallas_call(
        body, name=name, grid=(NS, S // tm), in_specs=in_specs,
        out_specs=[pl.BlockSpec((tm, n), lambda s, m: (m, s)),
                   pl.BlockSpec((tm, D), lambda s, m: (jnp.where(s == 0, m, nm - 1), 0))],
        out_shape=[jax.ShapeDtypeStruct((S, NS * n), out_dtype), jax.ShapeDtypeStruct((S, D), BF16)],
        compiler_params=_params(("arbitrary", "arbitrary")),
    )(*args)


def matmul_res_ln(a, w, x, gw, lg, lb, alpha, swiglu, name):
    S, D = x.shape
    K = w.shape[0]
    tm = min(ROW_TILE, S)
    ka = a.shape[1]

    def body(a_ref, w_ref, x_ref, gw_ref, lg_ref, lb_ref, xn_ref, xh_ref, rs_ref, y_ref, *act_ref):
        if swiglu:
            g = a_ref[:, :K].astype(F32)
            u = a_ref[:, K:].astype(F32)
            act = (g * _sigmoid(g) * u).astype(BF16)
            act_ref[0][...] = act
        else:
            act = a_ref[...].astype(BF16)
        y = jnp.dot(act, w_ref[...], preferred_element_type=F32)
        z = alpha * x_ref[...] + gw_ref[...] * y
        mu = jnp.mean(z, axis=-1, keepdims=True)
        zc = z - mu
        var = jnp.mean(zc * zc, axis=-1, keepdims=True)
        rstd = lax.rsqrt(var + LN_EPS)
        xhat = zc * rstd
        xn_ref[...] = xhat * lg_ref[...] + lb_ref[...]
        xh_ref[...] = xhat
        rs_ref[...] = rstd
        y_ref[...] = y.astype(BF16)

    vec = pl.BlockSpec((1, D), _fixed2)
    out_specs = [pl.BlockSpec((tm, D), _row(0)), pl.BlockSpec((tm, D), _row(0)), pl.BlockSpec((tm, 1), _row(0)),
                 pl.BlockSpec((tm, D), _row(0))]
    out_shape = [jax.ShapeDtypeStruct((S, D), F32), jax.ShapeDtypeStruct((S, D), F32),
                 jax.ShapeDtypeStruct((S, 1), F32), jax.ShapeDtypeStruct((S, D), BF16)]
    if swiglu:
        out_specs.append(pl.BlockSpec((tm, K), _row(0)))
        out_shape.append(jax.ShapeDtypeStruct((S, K), BF16))
    return pl.pallas_call(
        body, name=name, grid=(S // tm,),
        in_specs=[pl.BlockSpec((tm, ka), _row(0)), _resident((K, D)), pl.BlockSpec((tm, D), _row(0)),
                  vec, vec, vec],
        out_specs=out_specs, out_shape=out_shape,
        compiler_params=_params(("parallel",)),
    )(a, w, x, gw, lg, lb)


def _ln_res_bwd_tile(d, xh_ref, rs_ref, lg_ref, y_ref, gw_ref, wres, alpha, dxa_ref, dy_ref, acc_ref, row0):
    xh = xh_ref[...]
    dxh = d * lg_ref[...]
    m1 = jnp.mean(dxh, axis=-1, keepdims=True)
    m2 = jnp.mean(dxh * xh, axis=-1, keepdims=True)
    dz = rs_ref[...] * (dxh - m1 - xh * m2)
    dxa_ref[...] = alpha * dz
    dy_ref[...] = (gw_ref[...] * dz).astype(BF16)
    acc_ref[row0:row0 + 1, :] += jnp.sum(d * xh, axis=0, keepdims=True)
    acc_ref[row0 + 1:row0 + 2, :] += jnp.sum(d, axis=0, keepdims=True)
    acc_ref[row0 + 2:row0 + 3, :] += jnp.sum((wres * dz) * y_ref[...].astype(F32), axis=0, keepdims=True)


def ln_res_bwd(dxn, xhat, rstd, lg, y, gw, wres, alpha, name):
    S, D = dxn.shape
    tm = min(MATMUL_ROW_TILE, S)

    def body(dxn_ref, xh_ref, rs_ref, lg_ref, y_ref, gw_ref, dxa_ref, dy_ref, acc_ref):
        @pl.when(pl.program_id(0) == 0)
        def _():
            acc_ref[...] = jnp.zeros_like(acc_ref)

        _ln_res_bwd_tile(dxn_ref[...], xh_ref, rs_ref, lg_ref, y_ref, gw_ref, wres, alpha, dxa_ref, dy_ref, acc_ref, 0)

    vec = pl.BlockSpec((1, D), _fixed2)
    tile = pl.BlockSpec((tm, D), _row(0))
    return pl.pallas_call(
        body, name=name, grid=(S // tm,),
        in_specs=[tile, tile, pl.BlockSpec((tm, 1), _row(0)), vec, tile, vec],
        out_specs=[tile, tile, pl.BlockSpec((8, D), _fixed2)],
        out_shape=[jax.ShapeDtypeStruct((S, D), F32), jax.ShapeDtypeStruct((S, D), BF16),
                   jax.ShapeDtypeStruct((8, D), F32)],
        compiler_params=_params(("arbitrary",)),
    )(dxn, xhat, rstd, lg, y, gw)


def ffn_act_bwd(dy, wd, gu, after, name):
    S, D = dy.shape
    K = wd.shape[0]
    tm = min(ROW_TILE, S)

    def body(dy_ref, wd_ref, gu_ref, after_ref, o_ref):
        da = lax.dot_general(dy_ref[...], wd_ref[...], NT, preferred_element_type=F32).astype(BF16)
        g = gu_ref[:, :K]
        u = gu_ref[:, K:]
        sg = _sigmoid(g)
        o_ref[:, :K] = da * u * (sg * (1.0 + g * (1.0 - sg)))
        o_ref[:, K:] = da * (g * sg)

    return pl.pallas_call(
        body, name=name, grid=(S // tm,),
        in_specs=[pl.BlockSpec((tm, D), _row(0)), _resident((K, D)), pl.BlockSpec((tm, 2 * K), _row(0)), ANY],
        out_specs=pl.BlockSpec((tm, 2 * K), _row(0)),
        out_shape=jax.ShapeDtypeStruct((S, 2 * K), BF16),
        compiler_params=_params(("parallel",)),
    )(dy, wd, gu, after)


def matmul_nt(a, w, after, name):
    S, D = a.shape
    K = w.shape[0]
    tm = min(MATMUL_ROW_TILE, S)

    def body(a_ref, w_ref, after_ref, o_ref):
        o_ref[...] = lax.dot_general(a_ref[...], w_ref[...], NT, preferred_element_type=F32).astype(BF16)

    return pl.pallas_call(
        body, name=name, grid=(S // tm,),
        in_specs=[pl.BlockSpec((tm, D), _row(0)), _resident((K, D)), ANY],
        out_specs=pl.BlockSpec((tm, K), _row(0)),
        out_shape=jax.ShapeDtypeStruct((S, K), BF16),
        compiler_params=_params(("parallel",)),
    )(a, w, after)


def dgrad_mod(dpre, w, dxa, xin, scl, prev, alpha, name):
    S, D = xin.shape
    NS, _, n = w.shape
    tm = min(ROW_TILE, S)
    wres = prev[5] if prev is not None else None

    def body(*refs):
        dp_ref, w_ref, dxa_ref, xin_ref, scl_ref = refs[:5]
        acc_ref, dh_ref = refs[-2], refs[-1]
        s, m = pl.program_id(0), pl.program_id(1)
        rows = pl.ds(pl.multiple_of(m * tm, tm), tm)
        part = lax.dot_general(dp_ref[...].astype(BF16), w_ref[...], NT, preferred_element_type=F32)

        @pl.when(jnp.logical_and(s == 0, m == 0))
        def _():
            acc_ref[...] = jnp.zeros_like(acc_ref)

        @pl.when(s == 0)
        def _():
            dh_ref[rows, :] = part

        @pl.when(s > 0)
        def _():
            dh_ref[rows, :] += part

        @pl.when(s == NS - 1)
        def _():
            dh = dh_ref[rows, :]
            dx = dxa_ref[...] + dh * (1.0 + scl_ref[...])
            acc_ref[0:1, :] += jnp.sum(dh * xin_ref[...], axis=0, keepdims=True)
            acc_ref[1:2, :] += jnp.sum(dh, axis=0, keepdims=True)
            if prev is None:
                refs[5][...] = dx
            else:
                xh_ref, rs_ref, lg_ref, y_ref, gw_ref, pdxa_ref, pdy_ref = refs[5:12]
                _ln_res_bwd_tile(dx, xh_ref, rs_ref, lg_ref, y_ref, gw_ref, wres, alpha, pdxa_ref, pdy_ref,
                                 acc_ref, 2)

    def late(s, m):
        return (jnp.where(s == NS - 1, m, 0), 0)

    tile = pl.BlockSpec((tm, D), late)
    vec = pl.BlockSpec((1, D), lambda s, m: (0, 0))
    in_specs = [pl.BlockSpec((tm, n), lambda s, m: (m, s)), pl.BlockSpec((None, D, n), lambda s, m: (s, 0, 0)),
                tile, tile, vec]
    args = [dpre, w, dxa, xin, scl]
    if prev is None:
        out_specs = [tile]
        out_shape = [jax.ShapeDtypeStruct((S, D), F32)]
    else:
        in_specs += [tile, pl.BlockSpec((tm, 1), late), vec, tile, vec]
        args += list(prev[:5])
        out_specs = [tile, tile]
        out_shape = [jax.ShapeDtypeStruct((S, D), F32), jax.ShapeDtypeStruct((S, D), BF16)]
    return pl.pallas_call(
        body, name=name, grid=(NS, S // tm), in_specs=in_specs,
        out_specs=out_specs + [pl.BlockSpec((8, D), lambda s, m: (0, 0))],
        out_shape=out_shape + [jax.ShapeDtypeStruct((8, D), F32)],
        scratch_shapes=[pltpu.VMEM((S, D), F32)],
        compiler_params=_params(("arbitrary", "arbitrary")),
    )(*args)


def wgrad_pair(a, b, J, kb, nb, a_block, b_block, half_idx, name):
    S = a.shape[0]
    ts = min(WGRAD_ROWS, S)
    nsteps = S // ts

    def body(h_ref, a_ref, b_ref, o_ref, acc_ref, send_buf, recv_buf, send_sems, recv_sems):
        jj, si = pl.program_id(0), pl.program_id(1)
        x, y, c = _coords()
        j = lax.rem(jj, J)
        last = si == nsteps - 1

        def copy(blk):
            return pltpu.make_async_remote_copy(
                src_ref=send_buf.at[blk], dst_ref=recv_buf.at[blk], send_sem=send_sems.at[blk],
                recv_sem=recv_sems.at[blk], device_id=(x, y, 1 - c), device_id_type=MESH)

        @pl.when(si == 0)
        def _():
            acc_ref[...] = jnp.zeros_like(acc_ref)

        acc_ref[...] += lax.dot_general(a_ref[...], b_ref[...].astype(BF16), TN, preferred_element_type=F32)

        @pl.when(jnp.logical_and(last, jj < J))
        def _():
            send_buf[j] = acc_ref[...].astype(BF16)
            copy(j).start()

        @pl.when(jnp.logical_and(last, jj >= J))
        def _():
            copy(j).wait_recv()
            o_ref[...] = (acc_ref[...] + recv_buf[j].astype(F32)).astype(BF16)

        @pl.when(jnp.logical_and(last, jj == 2 * J - 1))
        def _():
            for blk in range(J):
                copy(blk).wait_send()

    def half(jj, h):
        return jnp.where(jj < J, 1 - h[0], h[0])

    return pl.pallas_call(
        body, name=name,
        grid_spec=pltpu.PrefetchScalarGridSpec(
            num_scalar_prefetch=1, grid=(2 * J, nsteps),
            in_specs=[pl.BlockSpec((ts, kb), lambda jj, s, h: (s, a_block(lax.rem(jj, J), half(jj, h)))),
                      pl.BlockSpec((ts, nb), lambda jj, s, h: (s, b_block(lax.rem(jj, J), half(jj, h))))],
            out_specs=pl.BlockSpec((None, kb, nb), lambda jj, s, h: (jnp.maximum(jj - J, 0), 0, 0)),
            scratch_shapes=[pltpu.VMEM((kb, nb), F32), pltpu.VMEM((J, kb, nb), BF16), pltpu.VMEM((J, kb, nb), BF16),
                            pltpu.SemaphoreType.DMA((J,)), pltpu.SemaphoreType.DMA((J,))]),
        out_shape=jax.ShapeDtypeStruct((J, kb, nb), BF16),
        compiler_params=_params(("arbitrary", "arbitrary")),
    )(half_idx, a, b)


def _window_mask():
    t = lax.broadcasted_iota(jnp.int32, (GMLP_WINDOW, GMLP_WINDOW), 0)
    s = lax.broadcasted_iota(jnp.int32, (GMLP_WINDOW, GMLP_WINDOW), 1)
    return ((s // CHUNK) <= (t // CHUNK)).astype(F32)


def sgu_fwd(pre, glg, glb, ws, bst, name):
    S, H2 = pre.shape
    H = H2 // 2
    W, G = GMLP_WINDOW, GMLP_GROUPS
    gd = H // G
    tm = min(ROW_TILE, S)

    def body(pre_ref, glg_ref, glb_ref, ws_ref, bst_ref, q_ref):
        u = _gelu(pre_ref[:, :H])
        v = _gelu(pre_ref[:, H:])
        mu = jnp.mean(v, axis=-1, keepdims=True)
        vc = v - mu
        var = jnp.mean(vc * vc, axis=-1, keepdims=True)
        vn = ((vc * lax.rsqrt(var + LN_EPS)) * glg_ref[...] + glb_ref[...]).astype(BF16)
        mask = _window_mask()
        for g in range(G):
            wsg = (ws_ref[g] * mask).astype(BF16)
            bcol = bst_ref[:, g:g + 1]
            for wi in range(tm // W):
                rows = slice(wi * W, (wi + 1) * W)
                cols = slice(g * gd, (g + 1) * gd)
                s = jnp.dot(wsg, vn[rows, cols], preferred_element_type=F32) + bcol
                q_ref[rows, cols] = (u[rows, cols] * s).astype(BF16)

    return pl.pallas_call(
        body, name=name, grid=(S // tm,),
        in_specs=[pl.BlockSpec((tm, H2), _row(0)), pl.BlockSpec((1, H), _fixed2), pl.BlockSpec((1, H), _fixed2),
                  pl.BlockSpec((G, W, W), _fixed3), pl.BlockSpec((W, G), _fixed2)],
        out_specs=pl.BlockSpec((tm, H), _row(0)),
        out_shape=jax.ShapeDtypeStruct((S, H), BF16),
        compiler_params=_params(("parallel",)),
    )(pre, glg, glb, ws, bst)


def sgu_bwd(dq, pre, glg, glb, ws, bst, name):
    S, H2 = pre.shape
    H = H2 // 2
    W, G = GMLP_WINDOW, GMLP_GROUPS
    gd = H // G
    tm = min(ROW_TILE, S)

    def body(dq_ref, pre_ref, glg_ref, glb_ref, ws_ref, bst_ref,
             dpre_ref, dws_ref, dss_ref, dgl_ref, dbin_ref, du_s, dvn_s):
        @pl.when(pl.program_id(0) == 0)
        def _():
            dws_ref[...] = jnp.zeros_like(dws_ref)
            dss_ref[...] = jnp.zeros_like(dss_ref)
            dgl_ref[...] = jnp.zeros_like(dgl_ref)
            dbin_ref[...] = jnp.zeros_like(dbin_ref)

        pu = pre_ref[:, :H]
        pv = pre_ref[:, H:]
        u = _gelu(pu)
        v = _gelu(pv)
        mu = jnp.mean(v, axis=-1, keepdims=True)
        vc = v - mu
        var = jnp.mean(vc * vc, axis=-1, keepdims=True)
        rstd = lax.rsqrt(var + LN_EPS)
        vhat = vc * rstd
        vn = (vhat * glg_ref[...] + glb_ref[...]).astype(BF16)
        mask = _window_mask()
        for g in range(G):
            wsg = (ws_ref[g] * mask).astype(BF16)
            bcol = bst_ref[:, g:g + 1]
            cols = slice(g * gd, (g + 1) * gd)
            for wi in range(tm // W):
                rows = slice(wi * W, (wi + 1) * W)
                vblk = vn[rows, cols]
                s = jnp.dot(wsg, vblk, preferred_element_type=F32) + bcol
                dqb = dq_ref[rows, cols].astype(F32)
                du_s[rows, cols] = dqb * s
                ds = dqb * u[rows, cols]
                dss_ref[:, cols] += ds
                dsb = ds.astype(BF16)
                dvn_s[rows, cols] = lax.dot_general(wsg, dsb, TN, preferred_element_type=F32)
                dws_ref[g] += lax.dot_general(dsb, vblk, NT, preferred_element_type=F32) * mask
        dvn = dvn_s[...]
        dgl_ref[0:1, :] += jnp.sum(dvn * vhat, axis=0, keepdims=True)
        dgl_ref[1:2, :] += jnp.sum(dvn, axis=0, keepdims=True)
        dvh = dvn * glg_ref[...]
        m1 = jnp.mean(dvh, axis=-1, keepdims=True)
        m2 = jnp.mean(dvh * vhat, axis=-1, keepdims=True)
        dv = rstd * (dvh - m1 - vhat * m2)
        dpu = du_s[...] * _gelu_grad(pu)
        dpv = dv * _gelu_grad(pv)
        dbin_ref[0:1, :H] += jnp.sum(dpu, axis=0, keepdims=True)
        dbin_ref[0:1, H:] += jnp.sum(dpv, axis=0, keepdims=True)
        dpre_ref[:, :H] = dpu.astype(BF16)
        dpre_ref[:, H:] = dpv.astype(BF16)

    return pl.pallas_call(
        body, name=name, grid=(S // tm,),
        in_specs=[pl.BlockSpec((tm, H), _row(0)), pl.BlockSpec((tm, H2), _row(0)), pl.BlockSpec((1, H), _fixed2),
                  pl.BlockSpec((1, H), _fixed2), pl.BlockSpec((G, W, W), _fixed3), pl.BlockSpec((W, G), _fixed2)],
        out_specs=[pl.BlockSpec((tm, H2), _row(0)), pl.BlockSpec((G, W, W), _fixed3), pl.BlockSpec((W, H), _fixed2),
                   pl.BlockSpec((8, H), _fixed2), pl.BlockSpec((8, H2), _fixed2)],
        out_shape=[jax.ShapeDtypeStruct((S, H2), BF16), jax.ShapeDtypeStruct((G, W, W), F32),
                   jax.ShapeDtypeStruct((W, H), F32), jax.ShapeDtypeStruct((8, H), F32),
                   jax.ShapeDtypeStruct((8, H2), F32)],
        scratch_shapes=[pltpu.VMEM((tm, H), F32), pltpu.VMEM((tm, H), F32)],
        compiler_params=_params(("arbitrary",)),
    )(dq, pre, glg, glb, ws, bst)


def group_lane_sum(dss, name):
    W, H = dss.shape
    gd = H // GMLP_GROUPS

    def body(d_ref, o_ref):
        j = lax.broadcasted_iota(jnp.int32, (H, LANES), 0)
        g = lax.broadcasted_iota(jnp.int32, (H, LANES), 1)
        ind = ((j // gd) == g).astype(F32)
        o_ref[...] = jnp.dot(d_ref[...], ind, preferred_element_type=F32, precision=HIGHEST)

    return pl.pallas_call(
        body, name=name, in_specs=[VMEM_SPEC], out_specs=VMEM_SPEC,
        out_shape=jax.ShapeDtypeStruct((W, LANES), F32), compiler_params=_params(),
    )(dss)


def _attn_load(j, cps, q_ref, k_ref, v_ref):
    r = lax.broadcasted_iota(jnp.int32, (CHUNK, BAND), 1)
    chunks = []
    for cc in range(cps):
        start = pl.multiple_of((j * cps + cc) * CHUNK, CHUNK)
        chunks.append((q_ref[cc * CHUNK:(cc + 1) * CHUNK, :], k_ref[pl.ds(start, BAND), :],
                       v_ref[pl.ds(start, BAND), :], (r + start) >= LEFT_PAD))
    return chunks


def _attn_probs(chunks, b_ref, sels, scale):
    qms = [[jnp.where(sel, q2, jnp.zeros_like(q2)) for sel in sels] for q2, _, _, _ in chunks]
    raw = [[lax.dot_general(qm, k2, NT, preferred_element_type=F32) for qm in qms[cc]]
           for cc, (_, k2, _, _) in enumerate(chunks)]
    probs = []
    for cc, (_, _, _, valid) in enumerate(chunks):
        row = []
        for sub in range(2):
            s = jnp.where(valid, raw[cc][sub] * scale + b_ref[sub], -jnp.inf)
            e = jnp.exp(s - jnp.max(s, axis=-1, keepdims=True))
            row.append(e / jnp.sum(e, axis=-1, keepdims=True))
        probs.append(row)
    return qms, probs


def attn_fwd(q, kpad, vpad, bias, name):
    S, D = q.shape
    HP = D // LANES
    cps = min(ATTN_CHUNKS_PER_STEP, S // CHUNK)
    tq = cps * CHUNK
    scale = HEAD_DIM ** -0.5

    def body(q_ref, k_ref, v_ref, b_ref, o_ref):
        sel0 = lax.broadcasted_iota(jnp.int32, (CHUNK, LANES), 1) < HEAD_DIM
        chunks = _attn_load(pl.program_id(1), cps, q_ref, k_ref, v_ref)
        _, probs = _attn_probs(chunks, b_ref, (sel0, jnp.logical_not(sel0)), scale)
        outs = [[jnp.dot(probs[cc][sub].astype(BF16), v2, preferred_element_type=F32) for sub in range(2)]
                for cc, (_, _, v2, _) in enumerate(chunks)]
        o_ref[...] = jnp.concatenate([jnp.where(sel0, o[0], o[1]) for o in outs], axis=0).astype(BF16)

    kv_spec = pl.BlockSpec((S + LEFT_PAD, LANES), lambda h, j: (0, h))
    return pl.pallas_call(
        body, name=name, grid=(HP, S // tq),
        in_specs=[pl.BlockSpec((tq, LANES), lambda h, j: (j, h)), kv_spec, kv_spec,
                  pl.BlockSpec((2, CHUNK, BAND), lambda h, j: (h, 0, 0))],
        out_specs=pl.BlockSpec((tq, LANES), lambda h, j: (j, h)),
        out_shape=jax.ShapeDtypeStruct((S, D), BF16),
        compiler_params=_params(("parallel", "parallel")),
    )(q, kpad, vpad, bias)


def attn_bwd(q, do, kpad, vpad, bias, dk_in, dv_in, name):
    S, D = q.shape
    HP = D // LANES
    NH = 2 * HP
    cps = min(ATTN_CHUNKS_PER_STEP, S // CHUNK)
    tq = cps * CHUNK
    nj = S // tq
    scale = HEAD_DIM ** -0.5

    def body(q_ref, do_ref, k_ref, v_ref, b_ref, dki_ref, dvi_ref, dq_ref, dk_ref, dv_ref, db_ref, dk_acc, dv_acc):
        j = pl.program_id(1)

        @pl.when(j == 0)
        def _():
            dk_acc[:LEFT_PAD, :] = jnp.zeros((LEFT_PAD, LANES), F32)
            dv_acc[:LEFT_PAD, :] = jnp.zeros((LEFT_PAD, LANES), F32)
            dk_acc[LEFT_PAD:, :] = dki_ref[...]
            dv_acc[LEFT_PAD:, :] = dvi_ref[...]
            db_ref[...] = jnp.zeros_like(db_ref)

        sel0 = lax.broadcasted_iota(jnp.int32, (CHUNK, LANES), 1) < HEAD_DIM
        sels = (sel0, jnp.logical_not(sel0))
        chunks = _attn_load(j, cps, q_ref, k_ref, v_ref)
        pairs = [(cc, sub) for cc in range(cps) for sub in range(2)]
        qms, probs = _attn_probs(chunks, b_ref, sels, scale)
        doms = [[jnp.where(sel, do_ref[cc * CHUNK:(cc + 1) * CHUNK, :], jnp.zeros((CHUNK, LANES), BF16))
                 for sel in sels] for cc in range(cps)]
        dps = {(cc, sub): lax.dot_general(doms[cc][sub], chunks[cc][2], NT, preferred_element_type=F32)
               for cc, sub in pairs}
        dss = {}
        for cc, sub in pairs:
            p = probs[cc][sub]
            dss[cc, sub] = p * (dps[cc, sub] - jnp.sum(dps[cc, sub] * p, axis=-1, keepdims=True))
        dsb = {key: ds.astype(BF16) for key, ds in dss.items()}
        dqs = {(cc, sub): jnp.dot(dsb[cc, sub], chunks[cc][1], preferred_element_type=F32) * scale
               for cc, sub in pairs}
        dks = {(cc, sub): lax.dot_general(dsb[cc, sub], qms[cc][sub], TN, preferred_element_type=F32) * scale
               for cc, sub in pairs}
        dvs = {(cc, sub): lax.dot_general(probs[cc][sub].astype(BF16), doms[cc][sub], TN,
                                          preferred_element_type=F32) for cc, sub in pairs}
        dq_ref[...] = jnp.concatenate([jnp.where(sel0, dqs[cc, 0], dqs[cc, 1]) for cc in range(cps)],
                                      axis=0).astype(BF16)
        for sub in range(2):
            total = dss[0, sub]
            for cc in range(1, cps):
                total = total + dss[cc, sub]
            db_ref[sub] += total
        dk_parts = [dks[cc, 0] + dks[cc, 1] for cc in range(cps)]
        dv_parts = [dvs[cc, 0] + dvs[cc, 1] for cc in range(cps)]

        def window(parts):
            blocks = []
            for rb in range(cps - 1 + BAND // CHUNK):
                acc = None
                for cc in range(cps):
                    b = rb - cc
                    if 0 <= b < BAND // CHUNK:
                        piece = parts[cc][b * CHUNK:(b + 1) * CHUNK, :]
                        acc = piece if acc is None else acc + piece
                blocks.append(acc)
            return jnp.concatenate(blocks, axis=0)

        span = pl.ds(pl.multiple_of(j * cps * CHUNK, CHUNK), (cps - 1) * CHUNK + BAND)
        dk_acc[span, :] += window(dk_parts)
        dv_acc[span, :] += window(dv_parts)

        @pl.when(j == nj - 1)
        def _():
            dk_ref[...] = dk_acc[LEFT_PAD:, :]
            dv_ref[...] = dv_acc[LEFT_PAD:, :]

    q_spec = pl.BlockSpec((tq, LANES), lambda h, j: (j, h))
    kv_spec = pl.BlockSpec((S + LEFT_PAD, LANES), lambda h, j: (0, h))
    col_spec = pl.BlockSpec((S, LANES), lambda h, j: (0, h))
    b_spec = pl.BlockSpec((2, CHUNK, BAND), lambda h, j: (h, 0, 0))
    return pl.pallas_call(
        body, name=name, grid=(HP, nj),
        in_specs=[q_spec, q_spec, kv_spec, kv_spec, b_spec, col_spec, col_spec],
        out_specs=[q_spec, col_spec, col_spec, b_spec],
        out_shape=[jax.ShapeDtypeStruct((S, D), BF16), jax.ShapeDtypeStruct((S, D), F32),
                   jax.ShapeDtypeStruct((S, D), F32), jax.ShapeDtypeStruct((NH, CHUNK, BAND), F32)],
        scratch_shapes=[pltpu.VMEM((S + LEFT_PAD, LANES), F32), pltpu.VMEM((S + LEFT_PAD, LANES), F32)],
        compiler_params=_params(("parallel", "arbitrary")),
    )(q, do, kpad, vpad, bias, dk_in, dv_in)


def _rel_onehot(t):
    r = lax.broadcasted_iota(jnp.int32, (BAND, N_REL), 0)
    i = lax.broadcasted_iota(jnp.int32, (BAND, N_REL), 1)
    idx = jnp.clip(t + LEFT_PAD - r, -(CHUNK - 1), MAX_REL) + (CHUNK - 1)
    return (idx == i).astype(F32)


def bias_expand(rb, name):
    NH = rb.shape[0]

    def body(rb_ref, o_ref):
        def step(t, carry):
            o_ref[t] = lax.dot_general(rb_ref[...], _rel_onehot(t), NT, preferred_element_type=F32,
                                       precision=HIGHEST)
            return carry

        lax.fori_loop(0, CHUNK, step, 0)

    return pl.pallas_call(
        body, name=name, in_specs=[VMEM_SPEC], out_specs=VMEM_SPEC,
        out_shape=jax.ShapeDtypeStruct((CHUNK, NH, BAND), F32), compiler_params=_params(),
    )(rb)


def bias_grad(dsum, name):
    NH = dsum.shape[1]

    def body(d_ref, o_ref):
        def step(t, acc):
            return acc + jnp.dot(d_ref[t], _rel_onehot(t), preferred_element_type=F32, precision=HIGHEST)

        o_ref[...] = lax.fori_loop(0, CHUNK, step, jnp.zeros((NH, N_REL), F32))

    return pl.pallas_call(
        body, name=name, in_specs=[VMEM_SPEC], out_specs=VMEM_SPEC,
        out_shape=jax.ShapeDtypeStruct((NH, N_REL), F32), compiler_params=_params(),
    )(dsum)


def loss_grad(y, tgt, name):
    S, D = y.shape
    tm = min(MATMUL_ROW_TILE, S)

    def body(y_ref, t_ref, d_ref, acc_ref):
        @pl.when(pl.program_id(0) == 0)
        def _():
            acc_ref[...] = jnp.zeros_like(acc_ref)

        err = y_ref[...] - t_ref[...]
        d_ref[...] = err * (1.0 / D)
        acc_ref[0:1, :] += jnp.sum(err * err, axis=0, keepdims=True)

    tile = pl.BlockSpec((tm, D), _row(0))
    return pl.pallas_call(
        body, name=name, grid=(S // tm,), in_specs=[tile, tile],
        out_specs=[tile, pl.BlockSpec((8, D), _fixed2)],
        out_shape=[jax.ShapeDtypeStruct((S, D), F32), jax.ShapeDtypeStruct((8, D), F32)],
        compiler_params=_params(("arbitrary",)),
    )(y, tgt)


def _col_tile(n):
    for t in (768, 512, 256, 128):
        if n % t == 0:
            return t
    return n


def ada_fwd(c_all, w, b, name):
    L, D, n = w.shape
    tn = _col_tile(n)

    def body(c_ref, w_ref, b_ref, o_ref):
        cv = c_ref[...]
        ca = cv * _sigmoid(cv)
        o_ref[...] = jnp.dot(ca, w_ref[...], preferred_element_type=F32, precision=HIGHEST) + b_ref[...]

    return pl.pallas_call(
        body, name=name, grid=(L, n // tn),
        in_specs=[pl.BlockSpec((N_DEV, D), lambda l, j: (0, 0)), pl.BlockSpec((None, D, tn), lambda l, j: (l, 0, j)),
                  pl.BlockSpec((None, 1, tn), lambda l, j: (l, 0, j))],
        out_specs=pl.BlockSpec((None, N_DEV, tn), lambda l, j: (l, 0, j)),
        out_shape=jax.ShapeDtypeStruct((L, N_DEV, n), F32),
        compiler_params=_params(("parallel", "parallel")),
    )(c_all, w, b)


def ada_wgrad(c_all_t, dmod, name):
    L, _, n = dmod.shape
    D = c_all_t.shape[0]
    tn = _col_tile(n)

    def body(c_ref, d_ref, o_ref):
        cv = c_ref[...]
        ca = cv * _sigmoid(cv)
        o_ref[...] = jnp.dot(ca, d_ref[...], preferred_element_type=F32, precision=HIGHEST)

    return pl.pallas_call(
        body, name=name, grid=(L, n // tn),
        in_specs=[pl.BlockSpec((D, N_DEV), lambda l, j: (0, 0)), pl.BlockSpec((None, N_DEV, tn), lambda l, j: (l, 0, j))],
        out_specs=pl.BlockSpec((None, D, tn), lambda l, j: (l, 0, j)),
        out_shape=jax.ShapeDtypeStruct((L, D, n), F32),
        compiler_params=_params(("parallel", "parallel")),
    )(c_all_t, dmod)


ELEMENTWISE_BLOCK_BYTES = 3 * 1024 * 1024


def _elementwise_rows(rows, row_bytes):
    for t in (4096, 2048, 1024, 512, 256, 128, 64, 32, 16):
        if rows % t == 0 and t * row_bytes <= ELEMENTWISE_BLOCK_BYTES:
            return t
    return rows


def sum_leading(a, name):
    n, M, N = a.shape
    tr = _elementwise_rows(M, n * N * 4)

    def body(a_ref, o_ref):
        acc = a_ref[0]
        for i in range(1, n):
            acc = acc + a_ref[i]
        o_ref[...] = acc

    return pl.pallas_call(
        body, name=name, grid=(M // tr,),
        in_specs=[pl.BlockSpec((n, tr, N), lambda i: (0, i, 0))],
        out_specs=pl.BlockSpec((tr, N), _row(0)),
        out_shape=jax.ShapeDtypeStruct((M, N), F32),
        compiler_params=_params(("parallel",)),
    )(a)


def chip_sum(psum, land, chip_idx, name):
    n, M, N = psum.shape
    tr = _elementwise_rows(M, N * 8)

    def body(s_ref, p_ref, a_ref, b_ref, c_ref, o_ref):
        o_ref[...] = ((p_ref[...].astype(F32) + a_ref[...].astype(F32)) + b_ref[...].astype(F32)) + c_ref[...].astype(F32)

    def entry(j):
        return pl.BlockSpec((None, tr, N), lambda i, s: ((s[0] + j) % n, i, 0))

    return pl.pallas_call(
        body, name=name,
        grid_spec=pltpu.PrefetchScalarGridSpec(
            num_scalar_prefetch=1, grid=(M // tr,),
            in_specs=[entry(0), entry(1), entry(2), entry(3)],
            out_specs=pl.BlockSpec((tr, N), lambda i, s: (i, 0))),
        out_shape=jax.ShapeDtypeStruct((M, N), F32),
        compiler_params=_params(("parallel",)),
    )(chip_idx, psum, land, land, land)


def adamw(w, g, m, v, name):
    M, N = w.shape
    tr = _elementwise_rows(M, N * 4)
    c1 = 1.0 - ADAM_B1 ** ADAM_STEP
    c2 = 1.0 - ADAM_B2 ** ADAM_STEP

    def body(w_ref, g_ref, m_ref, v_ref, d_ref, nm_ref, nv_ref):
        g = g_ref[...]
        nm = ADAM_B1 * m_ref[...] + (1.0 - ADAM_B1) * g
        nv = ADAM_B2 * v_ref[...] + (1.0 - ADAM_B2) * (g * g)
        d_ref[...] = -ADAM_LR * ((nm / c1) / (jnp.sqrt(nv / c2) + ADAM_EPS) + ADAM_WD * w_ref[...])
        nm_ref[...] = nm
        nv_ref[...] = nv

    spec = pl.BlockSpec((tr, N), _row(0))
    shp = jax.ShapeDtypeStruct((M, N), F32)
    return pl.pallas_call(
        body, name=name, grid=(M // tr,), in_specs=[spec] * 4, out_specs=[spec] * 3, out_shape=[shp] * 3,
        compiler_params=_params(("parallel",)),
    )(w, g, m, v)


def _coords():
    return lax.axis_index("x"), lax.axis_index("y"), lax.axis_index("c")


def all_gather8(block, name):
    m_per, n = block.shape

    def body(x_ref, out_ref, send_sems, recv_sems, local_sem):
        x, y, c = _coords()
        me, sibling = (x, y, c), (x, y, 1 - c)
        chips = [(1 - x, y), (x, 1 - y), (1 - x, 1 - y)]

        def rows(px, py, pc):
            return out_ref.at[pl.ds((4 * px + 2 * py + pc) * m_per, m_per), :]

        def copy(k, blk, to, src=None):
            return pltpu.make_async_remote_copy(
                src_ref=rows(*blk) if src is None else src, dst_ref=rows(*blk),
                send_sem=send_sems.at[k], recv_sem=recv_sems.at[k], device_id=to, device_id_type=MESH)

        mine = pltpu.make_async_copy(x_ref, rows(*me), local_sem)
        mine.start()
        first = [copy(0, me, sibling, src=x_ref)]
        first += [copy(1 + j, me, (*chip, c), src=x_ref) for j, chip in enumerate(chips)]
        for cp in first:
            cp.start()
        passed = [copy(4 + j, (*chip, c), sibling) for j, chip in enumerate(chips)]
        for j, chip in enumerate(chips):
            copy(1 + j, (*chip, c), me).wait_recv()
            passed[j].start()
        copy(0, sibling, me).wait_recv()
        for j, chip in enumerate(chips):
            copy(4 + j, (*chip, 1 - c), me).wait_recv()
        for cp in first + passed:
            cp.wait_send()
        mine.wait()

    return pl.pallas_call(
        body, name=name, in_specs=[VMEM_SPEC], out_specs=VMEM_SPEC,
        out_shape=jax.ShapeDtypeStruct((N_DEV * m_per, n), block.dtype),
        scratch_shapes=[pltpu.SemaphoreType.DMA((7,)), pltpu.SemaphoreType.DMA((7,)), pltpu.SemaphoreType.DMA],
        compiler_params=_params(),
    )(block)


def _other_chips(x, y):
    return [(1 - x, y), (x, 1 - y), (1 - x, 1 - y)]


HBM_SPEC = pl.BlockSpec(memory_space=pltpu.HBM)
SEM_SPEC = pl.BlockSpec(memory_space=pltpu.SEMAPHORE)
DATAFLOW = pltpu.SideEffectType.DATAFLOW_SIDE_EFFECTING


def _chip_peers(x, y, c):
    return [(px, py, c) for px, py in _other_chips(x, y)]


def _sibling_peer(x, y, c):
    return [(x, y, 1 - c)]


def _weight_desc(stack_ref, k, land_ref, peer, me):
    h = land_ref.shape[1] // 2
    rows = pl.ds(me[2] * h, h)
    return (stack_ref.at[k, rows, :], land_ref.at[2 * me[0] + me[1], rows, :],
            land_ref.at[2 * peer[0] + peer[1], rows, :])


def _grad_desc(psum_ref, k, land_ref, peer, me):
    return psum_ref.at[2 * peer[0] + peer[1]], land_ref.at[2 * me[0] + me[1]], land_ref.at[2 * peer[0] + peer[1]]


def _pair_desc(grad_ref, k, land_ref, peer, me):
    h = land_ref.shape[1]
    return grad_ref.at[:, pl.ds(peer[2] * h, h), :], land_ref, land_ref


def _whole_desc(src_ref, k, land_ref, peer, me):
    return src_ref, land_ref, land_ref


def exchange_start(srcs, lands, units, groups, desc, peers, after, name):
    n_s, n_l, n_g = len(srcs), len(lands), len(groups)
    n_p = len(peers(0, 0, 0))

    def body(*refs):
        s_refs, l_refs = refs[:n_s], refs[n_s:n_s + n_l]
        outs = refs[n_s + n_l + 1:]
        sems, token = outs[:2 * n_g], outs[-1]
        me = _coords()
        for g, ids in enumerate(groups):
            for i, u in enumerate(ids):
                si, k = units[u]
                for j, peer in enumerate(peers(*me)):
                    src, dst, _ = desc(s_refs[si], k, l_refs[u], peer, me)
                    pltpu.make_async_remote_copy(
                        src_ref=src, dst_ref=dst, send_sem=sems[2 * g].at[n_p * i + j],
                        recv_sem=sems[2 * g + 1].at[n_p * i + j], device_id=peer, device_id_type=MESH).start()
        token[...] = jnp.zeros_like(token)

    arrs = list(srcs) + list(lands)
    sem_shapes = [pltpu.SemaphoreType.DMA((n_p * len(ids),)) for ids in groups for _ in range(2)]
    outs = pl.pallas_call(
        body, name=name,
        in_specs=[HBM_SPEC] * len(arrs) + [ANY],
        out_specs=[SEM_SPEC] * (2 * n_g) + [HBM_SPEC] * len(arrs) + [VMEM_SPEC],
        out_shape=sem_shapes + [pltpu.HBM(a.shape, a.dtype) for a in arrs] + [jax.ShapeDtypeStruct((8, LANES), F32)],
        input_output_aliases={i: 2 * n_g + i for i in range(len(arrs))},
        compiler_params=pltpu.CompilerParams(has_side_effects=DATAFLOW),
    )(*[pltpu.with_memory_space_constraint(a, pltpu.HBM) for a in arrs], after)
    sems = outs[:2 * n_g]
    thru = outs[2 * n_g:2 * n_g + len(arrs)]
    return sems, list(thru[:n_s]), list(thru[n_s:]), outs[-1]


def exchange_wait(srcs, lands, units, send_sem, recv_sem, desc, peers, after, name):
    n_s, n_l = len(srcs), len(lands)
    n_p = len(peers(0, 0, 0))

    def body(*refs):
        s_refs, l_refs = refs[:n_s], refs[n_s:n_s + n_l]
        send_sems, recv_sems = refs[n_s + n_l], refs[n_s + n_l + 1]
        me = _coords()
        for i, (si, k) in enumerate(units):
            for j, peer in enumerate(peers(*me)):
                src, _, mine = desc(s_refs[si], k, l_refs[i], peer, me)
                cp = pltpu.make_async_remote_copy(
                    src_ref=src, dst_ref=mine, send_sem=send_sems.at[n_p * i + j], recv_sem=recv_sems.at[n_p * i + j],
                    device_id=peer, device_id_type=MESH)
                cp.wait_send()
                cp.wait_recv()

    arrs = list(srcs) + list(lands)
    outs = pl.pallas_call(
        body, name=name,
        in_specs=[HBM_SPEC] * len(arrs) + [SEM_SPEC, SEM_SPEC, ANY],
        out_specs=[HBM_SPEC] * len(arrs),
        out_shape=[pltpu.HBM(a.shape, a.dtype) for a in arrs],
        input_output_aliases={i: i for i in range(len(arrs))},
        compiler_params=pltpu.CompilerParams(has_side_effects=DATAFLOW),
    )(*arrs, send_sem, recv_sem, after)
    return list(outs[:n_s]), list(outs[n_s:])


def sibling_fill(lands, name):
    n_u = len(lands)

    def body(*refs):
        ins, outs = refs[:n_u], refs[n_u:2 * n_u]
        send_sems, recv_sems = refs[2 * n_u:]
        x, y, c = _coords()
        sends = []
        for u in range(n_u):
            h = ins[u].shape[1] // 2
            for j, (px, py) in enumerate(_other_chips(x, y)):
                part = (2 * px + py, pl.ds(c * h, h), slice(None))
                cp = pltpu.make_async_remote_copy(
                    src_ref=ins[u].at[part], dst_ref=outs[u].at[part], send_sem=send_sems.at[3 * u + j],
                    recv_sem=recv_sems.at[3 * u + j], device_id=(x, y, 1 - c), device_id_type=MESH)
                cp.start()
                sends.append(cp)
        for u in range(n_u):
            h = ins[u].shape[1] // 2
            for j, (px, py) in enumerate(_other_chips(x, y)):
                theirs = (2 * px + py, pl.ds((1 - c) * h, h), slice(None))
                pltpu.make_async_remote_copy(
                    src_ref=ins[u].at[theirs], dst_ref=outs[u].at[theirs], send_sem=send_sems.at[3 * u + j],
                    recv_sem=recv_sems.at[3 * u + j], device_id=(x, y, 1 - c), device_id_type=MESH).wait_recv()
        for cp in sends:
            cp.wait_send()

    return pl.pallas_call(
        body, name=name, in_specs=[ANY] * n_u, out_specs=[ANY] * n_u,
        out_shape=[jax.ShapeDtypeStruct(a.shape, a.dtype) for a in lands],
        input_output_aliases={i: i for i in range(n_u)},
        scratch_shapes=[pltpu.SemaphoreType.DMA((3 * n_u,)), pltpu.SemaphoreType.DMA((3 * n_u,))],
        compiler_params=_params(),
    )(*lands)


def _pack_rows(parts):
    flat = jnp.concatenate([p.reshape(-1).astype(F32) for p in parts])
    n = flat.shape[0]
    padded = -(-n // (8 * LANES)) * (8 * LANES)
    return jnp.pad(flat, (0, padded - n)).reshape(-1, LANES)


def _unpack_rows(packed, shapes):
    flat = packed.reshape(-1)
    out, off = [], 0
    for s in shapes:
        size = 1
        for d in s:
            size *= d
        out.append(flat[off:off + size].reshape(s))
        off += size
    return out


def _shard_last(full, s_me):
    n = full.shape[-1] // N_CHIP
    return lax.dynamic_slice_in_dim(full, s_me * n, n, axis=full.ndim - 1)


def _unshard_last(g):
    moved = jnp.moveaxis(g, 0, -2)
    return moved.reshape(moved.shape[:-2] + (moved.shape[-2] * moved.shape[-1],))


def kernel(x, c, w_ada, b_ada, ln_g, ln_b, ffn_gu, ffn_down, gmlp_w_in, gmlp_b_in, gmlp_ln_g, gmlp_ln_b, gmlp_w_s, gmlp_b_s, gmlp_w_out, w_ada_kv, b_ada_kv, w_kv, attn_w_q, attn_rel_bias, attn_w_o, loss_target, m_w_ada, m_b_ada, m_ln_g, m_ln_b, m_ffn_gu, m_ffn_down, m_gmlp_w_in, m_gmlp_b_in, m_gmlp_ln_g, m_gmlp_ln_b, m_gmlp_w_s, m_gmlp_b_s, m_gmlp_w_out, m_w_ada_kv, m_b_ada_kv, m_w_kv, m_attn_w_q, m_attn_rel_bias, m_attn_w_o, v_w_ada, v_b_ada, v_ln_g, v_ln_b, v_ffn_gu, v_ffn_down, v_gmlp_w_in, v_gmlp_b_in, v_gmlp_ln_g, v_gmlp_ln_b, v_gmlp_w_s, v_gmlp_b_s, v_gmlp_w_out, v_w_ada_kv, v_b_ada_kv, v_w_kv, v_attn_w_q, v_attn_rel_bias, v_attn_w_o):
    xi, yi, ci = _coords()
    s_me = 2 * xi + yi
    dev = 4 * xi + 2 * yi + ci

    x0 = x[0]
    tgt = loss_target[0]
    S, D = x0.shape
    L = w_ada.shape[0]
    NA = gmlp_w_in.shape[0]
    NB = attn_w_q.shape[0]
    NH = D // HEAD_DIM
    alpha = (2.0 * L) ** 0.25
    n_ada = w_ada.shape[2]
    n_kv = w_ada_kv.shape[1]

    stack_names = ["ffn_gu", "ffn_down", "gmlp_w_in", "gmlp_w_out", "w_kv", "attn_w_q", "attn_w_o"]
    stack_src = dict(ffn_gu=ffn_gu, ffn_down=ffn_down, gmlp_w_in=gmlp_w_in, gmlp_w_out=gmlp_w_out, w_kv=w_kv[None],
                     attn_w_q=attn_w_q, attn_w_o=attn_w_o)
    stacks = [stack_src[nm].astype(BF16).reshape((-1,) + stack_src[nm].shape[-2:]) for nm in stack_names]
    units = [(si, k) for si, st in enumerate(stacks) for k in range(st.shape[0])]
    unit_of = {(stack_names[si], k): u for u, (si, k) in enumerate(units)}
    assert NA >= 1
    weight_groups = [[("ffn_gu", 0)], [("ffn_down", 0)],
                     [("gmlp_w_in", 0), ("gmlp_w_out", 0), ("ffn_gu", 1), ("ffn_down", 1)]]
    for l in range(1, L):
        names = [("w_kv", 0)] if l == NA else []
        names += [("ffn_gu", 2 * l), ("ffn_down", 2 * l)]
        names += [("gmlp_w_in", l), ("gmlp_w_out", l)] if l < NA else [("attn_w_q", l - NA), ("attn_w_o", l - NA)]
        names += [("ffn_gu", 2 * l + 1), ("ffn_down", 2 * l + 1)]
        weight_groups.append(names)
    weight_groups = [[unit_of[n] for n in names] for names in weight_groups]

    c_all = all_gather8(jnp.broadcast_to(c, (8, D)), "ag_c").reshape(N_DEV, 8, D)[:, 0]
    b_ada_sh = lax.dynamic_slice_in_dim(b_ada, s_me * n_ada, n_ada, axis=1)
    b_kv_sh = lax.dynamic_slice_in_dim(b_ada_kv, s_me * n_kv, n_kv, axis=0)
    mod_part = ada_fwd(c_all, w_ada, b_ada_sh[:, None, :], "ada_fwd")
    mkv_part = ada_fwd(c_all, w_ada_kv[None], b_kv_sh[None, None, :], "ada_kv_fwd")
    part = jnp.concatenate([jnp.transpose(mod_part, (1, 0, 2)).reshape(N_DEV, L * n_ada), mkv_part[0]], axis=1)
    width = part.shape[1]
    pad_w = -(-width // LANES) * LANES - width
    all_part = all_gather8(jnp.pad(part, ((0, 0), (0, pad_w))), "ag_mod").reshape(N_DEV, N_DEV, width + pad_w)
    mine = lax.dynamic_index_in_dim(all_part[0::2], dev, axis=1, keepdims=False)
    mod = jnp.transpose(mine[:, :L * n_ada].reshape(N_CHIP, L, n_ada), (1, 0, 2)).reshape(L, N_MOD, D)
    mkv = mine[:, L * n_ada:width].reshape(2, D)

    def mrow(l, k):
        return mod[l, k][None, :]

    small_shapes = [ln_g.shape, ln_b.shape, gmlp_b_in.shape, gmlp_ln_g.shape, gmlp_ln_b.shape, attn_rel_bias.shape]
    small_pack = _pack_rows([ln_g, ln_b, gmlp_b_in, gmlp_ln_g, gmlp_ln_b, attn_rel_bias])
    small_all = all_gather8(small_pack, "ag_small_params").reshape((N_DEV,) + small_pack.shape)[0::2]
    sm = [_unpack_rows(small_all[s], small_shapes) for s in range(N_CHIP)]
    ln_g_f, ln_b_f, b_in_f, gln_g_f, gln_b_f, rel_f = [
        _unshard_last(jnp.stack([sm[s][i] for s in range(N_CHIP)])) for i in range(len(small_shapes))]

    lands0 = [lax.dynamic_update_slice(lax.empty((N_CHIP,) + stacks[si].shape[1:], BF16), stacks[si][k][None],
                                       (s_me, 0, 0)) for si, k in units]
    gathers_done = jnp.concatenate([mod.reshape(-1)[:LANES], small_all.reshape(-1)[:LANES]])
    w_sems, stacks_t, lands_t, _ = exchange_start(stacks, lands0, units, weight_groups, _weight_desc, _chip_peers, gathers_done,
                                                  "weight_send_start")
    wg = {}

    def fetch_weights(g, stacks_now, after):
        ids = weight_groups[g]
        stacks_next, got = exchange_wait(stacks_now, [lands_t[u] for u in ids], [units[u] for u in ids],
                                         w_sems[2 * g], w_sems[2 * g + 1], _weight_desc, _chip_peers, after,
                                         "weight_send_wait_%d" % g)
        for u, a in zip(ids, sibling_fill(got, "weight_sibling_fill")):
            wg[u] = a
        return stacks_next

    def W(nm, k):
        return wg[unit_of[(nm, k)]]

    def Wrows(nm, k):
        w4 = W(nm, k)
        return w4.reshape(w4.shape[0] * w4.shape[1], w4.shape[2])

    bst = [jnp.transpose(gmlp_b_s[j]) for j in range(NA)]
    biases = {}

    def make_bias(j, dep):
        rel, _ = lax.optimization_barrier((rel_f[j], dep))
        biases[j] = jnp.transpose(bias_expand(rel, "bias_expand"), (1, 0, 2))
        return biases[j]

    saved = []
    xc = x0
    kpad = vpad = xkv = None
    for l in range(L):
        after = mod if l == 0 else xc
        if l == 1 and NB > 1:
            after = make_bias(1, xc)
        stacks_t = fetch_weights(0 if l == 0 else l + 2, stacks_t, after)
        if l == NA:
            xkv = xc
            kv, hkv = mod_matmul(xc, mkv[1][None], mkv[0][None], W("w_kv", 0), None, BF16, "kv_proj")
            kpad = jnp.pad(kv[:, :D], ((LEFT_PAD, 0), (0, 0)))
            vpad = jnp.pad(kv[:, D:], ((LEFT_PAD, 0), (0, 0)))
        sv = {}
        for i in (0, 2):
            k = 2 * l + i // 2
            gu, hv = mod_matmul(xc, mrow(l, 3 * i + 1), mrow(l, 3 * i), W("ffn_gu", k), None, BF16, "ffn_up")
            if l == 0 and i == 0:
                stacks_t = fetch_weights(1, stacks_t, hv)
            gw = 0.5 * (1.0 + mrow(l, 3 * i + 2))
            xn, xh, rs, yv, av = matmul_res_ln(gu, Wrows("ffn_down", k), xc, gw, ln_g_f[l, i][None],
                                               ln_b_f[l, i][None], alpha, True, "ffn_down")
            sv[i] = dict(x=xc, h=hv, gu=gu, a=av, xh=xh, rs=rs, y=yv, gw=gw)
            xc = xn
            if i == 0:
                if l == 0:
                    stacks_t = fetch_weights(2, stacks_t, make_bias(0, xc) if NB > 0 else xc)
                gw = 1.0 + mrow(l, 5)
                if l < NA:
                    pre, hv = mod_matmul(xc, mrow(l, 4), mrow(l, 3), W("gmlp_w_in", l), b_in_f[l][None], F32,
                                         "gmlp_in")
                    qv = sgu_fwd(pre, gln_g_f[l][None], gln_b_f[l][None], gmlp_w_s[l], bst[l], "sgu_fwd")
                    xn, xh, rs, yv = matmul_res_ln(qv, Wrows("gmlp_w_out", l), xc, gw, ln_g_f[l, 1][None],
                                                   ln_b_f[l, 1][None], alpha, False, "gmlp_out")
                    sv[1] = dict(x=xc, h=hv, pre=pre, a=qv, xh=xh, rs=rs, y=yv, gw=gw)
                else:
                    j = l - NA
                    if j not in biases:
                        make_bias(j, xc)
                    qh, hv = mod_matmul(xc, mrow(l, 4), mrow(l, 3), Wrows("attn_w_q", j)[None], None, BF16, "attn_q")
                    ov = attn_fwd(qh, kpad, vpad, biases[j], "attn_fwd")
                    xn, xh, rs, yv = matmul_res_ln(ov, Wrows("attn_w_o", j), xc, gw, ln_g_f[l, 1][None],
                                                   ln_b_f[l, 1][None], alpha, False, "attn_out")
                    sv[1] = dict(x=xc, h=hv, q=qh, a=ov, xh=xh, rs=rs, y=yv, gw=gw)
                xc = xn
        saved.append(sv)

    dx, lacc = loss_grad(xc, tgt, "loss_grad")
    loss = lax.psum((0.5 / D) * jnp.sum(lacc[0]), ("x", "y", "c"))

    gpair = [None] * len(units)
    col_split = {u for u, (si, _) in enumerate(units) if stack_names[si] != "ffn_gu"}
    dmod = [[None] * N_MOD for _ in range(L)]
    d_ln_g = [[None] * 3 for _ in range(L)]
    d_ln_b = [[None] * 3 for _ in range(L)]
    d_b_in, d_gln_g, d_gln_b, d_ws, d_bs, d_rel = ([None] * NA, [None] * NA, [None] * NA, [None] * NA, [None] * NA,
                                                  [None] * NB)
    dk = jnp.zeros((S, D), F32)
    dv = jnp.zeros((S, D), F32)
    dmkv = None

    made = []

    core_idx = ci.astype(jnp.int32).reshape(1)
    chip_idx = s_me.astype(jnp.int32).reshape(1)

    def put(nm, k, a, b, name):
        u = unit_of[(nm, k)]
        rows, cols = stacks[units[u][0]].shape[1:]
        if nm == "ffn_gu":
            g = wgrad_pair(a, b, N_CHIP, rows // 2, cols, lambda j, p: p, lambda j, p: j, core_idx, name)
        elif nm in ("gmlp_w_in", "w_kv"):
            g = wgrad_pair(a, b, N_CHIP, rows, cols // 2, lambda j, p: 0, lambda j, p: 2 * j + p, core_idx, name)
        else:
            g = wgrad_pair(a, b, 1, N_CHIP * rows, cols // 2, lambda j, p: 0, lambda j, p: p, core_idx, name)
        gpair[u] = g.reshape((N_CHIP, -1, g.shape[-1]))
        made.append(u)

    own_half, sib_half = {}, {}
    n_started = [0]

    def start_grad_exchange(ids, after):
        psums = [gpair[u] for u in ids]
        n = len(ids)
        tag = n_started[0]
        n_started[0] += 1
        sems, ps_t, q_t, token = exchange_start(psums, [lax.empty(p.shape, p.dtype) for p in psums],
                                                [(i, 0) for i in range(n)], [list(range(n))], _grad_desc, _chip_peers,
                                                after, "grad_send_start_%d" % tag)
        return dict(ids=ids, tag=tag, sems=sems, ps=ps_t, q=q_t), token

    def finish_grad_exchange(pend, after):
        n = len(pend["ids"])
        ps_t, q = exchange_wait(pend["ps"], pend["q"], [(i, 0) for i in range(n)], pend["sems"][0], pend["sems"][1],
                                _grad_desc, _chip_peers, after, "grad_send_wait_%d" % pend["tag"])
        halves = [chip_sum(ps_t[i], q[i], chip_idx, "grad_chip_sum") for i in range(n)]
        sems, h_t, land_t, token = exchange_start(halves, [lax.empty(h.shape, h.dtype) for h in halves],
                                              [(i, 0) for i in range(n)], [list(range(n))], _whole_desc,
                                              _sibling_peer, halves[0], "half_send_start_%d" % pend["tag"])
        swaps.append(dict(ids=pend["ids"], tag=pend["tag"], sems=sems, h=h_t, land=land_t))
        return token

    def collect_halves(after):
        for sw in swaps:
            n = len(sw["ids"])
            h, land = exchange_wait(sw["h"], sw["land"], [(i, 0) for i in range(n)], sw["sems"][0], sw["sems"][1],
                                    _whole_desc, _sibling_peer, after, "half_send_wait_%d" % sw["tag"])
            for u, mine, theirs in zip(sw["ids"], h, land):
                own_half[u], sib_half[u] = mine, theirs
        swaps.clear()

    swaps = []
    pending = None
    started_before = jnp.zeros((8, LANES), F32)

    def ln_inputs(l, i):
        t = saved[l][i]
        return (t["xh"], t["rs"], ln_g_f[l, i][None], t["y"], t["gw"], 1.0 if i == 1 else 0.5)

    def record_ln(l, i, acc, row0):
        d_ln_g[l][i], d_ln_b[l][i], dmod[l][3 * i + 2] = acc[row0], acc[row0 + 1], acc[row0 + 2]

    ln_done = None
    for l in reversed(range(L)):
        if l == NA - 1:
            dkv = jnp.concatenate([dk, dv], axis=1)
            put("w_kv", 0, hkv, dkv, "kv_wgrad")
            pdxa, pdy, acc = dgrad_mod(dkv, W("w_kv", 0), dx, xkv, mkv[1][None], ln_inputs(l, 2), alpha, "kv_dgrad")
            dmkv = jnp.stack([acc[1], acc[0]])
            record_ln(l, 2, acc, 2)
            ln_done = (pdxa, pdy)
        sv = saved[l]
        for i in (2, 1, 0):
            t = sv[i]
            if ln_done is None:
                dxa, dy, acc1 = ln_res_bwd(dx, *ln_inputs(l, i), alpha, "ln_res_bwd")
                record_ln(l, i, acc1, 0)
            else:
                dxa, dy = ln_done
                ln_done = None
            before = (l, i - 1) if i > 0 else ((l - 1, 2) if l > 0 and l != NA else None)
            prev = ln_inputs(*before) if before is not None else None
            scl = mrow(l, 3 * i + 1)
            if i != 1:
                k = 2 * l + i // 2
                F = t["gu"].shape[1] // 2
                dgu = ffn_act_bwd(dy, Wrows("ffn_down", k), t["gu"], started_before, "ffn_act_bwd")
                put("ffn_down", k, t["a"], dy, "ffn_down_wgrad")
                put("ffn_gu", k, t["h"], dgu, "ffn_up_wgrad")
                res = dgrad_mod(dgu, W("ffn_gu", k), dxa, t["x"], scl, prev, alpha, "ffn_up_dgrad")
            elif l < NA:
                dq = matmul_nt(dy, Wrows("gmlp_w_out", l), started_before, "gmlp_out_dgrad")
                put("gmlp_w_out", l, t["a"], dy, "gmlp_out_wgrad")
                dpre, dws_l, dss, dgl, dbin = sgu_bwd(dq, t["pre"], gln_g_f[l][None], gln_b_f[l][None], gmlp_w_s[l],
                                                      bst[l], "sgu_bwd")
                d_ws[l] = dws_l
                d_bs[l] = jnp.transpose(group_lane_sum(dss, "sgu_bias_grad")[:, :GMLP_GROUPS])
                d_gln_g[l], d_gln_b[l], d_b_in[l] = dgl[0], dgl[1], dbin[0]
                put("gmlp_w_in", l, t["h"], dpre, "gmlp_in_wgrad")
                res = dgrad_mod(dpre, W("gmlp_w_in", l), dxa, t["x"], scl, prev, alpha, "gmlp_in_dgrad")
            else:
                j = l - NA
                do = matmul_nt(dy, Wrows("attn_w_o", j), started_before, "attn_out_dgrad")
                put("attn_w_o", j, t["a"], dy, "attn_out_wgrad")
                dqh, dk, dv, dbias = attn_bwd(t["q"], do, kpad, vpad, biases[j], dk, dv, "attn_bwd")
                d_rel[j] = bias_grad(jnp.transpose(dbias, (1, 0, 2)), "bias_grad")
                put("attn_w_q", j, t["h"], dqh, "attn_q_wgrad")
                res = dgrad_mod(dqh, Wrows("attn_w_q", j)[None], dxa, t["x"], scl, prev, alpha, "attn_q_dgrad")
            acc2 = res[-1]
            dmod[l][3 * i + 1], dmod[l][3 * i] = acc2[0], acc2[1]
            if before is None:
                dx = res[0]
            else:
                record_ln(*before, acc2, 2)
                ln_done = (res[0], res[1])
            if (i == 0 and l > 0) or (i == 1 and l == 0):
                started, started_before = start_grad_exchange(list(made), acc2)
                made.clear()
                if pending is not None:
                    started_before = started_before + finish_grad_exchange(pending, acc2)
                pending = started
    grad_x = dx[None]

    dvec = _pack_rows([jnp.stack([jnp.stack(r) for r in dmod]), dmkv])
    dvec = lax.optimization_barrier((dvec, [gpair[u] for u in made]))[0]
    n_dvec = L * N_MOD * D + 2 * D
    dall = all_gather8(dvec, "ag_dmod").reshape(N_DEV, -1, LANES)
    db_all = sum_leading(dall, "ada_bias_grad").reshape(-1)[:n_dvec]
    g_b_ada = db_all[:L * N_MOD * D].reshape(L, N_MOD * D)
    g_b_ada_kv = db_all[L * N_MOD * D:]
    dall2 = dall.reshape(N_DEV, -1)[:, :n_dvec]
    dmod_all = dall2[:, :L * N_MOD * D].reshape(N_DEV, L, N_MOD * D)
    dmod_sh = jnp.transpose(lax.dynamic_slice_in_dim(dmod_all, s_me * n_ada, n_ada, axis=2), (1, 0, 2))
    dmkv_sh = lax.dynamic_slice_in_dim(dall2[:, L * N_MOD * D:], s_me * n_kv, n_kv, axis=1)[None]
    c_all_t = jnp.transpose(c_all)
    g_w_ada = ada_wgrad(c_all_t, dmod_sh, "ada_wgrad")
    g_w_ada_kv = ada_wgrad(c_all_t, dmkv_sh, "ada_kv_wgrad")[0]

    small_g = [jnp.stack([jnp.stack(r) for r in d_ln_g]), jnp.stack([jnp.stack(r) for r in d_ln_b]),
               jnp.stack(d_b_in), jnp.stack(d_gln_g), jnp.stack(d_gln_b), jnp.stack(d_rel), jnp.stack(d_ws),
               jnp.stack(d_bs)]
    sg_shapes = [a.shape for a in small_g]
    sg_pack = _pack_rows(small_g)
    sg_all = all_gather8(sg_pack, "ag_small_grads").reshape(N_DEV, -1, LANES)
    sg_sum = _unpack_rows(sum_leading(sg_all, "small_grad_sum"), sg_shapes)
    g_ln_g, g_ln_b, g_b_in, g_gln_g, g_gln_b, g_rel = [_shard_last(a, s_me) for a in sg_sum[:6]]
    g_ws, g_bs = sg_sum[6], sg_sum[7]

    last, _ = start_grad_exchange(list(made), sg_all)

    grads = dict(w_ada=g_w_ada, b_ada=g_b_ada, ln_g=g_ln_g, ln_b=g_ln_b, gmlp_b_in=g_b_in, gmlp_ln_g=g_gln_g,
                 gmlp_ln_b=g_gln_b, gmlp_w_s=g_ws, gmlp_b_s=g_bs, w_ada_kv=g_w_ada_kv, b_ada_kv=g_b_ada_kv,
                 attn_rel_bias=g_rel)
    weights = dict(w_ada=w_ada, b_ada=b_ada, ln_g=ln_g, ln_b=ln_b, ffn_gu=ffn_gu, ffn_down=ffn_down,
                   gmlp_w_in=gmlp_w_in, gmlp_b_in=gmlp_b_in, gmlp_ln_g=gmlp_ln_g, gmlp_ln_b=gmlp_ln_b,
                   gmlp_w_s=gmlp_w_s, gmlp_b_s=gmlp_b_s, gmlp_w_out=gmlp_w_out, w_ada_kv=w_ada_kv,
                   b_ada_kv=b_ada_kv, w_kv=w_kv, attn_w_q=attn_w_q, attn_rel_bias=attn_rel_bias, attn_w_o=attn_w_o)
    ms = dict(w_ada=m_w_ada, b_ada=m_b_ada, ln_g=m_ln_g, ln_b=m_ln_b, ffn_gu=m_ffn_gu, ffn_down=m_ffn_down,
              gmlp_w_in=m_gmlp_w_in, gmlp_b_in=m_gmlp_b_in, gmlp_ln_g=m_gmlp_ln_g, gmlp_ln_b=m_gmlp_ln_b,
              gmlp_w_s=m_gmlp_w_s, gmlp_b_s=m_gmlp_b_s, gmlp_w_out=m_gmlp_w_out, w_ada_kv=m_w_ada_kv,
              b_ada_kv=m_b_ada_kv, w_kv=m_w_kv, attn_w_q=m_attn_w_q, attn_rel_bias=m_attn_rel_bias,
              attn_w_o=m_attn_w_o)
    vs = dict(w_ada=v_w_ada, b_ada=v_b_ada, ln_g=v_ln_g, ln_b=v_ln_b, ffn_gu=v_ffn_gu, ffn_down=v_ffn_down,
              gmlp_w_in=v_gmlp_w_in, gmlp_b_in=v_gmlp_b_in, gmlp_ln_g=v_gmlp_ln_g, gmlp_ln_b=v_gmlp_ln_b,
              gmlp_w_s=v_gmlp_w_s, gmlp_b_s=v_gmlp_b_s, gmlp_w_out=v_gmlp_w_out, w_ada_kv=v_w_ada_kv,
              b_ada_kv=v_b_ada_kv, w_kv=v_w_kv, attn_w_q=v_attn_w_q, attn_rel_bias=v_attn_rel_bias,
              attn_w_o=v_attn_w_o)
    order = ["w_ada", "b_ada", "ln_g", "ln_b", "ffn_gu", "ffn_down", "gmlp_w_in", "gmlp_b_in", "gmlp_ln_g",
             "gmlp_ln_b", "gmlp_w_s", "gmlp_b_s", "gmlp_w_out", "w_ada_kv", "b_ada_kv", "w_kv", "attn_w_q",
             "attn_rel_bias", "attn_w_o"]
    big_names = ["w_ada", "w_ada_kv"] + stack_names
    small_names = [nm for nm in order if nm not in big_names]
    delta, new_m, new_v = {}, {}, {}

    def adamw_big(nm):
        shp = weights[nm].shape
        two_d = (-1, shp[-1])
        d, a, b = adamw(weights[nm].reshape(two_d), grads[nm].reshape(two_d), ms[nm].reshape(two_d),
                        vs[nm].reshape(two_d), "adamw")
        delta[nm], new_m[nm], new_v[nm] = d.reshape(shp), a.reshape(shp), b.reshape(shp)

    adamw_big("w_ada")
    adamw_big("w_ada_kv")
    shapes = [weights[nm].shape for nm in small_names]
    d, a, b = adamw(_pack_rows([weights[nm] for nm in small_names]), _pack_rows([grads[nm] for nm in small_names]),
                    _pack_rows([ms[nm] for nm in small_names]), _pack_rows([vs[nm] for nm in small_names]),
                    "adamw_small")
    for nm, dd, aa, bb in zip(small_names, _unpack_rows(d, shapes), _unpack_rows(a, shapes), _unpack_rows(b, shapes)):
        delta[nm], new_m[nm], new_v[nm] = dd, aa, bb

    def full_grad(u):
        lo = jnp.where(ci == 0, own_half[u], sib_half[u])
        hi = jnp.where(ci == 0, sib_half[u], own_half[u])
        return jnp.concatenate([lo, hi], axis=1 if u in col_split else 0)

    def adamw_stack(nm):
        si = stack_names.index(nm)
        g = jnp.stack([full_grad(unit_of[(nm, k)]) for k in range(stacks[si].shape[0])])
        grads[nm] = g.reshape(weights[nm].shape)
        adamw_big(nm)

    late = [stack_names[units[u][0]] for u in last["ids"]]
    early = [nm for nm in stack_names if nm not in late]
    finish_grad_exchange(pending, delta["w_ada"])
    collect_halves(delta["w_ada"])
    for nm in early:
        adamw_stack(nm)
    finish_grad_exchange(last, delta[early[-1]])
    collect_halves(delta[early[-1]])
    for nm in stack_names:
        if nm in late:
            adamw_stack(nm)

    return (loss, grad_x, *[grads[nm] for nm in order], *[delta[nm] for nm in order],
            *[new_m[nm] for nm in order], *[new_v[nm] for nm in order])
```

```python
import functools

import jax
import jax.numpy as jnp
from jax import lax
from jax.experimental import pallas as pl
from jax.experimental.pallas import tpu as pltpu

F32 = jnp.float32
BF16 = jnp.bfloat16
MESH = pl.DeviceIdType.MESH
HIGHEST = lax.Precision.HIGHEST

CHUNK = 64
GMLP_WINDOW = 128
GMLP_GROUPS = 8
HEAD_DIM = 64
LEFT_CHUNKS = 8
BAND = (LEFT_CHUNKS + 1) * CHUNK
LEFT_PAD = LEFT_CHUNKS * CHUNK
MAX_REL = 4 * CHUNK
N_REL = (CHUNK - 1) + MAX_REL + 1
LN_EPS = 1e-5
N_MOD = 9
N_DEV = 8
N_CHIP = 4

ADAM_LR = 0.001
ADAM_B1 = 0.9
ADAM_B2 = 0.999
ADAM_EPS = 1e-08
ADAM_WD = 0.01
ADAM_STEP = 10

LANES = 128
ROW_TILE = 256
MATMUL_ROW_TILE = 512
WGRAD_ROWS = 1024
ATTN_CHUNKS_PER_STEP = 4
VMEM_LIMIT_MB = 56

NT = (((1,), (1,)), ((), ()))
TN = (((0,), (0,)), ((), ()))

ANY = pl.BlockSpec(memory_space=pl.ANY)
VMEM_SPEC = pl.BlockSpec(memory_space=pltpu.VMEM)


def _params(semantics=None):
    kw = dict(vmem_limit_bytes=VMEM_LIMIT_MB * 1024 * 1024)
    if semantics is not None:
        kw["dimension_semantics"] = semantics
    return pltpu.CompilerParams(**kw)


def _sigmoid(v):
    return 0.5 * (1.0 + jnp.tanh(0.5 * v))


def _gelu(v):
    return 0.5 * v * (1.0 + lax.erf(v * (2.0 ** -0.5)))


def _gelu_grad(v):
    return 0.5 * (1.0 + lax.erf(v * (2.0 ** -0.5))) + v * jnp.exp(-0.5 * v * v) * ((2.0 * jnp.pi) ** -0.5)


def _row(m):
    return lambda i: (i, 0)


def _fixed2(i):
    return (0, 0)


def _fixed3(i):
    return (0, 0, 0)


def _resident(shape):
    return pl.BlockSpec(shape, _fixed2 if len(shape) == 2 else _fixed3, pipeline_mode=pl.Buffered(1))


def mod_matmul(x, scl, shift, w, bias, out_dtype, name):
    S, D = x.shape
    NS, _, n = w.shape
    tm = min(MATMUL_ROW_TILE, S)
    has_bias = bias is not None

    def body(*refs):
        if has_bias:
            x_ref, scl_ref, sh_ref, w_ref, b_ref, o_ref, h_ref = refs
        else:
            x_ref, scl_ref, sh_ref, w_ref, o_ref, h_ref = refs
        h = (x_ref[...] * (1.0 + scl_ref[...]) + sh_ref[...]).astype(BF16)
        h_ref[...] = h
        for s in range(NS):
            acc = jnp.dot(h, w_ref[s], preferred_element_type=F32)
            if has_bias:
                acc = acc + b_ref[:, s * n:(s + 1) * n]
            o_ref[:, s * n:(s + 1) * n] = acc.astype(out_dtype)

    in_specs = [pl.BlockSpec((tm, D), _row(0)), pl.BlockSpec((1, D), _fixed2), pl.BlockSpec((1, D), _fixed2),
                _resident((NS, D, n))]
    args = [x, scl, shift, w]
    if has_bias:
        in_specs.append(pl.BlockSpec((1, NS * n), _fixed2))
        args.append(bias)
    return pl.pallas_call(
        body, name=name, grid=(S // tm,), in_specs=in_specs,
        out_specs=[pl.BlockSpec((tm, NS * n), _row(0)), pl.BlockSpec((tm, D), _row(0))],
        out_shape=[jax.ShapeDtypeStruct((S, NS * n), out_dtype), jax.ShapeDtypeStruct((S, D), BF16)],
        compiler_params=_params(("parallel",)),
    )(*args)


def matmul_res_ln(a, w, x, gw, lg, lb, alpha, swiglu, name):
    S, D = x.shape
    K = w.shape[0]
    tm = min(ROW_TILE, S)
    ka = a.shape[1]

    def body(a_ref, w_ref, x_ref, gw_ref, lg_ref, lb_ref, xn_ref, xh_ref, rs_ref, y_ref, *act_ref):
        if swiglu:
            g = a_ref[:, :K].astype(F32)
            u = a_ref[:, K:].astype(F32)
            act = (g * _sigmoid(g) * u).astype(BF16)
            act_ref[0][...] = act
        else:
            act = a_ref[...].astype(BF16)
        y = jnp.dot(act, w_ref[...], preferred_element_type=F32)
        z = alpha * x_ref[...] + gw_ref[...] * y
        mu = jnp.mean(z, axis=-1, keepdims=True)
        zc = z - mu
        var = jnp.mean(zc * zc, axis=-1, keepdims=True)
        rstd = lax.rsqrt(var + LN_EPS)
        xhat = zc * rstd
        xn_ref[...] = xhat * lg_ref[...] + lb_ref[...]
        xh_ref[...] = xhat
        rs_ref[...] = rstd
        y_ref[...] = y.astype(BF16)

    vec = pl.BlockSpec((1, D), _fixed2)
    out_specs = [pl.BlockSpec((tm, D), _row(0)), pl.BlockSpec((tm, D), _row(0)), pl.BlockSpec((tm, 1), _row(0)),
                 pl.BlockSpec((tm, D), _row(0))]
    out_shape = [jax.ShapeDtypeStruct((S, D), F32), jax.ShapeDtypeStruct((S, D), F32),
                 jax.ShapeDtypeStruct((S, 1), F32), jax.ShapeDtypeStruct((S, D), BF16)]
    if swiglu:
        out_specs.append(pl.BlockSpec((tm, K), _row(0)))
        out_shape.append(jax.ShapeDtypeStruct((S, K), BF16))
    return pl.pallas_call(
        body, name=name, grid=(S // tm,),
        in_specs=[pl.BlockSpec((tm, ka), _row(0)), _resident((K, D)), pl.BlockSpec((tm, D), _row(0)),
                  vec, vec, vec],
        out_specs=out_specs, out_shape=out_shape,
        compiler_params=_params(("parallel",)),
    )(a, w, x, gw, lg, lb)


def _ln_res_bwd_tile(d, xh_ref, rs_ref, lg_ref, y_ref, gw_ref, wres, alpha, dxa_ref, dy_ref, acc_ref, row0):
    xh = xh_ref[...]
    dxh = d * lg_ref[...]
    m1 = jnp.mean(dxh, axis=-1, keepdims=True)
    m2 = jnp.mean(dxh * xh, axis=-1, keepdims=True)
    dz = rs_ref[...] * (dxh - m1 - xh * m2)
    dxa_ref[...] = alpha * dz
    dy_ref[...] = (gw_ref[...] * dz).astype(BF16)
    acc_ref[row0:row0 + 1, :] += jnp.sum(d * xh, axis=0, keepdims=True)
    acc_ref[row0 + 1:row0 + 2, :] += jnp.sum(d, axis=0, keepdims=True)
    acc_ref[row0 + 2:row0 + 3, :] += jnp.sum((wres * dz) * y_ref[...].astype(F32), axis=0, keepdims=True)


def ln_res_bwd(dxn, xhat, rstd, lg, y, gw, wres, alpha, name):
    S, D = dxn.shape
    tm = min(MATMUL_ROW_TILE, S)

    def body(dxn_ref, xh_ref, rs_ref, lg_ref, y_ref, gw_ref, dxa_ref, dy_ref, acc_ref):
        @pl.when(pl.program_id(0) == 0)
        def _():
            acc_ref[...] = jnp.zeros_like(acc_ref)

        _ln_res_bwd_tile(dxn_ref[...], xh_ref, rs_ref, lg_ref, y_ref, gw_ref, wres, alpha, dxa_ref, dy_ref, acc_ref, 0)

    vec = pl.BlockSpec((1, D), _fixed2)
    tile = pl.BlockSpec((tm, D), _row(0))
    return pl.pallas_call(
        body, name=name, grid=(S // tm,),
        in_specs=[tile, tile, pl.BlockSpec((tm, 1), _row(0)), vec, tile, vec],
        out_specs=[tile, tile, pl.BlockSpec((8, D), _fixed2)],
        out_shape=[jax.ShapeDtypeStruct((S, D), F32), jax.ShapeDtypeStruct((S, D), BF16),
                   jax.ShapeDtypeStruct((8, D), F32)],
        compiler_params=_params(("arbitrary",)),
    )(dxn, xhat, rstd, lg, y, gw)


def ffn_act_bwd(dy, wd, gu, after, name):
    S, D = dy.shape
    K = wd.shape[0]
    tm = min(ROW_TILE, S)

    def body(dy_ref, wd_ref, gu_ref, after_ref, o_ref):
        da = lax.dot_general(dy_ref[...], wd_ref[...], NT, preferred_element_type=F32).astype(BF16)
        g = gu_ref[:, :K]
        u = gu_ref[:, K:]
        sg = _sigmoid(g)
        o_ref[:, :K] = da * u * (sg * (1.0 + g * (1.0 - sg)))
        o_ref[:, K:] = da * (g * sg)

    return pl.pallas_call(
        body, name=name, grid=(S // tm,),
        in_specs=[pl.BlockSpec((tm, D), _row(0)), _resident((K, D)), pl.BlockSpec((tm, 2 * K), _row(0)), ANY],
        out_specs=pl.BlockSpec((tm, 2 * K), _row(0)),
        out_shape=jax.ShapeDtypeStruct((S, 2 * K), BF16),
        compiler_params=_params(("parallel",)),
    )(dy, wd, gu, after)


def matmul_nt(a, w, after, name):
    S, D = a.shape
    K = w.shape[0]
    tm = min(MATMUL_ROW_TILE, S)

    def body(a_ref, w_ref, after_ref, o_ref):
        o_ref[...] = lax.dot_general(a_ref[...], w_ref[...], NT, preferred_element_type=F32).astype(BF16)

    return pl.pallas_call(
        body, name=name, grid=(S // tm,),
        in_specs=[pl.BlockSpec((tm, D), _row(0)), _resident((K, D)), ANY],
        out_specs=pl.BlockSpec((tm, K), _row(0)),
        out_shape=jax.ShapeDtypeStruct((S, K), BF16),
        compiler_params=_params(("parallel",)),
    )(a, w, after)


def dgrad_mod(dpre, w, dxa, xin, scl, prev, alpha, name):
    S, D = xin.shape
    NS, _, n = w.shape
    tm = min(ROW_TILE, S)
    wres = prev[5] if prev is not None else None

    def body(*refs):
        dp_ref, w_ref, dxa_ref, xin_ref, scl_ref = refs[:5]
        acc_ref = refs[-1]

        @pl.when(pl.program_id(0) == 0)
        def _():
            acc_ref[...] = jnp.zeros_like(acc_ref)

        dh = jnp.zeros((tm, D), F32)
        for s in range(NS):
            dh = dh + lax.dot_general(dp_ref[:, s * n:(s + 1) * n].astype(BF16), w_ref[s], NT,
                                      preferred_element_type=F32)
        dx = dxa_ref[...] + dh * (1.0 + scl_ref[...])
        acc_ref[0:1, :] += jnp.sum(dh * xin_ref[...], axis=0, keepdims=True)
        acc_ref[1:2, :] += jnp.sum(dh, axis=0, keepdims=True)
        if prev is None:
            refs[5][...] = dx
        else:
            xh_ref, rs_ref, lg_ref, y_ref, gw_ref, pdxa_ref, pdy_ref = refs[5:12]
            _ln_res_bwd_tile(dx, xh_ref, rs_ref, lg_ref, y_ref, gw_ref, wres, alpha, pdxa_ref, pdy_ref, acc_ref, 2)

    tile = pl.BlockSpec((tm, D), _row(0))
    vec = pl.BlockSpec((1, D), _fixed2)
    in_specs = [pl.BlockSpec((tm, NS * n), _row(0)), _resident((NS, D, n)), tile, tile, vec]
    args = [dpre, w, dxa, xin, scl]
    if prev is None:
        out_specs = [tile]
        out_shape = [jax.ShapeDtypeStruct((S, D), F32)]
    else:
        in_specs += [tile, pl.BlockSpec((tm, 1), _row(0)), vec, tile, vec]
        args += list(prev[:5])
        out_specs = [tile, tile]
        out_shape = [jax.ShapeDtypeStruct((S, D), F32), jax.ShapeDtypeStruct((S, D), BF16)]
    return pl.pallas_call(
        body, name=name, grid=(S // tm,), in_specs=in_specs,
        out_specs=out_specs + [pl.BlockSpec((8, D), _fixed2)],
        out_shape=out_shape + [jax.ShapeDtypeStruct((8, D), F32)],
        compiler_params=_params(("arbitrary",)),
    )(*args)


def wgrad_pair(a, b, J, kb, nb, a_block, b_block, half_idx, name):
    S = a.shape[0]
    ts = min(WGRAD_ROWS, S)
    nsteps = S // ts

    def body(h_ref, a_ref, b_ref, o_ref, acc_ref, send_buf, recv_buf, send_sems, recv_sems):
        jj, si = pl.program_id(0), pl.program_id(1)
        x, y, c = _coords()
        j = lax.rem(jj, J)
        last = si == nsteps - 1

        def copy(blk):
            return pltpu.make_async_remote_copy(
                src_ref=send_buf.at[blk], dst_ref=recv_buf.at[blk], send_sem=send_sems.at[blk],
                recv_sem=recv_sems.at[blk], device_id=(x, y, 1 - c), device_id_type=MESH)

        @pl.when(si == 0)
        def _():
            acc_ref[...] = jnp.zeros_like(acc_ref)

        acc_ref[...] += lax.dot_general(a_ref[...], b_ref[...].astype(BF16), TN, preferred_element_type=F32)

        @pl.when(jnp.logical_and(last, jj < J))
        def _():
            send_buf[j] = acc_ref[...].astype(BF16)
            copy(j).start()

        @pl.when(jnp.logical_and(last, jj >= J))
        def _():
            copy(j).wait_recv()
            o_ref[...] = (acc_ref[...] + recv_buf[j].astype(F32)).astype(BF16)

        @pl.when(jnp.logical_and(last, jj == 2 * J - 1))
        def _():
            for blk in range(J):
                copy(blk).wait_send()

    def half(jj, h):
        return jnp.where(jj < J, 1 - h[0], h[0])

    return pl.pallas_call(
        body, name=name,
        grid_spec=pltpu.PrefetchScalarGridSpec(
            num_scalar_prefetch=1, grid=(2 * J, nsteps),
            in_specs=[pl.BlockSpec((ts, kb), lambda jj, s, h: (s, a_block(lax.rem(jj, J), half(jj, h)))),
                      pl.BlockSpec((ts, nb), lambda jj, s, h: (s, b_block(lax.rem(jj, J), half(jj, h))))],
            out_specs=pl.BlockSpec((None, kb, nb), lambda jj, s, h: (jnp.maximum(jj - J, 0), 0, 0)),
            scratch_shapes=[pltpu.VMEM((kb, nb), F32), pltpu.VMEM((J, kb, nb), BF16), pltpu.VMEM((J, kb, nb), BF16),
                            pltpu.SemaphoreType.DMA((J,)), pltpu.SemaphoreType.DMA((J,))]),
        out_shape=jax.ShapeDtypeStruct((J, kb, nb), BF16),
        compiler_params=_params(("arbitrary", "arbitrary")),
    )(half_idx, a, b)


def _window_mask():
    t = lax.broadcasted_iota(jnp.int32, (GMLP_WINDOW, GMLP_WINDOW), 0)
    s = lax.broadcasted_iota(jnp.int32, (GMLP_WINDOW, GMLP_WINDOW), 1)
    return ((s // CHUNK) <= (t // CHUNK)).astype(F32)


def sgu_fwd(pre, glg, glb, ws, bst, name):
    S, H2 = pre.shape
    H = H2 // 2
    W, G = GMLP_WINDOW, GMLP_GROUPS
    gd = H // G
    tm = min(ROW_TILE, S)

    def body(pre_ref, glg_ref, glb_ref, ws_ref, bst_ref, q_ref):
        u = _gelu(pre_ref[:, :H])
        v = _gelu(pre_ref[:, H:])
        mu = jnp.mean(v, axis=-1, keepdims=True)
        vc = v - mu
        var = jnp.mean(vc * vc, axis=-1, keepdims=True)
        vn = ((vc * lax.rsqrt(var + LN_EPS)) * glg_ref[...] + glb_ref[...]).astype(BF16)
        mask = _window_mask()
        for g in range(G):
            wsg = (ws_ref[g] * mask).astype(BF16)
            bcol = bst_ref[:, g:g + 1]
            for wi in range(tm // W):
                rows = slice(wi * W, (wi + 1) * W)
                cols = slice(g * gd, (g + 1) * gd)
                s = jnp.dot(wsg, vn[rows, cols], preferred_element_type=F32) + bcol
                q_ref[rows, cols] = (u[rows, cols] * s).astype(BF16)

    return pl.pallas_call(
        body, name=name, grid=(S // tm,),
        in_specs=[pl.BlockSpec((tm, H2), _row(0)), pl.BlockSpec((1, H), _fixed2), pl.BlockSpec((1, H), _fixed2),
                  pl.BlockSpec((G, W, W), _fixed3), pl.BlockSpec((W, G), _fixed2)],
        out_specs=pl.BlockSpec((tm, H), _row(0)),
        out_shape=jax.ShapeDtypeStruct((S, H), BF16),
        compiler_params=_params(("parallel",)),
    )(pre, glg, glb, ws, bst)


def sgu_bwd(dq, pre, glg, glb, ws, bst, name):
    S, H2 = pre.shape
    H = H2 // 2
    W, G = GMLP_WINDOW, GMLP_GROUPS
    gd = H // G
    tm = min(ROW_TILE, S)

    def body(dq_ref, pre_ref, glg_ref, glb_ref, ws_ref, bst_ref,
             dpre_ref, dws_ref, dss_ref, dgl_ref, dbin_ref, du_s, dvn_s):
        @pl.when(pl.program_id(0) == 0)
        def _():
            dws_ref[...] = jnp.zeros_like(dws_ref)
            dss_ref[...] = jnp.zeros_like(dss_ref)
            dgl_ref[...] = jnp.zeros_like(dgl_ref)
            dbin_ref[...] = jnp.zeros_like(dbin_ref)

        pu = pre_ref[:, :H]
        pv = pre_ref[:, H:]
        u = _gelu(pu)
        v = _gelu(pv)
        mu = jnp.mean(v, axis=-1, keepdims=True)
        vc = v - mu
        var = jnp.mean(vc * vc, axis=-1, keepdims=True)
        rstd = lax.rsqrt(var + LN_EPS)
        vhat = vc * rstd
        vn = (vhat * glg_ref[...] + glb_ref[...]).astype(BF16)
        mask = _window_mask()
        for g in range(G):
            wsg = (ws_ref[g] * mask).astype(BF16)
            bcol = bst_ref[:, g:g + 1]
            cols = slice(g * gd, (g + 1) * gd)
            for wi in range(tm // W):
                rows = slice(wi * W, (wi + 1) * W)
                vblk = vn[rows, cols]
                s = jnp.dot(wsg, vblk, preferred_element_type=F32) + bcol
                dqb = dq_ref[rows, cols].astype(F32)
                du_s[rows, cols] = dqb * s
                ds = dqb * u[rows, cols]
                dss_ref[:, cols] += ds
                dsb = ds.astype(BF16)
                dvn_s[rows, cols] = lax.dot_general(wsg, dsb, TN, preferred_element_type=F32)
                dws_ref[g] += lax.dot_general(dsb, vblk, NT, preferred_element_type=F32) * mask
        dvn = dvn_s[...]
        dgl_ref[0:1, :] += jnp.sum(dvn * vhat, axis=0, keepdims=True)
        dgl_ref[1:2, :] += jnp.sum(dvn, axis=0, keepdims=True)
        dvh = dvn * glg_ref[...]
        m1 = jnp.mean(dvh, axis=-1, keepdims=True)
        m2 = jnp.mean(dvh * vhat, axis=-1, keepdims=True)
        dv = rstd * (dvh - m1 - vhat * m2)
        dpu = du_s[...] * _gelu_grad(pu)
        dpv = dv * _gelu_grad(pv)
        dbin_ref[0:1, :H] += jnp.sum(dpu, axis=0, keepdims=True)
        dbin_ref[0:1, H:] += jnp.sum(dpv, axis=0, keepdims=True)
        dpre_ref[:, :H] = dpu.astype(BF16)
        dpre_ref[:, H:] = dpv.astype(BF16)

    return pl.pallas_call(
        body, name=name, grid=(S // tm,),
        in_specs=[pl.BlockSpec((tm, H), _row(0)), pl.BlockSpec((tm, H2), _row(0)), pl.BlockSpec((1, H), _fixed2),
                  pl.BlockSpec((1, H), _fixed2), pl.BlockSpec((G, W, W), _fixed3), pl.BlockSpec((W, G), _fixed2)],
        out_specs=[pl.BlockSpec((tm, H2), _row(0)), pl.BlockSpec((G, W, W), _fixed3), pl.BlockSpec((W, H), _fixed2),
                   pl.BlockSpec((8, H), _fixed2), pl.BlockSpec((8, H2), _fixed2)],
        out_shape=[jax.ShapeDtypeStruct((S, H2), BF16), jax.ShapeDtypeStruct((G, W, W), F32),
                   jax.ShapeDtypeStruct((W, H), F32), jax.ShapeDtypeStruct((8, H), F32),
                   jax.ShapeDtypeStruct((8, H2), F32)],
        scratch_shapes=[pltpu.VMEM((tm, H), F32), pltpu.VMEM((tm, H), F32)],
        compiler_params=_params(("arbitrary",)),
    )(dq, pre, glg, glb, ws, bst)


def group_lane_sum(dss, name):
    W, H = dss.shape
    gd = H // GMLP_GROUPS

    def body(d_ref, o_ref):
        j = lax.broadcasted_iota(jnp.int32, (H, LANES), 0)
        g = lax.broadcasted_iota(jnp.int32, (H, LANES), 1)
        ind = ((j // gd) == g).astype(F32)
        o_ref[...] = jnp.dot(d_ref[...], ind, preferred_element_type=F32, precision=HIGHEST)

    return pl.pallas_call(
        body, name=name, in_specs=[VMEM_SPEC], out_specs=VMEM_SPEC,
        out_shape=jax.ShapeDtypeStruct((W, LANES), F32), compiler_params=_params(),
    )(dss)


def _attn_load(j, cps, q_ref, k_ref, v_ref):
    r = lax.broadcasted_iota(jnp.int32, (CHUNK, BAND), 1)
    chunks = []
    for cc in range(cps):
        start = pl.multiple_of((j * cps + cc) * CHUNK, CHUNK)
        chunks.append((q_ref[cc * CHUNK:(cc + 1) * CHUNK, :], k_ref[pl.ds(start, BAND), :],
                       v_ref[pl.ds(start, BAND), :], (r + start) >= LEFT_PAD))
    return chunks


def _attn_probs(chunks, b_ref, sels, scale):
    qms = [[jnp.where(sel, q2, jnp.zeros_like(q2)) for sel in sels] for q2, _, _, _ in chunks]
    raw = [[lax.dot_general(qm, k2, NT, preferred_element_type=F32) for qm in qms[cc]]
           for cc, (_, k2, _, _) in enumerate(chunks)]
    probs = []
    for cc, (_, _, _, valid) in enumerate(chunks):
        row = []
        for sub in range(2):
            s = jnp.where(valid, raw[cc][sub] * scale + b_ref[sub], -jnp.inf)
            e = jnp.exp(s - jnp.max(s, axis=-1, keepdims=True))
            row.append(e / jnp.sum(e, axis=-1, keepdims=True))
        probs.append(row)
    return qms, probs


def attn_fwd(q, kpad, vpad, bias, name):
    S, D = q.shape
    HP = D // LANES
    cps = min(ATTN_CHUNKS_PER_STEP, S // CHUNK)
    tq = cps * CHUNK
    scale = HEAD_DIM ** -0.5

    def body(q_ref, k_ref, v_ref, b_ref, o_ref):
        sel0 = lax.broadcasted_iota(jnp.int32, (CHUNK, LANES), 1) < HEAD_DIM
        chunks = _attn_load(pl.program_id(1), cps, q_ref, k_ref, v_ref)
        _, probs = _attn_probs(chunks, b_ref, (sel0, jnp.logical_not(sel0)), scale)
        outs = [[jnp.dot(probs[cc][sub].astype(BF16), v2, preferred_element_type=F32) for sub in range(2)]
                for cc, (_, _, v2, _) in enumerate(chunks)]
        o_ref[...] = jnp.concatenate([jnp.where(sel0, o[0], o[1]) for o in outs], axis=0).astype(BF16)

    kv_spec = pl.BlockSpec((S + LEFT_PAD, LANES), lambda h, j: (0, h))
    return pl.pallas_call(
        body, name=name, grid=(HP, S // tq),
        in_specs=[pl.BlockSpec((tq, LANES), lambda h, j: (j, h)), kv_spec, kv_spec,
                  pl.BlockSpec((2, CHUNK, BAND), lambda h, j: (h, 0, 0))],
        out_specs=pl.BlockSpec((tq, LANES), lambda h, j: (j, h)),
        out_shape=jax.ShapeDtypeStruct((S, D), BF16),
        compiler_params=_params(("parallel", "parallel")),
    )(q, kpad, vpad, bias)


def attn_bwd(q, do, kpad, vpad, bias, dk_in, dv_in, name):
    S, D = q.shape
    HP = D // LANES
    NH = 2 * HP
    cps = min(ATTN_CHUNKS_PER_STEP, S // CHUNK)
    tq = cps * CHUNK
    nj = S // tq
    scale = HEAD_DIM ** -0.5

    def body(q_ref, do_ref, k_ref, v_ref, b_ref, dki_ref, dvi_ref, dq_ref, dk_ref, dv_ref, db_ref, dk_acc, dv_acc):
        j = pl.program_id(1)

        @pl.when(j == 0)
        def _():
            dk_acc[:LEFT_PAD, :] = jnp.zeros((LEFT_PAD, LANES), F32)
            dv_acc[:LEFT_PAD, :] = jnp.zeros((LEFT_PAD, LANES), F32)
            dk_acc[LEFT_PAD:, :] = dki_ref[...]
            dv_acc[LEFT_PAD:, :] = dvi_ref[...]
            db_ref[...] = jnp.zeros_like(db_ref)

        sel0 = lax.broadcasted_iota(jnp.int32, (CHUNK, LANES), 1) < HEAD_DIM
        sels = (sel0, jnp.logical_not(sel0))
        chunks = _attn_load(j, cps, q_ref, k_ref, v_ref)
        pairs = [(cc, sub) for cc in range(cps) for sub in range(2)]
        qms, probs = _attn_probs(chunks, b_ref, sels, scale)
        doms = [[jnp.where(sel, do_ref[cc * CHUNK:(cc + 1) * CHUNK, :], jnp.zeros((CHUNK, LANES), BF16))
                 for sel in sels] for cc in range(cps)]
        dps = {(cc, sub): lax.dot_general(doms[cc][sub], chunks[cc][2], NT, preferred_element_type=F32)
               for cc, sub in pairs}
        dss = {}
        for cc, sub in pairs:
            p = probs[cc][sub]
            dss[cc, sub] = p * (dps[cc, sub] - jnp.sum(dps[cc, sub] * p, axis=-1, keepdims=True))
        dsb = {key: ds.astype(BF16) for key, ds in dss.items()}
        dqs = {(cc, sub): jnp.dot(dsb[cc, sub], chunks[cc][1], preferred_element_type=F32) * scale
               for cc, sub in pairs}
        dks = {(cc, sub): lax.dot_general(dsb[cc, sub], qms[cc][sub], TN, preferred_element_type=F32) * scale
               for cc, sub in pairs}
        dvs = {(cc, sub): lax.dot_general(probs[cc][sub].astype(BF16), doms[cc][sub], TN,
                                          preferred_element_type=F32) for cc, sub in pairs}
        dq_ref[...] = jnp.concatenate([jnp.where(sel0, dqs[cc, 0], dqs[cc, 1]) for cc in range(cps)],
                                      axis=0).astype(BF16)
        for sub in range(2):
            total = dss[0, sub]
            for cc in range(1, cps):
                total = total + dss[cc, sub]
            db_ref[sub] += total
        dk_parts = [dks[cc, 0] + dks[cc, 1] for cc in range(cps)]
        dv_parts = [dvs[cc, 0] + dvs[cc, 1] for cc in range(cps)]

        def window(parts):
            blocks = []
            for rb in range(cps - 1 + BAND // CHUNK):
                acc = None
                for cc in range(cps):
                    b = rb - cc
                    if 0 <= b < BAND // CHUNK:
                        piece = parts[cc][b * CHUNK:(b + 1) * CHUNK, :]
                        acc = piece if acc is None else acc + piece
                blocks.append(acc)
            return jnp.concatenate(blocks, axis=0)

        span = pl.ds(pl.multiple_of(j * cps * CHUNK, CHUNK), (cps - 1) * CHUNK + BAND)
        dk_acc[span, :] += window(dk_parts)
        dv_acc[span, :] += window(dv_parts)

        @pl.when(j == nj - 1)
        def _():
            dk_ref[...] = dk_acc[LEFT_PAD:, :]
            dv_ref[...] = dv_acc[LEFT_PAD:, :]

    q_spec = pl.BlockSpec((tq, LANES), lambda h, j: (j, h))
    kv_spec = pl.BlockSpec((S + LEFT_PAD, LANES), lambda h, j: (0, h))
    col_spec = pl.BlockSpec((S, LANES), lambda h, j: (0, h))
    b_spec = pl.BlockSpec((2, CHUNK, BAND), lambda h, j: (h, 0, 0))
    return pl.pallas_call(
        body, name=name, grid=(HP, nj),
        in_specs=[q_spec, q_spec, kv_spec, kv_spec, b_spec, col_spec, col_spec],
        out_specs=[q_spec, col_spec, col_spec, b_spec],
        out_shape=[jax.ShapeDtypeStruct((S, D), BF16), jax.ShapeDtypeStruct((S, D), F32),
                   jax.ShapeDtypeStruct((S, D), F32), jax.ShapeDtypeStruct((NH, CHUNK, BAND), F32)],
        scratch_shapes=[pltpu.VMEM((S + LEFT_PAD, LANES), F32), pltpu.VMEM((S + LEFT_PAD, LANES), F32)],
        compiler_params=_params(("parallel", "arbitrary")),
    )(q, do, kpad, vpad, bias, dk_in, dv_in)


def _rel_onehot(t):
    r = lax.broadcasted_iota(jnp.int32, (BAND, N_REL), 0)
    i = lax.broadcasted_iota(jnp.int32, (BAND, N_REL), 1)
    idx = jnp.clip(t + LEFT_PAD - r, -(CHUNK - 1), MAX_REL) + (CHUNK - 1)
    return (idx == i).astype(F32)


def bias_expand(rb, name):
    NH = rb.shape[0]

    def body(rb_ref, o_ref):
        def step(t, carry):
            o_ref[t] = lax.dot_general(rb_ref[...], _rel_onehot(t), NT, preferred_element_type=F32,
                                       precision=HIGHEST)
            return carry

        lax.fori_loop(0, CHUNK, step, 0)

    return pl.pallas_call(
        body, name=name, in_specs=[VMEM_SPEC], out_specs=VMEM_SPEC,
        out_shape=jax.ShapeDtypeStruct((CHUNK, NH, BAND), F32), compiler_params=_params(),
    )(rb)


def bias_grad(dsum, name):
    NH = dsum.shape[1]

    def body(d_ref, o_ref):
        def step(t, acc):
            return acc + jnp.dot(d_ref[t], _rel_onehot(t), preferred_element_type=F32, precision=HIGHEST)

        o_ref[...] = lax.fori_loop(0, CHUNK, step, jnp.zeros((NH, N_REL), F32))

    return pl.pallas_call(
        body, name=name, in_specs=[VMEM_SPEC], out_specs=VMEM_SPEC,
        out_shape=jax.ShapeDtypeStruct((NH, N_REL), F32), compiler_params=_params(),
    )(dsum)


def loss_grad(y, tgt, name):
    S, D = y.shape
    tm = min(MATMUL_ROW_TILE, S)

    def body(y_ref, t_ref, d_ref, acc_ref):
        @pl.when(pl.program_id(0) == 0)
        def _():
            acc_ref[...] = jnp.zeros_like(acc_ref)

        err = y_ref[...] - t_ref[...]
        d_ref[...] = err * (1.0 / D)
        acc_ref[0:1, :] += jnp.sum(err * err, axis=0, keepdims=True)

    tile = pl.BlockSpec((tm, D), _row(0))
    return pl.pallas_call(
        body, name=name, grid=(S // tm,), in_specs=[tile, tile],
        out_specs=[tile, pl.BlockSpec((8, D), _fixed2)],
        out_shape=[jax.ShapeDtypeStruct((S, D), F32), jax.ShapeDtypeStruct((8, D), F32)],
        compiler_params=_params(("arbitrary",)),
    )(y, tgt)


def _col_tile(n):
    for t in (768, 512, 256, 128):
        if n % t == 0:
            return t
    return n


def ada_fwd(c_all, w, b, name):
    L, D, n = w.shape
    tn = _col_tile(n)

    def body(c_ref, w_ref, b_ref, o_ref):
        cv = c_ref[...]
        ca = cv * _sigmoid(cv)
        o_ref[...] = jnp.dot(ca, w_ref[...], preferred_element_type=F32, precision=HIGHEST) + b_ref[...]

    return pl.pallas_call(
        body, name=name, grid=(L, n // tn),
        in_specs=[pl.BlockSpec((N_DEV, D), lambda l, j: (0, 0)), pl.BlockSpec((None, D, tn), lambda l, j: (l, 0, j)),
                  pl.BlockSpec((None, 1, tn), lambda l, j: (l, 0, j))],
        out_specs=pl.BlockSpec((None, N_DEV, tn), lambda l, j: (l, 0, j)),
        out_shape=jax.ShapeDtypeStruct((L, N_DEV, n), F32),
        compiler_params=_params(("parallel", "parallel")),
    )(c_all, w, b)


def ada_wgrad(c_all_t, dmod, name):
    L, _, n = dmod.shape
    D = c_all_t.shape[0]
    tn = _col_tile(n)

    def body(c_ref, d_ref, o_ref):
        cv = c_ref[...]
        ca = cv * _sigmoid(cv)
        o_ref[...] = jnp.dot(ca, d_ref[...], preferred_element_type=F32, precision=HIGHEST)

    return pl.pallas_call(
        body, name=name, grid=(L, n // tn),
        in_specs=[pl.BlockSpec((D, N_DEV), lambda l, j: (0, 0)), pl.BlockSpec((None, N_DEV, tn), lambda l, j: (l, 0, j))],
        out_specs=pl.BlockSpec((None, D, tn), lambda l, j: (l, 0, j)),
        out_shape=jax.ShapeDtypeStruct((L, D, n), F32),
        compiler_params=_params(("parallel", "parallel")),
    )(c_all_t, dmod)


ELEMENTWISE_BLOCK_BYTES = 3 * 1024 * 1024


def _elementwise_rows(rows, row_bytes):
    for t in (4096, 2048, 1024, 512, 256, 128, 64, 32, 16):
        if rows % t == 0 and t * row_bytes <= ELEMENTWISE_BLOCK_BYTES:
            return t
    return rows


def sum_leading(a, name):
    n, M, N = a.shape
    tr = _elementwise_rows(M, n * N * 4)

    def body(a_ref, o_ref):
        acc = a_ref[0]
        for i in range(1, n):
            acc = acc + a_ref[i]
        o_ref[...] = acc

    return pl.pallas_call(
        body, name=name, grid=(M // tr,),
        in_specs=[pl.BlockSpec((n, tr, N), lambda i: (0, i, 0))],
        out_specs=pl.BlockSpec((tr, N), _row(0)),
        out_shape=jax.ShapeDtypeStruct((M, N), F32),
        compiler_params=_params(("parallel",)),
    )(a)


def chip_sum(psum, land, chip_idx, transposed, name):
    n, M, N = psum.shape
    tr = M if transposed else _elementwise_rows(M, N * 8)

    def body(s_ref, p_ref, a_ref, b_ref, c_ref, o_ref):
        total = ((p_ref[...].astype(F32) + a_ref[...].astype(F32)) + b_ref[...].astype(F32)) + c_ref[...].astype(F32)
        o_ref[...] = jnp.transpose(total) if transposed else total

    def entry(j):
        return pl.BlockSpec((None, tr, N), lambda i, s: ((s[0] + j) % n, i, 0))

    out_block, out_dims = ((N, tr), (N, M)) if transposed else ((tr, N), (M, N))
    return pl.pallas_call(
        body, name=name,
        grid_spec=pltpu.PrefetchScalarGridSpec(
            num_scalar_prefetch=1, grid=(M // tr,),
            in_specs=[entry(0), entry(1), entry(2), entry(3)],
            out_specs=pl.BlockSpec(out_block, lambda i, s: (0, 0) if transposed else (i, 0))),
        out_shape=jax.ShapeDtypeStruct(out_dims, F32),
        compiler_params=_params(("parallel",)),
    )(chip_idx, psum, land, land, land)


def adamw(w, g, m, v, name):
    M, N = w.shape
    tr = _elementwise_rows(M, N * 4)
    c1 = 1.0 - ADAM_B1 ** ADAM_STEP
    c2 = 1.0 - ADAM_B2 ** ADAM_STEP

    def body(w_ref, g_ref, m_ref, v_ref, d_ref, nm_ref, nv_ref):
        g = g_ref[...]
        nm = ADAM_B1 * m_ref[...] + (1.0 - ADAM_B1) * g
        nv = ADAM_B2 * v_ref[...] + (1.0 - ADAM_B2) * (g * g)
        d_ref[...] = -ADAM_LR * ((nm / c1) / (jnp.sqrt(nv / c2) + ADAM_EPS) + ADAM_WD * w_ref[...])
        nm_ref[...] = nm
        nv_ref[...] = nv

    spec = pl.BlockSpec((tr, N), _row(0))
    shp = jax.ShapeDtypeStruct((M, N), F32)
    return pl.pallas_call(
        body, name=name, grid=(M // tr,), in_specs=[spec] * 4, out_specs=[spec] * 3, out_shape=[shp] * 3,
        compiler_params=_params(("parallel",)),
    )(w, g, m, v)


def _coords():
    return lax.axis_index("x"), lax.axis_index("y"), lax.axis_index("c")


def all_gather8(block, name):
    m_per, n = block.shape

    def body(x_ref, out_ref, send_sems, recv_sems, local_sem):
        x, y, c = _coords()
        me, sibling = (x, y, c), (x, y, 1 - c)
        chips = [(1 - x, y), (x, 1 - y), (1 - x, 1 - y)]

        def rows(px, py, pc):
            return out_ref.at[pl.ds((4 * px + 2 * py + pc) * m_per, m_per), :]

        def copy(k, blk, to, src=None):
            return pltpu.make_async_remote_copy(
                src_ref=rows(*blk) if src is None else src, dst_ref=rows(*blk),
                send_sem=send_sems.at[k], recv_sem=recv_sems.at[k], device_id=to, device_id_type=MESH)

        mine = pltpu.make_async_copy(x_ref, rows(*me), local_sem)
        mine.start()
        first = [copy(0, me, sibling, src=x_ref)]
        first += [copy(1 + j, me, (*chip, c), src=x_ref) for j, chip in enumerate(chips)]
        for cp in first:
            cp.start()
        passed = [copy(4 + j, (*chip, c), sibling) for j, chip in enumerate(chips)]
        for j, chip in enumerate(chips):
            copy(1 + j, (*chip, c), me).wait_recv()
            passed[j].start()
        copy(0, sibling, me).wait_recv()
        for j, chip in enumerate(chips):
            copy(4 + j, (*chip, 1 - c), me).wait_recv()
        for cp in first + passed:
            cp.wait_send()
        mine.wait()

    return pl.pallas_call(
        body, name=name, in_specs=[VMEM_SPEC], out_specs=VMEM_SPEC,
        out_shape=jax.ShapeDtypeStruct((N_DEV * m_per, n), block.dtype),
        scratch_shapes=[pltpu.SemaphoreType.DMA((7,)), pltpu.SemaphoreType.DMA((7,)), pltpu.SemaphoreType.DMA],
        compiler_params=_params(),
    )(block)


def _other_chips(x, y):
    return [(1 - x, y), (x, 1 - y), (1 - x, 1 - y)]


HBM_SPEC = pl.BlockSpec(memory_space=pltpu.HBM)
SEM_SPEC = pl.BlockSpec(memory_space=pltpu.SEMAPHORE)
DATAFLOW = pltpu.SideEffectType.DATAFLOW_SIDE_EFFECTING


def _chip_peers(x, y, c):
    return [(px, py, c) for px, py in _other_chips(x, y)]


def _sibling_peer(x, y, c):
    return [(x, y, 1 - c)]


def _weight_desc(_, k, land_ref, peer, me):
    h = land_ref.shape[1] // 2
    rows = pl.ds(me[2] * h, h)
    mine = land_ref.at[2 * me[0] + me[1], rows, :]
    return mine, mine, land_ref.at[2 * peer[0] + peer[1], rows, :]


def _grad_desc(psum_ref, k, land_ref, peer, me):
    return psum_ref.at[2 * peer[0] + peer[1]], land_ref.at[2 * me[0] + me[1]], land_ref.at[2 * peer[0] + peer[1]]


def _pair_desc(grad_ref, k, land_ref, peer, me):
    h = land_ref.shape[1]
    return grad_ref.at[:, pl.ds(peer[2] * h, h), :], land_ref, land_ref


def _whole_desc(src_ref, k, land_ref, peer, me):
    return src_ref, land_ref, land_ref


def exchange_start(srcs, lands, units, groups, desc, peers, after, name):
    n_s, n_l, n_g = len(srcs), len(lands), len(groups)
    n_p = len(peers(0, 0, 0))

    def body(*refs):
        s_refs, l_refs = refs[:n_s], refs[n_s:n_s + n_l]
        outs = refs[n_s + n_l + 1:]
        sems, token = outs[:2 * n_g], outs[-1]
        me = _coords()
        for g, ids in enumerate(groups):
            for i, u in enumerate(ids):
                si, k = units[u]
                for j, peer in enumerate(peers(*me)):
                    src, dst, _ = desc(s_refs[si] if s_refs else None, k, l_refs[u], peer, me)
                    pltpu.make_async_remote_copy(
                        src_ref=src, dst_ref=dst, send_sem=sems[2 * g].at[n_p * i + j],
                        recv_sem=sems[2 * g + 1].at[n_p * i + j], device_id=peer, device_id_type=MESH).start()
        token[...] = jnp.zeros_like(token)

    arrs = list(srcs) + list(lands)
    sem_shapes = [pltpu.SemaphoreType.DMA((n_p * len(ids),)) for ids in groups for _ in range(2)]
    outs = pl.pallas_call(
        body, name=name,
        in_specs=[HBM_SPEC] * len(arrs) + [ANY],
        out_specs=[SEM_SPEC] * (2 * n_g) + [HBM_SPEC] * len(arrs) + [VMEM_SPEC],
        out_shape=sem_shapes + [pltpu.HBM(a.shape, a.dtype) for a in arrs] + [jax.ShapeDtypeStruct((8, LANES), F32)],
        input_output_aliases={i: 2 * n_g + i for i in range(len(arrs))},
        compiler_params=pltpu.CompilerParams(has_side_effects=DATAFLOW),
    )(*[pltpu.with_memory_space_constraint(a, pltpu.HBM) for a in arrs], after)
    sems = outs[:2 * n_g]
    thru = outs[2 * n_g:2 * n_g + len(arrs)]
    return sems, list(thru[:n_s]), list(thru[n_s:]), outs[-1]


def exchange_wait(srcs, lands, units, send_sem, recv_sem, desc, peers, after, name):
    n_s, n_l = len(srcs), len(lands)
    n_p = len(peers(0, 0, 0))

    def body(*refs):
        s_refs, l_refs = refs[:n_s], refs[n_s:n_s + n_l]
        send_sems, recv_sems = refs[n_s + n_l], refs[n_s + n_l + 1]
        me = _coords()
        for i, (si, k) in enumerate(units):
            for j, peer in enumerate(peers(*me)):
                src, _, mine = desc(s_refs[si] if s_refs else None, k, l_refs[i], peer, me)
                cp = pltpu.make_async_remote_copy(
                    src_ref=src, dst_ref=mine, send_sem=send_sems.at[n_p * i + j], recv_sem=recv_sems.at[n_p * i + j],
                    device_id=peer, device_id_type=MESH)
                cp.wait_send()
                cp.wait_recv()

    arrs = list(srcs) + list(lands)
    outs = pl.pallas_call(
        body, name=name,
        in_specs=[HBM_SPEC] * len(arrs) + [SEM_SPEC, SEM_SPEC, ANY],
        out_specs=[HBM_SPEC] * len(arrs),
        out_shape=[pltpu.HBM(a.shape, a.dtype) for a in arrs],
        input_output_aliases={i: i for i in range(len(arrs))},
        compiler_params=pltpu.CompilerParams(has_side_effects=DATAFLOW),
    )(*arrs, send_sem, recv_sem, after)
    return list(outs[:n_s]), list(outs[n_s:])


def sibling_fill(lands, name):
    n_u = len(lands)

    def body(*refs):
        ins, outs = refs[:n_u], refs[n_u:2 * n_u]
        send_sems, recv_sems = refs[2 * n_u:]
        x, y, c = _coords()
        sends = []
        for u in range(n_u):
            h = ins[u].shape[1] // 2
            for j, (px, py) in enumerate(_other_chips(x, y)):
                part = (2 * px + py, pl.ds(c * h, h), slice(None))
                cp = pltpu.make_async_remote_copy(
                    src_ref=ins[u].at[part], dst_ref=outs[u].at[part], send_sem=send_sems.at[3 * u + j],
                    recv_sem=recv_sems.at[3 * u + j], device_id=(x, y, 1 - c), device_id_type=MESH)
                cp.start()
                sends.append(cp)
        for u in range(n_u):
            h = ins[u].shape[1] // 2
            for j, (px, py) in enumerate(_other_chips(x, y)):
                theirs = (2 * px + py, pl.ds((1 - c) * h, h), slice(None))
                pltpu.make_async_remote_copy(
                    src_ref=ins[u].at[theirs], dst_ref=outs[u].at[theirs], send_sem=send_sems.at[3 * u + j],
                    recv_sem=recv_sems.at[3 * u + j], device_id=(x, y, 1 - c), device_id_type=MESH).wait_recv()
        for cp in sends:
            cp.wait_send()

    return pl.pallas_call(
        body, name=name, in_specs=[ANY] * n_u, out_specs=[ANY] * n_u,
        out_shape=[jax.ShapeDtypeStruct(a.shape, a.dtype) for a in lands],
        input_output_aliases={i: i for i in range(n_u)},
        scratch_shapes=[pltpu.SemaphoreType.DMA((3 * n_u,)), pltpu.SemaphoreType.DMA((3 * n_u,))],
        compiler_params=_params(),
    )(*lands)


def _pack_rows(parts):
    flat = jnp.concatenate([p.reshape(-1).astype(F32) for p in parts])
    n = flat.shape[0]
    padded = -(-n // (8 * LANES)) * (8 * LANES)
    return jnp.pad(flat, (0, padded - n)).reshape(-1, LANES)


def _unpack_rows(packed, shapes):
    flat = packed.reshape(-1)
    out, off = [], 0
    for s in shapes:
        size = 1
        for d in s:
            size *= d
        out.append(flat[off:off + size].reshape(s))
        off += size
    return out


def _shard_last(full, s_me):
    n = full.shape[-1] // N_CHIP
    return lax.dynamic_slice_in_dim(full, s_me * n, n, axis=full.ndim - 1)


def _unshard_last(g):
    moved = jnp.moveaxis(g, 0, -2)
    return moved.reshape(moved.shape[:-2] + (moved.shape[-2] * moved.shape[-1],))


def kernel(x, c, w_ada, b_ada, ln_g, ln_b, ffn_gu, ffn_down, gmlp_w_in, gmlp_b_in, gmlp_ln_g, gmlp_ln_b, gmlp_w_s, gmlp_b_s, gmlp_w_out, w_ada_kv, b_ada_kv, w_kv, attn_w_q, attn_rel_bias, attn_w_o, loss_target, m_w_ada, m_b_ada, m_ln_g, m_ln_b, m_ffn_gu, m_ffn_down, m_gmlp_w_in, m_gmlp_b_in, m_gmlp_ln_g, m_gmlp_ln_b, m_gmlp_w_s, m_gmlp_b_s, m_gmlp_w_out, m_w_ada_kv, m_b_ada_kv, m_w_kv, m_attn_w_q, m_attn_rel_bias, m_attn_w_o, v_w_ada, v_b_ada, v_ln_g, v_ln_b, v_ffn_gu, v_ffn_down, v_gmlp_w_in, v_gmlp_b_in, v_gmlp_ln_g, v_gmlp_ln_b, v_gmlp_w_s, v_gmlp_b_s, v_gmlp_w_out, v_w_ada_kv, v_b_ada_kv, v_w_kv, v_attn_w_q, v_attn_rel_bias, v_attn_w_o):
    xi, yi, ci = _coords()
    s_me = 2 * xi + yi
    dev = 4 * xi + 2 * yi + ci

    x0 = x[0]
    tgt = loss_target[0]
    S, D = x0.shape
    L = w_ada.shape[0]
    NA = gmlp_w_in.shape[0]
    NB = attn_w_q.shape[0]
    NH = D // HEAD_DIM
    alpha = (2.0 * L) ** 0.25
    n_ada = w_ada.shape[2]
    n_kv = w_ada_kv.shape[1]

    stack_names = ["ffn_gu", "ffn_down", "gmlp_w_in", "gmlp_w_out", "w_kv", "attn_w_q", "attn_w_o"]
    stack_src = dict(ffn_gu=ffn_gu, ffn_down=ffn_down, gmlp_w_in=gmlp_w_in, gmlp_w_out=gmlp_w_out, w_kv=w_kv[None],
                     attn_w_q=attn_w_q, attn_w_o=attn_w_o)
    stacks = [stack_src[nm].astype(BF16).reshape((-1,) + stack_src[nm].shape[-2:]) for nm in stack_names]
    units = [(si, k) for si, st in enumerate(stacks) for k in range(st.shape[0])]
    unit_of = {(stack_names[si], k): u for u, (si, k) in enumerate(units)}
    assert NA >= 1
    weight_groups = [[("ffn_gu", 0)], [("ffn_down", 0)],
                     [("gmlp_w_in", 0), ("gmlp_w_out", 0), ("ffn_gu", 1), ("ffn_down", 1)]]
    for l in range(1, L):
        names = [("w_kv", 0)] if l == NA else []
        names += [("ffn_gu", 2 * l), ("ffn_down", 2 * l)]
        names += [("gmlp_w_in", l), ("gmlp_w_out", l)] if l < NA else [("attn_w_q", l - NA), ("attn_w_o", l - NA)]
        names += [("ffn_gu", 2 * l + 1), ("ffn_down", 2 * l + 1)]
        weight_groups.append(names)
    weight_groups = [[unit_of[n] for n in names] for names in weight_groups]

    c_all = all_gather8(jnp.broadcast_to(c, (8, D)), "ag_c").reshape(N_DEV, 8, D)[:, 0]
    b_ada_sh = lax.dynamic_slice_in_dim(b_ada, s_me * n_ada, n_ada, axis=1)
    b_kv_sh = lax.dynamic_slice_in_dim(b_ada_kv, s_me * n_kv, n_kv, axis=0)
    mod_part = ada_fwd(c_all, w_ada, b_ada_sh[:, None, :], "ada_fwd")
    mkv_part = ada_fwd(c_all, w_ada_kv[None], b_kv_sh[None, None, :], "ada_kv_fwd")
    part = jnp.concatenate([jnp.transpose(mod_part, (1, 0, 2)).reshape(N_DEV, L * n_ada), mkv_part[0]], axis=1)
    width = part.shape[1]
    pad_w = -(-width // LANES) * LANES - width
    all_part = all_gather8(jnp.pad(part, ((0, 0), (0, pad_w))), "ag_mod").reshape(N_DEV, N_DEV, width + pad_w)
    mine = lax.dynamic_index_in_dim(all_part[0::2], dev, axis=1, keepdims=False)
    mod = jnp.transpose(mine[:, :L * n_ada].reshape(N_CHIP, L, n_ada), (1, 0, 2)).reshape(L, N_MOD, D)
    mkv = mine[:, L * n_ada:width].reshape(2, D)

    def mrow(l, k):
        return mod[l, k][None, :]

    small_shapes = [ln_g.shape, ln_b.shape, gmlp_b_in.shape, gmlp_ln_g.shape, gmlp_ln_b.shape, attn_rel_bias.shape]
    small_pack = _pack_rows([ln_g, ln_b, gmlp_b_in, gmlp_ln_g, gmlp_ln_b, attn_rel_bias])
    small_all = all_gather8(small_pack, "ag_small_params").reshape((N_DEV,) + small_pack.shape)[0::2]
    sm = [_unpack_rows(small_all[s], small_shapes) for s in range(N_CHIP)]
    ln_g_f, ln_b_f, b_in_f, gln_g_f, gln_b_f, rel_f = [
        _unshard_last(jnp.stack([sm[s][i] for s in range(N_CHIP)])) for i in range(len(small_shapes))]

    lands0 = [lax.dynamic_update_slice(lax.empty((N_CHIP,) + stacks[si].shape[1:], BF16), stacks[si][k][None],
                                       (s_me, 0, 0)) for si, k in units]
    gathers_done = jnp.concatenate([mod.reshape(-1)[:LANES], small_all.reshape(-1)[:LANES]])
    w_sems, _, lands_t, _ = exchange_start([], lands0, units, weight_groups, _weight_desc, _chip_peers, gathers_done,
                                           "weight_send_start")
    wg = {}

    def fetch_weights(g, after):
        ids = weight_groups[g]
        _, got = exchange_wait([], [lands_t[u] for u in ids], [units[u] for u in ids], w_sems[2 * g],
                               w_sems[2 * g + 1], _weight_desc, _chip_peers, after, "weight_send_wait_%d" % g)
        for u, a in zip(ids, sibling_fill(got, "weight_sibling_fill")):
            wg[u] = a

    def W(nm, k):
        return wg[unit_of[(nm, k)]]

    def Wrows(nm, k):
        w4 = W(nm, k)
        return w4.reshape(w4.shape[0] * w4.shape[1], w4.shape[2])

    bst = [jnp.transpose(gmlp_b_s[j]) for j in range(NA)]
    biases = {}

    def make_bias(j, dep):
        rel, _ = lax.optimization_barrier((rel_f[j], dep))
        biases[j] = jnp.transpose(bias_expand(rel, "bias_expand"), (1, 0, 2))
        return biases[j]

    saved = []
    xc = x0
    kpad = vpad = xkv = None
    for l in range(L):
        after = mod if l == 0 else xc
        if l == 1 and NB > 1:
            after = make_bias(1, xc)
        fetch_weights(0 if l == 0 else l + 2, after)
        if l == NA:
            xkv = xc
            kv, hkv = mod_matmul(xc, mkv[1][None], mkv[0][None], W("w_kv", 0), None, BF16, "kv_proj")
            kpad = jnp.pad(kv[:, :D], ((LEFT_PAD, 0), (0, 0)))
            vpad = jnp.pad(kv[:, D:], ((LEFT_PAD, 0), (0, 0)))
        sv = {}
        for i in (0, 2):
            k = 2 * l + i // 2
            gu, hv = mod_matmul(xc, mrow(l, 3 * i + 1), mrow(l, 3 * i), W("ffn_gu", k), None, BF16, "ffn_up")
            if l == 0 and i == 0:
                fetch_weights(1, hv)
            gw = 0.5 * (1.0 + mrow(l, 3 * i + 2))
            xn, xh, rs, yv, av = matmul_res_ln(gu, Wrows("ffn_down", k), xc, gw, ln_g_f[l, i][None],
                                               ln_b_f[l, i][None], alpha, True, "ffn_down")
            sv[i] = dict(x=xc, h=hv, gu=gu, a=av, xh=xh, rs=rs, y=yv, gw=gw)
            xc = xn
            if i == 0:
                if l == 0:
                    fetch_weights(2, make_bias(0, xc) if NB > 0 else xc)
                gw = 1.0 + mrow(l, 5)
                if l < NA:
                    pre, hv = mod_matmul(xc, mrow(l, 4), mrow(l, 3), W("gmlp_w_in", l), b_in_f[l][None], F32,
                                         "gmlp_in")
                    qv = sgu_fwd(pre, gln_g_f[l][None], gln_b_f[l][None], gmlp_w_s[l], bst[l], "sgu_fwd")
                    xn, xh, rs, yv = matmul_res_ln(qv, Wrows("gmlp_w_out", l), xc, gw, ln_g_f[l, 1][None],
                                                   ln_b_f[l, 1][None], alpha, False, "gmlp_out")
                    sv[1] = dict(x=xc, h=hv, pre=pre, a=qv, xh=xh, rs=rs, y=yv, gw=gw)
                else:
                    j = l - NA
                    if j not in biases:
                        make_bias(j, xc)
                    qh, hv = mod_matmul(xc, mrow(l, 4), mrow(l, 3), Wrows("attn_w_q", j)[None], None, BF16, "attn_q")
                    ov = attn_fwd(qh, kpad, vpad, biases[j], "attn_fwd")
                    xn, xh, rs, yv = matmul_res_ln(ov, Wrows("attn_w_o", j), xc, gw, ln_g_f[l, 1][None],
                                                   ln_b_f[l, 1][None], alpha, False, "attn_out")
                    sv[1] = dict(x=xc, h=hv, q=qh, a=ov, xh=xh, rs=rs, y=yv, gw=gw)
                xc = xn
        saved.append(sv)

    dx, lacc = loss_grad(xc, tgt, "loss_grad")
    loss = lax.psum((0.5 / D) * jnp.sum(lacc[0]), ("x", "y", "c"))

    gpair = [None] * len(units)
    col_split = {u for u, (si, _) in enumerate(units) if stack_names[si] != "ffn_gu"}
    dmod = [[None] * N_MOD for _ in range(L)]
    d_ln_g = [[None] * 3 for _ in range(L)]
    d_ln_b = [[None] * 3 for _ in range(L)]
    d_b_in, d_gln_g, d_gln_b, d_ws, d_bs, d_rel = ([None] * NA, [None] * NA, [None] * NA, [None] * NA, [None] * NA,
                                                  [None] * NB)
    dk = jnp.zeros((S, D), F32)
    dv = jnp.zeros((S, D), F32)
    dmkv = None

    made = []

    core_idx = ci.astype(jnp.int32).reshape(1)
    chip_idx = s_me.astype(jnp.int32).reshape(1)

    def put(nm, k, a, b, name):
        u = unit_of[(nm, k)]
        rows, cols = stacks[units[u][0]].shape[1:]
        if nm == "ffn_gu":
            g = wgrad_pair(b, a, N_CHIP, cols, rows // 2, lambda j, p: j, lambda j, p: p, core_idx, name)
        elif nm in ("gmlp_w_in", "w_kv"):
            g = wgrad_pair(a, b, N_CHIP, rows, cols // 2, lambda j, p: 0, lambda j, p: 2 * j + p, core_idx, name)
        else:
            g = wgrad_pair(a, b, 1, N_CHIP * rows, cols // 2, lambda j, p: 0, lambda j, p: p, core_idx, name)
        gpair[u] = g.reshape((N_CHIP, -1, g.shape[-1]))
        made.append(u)

    own_half, sib_half = {}, {}
    n_started = [0]

    def start_grad_exchange(ids, after):
        psums = [gpair[u] for u in ids]
        n = len(ids)
        tag = n_started[0]
        n_started[0] += 1
        sems, ps_t, q_t, token = exchange_start(psums, [lax.empty(p.shape, p.dtype) for p in psums],
                                                [(i, 0) for i in range(n)], [list(range(n))], _grad_desc, _chip_peers,
                                                after, "grad_send_start_%d" % tag)
        return dict(ids=ids, tag=tag, sems=sems, ps=ps_t, q=q_t), token

    def finish_grad_exchange(pend, after):
        n = len(pend["ids"])
        ps_t, q = exchange_wait(pend["ps"], pend["q"], [(i, 0) for i in range(n)], pend["sems"][0], pend["sems"][1],
                                _grad_desc, _chip_peers, after, "grad_send_wait_%d" % pend["tag"])
        halves = [chip_sum(ps_t[i], q[i], chip_idx, u not in col_split, "grad_chip_sum")
                  for i, u in enumerate(pend["ids"])]
        sems, h_t, land_t, token = exchange_start(halves, [lax.empty(h.shape, h.dtype) for h in halves],
                                              [(i, 0) for i in range(n)], [list(range(n))], _whole_desc,
                                              _sibling_peer, halves[0], "half_send_start_%d" % pend["tag"])
        swaps.append(dict(ids=pend["ids"], tag=pend["tag"], sems=sems, h=h_t, land=land_t))
        return token

    def collect_halves(after):
        for sw in swaps:
            n = len(sw["ids"])
            h, land = exchange_wait(sw["h"], sw["land"], [(i, 0) for i in range(n)], sw["sems"][0], sw["sems"][1],
                                    _whole_desc, _sibling_peer, after, "half_send_wait_%d" % sw["tag"])
            for u, mine, theirs in zip(sw["ids"], h, land):
                own_half[u], sib_half[u] = mine, theirs
        swaps.clear()

    swaps = []
    pending = None
    started_before = jnp.zeros((8, LANES), F32)

    def ln_inputs(l, i):
        t = saved[l][i]
        return (t["xh"], t["rs"], ln_g_f[l, i][None], t["y"], t["gw"], 1.0 if i == 1 else 0.5)

    def record_ln(l, i, acc, row0):
        d_ln_g[l][i], d_ln_b[l][i], dmod[l][3 * i + 2] = acc[row0], acc[row0 + 1], acc[row0 + 2]

    ln_done = None
    for l in reversed(range(L)):
        if l == NA - 1:
            dkv = jnp.concatenate([dk, dv], axis=1)
            put("w_kv", 0, hkv, dkv, "kv_wgrad")
            pdxa, pdy, acc = dgrad_mod(dkv, W("w_kv", 0), dx, xkv, mkv[1][None], ln_inputs(l, 2), alpha, "kv_dgrad")
            dmkv = jnp.stack([acc[1], acc[0]])
            record_ln(l, 2, acc, 2)
            ln_done = (pdxa, pdy)
        sv = saved[l]
        for i in (2, 1, 0):
            t = sv[i]
            if ln_done is None:
                dxa, dy, acc1 = ln_res_bwd(dx, *ln_inputs(l, i), alpha, "ln_res_bwd")
                record_ln(l, i, acc1, 0)
            else:
                dxa, dy = ln_done
                ln_done = None
            before = (l, i - 1) if i > 0 else ((l - 1, 2) if l > 0 and l != NA else None)
            prev = ln_inputs(*before) if before is not None else None
            scl = mrow(l, 3 * i + 1)
            if i != 1:
                k = 2 * l + i // 2
                F = t["gu"].shape[1] // 2
                dgu = ffn_act_bwd(dy, Wrows("ffn_down", k), t["gu"], started_before, "ffn_act_bwd")
                put("ffn_down", k, t["a"], dy, "ffn_down_wgrad")
                put("ffn_gu", k, t["h"], dgu, "ffn_up_wgrad")
                res = dgrad_mod(dgu, W("ffn_gu", k), dxa, t["x"], scl, prev, alpha, "ffn_up_dgrad")
            elif l < NA:
                dq = matmul_nt(dy, Wrows("gmlp_w_out", l), started_before, "gmlp_out_dgrad")
                put("gmlp_w_out", l, t["a"], dy, "gmlp_out_wgrad")
                dpre, dws_l, dss, dgl, dbin = sgu_bwd(dq, t["pre"], gln_g_f[l][None], gln_b_f[l][None], gmlp_w_s[l],
                                                      bst[l], "sgu_bwd")
                d_ws[l] = dws_l
                d_bs[l] = jnp.transpose(group_lane_sum(dss, "sgu_bias_grad")[:, :GMLP_GROUPS])
                d_gln_g[l], d_gln_b[l], d_b_in[l] = dgl[0], dgl[1], dbin[0]
                put("gmlp_w_in", l, t["h"], dpre, "gmlp_in_wgrad")
                res = dgrad_mod(dpre, W("gmlp_w_in", l), dxa, t["x"], scl, prev, alpha, "gmlp_in_dgrad")
            else:
                j = l - NA
                do = matmul_nt(dy, Wrows("attn_w_o", j), started_before, "attn_out_dgrad")
                put("attn_w_o", j, t["a"], dy, "attn_out_wgrad")
                dqh, dk, dv, dbias = attn_bwd(t["q"], do, kpad, vpad, biases[j], dk, dv, "attn_bwd")
                d_rel[j] = bias_grad(jnp.transpose(dbias, (1, 0, 2)), "bias_grad")
                put("attn_w_q", j, t["h"], dqh, "attn_q_wgrad")
                res = dgrad_mod(dqh, Wrows("attn_w_q", j)[None], dxa, t["x"], scl, prev, alpha, "attn_q_dgrad")
            acc2 = res[-1]
            dmod[l][3 * i + 1], dmod[l][3 * i] = acc2[0], acc2[1]
            if before is None:
                dx = res[0]
            else:
                record_ln(*before, acc2, 2)
                ln_done = (res[0], res[1])
            if (i == 0 and l > 0) or (i == 1 and l == 0):
                started, started_before = start_grad_exchange(list(made), acc2)
                made.clear()
                if pending is not None:
                    started_before = started_before + finish_grad_exchange(pending, acc2)
                pending = started
    grad_x = dx[None]

    dvec = _pack_rows([jnp.stack([jnp.stack(r) for r in dmod]), dmkv])
    dvec = lax.optimization_barrier((dvec, [gpair[u] for u in made]))[0]
    n_dvec = L * N_MOD * D + 2 * D
    dall = all_gather8(dvec, "ag_dmod").reshape(N_DEV, -1, LANES)
    db_all = sum_leading(dall, "ada_bias_grad").reshape(-1)[:n_dvec]
    g_b_ada = db_all[:L * N_MOD * D].reshape(L, N_MOD * D)
    g_b_ada_kv = db_all[L * N_MOD * D:]
    dall2 = dall.reshape(N_DEV, -1)[:, :n_dvec]
    dmod_all = dall2[:, :L * N_MOD * D].reshape(N_DEV, L, N_MOD * D)
    dmod_sh = jnp.transpose(lax.dynamic_slice_in_dim(dmod_all, s_me * n_ada, n_ada, axis=2), (1, 0, 2))
    dmkv_sh = lax.dynamic_slice_in_dim(dall2[:, L * N_MOD * D:], s_me * n_kv, n_kv, axis=1)[None]
    c_all_t = jnp.transpose(c_all)
    g_w_ada = ada_wgrad(c_all_t, dmod_sh, "ada_wgrad")
    g_w_ada_kv = ada_wgrad(c_all_t, dmkv_sh, "ada_kv_wgrad")[0]

    small_g = [jnp.stack([jnp.stack(r) for r in d_ln_g]), jnp.stack([jnp.stack(r) for r in d_ln_b]),
               jnp.stack(d_b_in), jnp.stack(d_gln_g), jnp.stack(d_gln_b), jnp.stack(d_rel), jnp.stack(d_ws),
               jnp.stack(d_bs)]
    sg_shapes = [a.shape for a in small_g]
    sg_pack = _pack_rows(small_g)
    sg_all = all_gather8(sg_pack, "ag_small_grads").reshape(N_DEV, -1, LANES)
    sg_sum = _unpack_rows(sum_leading(sg_all, "small_grad_sum"), sg_shapes)
    g_ln_g, g_ln_b, g_b_in, g_gln_g, g_gln_b, g_rel = [_shard_last(a, s_me) for a in sg_sum[:6]]
    g_ws, g_bs = sg_sum[6], sg_sum[7]

    last, _ = start_grad_exchange(list(made), sg_all)

    grads = dict(w_ada=g_w_ada, b_ada=g_b_ada, ln_g=g_ln_g, ln_b=g_ln_b, gmlp_b_in=g_b_in, gmlp_ln_g=g_gln_g,
                 gmlp_ln_b=g_gln_b, gmlp_w_s=g_ws, gmlp_b_s=g_bs, w_ada_kv=g_w_ada_kv, b_ada_kv=g_b_ada_kv,
                 attn_rel_bias=g_rel)
    weights = dict(w_ada=w_ada, b_ada=b_ada, ln_g=ln_g, ln_b=ln_b, ffn_gu=ffn_gu, ffn_down=ffn_down,
                   gmlp_w_in=gmlp_w_in, gmlp_b_in=gmlp_b_in, gmlp_ln_g=gmlp_ln_g, gmlp_ln_b=gmlp_ln_b,
                   gmlp_w_s=gmlp_w_s, gmlp_b_s=gmlp_b_s, gmlp_w_out=gmlp_w_out, w_ada_kv=w_ada_kv,
                   b_ada_kv=b_ada_kv, w_kv=w_kv, attn_w_q=attn_w_q, attn_rel_bias=attn_rel_bias, attn_w_o=attn_w_o)
    ms = dict(w_ada=m_w_ada, b_ada=m_b_ada, ln_g=m_ln_g, ln_b=m_ln_b, ffn_gu=m_ffn_gu, ffn_down=m_ffn_down,
              gmlp_w_in=m_gmlp_w_in, gmlp_b_in=m_gmlp_b_in, gmlp_ln_g=m_gmlp_ln_g, gmlp_ln_b=m_gmlp_ln_b,
              gmlp_w_s=m_gmlp_w_s, gmlp_b_s=m_gmlp_b_s, gmlp_w_out=m_gmlp_w_out, w_ada_kv=m_w_ada_kv,
              b_ada_kv=m_b_ada_kv, w_kv=m_w_kv, attn_w_q=m_attn_w_q, attn_rel_bias=m_attn_rel_bias,
              attn_w_o=m_attn_w_o)
    vs = dict(w_ada=v_w_ada, b_ada=v_b_ada, ln_g=v_ln_g, ln_b=v_ln_b, ffn_gu=v_ffn_gu, ffn_down=v_ffn_down,
              gmlp_w_in=v_gmlp_w_in, gmlp_b_in=v_gmlp_b_in, gmlp_ln_g=v_gmlp_ln_g, gmlp_ln_b=v_gmlp_ln_b,
              gmlp_w_s=v_gmlp_w_s, gmlp_b_s=v_gmlp_b_s, gmlp_w_out=v_gmlp_w_out, w_ada_kv=v_w_ada_kv,
              b_ada_kv=v_b_ada_kv, w_kv=v_w_kv, attn_w_q=v_attn_w_q, attn_rel_bias=v_attn_rel_bias,
              attn_w_o=v_attn_w_o)
    order = ["w_ada", "b_ada", "ln_g", "ln_b", "ffn_gu", "ffn_down", "gmlp_w_in", "gmlp_b_in", "gmlp_ln_g",
             "gmlp_ln_b", "gmlp_w_s", "gmlp_b_s", "gmlp_w_out", "w_ada_kv", "b_ada_kv", "w_kv", "attn_w_q",
             "attn_rel_bias", "attn_w_o"]
    big_names = ["w_ada", "w_ada_kv"] + stack_names
    small_names = [nm for nm in order if nm not in big_names]
    delta, new_m, new_v = {}, {}, {}

    def adamw_big(nm):
        shp = weights[nm].shape
        two_d = (-1, shp[-1])
        d, a, b = adamw(weights[nm].reshape(two_d), grads[nm].reshape(two_d), ms[nm].reshape(two_d),
                        vs[nm].reshape(two_d), "adamw")
        delta[nm], new_m[nm], new_v[nm] = d.reshape(shp), a.reshape(shp), b.reshape(shp)

    adamw_big("w_ada")
    adamw_big("w_ada_kv")
    shapes = [weights[nm].shape for nm in small_names]
    d, a, b = adamw(_pack_rows([weights[nm] for nm in small_names]), _pack_rows([grads[nm] for nm in small_names]),
                    _pack_rows([ms[nm] for nm in small_names]), _pack_rows([vs[nm] for nm in small_names]),
                    "adamw_small")
    for nm, dd, aa, bb in zip(small_names, _unpack_rows(d, shapes), _unpack_rows(a, shapes), _unpack_rows(b, shapes)):
        delta[nm], new_m[nm], new_v[nm] = dd, aa, bb

    def full_grad(u):
        lo = jnp.where(ci == 0, own_half[u], sib_half[u])
        hi = jnp.where(ci == 0, sib_half[u], own_half[u])
        return jnp.concatenate([lo, hi], axis=1 if u in col_split else 0)

    def adamw_stack(nm):
        si = stack_names.index(nm)
        g = jnp.stack([full_grad(unit_of[(nm, k)]) for k in range(stacks[si].shape[0])])
        grads[nm] = g.reshape(weights[nm].shape)
        adamw_big(nm)

    late = [stack_names[units[u][0]] for u in last["ids"]]
    early = [nm for nm in stack_names if nm not in late]
    finish_grad_exchange(pending, delta["w_ada"])
    collect_halves(delta["w_ada"])
    for nm in early:
        adamw_stack(nm)
    finish_grad_exchange(last, delta[early[-1]])
    collect_halves(delta[early[-1]])
    for nm in stack_names:
        if nm in late:
            adamw_stack(nm)

    return (loss, grad_x, *[grads[nm] for nm in order], *[delta[nm] for nm in order],
            *[new_m[nm] for nm in order], *[new_v[nm] for nm in order])
```

```python
import functools

import jax
import jax.numpy as jnp
from jax import lax
from jax.experimental import pallas as pl
from jax.experimental.pallas import tpu as pltpu

F32 = jnp.float32
BF16 = jnp.bfloat16
MESH = pl.DeviceIdType.MESH
HIGHEST = lax.Precision.HIGHEST

CHUNK = 64
GMLP_WINDOW = 128
GMLP_GROUPS = 8
HEAD_DIM = 64
LEFT_CHUNKS = 8
BAND = (LEFT_CHUNKS + 1) * CHUNK
LEFT_PAD = LEFT_CHUNKS * CHUNK
MAX_REL = 4 * CHUNK
N_REL = (CHUNK - 1) + MAX_REL + 1
LN_EPS = 1e-5
N_MOD = 9
N_DEV = 8
N_CHIP = 4

ADAM_LR = 0.001
ADAM_B1 = 0.9
ADAM_B2 = 0.999
ADAM_EPS = 1e-08
ADAM_WD = 0.01
ADAM_STEP = 10

LANES = 128
ROW_TILE = 256
MATMUL_ROW_TILE = 512
WGRAD_ROWS = 1024
ATTN_CHUNKS_PER_STEP = 4
VMEM_LIMIT_MB = 56

NT = (((1,), (1,)), ((), ()))
TN = (((0,), (0,)), ((), ()))

ANY = pl.BlockSpec(memory_space=pl.ANY)
VMEM_SPEC = pl.BlockSpec(memory_space=pltpu.VMEM)


def _params(semantics=None):
    kw = dict(vmem_limit_bytes=VMEM_LIMIT_MB * 1024 * 1024)
    if semantics is not None:
        kw["dimension_semantics"] = semantics
    return pltpu.CompilerParams(**kw)


def _sigmoid(v):
    return 0.5 * (1.0 + jnp.tanh(0.5 * v))


def _gelu(v):
    return 0.5 * v * (1.0 + lax.erf(v * (2.0 ** -0.5)))


def _gelu_grad(v):
    return 0.5 * (1.0 + lax.erf(v * (2.0 ** -0.5))) + v * jnp.exp(-0.5 * v * v) * ((2.0 * jnp.pi) ** -0.5)


def _row(m):
    return lambda i: (i, 0)


def _fixed2(i):
    return (0, 0)


def _fixed3(i):
    return (0, 0, 0)


def _resident(shape):
    return pl.BlockSpec(shape, _fixed2 if len(shape) == 2 else _fixed3, pipeline_mode=pl.Buffered(1))


def mod_matmul(x, scl, shift, w, bias, out_dtype, name):
    S, D = x.shape
    NS, _, n = w.shape
    tm = min(MATMUL_ROW_TILE, S)
    has_bias = bias is not None

    def body(*refs):
        if has_bias:
            x_ref, scl_ref, sh_ref, w_ref, b_ref, o_ref, h_ref = refs
        else:
            x_ref, scl_ref, sh_ref, w_ref, o_ref, h_ref = refs
        h = (x_ref[...] * (1.0 + scl_ref[...]) + sh_ref[...]).astype(BF16)
        h_ref[...] = h
        for s in range(NS):
            acc = jnp.dot(h, w_ref[s], preferred_element_type=F32)
            if has_bias:
                acc = acc + b_ref[:, s * n:(s + 1) * n]
            o_ref[:, s * n:(s + 1) * n] = acc.astype(out_dtype)

    in_specs = [pl.BlockSpec((tm, D), _row(0)), pl.BlockSpec((1, D), _fixed2), pl.BlockSpec((1, D), _fixed2),
                _resident((NS, D, n))]
    args = [x, scl, shift, w]
    if has_bias:
        in_specs.append(pl.BlockSpec((1, NS * n), _fixed2))
        args.append(bias)
    return pl.pallas_call(
        body, name=name, grid=(S // tm,), in_specs=in_specs,
        out_specs=[pl.BlockSpec((tm, NS * n), _row(0)), pl.BlockSpec((tm, D), _row(0))],
        out_shape=[jax.ShapeDtypeStruct((S, NS * n), out_dtype), jax.ShapeDtypeStruct((S, D), BF16)],
        compiler_params=_params(("parallel",)),
    )(*args)


def matmul_res_ln(a, w, x, gw, lg, lb, alpha, swiglu, name):
    S, D = x.shape
    K = w.shape[0]
    tm = min(ROW_TILE, S)
    ka = a.shape[1]

    def body(a_ref, w_ref, x_ref, gw_ref, lg_ref, lb_ref, xn_ref, xh_ref, rs_ref, y_ref, *act_ref):
        if swiglu:
            g = a_ref[:, :K].astype(F32)
            u = a_ref[:, K:].astype(F32)
            act = (g * _sigmoid(g) * u).astype(BF16)
            act_ref[0][...] = act
        else:
            act = a_ref[...].astype(BF16)
        y = jnp.dot(act, w_ref[...], preferred_element_type=F32)
        z = alpha * x_ref[...] + gw_ref[...] * y
        mu = jnp.mean(z, axis=-1, keepdims=True)
        zc = z - mu
        var = jnp.mean(zc * zc, axis=-1, keepdims=True)
        rstd = lax.rsqrt(var + LN_EPS)
        xhat = zc * rstd
        xn_ref[...] = xhat * lg_ref[...] + lb_ref[...]
        xh_ref[...] = xhat
        rs_ref[...] = rstd
        y_ref[...] = y.astype(BF16)

    vec = pl.BlockSpec((1, D), _fixed2)
    out_specs = [pl.BlockSpec((tm, D), _row(0)), pl.BlockSpec((tm, D), _row(0)), pl.BlockSpec((tm, 1), _row(0)),
                 pl.BlockSpec((tm, D), _row(0))]
    out_shape = [jax.ShapeDtypeStruct((S, D), F32), jax.ShapeDtypeStruct((S, D), F32),
                 jax.ShapeDtypeStruct((S, 1), F32), jax.ShapeDtypeStruct((S, D), BF16)]
    if swiglu:
        out_specs.append(pl.BlockSpec((tm, K), _row(0)))
        out_shape.append(jax.ShapeDtypeStruct((S, K), BF16))
    return pl.pallas_call(
        body, name=name, grid=(S // tm,),
        in_specs=[pl.BlockSpec((tm, ka), _row(0)), _resident((K, D)), pl.BlockSpec((tm, D), _row(0)),
                  vec, vec, vec],
        out_specs=out_specs, out_shape=out_shape,
        compiler_params=_params(("parallel",)),
    )(a, w, x, gw, lg, lb)


def _ln_res_bwd_tile(d, xh_ref, rs_ref, lg_ref, y_ref, gw_ref, wres, alpha, dxa_ref, dy_ref, acc_ref, row0):
    xh = xh_ref[...]
    dxh = d * lg_ref[...]
    m1 = jnp.mean(dxh, axis=-1, keepdims=True)
    m2 = jnp.mean(dxh * xh, axis=-1, keepdims=True)
    dz = rs_ref[...] * (dxh - m1 - xh * m2)
    dxa_ref[...] = alpha * dz
    dy_ref[...] = (gw_ref[...] * dz).astype(BF16)
    acc_ref[row0:row0 + 1, :] += jnp.sum(d * xh, axis=0, keepdims=True)
    acc_ref[row0 + 1:row0 + 2, :] += jnp.sum(d, axis=0, keepdims=True)
    acc_ref[row0 + 2:row0 + 3, :] += jnp.sum((wres * dz) * y_ref[...].astype(F32), axis=0, keepdims=True)


def ln_res_bwd(dxn, xhat, rstd, lg, y, gw, wres, alpha, name):
    S, D = dxn.shape
    tm = min(MATMUL_ROW_TILE, S)

    def body(dxn_ref, xh_ref, rs_ref, lg_ref, y_ref, gw_ref, dxa_ref, dy_ref, acc_ref):
        @pl.when(pl.program_id(0) == 0)
        def _():
            acc_ref[...] = jnp.zeros_like(acc_ref)

        _ln_res_bwd_tile(dxn_ref[...], xh_ref, rs_ref, lg_ref, y_ref, gw_ref, wres, alpha, dxa_ref, dy_ref, acc_ref, 0)

    vec = pl.BlockSpec((1, D), _fixed2)
    tile = pl.BlockSpec((tm, D), _row(0))
    return pl.pallas_call(
        body, name=name, grid=(S // tm,),
        in_specs=[tile, tile, pl.BlockSpec((tm, 1), _row(0)), vec, tile, vec],
        out_specs=[tile, tile, pl.BlockSpec((8, D), _fixed2)],
        out_shape=[jax.ShapeDtypeStruct((S, D), F32), jax.ShapeDtypeStruct((S, D), BF16),
                   jax.ShapeDtypeStruct((8, D), F32)],
        compiler_params=_params(("arbitrary",)),
    )(dxn, xhat, rstd, lg, y, gw)


def ffn_act_bwd(dy, wd, gu, after, name):
    S, D = dy.shape
    K = wd.shape[0]
    tm = min(ROW_TILE, S)

    def body(dy_ref, wd_ref, gu_ref, after_ref, o_ref):
        da = lax.dot_general(dy_ref[...], wd_ref[...], NT, preferred_element_type=F32).astype(BF16)
        g = gu_ref[:, :K]
        u = gu_ref[:, K:]
        sg = _sigmoid(g)
        o_ref[:, :K] = da * u * (sg * (1.0 + g * (1.0 - sg)))
        o_ref[:, K:] = da * (g * sg)

    return pl.pallas_call(
        body, name=name, grid=(S // tm,),
        in_specs=[pl.BlockSpec((tm, D), _row(0)), _resident((K, D)), pl.BlockSpec((tm, 2 * K), _row(0)), ANY],
        out_specs=pl.BlockSpec((tm, 2 * K), _row(0)),
        out_shape=jax.ShapeDtypeStruct((S, 2 * K), BF16),
        compiler_params=_params(("parallel",)),
    )(dy, wd, gu, after)


def matmul_nt(a, w, after, name):
    S, D = a.shape
    K = w.shape[0]
    tm = min(MATMUL_ROW_TILE, S)

    def body(a_ref, w_ref, after_ref, o_ref):
        o_ref[...] = lax.dot_general(a_ref[...], w_ref[...], NT, preferred_element_type=F32).astype(BF16)

    return pl.pallas_call(
        body, name=name, grid=(S // tm,),
        in_specs=[pl.BlockSpec((tm, D), _row(0)), _resident((K, D)), ANY],
        out_specs=pl.BlockSpec((tm, K), _row(0)),
        out_shape=jax.ShapeDtypeStruct((S, K), BF16),
        compiler_params=_params(("parallel",)),
    )(a, w, after)


def dgrad_mod(dpre, w, dxa, xin, scl, prev, alpha, name):
    S, D = xin.shape
    NS, _, n = w.shape
    tm = min(ROW_TILE, S)
    wres = prev[5] if prev is not None else None

    def body(*refs):
        dp_ref, w_ref, dxa_ref, xin_ref, scl_ref = refs[:5]
        acc_ref = refs[-1]

        @pl.when(pl.program_id(0) == 0)
        def _():
            acc_ref[...] = jnp.zeros_like(acc_ref)

        dh = jnp.zeros((tm, D), F32)
        for s in range(NS):
            dh = dh + lax.dot_general(dp_ref[:, s * n:(s + 1) * n].astype(BF16), w_ref[s], NT,
                                      preferred_element_type=F32)
        dx = dxa_ref[...] + dh * (1.0 + scl_ref[...])
        acc_ref[0:1, :] += jnp.sum(dh * xin_ref[...], axis=0, keepdims=True)
        acc_ref[1:2, :] += jnp.sum(dh, axis=0, keepdims=True)
        if prev is None:
            refs[5][...] = dx
        else:
            xh_ref, rs_ref, lg_ref, y_ref, gw_ref, pdxa_ref, pdy_ref = refs[5:12]
            _ln_res_bwd_tile(dx, xh_ref, rs_ref, lg_ref, y_ref, gw_ref, wres, alpha, pdxa_ref, pdy_ref, acc_ref, 2)

    tile = pl.BlockSpec((tm, D), _row(0))
    vec = pl.BlockSpec((1, D), _fixed2)
    in_specs = [pl.BlockSpec((tm, NS * n), _row(0)), _resident((NS, D, n)), tile, tile, vec]
    args = [dpre, w, dxa, xin, scl]
    if prev is None:
        out_specs = [tile]
        out_shape = [jax.ShapeDtypeStruct((S, D), F32)]
    else:
        in_specs += [tile, pl.BlockSpec((tm, 1), _row(0)), vec, tile, vec]
        args += list(prev[:5])
        out_specs = [tile, tile]
        out_shape = [jax.ShapeDtypeStruct((S, D), F32), jax.ShapeDtypeStruct((S, D), BF16)]
    return pl.pallas_call(
        body, name=name, grid=(S // tm,), in_specs=in_specs,
        out_specs=out_specs + [pl.BlockSpec((8, D), _fixed2)],
        out_shape=out_shape + [jax.ShapeDtypeStruct((8, D), F32)],
        compiler_params=_params(("arbitrary",)),
    )(*args)


PAIR_COLLECTIVE_ID = 1


def wgrad_pair(a, b, J, kb, nb, a_block, b_block, half_idx, name):
    S = b.shape[0]
    ts = min(WGRAD_ROWS, S)
    nsteps = S // ts

    def body(h_ref, a_ref, b_ref, o_ref, acc_ref, send_buf, recv_buf, send_sems, recv_sems):
        jj, si = pl.program_id(0), pl.program_id(1)
        x, y, c = _coords()
        j = lax.rem(jj, J)
        last = si == nsteps - 1

        def copy(blk):
            return pltpu.make_async_remote_copy(
                src_ref=send_buf.at[blk], dst_ref=recv_buf.at[blk], send_sem=send_sems.at[blk],
                recv_sem=recv_sems.at[blk], device_id=(x, y, 1 - c), device_id_type=MESH)

        @pl.when(jnp.logical_and(jj == 0, si == 0))
        def _():
            barrier = pltpu.get_barrier_semaphore()
            pl.semaphore_signal(barrier, inc=1, device_id=(x, y, 1 - c), device_id_type=MESH)
            pl.semaphore_wait(barrier, 1)

        @pl.when(si == 0)
        def _():
            acc_ref[...] = jnp.zeros_like(acc_ref)

        acc_ref[...] += lax.dot_general(a_ref[...], b_ref[...].astype(BF16), TN, preferred_element_type=F32)

        @pl.when(jnp.logical_and(last, jj < J))
        def _():
            send_buf[j] = acc_ref[...].astype(BF16)
            copy(j).start()

        @pl.when(jnp.logical_and(last, jj >= J))
        def _():
            copy(j).wait_recv()
            o_ref[...] = (acc_ref[...] + recv_buf[j].astype(F32)).astype(BF16)

        @pl.when(jnp.logical_and(last, jj == 2 * J - 1))
        def _():
            for blk in range(J):
                copy(blk).wait_send()

    def half(jj, h):
        return jnp.where(jj < J, 1 - h[0], h[0])

    return pl.pallas_call(
        body, name=name,
        grid_spec=pltpu.PrefetchScalarGridSpec(
            num_scalar_prefetch=1, grid=(2 * J, nsteps),
            in_specs=[pl.BlockSpec((ts, kb), lambda jj, s, h: (s, a_block(lax.rem(jj, J), half(jj, h)))),
                      pl.BlockSpec((ts, nb), lambda jj, s, h: (s, b_block(lax.rem(jj, J), half(jj, h))))],
            out_specs=pl.BlockSpec((None, kb, nb), lambda jj, s, h: (jnp.maximum(jj - J, 0), 0, 0)),
            scratch_shapes=[pltpu.VMEM((kb, nb), F32), pltpu.VMEM((J, kb, nb), BF16), pltpu.VMEM((J, kb, nb), BF16),
                            pltpu.SemaphoreType.DMA((J,)), pltpu.SemaphoreType.DMA((J,))]),
        out_shape=jax.ShapeDtypeStruct((J, kb, nb), BF16),
        compiler_params=pltpu.CompilerParams(
            vmem_limit_bytes=VMEM_LIMIT_MB * 1024 * 1024, dimension_semantics=("arbitrary", "arbitrary"),
            collective_id=PAIR_COLLECTIVE_ID),
    )(half_idx, a, b)


def _window_mask():
    t = lax.broadcasted_iota(jnp.int32, (GMLP_WINDOW, GMLP_WINDOW), 0)
    s = lax.broadcasted_iota(jnp.int32, (GMLP_WINDOW, GMLP_WINDOW), 1)
    return ((s // CHUNK) <= (t // CHUNK)).astype(F32)


def sgu_fwd(pre, glg, glb, ws, bst, name):
    S, H2 = pre.shape
    H = H2 // 2
    W, G = GMLP_WINDOW, GMLP_GROUPS
    gd = H // G
    tm = min(ROW_TILE, S)

    def body(pre_ref, glg_ref, glb_ref, ws_ref, bst_ref, q_ref):
        u = _gelu(pre_ref[:, :H])
        v = _gelu(pre_ref[:, H:])
        mu = jnp.mean(v, axis=-1, keepdims=True)
        vc = v - mu
        var = jnp.mean(vc * vc, axis=-1, keepdims=True)
        vn = ((vc * lax.rsqrt(var + LN_EPS)) * glg_ref[...] + glb_ref[...]).astype(BF16)
        mask = _window_mask()
        for g in range(G):
            wsg = (ws_ref[g] * mask).astype(BF16)
            bcol = bst_ref[:, g:g + 1]
            for wi in range(tm // W):
                rows = slice(wi * W, (wi + 1) * W)
                cols = slice(g * gd, (g + 1) * gd)
                s = jnp.dot(wsg, vn[rows, cols], preferred_element_type=F32) + bcol
                q_ref[rows, cols] = (u[rows, cols] * s).astype(BF16)

    return pl.pallas_call(
        body, name=name, grid=(S // tm,),
        in_specs=[pl.BlockSpec((tm, H2), _row(0)), pl.BlockSpec((1, H), _fixed2), pl.BlockSpec((1, H), _fixed2),
                  pl.BlockSpec((G, W, W), _fixed3), pl.BlockSpec((W, G), _fixed2)],
        out_specs=pl.BlockSpec((tm, H), _row(0)),
        out_shape=jax.ShapeDtypeStruct((S, H), BF16),
        compiler_params=_params(("parallel",)),
    )(pre, glg, glb, ws, bst)


def sgu_bwd(dq, pre, glg, glb, ws, bst, name):
    S, H2 = pre.shape
    H = H2 // 2
    W, G = GMLP_WINDOW, GMLP_GROUPS
    gd = H // G
    tm = min(ROW_TILE, S)

    def body(dq_ref, pre_ref, glg_ref, glb_ref, ws_ref, bst_ref,
             dpre_ref, dws_ref, dss_ref, dgl_ref, dbin_ref, du_s, dvn_s):
        @pl.when(pl.program_id(0) == 0)
        def _():
            dws_ref[...] = jnp.zeros_like(dws_ref)
            dss_ref[...] = jnp.zeros_like(dss_ref)
            dgl_ref[...] = jnp.zeros_like(dgl_ref)
            dbin_ref[...] = jnp.zeros_like(dbin_ref)

        pu = pre_ref[:, :H]
        pv = pre_ref[:, H:]
        u = _gelu(pu)
        v = _gelu(pv)
        mu = jnp.mean(v, axis=-1, keepdims=True)
        vc = v - mu
        var = jnp.mean(vc * vc, axis=-1, keepdims=True)
        rstd = lax.rsqrt(var + LN_EPS)
        vhat = vc * rstd
        vn = (vhat * glg_ref[...] + glb_ref[...]).astype(BF16)
        mask = _window_mask()
        for g in range(G):
            wsg = (ws_ref[g] * mask).astype(BF16)
            bcol = bst_ref[:, g:g + 1]
            cols = slice(g * gd, (g + 1) * gd)
            for wi in range(tm // W):
                rows = slice(wi * W, (wi + 1) * W)
                vblk = vn[rows, cols]
                s = jnp.dot(wsg, vblk, preferred_element_type=F32) + bcol
                dqb = dq_ref[rows, cols].astype(F32)
                du_s[rows, cols] = dqb * s
                ds = dqb * u[rows, cols]
                dss_ref[:, cols] += ds
                dsb = ds.astype(BF16)
                dvn_s[rows, cols] = lax.dot_general(wsg, dsb, TN, preferred_element_type=F32)
                dws_ref[g] += lax.dot_general(dsb, vblk, NT, preferred_element_type=F32) * mask
        dvn = dvn_s[...]
        dgl_ref[0:1, :] += jnp.sum(dvn * vhat, axis=0, keepdims=True)
        dgl_ref[1:2, :] += jnp.sum(dvn, axis=0, keepdims=True)
        dvh = dvn * glg_ref[...]
        m1 = jnp.mean(dvh, axis=-1, keepdims=True)
        m2 = jnp.mean(dvh * vhat, axis=-1, keepdims=True)
        dv = rstd * (dvh - m1 - vhat * m2)
        dpu = du_s[...] * _gelu_grad(pu)
        dpv = dv * _gelu_grad(pv)
        dbin_ref[0:1, :H] += jnp.sum(dpu, axis=0, keepdims=True)
        dbin_ref[0:1, H:] += jnp.sum(dpv, axis=0, keepdims=True)
        dpre_ref[:, :H] = dpu.astype(BF16)
        dpre_ref[:, H:] = dpv.astype(BF16)

    return pl.pallas_call(
        body, name=name, grid=(S // tm,),
        in_specs=[pl.BlockSpec((tm, H), _row(0)), pl.BlockSpec((tm, H2), _row(0)), pl.BlockSpec((1, H), _fixed2),
                  pl.BlockSpec((1, H), _fixed2), pl.BlockSpec((G, W, W), _fixed3), pl.BlockSpec((W, G), _fixed2)],
        out_specs=[pl.BlockSpec((tm, H2), _row(0)), pl.BlockSpec((G, W, W), _fixed3), pl.BlockSpec((W, H), _fixed2),
                   pl.BlockSpec((8, H), _fixed2), pl.BlockSpec((8, H2), _fixed2)],
        out_shape=[jax.ShapeDtypeStruct((S, H2), BF16), jax.ShapeDtypeStruct((G, W, W), F32),
                   jax.ShapeDtypeStruct((W, H), F32), jax.ShapeDtypeStruct((8, H), F32),
                   jax.ShapeDtypeStruct((8, H2), F32)],
        scratch_shapes=[pltpu.VMEM((tm, H), F32), pltpu.VMEM((tm, H), F32)],
        compiler_params=_params(("arbitrary",)),
    )(dq, pre, glg, glb, ws, bst)


def group_lane_sum(dss, name):
    W, H = dss.shape
    gd = H // GMLP_GROUPS

    def body(d_ref, o_ref):
        j = lax.broadcasted_iota(jnp.int32, (H, LANES), 0)
        g = lax.broadcasted_iota(jnp.int32, (H, LANES), 1)
        ind = ((j // gd) == g).astype(F32)
        o_ref[...] = jnp.dot(d_ref[...], ind, preferred_element_type=F32, precision=HIGHEST)

    return pl.pallas_call(
        body, name=name, in_specs=[VMEM_SPEC], out_specs=VMEM_SPEC,
        out_shape=jax.ShapeDtypeStruct((W, LANES), F32), compiler_params=_params(),
    )(dss)


def _attn_load(j, cps, q_ref, k_ref, v_ref):
    r = lax.broadcasted_iota(jnp.int32, (CHUNK, BAND), 1)
    chunks = []
    for cc in range(cps):
        start = pl.multiple_of((j * cps + cc) * CHUNK, CHUNK)
        chunks.append((q_ref[cc * CHUNK:(cc + 1) * CHUNK, :], k_ref[pl.ds(start, BAND), :],
                       v_ref[pl.ds(start, BAND), :], (r + start) >= LEFT_PAD))
    return chunks


def _attn_probs(chunks, b_ref, sels, scale):
    qms = [[jnp.where(sel, q2, jnp.zeros_like(q2)) for sel in sels] for q2, _, _, _ in chunks]
    raw = [[lax.dot_general(qm, k2, NT, preferred_element_type=F32) for qm in qms[cc]]
           for cc, (_, k2, _, _) in enumerate(chunks)]
    probs = []
    for cc, (_, _, _, valid) in enumerate(chunks):
        row = []
        for sub in range(2):
            s = jnp.where(valid, raw[cc][sub] * scale + b_ref[sub], -jnp.inf)
            e = jnp.exp(s - jnp.max(s, axis=-1, keepdims=True))
            row.append(e / jnp.sum(e, axis=-1, keepdims=True))
        probs.append(row)
    return qms, probs


def attn_fwd(q, kpad, vpad, bias, name):
    S, D = q.shape
    HP = D // LANES
    cps = min(ATTN_CHUNKS_PER_STEP, S // CHUNK)
    tq = cps * CHUNK
    scale = HEAD_DIM ** -0.5

    def body(q_ref, k_ref, v_ref, b_ref, o_ref):
        sel0 = lax.broadcasted_iota(jnp.int32, (CHUNK, LANES), 1) < HEAD_DIM
        chunks = _attn_load(pl.program_id(1), cps, q_ref, k_ref, v_ref)
        _, probs = _attn_probs(chunks, b_ref, (sel0, jnp.logical_not(sel0)), scale)
        outs = [[jnp.dot(probs[cc][sub].astype(BF16), v2, preferred_element_type=F32) for sub in range(2)]
                for cc, (_, _, v2, _) in enumerate(chunks)]
        o_ref[...] = jnp.concatenate([jnp.where(sel0, o[0], o[1]) for o in outs], axis=0).astype(BF16)

    kv_spec = pl.BlockSpec((S + LEFT_PAD, LANES), lambda h, j: (0, h))
    return pl.pallas_call(
        body, name=name, grid=(HP, S // tq),
        in_specs=[pl.BlockSpec((tq, LANES), lambda h, j: (j, h)), kv_spec, kv_spec,
                  pl.BlockSpec((2, CHUNK, BAND), lambda h, j: (h, 0, 0))],
        out_specs=pl.BlockSpec((tq, LANES), lambda h, j: (j, h)),
        out_shape=jax.ShapeDtypeStruct((S, D), BF16),
        compiler_params=_params(("parallel", "parallel")),
    )(q, kpad, vpad, bias)


def attn_bwd(q, do, kpad, vpad, bias, dk_in, dv_in, name):
    S, D = q.shape
    HP = D // LANES
    NH = 2 * HP
    cps = min(ATTN_CHUNKS_PER_STEP, S // CHUNK)
    tq = cps * CHUNK
    nj = S // tq
    scale = HEAD_DIM ** -0.5

    def body(q_ref, do_ref, k_ref, v_ref, b_ref, dki_ref, dvi_ref, dq_ref, dk_ref, dv_ref, db_ref, dk_acc, dv_acc):
        j = pl.program_id(1)

        @pl.when(j == 0)
        def _():
            dk_acc[:LEFT_PAD, :] = jnp.zeros((LEFT_PAD, LANES), F32)
            dv_acc[:LEFT_PAD, :] = jnp.zeros((LEFT_PAD, LANES), F32)
            dk_acc[LEFT_PAD:, :] = dki_ref[...]
            dv_acc[LEFT_PAD:, :] = dvi_ref[...]
            db_ref[...] = jnp.zeros_like(db_ref)

        sel0 = lax.broadcasted_iota(jnp.int32, (CHUNK, LANES), 1) < HEAD_DIM
        sels = (sel0, jnp.logical_not(sel0))
        chunks = _attn_load(j, cps, q_ref, k_ref, v_ref)
        pairs = [(cc, sub) for cc in range(cps) for sub in range(2)]
        qms, probs = _attn_probs(chunks, b_ref, sels, scale)
        doms = [[jnp.where(sel, do_ref[cc * CHUNK:(cc + 1) * CHUNK, :], jnp.zeros((CHUNK, LANES), BF16))
                 for sel in sels] for cc in range(cps)]
        dps = {(cc, sub): lax.dot_general(doms[cc][sub], chunks[cc][2], NT, preferred_element_type=F32)
               for cc, sub in pairs}
        dss = {}
        for cc, sub in pairs:
            p = probs[cc][sub]
            dss[cc, sub] = p * (dps[cc, sub] - jnp.sum(dps[cc, sub] * p, axis=-1, keepdims=True))
        dsb = {key: ds.astype(BF16) for key, ds in dss.items()}
        dqs = {(cc, sub): jnp.dot(dsb[cc, sub], chunks[cc][1], preferred_element_type=F32) * scale
               for cc, sub in pairs}
        dks = {(cc, sub): lax.dot_general(dsb[cc, sub], qms[cc][sub], TN, preferred_element_type=F32) * scale
               for cc, sub in pairs}
        dvs = {(cc, sub): lax.dot_general(probs[cc][sub].astype(BF16), doms[cc][sub], TN,
                                          preferred_element_type=F32) for cc, sub in pairs}
        dq_ref[...] = jnp.concatenate([jnp.where(sel0, dqs[cc, 0], dqs[cc, 1]) for cc in range(cps)],
                                      axis=0).astype(BF16)
        for sub in range(2):
            total = dss[0, sub]
            for cc in range(1, cps):
                total = total + dss[cc, sub]
            db_ref[sub] += total
        dk_parts = [dks[cc, 0] + dks[cc, 1] for cc in range(cps)]
        dv_parts = [dvs[cc, 0] + dvs[cc, 1] for cc in range(cps)]

        def window(parts):
            blocks = []
            for rb in range(cps - 1 + BAND // CHUNK):
                acc = None
                for cc in range(cps):
                    b = rb - cc
                    if 0 <= b < BAND // CHUNK:
                        piece = parts[cc][b * CHUNK:(b + 1) * CHUNK, :]
                        acc = piece if acc is None else acc + piece
                blocks.append(acc)
            return jnp.concatenate(blocks, axis=0)

        span = pl.ds(pl.multiple_of(j * cps * CHUNK, CHUNK), (cps - 1) * CHUNK + BAND)
        dk_acc[span, :] += window(dk_parts)
        dv_acc[span, :] += window(dv_parts)

        @pl.when(j == nj - 1)
        def _():
            dk_ref[...] = dk_acc[LEFT_PAD:, :]
            dv_ref[...] = dv_acc[LEFT_PAD:, :]

    q_spec = pl.BlockSpec((tq, LANES), lambda h, j: (j, h))
    kv_spec = pl.BlockSpec((S + LEFT_PAD, LANES), lambda h, j: (0, h))
    col_spec = pl.BlockSpec((S, LANES), lambda h, j: (0, h))
    b_spec = pl.BlockSpec((2, CHUNK, BAND), lambda h, j: (h, 0, 0))
    return pl.pallas_call(
        body, name=name, grid=(HP, nj),
        in_specs=[q_spec, q_spec, kv_spec, kv_spec, b_spec, col_spec, col_spec],
        out_specs=[q_spec, col_spec, col_spec, b_spec],
        out_shape=[jax.ShapeDtypeStruct((S, D), BF16), jax.ShapeDtypeStruct((S, D), F32),
                   jax.ShapeDtypeStruct((S, D), F32), jax.ShapeDtypeStruct((NH, CHUNK, BAND), F32)],
        scratch_shapes=[pltpu.VMEM((S + LEFT_PAD, LANES), F32), pltpu.VMEM((S + LEFT_PAD, LANES), F32)],
        compiler_params=_params(("parallel", "arbitrary")),
    )(q, do, kpad, vpad, bias, dk_in, dv_in)


def _rel_onehot(t):
    r = lax.broadcasted_iota(jnp.int32, (BAND, N_REL), 0)
    i = lax.broadcasted_iota(jnp.int32, (BAND, N_REL), 1)
    idx = jnp.clip(t + LEFT_PAD - r, -(CHUNK - 1), MAX_REL) + (CHUNK - 1)
    return (idx == i).astype(F32)


def bias_expand(rb, name):
    NH = rb.shape[0]

    def body(rb_ref, o_ref):
        def step(t, carry):
            o_ref[t] = lax.dot_general(rb_ref[...], _rel_onehot(t), NT, preferred_element_type=F32,
                                       precision=HIGHEST)
            return carry

        lax.fori_loop(0, CHUNK, step, 0)

    return pl.pallas_call(
        body, name=name, in_specs=[VMEM_SPEC], out_specs=VMEM_SPEC,
        out_shape=jax.ShapeDtypeStruct((CHUNK, NH, BAND), F32), compiler_params=_params(),
    )(rb)


def bias_grad(dsum, name):
    NH = dsum.shape[1]

    def body(d_ref, o_ref):
        def step(t, acc):
            return acc + jnp.dot(d_ref[t], _rel_onehot(t), preferred_element_type=F32, precision=HIGHEST)

        o_ref[...] = lax.fori_loop(0, CHUNK, step, jnp.zeros((NH, N_REL), F32))

    return pl.pallas_call(
        body, name=name, in_specs=[VMEM_SPEC], out_specs=VMEM_SPEC,
        out_shape=jax.ShapeDtypeStruct((NH, N_REL), F32), compiler_params=_params(),
    )(dsum)


def loss_grad(y, tgt, name):
    S, D = y.shape
    tm = min(MATMUL_ROW_TILE, S)

    def body(y_ref, t_ref, d_ref, acc_ref):
        @pl.when(pl.program_id(0) == 0)
        def _():
            acc_ref[...] = jnp.zeros_like(acc_ref)

        err = y_ref[...] - t_ref[...]
        d_ref[...] = err * (1.0 / D)
        acc_ref[0:1, :] += jnp.sum(err * err, axis=0, keepdims=True)

    tile = pl.BlockSpec((tm, D), _row(0))
    return pl.pallas_call(
        body, name=name, grid=(S // tm,), in_specs=[tile, tile],
        out_specs=[tile, pl.BlockSpec((8, D), _fixed2)],
        out_shape=[jax.ShapeDtypeStruct((S, D), F32), jax.ShapeDtypeStruct((8, D), F32)],
        compiler_params=_params(("arbitrary",)),
    )(y, tgt)


def _col_tile(n):
    for t in (768, 512, 256, 128):
        if n % t == 0:
            return t
    return n


def ada_fwd(c_all, w, b, name):
    L, D, n = w.shape
    tn = _col_tile(n)

    def body(c_ref, w_ref, b_ref, o_ref):
        cv = c_ref[...]
        ca = cv * _sigmoid(cv)
        o_ref[...] = jnp.dot(ca, w_ref[...], preferred_element_type=F32, precision=HIGHEST) + b_ref[...]

    return pl.pallas_call(
        body, name=name, grid=(L, n // tn),
        in_specs=[pl.BlockSpec((N_DEV, D), lambda l, j: (0, 0)), pl.BlockSpec((None, D, tn), lambda l, j: (l, 0, j)),
                  pl.BlockSpec((None, 1, tn), lambda l, j: (l, 0, j))],
        out_specs=pl.BlockSpec((None, N_DEV, tn), lambda l, j: (l, 0, j)),
        out_shape=jax.ShapeDtypeStruct((L, N_DEV, n), F32),
        compiler_params=_params(("parallel", "parallel")),
    )(c_all, w, b)


def ada_wgrad(c_all_t, dmod, name):
    L, _, n = dmod.shape
    D = c_all_t.shape[0]
    tn = _col_tile(n)

    def body(c_ref, d_ref, o_ref):
        cv = c_ref[...]
        ca = cv * _sigmoid(cv)
        o_ref[...] = jnp.dot(ca, d_ref[...], preferred_element_type=F32, precision=HIGHEST)

    return pl.pallas_call(
        body, name=name, grid=(L, n // tn),
        in_specs=[pl.BlockSpec((D, N_DEV), lambda l, j: (0, 0)), pl.BlockSpec((None, N_DEV, tn), lambda l, j: (l, 0, j))],
        out_specs=pl.BlockSpec((None, D, tn), lambda l, j: (l, 0, j)),
        out_shape=jax.ShapeDtypeStruct((L, D, n), F32),
        compiler_params=_params(("parallel", "parallel")),
    )(c_all_t, dmod)


ELEMENTWISE_BLOCK_BYTES = 3 * 1024 * 1024


def _elementwise_rows(rows, row_bytes):
    for t in (4096, 2048, 1024, 512, 256, 128, 64, 32, 16):
        if rows % t == 0 and t * row_bytes <= ELEMENTWISE_BLOCK_BYTES:
            return t
    return rows


def sum_leading(a, name):
    n, M, N = a.shape
    tr = _elementwise_rows(M, n * N * 4)

    def body(a_ref, o_ref):
        acc = a_ref[0]
        for i in range(1, n):
            acc = acc + a_ref[i]
        o_ref[...] = acc

    return pl.pallas_call(
        body, name=name, grid=(M // tr,),
        in_specs=[pl.BlockSpec((n, tr, N), lambda i: (0, i, 0))],
        out_specs=pl.BlockSpec((tr, N), _row(0)),
        out_shape=jax.ShapeDtypeStruct((M, N), F32),
        compiler_params=_params(("parallel",)),
    )(a)


def chip_sum(psum, land, chip_idx, transposed, name):
    n, M, N = psum.shape
    tr = M if transposed else _elementwise_rows(M, N * 8)

    def body(s_ref, p_ref, a_ref, b_ref, c_ref, o_ref):
        total = ((p_ref[...].astype(F32) + a_ref[...].astype(F32)) + b_ref[...].astype(F32)) + c_ref[...].astype(F32)
        o_ref[...] = jnp.transpose(total) if transposed else total

    def entry(j):
        return pl.BlockSpec((None, tr, N), lambda i, s: ((s[0] + j) % n, i, 0))

    out_block, out_dims = ((N, tr), (N, M)) if transposed else ((tr, N), (M, N))
    return pl.pallas_call(
        body, name=name,
        grid_spec=pltpu.PrefetchScalarGridSpec(
            num_scalar_prefetch=1, grid=(M // tr,),
            in_specs=[entry(0), entry(1), entry(2), entry(3)],
            out_specs=pl.BlockSpec(out_block, lambda i, s: (0, 0) if transposed else (i, 0))),
        out_shape=jax.ShapeDtypeStruct(out_dims, F32),
        compiler_params=_params(("parallel",)),
    )(chip_idx, psum, land, land, land)


def adamw(w, g, m, v, name):
    M, N = w.shape
    tr = _elementwise_rows(M, N * 4)
    c1 = 1.0 - ADAM_B1 ** ADAM_STEP
    c2 = 1.0 - ADAM_B2 ** ADAM_STEP

    def body(w_ref, g_ref, m_ref, v_ref, d_ref, nm_ref, nv_ref):
        g = g_ref[...]
        nm = ADAM_B1 * m_ref[...] + (1.0 - ADAM_B1) * g
        nv = ADAM_B2 * v_ref[...] + (1.0 - ADAM_B2) * (g * g)
        d_ref[...] = -ADAM_LR * ((nm / c1) / (jnp.sqrt(nv / c2) + ADAM_EPS) + ADAM_WD * w_ref[...])
        nm_ref[...] = nm
        nv_ref[...] = nv

    spec = pl.BlockSpec((tr, N), _row(0))
    shp = jax.ShapeDtypeStruct((M, N), F32)
    return pl.pallas_call(
        body, name=name, grid=(M // tr,), in_specs=[spec] * 4, out_specs=[spec] * 3, out_shape=[shp] * 3,
        compiler_params=_params(("parallel",)),
    )(w, g, m, v)


def _coords():
    return lax.axis_index("x"), lax.axis_index("y"), lax.axis_index("c")


def all_gather8(block, name):
    m_per, n = block.shape

    def body(x_ref, out_ref, send_sems, recv_sems, local_sem):
        x, y, c = _coords()
        me, sibling = (x, y, c), (x, y, 1 - c)
        chips = [(1 - x, y), (x, 1 - y), (1 - x, 1 - y)]

        def rows(px, py, pc):
            return out_ref.at[pl.ds((4 * px + 2 * py + pc) * m_per, m_per), :]

        def copy(k, blk, to, src=None):
            return pltpu.make_async_remote_copy(
                src_ref=rows(*blk) if src is None else src, dst_ref=rows(*blk),
                send_sem=send_sems.at[k], recv_sem=recv_sems.at[k], device_id=to, device_id_type=MESH)

        mine = pltpu.make_async_copy(x_ref, rows(*me), local_sem)
        mine.start()
        first = [copy(0, me, sibling, src=x_ref)]
        first += [copy(1 + j, me, (*chip, c), src=x_ref) for j, chip in enumerate(chips)]
        for cp in first:
            cp.start()
        passed = [copy(4 + j, (*chip, c), sibling) for j, chip in enumerate(chips)]
        for j, chip in enumerate(chips):
            copy(1 + j, (*chip, c), me).wait_recv()
            passed[j].start()
        copy(0, sibling, me).wait_recv()
        for j, chip in enumerate(chips):
            copy(4 + j, (*chip, 1 - c), me).wait_recv()
        for cp in first + passed:
            cp.wait_send()
        mine.wait()

    return pl.pallas_call(
        body, name=name, in_specs=[VMEM_SPEC], out_specs=VMEM_SPEC,
        out_shape=jax.ShapeDtypeStruct((N_DEV * m_per, n), block.dtype),
        scratch_shapes=[pltpu.SemaphoreType.DMA((7,)), pltpu.SemaphoreType.DMA((7,)), pltpu.SemaphoreType.DMA],
        compiler_params=_params(),
    )(block)


def _other_chips(x, y):
    return [(1 - x, y), (x, 1 - y), (1 - x, 1 - y)]


HBM_SPEC = pl.BlockSpec(memory_space=pltpu.HBM)
SEM_SPEC = pl.BlockSpec(memory_space=pltpu.SEMAPHORE)
DATAFLOW = pltpu.SideEffectType.DATAFLOW_SIDE_EFFECTING


def _chip_peers(x, y, c):
    return [(px, py, c) for px, py in _other_chips(x, y)]


def _sibling_peer(x, y, c):
    return [(x, y, 1 - c)]


def _weight_desc(_, k, land_ref, peer, me):
    h = land_ref.shape[1] // 2
    rows = pl.ds(me[2] * h, h)
    mine = land_ref.at[2 * me[0] + me[1], rows, :]
    return mine, mine, land_ref.at[2 * peer[0] + peer[1], rows, :]


def _grad_desc(psum_ref, k, land_ref, peer, me):
    return psum_ref.at[2 * peer[0] + peer[1]], land_ref.at[2 * me[0] + me[1]], land_ref.at[2 * peer[0] + peer[1]]


def _pair_desc(grad_ref, k, land_ref, peer, me):
    h = land_ref.shape[1]
    return grad_ref.at[:, pl.ds(peer[2] * h, h), :], land_ref, land_ref


def _whole_desc(src_ref, k, land_ref, peer, me):
    return src_ref, land_ref, land_ref


def exchange_start(srcs, lands, units, groups, desc, peers, after, name):
    n_s, n_l, n_g = len(srcs), len(lands), len(groups)
    n_p = len(peers(0, 0, 0))

    def body(*refs):
        s_refs, l_refs = refs[:n_s], refs[n_s:n_s + n_l]
        outs = refs[n_s + n_l + 1:]
        sems, token = outs[:2 * n_g], outs[-1]
        me = _coords()
        for g, ids in enumerate(groups):
            for i, u in enumerate(ids):
                si, k = units[u]
                for j, peer in enumerate(peers(*me)):
                    src, dst, _ = desc(s_refs[si] if s_refs else None, k, l_refs[u], peer, me)
                    pltpu.make_async_remote_copy(
                        src_ref=src, dst_ref=dst, send_sem=sems[2 * g].at[n_p * i + j],
                        recv_sem=sems[2 * g + 1].at[n_p * i + j], device_id=peer, device_id_type=MESH).start()
        token[...] = jnp.zeros_like(token)

    arrs = list(srcs) + list(lands)
    sem_shapes = [pltpu.SemaphoreType.DMA((n_p * len(ids),)) for ids in groups for _ in range(2)]
    outs = pl.pallas_call(
        body, name=name,
        in_specs=[HBM_SPEC] * len(arrs) + [ANY],
        out_specs=[SEM_SPEC] * (2 * n_g) + [HBM_SPEC] * len(arrs) + [VMEM_SPEC],
        out_shape=sem_shapes + [pltpu.HBM(a.shape, a.dtype) for a in arrs] + [jax.ShapeDtypeStruct((8, LANES), F32)],
        input_output_aliases={i: 2 * n_g + i for i in range(len(arrs))},
        compiler_params=pltpu.CompilerParams(has_side_effects=DATAFLOW),
    )(*[pltpu.with_memory_space_constraint(a, pltpu.HBM) for a in arrs], after)
    sems = outs[:2 * n_g]
    thru = outs[2 * n_g:2 * n_g + len(arrs)]
    return sems, list(thru[:n_s]), list(thru[n_s:]), outs[-1]


def exchange_wait(srcs, lands, units, send_sem, recv_sem, desc, peers, after, name):
    n_s, n_l = len(srcs), len(lands)
    n_p = len(peers(0, 0, 0))

    def body(*refs):
        s_refs, l_refs = refs[:n_s], refs[n_s:n_s + n_l]
        send_sems, recv_sems = refs[n_s + n_l], refs[n_s + n_l + 1]
        me = _coords()
        for i, (si, k) in enumerate(units):
            for j, peer in enumerate(peers(*me)):
                src, _, mine = desc(s_refs[si] if s_refs else None, k, l_refs[i], peer, me)
                cp = pltpu.make_async_remote_copy(
                    src_ref=src, dst_ref=mine, send_sem=send_sems.at[n_p * i + j], recv_sem=recv_sems.at[n_p * i + j],
                    device_id=peer, device_id_type=MESH)
                cp.wait_send()
                cp.wait_recv()

    arrs = list(srcs) + list(lands)
    outs = pl.pallas_call(
        body, name=name,
        in_specs=[HBM_SPEC] * len(arrs) + [SEM_SPEC, SEM_SPEC, ANY],
        out_specs=[HBM_SPEC] * len(arrs),
        out_shape=[pltpu.HBM(a.shape, a.dtype) for a in arrs],
        input_output_aliases={i: i for i in range(len(arrs))},
        compiler_params=pltpu.CompilerParams(has_side_effects=DATAFLOW),
    )(*arrs, send_sem, recv_sem, after)
    return list(outs[:n_s]), list(outs[n_s:])


def sibling_fill(lands, name):
    n_u = len(lands)

    def body(*refs):
        ins, outs = refs[:n_u], refs[n_u:2 * n_u]
        send_sems, recv_sems = refs[2 * n_u:]
        x, y, c = _coords()
        sends = []
        for u in range(n_u):
            h = ins[u].shape[1] // 2
            for j, (px, py) in enumerate(_other_chips(x, y)):
                part = (2 * px + py, pl.ds(c * h, h), slice(None))
                cp = pltpu.make_async_remote_copy(
                    src_ref=ins[u].at[part], dst_ref=outs[u].at[part], send_sem=send_sems.at[3 * u + j],
                    recv_sem=recv_sems.at[3 * u + j], device_id=(x, y, 1 - c), device_id_type=MESH)
                cp.start()
                sends.append(cp)
        for u in range(n_u):
            h = ins[u].shape[1] // 2
            for j, (px, py) in enumerate(_other_chips(x, y)):
                theirs = (2 * px + py, pl.ds((1 - c) * h, h), slice(None))
                pltpu.make_async_remote_copy(
                    src_ref=ins[u].at[theirs], dst_ref=outs[u].at[theirs], send_sem=send_sems.at[3 * u + j],
                    recv_sem=recv_sems.at[3 * u + j], device_id=(x, y, 1 - c), device_id_type=MESH).wait_recv()
        for cp in sends:
            cp.wait_send()

    return pl.pallas_call(
        body, name=name, in_specs=[ANY] * n_u, out_specs=[ANY] * n_u,
        out_shape=[jax.ShapeDtypeStruct(a.shape, a.dtype) for a in lands],
        input_output_aliases={i: i for i in range(n_u)},
        scratch_shapes=[pltpu.SemaphoreType.DMA((3 * n_u,)), pltpu.SemaphoreType.DMA((3 * n_u,))],
        compiler_params=_params(),
    )(*lands)


def _pack_rows(parts):
    flat = jnp.concatenate([p.reshape(-1).astype(F32) for p in parts])
    n = flat.shape[0]
    padded = -(-n // (8 * LANES)) * (8 * LANES)
    return jnp.pad(flat, (0, padded - n)).reshape(-1, LANES)


def _unpack_rows(packed, shapes):
    flat = packed.reshape(-1)
    out, off = [], 0
    for s in shapes:
        size = 1
        for d in s:
            size *= d
        out.append(flat[off:off + size].reshape(s))
        off += size
    return out


def _shard_last(full, s_me):
    n = full.shape[-1] // N_CHIP
    return lax.dynamic_slice_in_dim(full, s_me * n, n, axis=full.ndim - 1)


def _unshard_last(g):
    moved = jnp.moveaxis(g, 0, -2)
    return moved.reshape(moved.shape[:-2] + (moved.shape[-2] * moved.shape[-1],))


def kernel(x, c, w_ada, b_ada, ln_g, ln_b, ffn_gu, ffn_down, gmlp_w_in, gmlp_b_in, gmlp_ln_g, gmlp_ln_b, gmlp_w_s, gmlp_b_s, gmlp_w_out, w_ada_kv, b_ada_kv, w_kv, attn_w_q, attn_rel_bias, attn_w_o, loss_target, m_w_ada, m_b_ada, m_ln_g, m_ln_b, m_ffn_gu, m_ffn_down, m_gmlp_w_in, m_gmlp_b_in, m_gmlp_ln_g, m_gmlp_ln_b, m_gmlp_w_s, m_gmlp_b_s, m_gmlp_w_out, m_w_ada_kv, m_b_ada_kv, m_w_kv, m_attn_w_q, m_attn_rel_bias, m_attn_w_o, v_w_ada, v_b_ada, v_ln_g, v_ln_b, v_ffn_gu, v_ffn_down, v_gmlp_w_in, v_gmlp_b_in, v_gmlp_ln_g, v_gmlp_ln_b, v_gmlp_w_s, v_gmlp_b_s, v_gmlp_w_out, v_w_ada_kv, v_b_ada_kv, v_w_kv, v_attn_w_q, v_attn_rel_bias, v_attn_w_o):
    xi, yi, ci = _coords()
    s_me = 2 * xi + yi
    dev = 4 * xi + 2 * yi + ci

    x0 = x[0]
    tgt = loss_target[0]
    S, D = x0.shape
    L = w_ada.shape[0]
    NA = gmlp_w_in.shape[0]
    NB = attn_w_q.shape[0]
    NH = D // HEAD_DIM
    alpha = (2.0 * L) ** 0.25
    n_ada = w_ada.shape[2]
    n_kv = w_ada_kv.shape[1]

    stack_names = ["ffn_gu", "ffn_down", "gmlp_w_in", "gmlp_w_out", "w_kv", "attn_w_q", "attn_w_o"]
    stack_src = dict(ffn_gu=ffn_gu, ffn_down=ffn_down, gmlp_w_in=gmlp_w_in, gmlp_w_out=gmlp_w_out, w_kv=w_kv[None],
                     attn_w_q=attn_w_q, attn_w_o=attn_w_o)
    stacks = [stack_src[nm].astype(BF16).reshape((-1,) + stack_src[nm].shape[-2:]) for nm in stack_names]
    units = [(si, k) for si, st in enumerate(stacks) for k in range(st.shape[0])]
    unit_of = {(stack_names[si], k): u for u, (si, k) in enumerate(units)}
    assert NA >= 1
    weight_groups = [[("ffn_gu", 0)], [("ffn_down", 0)],
                     [("gmlp_w_in", 0), ("gmlp_w_out", 0), ("ffn_gu", 1), ("ffn_down", 1)]]
    for l in range(1, L):
        names = [("w_kv", 0)] if l == NA else []
        names += [("ffn_gu", 2 * l), ("ffn_down", 2 * l)]
        names += [("gmlp_w_in", l), ("gmlp_w_out", l)] if l < NA else [("attn_w_q", l - NA), ("attn_w_o", l - NA)]
        names += [("ffn_gu", 2 * l + 1), ("ffn_down", 2 * l + 1)]
        weight_groups.append(names)
    weight_groups = [[unit_of[n] for n in names] for names in weight_groups]

    c_all = all_gather8(jnp.broadcast_to(c, (8, D)), "ag_c").reshape(N_DEV, 8, D)[:, 0]
    b_ada_sh = lax.dynamic_slice_in_dim(b_ada, s_me * n_ada, n_ada, axis=1)
    b_kv_sh = lax.dynamic_slice_in_dim(b_ada_kv, s_me * n_kv, n_kv, axis=0)
    mod_part = ada_fwd(c_all, w_ada, b_ada_sh[:, None, :], "ada_fwd")
    mkv_part = ada_fwd(c_all, w_ada_kv[None], b_kv_sh[None, None, :], "ada_kv_fwd")
    part = jnp.concatenate([jnp.transpose(mod_part, (1, 0, 2)).reshape(N_DEV, L * n_ada), mkv_part[0]], axis=1)
    width = part.shape[1]
    pad_w = -(-width // LANES) * LANES - width
    all_part = all_gather8(jnp.pad(part, ((0, 0), (0, pad_w))), "ag_mod").reshape(N_DEV, N_DEV, width + pad_w)
    mine = lax.dynamic_index_in_dim(all_part[0::2], dev, axis=1, keepdims=False)
    mod = jnp.transpose(mine[:, :L * n_ada].reshape(N_CHIP, L, n_ada), (1, 0, 2)).reshape(L, N_MOD, D)
    mkv = mine[:, L * n_ada:width].reshape(2, D)

    def mrow(l, k):
        return mod[l, k][None, :]

    small_shapes = [ln_g.shape, ln_b.shape, gmlp_b_in.shape, gmlp_ln_g.shape, gmlp_ln_b.shape, attn_rel_bias.shape]
    small_pack = _pack_rows([ln_g, ln_b, gmlp_b_in, gmlp_ln_g, gmlp_ln_b, attn_rel_bias])
    small_all = all_gather8(small_pack, "ag_small_params").reshape((N_DEV,) + small_pack.shape)[0::2]
    sm = [_unpack_rows(small_all[s], small_shapes) for s in range(N_CHIP)]
    ln_g_f, ln_b_f, b_in_f, gln_g_f, gln_b_f, rel_f = [
        _unshard_last(jnp.stack([sm[s][i] for s in range(N_CHIP)])) for i in range(len(small_shapes))]

    lands0 = [lax.dynamic_update_slice(lax.empty((N_CHIP,) + stacks[si].shape[1:], BF16), stacks[si][k][None],
                                       (s_me, 0, 0)) for si, k in units]
    gathers_done = jnp.concatenate([mod.reshape(-1)[:LANES], small_all.reshape(-1)[:LANES]])
    w_sems, _, lands_t, _ = exchange_start([], lands0, units, weight_groups, _weight_desc, _chip_peers, gathers_done,
                                           "weight_send_start")
    wg = {}

    def fetch_weights(g, after):
        ids = weight_groups[g]
        _, got = exchange_wait([], [lands_t[u] for u in ids], [units[u] for u in ids], w_sems[2 * g],
                               w_sems[2 * g + 1], _weight_desc, _chip_peers, after, "weight_send_wait_%d" % g)
        for u, a in zip(ids, sibling_fill(got, "weight_sibling_fill")):
            wg[u] = a

    def W(nm, k):
        return wg[unit_of[(nm, k)]]

    def Wrows(nm, k):
        w4 = W(nm, k)
        return w4.reshape(w4.shape[0] * w4.shape[1], w4.shape[2])

    bst = [jnp.transpose(gmlp_b_s[j]) for j in range(NA)]
    biases = {}

    def make_bias(j, dep):
        rel, _ = lax.optimization_barrier((rel_f[j], dep))
        biases[j] = jnp.transpose(bias_expand(rel, "bias_expand"), (1, 0, 2))
        return biases[j]

    saved = []
    xc = x0
    kpad = vpad = xkv = None
    for l in range(L):
        after = mod if l == 0 else xc
        if l == 1 and NB > 1:
            after = make_bias(1, xc)
        fetch_weights(0 if l == 0 else l + 2, after)
        if l == NA:
            xkv = xc
            kv, hkv = mod_matmul(xc, mkv[1][None], mkv[0][None], W("w_kv", 0), None, BF16, "kv_proj")
            kpad = jnp.pad(kv[:, :D], ((LEFT_PAD, 0), (0, 0)))
            vpad = jnp.pad(kv[:, D:], ((LEFT_PAD, 0), (0, 0)))
        sv = {}
        for i in (0, 2):
            k = 2 * l + i // 2
            gu, hv = mod_matmul(xc, mrow(l, 3 * i + 1), mrow(l, 3 * i), W("ffn_gu", k), None, BF16, "ffn_up")
            if l == 0 and i == 0:
                fetch_weights(1, hv)
            gw = 0.5 * (1.0 + mrow(l, 3 * i + 2))
            xn, xh, rs, yv, av = matmul_res_ln(gu, Wrows("ffn_down", k), xc, gw, ln_g_f[l, i][None],
                                               ln_b_f[l, i][None], alpha, True, "ffn_down")
            sv[i] = dict(x=xc, h=hv, gu=gu, a=av, xh=xh, rs=rs, y=yv, gw=gw)
            xc = xn
            if i == 0:
                if l == 0:
                    fetch_weights(2, make_bias(0, xc) if NB > 0 else xc)
                gw = 1.0 + mrow(l, 5)
                if l < NA:
                    pre, hv = mod_matmul(xc, mrow(l, 4), mrow(l, 3), W("gmlp_w_in", l), b_in_f[l][None], F32,
                                         "gmlp_in")
                    qv = sgu_fwd(pre, gln_g_f[l][None], gln_b_f[l][None], gmlp_w_s[l], bst[l], "sgu_fwd")
                    xn, xh, rs, yv = matmul_res_ln(qv, Wrows("gmlp_w_out", l), xc, gw, ln_g_f[l, 1][None],
                                                   ln_b_f[l, 1][None], alpha, False, "gmlp_out")
                    sv[1] = dict(x=xc, h=hv, pre=pre, a=qv, xh=xh, rs=rs, y=yv, gw=gw)
                else:
                    j = l - NA
                    if j not in biases:
                        make_bias(j, xc)
                    qh, hv = mod_matmul(xc, mrow(l, 4), mrow(l, 3), Wrows("attn_w_q", j)[None], None, BF16, "attn_q")
                    ov = attn_fwd(qh, kpad, vpad, biases[j], "attn_fwd")
                    xn, xh, rs, yv = matmul_res_ln(ov, Wrows("attn_w_o", j), xc, gw, ln_g_f[l, 1][None],
                                                   ln_b_f[l, 1][None], alpha, False, "attn_out")
                    sv[1] = dict(x=xc, h=hv, q=qh, a=ov, xh=xh, rs=rs, y=yv, gw=gw)
                xc = xn
        saved.append(sv)

    dx, lacc = loss_grad(xc, tgt, "loss_grad")
    loss = lax.psum((0.5 / D) * jnp.sum(lacc[0]), ("x", "y", "c"))

    gpair = [None] * len(units)
    col_split = {u for u, (si, _) in enumerate(units) if stack_names[si] != "ffn_gu"}
    dmod = [[None] * N_MOD for _ in range(L)]
    d_ln_g = [[None] * 3 for _ in range(L)]
    d_ln_b = [[None] * 3 for _ in range(L)]
    d_b_in, d_gln_g, d_gln_b, d_ws, d_bs, d_rel = ([None] * NA, [None] * NA, [None] * NA, [None] * NA, [None] * NA,
                                                  [None] * NB)
    dk = jnp.zeros((S, D), F32)
    dv = jnp.zeros((S, D), F32)
    dmkv = None

    made = []

    core_idx = ci.astype(jnp.int32).reshape(1)
    chip_idx = s_me.astype(jnp.int32).reshape(1)

    def put(nm, k, a, b, name):
        u = unit_of[(nm, k)]
        rows, cols = stacks[units[u][0]].shape[1:]
        if nm == "ffn_gu":
            g = wgrad_pair(b, a, N_CHIP, cols, rows // 2, lambda j, p: j, lambda j, p: p, core_idx, name)
        elif nm in ("gmlp_w_in", "w_kv"):
            g = wgrad_pair(a, b, N_CHIP, rows, cols // 2, lambda j, p: 0, lambda j, p: 2 * j + p, core_idx, name)
        else:
            g = wgrad_pair(a, b, 1, N_CHIP * rows, cols // 2, lambda j, p: 0, lambda j, p: p, core_idx, name)
        gpair[u] = g.reshape((N_CHIP, -1, g.shape[-1]))
        made.append(u)

    own_half, sib_half = {}, {}
    n_started = [0]

    def start_grad_exchange(ids, after):
        psums = [gpair[u] for u in ids]
        n = len(ids)
        tag = n_started[0]
        n_started[0] += 1
        sems, ps_t, q_t, token = exchange_start(psums, [lax.empty(p.shape, p.dtype) for p in psums],
                                                [(i, 0) for i in range(n)], [list(range(n))], _grad_desc, _chip_peers,
                                                after, "grad_send_start_%d" % tag)
        return dict(ids=ids, tag=tag, sems=sems, ps=ps_t, q=q_t), token

    def finish_grad_exchange(pend, after):
        n = len(pend["ids"])
        ps_t, q = exchange_wait(pend["ps"], pend["q"], [(i, 0) for i in range(n)], pend["sems"][0], pend["sems"][1],
                                _grad_desc, _chip_peers, after, "grad_send_wait_%d" % pend["tag"])
        halves = [chip_sum(ps_t[i], q[i], chip_idx, u not in col_split, "grad_chip_sum")
                  for i, u in enumerate(pend["ids"])]
        sems, h_t, land_t, token = exchange_start(halves, [lax.empty(h.shape, h.dtype) for h in halves],
                                              [(i, 0) for i in range(n)], [list(range(n))], _whole_desc,
                                              _sibling_peer, halves[0], "half_send_start_%d" % pend["tag"])
        swaps.append(dict(ids=pend["ids"], tag=pend["tag"], sems=sems, h=h_t, land=land_t))
        return token

    def collect_halves(after):
        for sw in swaps:
            n = len(sw["ids"])
            h, land = exchange_wait(sw["h"], sw["land"], [(i, 0) for i in range(n)], sw["sems"][0], sw["sems"][1],
                                    _whole_desc, _sibling_peer, after, "half_send_wait_%d" % sw["tag"])
            for u, mine, theirs in zip(sw["ids"], h, land):
                own_half[u], sib_half[u] = mine, theirs
        swaps.clear()

    swaps = []
    pending = None
    started_before = jnp.zeros((8, LANES), F32)

    def ln_inputs(l, i):
        t = saved[l][i]
        return (t["xh"], t["rs"], ln_g_f[l, i][None], t["y"], t["gw"], 1.0 if i == 1 else 0.5)

    def record_ln(l, i, acc, row0):
        d_ln_g[l][i], d_ln_b[l][i], dmod[l][3 * i + 2] = acc[row0], acc[row0 + 1], acc[row0 + 2]

    ln_done = None
    for l in reversed(range(L)):
        if l == NA - 1:
            dkv = jnp.concatenate([dk, dv], axis=1)
            put("w_kv", 0, hkv, dkv, "kv_wgrad")
            pdxa, pdy, acc = dgrad_mod(dkv, W("w_kv", 0), dx, xkv, mkv[1][None], ln_inputs(l, 2), alpha, "kv_dgrad")
            dmkv = jnp.stack([acc[1], acc[0]])
            record_ln(l, 2, acc, 2)
            ln_done = (pdxa, pdy)
        sv = saved[l]
        for i in (2, 1, 0):
            t = sv[i]
            if ln_done is None:
                dxa, dy, acc1 = ln_res_bwd(dx, *ln_inputs(l, i), alpha, "ln_res_bwd")
                record_ln(l, i, acc1, 0)
            else:
                dxa, dy = ln_done
                ln_done = None
            before = (l, i - 1) if i > 0 else ((l - 1, 2) if l > 0 and l != NA else None)
            prev = ln_inputs(*before) if before is not None else None
            scl = mrow(l, 3 * i + 1)
            if i != 1:
                k = 2 * l + i // 2
                F = t["gu"].shape[1] // 2
                dgu = ffn_act_bwd(dy, Wrows("ffn_down", k), t["gu"], started_before, "ffn_act_bwd")
                put("ffn_down", k, t["a"], dy, "ffn_down_wgrad")
                put("ffn_gu", k, t["h"], dgu, "ffn_up_wgrad")
                res = dgrad_mod(dgu, W("ffn_gu", k), dxa, t["x"], scl, prev, alpha, "ffn_up_dgrad")
            elif l < NA:
                dq = matmul_nt(dy, Wrows("gmlp_w_out", l), started_before, "gmlp_out_dgrad")
                put("gmlp_w_out", l, t["a"], dy, "gmlp_out_wgrad")
                dpre, dws_l, dss, dgl, dbin = sgu_bwd(dq, t["pre"], gln_g_f[l][None], gln_b_f[l][None], gmlp_w_s[l],
                                                      bst[l], "sgu_bwd")
                d_ws[l] = dws_l
                d_bs[l] = jnp.transpose(group_lane_sum(dss, "sgu_bias_grad")[:, :GMLP_GROUPS])
                d_gln_g[l], d_gln_b[l], d_b_in[l] = dgl[0], dgl[1], dbin[0]
                put("gmlp_w_in", l, t["h"], dpre, "gmlp_in_wgrad")
                res = dgrad_mod(dpre, W("gmlp_w_in", l), dxa, t["x"], scl, prev, alpha, "gmlp_in_dgrad")
            else:
                j = l - NA
                do = matmul_nt(dy, Wrows("attn_w_o", j), started_before, "attn_out_dgrad")
                put("attn_w_o", j, t["a"], dy, "attn_out_wgrad")
                dqh, dk, dv, dbias = attn_bwd(t["q"], do, kpad, vpad, biases[j], dk, dv, "attn_bwd")
                d_rel[j] = bias_grad(jnp.transpose(dbias, (1, 0, 2)), "bias_grad")
                put("attn_w_q", j, t["h"], dqh, "attn_q_wgrad")
                res = dgrad_mod(dqh, Wrows("attn_w_q", j)[None], dxa, t["x"], scl, prev, alpha, "attn_q_dgrad")
            acc2 = res[-1]
            dmod[l][3 * i + 1], dmod[l][3 * i] = acc2[0], acc2[1]
            if before is None:
                dx = res[0]
            else:
                record_ln(*before, acc2, 2)
                ln_done = (res[0], res[1])
            if (i == 0 and l > 0) or (i == 1 and l == 0):
                started, started_before = start_grad_exchange(list(made), acc2)
                made.clear()
                if pending is not None:
                    started_before = started_before + finish_grad_exchange(pending, acc2)
                pending = started
    grad_x = dx[None]

    dvec = _pack_rows([jnp.stack([jnp.stack(r) for r in dmod]), dmkv])
    dvec = lax.optimization_barrier((dvec, [gpair[u] for u in made]))[0]
    n_dvec = L * N_MOD * D + 2 * D
    dall = all_gather8(dvec, "ag_dmod").reshape(N_DEV, -1, LANES)
    db_all = sum_leading(dall, "ada_bias_grad").reshape(-1)[:n_dvec]
    g_b_ada = db_all[:L * N_MOD * D].reshape(L, N_MOD * D)
    g_b_ada_kv = db_all[L * N_MOD * D:]
    dall2 = dall.reshape(N_DEV, -1)[:, :n_dvec]
    dmod_all = dall2[:, :L * N_MOD * D].reshape(N_DEV, L, N_MOD * D)
    dmod_sh = jnp.transpose(lax.dynamic_slice_in_dim(dmod_all, s_me * n_ada, n_ada, axis=2), (1, 0, 2))
    dmkv_sh = lax.dynamic_slice_in_dim(dall2[:, L * N_MOD * D:], s_me * n_kv, n_kv, axis=1)[None]
    c_all_t = jnp.transpose(c_all)
    g_w_ada = ada_wgrad(c_all_t, dmod_sh, "ada_wgrad")
    g_w_ada_kv = ada_wgrad(c_all_t, dmkv_sh, "ada_kv_wgrad")[0]

    small_g = [jnp.stack([jnp.stack(r) for r in d_ln_g]), jnp.stack([jnp.stack(r) for r in d_ln_b]),
               jnp.stack(d_b_in), jnp.stack(d_gln_g), jnp.stack(d_gln_b), jnp.stack(d_rel), jnp.stack(d_ws),
               jnp.stack(d_bs)]
    sg_shapes = [a.shape for a in small_g]
    sg_pack = _pack_rows(small_g)
    sg_all = all_gather8(sg_pack, "ag_small_grads").reshape(N_DEV, -1, LANES)
    sg_sum = _unpack_rows(sum_leading(sg_all, "small_grad_sum"), sg_shapes)
    g_ln_g, g_ln_b, g_b_in, g_gln_g, g_gln_b, g_rel = [_shard_last(a, s_me) for a in sg_sum[:6]]
    g_ws, g_bs = sg_sum[6], sg_sum[7]

    last, _ = start_grad_exchange(list(made), sg_all)

    grads = dict(w_ada=g_w_ada, b_ada=g_b_ada, ln_g=g_ln_g, ln_b=g_ln_b, gmlp_b_in=g_b_in, gmlp_ln_g=g_gln_g,
                 gmlp_ln_b=g_gln_b, gmlp_w_s=g_ws, gmlp_b_s=g_bs, w_ada_kv=g_w_ada_kv, b_ada_kv=g_b_ada_kv,
                 attn_rel_bias=g_rel)
    weights = dict(w_ada=w_ada, b_ada=b_ada, ln_g=ln_g, ln_b=ln_b, ffn_gu=ffn_gu, ffn_down=ffn_down,
                   gmlp_w_in=gmlp_w_in, gmlp_b_in=gmlp_b_in, gmlp_ln_g=gmlp_ln_g, gmlp_ln_b=gmlp_ln_b,
                   gmlp_w_s=gmlp_w_s, gmlp_b_s=gmlp_b_s, gmlp_w_out=gmlp_w_out, w_ada_kv=w_ada_kv,
                   b_ada_kv=b_ada_kv, w_kv=w_kv, attn_w_q=attn_w_q, attn_rel_bias=attn_rel_bias, attn_w_o=attn_w_o)
    ms = dict(w_ada=m_w_ada, b_ada=m_b_ada, ln_g=m_ln_g, ln_b=m_ln_b, ffn_gu=m_ffn_gu, ffn_down=m_ffn_down,
              gmlp_w_in=m_gmlp_w_in, gmlp_b_in=m_gmlp_b_in, gmlp_ln_g=m_gmlp_ln_g, gmlp_ln_b=m_gmlp_ln_b,
              gmlp_w_s=m_gmlp_w_s, gmlp_b_s=m_gmlp_b_s, gmlp_w_out=m_gmlp_w_out, w_ada_kv=m_w_ada_kv,
              b_ada_kv=m_b_ada_kv, w_kv=m_w_kv, attn_w_q=m_attn_w_q, attn_rel_bias=m_attn_rel_bias,
              attn_w_o=m_attn_w_o)
    vs = dict(w_ada=v_w_ada, b_ada=v_b_ada, ln_g=v_ln_g, ln_b=v_ln_b, ffn_gu=v_ffn_gu, ffn_down=v_ffn_down,
              gmlp_w_in=v_gmlp_w_in, gmlp_b_in=v_gmlp_b_in, gmlp_ln_g=v_gmlp_ln_g, gmlp_ln_b=v_gmlp_ln_b,
              gmlp_w_s=v_gmlp_w_s, gmlp_b_s=v_gmlp_b_s, gmlp_w_out=v_gmlp_w_out, w_ada_kv=v_w_ada_kv,
              b_ada_kv=v_b_ada_kv, w_kv=v_w_kv, attn_w_q=v_attn_w_q, attn_rel_bias=v_attn_rel_bias,
              attn_w_o=v_attn_w_o)
    order = ["w_ada", "b_ada", "ln_g", "ln_b", "ffn_gu", "ffn_down", "gmlp_w_in", "gmlp_b_in", "gmlp_ln_g",
             "gmlp_ln_b", "gmlp_w_s", "gmlp_b_s", "gmlp_w_out", "w_ada_kv", "b_ada_kv", "w_kv", "attn_w_q",
             "attn_rel_bias", "attn_w_o"]
    big_names = ["w_ada", "w_ada_kv"] + stack_names
    small_names = [nm for nm in order if nm not in big_names]
    delta, new_m, new_v = {}, {}, {}

    def adamw_big(nm):
        shp = weights[nm].shape
        two_d = (-1, shp[-1])
        d, a, b = adamw(weights[nm].reshape(two_d), grads[nm].reshape(two_d), ms[nm].reshape(two_d),
                        vs[nm].reshape(two_d), "adamw")
        delta[nm], new_m[nm], new_v[nm] = d.reshape(shp), a.reshape(shp), b.reshape(shp)

    adamw_big("w_ada")
    adamw_big("w_ada_kv")
    shapes = [weights[nm].shape for nm in small_names]
    d, a, b = adamw(_pack_rows([weights[nm] for nm in small_names]), _pack_rows([grads[nm] for nm in small_names]),
                    _pack_rows([ms[nm] for nm in small_names]), _pack_rows([vs[nm] for nm in small_names]),
                    "adamw_small")
    for nm, dd, aa, bb in zip(small_names, _unpack_rows(d, shapes), _unpack_rows(a, shapes), _unpack_rows(b, shapes)):
        delta[nm], new_m[nm], new_v[nm] = dd, aa, bb

    def full_grad(u):
        lo = jnp.where(ci == 0, own_half[u], sib_half[u])
        hi = jnp.where(ci == 0, sib_half[u], own_half[u])
        return jnp.concatenate([lo, hi], axis=1 if u in col_split else 0)

    def adamw_stack(nm):
        si = stack_names.index(nm)
        g = jnp.stack([full_grad(unit_of[(nm, k)]) for k in range(stacks[si].shape[0])])
        grads[nm] = g.reshape(weights[nm].shape)
        adamw_big(nm)

    late = [stack_names[units[u][0]] for u in last["ids"]]
    early = [nm for nm in stack_names if nm not in late]
    finish_grad_exchange(pending, delta["w_ada"])
    collect_halves(delta["w_ada"])
    for nm in early:
        adamw_stack(nm)
    finish_grad_exchange(last, delta[early[-1]])
    collect_halves(delta[early[-1]])
    for nm in stack_names:
        if nm in late:
            adamw_stack(nm)

    return (loss, grad_x, *[grads[nm] for nm in order], *[delta[nm] for nm in order],
            *[new_m[nm] for nm in order], *[new_v[nm] for nm in order])
```

```python
import functools

import jax
import jax.numpy as jnp
from jax import lax
from jax.experimental import pallas as pl
from jax.experimental.pallas import tpu as pltpu

F32 = jnp.float32
BF16 = jnp.bfloat16
MESH = pl.DeviceIdType.MESH
HIGHEST = lax.Precision.HIGHEST

CHUNK = 64
GMLP_WINDOW = 128
GMLP_GROUPS = 8
HEAD_DIM = 64
LEFT_CHUNKS = 8
BAND = (LEFT_CHUNKS + 1) * CHUNK
LEFT_PAD = LEFT_CHUNKS * CHUNK
MAX_REL = 4 * CHUNK
N_REL = (CHUNK - 1) + MAX_REL + 1
LN_EPS = 1e-5
N_MOD = 9
N_DEV = 8
N_CHIP = 4

ADAM_LR = 0.001
ADAM_B1 = 0.9
ADAM_B2 = 0.999
ADAM_EPS = 1e-08
ADAM_WD = 0.01
ADAM_STEP = 10

LANES = 128
ROW_TILE = 256
MATMUL_ROW_TILE = 512
WGRAD_ROWS = 1024
ATTN_CHUNKS_PER_STEP = 4
VMEM_LIMIT_MB = 56

NT = (((1,), (1,)), ((), ()))
TN = (((0,), (0,)), ((), ()))

ANY = pl.BlockSpec(memory_space=pl.ANY)
VMEM_SPEC = pl.BlockSpec(memory_space=pltpu.VMEM)


def _params(semantics=None):
    kw = dict(vmem_limit_bytes=VMEM_LIMIT_MB * 1024 * 1024)
    if semantics is not None:
        kw["dimension_semantics"] = semantics
    return pltpu.CompilerParams(**kw)


def _sigmoid(v):
    return 0.5 * (1.0 + jnp.tanh(0.5 * v))


def _gelu(v):
    return 0.5 * v * (1.0 + lax.erf(v * (2.0 ** -0.5)))


def _gelu_grad(v):
    return 0.5 * (1.0 + lax.erf(v * (2.0 ** -0.5))) + v * jnp.exp(-0.5 * v * v) * ((2.0 * jnp.pi) ** -0.5)


def _row(m):
    return lambda i: (i, 0)


def _fixed2(i):
    return (0, 0)


def _fixed3(i):
    return (0, 0, 0)


def _resident(shape):
    return pl.BlockSpec(shape, _fixed2 if len(shape) == 2 else _fixed3, pipeline_mode=pl.Buffered(1))


def mod_matmul(x, scl, shift, w, bias, out_dtype, name):
    S, D = x.shape
    NS, _, n = w.shape
    tm = min(MATMUL_ROW_TILE, S)
    has_bias = bias is not None

    def body(*refs):
        if has_bias:
            x_ref, scl_ref, sh_ref, w_ref, b_ref, o_ref, h_ref = refs
        else:
            x_ref, scl_ref, sh_ref, w_ref, o_ref, h_ref = refs
        h = (x_ref[...] * (1.0 + scl_ref[...]) + sh_ref[...]).astype(BF16)
        h_ref[...] = h
        for s in range(NS):
            acc = jnp.dot(h, w_ref[s], preferred_element_type=F32)
            if has_bias:
                acc = acc + b_ref[:, s * n:(s + 1) * n]
            o_ref[:, s * n:(s + 1) * n] = acc.astype(out_dtype)

    in_specs = [pl.BlockSpec((tm, D), _row(0)), pl.BlockSpec((1, D), _fixed2), pl.BlockSpec((1, D), _fixed2),
                _resident((NS, D, n))]
    args = [x, scl, shift, w]
    if has_bias:
        in_specs.append(pl.BlockSpec((1, NS * n), _fixed2))
        args.append(bias)
    return pl.pallas_call(
        body, name=name, grid=(S // tm,), in_specs=in_specs,
        out_specs=[pl.BlockSpec((tm, NS * n), _row(0)), pl.BlockSpec((tm, D), _row(0))],
        out_shape=[jax.ShapeDtypeStruct((S, NS * n), out_dtype), jax.ShapeDtypeStruct((S, D), BF16)],
        compiler_params=_params(("parallel",)),
    )(*args)


def matmul_res_ln(a, w, x, gw, lg, lb, alpha, swiglu, name):
    S, D = x.shape
    K = w.shape[0]
    tm = min(ROW_TILE, S)
    ka = a.shape[1]

    def body(a_ref, w_ref, x_ref, gw_ref, lg_ref, lb_ref, xn_ref, xh_ref, rs_ref, y_ref, *act_ref):
        if swiglu:
            g = a_ref[:, :K].astype(F32)
            u = a_ref[:, K:].astype(F32)
            act = (g * _sigmoid(g) * u).astype(BF16)
            act_ref[0][...] = act
        else:
            act = a_ref[...].astype(BF16)
        y = jnp.dot(act, w_ref[...], preferred_element_type=F32)
        z = alpha * x_ref[...] + gw_ref[...] * y
        mu = jnp.mean(z, axis=-1, keepdims=True)
        zc = z - mu
        var = jnp.mean(zc * zc, axis=-1, keepdims=True)
        rstd = lax.rsqrt(var + LN_EPS)
        xhat = zc * rstd
        xn_ref[...] = xhat * lg_ref[...] + lb_ref[...]
        xh_ref[...] = xhat
        rs_ref[...] = rstd
        y_ref[...] = y.astype(BF16)

    vec = pl.BlockSpec((1, D), _fixed2)
    out_specs = [pl.BlockSpec((tm, D), _row(0)), pl.BlockSpec((tm, D), _row(0)), pl.BlockSpec((tm, 1), _row(0)),
                 pl.BlockSpec((tm, D), _row(0))]
    out_shape = [jax.ShapeDtypeStruct((S, D), F32), jax.ShapeDtypeStruct((S, D), F32),
                 jax.ShapeDtypeStruct((S, 1), F32), jax.ShapeDtypeStruct((S, D), BF16)]
    if swiglu:
        out_specs.append(pl.BlockSpec((tm, K), _row(0)))
        out_shape.append(jax.ShapeDtypeStruct((S, K), BF16))
    return pl.pallas_call(
        body, name=name, grid=(S // tm,),
        in_specs=[pl.BlockSpec((tm, ka), _row(0)), _resident((K, D)), pl.BlockSpec((tm, D), _row(0)),
                  vec, vec, vec],
        out_specs=out_specs, out_shape=out_shape,
        compiler_params=_params(("parallel",)),
    )(a, w, x, gw, lg, lb)


def _ln_res_bwd_tile(d, xh_ref, rs_ref, lg_ref, y_ref, gw_ref, wres, alpha, dxa_ref, dy_ref, acc_ref, row0):
    xh = xh_ref[...]
    dxh = d * lg_ref[...]
    m1 = jnp.mean(dxh, axis=-1, keepdims=True)
    m2 = jnp.mean(dxh * xh, axis=-1, keepdims=True)
    dz = rs_ref[...] * (dxh - m1 - xh * m2)
    dxa_ref[...] = alpha * dz
    dy_ref[...] = (gw_ref[...] * dz).astype(BF16)
    acc_ref[row0:row0 + 1, :] += jnp.sum(d * xh, axis=0, keepdims=True)
    acc_ref[row0 + 1:row0 + 2, :] += jnp.sum(d, axis=0, keepdims=True)
    acc_ref[row0 + 2:row0 + 3, :] += jnp.sum((wres * dz) * y_ref[...].astype(F32), axis=0, keepdims=True)


def ln_res_bwd(dxn, xhat, rstd, lg, y, gw, wres, alpha, name):
    S, D = dxn.shape
    tm = min(MATMUL_ROW_TILE, S)

    def body(dxn_ref, xh_ref, rs_ref, lg_ref, y_ref, gw_ref, dxa_ref, dy_ref, acc_ref):
        @pl.when(pl.program_id(0) == 0)
        def _():
            acc_ref[...] = jnp.zeros_like(acc_ref)

        _ln_res_bwd_tile(dxn_ref[...], xh_ref, rs_ref, lg_ref, y_ref, gw_ref, wres, alpha, dxa_ref, dy_ref, acc_ref, 0)

    vec = pl.BlockSpec((1, D), _fixed2)
    tile = pl.BlockSpec((tm, D), _row(0))
    return pl.pallas_call(
        body, name=name, grid=(S // tm,),
        in_specs=[tile, tile, pl.BlockSpec((tm, 1), _row(0)), vec, tile, vec],
        out_specs=[tile, tile, pl.BlockSpec((8, D), _fixed2)],
        out_shape=[jax.ShapeDtypeStruct((S, D), F32), jax.ShapeDtypeStruct((S, D), BF16),
                   jax.ShapeDtypeStruct((8, D), F32)],
        compiler_params=_params(("arbitrary",)),
    )(dxn, xhat, rstd, lg, y, gw)


def ffn_act_bwd(dy, wd, gu, after, name):
    S, D = dy.shape
    K = wd.shape[0]
    tm = min(ROW_TILE, S)

    def body(dy_ref, wd_ref, gu_ref, after_ref, o_ref):
        da = lax.dot_general(dy_ref[...], wd_ref[...], NT, preferred_element_type=F32).astype(BF16)
        g = gu_ref[:, :K]
        u = gu_ref[:, K:]
        sg = _sigmoid(g)
        o_ref[:, :K] = da * u * (sg * (1.0 + g * (1.0 - sg)))
        o_ref[:, K:] = da * (g * sg)

    return pl.pallas_call(
        body, name=name, grid=(S // tm,),
        in_specs=[pl.BlockSpec((tm, D), _row(0)), _resident((K, D)), pl.BlockSpec((tm, 2 * K), _row(0)), ANY],
        out_specs=pl.BlockSpec((tm, 2 * K), _row(0)),
        out_shape=jax.ShapeDtypeStruct((S, 2 * K), BF16),
        compiler_params=_params(("parallel",)),
    )(dy, wd, gu, after)


def matmul_nt(a, w, after, name):
    S, D = a.shape
    K = w.shape[0]
    tm = min(MATMUL_ROW_TILE, S)

    def body(a_ref, w_ref, after_ref, o_ref):
        o_ref[...] = lax.dot_general(a_ref[...], w_ref[...], NT, preferred_element_type=F32).astype(BF16)

    return pl.pallas_call(
        body, name=name, grid=(S // tm,),
        in_specs=[pl.BlockSpec((tm, D), _row(0)), _resident((K, D)), ANY],
        out_specs=pl.BlockSpec((tm, K), _row(0)),
        out_shape=jax.ShapeDtypeStruct((S, K), BF16),
        compiler_params=_params(("parallel",)),
    )(a, w, after)


def dgrad_mod(dpre, w, dxa, xin, scl, prev, alpha, name):
    S, D = xin.shape
    NS, _, n = w.shape
    tm = min(ROW_TILE, S)
    wres = prev[5] if prev is not None else None

    def body(*refs):
        dp_ref, w_ref, dxa_ref, xin_ref, scl_ref = refs[:5]
        acc_ref = refs[-1]

        @pl.when(pl.program_id(0) == 0)
        def _():
            acc_ref[...] = jnp.zeros_like(acc_ref)

        dh = jnp.zeros((tm, D), F32)
        for s in range(NS):
            dh = dh + lax.dot_general(dp_ref[:, s * n:(s + 1) * n].astype(BF16), w_ref[s], NT,
                                      preferred_element_type=F32)
        dx = dxa_ref[...] + dh * (1.0 + scl_ref[...])
        acc_ref[0:1, :] += jnp.sum(dh * xin_ref[...], axis=0, keepdims=True)
        acc_ref[1:2, :] += jnp.sum(dh, axis=0, keepdims=True)
        if prev is None:
            refs[5][...] = dx
        else:
            xh_ref, rs_ref, lg_ref, y_ref, gw_ref, pdxa_ref, pdy_ref = refs[5:12]
            _ln_res_bwd_tile(dx, xh_ref, rs_ref, lg_ref, y_ref, gw_ref, wres, alpha, pdxa_ref, pdy_ref, acc_ref, 2)

    tile = pl.BlockSpec((tm, D), _row(0))
    vec = pl.BlockSpec((1, D), _fixed2)
    in_specs = [pl.BlockSpec((tm, NS * n), _row(0)), _resident((NS, D, n)), tile, tile, vec]
    args = [dpre, w, dxa, xin, scl]
    if prev is None:
        out_specs = [tile]
        out_shape = [jax.ShapeDtypeStruct((S, D), F32)]
    else:
        in_specs += [tile, pl.BlockSpec((tm, 1), _row(0)), vec, tile, vec]
        args += list(prev[:5])
        out_specs = [tile, tile]
        out_shape = [jax.ShapeDtypeStruct((S, D), F32), jax.ShapeDtypeStruct((S, D), BF16)]
    return pl.pallas_call(
        body, name=name, grid=(S // tm,), in_specs=in_specs,
        out_specs=out_specs + [pl.BlockSpec((8, D), _fixed2)],
        out_shape=out_shape + [jax.ShapeDtypeStruct((8, D), F32)],
        compiler_params=_params(("arbitrary",)),
    )(*args)


PAIR_COLLECTIVE_ID = 1


def wgrad_pair(a, b, J, kb, nb, a_block, b_block, half_idx, name):
    S = b.shape[0]
    ts = min(WGRAD_ROWS, S)
    nsteps = S // ts

    def body(h_ref, a_ref, b_ref, o_ref, acc_ref, send_buf, recv_buf, send_sems, recv_sems):
        jj, si = pl.program_id(0), pl.program_id(1)
        x, y, c = _coords()
        j = lax.rem(jj, J)
        last = si == nsteps - 1

        def copy(blk):
            return pltpu.make_async_remote_copy(
                src_ref=send_buf.at[blk], dst_ref=recv_buf.at[blk], send_sem=send_sems.at[blk],
                recv_sem=recv_sems.at[blk], device_id=(x, y, 1 - c), device_id_type=MESH)

        @pl.when(jnp.logical_and(jj == 0, si == 0))
        def _():
            barrier = pltpu.get_barrier_semaphore()
            pl.semaphore_signal(barrier, inc=1, device_id=(x, y, 1 - c), device_id_type=MESH)
            pl.semaphore_wait(barrier, 1)

        @pl.when(si == 0)
        def _():
            acc_ref[...] = jnp.zeros_like(acc_ref)

        acc_ref[...] += lax.dot_general(a_ref[...], b_ref[...].astype(BF16), TN, preferred_element_type=F32)

        @pl.when(jnp.logical_and(last, jj < J))
        def _():
            send_buf[j] = acc_ref[...].astype(BF16)
            copy(j).start()

        @pl.when(jnp.logical_and(last, jj >= J))
        def _():
            copy(j).wait_recv()
            o_ref[...] = (acc_ref[...] + recv_buf[j].astype(F32)).astype(BF16)

        @pl.when(jnp.logical_and(last, jj == 2 * J - 1))
        def _():
            for blk in range(J):
                copy(blk).wait_send()

    def half(jj, h):
        return jnp.where(jj < J, 1 - h[0], h[0])

    return pl.pallas_call(
        body, name=name,
        grid_spec=pltpu.PrefetchScalarGridSpec(
            num_scalar_prefetch=1, grid=(2 * J, nsteps),
            in_specs=[pl.BlockSpec((ts, kb), lambda jj, s, h: (s, a_block(lax.rem(jj, J), half(jj, h)))),
                      pl.BlockSpec((ts, nb), lambda jj, s, h: (s, b_block(lax.rem(jj, J), half(jj, h))))],
            out_specs=pl.BlockSpec((None, kb, nb), lambda jj, s, h: (jnp.maximum(jj - J, 0), 0, 0)),
            scratch_shapes=[pltpu.VMEM((kb, nb), F32), pltpu.VMEM((J, kb, nb), BF16), pltpu.VMEM((J, kb, nb), BF16),
                            pltpu.SemaphoreType.DMA((J,)), pltpu.SemaphoreType.DMA((J,))]),
        out_shape=jax.ShapeDtypeStruct((J, kb, nb), BF16),
        compiler_params=pltpu.CompilerParams(
            vmem_limit_bytes=VMEM_LIMIT_MB * 1024 * 1024, dimension_semantics=("arbitrary", "arbitrary"),
            collective_id=PAIR_COLLECTIVE_ID),
    )(half_idx, a, b)


def _window_mask():
    t = lax.broadcasted_iota(jnp.int32, (GMLP_WINDOW, GMLP_WINDOW), 0)
    s = lax.broadcasted_iota(jnp.int32, (GMLP_WINDOW, GMLP_WINDOW), 1)
    return ((s // CHUNK) <= (t // CHUNK)).astype(F32)


def sgu_fwd(pre, glg, glb, ws, bst, name):
    S, H2 = pre.shape
    H = H2 // 2
    W, G = GMLP_WINDOW, GMLP_GROUPS
    gd = H // G
    tm = min(ROW_TILE, S)

    def body(pre_ref, glg_ref, glb_ref, ws_ref, bst_ref, q_ref):
        u = _gelu(pre_ref[:, :H])
        v = _gelu(pre_ref[:, H:])
        mu = jnp.mean(v, axis=-1, keepdims=True)
        vc = v - mu
        var = jnp.mean(vc * vc, axis=-1, keepdims=True)
        vn = ((vc * lax.rsqrt(var + LN_EPS)) * glg_ref[...] + glb_ref[...]).astype(BF16)
        mask = _window_mask()
        for g in range(G):
            wsg = (ws_ref[g] * mask).astype(BF16)
            bcol = bst_ref[:, g:g + 1]
            for wi in range(tm // W):
                rows = slice(wi * W, (wi + 1) * W)
                cols = slice(g * gd, (g + 1) * gd)
                s = jnp.dot(wsg, vn[rows, cols], preferred_element_type=F32) + bcol
                q_ref[rows, cols] = (u[rows, cols] * s).astype(BF16)

    return pl.pallas_call(
        body, name=name, grid=(S // tm,),
        in_specs=[pl.BlockSpec((tm, H2), _row(0)), pl.BlockSpec((1, H), _fixed2), pl.BlockSpec((1, H), _fixed2),
                  pl.BlockSpec((G, W, W), _fixed3), pl.BlockSpec((W, G), _fixed2)],
        out_specs=pl.BlockSpec((tm, H), _row(0)),
        out_shape=jax.ShapeDtypeStruct((S, H), BF16),
        compiler_params=_params(("parallel",)),
    )(pre, glg, glb, ws, bst)


def sgu_bwd(dq, pre, glg, glb, ws, bst, name):
    S, H2 = pre.shape
    H = H2 // 2
    W, G = GMLP_WINDOW, GMLP_GROUPS
    gd = H // G
    tm = min(ROW_TILE, S)

    def body(dq_ref, pre_ref, glg_ref, glb_ref, ws_ref, bst_ref,
             dpre_ref, dws_ref, dss_ref, dgl_ref, dbin_ref, du_s, dvn_s):
        @pl.when(pl.program_id(0) == 0)
        def _():
            dws_ref[...] = jnp.zeros_like(dws_ref)
            dss_ref[...] = jnp.zeros_like(dss_ref)
            dgl_ref[...] = jnp.zeros_like(dgl_ref)
            dbin_ref[...] = jnp.zeros_like(dbin_ref)

        pu = pre_ref[:, :H]
        pv = pre_ref[:, H:]
        u = _gelu(pu)
        v = _gelu(pv)
        mu = jnp.mean(v, axis=-1, keepdims=True)
        vc = v - mu
        var = jnp.mean(vc * vc, axis=-1, keepdims=True)
        rstd = lax.rsqrt(var + LN_EPS)
        vhat = vc * rstd
        vn = (vhat * glg_ref[...] + glb_ref[...]).astype(BF16)
        mask = _window_mask()
        for g in range(G):
            wsg = (ws_ref[g] * mask).astype(BF16)
            bcol = bst_ref[:, g:g + 1]
            cols = slice(g * gd, (g + 1) * gd)
            for wi in range(tm // W):
                rows = slice(wi * W, (wi + 1) * W)
                vblk = vn[rows, cols]
                s = jnp.dot(wsg, vblk, preferred_element_type=F32) + bcol
                dqb = dq_ref[rows, cols].astype(F32)
                du_s[rows, cols] = dqb * s
                ds = dqb * u[rows, cols]
                dss_ref[:, cols] += ds
                dsb = ds.astype(BF16)
                dvn_s[rows, cols] = lax.dot_general(wsg, dsb, TN, preferred_element_type=F32)
                dws_ref[g] += lax.dot_general(dsb, vblk, NT, preferred_element_type=F32) * mask
        dvn = dvn_s[...]
        dgl_ref[0:1, :] += jnp.sum(dvn * vhat, axis=0, keepdims=True)
        dgl_ref[1:2, :] += jnp.sum(dvn, axis=0, keepdims=True)
        dvh = dvn * glg_ref[...]
        m1 = jnp.mean(dvh, axis=-1, keepdims=True)
        m2 = jnp.mean(dvh * vhat, axis=-1, keepdims=True)
        dv = rstd * (dvh - m1 - vhat * m2)
        dpu = du_s[...] * _gelu_grad(pu)
        dpv = dv * _gelu_grad(pv)
        dbin_ref[0:1, :H] += jnp.sum(dpu, axis=0, keepdims=True)
        dbin_ref[0:1, H:] += jnp.sum(dpv, axis=0, keepdims=True)
        dpre_ref[:, :H] = dpu.astype(BF16)
        dpre_ref[:, H:] = dpv.astype(BF16)

    return pl.pallas_call(
        body, name=name, grid=(S // tm,),
        in_specs=[pl.BlockSpec((tm, H), _row(0)), pl.BlockSpec((tm, H2), _row(0)), pl.BlockSpec((1, H), _fixed2),
                  pl.BlockSpec((1, H), _fixed2), pl.BlockSpec((G, W, W), _fixed3), pl.BlockSpec((W, G), _fixed2)],
        out_specs=[pl.BlockSpec((tm, H2), _row(0)), pl.BlockSpec((G, W, W), _fixed3), pl.BlockSpec((W, H), _fixed2),
                   pl.BlockSpec((8, H), _fixed2), pl.BlockSpec((8, H2), _fixed2)],
        out_shape=[jax.ShapeDtypeStruct((S, H2), BF16), jax.ShapeDtypeStruct((G, W, W), F32),
                   jax.ShapeDtypeStruct((W, H), F32), jax.ShapeDtypeStruct((8, H), F32),
                   jax.ShapeDtypeStruct((8, H2), F32)],
        scratch_shapes=[pltpu.VMEM((tm, H), F32), pltpu.VMEM((tm, H), F32)],
        compiler_params=_params(("arbitrary",)),
    )(dq, pre, glg, glb, ws, bst)


def group_lane_sum(dss, name):
    W, H = dss.shape
    gd = H // GMLP_GROUPS

    def body(d_ref, o_ref):
        j = lax.broadcasted_iota(jnp.int32, (H, LANES), 0)
        g = lax.broadcasted_iota(jnp.int32, (H, LANES), 1)
        ind = ((j // gd) == g).astype(F32)
        o_ref[...] = jnp.dot(d_ref[...], ind, preferred_element_type=F32, precision=HIGHEST)

    return pl.pallas_call(
        body, name=name, in_specs=[VMEM_SPEC], out_specs=VMEM_SPEC,
        out_shape=jax.ShapeDtypeStruct((W, LANES), F32), compiler_params=_params(),
    )(dss)


def _attn_load(j, cps, q_ref, k_ref, v_ref):
    r = lax.broadcasted_iota(jnp.int32, (CHUNK, BAND), 1)
    chunks = []
    for cc in range(cps):
        start = pl.multiple_of((j * cps + cc) * CHUNK, CHUNK)
        chunks.append((q_ref[cc * CHUNK:(cc + 1) * CHUNK, :], k_ref[pl.ds(start, BAND), :],
                       v_ref[pl.ds(start, BAND), :], (r + start) >= LEFT_PAD))
    return chunks


def _attn_probs(chunks, b_ref, sels, scale):
    qms = [[jnp.where(sel, q2, jnp.zeros_like(q2)) for sel in sels] for q2, _, _, _ in chunks]
    raw = [[lax.dot_general(qm, k2, NT, preferred_element_type=F32) for qm in qms[cc]]
           for cc, (_, k2, _, _) in enumerate(chunks)]
    probs = []
    for cc, (_, _, _, valid) in enumerate(chunks):
        row = []
        for sub in range(2):
            s = jnp.where(valid, raw[cc][sub] * scale + b_ref[sub], -jnp.inf)
            e = jnp.exp(s - jnp.max(s, axis=-1, keepdims=True))
            row.append(e / jnp.sum(e, axis=-1, keepdims=True))
        probs.append(row)
    return qms, probs


def attn_fwd(q, kpad, vpad, bias, name):
    S, D = q.shape
    HP = D // LANES
    cps = min(ATTN_CHUNKS_PER_STEP, S // CHUNK)
    tq = cps * CHUNK
    scale = HEAD_DIM ** -0.5

    def body(q_ref, k_ref, v_ref, b_ref, o_ref):
        sel0 = lax.broadcasted_iota(jnp.int32, (CHUNK, LANES), 1) < HEAD_DIM
        chunks = _attn_load(pl.program_id(1), cps, q_ref, k_ref, v_ref)
        _, probs = _attn_probs(chunks, b_ref, (sel0, jnp.logical_not(sel0)), scale)
        outs = [[jnp.dot(probs[cc][sub].astype(BF16), v2, preferred_element_type=F32) for sub in range(2)]
                for cc, (_, _, v2, _) in enumerate(chunks)]
        o_ref[...] = jnp.concatenate([jnp.where(sel0, o[0], o[1]) for o in outs], axis=0).astype(BF16)

    kv_spec = pl.BlockSpec((S + LEFT_PAD, LANES), lambda h, j: (0, h))
    return pl.pallas_call(
        body, name=name, grid=(HP, S // tq),
        in_specs=[pl.BlockSpec((tq, LANES), lambda h, j: (j, h)), kv_spec, kv_spec,
                  pl.BlockSpec((2, CHUNK, BAND), lambda h, j: (h, 0, 0))],
        out_specs=pl.BlockSpec((tq, LANES), lambda h, j: (j, h)),
        out_shape=jax.ShapeDtypeStruct((S, D), BF16),
        compiler_params=_params(("parallel", "parallel")),
    )(q, kpad, vpad, bias)


def attn_bwd(q, do, kpad, vpad, bias, dk_in, dv_in, name):
    S, D = q.shape
    HP = D // LANES
    NH = 2 * HP
    cps = min(ATTN_CHUNKS_PER_STEP, S // CHUNK)
    tq = cps * CHUNK
    nj = S // tq
    scale = HEAD_DIM ** -0.5

    def body(q_ref, do_ref, k_ref, v_ref, b_ref, dki_ref, dvi_ref, dq_ref, dk_ref, dv_ref, db_ref, dk_acc, dv_acc):
        j = pl.program_id(1)

        @pl.when(j == 0)
        def _():
            dk_acc[:LEFT_PAD, :] = jnp.zeros((LEFT_PAD, LANES), F32)
            dv_acc[:LEFT_PAD, :] = jnp.zeros((LEFT_PAD, LANES), F32)
            dk_acc[LEFT_PAD:, :] = dki_ref[...]
            dv_acc[LEFT_PAD:, :] = dvi_ref[...]
            db_ref[...] = jnp.zeros_like(db_ref)

        sel0 = lax.broadcasted_iota(jnp.int32, (CHUNK, LANES), 1) < HEAD_DIM
        sels = (sel0, jnp.logical_not(sel0))
        chunks = _attn_load(j, cps, q_ref, k_ref, v_ref)
        pairs = [(cc, sub) for cc in range(cps) for sub in range(2)]
        qms, probs = _attn_probs(chunks, b_ref, sels, scale)
        doms = [[jnp.where(sel, do_ref[cc * CHUNK:(cc + 1) * CHUNK, :], jnp.zeros((CHUNK, LANES), BF16))
                 for sel in sels] for cc in range(cps)]
        dps = {(cc, sub): lax.dot_general(doms[cc][sub], chunks[cc][2], NT, preferred_element_type=F32)
               for cc, sub in pairs}
        dss = {}
        for cc, sub in pairs:
            p = probs[cc][sub]
            dss[cc, sub] = p * (dps[cc, sub] - jnp.sum(dps[cc, sub] * p, axis=-1, keepdims=True))
        dsb = {key: ds.astype(BF16) for key, ds in dss.items()}
        dqs = {(cc, sub): jnp.dot(dsb[cc, sub], chunks[cc][1], preferred_element_type=F32) * scale
               for cc, sub in pairs}
        dks = {(cc, sub): lax.dot_general(dsb[cc, sub], qms[cc][sub], TN, preferred_element_type=F32) * scale
               for cc, sub in pairs}
        dvs = {(cc, sub): lax.dot_general(probs[cc][sub].astype(BF16), doms[cc][sub], TN,
                                          preferred_element_type=F32) for cc, sub in pairs}
        dq_ref[...] = jnp.concatenate([jnp.where(sel0, dqs[cc, 0], dqs[cc, 1]) for cc in range(cps)],
                                      axis=0).astype(BF16)
        for sub in range(2):
            total = dss[0, sub]
            for cc in range(1, cps):
                total = total + dss[cc, sub]
            db_ref[sub] += total
        dk_parts = [dks[cc, 0] + dks[cc, 1] for cc in range(cps)]
        dv_parts = [dvs[cc, 0] + dvs[cc, 1] for cc in range(cps)]

        def window(parts):
            blocks = []
            for rb in range(cps - 1 + BAND // CHUNK):
                acc = None
                for cc in range(cps):
                    b = rb - cc
                    if 0 <= b < BAND // CHUNK:
                        piece = parts[cc][b * CHUNK:(b + 1) * CHUNK, :]
                        acc = piece if acc is None else acc + piece
                blocks.append(acc)
            return jnp.concatenate(blocks, axis=0)

        span = pl.ds(pl.multiple_of(j * cps * CHUNK, CHUNK), (cps - 1) * CHUNK + BAND)
        dk_acc[span, :] += window(dk_parts)
        dv_acc[span, :] += window(dv_parts)

        @pl.when(j == nj - 1)
        def _():
            dk_ref[...] = dk_acc[LEFT_PAD:, :]
            dv_ref[...] = dv_acc[LEFT_PAD:, :]

    q_spec = pl.BlockSpec((tq, LANES), lambda h, j: (j, h))
    kv_spec = pl.BlockSpec((S + LEFT_PAD, LANES), lambda h, j: (0, h))
    col_spec = pl.BlockSpec((S, LANES), lambda h, j: (0, h))
    b_spec = pl.BlockSpec((2, CHUNK, BAND), lambda h, j: (h, 0, 0))
    return pl.pallas_call(
        body, name=name, grid=(HP, nj),
        in_specs=[q_spec, q_spec, kv_spec, kv_spec, b_spec, col_spec, col_spec],
        out_specs=[q_spec, col_spec, col_spec, b_spec],
        out_shape=[jax.ShapeDtypeStruct((S, D), BF16), jax.ShapeDtypeStruct((S, D), F32),
                   jax.ShapeDtypeStruct((S, D), F32), jax.ShapeDtypeStruct((NH, CHUNK, BAND), F32)],
        scratch_shapes=[pltpu.VMEM((S + LEFT_PAD, LANES), F32), pltpu.VMEM((S + LEFT_PAD, LANES), F32)],
        compiler_params=_params(("parallel", "arbitrary")),
    )(q, do, kpad, vpad, bias, dk_in, dv_in)


def _rel_onehot(t):
    r = lax.broadcasted_iota(jnp.int32, (BAND, N_REL), 0)
    i = lax.broadcasted_iota(jnp.int32, (BAND, N_REL), 1)
    idx = jnp.clip(t + LEFT_PAD - r, -(CHUNK - 1), MAX_REL) + (CHUNK - 1)
    return (idx == i).astype(BF16)


def _split3(v):
    hi = v.astype(BF16)
    rest = v - hi.astype(F32)
    mid = rest.astype(BF16)
    return hi, mid, (rest - mid.astype(F32)).astype(BF16)


def _dot_onehot(parts, onehot, dims):
    hi, mid, lo = [lax.dot_general(p, onehot, dims, preferred_element_type=F32) for p in parts]
    return (hi + mid) + lo


def bias_expand(rb, name):
    NH = rb.shape[0]

    def body(rb_ref, o_ref):
        parts = _split3(rb_ref[...])

        def step(t, carry):
            o_ref[t] = _dot_onehot(parts, _rel_onehot(t), NT)
            return carry

        lax.fori_loop(0, CHUNK, step, 0)

    return pl.pallas_call(
        body, name=name, in_specs=[VMEM_SPEC], out_specs=VMEM_SPEC,
        out_shape=jax.ShapeDtypeStruct((CHUNK, NH, BAND), F32), compiler_params=_params(),
    )(rb)


def bias_grad(dsum, name):
    NH = dsum.shape[1]

    def body(d_ref, o_ref):
        def step(t, acc):
            return acc + _dot_onehot(_split3(d_ref[t]), _rel_onehot(t), (((1,), (0,)), ((), ())))

        o_ref[...] = lax.fori_loop(0, CHUNK, step, jnp.zeros((NH, N_REL), F32))

    return pl.pallas_call(
        body, name=name, in_specs=[VMEM_SPEC], out_specs=VMEM_SPEC,
        out_shape=jax.ShapeDtypeStruct((NH, N_REL), F32), compiler_params=_params(),
    )(dsum)


def loss_grad(y, tgt, name):
    S, D = y.shape
    tm = min(MATMUL_ROW_TILE, S)

    def body(y_ref, t_ref, d_ref, acc_ref):
        @pl.when(pl.program_id(0) == 0)
        def _():
            acc_ref[...] = jnp.zeros_like(acc_ref)

        err = y_ref[...] - t_ref[...]
        d_ref[...] = err * (1.0 / D)
        acc_ref[0:1, :] += jnp.sum(err * err, axis=0, keepdims=True)

    tile = pl.BlockSpec((tm, D), _row(0))
    return pl.pallas_call(
        body, name=name, grid=(S // tm,), in_specs=[tile, tile],
        out_specs=[tile, pl.BlockSpec((8, D), _fixed2)],
        out_shape=[jax.ShapeDtypeStruct((S, D), F32), jax.ShapeDtypeStruct((8, D), F32)],
        compiler_params=_params(("arbitrary",)),
    )(y, tgt)


def _col_tile(n):
    for t in (768, 512, 256, 128):
        if n % t == 0:
            return t
    return n


def ada_fwd(c_all, w, b, name):
    L, D, n = w.shape
    tn = _col_tile(n)

    def body(c_ref, w_ref, b_ref, o_ref):
        cv = c_ref[...]
        ca = cv * _sigmoid(cv)
        o_ref[...] = jnp.dot(ca, w_ref[...], preferred_element_type=F32, precision=HIGHEST) + b_ref[...]

    return pl.pallas_call(
        body, name=name, grid=(L, n // tn),
        in_specs=[pl.BlockSpec((N_DEV, D), lambda l, j: (0, 0)), pl.BlockSpec((None, D, tn), lambda l, j: (l, 0, j)),
                  pl.BlockSpec((None, 1, tn), lambda l, j: (l, 0, j))],
        out_specs=pl.BlockSpec((None, N_DEV, tn), lambda l, j: (l, 0, j)),
        out_shape=jax.ShapeDtypeStruct((L, N_DEV, n), F32),
        compiler_params=_params(("parallel", "parallel")),
    )(c_all, w, b)


def ada_wgrad(c_all_t, dmod, name):
    L, _, n = dmod.shape
    D = c_all_t.shape[0]
    tn = _col_tile(n)

    def body(c_ref, d_ref, o_ref):
        cv = c_ref[...]
        ca = cv * _sigmoid(cv)
        o_ref[...] = jnp.dot(ca, d_ref[...], preferred_element_type=F32, precision=HIGHEST)

    return pl.pallas_call(
        body, name=name, grid=(L, n // tn),
        in_specs=[pl.BlockSpec((D, N_DEV), lambda l, j: (0, 0)), pl.BlockSpec((None, N_DEV, tn), lambda l, j: (l, 0, j))],
        out_specs=pl.BlockSpec((None, D, tn), lambda l, j: (l, 0, j)),
        out_shape=jax.ShapeDtypeStruct((L, D, n), F32),
        compiler_params=_params(("parallel", "parallel")),
    )(c_all_t, dmod)


ELEMENTWISE_BLOCK_BYTES = 3 * 1024 * 1024


def _elementwise_rows(rows, row_bytes):
    for t in (4096, 2048, 1024, 512, 256, 128, 64, 32, 16):
        if rows % t == 0 and t * row_bytes <= ELEMENTWISE_BLOCK_BYTES:
            return t
    return rows


def sum_leading(a, name):
    n, M, N = a.shape
    tr = _elementwise_rows(M, n * N * 4)

    def body(a_ref, o_ref):
        acc = a_ref[0]
        for i in range(1, n):
            acc = acc + a_ref[i]
        o_ref[...] = acc

    return pl.pallas_call(
        body, name=name, grid=(M // tr,),
        in_specs=[pl.BlockSpec((n, tr, N), lambda i: (0, i, 0))],
        out_specs=pl.BlockSpec((tr, N), _row(0)),
        out_shape=jax.ShapeDtypeStruct((M, N), F32),
        compiler_params=_params(("parallel",)),
    )(a)


def chip_sum(psum, land, chip_idx, transposed, name):
    n, M, N = psum.shape
    tr = M if transposed else _elementwise_rows(M, N * 8)

    def body(s_ref, p_ref, a_ref, b_ref, c_ref, o_ref):
        total = ((p_ref[...].astype(F32) + a_ref[...].astype(F32)) + b_ref[...].astype(F32)) + c_ref[...].astype(F32)
        o_ref[...] = jnp.transpose(total) if transposed else total

    def entry(j):
        return pl.BlockSpec((None, tr, N), lambda i, s: ((s[0] + j) % n, i, 0))

    out_block, out_dims = ((N, tr), (N, M)) if transposed else ((tr, N), (M, N))
    return pl.pallas_call(
        body, name=name,
        grid_spec=pltpu.PrefetchScalarGridSpec(
            num_scalar_prefetch=1, grid=(M // tr,),
            in_specs=[entry(0), entry(1), entry(2), entry(3)],
            out_specs=pl.BlockSpec(out_block, lambda i, s: (0, 0) if transposed else (i, 0))),
        out_shape=jax.ShapeDtypeStruct(out_dims, F32),
        compiler_params=_params(("parallel",)),
    )(chip_idx, psum, land, land, land)


def adamw(w, g, m, v, name):
    M, N = w.shape
    tr = _elementwise_rows(M, N * 4)
    c1 = 1.0 - ADAM_B1 ** ADAM_STEP
    c2 = 1.0 - ADAM_B2 ** ADAM_STEP

    def body(w_ref, g_ref, m_ref, v_ref, d_ref, nm_ref, nv_ref):
        g = g_ref[...]
        nm = ADAM_B1 * m_ref[...] + (1.0 - ADAM_B1) * g
        nv = ADAM_B2 * v_ref[...] + (1.0 - ADAM_B2) * (g * g)
        d_ref[...] = -ADAM_LR * ((nm / c1) / (jnp.sqrt(nv / c2) + ADAM_EPS) + ADAM_WD * w_ref[...])
        nm_ref[...] = nm
        nv_ref[...] = nv

    spec = pl.BlockSpec((tr, N), _row(0))
    shp = jax.ShapeDtypeStruct((M, N), F32)
    return pl.pallas_call(
        body, name=name, grid=(M // tr,), in_specs=[spec] * 4, out_specs=[spec] * 3, out_shape=[shp] * 3,
        compiler_params=_params(("parallel",)),
    )(w, g, m, v)


def _coords():
    return lax.axis_index("x"), lax.axis_index("y"), lax.axis_index("c")


def all_gather8(block, name):
    m_per, n = block.shape

    def body(x_ref, out_ref, send_sems, recv_sems, local_sem):
        x, y, c = _coords()
        me, sibling = (x, y, c), (x, y, 1 - c)
        chips = [(1 - x, y), (x, 1 - y), (1 - x, 1 - y)]

        def rows(px, py, pc):
            return out_ref.at[pl.ds((4 * px + 2 * py + pc) * m_per, m_per), :]

        def copy(k, blk, to, src=None):
            return pltpu.make_async_remote_copy(
                src_ref=rows(*blk) if src is None else src, dst_ref=rows(*blk),
                send_sem=send_sems.at[k], recv_sem=recv_sems.at[k], device_id=to, device_id_type=MESH)

        mine = pltpu.make_async_copy(x_ref, rows(*me), local_sem)
        mine.start()
        first = [copy(0, me, sibling, src=x_ref)]
        first += [copy(1 + j, me, (*chip, c), src=x_ref) for j, chip in enumerate(chips)]
        for cp in first:
            cp.start()
        passed = [copy(4 + j, (*chip, c), sibling) for j, chip in enumerate(chips)]
        for j, chip in enumerate(chips):
            copy(1 + j, (*chip, c), me).wait_recv()
            passed[j].start()
        copy(0, sibling, me).wait_recv()
        for j, chip in enumerate(chips):
            copy(4 + j, (*chip, 1 - c), me).wait_recv()
        for cp in first + passed:
            cp.wait_send()
        mine.wait()

    return pl.pallas_call(
        body, name=name, in_specs=[VMEM_SPEC], out_specs=VMEM_SPEC,
        out_shape=jax.ShapeDtypeStruct((N_DEV * m_per, n), block.dtype),
        scratch_shapes=[pltpu.SemaphoreType.DMA((7,)), pltpu.SemaphoreType.DMA((7,)), pltpu.SemaphoreType.DMA],
        compiler_params=_params(),
    )(block)


def _other_chips(x, y):
    return [(1 - x, y), (x, 1 - y), (1 - x, 1 - y)]


HBM_SPEC = pl.BlockSpec(memory_space=pltpu.HBM)
SEM_SPEC = pl.BlockSpec(memory_space=pltpu.SEMAPHORE)
DATAFLOW = pltpu.SideEffectType.DATAFLOW_SIDE_EFFECTING


def _chip_peers(x, y, c):
    return [(px, py, c) for px, py in _other_chips(x, y)]


def _sibling_peer(x, y, c):
    return [(x, y, 1 - c)]


def _weight_desc(_, k, land_ref, peer, me):
    h = land_ref.shape[1] // 2
    rows = pl.ds(me[2] * h, h)
    mine = land_ref.at[2 * me[0] + me[1], rows, :]
    return mine, mine, land_ref.at[2 * peer[0] + peer[1], rows, :]


def _grad_desc(psum_ref, k, land_ref, peer, me):
    return psum_ref.at[2 * peer[0] + peer[1]], land_ref.at[2 * me[0] + me[1]], land_ref.at[2 * peer[0] + peer[1]]


def _pair_desc(grad_ref, k, land_ref, peer, me):
    h = land_ref.shape[1]
    return grad_ref.at[:, pl.ds(peer[2] * h, h), :], land_ref, land_ref


def _whole_desc(src_ref, k, land_ref, peer, me):
    return src_ref, land_ref, land_ref


def exchange_start(srcs, lands, units, groups, desc, peers, after, name):
    n_s, n_l, n_g = len(srcs), len(lands), len(groups)
    n_p = len(peers(0, 0, 0))

    def body(*refs):
        s_refs, l_refs = refs[:n_s], refs[n_s:n_s + n_l]
        outs = refs[n_s + n_l + 1:]
        sems, token = outs[:2 * n_g], outs[-1]
        me = _coords()
        for g, ids in enumerate(groups):
            for i, u in enumerate(ids):
                si, k = units[u]
                for j, peer in enumerate(peers(*me)):
                    src, dst, _ = desc(s_refs[si] if s_refs else None, k, l_refs[u], peer, me)
                    pltpu.make_async_remote_copy(
                        src_ref=src, dst_ref=dst, send_sem=sems[2 * g].at[n_p * i + j],
                        recv_sem=sems[2 * g + 1].at[n_p * i + j], device_id=peer, device_id_type=MESH).start()
        token[...] = jnp.zeros_like(token)

    arrs = list(srcs) + list(lands)
    sem_shapes = [pltpu.SemaphoreType.DMA((n_p * len(ids),)) for ids in groups for _ in range(2)]
    outs = pl.pallas_call(
        body, name=name,
        in_specs=[HBM_SPEC] * len(arrs) + [ANY],
        out_specs=[SEM_SPEC] * (2 * n_g) + [HBM_SPEC] * len(arrs) + [VMEM_SPEC],
        out_shape=sem_shapes + [pltpu.HBM(a.shape, a.dtype) for a in arrs] + [jax.ShapeDtypeStruct((8, LANES), F32)],
        input_output_aliases={i: 2 * n_g + i for i in range(len(arrs))},
        compiler_params=pltpu.CompilerParams(has_side_effects=DATAFLOW),
    )(*[pltpu.with_memory_space_constraint(a, pltpu.HBM) for a in arrs], after)
    sems = outs[:2 * n_g]
    thru = outs[2 * n_g:2 * n_g + len(arrs)]
    return sems, list(thru[:n_s]), list(thru[n_s:]), outs[-1]


def exchange_wait(srcs, lands, units, send_sem, recv_sem, desc, peers, after, name):
    n_s, n_l = len(srcs), len(lands)
    n_p = len(peers(0, 0, 0))

    def body(*refs):
        s_refs, l_refs = refs[:n_s], refs[n_s:n_s + n_l]
        send_sems, recv_sems = refs[n_s + n_l], refs[n_s + n_l + 1]
        me = _coords()
        for i, (si, k) in enumerate(units):
            for j, peer in enumerate(peers(*me)):
                src, _, mine = desc(s_refs[si] if s_refs else None, k, l_refs[i], peer, me)
                cp = pltpu.make_async_remote_copy(
                    src_ref=src, dst_ref=mine, send_sem=send_sems.at[n_p * i + j], recv_sem=recv_sems.at[n_p * i + j],
                    device_id=peer, device_id_type=MESH)
                cp.wait_send()
                cp.wait_recv()

    arrs = list(srcs) + list(lands)
    outs = pl.pallas_call(
        body, name=name,
        in_specs=[HBM_SPEC] * len(arrs) + [SEM_SPEC, SEM_SPEC, ANY],
        out_specs=[HBM_SPEC] * len(arrs),
        out_shape=[pltpu.HBM(a.shape, a.dtype) for a in arrs],
        input_output_aliases={i: i for i in range(len(arrs))},
        compiler_params=pltpu.CompilerParams(has_side_effects=DATAFLOW),
    )(*arrs, send_sem, recv_sem, after)
    return list(outs[:n_s]), list(outs[n_s:])


def sibling_fill(lands, name):
    n_u = len(lands)

    def body(*refs):
        ins, outs = refs[:n_u], refs[n_u:2 * n_u]
        send_sems, recv_sems = refs[2 * n_u:]
        x, y, c = _coords()
        sends = []
        for u in range(n_u):
            h = ins[u].shape[1] // 2
            for j, (px, py) in enumerate(_other_chips(x, y)):
                part = (2 * px + py, pl.ds(c * h, h), slice(None))
                cp = pltpu.make_async_remote_copy(
                    src_ref=ins[u].at[part], dst_ref=outs[u].at[part], send_sem=send_sems.at[3 * u + j],
                    recv_sem=recv_sems.at[3 * u + j], device_id=(x, y, 1 - c), device_id_type=MESH)
                cp.start()
                sends.append(cp)
        for u in range(n_u):
            h = ins[u].shape[1] // 2
            for j, (px, py) in enumerate(_other_chips(x, y)):
                theirs = (2 * px + py, pl.ds((1 - c) * h, h), slice(None))
                pltpu.make_async_remote_copy(
                    src_ref=ins[u].at[theirs], dst_ref=outs[u].at[theirs], send_sem=send_sems.at[3 * u + j],
                    recv_sem=recv_sems.at[3 * u + j], device_id=(x, y, 1 - c), device_id_type=MESH).wait_recv()
        for cp in sends:
            cp.wait_send()

    return pl.pallas_call(
        body, name=name, in_specs=[ANY] * n_u, out_specs=[ANY] * n_u,
        out_shape=[jax.ShapeDtypeStruct(a.shape, a.dtype) for a in lands],
        input_output_aliases={i: i for i in range(n_u)},
        scratch_shapes=[pltpu.SemaphoreType.DMA((3 * n_u,)), pltpu.SemaphoreType.DMA((3 * n_u,))],
        compiler_params=_params(),
    )(*lands)


def _pack_rows(parts):
    flat = jnp.concatenate([p.reshape(-1).astype(F32) for p in parts])
    n = flat.shape[0]
    padded = -(-n // (8 * LANES)) * (8 * LANES)
    return jnp.pad(flat, (0, padded - n)).reshape(-1, LANES)


def _unpack_rows(packed, shapes):
    flat = packed.reshape(-1)
    out, off = [], 0
    for s in shapes:
        size = 1
        for d in s:
            size *= d
        out.append(flat[off:off + size].reshape(s))
        off += size
    return out


def _shard_last(full, s_me):
    n = full.shape[-1] // N_CHIP
    return lax.dynamic_slice_in_dim(full, s_me * n, n, axis=full.ndim - 1)


def _unshard_last(g):
    moved = jnp.moveaxis(g, 0, -2)
    return moved.reshape(moved.shape[:-2] + (moved.shape[-2] * moved.shape[-1],))


def kernel(x, c, w_ada, b_ada, ln_g, ln_b, ffn_gu, ffn_down, gmlp_w_in, gmlp_b_in, gmlp_ln_g, gmlp_ln_b, gmlp_w_s, gmlp_b_s, gmlp_w_out, w_ada_kv, b_ada_kv, w_kv, attn_w_q, attn_rel_bias, attn_w_o, loss_target, m_w_ada, m_b_ada, m_ln_g, m_ln_b, m_ffn_gu, m_ffn_down, m_gmlp_w_in, m_gmlp_b_in, m_gmlp_ln_g, m_gmlp_ln_b, m_gmlp_w_s, m_gmlp_b_s, m_gmlp_w_out, m_w_ada_kv, m_b_ada_kv, m_w_kv, m_attn_w_q, m_attn_rel_bias, m_attn_w_o, v_w_ada, v_b_ada, v_ln_g, v_ln_b, v_ffn_gu, v_ffn_down, v_gmlp_w_in, v_gmlp_b_in, v_gmlp_ln_g, v_gmlp_ln_b, v_gmlp_w_s, v_gmlp_b_s, v_gmlp_w_out, v_w_ada_kv, v_b_ada_kv, v_w_kv, v_attn_w_q, v_attn_rel_bias, v_attn_w_o):
    xi, yi, ci = _coords()
    s_me = 2 * xi + yi
    dev = 4 * xi + 2 * yi + ci

    x0 = x[0]
    tgt = loss_target[0]
    S, D = x0.shape
    L = w_ada.shape[0]
    NA = gmlp_w_in.shape[0]
    NB = attn_w_q.shape[0]
    NH = D // HEAD_DIM
    alpha = (2.0 * L) ** 0.25
    n_ada = w_ada.shape[2]
    n_kv = w_ada_kv.shape[1]

    stack_names = ["ffn_gu", "ffn_down", "gmlp_w_in", "gmlp_w_out", "w_kv", "attn_w_q", "attn_w_o"]
    stack_src = dict(ffn_gu=ffn_gu, ffn_down=ffn_down, gmlp_w_in=gmlp_w_in, gmlp_w_out=gmlp_w_out, w_kv=w_kv[None],
                     attn_w_q=attn_w_q, attn_w_o=attn_w_o)
    stacks = [stack_src[nm].reshape((-1,) + stack_src[nm].shape[-2:]) for nm in stack_names]
    units = [(si, k) for si, st in enumerate(stacks) for k in range(st.shape[0])]
    unit_of = {(stack_names[si], k): u for u, (si, k) in enumerate(units)}
    assert NA >= 1
    weight_groups = [[("ffn_gu", 0)], [("ffn_down", 0)],
                     [("gmlp_w_in", 0), ("gmlp_w_out", 0), ("ffn_gu", 1), ("ffn_down", 1)]]
    for l in range(1, L):
        names = [("w_kv", 0)] if l == NA else []
        names += [("ffn_gu", 2 * l), ("ffn_down", 2 * l)]
        names += [("gmlp_w_in", l), ("gmlp_w_out", l)] if l < NA else [("attn_w_q", l - NA), ("attn_w_o", l - NA)]
        names += [("ffn_gu", 2 * l + 1), ("ffn_down", 2 * l + 1)]
        weight_groups.append(names)
    weight_groups = [[unit_of[n] for n in names] for names in weight_groups]

    c_all = all_gather8(jnp.broadcast_to(c, (8, D)), "ag_c").reshape(N_DEV, 8, D)[:, 0]
    b_ada_sh = lax.dynamic_slice_in_dim(b_ada, s_me * n_ada, n_ada, axis=1)
    b_kv_sh = lax.dynamic_slice_in_dim(b_ada_kv, s_me * n_kv, n_kv, axis=0)
    mod_part = ada_fwd(c_all, w_ada, b_ada_sh[:, None, :], "ada_fwd")
    mkv_part = ada_fwd(c_all, w_ada_kv[None], b_kv_sh[None, None, :], "ada_kv_fwd")
    part = jnp.concatenate([jnp.transpose(mod_part, (1, 0, 2)).reshape(N_DEV, L * n_ada), mkv_part[0]], axis=1)
    width = part.shape[1]
    pad_w = -(-width // LANES) * LANES - width
    all_part = all_gather8(jnp.pad(part, ((0, 0), (0, pad_w))), "ag_mod").reshape(N_DEV, N_DEV, width + pad_w)
    mine = lax.dynamic_index_in_dim(all_part[0::2], dev, axis=1, keepdims=False)
    mod = jnp.transpose(mine[:, :L * n_ada].reshape(N_CHIP, L, n_ada), (1, 0, 2)).reshape(L, N_MOD, D)
    mkv = mine[:, L * n_ada:width].reshape(2, D)

    def mrow(l, k):
        return mod[l, k][None, :]

    small_shapes = [ln_g.shape, ln_b.shape, gmlp_b_in.shape, gmlp_ln_g.shape, gmlp_ln_b.shape, attn_rel_bias.shape]
    small_pack = _pack_rows([ln_g, ln_b, gmlp_b_in, gmlp_ln_g, gmlp_ln_b, attn_rel_bias])
    small_all = all_gather8(small_pack, "ag_small_params").reshape((N_DEV,) + small_pack.shape)[0::2]
    sm = [_unpack_rows(small_all[s], small_shapes) for s in range(N_CHIP)]
    ln_g_f, ln_b_f, b_in_f, gln_g_f, gln_b_f, rel_f = [
        _unshard_last(jnp.stack([sm[s][i] for s in range(N_CHIP)])) for i in range(len(small_shapes))]

    def landing(u):
        si, k = units[u]
        shard = stacks[si][k].astype(BF16)
        return lax.dynamic_update_slice(lax.empty((N_CHIP,) + shard.shape, BF16), shard[None], (s_me, 0, 0))

    gathers_done = jnp.concatenate([mod.reshape(-1)[:LANES], small_all.reshape(-1)[:LANES]])
    w_sems, lands_t = {}, {}
    for part, groups, name in ((0, weight_groups[:2], "weight_send_start_first"),
                               (1, weight_groups[2:], "weight_send_start_rest")):
        ids = [u for grp in groups for u in grp]
        local = [[ids.index(u) for u in grp] for grp in groups]
        sems, _, lands, gathers_done = exchange_start([], [landing(u) for u in ids], [units[u] for u in ids], local,
                                                      _weight_desc, _chip_peers, gathers_done, name)
        for i, grp in enumerate(groups):
            w_sems[2 * part + i] = (sems[2 * i], sems[2 * i + 1])
        lands_t.update(zip(ids, lands))
    wg = {}

    def fetch_weights(g, after):
        ids = weight_groups[g]
        _, got = exchange_wait([], [lands_t[u] for u in ids], [units[u] for u in ids], w_sems[g][0], w_sems[g][1],
                               _weight_desc, _chip_peers, after, "weight_send_wait_%d" % g)
        for u, a in zip(ids, sibling_fill(got, "weight_sibling_fill")):
            wg[u] = a

    def W(nm, k):
        return wg[unit_of[(nm, k)]]

    def Wrows(nm, k):
        w4 = W(nm, k)
        return w4.reshape(w4.shape[0] * w4.shape[1], w4.shape[2])

    bst = [jnp.transpose(gmlp_b_s[j]) for j in range(NA)]
    biases = {}

    def make_bias(j, dep):
        rel, _ = lax.optimization_barrier((rel_f[j], dep))
        biases[j] = jnp.transpose(bias_expand(rel, "bias_expand"), (1, 0, 2))
        return biases[j]

    saved = []
    xc = x0
    kpad = vpad = xkv = None
    for l in range(L):
        after = mod if l == 0 else xc
        if l == 1 and NB > 1:
            after = make_bias(1, xc)
        fetch_weights(0 if l == 0 else l + 2, after)
        if l == NA:
            xkv = xc
            kv, hkv = mod_matmul(xc, mkv[1][None], mkv[0][None], W("w_kv", 0), None, BF16, "kv_proj")
            kpad = jnp.pad(kv[:, :D], ((LEFT_PAD, 0), (0, 0)))
            vpad = jnp.pad(kv[:, D:], ((LEFT_PAD, 0), (0, 0)))
        sv = {}
        for i in (0, 2):
            k = 2 * l + i // 2
            gu, hv = mod_matmul(xc, mrow(l, 3 * i + 1), mrow(l, 3 * i), W("ffn_gu", k), None, BF16, "ffn_up")
            if l == 0 and i == 0:
                fetch_weights(1, hv)
            gw = 0.5 * (1.0 + mrow(l, 3 * i + 2))
            xn, xh, rs, yv, av = matmul_res_ln(gu, Wrows("ffn_down", k), xc, gw, ln_g_f[l, i][None],
                                               ln_b_f[l, i][None], alpha, True, "ffn_down")
            sv[i] = dict(x=xc, h=hv, gu=gu, a=av, xh=xh, rs=rs, y=yv, gw=gw)
            xc = xn
            if i == 0:
                if l == 0:
                    fetch_weights(2, make_bias(0, xc) if NB > 0 else xc)
                gw = 1.0 + mrow(l, 5)
                if l < NA:
                    pre, hv = mod_matmul(xc, mrow(l, 4), mrow(l, 3), W("gmlp_w_in", l), b_in_f[l][None], F32,
                                         "gmlp_in")
                    qv = sgu_fwd(pre, gln_g_f[l][None], gln_b_f[l][None], gmlp_w_s[l], bst[l], "sgu_fwd")
                    xn, xh, rs, yv = matmul_res_ln(qv, Wrows("gmlp_w_out", l), xc, gw, ln_g_f[l, 1][None],
                                                   ln_b_f[l, 1][None], alpha, False, "gmlp_out")
                    sv[1] = dict(x=xc, h=hv, pre=pre, a=qv, xh=xh, rs=rs, y=yv, gw=gw)
                else:
                    j = l - NA
                    if j not in biases:
                        make_bias(j, xc)
                    qh, hv = mod_matmul(xc, mrow(l, 4), mrow(l, 3), Wrows("attn_w_q", j)[None], None, BF16, "attn_q")
                    ov = attn_fwd(qh, kpad, vpad, biases[j], "attn_fwd")
                    xn, xh, rs, yv = matmul_res_ln(ov, Wrows("attn_w_o", j), xc, gw, ln_g_f[l, 1][None],
                                                   ln_b_f[l, 1][None], alpha, False, "attn_out")
                    sv[1] = dict(x=xc, h=hv, q=qh, a=ov, xh=xh, rs=rs, y=yv, gw=gw)
                xc = xn
        saved.append(sv)

    dx, lacc = loss_grad(xc, tgt, "loss_grad")
    loss = lax.psum((0.5 / D) * jnp.sum(lacc[0]), ("x", "y", "c"))

    gpair = [None] * len(units)
    col_split = {u for u, (si, _) in enumerate(units) if stack_names[si] != "ffn_gu"}
    dmod = [[None] * N_MOD for _ in range(L)]
    d_ln_g = [[None] * 3 for _ in range(L)]
    d_ln_b = [[None] * 3 for _ in range(L)]
    d_b_in, d_gln_g, d_gln_b, d_ws, d_bs, d_rel = ([None] * NA, [None] * NA, [None] * NA, [None] * NA, [None] * NA,
                                                  [None] * NB)
    dk = jnp.zeros((S, D), F32)
    dv = jnp.zeros((S, D), F32)
    dmkv = None

    made = []

    core_idx = ci.astype(jnp.int32).reshape(1)
    chip_idx = s_me.astype(jnp.int32).reshape(1)

    def put(nm, k, a, b, name):
        u = unit_of[(nm, k)]
        rows, cols = stacks[units[u][0]].shape[1:]
        if nm == "ffn_gu":
            g = wgrad_pair(b, a, N_CHIP, cols, rows // 2, lambda j, p: j, lambda j, p: p, core_idx, name)
        elif nm in ("gmlp_w_in", "w_kv"):
            g = wgrad_pair(a, b, N_CHIP, rows, cols // 2, lambda j, p: 0, lambda j, p: 2 * j + p, core_idx, name)
        else:
            g = wgrad_pair(a, b, 1, N_CHIP * rows, cols // 2, lambda j, p: 0, lambda j, p: p, core_idx, name)
        gpair[u] = g.reshape((N_CHIP, -1, g.shape[-1]))
        made.append(u)

    own_half, sib_half = {}, {}
    n_started = [0]

    def start_grad_exchange(ids, after):
        psums = [gpair[u] for u in ids]
        n = len(ids)
        tag = n_started[0]
        n_started[0] += 1
        sems, ps_t, q_t, token = exchange_start(psums, [lax.empty(p.shape, p.dtype) for p in psums],
                                                [(i, 0) for i in range(n)], [list(range(n))], _grad_desc, _chip_peers,
                                                after, "grad_send_start_%d" % tag)
        return dict(ids=ids, tag=tag, sems=sems, ps=ps_t, q=q_t), token

    def finish_grad_exchange(pend, after):
        n = len(pend["ids"])
        ps_t, q = exchange_wait(pend["ps"], pend["q"], [(i, 0) for i in range(n)], pend["sems"][0], pend["sems"][1],
                                _grad_desc, _chip_peers, after, "grad_send_wait_%d" % pend["tag"])
        halves = [chip_sum(ps_t[i], q[i], chip_idx, u not in col_split, "grad_chip_sum")
                  for i, u in enumerate(pend["ids"])]
        sems, h_t, land_t, token = exchange_start(halves, [lax.empty(h.shape, h.dtype) for h in halves],
                                              [(i, 0) for i in range(n)], [list(range(n))], _whole_desc,
                                              _sibling_peer, halves[0], "half_send_start_%d" % pend["tag"])
        swaps.append(dict(ids=pend["ids"], tag=pend["tag"], sems=sems, h=h_t, land=land_t))
        return token

    def collect_halves(after):
        for sw in swaps:
            n = len(sw["ids"])
            h, land = exchange_wait(sw["h"], sw["land"], [(i, 0) for i in range(n)], sw["sems"][0], sw["sems"][1],
                                    _whole_desc, _sibling_peer, after, "half_send_wait_%d" % sw["tag"])
            for u, mine, theirs in zip(sw["ids"], h, land):
                own_half[u], sib_half[u] = mine, theirs
        swaps.clear()

    swaps = []
    pending = None
    started_before = jnp.zeros((8, LANES), F32)

    def ln_inputs(l, i):
        t = saved[l][i]
        return (t["xh"], t["rs"], ln_g_f[l, i][None], t["y"], t["gw"], 1.0 if i == 1 else 0.5)

    def record_ln(l, i, acc, row0):
        d_ln_g[l][i], d_ln_b[l][i], dmod[l][3 * i + 2] = acc[row0], acc[row0 + 1], acc[row0 + 2]

    ln_done = None
    for l in reversed(range(L)):
        if l == NA - 1:
            dkv = jnp.concatenate([dk, dv], axis=1)
            put("w_kv", 0, hkv, dkv, "kv_wgrad")
            pdxa, pdy, acc = dgrad_mod(dkv, W("w_kv", 0), dx, xkv, mkv[1][None], ln_inputs(l, 2), alpha, "kv_dgrad")
            dmkv = jnp.stack([acc[1], acc[0]])
            record_ln(l, 2, acc, 2)
            ln_done = (pdxa, pdy)
        sv = saved[l]
        for i in (2, 1, 0):
            t = sv[i]
            if ln_done is None:
                dxa, dy, acc1 = ln_res_bwd(dx, *ln_inputs(l, i), alpha, "ln_res_bwd")
                record_ln(l, i, acc1, 0)
            else:
                dxa, dy = ln_done
                ln_done = None
            before = (l, i - 1) if i > 0 else ((l - 1, 2) if l > 0 and l != NA else None)
            prev = ln_inputs(*before) if before is not None else None
            scl = mrow(l, 3 * i + 1)
            if i != 1:
                k = 2 * l + i // 2
                F = t["gu"].shape[1] // 2
                dgu = ffn_act_bwd(dy, Wrows("ffn_down", k), t["gu"], started_before, "ffn_act_bwd")
                put("ffn_down", k, t["a"], dy, "ffn_down_wgrad")
                put("ffn_gu", k, t["h"], dgu, "ffn_up_wgrad")
                res = dgrad_mod(dgu, W("ffn_gu", k), dxa, t["x"], scl, prev, alpha, "ffn_up_dgrad")
            elif l < NA:
                dq = matmul_nt(dy, Wrows("gmlp_w_out", l), started_before, "gmlp_out_dgrad")
                put("gmlp_w_out", l, t["a"], dy, "gmlp_out_wgrad")
                dpre, dws_l, dss, dgl, dbin = sgu_bwd(dq, t["pre"], gln_g_f[l][None], gln_b_f[l][None], gmlp_w_s[l],
                                                      bst[l], "sgu_bwd")
                d_ws[l] = dws_l
                d_bs[l] = jnp.transpose(group_lane_sum(dss, "sgu_bias_grad")[:, :GMLP_GROUPS])
                d_gln_g[l], d_gln_b[l], d_b_in[l] = dgl[0], dgl[1], dbin[0]
                put("gmlp_w_in", l, t["h"], dpre, "gmlp_in_wgrad")
                res = dgrad_mod(dpre, W("gmlp_w_in", l), dxa, t["x"], scl, prev, alpha, "gmlp_in_dgrad")
            else:
                j = l - NA
                do = matmul_nt(dy, Wrows("attn_w_o", j), started_before, "attn_out_dgrad")
                put("attn_w_o", j, t["a"], dy, "attn_out_wgrad")
                dqh, dk, dv, dbias = attn_bwd(t["q"], do, kpad, vpad, biases[j], dk, dv, "attn_bwd")
                d_rel[j] = bias_grad(jnp.transpose(dbias, (1, 0, 2)), "bias_grad")
                put("attn_w_q", j, t["h"], dqh, "attn_q_wgrad")
                res = dgrad_mod(dqh, Wrows("attn_w_q", j)[None], dxa, t["x"], scl, prev, alpha, "attn_q_dgrad")
            acc2 = res[-1]
            dmod[l][3 * i + 1], dmod[l][3 * i] = acc2[0], acc2[1]
            if before is None:
                dx = res[0]
            else:
                record_ln(*before, acc2, 2)
                ln_done = (res[0], res[1])
            if (i == 0 and l > 0) or (i == 1 and l == 0):
                started, started_before = start_grad_exchange(list(made), acc2)
                made.clear()
                if pending is not None:
                    started_before = started_before + finish_grad_exchange(pending, acc2)
                pending = started
    grad_x = dx[None]

    dvec = _pack_rows([jnp.stack([jnp.stack(r) for r in dmod]), dmkv])
    dvec = lax.optimization_barrier((dvec, [gpair[u] for u in made]))[0]
    n_dvec = L * N_MOD * D + 2 * D
    dall = all_gather8(dvec, "ag_dmod").reshape(N_DEV, -1, LANES)
    db_all = sum_leading(dall, "ada_bias_grad").reshape(-1)[:n_dvec]
    g_b_ada = db_all[:L * N_MOD * D].reshape(L, N_MOD * D)
    g_b_ada_kv = db_all[L * N_MOD * D:]
    dall2 = dall.reshape(N_DEV, -1)[:, :n_dvec]
    dmod_all = dall2[:, :L * N_MOD * D].reshape(N_DEV, L, N_MOD * D)
    dmod_sh = jnp.transpose(lax.dynamic_slice_in_dim(dmod_all, s_me * n_ada, n_ada, axis=2), (1, 0, 2))
    dmkv_sh = lax.dynamic_slice_in_dim(dall2[:, L * N_MOD * D:], s_me * n_kv, n_kv, axis=1)[None]
    c_all_t = jnp.transpose(c_all)
    g_w_ada = ada_wgrad(c_all_t, dmod_sh, "ada_wgrad")
    g_w_ada_kv = ada_wgrad(c_all_t, dmkv_sh, "ada_kv_wgrad")[0]

    small_g = [jnp.stack([jnp.stack(r) for r in d_ln_g]), jnp.stack([jnp.stack(r) for r in d_ln_b]),
               jnp.stack(d_b_in), jnp.stack(d_gln_g), jnp.stack(d_gln_b), jnp.stack(d_rel), jnp.stack(d_ws),
               jnp.stack(d_bs)]
    sg_shapes = [a.shape for a in small_g]
    sg_pack = _pack_rows(small_g)
    sg_all = all_gather8(sg_pack, "ag_small_grads").reshape(N_DEV, -1, LANES)
    sg_sum = _unpack_rows(sum_leading(sg_all, "small_grad_sum"), sg_shapes)
    g_ln_g, g_ln_b, g_b_in, g_gln_g, g_gln_b, g_rel = [_shard_last(a, s_me) for a in sg_sum[:6]]
    g_ws, g_bs = sg_sum[6], sg_sum[7]

    last, _ = start_grad_exchange(list(made), sg_all)

    grads = dict(w_ada=g_w_ada, b_ada=g_b_ada, ln_g=g_ln_g, ln_b=g_ln_b, gmlp_b_in=g_b_in, gmlp_ln_g=g_gln_g,
                 gmlp_ln_b=g_gln_b, gmlp_w_s=g_ws, gmlp_b_s=g_bs, w_ada_kv=g_w_ada_kv, b_ada_kv=g_b_ada_kv,
                 attn_rel_bias=g_rel)
    weights = dict(w_ada=w_ada, b_ada=b_ada, ln_g=ln_g, ln_b=ln_b, ffn_gu=ffn_gu, ffn_down=ffn_down,
                   gmlp_w_in=gmlp_w_in, gmlp_b_in=gmlp_b_in, gmlp_ln_g=gmlp_ln_g, gmlp_ln_b=gmlp_ln_b,
                   gmlp_w_s=gmlp_w_s, gmlp_b_s=gmlp_b_s, gmlp_w_out=gmlp_w_out, w_ada_kv=w_ada_kv,
                   b_ada_kv=b_ada_kv, w_kv=w_kv, attn_w_q=attn_w_q, attn_rel_bias=attn_rel_bias, attn_w_o=attn_w_o)
    ms = dict(w_ada=m_w_ada, b_ada=m_b_ada, ln_g=m_ln_g, ln_b=m_ln_b, ffn_gu=m_ffn_gu, ffn_down=m_ffn_down,
              gmlp_w_in=m_gmlp_w_in, gmlp_b_in=m_gmlp_b_in, gmlp_ln_g=m_gmlp_ln_g, gmlp_ln_b=m_gmlp_ln_b,
              gmlp_w_s=m_gmlp_w_s, gmlp_b_s=m_gmlp_b_s, gmlp_w_out=m_gmlp_w_out, w_ada_kv=m_w_ada_kv,
              b_ada_kv=m_b_ada_kv, w_kv=m_w_kv, attn_w_q=m_attn_w_q, attn_rel_bias=m_attn_rel_bias,
              attn_w_o=m_attn_w_o)
    vs = dict(w_ada=v_w_ada, b_ada=v_b_ada, ln_g=v_ln_g, ln_b=v_ln_b, ffn_gu=v_ffn_gu, ffn_down=v_ffn_down,
              gmlp_w_in=v_gmlp_w_in, gmlp_b_in=v_gmlp_b_in, gmlp_ln_g=v_gmlp_ln_g, gmlp_ln_b=v_gmlp_ln_b,
              gmlp_w_s=v_gmlp_w_s, gmlp_b_s=v_gmlp_b_s, gmlp_w_out=v_gmlp_w_out, w_ada_kv=v_w_ada_kv,
              b_ada_kv=v_b_ada_kv, w_kv=v_w_kv, attn_w_q=v_attn_w_q, attn_rel_bias=v_attn_rel_bias,
              attn_w_o=v_attn_w_o)
    order = ["w_ada", "b_ada", "ln_g", "ln_b", "ffn_gu", "ffn_down", "gmlp_w_in", "gmlp_b_in", "gmlp_ln_g",
             "gmlp_ln_b", "gmlp_w_s", "gmlp_b_s", "gmlp_w_out", "w_ada_kv", "b_ada_kv", "w_kv", "attn_w_q",
             "attn_rel_bias", "attn_w_o"]
    big_names = ["w_ada", "w_ada_kv"] + stack_names
    small_names = [nm for nm in order if nm not in big_names]
    delta, new_m, new_v = {}, {}, {}

    def adamw_big(nm):
        shp = weights[nm].shape
        two_d = (-1, shp[-1])
        d, a, b = adamw(weights[nm].reshape(two_d), grads[nm].reshape(two_d), ms[nm].reshape(two_d),
                        vs[nm].reshape(two_d), "adamw")
        delta[nm], new_m[nm], new_v[nm] = d.reshape(shp), a.reshape(shp), b.reshape(shp)

    adamw_big("w_ada")
    adamw_big("w_ada_kv")
    shapes = [weights[nm].shape for nm in small_names]
    d, a, b = adamw(_pack_rows([weights[nm] for nm in small_names]), _pack_rows([grads[nm] for nm in small_names]),
                    _pack_rows([ms[nm] for nm in small_names]), _pack_rows([vs[nm] for nm in small_names]),
                    "adamw_small")
    for nm, dd, aa, bb in zip(small_names, _unpack_rows(d, shapes), _unpack_rows(a, shapes), _unpack_rows(b, shapes)):
        delta[nm], new_m[nm], new_v[nm] = dd, aa, bb

    def full_grad(u):
        lo = jnp.where(ci == 0, own_half[u], sib_half[u])
        hi = jnp.where(ci == 0, sib_half[u], own_half[u])
        return jnp.concatenate([lo, hi], axis=1 if u in col_split else 0)

    def adamw_stack(nm):
        si = stack_names.index(nm)
        g = jnp.stack([full_grad(unit_of[(nm, k)]) for k in range(stacks[si].shape[0])])
        grads[nm] = g.reshape(weights[nm].shape)
        adamw_big(nm)

    late = [stack_names[units[u][0]] for u in last["ids"]]
    early = [nm for nm in stack_names if nm not in late]
    finish_grad_exchange(pending, delta["w_ada"])
    collect_halves(delta["w_ada"])
    for nm in early:
        adamw_stack(nm)
    finish_grad_exchange(last, delta[early[-1]])
    collect_halves(delta[early[-1]])
    for nm in stack_names:
        if nm in late:
            adamw_stack(nm)

    return (loss, grad_x, *[grads[nm] for nm in order], *[delta[nm] for nm in order],
            *[new_m[nm] for nm in order], *[new_v[nm] for nm in order])
```

```python
import functools

import jax
import jax.numpy as jnp
from jax import lax
from jax.experimental import pallas as pl
from jax.experimental.pallas import tpu as pltpu

F32 = jnp.float32
BF16 = jnp.bfloat16
MESH = pl.DeviceIdType.MESH
HIGHEST = lax.Precision.HIGHEST

CHUNK = 64
GMLP_WINDOW = 128
GMLP_GROUPS = 8
HEAD_DIM = 64
LEFT_CHUNKS = 8
BAND = (LEFT_CHUNKS + 1) * CHUNK
LEFT_PAD = LEFT_CHUNKS * CHUNK
MAX_REL = 4 * CHUNK
N_REL = (CHUNK - 1) + MAX_REL + 1
LN_EPS = 1e-5
N_MOD = 9
N_DEV = 8
N_CHIP = 4

ADAM_LR = 0.001
ADAM_B1 = 0.9
ADAM_B2 = 0.999
ADAM_EPS = 1e-08
ADAM_WD = 0.01
ADAM_STEP = 10

LANES = 128
ROW_TILE = 256
MATMUL_ROW_TILE = 512
WGRAD_ROWS = 1024
ATTN_CHUNKS_PER_STEP = 4
VMEM_LIMIT_MB = 56

NT = (((1,), (1,)), ((), ()))
TN = (((0,), (0,)), ((), ()))

ANY = pl.BlockSpec(memory_space=pl.ANY)
VMEM_SPEC = pl.BlockSpec(memory_space=pltpu.VMEM)


def _params(semantics=None):
    kw = dict(vmem_limit_bytes=VMEM_LIMIT_MB * 1024 * 1024)
    if semantics is not None:
        kw["dimension_semantics"] = semantics
    return pltpu.CompilerParams(**kw)


def _sigmoid(v):
    return 0.5 * (1.0 + jnp.tanh(0.5 * v))


def _gelu(v):
    return 0.5 * v * (1.0 + lax.erf(v * (2.0 ** -0.5)))


def _gelu_grad(v):
    return 0.5 * (1.0 + lax.erf(v * (2.0 ** -0.5))) + v * jnp.exp(-0.5 * v * v) * ((2.0 * jnp.pi) ** -0.5)


def _row(m):
    return lambda i: (i, 0)


def _fixed2(i):
    return (0, 0)


def _fixed3(i):
    return (0, 0, 0)


def _resident(shape):
    return pl.BlockSpec(shape, _fixed2 if len(shape) == 2 else _fixed3, pipeline_mode=pl.Buffered(1))


def mod_matmul(x, scl, shift, w, bias, out_dtype, name):
    S, D = x.shape
    NS, _, n = w.shape
    tm = min(MATMUL_ROW_TILE, S)
    has_bias = bias is not None

    def body(*refs):
        if has_bias:
            x_ref, scl_ref, sh_ref, w_ref, b_ref, o_ref, h_ref = refs
        else:
            x_ref, scl_ref, sh_ref, w_ref, o_ref, h_ref = refs
        h = (x_ref[...] * (1.0 + scl_ref[...]) + sh_ref[...]).astype(BF16)
        h_ref[...] = h
        for s in range(NS):
            acc = jnp.dot(h, w_ref[s], preferred_element_type=F32)
            if has_bias:
                acc = acc + b_ref[:, s * n:(s + 1) * n]
            o_ref[:, s * n:(s + 1) * n] = acc.astype(out_dtype)

    in_specs = [pl.BlockSpec((tm, D), _row(0)), pl.BlockSpec((1, D), _fixed2), pl.BlockSpec((1, D), _fixed2),
                _resident((NS, D, n))]
    args = [x, scl, shift, w]
    if has_bias:
        in_specs.append(pl.BlockSpec((1, NS * n), _fixed2))
        args.append(bias)
    return pl.pallas_call(
        body, name=name, grid=(S // tm,), in_specs=in_specs,
        out_specs=[pl.BlockSpec((tm, NS * n), _row(0)), pl.BlockSpec((tm, D), _row(0))],
        out_shape=[jax.ShapeDtypeStruct((S, NS * n), out_dtype), jax.ShapeDtypeStruct((S, D), BF16)],
        compiler_params=_params(("parallel",)),
    )(*args)


def matmul_res_ln(a, w, x, gw, lg, lb, alpha, swiglu, name):
    S, D = x.shape
    K = w.shape[0]
    tm = min(ROW_TILE, S)
    ka = a.shape[1]

    def body(a_ref, w_ref, x_ref, gw_ref, lg_ref, lb_ref, xn_ref, xh_ref, rs_ref, y_ref, *act_ref):
        if swiglu:
            g = a_ref[:, :K].astype(F32)
            u = a_ref[:, K:].astype(F32)
            act = (g * _sigmoid(g) * u).astype(BF16)
            act_ref[0][...] = act
        else:
            act = a_ref[...].astype(BF16)
        y = jnp.dot(act, w_ref[...], preferred_element_type=F32)
        z = alpha * x_ref[...] + gw_ref[...] * y
        mu = jnp.mean(z, axis=-1, keepdims=True)
        zc = z - mu
        var = jnp.mean(zc * zc, axis=-1, keepdims=True)
        rstd = lax.rsqrt(var + LN_EPS)
        xhat = zc * rstd
        xn_ref[...] = xhat * lg_ref[...] + lb_ref[...]
        xh_ref[...] = xhat
        rs_ref[...] = rstd
        y_ref[...] = y.astype(BF16)

    vec = pl.BlockSpec((1, D), _fixed2)
    out_specs = [pl.BlockSpec((tm, D), _row(0)), pl.BlockSpec((tm, D), _row(0)), pl.BlockSpec((tm, 1), _row(0)),
                 pl.BlockSpec((tm, D), _row(0))]
    out_shape = [jax.ShapeDtypeStruct((S, D), F32), jax.ShapeDtypeStruct((S, D), F32),
                 jax.ShapeDtypeStruct((S, 1), F32), jax.ShapeDtypeStruct((S, D), BF16)]
    if swiglu:
        out_specs.append(pl.BlockSpec((tm, K), _row(0)))
        out_shape.append(jax.ShapeDtypeStruct((S, K), BF16))
    return pl.pallas_call(
        body, name=name, grid=(S // tm,),
        in_specs=[pl.BlockSpec((tm, ka), _row(0)), _resident((K, D)), pl.BlockSpec((tm, D), _row(0)),
                  vec, vec, vec],
        out_specs=out_specs, out_shape=out_shape,
        compiler_params=_params(("parallel",)),
    )(a, w, x, gw, lg, lb)


def _ln_res_bwd_tile(d, xh_ref, rs_ref, lg_ref, y_ref, gw_ref, wres, alpha, dxa_ref, dy_ref, acc_ref, row0):
    xh = xh_ref[...]
    dxh = d * lg_ref[...]
    m1 = jnp.mean(dxh, axis=-1, keepdims=True)
    m2 = jnp.mean(dxh * xh, axis=-1, keepdims=True)
    dz = rs_ref[...] * (dxh - m1 - xh * m2)
    dxa_ref[...] = alpha * dz
    dy_ref[...] = (gw_ref[...] * dz).astype(BF16)
    acc_ref[row0:row0 + 1, :] += jnp.sum(d * xh, axis=0, keepdims=True)
    acc_ref[row0 + 1:row0 + 2, :] += jnp.sum(d, axis=0, keepdims=True)
    acc_ref[row0 + 2:row0 + 3, :] += jnp.sum((wres * dz) * y_ref[...].astype(F32), axis=0, keepdims=True)


def ln_res_bwd(dxn, xhat, rstd, lg, y, gw, wres, alpha, name):
    S, D = dxn.shape
    tm = min(MATMUL_ROW_TILE, S)

    def body(dxn_ref, xh_ref, rs_ref, lg_ref, y_ref, gw_ref, dxa_ref, dy_ref, acc_ref):
        @pl.when(pl.program_id(0) == 0)
        def _():
            acc_ref[...] = jnp.zeros_like(acc_ref)

        _ln_res_bwd_tile(dxn_ref[...], xh_ref, rs_ref, lg_ref, y_ref, gw_ref, wres, alpha, dxa_ref, dy_ref, acc_ref, 0)

    vec = pl.BlockSpec((1, D), _fixed2)
    tile = pl.BlockSpec((tm, D), _row(0))
    return pl.pallas_call(
        body, name=name, grid=(S // tm,),
        in_specs=[tile, tile, pl.BlockSpec((tm, 1), _row(0)), vec, tile, vec],
        out_specs=[tile, tile, pl.BlockSpec((8, D), _fixed2)],
        out_shape=[jax.ShapeDtypeStruct((S, D), F32), jax.ShapeDtypeStruct((S, D), BF16),
                   jax.ShapeDtypeStruct((8, D), F32)],
        compiler_params=_params(("arbitrary",)),
    )(dxn, xhat, rstd, lg, y, gw)


def ffn_act_bwd(dy, wd, gu, after, name):
    S, D = dy.shape
    K = wd.shape[0]
    tm = min(ROW_TILE, S)

    def body(dy_ref, wd_ref, gu_ref, after_ref, o_ref):
        da = lax.dot_general(dy_ref[...], wd_ref[...], NT, preferred_element_type=F32).astype(BF16)
        g = gu_ref[:, :K]
        u = gu_ref[:, K:]
        sg = _sigmoid(g)
        o_ref[:, :K] = da * u * (sg * (1.0 + g * (1.0 - sg)))
        o_ref[:, K:] = da * (g * sg)

    return pl.pallas_call(
        body, name=name, grid=(S // tm,),
        in_specs=[pl.BlockSpec((tm, D), _row(0)), _resident((K, D)), pl.BlockSpec((tm, 2 * K), _row(0)), ANY],
        out_specs=pl.BlockSpec((tm, 2 * K), _row(0)),
        out_shape=jax.ShapeDtypeStruct((S, 2 * K), BF16),
        compiler_params=_params(("parallel",)),
    )(dy, wd, gu, after)


def matmul_nt(a, w, after, name):
    S, D = a.shape
    K = w.shape[0]
    tm = min(MATMUL_ROW_TILE, S)

    def body(a_ref, w_ref, after_ref, o_ref):
        o_ref[...] = lax.dot_general(a_ref[...], w_ref[...], NT, preferred_element_type=F32).astype(BF16)

    return pl.pallas_call(
        body, name=name, grid=(S // tm,),
        in_specs=[pl.BlockSpec((tm, D), _row(0)), _resident((K, D)), ANY],
        out_specs=pl.BlockSpec((tm, K), _row(0)),
        out_shape=jax.ShapeDtypeStruct((S, K), BF16),
        compiler_params=_params(("parallel",)),
    )(a, w, after)


def dgrad_mod(dpre, w, dxa, xin, scl, prev, alpha, name):
    S, D = xin.shape
    NS, _, n = w.shape
    tm = min(ROW_TILE, S)
    wres = prev[5] if prev is not None else None

    def body(*refs):
        dp_ref, w_ref, dxa_ref, xin_ref, scl_ref = refs[:5]
        acc_ref = refs[-1]

        @pl.when(pl.program_id(0) == 0)
        def _():
            acc_ref[...] = jnp.zeros_like(acc_ref)

        dh = jnp.zeros((tm, D), F32)
        for s in range(NS):
            dh = dh + lax.dot_general(dp_ref[:, s * n:(s + 1) * n].astype(BF16), w_ref[s], NT,
                                      preferred_element_type=F32)
        dx = dxa_ref[...] + dh * (1.0 + scl_ref[...])
        acc_ref[0:1, :] += jnp.sum(dh * xin_ref[...], axis=0, keepdims=True)
        acc_ref[1:2, :] += jnp.sum(dh, axis=0, keepdims=True)
        if prev is None:
            refs[5][...] = dx
        else:
            xh_ref, rs_ref, lg_ref, y_ref, gw_ref, pdxa_ref, pdy_ref = refs[5:12]
            _ln_res_bwd_tile(dx, xh_ref, rs_ref, lg_ref, y_ref, gw_ref, wres, alpha, pdxa_ref, pdy_ref, acc_ref, 2)

    tile = pl.BlockSpec((tm, D), _row(0))
    vec = pl.BlockSpec((1, D), _fixed2)
    in_specs = [pl.BlockSpec((tm, NS * n), _row(0)), _resident((NS, D, n)), tile, tile, vec]
    args = [dpre, w, dxa, xin, scl]
    if prev is None:
        out_specs = [tile]
        out_shape = [jax.ShapeDtypeStruct((S, D), F32)]
    else:
        in_specs += [tile, pl.BlockSpec((tm, 1), _row(0)), vec, tile, vec]
        args += list(prev[:5])
        out_specs = [tile, tile]
        out_shape = [jax.ShapeDtypeStruct((S, D), F32), jax.ShapeDtypeStruct((S, D), BF16)]
    return pl.pallas_call(
        body, name=name, grid=(S // tm,), in_specs=in_specs,
        out_specs=out_specs + [pl.BlockSpec((8, D), _fixed2)],
        out_shape=out_shape + [jax.ShapeDtypeStruct((8, D), F32)],
        compiler_params=_params(("arbitrary",)),
    )(*args)


PAIR_COLLECTIVE_ID = 1


def wgrad_pair(a, b, J, kb, nb, a_block, b_block, half_idx, name):
    S = b.shape[0]
    ts = min(WGRAD_ROWS, S)
    nsteps = S // ts

    def body(h_ref, a_ref, b_ref, o_ref, acc_ref, send_buf, recv_buf, send_sems, recv_sems):
        jj, si = pl.program_id(0), pl.program_id(1)
        x, y, c = _coords()
        j = lax.rem(jj, J)
        last = si == nsteps - 1

        def copy(blk):
            return pltpu.make_async_remote_copy(
                src_ref=send_buf.at[blk], dst_ref=recv_buf.at[blk], send_sem=send_sems.at[blk],
                recv_sem=recv_sems.at[blk], device_id=(x, y, 1 - c), device_id_type=MESH)

        @pl.when(jnp.logical_and(jj == 0, si == 0))
        def _():
            barrier = pltpu.get_barrier_semaphore()
            pl.semaphore_signal(barrier, inc=1, device_id=(x, y, 1 - c), device_id_type=MESH)
            pl.semaphore_wait(barrier, 1)

        @pl.when(si == 0)
        def _():
            acc_ref[...] = jnp.zeros_like(acc_ref)

        acc_ref[...] += lax.dot_general(a_ref[...], b_ref[...].astype(BF16), TN, preferred_element_type=F32)

        @pl.when(jnp.logical_and(last, jj < J))
        def _():
            send_buf[j] = acc_ref[...].astype(BF16)
            copy(j).start()

        @pl.when(jnp.logical_and(last, jj >= J))
        def _():
            copy(j).wait_recv()
            o_ref[...] = (acc_ref[...] + recv_buf[j].astype(F32)).astype(BF16)

        @pl.when(jnp.logical_and(last, jj == 2 * J - 1))
        def _():
            for blk in range(J):
                copy(blk).wait_send()

    def half(jj, h):
        return jnp.where(jj < J, 1 - h[0], h[0])

    return pl.pallas_call(
        body, name=name,
        grid_spec=pltpu.PrefetchScalarGridSpec(
            num_scalar_prefetch=1, grid=(2 * J, nsteps),
            in_specs=[pl.BlockSpec((ts, kb), lambda jj, s, h: (s, a_block(lax.rem(jj, J), half(jj, h)))),
                      pl.BlockSpec((ts, nb), lambda jj, s, h: (s, b_block(lax.rem(jj, J), half(jj, h))))],
            out_specs=pl.BlockSpec((None, kb, nb), lambda jj, s, h: (jnp.maximum(jj - J, 0), 0, 0)),
            scratch_shapes=[pltpu.VMEM((kb, nb), F32), pltpu.VMEM((J, kb, nb), BF16), pltpu.VMEM((J, kb, nb), BF16),
                            pltpu.SemaphoreType.DMA((J,)), pltpu.SemaphoreType.DMA((J,))]),
        out_shape=jax.ShapeDtypeStruct((J, kb, nb), BF16),
        compiler_params=pltpu.CompilerParams(
            vmem_limit_bytes=VMEM_LIMIT_MB * 1024 * 1024, dimension_semantics=("arbitrary", "arbitrary"),
            collective_id=PAIR_COLLECTIVE_ID),
    )(half_idx, a, b)


def _window_mask():
    t = lax.broadcasted_iota(jnp.int32, (GMLP_WINDOW, GMLP_WINDOW), 0)
    s = lax.broadcasted_iota(jnp.int32, (GMLP_WINDOW, GMLP_WINDOW), 1)
    return ((s // CHUNK) <= (t // CHUNK)).astype(F32)


def sgu_fwd(pre, glg, glb, ws, bst, name):
    S, H2 = pre.shape
    H = H2 // 2
    W, G = GMLP_WINDOW, GMLP_GROUPS
    gd = H // G
    tm = min(ROW_TILE, S)

    def body(pre_ref, glg_ref, glb_ref, ws_ref, bst_ref, q_ref):
        u = _gelu(pre_ref[:, :H])
        v = _gelu(pre_ref[:, H:])
        mu = jnp.mean(v, axis=-1, keepdims=True)
        vc = v - mu
        var = jnp.mean(vc * vc, axis=-1, keepdims=True)
        vn = ((vc * lax.rsqrt(var + LN_EPS)) * glg_ref[...] + glb_ref[...]).astype(BF16)
        mask = _window_mask()
        for g in range(G):
            wsg = (ws_ref[g] * mask).astype(BF16)
            bcol = bst_ref[:, g:g + 1]
            for wi in range(tm // W):
                rows = slice(wi * W, (wi + 1) * W)
                cols = slice(g * gd, (g + 1) * gd)
                s = jnp.dot(wsg, vn[rows, cols], preferred_element_type=F32) + bcol
                q_ref[rows, cols] = (u[rows, cols] * s).astype(BF16)

    return pl.pallas_call(
        body, name=name, grid=(S // tm,),
        in_specs=[pl.BlockSpec((tm, H2), _row(0)), pl.BlockSpec((1, H), _fixed2), pl.BlockSpec((1, H), _fixed2),
                  pl.BlockSpec((G, W, W), _fixed3), pl.BlockSpec((W, G), _fixed2)],
        out_specs=pl.BlockSpec((tm, H), _row(0)),
        out_shape=jax.ShapeDtypeStruct((S, H), BF16),
        compiler_params=_params(("parallel",)),
    )(pre, glg, glb, ws, bst)


def sgu_bwd(dq, pre, glg, glb, ws, bst, name):
    S, H2 = pre.shape
    H = H2 // 2
    W, G = GMLP_WINDOW, GMLP_GROUPS
    gd = H // G
    tm = min(ROW_TILE, S)

    def body(dq_ref, pre_ref, glg_ref, glb_ref, ws_ref, bst_ref,
             dpre_ref, dws_ref, dss_ref, dgl_ref, dbin_ref, du_s, dvn_s):
        @pl.when(pl.program_id(0) == 0)
        def _():
            dws_ref[...] = jnp.zeros_like(dws_ref)
            dss_ref[...] = jnp.zeros_like(dss_ref)
            dgl_ref[...] = jnp.zeros_like(dgl_ref)
            dbin_ref[...] = jnp.zeros_like(dbin_ref)

        pu = pre_ref[:, :H]
        pv = pre_ref[:, H:]
        u = _gelu(pu)
        v = _gelu(pv)
        mu = jnp.mean(v, axis=-1, keepdims=True)
        vc = v - mu
        var = jnp.mean(vc * vc, axis=-1, keepdims=True)
        rstd = lax.rsqrt(var + LN_EPS)
        vhat = vc * rstd
        vn = (vhat * glg_ref[...] + glb_ref[...]).astype(BF16)
        mask = _window_mask()
        for g in range(G):
            wsg = (ws_ref[g] * mask).astype(BF16)
            bcol = bst_ref[:, g:g + 1]
            cols = slice(g * gd, (g + 1) * gd)
            for wi in range(tm // W):
                rows = slice(wi * W, (wi + 1) * W)
                vblk = vn[rows, cols]
                s = jnp.dot(wsg, vblk, preferred_element_type=F32) + bcol
                dqb = dq_ref[rows, cols].astype(F32)
                du_s[rows, cols] = dqb * s
                ds = dqb * u[rows, cols]
                dss_ref[:, cols] += ds
                dsb = ds.astype(BF16)
                dvn_s[rows, cols] = lax.dot_general(wsg, dsb, TN, preferred_element_type=F32)
                dws_ref[g] += lax.dot_general(dsb, vblk, NT, preferred_element_type=F32) * mask
        dvn = dvn_s[...]
        dgl_ref[0:1, :] += jnp.sum(dvn * vhat, axis=0, keepdims=True)
        dgl_ref[1:2, :] += jnp.sum(dvn, axis=0, keepdims=True)
        dvh = dvn * glg_ref[...]
        m1 = jnp.mean(dvh, axis=-1, keepdims=True)
        m2 = jnp.mean(dvh * vhat, axis=-1, keepdims=True)
        dv = rstd * (dvh - m1 - vhat * m2)
        dpu = du_s[...] * _gelu_grad(pu)
        dpv = dv * _gelu_grad(pv)
        dbin_ref[0:1, :H] += jnp.sum(dpu, axis=0, keepdims=True)
        dbin_ref[0:1, H:] += jnp.sum(dpv, axis=0, keepdims=True)
        dpre_ref[:, :H] = dpu.astype(BF16)
        dpre_ref[:, H:] = dpv.astype(BF16)

    return pl.pallas_call(
        body, name=name, grid=(S // tm,),
        in_specs=[pl.BlockSpec((tm, H), _row(0)), pl.BlockSpec((tm, H2), _row(0)), pl.BlockSpec((1, H), _fixed2),
                  pl.BlockSpec((1, H), _fixed2), pl.BlockSpec((G, W, W), _fixed3), pl.BlockSpec((W, G), _fixed2)],
        out_specs=[pl.BlockSpec((tm, H2), _row(0)), pl.BlockSpec((G, W, W), _fixed3), pl.BlockSpec((W, H), _fixed2),
                   pl.BlockSpec((8, H), _fixed2), pl.BlockSpec((8, H2), _fixed2)],
        out_shape=[jax.ShapeDtypeStruct((S, H2), BF16), jax.ShapeDtypeStruct((G, W, W), F32),
                   jax.ShapeDtypeStruct((W, H), F32), jax.ShapeDtypeStruct((8, H), F32),
                   jax.ShapeDtypeStruct((8, H2), F32)],
        scratch_shapes=[pltpu.VMEM((tm, H), F32), pltpu.VMEM((tm, H), F32)],
        compiler_params=_params(("arbitrary",)),
    )(dq, pre, glg, glb, ws, bst)


def group_lane_sum(dss, name):
    W, H = dss.shape
    gd = H // GMLP_GROUPS

    def body(d_ref, o_ref):
        j = lax.broadcasted_iota(jnp.int32, (H, LANES), 0)
        g = lax.broadcasted_iota(jnp.int32, (H, LANES), 1)
        ind = ((j // gd) == g).astype(F32)
        o_ref[...] = jnp.dot(d_ref[...], ind, preferred_element_type=F32, precision=HIGHEST)

    return pl.pallas_call(
        body, name=name, in_specs=[VMEM_SPEC], out_specs=VMEM_SPEC,
        out_shape=jax.ShapeDtypeStruct((W, LANES), F32), compiler_params=_params(),
    )(dss)


def _attn_load(j, cps, q_ref, k_ref, v_ref):
    r = lax.broadcasted_iota(jnp.int32, (CHUNK, BAND), 1)
    chunks = []
    for cc in range(cps):
        start = pl.multiple_of((j * cps + cc) * CHUNK, CHUNK)
        chunks.append((q_ref[cc * CHUNK:(cc + 1) * CHUNK, :], k_ref[pl.ds(start, BAND), :],
                       v_ref[pl.ds(start, BAND), :], (r + start) >= LEFT_PAD))
    return chunks


def _attn_probs(chunks, b_ref, sels, scale):
    qms = [[jnp.where(sel, q2, jnp.zeros_like(q2)) for sel in sels] for q2, _, _, _ in chunks]
    raw = [[lax.dot_general(qm, k2, NT, preferred_element_type=F32) for qm in qms[cc]]
           for cc, (_, k2, _, _) in enumerate(chunks)]
    probs = []
    for cc, (_, _, _, valid) in enumerate(chunks):
        row = []
        for sub in range(2):
            s = jnp.where(valid, raw[cc][sub] * scale + b_ref[sub], -jnp.inf)
            e = jnp.exp(s - jnp.max(s, axis=-1, keepdims=True))
            row.append(e / jnp.sum(e, axis=-1, keepdims=True))
        probs.append(row)
    return qms, probs


def attn_fwd(q, kpad, vpad, bias, name):
    S, D = q.shape
    HP = D // LANES
    cps = min(ATTN_CHUNKS_PER_STEP, S // CHUNK)
    tq = cps * CHUNK
    scale = HEAD_DIM ** -0.5

    def body(q_ref, k_ref, v_ref, b_ref, o_ref):
        sel0 = lax.broadcasted_iota(jnp.int32, (CHUNK, LANES), 1) < HEAD_DIM
        chunks = _attn_load(pl.program_id(1), cps, q_ref, k_ref, v_ref)
        _, probs = _attn_probs(chunks, b_ref, (sel0, jnp.logical_not(sel0)), scale)
        outs = [[jnp.dot(probs[cc][sub].astype(BF16), v2, preferred_element_type=F32) for sub in range(2)]
                for cc, (_, _, v2, _) in enumerate(chunks)]
        o_ref[...] = jnp.concatenate([jnp.where(sel0, o[0], o[1]) for o in outs], axis=0).astype(BF16)

    kv_spec = pl.BlockSpec((S + LEFT_PAD, LANES), lambda h, j: (0, h))
    return pl.pallas_call(
        body, name=name, grid=(HP, S // tq),
        in_specs=[pl.BlockSpec((tq, LANES), lambda h, j: (j, h)), kv_spec, kv_spec,
                  pl.BlockSpec((2, CHUNK, BAND), lambda h, j: (h, 0, 0))],
        out_specs=pl.BlockSpec((tq, LANES), lambda h, j: (j, h)),
        out_shape=jax.ShapeDtypeStruct((S, D), BF16),
        compiler_params=_params(("parallel", "parallel")),
    )(q, kpad, vpad, bias)


def attn_bwd(q, do, kpad, vpad, bias, dk_in, dv_in, name):
    S, D = q.shape
    HP = D // LANES
    NH = 2 * HP
    cps = min(ATTN_CHUNKS_PER_STEP, S // CHUNK)
    tq = cps * CHUNK
    nj = S // tq
    scale = HEAD_DIM ** -0.5

    def body(q_ref, do_ref, k_ref, v_ref, b_ref, dki_ref, dvi_ref, dq_ref, dk_ref, dv_ref, db_ref, dk_acc, dv_acc):
        j = pl.program_id(1)

        @pl.when(j == 0)
        def _():
            dk_acc[:LEFT_PAD, :] = jnp.zeros((LEFT_PAD, LANES), F32)
            dv_acc[:LEFT_PAD, :] = jnp.zeros((LEFT_PAD, LANES), F32)
            dk_acc[LEFT_PAD:, :] = dki_ref[...]
            dv_acc[LEFT_PAD:, :] = dvi_ref[...]
            db_ref[...] = jnp.zeros_like(db_ref)

        sel0 = lax.broadcasted_iota(jnp.int32, (CHUNK, LANES), 1) < HEAD_DIM
        sels = (sel0, jnp.logical_not(sel0))
        chunks = _attn_load(j, cps, q_ref, k_ref, v_ref)
        pairs = [(cc, sub) for cc in range(cps) for sub in range(2)]
        qms, probs = _attn_probs(chunks, b_ref, sels, scale)
        doms = [[jnp.where(sel, do_ref[cc * CHUNK:(cc + 1) * CHUNK, :], jnp.zeros((CHUNK, LANES), BF16))
                 for sel in sels] for cc in range(cps)]
        dps = {(cc, sub): lax.dot_general(doms[cc][sub], chunks[cc][2], NT, preferred_element_type=F32)
               for cc, sub in pairs}
        dss = {}
        for cc, sub in pairs:
            p = probs[cc][sub]
            dss[cc, sub] = p * (dps[cc, sub] - jnp.sum(dps[cc, sub] * p, axis=-1, keepdims=True))
        dsb = {key: ds.astype(BF16) for key, ds in dss.items()}
        dqs = {(cc, sub): jnp.dot(dsb[cc, sub], chunks[cc][1], preferred_element_type=F32) * scale
               for cc, sub in pairs}
        dks = {(cc, sub): lax.dot_general(dsb[cc, sub], qms[cc][sub], TN, preferred_element_type=F32) * scale
               for cc, sub in pairs}
        dvs = {(cc, sub): lax.dot_general(probs[cc][sub].astype(BF16), doms[cc][sub], TN,
                                          preferred_element_type=F32) for cc, sub in pairs}
        dq_ref[...] = jnp.concatenate([jnp.where(sel0, dqs[cc, 0], dqs[cc, 1]) for cc in range(cps)],
                                      axis=0).astype(BF16)
        for sub in range(2):
            total = dss[0, sub]
            for cc in range(1, cps):
                total = total + dss[cc, sub]
            db_ref[sub] += total
        dk_parts = [dks[cc, 0] + dks[cc, 1] for cc in range(cps)]
        dv_parts = [dvs[cc, 0] + dvs[cc, 1] for cc in range(cps)]

        def window(parts):
            blocks = []
            for rb in range(cps - 1 + BAND // CHUNK):
                acc = None
                for cc in range(cps):
                    b = rb - cc
                    if 0 <= b < BAND // CHUNK:
                        piece = parts[cc][b * CHUNK:(b + 1) * CHUNK, :]
                        acc = piece if acc is None else acc + piece
                blocks.append(acc)
            return jnp.concatenate(blocks, axis=0)

        span = pl.ds(pl.multiple_of(j * cps * CHUNK, CHUNK), (cps - 1) * CHUNK + BAND)
        dk_acc[span, :] += window(dk_parts)
        dv_acc[span, :] += window(dv_parts)

        @pl.when(j == nj - 1)
        def _():
            dk_ref[...] = dk_acc[LEFT_PAD:, :]
            dv_ref[...] = dv_acc[LEFT_PAD:, :]

    q_spec = pl.BlockSpec((tq, LANES), lambda h, j: (j, h))
    kv_spec = pl.BlockSpec((S + LEFT_PAD, LANES), lambda h, j: (0, h))
    col_spec = pl.BlockSpec((S, LANES), lambda h, j: (0, h))
    b_spec = pl.BlockSpec((2, CHUNK, BAND), lambda h, j: (h, 0, 0))
    return pl.pallas_call(
        body, name=name, grid=(HP, nj),
        in_specs=[q_spec, q_spec, kv_spec, kv_spec, b_spec, col_spec, col_spec],
        out_specs=[q_spec, col_spec, col_spec, b_spec],
        out_shape=[jax.ShapeDtypeStruct((S, D), BF16), jax.ShapeDtypeStruct((S, D), F32),
                   jax.ShapeDtypeStruct((S, D), F32), jax.ShapeDtypeStruct((NH, CHUNK, BAND), F32)],
        scratch_shapes=[pltpu.VMEM((S + LEFT_PAD, LANES), F32), pltpu.VMEM((S + LEFT_PAD, LANES), F32)],
        compiler_params=_params(("parallel", "arbitrary")),
    )(q, do, kpad, vpad, bias, dk_in, dv_in)


def _rel_onehot(t):
    r = lax.broadcasted_iota(jnp.int32, (BAND, N_REL), 0)
    i = lax.broadcasted_iota(jnp.int32, (BAND, N_REL), 1)
    idx = jnp.clip(t + LEFT_PAD - r, -(CHUNK - 1), MAX_REL) + (CHUNK - 1)
    return (idx == i).astype(BF16)


def _split3(v):
    hi = v.astype(BF16)
    rest = v - hi.astype(F32)
    mid = rest.astype(BF16)
    return hi, mid, (rest - mid.astype(F32)).astype(BF16)


def _dot_onehot(parts, onehot, dims):
    hi, mid, lo = [lax.dot_general(p, onehot, dims, preferred_element_type=F32) for p in parts]
    return (hi + mid) + lo


def bias_expand(rb, name):
    NH = rb.shape[0]

    def body(rb_ref, o_ref):
        parts = _split3(rb_ref[...])

        def step(t, carry):
            o_ref[t] = _dot_onehot(parts, _rel_onehot(t), NT)
            return carry

        lax.fori_loop(0, CHUNK, step, 0)

    return pl.pallas_call(
        body, name=name, in_specs=[VMEM_SPEC], out_specs=VMEM_SPEC,
        out_shape=jax.ShapeDtypeStruct((CHUNK, NH, BAND), F32), compiler_params=_params(),
    )(rb)


def bias_grad(dsum, name):
    NH = dsum.shape[1]

    def body(d_ref, o_ref):
        def step(t, acc):
            return acc + _dot_onehot(_split3(d_ref[t]), _rel_onehot(t), (((1,), (0,)), ((), ())))

        o_ref[...] = lax.fori_loop(0, CHUNK, step, jnp.zeros((NH, N_REL), F32))

    return pl.pallas_call(
        body, name=name, in_specs=[VMEM_SPEC], out_specs=VMEM_SPEC,
        out_shape=jax.ShapeDtypeStruct((NH, N_REL), F32), compiler_params=_params(),
    )(dsum)


def loss_grad(y, tgt, name):
    S, D = y.shape
    tm = min(MATMUL_ROW_TILE, S)

    def body(y_ref, t_ref, d_ref, acc_ref):
        @pl.when(pl.program_id(0) == 0)
        def _():
            acc_ref[...] = jnp.zeros_like(acc_ref)

        err = y_ref[...] - t_ref[...]
        d_ref[...] = err * (1.0 / D)
        acc_ref[0:1, :] += jnp.sum(err * err, axis=0, keepdims=True)

    tile = pl.BlockSpec((tm, D), _row(0))
    return pl.pallas_call(
        body, name=name, grid=(S // tm,), in_specs=[tile, tile],
        out_specs=[tile, pl.BlockSpec((8, D), _fixed2)],
        out_shape=[jax.ShapeDtypeStruct((S, D), F32), jax.ShapeDtypeStruct((8, D), F32)],
        compiler_params=_params(("arbitrary",)),
    )(y, tgt)


def _col_tile(n):
    for t in (768, 512, 256, 128):
        if n % t == 0:
            return t
    return n


def ada_fwd(c_all, w, b, name):
    L, D, n = w.shape
    tn = _col_tile(n)

    def body(c_ref, w_ref, b_ref, o_ref):
        cv = c_ref[...]
        ca = cv * _sigmoid(cv)
        o_ref[...] = jnp.dot(ca, w_ref[...], preferred_element_type=F32, precision=HIGHEST) + b_ref[...]

    return pl.pallas_call(
        body, name=name, grid=(L, n // tn),
        in_specs=[pl.BlockSpec((N_DEV, D), lambda l, j: (0, 0)), pl.BlockSpec((None, D, tn), lambda l, j: (l, 0, j)),
                  pl.BlockSpec((None, 1, tn), lambda l, j: (l, 0, j))],
        out_specs=pl.BlockSpec((None, N_DEV, tn), lambda l, j: (l, 0, j)),
        out_shape=jax.ShapeDtypeStruct((L, N_DEV, n), F32),
        compiler_params=_params(("parallel", "parallel")),
    )(c_all, w, b)


def ada_wgrad(c_all_t, dmod, name):
    L, _, n = dmod.shape
    D = c_all_t.shape[0]
    tn = _col_tile(n)

    def body(c_ref, d_ref, o_ref):
        cv = c_ref[...]
        ca = cv * _sigmoid(cv)
        o_ref[...] = jnp.dot(ca, d_ref[...], preferred_element_type=F32, precision=HIGHEST)

    return pl.pallas_call(
        body, name=name, grid=(L, n // tn),
        in_specs=[pl.BlockSpec((D, N_DEV), lambda l, j: (0, 0)), pl.BlockSpec((None, N_DEV, tn), lambda l, j: (l, 0, j))],
        out_specs=pl.BlockSpec((None, D, tn), lambda l, j: (l, 0, j)),
        out_shape=jax.ShapeDtypeStruct((L, D, n), F32),
        compiler_params=_params(("parallel", "parallel")),
    )(c_all_t, dmod)


ELEMENTWISE_BLOCK_BYTES = 3 * 1024 * 1024


def _elementwise_rows(rows, row_bytes):
    for t in (4096, 2048, 1024, 512, 256, 128, 64, 32, 16):
        if rows % t == 0 and t * row_bytes <= ELEMENTWISE_BLOCK_BYTES:
            return t
    return rows


def sum_leading(a, name):
    n, M, N = a.shape
    tr = _elementwise_rows(M, n * N * 4)

    def body(a_ref, o_ref):
        acc = a_ref[0]
        for i in range(1, n):
            acc = acc + a_ref[i]
        o_ref[...] = acc

    return pl.pallas_call(
        body, name=name, grid=(M // tr,),
        in_specs=[pl.BlockSpec((n, tr, N), lambda i: (0, i, 0))],
        out_specs=pl.BlockSpec((tr, N), _row(0)),
        out_shape=jax.ShapeDtypeStruct((M, N), F32),
        compiler_params=_params(("parallel",)),
    )(a)


def chip_sum(psum, land, chip_idx, transposed, name):
    n, M, N = psum.shape
    tr = M if transposed else _elementwise_rows(M, N * 8)

    def body(s_ref, p_ref, a_ref, b_ref, c_ref, o_ref):
        total = ((p_ref[...].astype(F32) + a_ref[...].astype(F32)) + b_ref[...].astype(F32)) + c_ref[...].astype(F32)
        o_ref[...] = jnp.transpose(total) if transposed else total

    def entry(j):
        return pl.BlockSpec((None, tr, N), lambda i, s: ((s[0] + j) % n, i, 0))

    out_block, out_dims = ((N, tr), (N, M)) if transposed else ((tr, N), (M, N))
    return pl.pallas_call(
        body, name=name,
        grid_spec=pltpu.PrefetchScalarGridSpec(
            num_scalar_prefetch=1, grid=(M // tr,),
            in_specs=[entry(0), entry(1), entry(2), entry(3)],
            out_specs=pl.BlockSpec(out_block, lambda i, s: (0, 0) if transposed else (i, 0))),
        out_shape=jax.ShapeDtypeStruct(out_dims, F32),
        compiler_params=_params(("parallel",)),
    )(chip_idx, psum, land, land, land)


def adamw(w, g, m, v, name):
    M, N = w.shape
    tr = _elementwise_rows(M, N * 4)
    c1 = 1.0 - ADAM_B1 ** ADAM_STEP
    c2 = 1.0 - ADAM_B2 ** ADAM_STEP

    def body(w_ref, g_ref, m_ref, v_ref, d_ref, nm_ref, nv_ref):
        g = g_ref[...]
        nm = ADAM_B1 * m_ref[...] + (1.0 - ADAM_B1) * g
        nv = ADAM_B2 * v_ref[...] + (1.0 - ADAM_B2) * (g * g)
        d_ref[...] = -ADAM_LR * ((nm / c1) / (jnp.sqrt(nv / c2) + ADAM_EPS) + ADAM_WD * w_ref[...])
        nm_ref[...] = nm
        nv_ref[...] = nv

    spec = pl.BlockSpec((tr, N), _row(0))
    shp = jax.ShapeDtypeStruct((M, N), F32)
    return pl.pallas_call(
        body, name=name, grid=(M // tr,), in_specs=[spec] * 4, out_specs=[spec] * 3, out_shape=[shp] * 3,
        compiler_params=_params(("parallel",)),
    )(w, g, m, v)


def _coords():
    return lax.axis_index("x"), lax.axis_index("y"), lax.axis_index("c")


def all_gather8(block, name):
    m_per, n = block.shape

    def body(x_ref, out_ref, send_sems, recv_sems, local_sem):
        x, y, c = _coords()
        me, sibling = (x, y, c), (x, y, 1 - c)
        chips = [(1 - x, y), (x, 1 - y), (1 - x, 1 - y)]

        def rows(px, py, pc):
            return out_ref.at[pl.ds((4 * px + 2 * py + pc) * m_per, m_per), :]

        def copy(k, blk, to, src=None):
            return pltpu.make_async_remote_copy(
                src_ref=rows(*blk) if src is None else src, dst_ref=rows(*blk),
                send_sem=send_sems.at[k], recv_sem=recv_sems.at[k], device_id=to, device_id_type=MESH)

        mine = pltpu.make_async_copy(x_ref, rows(*me), local_sem)
        mine.start()
        first = [copy(0, me, sibling, src=x_ref)]
        first += [copy(1 + j, me, (*chip, c), src=x_ref) for j, chip in enumerate(chips)]
        for cp in first:
            cp.start()
        passed = [copy(4 + j, (*chip, c), sibling) for j, chip in enumerate(chips)]
        for j, chip in enumerate(chips):
            copy(1 + j, (*chip, c), me).wait_recv()
            passed[j].start()
        copy(0, sibling, me).wait_recv()
        for j, chip in enumerate(chips):
            copy(4 + j, (*chip, 1 - c), me).wait_recv()
        for cp in first + passed:
            cp.wait_send()
        mine.wait()

    return pl.pallas_call(
        body, name=name, in_specs=[VMEM_SPEC], out_specs=VMEM_SPEC,
        out_shape=jax.ShapeDtypeStruct((N_DEV * m_per, n), block.dtype),
        scratch_shapes=[pltpu.SemaphoreType.DMA((7,)), pltpu.SemaphoreType.DMA((7,)), pltpu.SemaphoreType.DMA],
        compiler_params=_params(),
    )(block)


def _other_chips(x, y):
    return [(1 - x, y), (x, 1 - y), (1 - x, 1 - y)]


HBM_SPEC = pl.BlockSpec(memory_space=pltpu.HBM)
SEM_SPEC = pl.BlockSpec(memory_space=pltpu.SEMAPHORE)
DATAFLOW = pltpu.SideEffectType.DATAFLOW_SIDE_EFFECTING


def _chip_peers(x, y, c):
    return [(px, py, c) for px, py in _other_chips(x, y)]


def _sibling_peer(x, y, c):
    return [(x, y, 1 - c)]


def _weight_desc(_, k, land_ref, peer, me):
    h = land_ref.shape[1] // 2
    rows = pl.ds(me[2] * h, h)
    mine = land_ref.at[2 * me[0] + me[1], rows, :]
    return mine, mine, land_ref.at[2 * peer[0] + peer[1], rows, :]


def _grad_desc(psum_ref, k, land_ref, peer, me):
    return psum_ref.at[2 * peer[0] + peer[1]], land_ref.at[2 * me[0] + me[1]], land_ref.at[2 * peer[0] + peer[1]]


def _pair_desc(grad_ref, k, land_ref, peer, me):
    h = land_ref.shape[1]
    return grad_ref.at[:, pl.ds(peer[2] * h, h), :], land_ref, land_ref


def _whole_desc(src_ref, k, land_ref, peer, me):
    return src_ref, land_ref, land_ref


def exchange_start(srcs, lands, units, groups, desc, peers, after, name):
    n_s, n_l, n_g = len(srcs), len(lands), len(groups)
    n_p = len(peers(0, 0, 0))

    def body(*refs):
        s_refs, l_refs = refs[:n_s], refs[n_s:n_s + n_l]
        outs = refs[n_s + n_l + 1:]
        sems, token = outs[:2 * n_g], outs[-1]
        me = _coords()
        for g, ids in enumerate(groups):
            for i, u in enumerate(ids):
                si, k = units[u]
                for j, peer in enumerate(peers(*me)):
                    src, dst, _ = desc(s_refs[si] if s_refs else None, k, l_refs[u], peer, me)
                    pltpu.make_async_remote_copy(
                        src_ref=src, dst_ref=dst, send_sem=sems[2 * g].at[n_p * i + j],
                        recv_sem=sems[2 * g + 1].at[n_p * i + j], device_id=peer, device_id_type=MESH).start()
        token[...] = jnp.zeros_like(token)

    arrs = list(srcs) + list(lands)
    sem_shapes = [pltpu.SemaphoreType.DMA((n_p * len(ids),)) for ids in groups for _ in range(2)]
    outs = pl.pallas_call(
        body, name=name,
        in_specs=[HBM_SPEC] * len(arrs) + [ANY],
        out_specs=[SEM_SPEC] * (2 * n_g) + [HBM_SPEC] * len(arrs) + [VMEM_SPEC],
        out_shape=sem_shapes + [pltpu.HBM(a.shape, a.dtype) for a in arrs] + [jax.ShapeDtypeStruct((8, LANES), F32)],
        input_output_aliases={i: 2 * n_g + i for i in range(len(arrs))},
        compiler_params=pltpu.CompilerParams(has_side_effects=DATAFLOW),
    )(*[pltpu.with_memory_space_constraint(a, pltpu.HBM) for a in arrs], after)
    sems = outs[:2 * n_g]
    thru = outs[2 * n_g:2 * n_g + len(arrs)]
    return sems, list(thru[:n_s]), list(thru[n_s:]), outs[-1]


def exchange_wait(srcs, lands, units, send_sem, recv_sem, desc, peers, after, name):
    n_s, n_l = len(srcs), len(lands)
    n_p = len(peers(0, 0, 0))

    def body(*refs):
        s_refs, l_refs = refs[:n_s], refs[n_s:n_s + n_l]
        send_sems, recv_sems = refs[n_s + n_l], refs[n_s + n_l + 1]
        me = _coords()
        for i, (si, k) in enumerate(units):
            for j, peer in enumerate(peers(*me)):
                src, _, mine = desc(s_refs[si] if s_refs else None, k, l_refs[i], peer, me)
                cp = pltpu.make_async_remote_copy(
                    src_ref=src, dst_ref=mine, send_sem=send_sems.at[n_p * i + j], recv_sem=recv_sems.at[n_p * i + j],
                    device_id=peer, device_id_type=MESH)
                cp.wait_send()
                cp.wait_recv()

    arrs = list(srcs) + list(lands)
    outs = pl.pallas_call(
        body, name=name,
        in_specs=[HBM_SPEC] * len(arrs) + [SEM_SPEC, SEM_SPEC, ANY],
        out_specs=[HBM_SPEC] * len(arrs),
        out_shape=[pltpu.HBM(a.shape, a.dtype) for a in arrs],
        input_output_aliases={i: i for i in range(len(arrs))},
        compiler_params=pltpu.CompilerParams(has_side_effects=DATAFLOW),
    )(*arrs, send_sem, recv_sem, after)
    return list(outs[:n_s]), list(outs[n_s:])


def sibling_fill(lands, name):
    n_u = len(lands)

    def body(*refs):
        ins, outs = refs[:n_u], refs[n_u:2 * n_u]
        send_sems, recv_sems = refs[2 * n_u:]
        x, y, c = _coords()
        sends = []
        for u in range(n_u):
            h = ins[u].shape[1] // 2
            for j, (px, py) in enumerate(_other_chips(x, y)):
                part = (2 * px + py, pl.ds(c * h, h), slice(None))
                cp = pltpu.make_async_remote_copy(
                    src_ref=ins[u].at[part], dst_ref=outs[u].at[part], send_sem=send_sems.at[3 * u + j],
                    recv_sem=recv_sems.at[3 * u + j], device_id=(x, y, 1 - c), device_id_type=MESH)
                cp.start()
                sends.append(cp)
        for u in range(n_u):
            h = ins[u].shape[1] // 2
            for j, (px, py) in enumerate(_other_chips(x, y)):
                theirs = (2 * px + py, pl.ds((1 - c) * h, h), slice(None))
                pltpu.make_async_remote_copy(
                    src_ref=ins[u].at[theirs], dst_ref=outs[u].at[theirs], send_sem=send_sems.at[3 * u + j],
                    recv_sem=recv_sems.at[3 * u + j], device_id=(x, y, 1 - c), device_id_type=MESH).wait_recv()
        for cp in sends:
            cp.wait_send()

    return pl.pallas_call(
        body, name=name, in_specs=[ANY] * n_u, out_specs=[ANY] * n_u,
        out_shape=[jax.ShapeDtypeStruct(a.shape, a.dtype) for a in lands],
        input_output_aliases={i: i for i in range(n_u)},
        scratch_shapes=[pltpu.SemaphoreType.DMA((3 * n_u,)), pltpu.SemaphoreType.DMA((3 * n_u,))],
        compiler_params=_params(),
    )(*lands)


def _pack_rows(parts):
    flat = jnp.concatenate([p.reshape(-1).astype(F32) for p in parts])
    n = flat.shape[0]
    padded = -(-n // (8 * LANES)) * (8 * LANES)
    return jnp.pad(flat, (0, padded - n)).reshape(-1, LANES)


def _unpack_rows(packed, shapes):
    flat = packed.reshape(-1)
    out, off = [], 0
    for s in shapes:
        size = 1
        for d in s:
            size *= d
        out.append(flat[off:off + size].reshape(s))
        off += size
    return out


def _shard_last(full, s_me):
    n = full.shape[-1] // N_CHIP
    return lax.dynamic_slice_in_dim(full, s_me * n, n, axis=full.ndim - 1)


def _unshard_last(g):
    moved = jnp.moveaxis(g, 0, -2)
    return moved.reshape(moved.shape[:-2] + (moved.shape[-2] * moved.shape[-1],))


def kernel(x, c, w_ada, b_ada, ln_g, ln_b, ffn_gu, ffn_down, gmlp_w_in, gmlp_b_in, gmlp_ln_g, gmlp_ln_b, gmlp_w_s, gmlp_b_s, gmlp_w_out, w_ada_kv, b_ada_kv, w_kv, attn_w_q, attn_rel_bias, attn_w_o, loss_target, m_w_ada, m_b_ada, m_ln_g, m_ln_b, m_ffn_gu, m_ffn_down, m_gmlp_w_in, m_gmlp_b_in, m_gmlp_ln_g, m_gmlp_ln_b, m_gmlp_w_s, m_gmlp_b_s, m_gmlp_w_out, m_w_ada_kv, m_b_ada_kv, m_w_kv, m_attn_w_q, m_attn_rel_bias, m_attn_w_o, v_w_ada, v_b_ada, v_ln_g, v_ln_b, v_ffn_gu, v_ffn_down, v_gmlp_w_in, v_gmlp_b_in, v_gmlp_ln_g, v_gmlp_ln_b, v_gmlp_w_s, v_gmlp_b_s, v_gmlp_w_out, v_w_ada_kv, v_b_ada_kv, v_w_kv, v_attn_w_q, v_attn_rel_bias, v_attn_w_o):
    xi, yi, ci = _coords()
    s_me = 2 * xi + yi
    dev = 4 * xi + 2 * yi + ci

    x0 = x[0]
    tgt = loss_target[0]
    S, D = x0.shape
    L = w_ada.shape[0]
    NA = gmlp_w_in.shape[0]
    NB = attn_w_q.shape[0]
    NH = D // HEAD_DIM
    alpha = (2.0 * L) ** 0.25
    n_ada = w_ada.shape[2]
    n_kv = w_ada_kv.shape[1]

    stack_names = ["ffn_gu", "ffn_down", "gmlp_w_in", "gmlp_w_out", "w_kv", "attn_w_q", "attn_w_o"]
    stack_src = dict(ffn_gu=ffn_gu, ffn_down=ffn_down, gmlp_w_in=gmlp_w_in, gmlp_w_out=gmlp_w_out, w_kv=w_kv[None],
                     attn_w_q=attn_w_q, attn_w_o=attn_w_o)
    stacks = [stack_src[nm].reshape((-1,) + stack_src[nm].shape[-2:]) for nm in stack_names]
    units = [(si, k) for si, st in enumerate(stacks) for k in range(st.shape[0])]
    unit_of = {(stack_names[si], k): u for u, (si, k) in enumerate(units)}
    weight_groups = [[("ffn_gu", 0)], [("ffn_down", 0)]]
    for l in range(L):
        mixer = [("gmlp_w_in", l), ("gmlp_w_out", l)] if l < NA else [("attn_w_q", l - NA), ("attn_w_o", l - NA)]
        first, last = [("ffn_gu", 2 * l), ("ffn_down", 2 * l)], [("ffn_gu", 2 * l + 1), ("ffn_down", 2 * l + 1)]
        if l == 0:
            weight_groups += [mixer, last]
        elif l < NA:
            weight_groups += [first, mixer, last]
        else:
            weight_groups.append(([("w_kv", 0)] if l == NA else []) + first + mixer + last)
    weight_groups = [[unit_of[n] for n in names] for names in weight_groups]
    group_of = {u: g for g, ids in enumerate(weight_groups) for u in ids}

    c_all = all_gather8(jnp.broadcast_to(c, (8, D)), "ag_c").reshape(N_DEV, 8, D)[:, 0]
    b_ada_sh = lax.dynamic_slice_in_dim(b_ada, s_me * n_ada, n_ada, axis=1)
    b_kv_sh = lax.dynamic_slice_in_dim(b_ada_kv, s_me * n_kv, n_kv, axis=0)
    mod_part = ada_fwd(c_all, w_ada, b_ada_sh[:, None, :], "ada_fwd")
    mkv_part = ada_fwd(c_all, w_ada_kv[None], b_kv_sh[None, None, :], "ada_kv_fwd")
    part = jnp.concatenate([jnp.transpose(mod_part, (1, 0, 2)).reshape(N_DEV, L * n_ada), mkv_part[0]], axis=1)
    width = part.shape[1]
    pad_w = -(-width // LANES) * LANES - width
    all_part = all_gather8(jnp.pad(part, ((0, 0), (0, pad_w))), "ag_mod").reshape(N_DEV, N_DEV, width + pad_w)
    mine = lax.dynamic_index_in_dim(all_part[0::2], dev, axis=1, keepdims=False)
    mod = jnp.transpose(mine[:, :L * n_ada].reshape(N_CHIP, L, n_ada), (1, 0, 2)).reshape(L, N_MOD, D)
    mkv = mine[:, L * n_ada:width].reshape(2, D)

    def mrow(l, k):
        return mod[l, k][None, :]

    small_shapes = [ln_g.shape, ln_b.shape, gmlp_b_in.shape, gmlp_ln_g.shape, gmlp_ln_b.shape, attn_rel_bias.shape]
    small_pack = _pack_rows([ln_g, ln_b, gmlp_b_in, gmlp_ln_g, gmlp_ln_b, attn_rel_bias])
    small_all = all_gather8(small_pack, "ag_small_params").reshape((N_DEV,) + small_pack.shape)[0::2]
    sm = [_unpack_rows(small_all[s], small_shapes) for s in range(N_CHIP)]
    ln_g_f, ln_b_f, b_in_f, gln_g_f, gln_b_f, rel_f = [
        _unshard_last(jnp.stack([sm[s][i] for s in range(N_CHIP)])) for i in range(len(small_shapes))]

    def landing(u):
        si, k = units[u]
        shard = stacks[si][k]
        if group_of[u] < 2:
            shard = lax.optimization_barrier(shard)
        shard = shard.astype(BF16)
        return lax.dynamic_update_slice(lax.empty((N_CHIP,) + shard.shape, BF16), shard[None], (s_me, 0, 0))

    gathers_done = jnp.concatenate([mod.reshape(-1)[:LANES], small_all.reshape(-1)[:LANES]])
    w_sems, lands_t = {}, {}
    for part, groups, name in ((0, weight_groups[:2], "weight_send_start_first"),
                               (1, weight_groups[2:], "weight_send_start_rest")):
        ids = [u for grp in groups for u in grp]
        local = [[ids.index(u) for u in grp] for grp in groups]
        sems, _, lands, gathers_done = exchange_start([], [landing(u) for u in ids], [units[u] for u in ids], local,
                                                      _weight_desc, _chip_peers, gathers_done, name)
        for i, grp in enumerate(groups):
            w_sems[2 * part + i] = (sems[2 * i], sems[2 * i + 1])
        lands_t.update(zip(ids, lands))
    wg = {}
    latest = [mod]

    def W(nm, k):
        u = unit_of[(nm, k)]
        if u not in wg:
            g = group_of[u]
            ids = weight_groups[g]
            _, got = exchange_wait([], [lands_t[v] for v in ids], [units[v] for v in ids], w_sems[g][0],
                                   w_sems[g][1], _weight_desc, _chip_peers, latest[0], "weight_send_wait_%d" % g)
            wg.update(zip(ids, sibling_fill(got, "weight_sibling_fill")))
        return wg[u]

    def Wrows(nm, k):
        w4 = W(nm, k)
        return w4.reshape(w4.shape[0] * w4.shape[1], w4.shape[2])

    bst = [jnp.transpose(gmlp_b_s[j]) for j in range(NA)]
    biases = {}

    def make_bias(j, dep):
        rel, _ = lax.optimization_barrier((rel_f[j], dep))
        biases[j] = jnp.transpose(bias_expand(rel, "bias_expand"), (1, 0, 2))
        return biases[j]

    saved = []
    xc = x0
    kpad = vpad = xkv = None
    for l in range(L):
        if l == 1 and NB > 1:
            latest[0] = make_bias(1, xc)
        if l == NA:
            xkv = xc
            kv, hkv = mod_matmul(xc, mkv[1][None], mkv[0][None], W("w_kv", 0), None, BF16, "kv_proj")
            kpad = jnp.pad(kv[:, :D], ((LEFT_PAD, 0), (0, 0)))
            vpad = jnp.pad(kv[:, D:], ((LEFT_PAD, 0), (0, 0)))
        sv = {}
        for i in (0, 2):
            k = 2 * l + i // 2
            gu, hv = mod_matmul(xc, mrow(l, 3 * i + 1), mrow(l, 3 * i), W("ffn_gu", k), None, BF16, "ffn_up")
            latest[0] = hv
            gw = 0.5 * (1.0 + mrow(l, 3 * i + 2))
            xn, xh, rs, yv, av = matmul_res_ln(gu, Wrows("ffn_down", k), xc, gw, ln_g_f[l, i][None],
                                               ln_b_f[l, i][None], alpha, True, "ffn_down")
            sv[i] = dict(x=xc, h=hv, gu=gu, a=av, xh=xh, rs=rs, y=yv, gw=gw)
            xc = latest[0] = xn
            if i == 0:
                if l == 0 and NB > 0:
                    latest[0] = make_bias(0, xc)
                gw = 1.0 + mrow(l, 5)
                if l < NA:
                    pre, hv = mod_matmul(xc, mrow(l, 4), mrow(l, 3), W("gmlp_w_in", l), b_in_f[l][None], F32,
                                         "gmlp_in")
                    qv = sgu_fwd(pre, gln_g_f[l][None], gln_b_f[l][None], gmlp_w_s[l], bst[l], "sgu_fwd")
                    xn, xh, rs, yv = matmul_res_ln(qv, Wrows("gmlp_w_out", l), xc, gw, ln_g_f[l, 1][None],
                                                   ln_b_f[l, 1][None], alpha, False, "gmlp_out")
                    sv[1] = dict(x=xc, h=hv, pre=pre, a=qv, xh=xh, rs=rs, y=yv, gw=gw)
                else:
                    j = l - NA
                    if j not in biases:
                        make_bias(j, xc)
                    qh, hv = mod_matmul(xc, mrow(l, 4), mrow(l, 3), Wrows("attn_w_q", j)[None], None, BF16, "attn_q")
                    ov = attn_fwd(qh, kpad, vpad, biases[j], "attn_fwd")
                    xn, xh, rs, yv = matmul_res_ln(ov, Wrows("attn_w_o", j), xc, gw, ln_g_f[l, 1][None],
                                                   ln_b_f[l, 1][None], alpha, False, "attn_out")
                    sv[1] = dict(x=xc, h=hv, q=qh, a=ov, xh=xh, rs=rs, y=yv, gw=gw)
                xc = latest[0] = xn
        saved.append(sv)

    dx, lacc = loss_grad(xc, tgt, "loss_grad")
    loss = lax.psum((0.5 / D) * jnp.sum(lacc[0]), ("x", "y", "c"))

    gpair = [None] * len(units)
    col_split = {u for u, (si, _) in enumerate(units) if stack_names[si] != "ffn_gu"}
    dmod = [[None] * N_MOD for _ in range(L)]
    d_ln_g = [[None] * 3 for _ in range(L)]
    d_ln_b = [[None] * 3 for _ in range(L)]
    d_b_in, d_gln_g, d_gln_b, d_ws, d_bs, d_rel = ([None] * NA, [None] * NA, [None] * NA, [None] * NA, [None] * NA,
                                                  [None] * NB)
    dk = jnp.zeros((S, D), F32)
    dv = jnp.zeros((S, D), F32)
    dmkv = None

    made = []

    core_idx = ci.astype(jnp.int32).reshape(1)
    chip_idx = s_me.astype(jnp.int32).reshape(1)

    def put(nm, k, a, b, name):
        u = unit_of[(nm, k)]
        rows, cols = stacks[units[u][0]].shape[1:]
        if nm == "ffn_gu":
            g = wgrad_pair(b, a, N_CHIP, cols, rows // 2, lambda j, p: j, lambda j, p: p, core_idx, name)
        elif nm in ("gmlp_w_in", "w_kv"):
            g = wgrad_pair(a, b, N_CHIP, rows, cols // 2, lambda j, p: 0, lambda j, p: 2 * j + p, core_idx, name)
        else:
            g = wgrad_pair(a, b, 1, N_CHIP * rows, cols // 2, lambda j, p: 0, lambda j, p: p, core_idx, name)
        gpair[u] = g.reshape((N_CHIP, -1, g.shape[-1]))
        made.append(u)

    own_half, sib_half = {}, {}
    n_started = [0]

    def start_grad_exchange(ids, after):
        psums = [gpair[u] for u in ids]
        n = len(ids)
        tag = n_started[0]
        n_started[0] += 1
        sems, ps_t, q_t, token = exchange_start(psums, [lax.empty(p.shape, p.dtype) for p in psums],
                                                [(i, 0) for i in range(n)], [list(range(n))], _grad_desc, _chip_peers,
                                                after, "grad_send_start_%d" % tag)
        return dict(ids=ids, tag=tag, sems=sems, ps=ps_t, q=q_t), token

    def finish_grad_exchange(pend, after):
        n = len(pend["ids"])
        ps_t, q = exchange_wait(pend["ps"], pend["q"], [(i, 0) for i in range(n)], pend["sems"][0], pend["sems"][1],
                                _grad_desc, _chip_peers, after, "grad_send_wait_%d" % pend["tag"])
        halves = [chip_sum(ps_t[i], q[i], chip_idx, u not in col_split, "grad_chip_sum")
                  for i, u in enumerate(pend["ids"])]
        sems, h_t, land_t, token = exchange_start(halves, [lax.empty(h.shape, h.dtype) for h in halves],
                                              [(i, 0) for i in range(n)], [list(range(n))], _whole_desc,
                                              _sibling_peer, halves[0], "half_send_start_%d" % pend["tag"])
        swaps.append(dict(ids=pend["ids"], tag=pend["tag"], sems=sems, h=h_t, land=land_t))
        return token

    def collect_halves(after):
        for sw in swaps:
            n = len(sw["ids"])
            h, land = exchange_wait(sw["h"], sw["land"], [(i, 0) for i in range(n)], sw["sems"][0], sw["sems"][1],
                                    _whole_desc, _sibling_peer, after, "half_send_wait_%d" % sw["tag"])
            for u, mine, theirs in zip(sw["ids"], h, land):
                own_half[u], sib_half[u] = mine, theirs
        swaps.clear()

    swaps = []
    pending = None
    started_before = jnp.zeros((8, LANES), F32)

    def ln_inputs(l, i):
        t = saved[l][i]
        return (t["xh"], t["rs"], ln_g_f[l, i][None], t["y"], t["gw"], 1.0 if i == 1 else 0.5)

    def record_ln(l, i, acc, row0):
        d_ln_g[l][i], d_ln_b[l][i], dmod[l][3 * i + 2] = acc[row0], acc[row0 + 1], acc[row0 + 2]

    ln_done = None
    for l in reversed(range(L)):
        if l == NA - 1:
            dkv = jnp.concatenate([dk, dv], axis=1)
            put("w_kv", 0, hkv, dkv, "kv_wgrad")
            pdxa, pdy, acc = dgrad_mod(dkv, W("w_kv", 0), dx, xkv, mkv[1][None], ln_inputs(l, 2), alpha, "kv_dgrad")
            dmkv = jnp.stack([acc[1], acc[0]])
            record_ln(l, 2, acc, 2)
            ln_done = (pdxa, pdy)
        sv = saved[l]
        for i in (2, 1, 0):
            t = sv[i]
            if ln_done is None:
                dxa, dy, acc1 = ln_res_bwd(dx, *ln_inputs(l, i), alpha, "ln_res_bwd")
                record_ln(l, i, acc1, 0)
            else:
                dxa, dy = ln_done
                ln_done = None
            before = (l, i - 1) if i > 0 else ((l - 1, 2) if l > 0 and l != NA else None)
            prev = ln_inputs(*before) if before is not None else None
            scl = mrow(l, 3 * i + 1)
            if i != 1:
                k = 2 * l + i // 2
                F = t["gu"].shape[1] // 2
                dgu = ffn_act_bwd(dy, Wrows("ffn_down", k), t["gu"], started_before, "ffn_act_bwd")
                put("ffn_down", k, t["a"], dy, "ffn_down_wgrad")
                put("ffn_gu", k, t["h"], dgu, "ffn_up_wgrad")
                res = dgrad_mod(dgu, W("ffn_gu", k), dxa, t["x"], scl, prev, alpha, "ffn_up_dgrad")
            elif l < NA:
                dq = matmul_nt(dy, Wrows("gmlp_w_out", l), started_before, "gmlp_out_dgrad")
                put("gmlp_w_out", l, t["a"], dy, "gmlp_out_wgrad")
                dpre, dws_l, dss, dgl, dbin = sgu_bwd(dq, t["pre"], gln_g_f[l][None], gln_b_f[l][None], gmlp_w_s[l],
                                                      bst[l], "sgu_bwd")
                d_ws[l] = dws_l
                d_bs[l] = jnp.transpose(group_lane_sum(dss, "sgu_bias_grad")[:, :GMLP_GROUPS])
                d_gln_g[l], d_gln_b[l], d_b_in[l] = dgl[0], dgl[1], dbin[0]
                put("gmlp_w_in", l, t["h"], dpre, "gmlp_in_wgrad")
                res = dgrad_mod(dpre, W("gmlp_w_in", l), dxa, t["x"], scl, prev, alpha, "gmlp_in_dgrad")
            else:
                j = l - NA
                do = matmul_nt(dy, Wrows("attn_w_o", j), started_before, "attn_out_dgrad")
                put("attn_w_o", j, t["a"], dy, "attn_out_wgrad")
                dqh, dk, dv, dbias = attn_bwd(t["q"], do, kpad, vpad, biases[j], dk, dv, "attn_bwd")
                d_rel[j] = bias_grad(jnp.transpose(dbias, (1, 0, 2)), "bias_grad")
                put("attn_w_q", j, t["h"], dqh, "attn_q_wgrad")
                res = dgrad_mod(dqh, Wrows("attn_w_q", j)[None], dxa, t["x"], scl, prev, alpha, "attn_q_dgrad")
            acc2 = res[-1]
            dmod[l][3 * i + 1], dmod[l][3 * i] = acc2[0], acc2[1]
            if before is None:
                dx = res[0]
            else:
                record_ln(*before, acc2, 2)
                ln_done = (res[0], res[1])
            if (i == 0 and l > 0) or (i == 1 and l == 0):
                started, started_before = start_grad_exchange(list(made), acc2)
                made.clear()
                if pending is not None:
                    started_before = started_before + finish_grad_exchange(pending, acc2)
                pending = started
    grad_x = dx[None]

    dvec = _pack_rows([jnp.stack([jnp.stack(r) for r in dmod]), dmkv])
    dvec = lax.optimization_barrier((dvec, [gpair[u] for u in made]))[0]
    n_dvec = L * N_MOD * D + 2 * D
    dall = all_gather8(dvec, "ag_dmod").reshape(N_DEV, -1, LANES)
    db_all = sum_leading(dall, "ada_bias_grad").reshape(-1)[:n_dvec]
    g_b_ada = db_all[:L * N_MOD * D].reshape(L, N_MOD * D)
    g_b_ada_kv = db_all[L * N_MOD * D:]
    dall2 = dall.reshape(N_DEV, -1)[:, :n_dvec]
    dmod_all = dall2[:, :L * N_MOD * D].reshape(N_DEV, L, N_MOD * D)
    dmod_sh = jnp.transpose(lax.dynamic_slice_in_dim(dmod_all, s_me * n_ada, n_ada, axis=2), (1, 0, 2))
    dmkv_sh = lax.dynamic_slice_in_dim(dall2[:, L * N_MOD * D:], s_me * n_kv, n_kv, axis=1)[None]
    c_all_t = jnp.transpose(c_all)
    g_w_ada = ada_wgrad(c_all_t, dmod_sh, "ada_wgrad")
    g_w_ada_kv = ada_wgrad(c_all_t, dmkv_sh, "ada_kv_wgrad")[0]

    small_g = [jnp.stack([jnp.stack(r) for r in d_ln_g]), jnp.stack([jnp.stack(r) for r in d_ln_b]),
               jnp.stack(d_b_in), jnp.stack(d_gln_g), jnp.stack(d_gln_b), jnp.stack(d_rel), jnp.stack(d_ws),
               jnp.stack(d_bs)]
    sg_shapes = [a.shape for a in small_g]
    sg_pack = _pack_rows(small_g)
    sg_all = all_gather8(sg_pack, "ag_small_grads").reshape(N_DEV, -1, LANES)
    sg_sum = _unpack_rows(sum_leading(sg_all, "small_grad_sum"), sg_shapes)
    g_ln_g, g_ln_b, g_b_in, g_gln_g, g_gln_b, g_rel = [_shard_last(a, s_me) for a in sg_sum[:6]]
    g_ws, g_bs = sg_sum[6], sg_sum[7]

    last, _ = start_grad_exchange(list(made), sg_all)

    grads = dict(w_ada=g_w_ada, b_ada=g_b_ada, ln_g=g_ln_g, ln_b=g_ln_b, gmlp_b_in=g_b_in, gmlp_ln_g=g_gln_g,
                 gmlp_ln_b=g_gln_b, gmlp_w_s=g_ws, gmlp_b_s=g_bs, w_ada_kv=g_w_ada_kv, b_ada_kv=g_b_ada_kv,
                 attn_rel_bias=g_rel)
    weights = dict(w_ada=w_ada, b_ada=b_ada, ln_g=ln_g, ln_b=ln_b, ffn_gu=ffn_gu, ffn_down=ffn_down,
                   gmlp_w_in=gmlp_w_in, gmlp_b_in=gmlp_b_in, gmlp_ln_g=gmlp_ln_g, gmlp_ln_b=gmlp_ln_b,
                   gmlp_w_s=gmlp_w_s, gmlp_b_s=gmlp_b_s, gmlp_w_out=gmlp_w_out, w_ada_kv=w_ada_kv,
                   b_ada_kv=b_ada_kv, w_kv=w_kv, attn_w_q=attn_w_q, attn_rel_bias=attn_rel_bias, attn_w_o=attn_w_o)
    ms = dict(w_ada=m_w_ada, b_ada=m_b_ada, ln_g=m_ln_g, ln_b=m_ln_b, ffn_gu=m_ffn_gu, ffn_down=m_ffn_down,
              gmlp_w_in=m_gmlp_w_in, gmlp_b_in=m_gmlp_b_in, gmlp_ln_g=m_gmlp_ln_g, gmlp_ln_b=m_gmlp_ln_b,
              gmlp_w_s=m_gmlp_w_s, gmlp_b_s=m_gmlp_b_s, gmlp_w_out=m_gmlp_w_out, w_ada_kv=m_w_ada_kv,
              b_ada_kv=m_b_ada_kv, w_kv=m_w_kv, attn_w_q=m_attn_w_q, attn_rel_bias=m_attn_rel_bias,
              attn_w_o=m_attn_w_o)
    vs = dict(w_ada=v_w_ada, b_ada=v_b_ada, ln_g=v_ln_g, ln_b=v_ln_b, ffn_gu=v_ffn_gu, ffn_down=v_ffn_down,
              gmlp_w_in=v_gmlp_w_in, gmlp_b_in=v_gmlp_b_in, gmlp_ln_g=v_gmlp_ln_g, gmlp_ln_b=v_gmlp_ln_b,
              gmlp_w_s=v_gmlp_w_s, gmlp_b_s=v_gmlp_b_s, gmlp_w_out=v_gmlp_w_out, w_ada_kv=v_w_ada_kv,
              b_ada_kv=v_b_ada_kv, w_kv=v_w_kv, attn_w_q=v_attn_w_q, attn_rel_bias=v_attn_rel_bias,
              attn_w_o=v_attn_w_o)
    order = ["w_ada", "b_ada", "ln_g", "ln_b", "ffn_gu", "ffn_down", "gmlp_w_in", "gmlp_b_in", "gmlp_ln_g",
             "gmlp_ln_b", "gmlp_w_s", "gmlp_b_s", "gmlp_w_out", "w_ada_kv", "b_ada_kv", "w_kv", "attn_w_q",
             "attn_rel_bias", "attn_w_o"]
    big_names = ["w_ada", "w_ada_kv"] + stack_names
    small_names = [nm for nm in order if nm not in big_names]
    delta, new_m, new_v = {}, {}, {}

    def adamw_big(nm):
        shp = weights[nm].shape
        two_d = (-1, shp[-1])
        d, a, b = adamw(weights[nm].reshape(two_d), grads[nm].reshape(two_d), ms[nm].reshape(two_d),
                        vs[nm].reshape(two_d), "adamw")
        delta[nm], new_m[nm], new_v[nm] = d.reshape(shp), a.reshape(shp), b.reshape(shp)

    adamw_big("w_ada")
    adamw_big("w_ada_kv")
    shapes = [weights[nm].shape for nm in small_names]
    d, a, b = adamw(_pack_rows([weights[nm] for nm in small_names]), _pack_rows([grads[nm] for nm in small_names]),
                    _pack_rows([ms[nm] for nm in small_names]), _pack_rows([vs[nm] for nm in small_names]),
                    "adamw_small")
    for nm, dd, aa, bb in zip(small_names, _unpack_rows(d, shapes), _unpack_rows(a, shapes), _unpack_rows(b, shapes)):
        delta[nm], new_m[nm], new_v[nm] = dd, aa, bb

    def full_grad(u):
        lo = jnp.where(ci == 0, own_half[u], sib_half[u])
        hi = jnp.where(ci == 0, sib_half[u], own_half[u])
        return jnp.concatenate([lo, hi], axis=1 if u in col_split else 0)

    def adamw_stack(nm):
        si = stack_names.index(nm)
        g = jnp.stack([full_grad(unit_of[(nm, k)]) for k in range(stacks[si].shape[0])])
        grads[nm] = g.reshape(weights[nm].shape)
        adamw_big(nm)

    late = [stack_names[units[u][0]] for u in last["ids"]]
    early = [nm for nm in stack_names if nm not in late]
    finish_grad_exchange(pending, delta["w_ada"])
    collect_halves(delta["w_ada"])
    for nm in early:
        adamw_stack(nm)
    finish_grad_exchange(last, delta[early[-1]])
    collect_halves(delta[early[-1]])
    for nm in stack_names:
        if nm in late:
            adamw_stack(nm)

    return (loss, grad_x, *[grads[nm] for nm in order], *[delta[nm] for nm in order],
            *[new_m[nm] for nm in order], *[new_v[nm] for nm in order])
```

```python
import functools

import jax
import jax.numpy as jnp
from jax import lax
from jax.experimental import pallas as pl
from jax.experimental.pallas import tpu as pltpu

F32 = jnp.float32
BF16 = jnp.bfloat16
MESH = pl.DeviceIdType.MESH
HIGHEST = lax.Precision.HIGHEST

CHUNK = 64
GMLP_WINDOW = 128
GMLP_GROUPS = 8
HEAD_DIM = 64
LEFT_CHUNKS = 8
BAND = (LEFT_CHUNKS + 1) * CHUNK
LEFT_PAD = LEFT_CHUNKS * CHUNK
MAX_REL = 4 * CHUNK
N_REL = (CHUNK - 1) + MAX_REL + 1
LN_EPS = 1e-5
N_MOD = 9
N_DEV = 8
N_CHIP = 4

ADAM_LR = 0.001
ADAM_B1 = 0.9
ADAM_B2 = 0.999
ADAM_EPS = 1e-08
ADAM_WD = 0.01
ADAM_STEP = 10

LANES = 128
ROW_TILE = 256
MATMUL_ROW_TILE = 512
WGRAD_ROWS = 1024
ATTN_CHUNKS_PER_STEP = 4
VMEM_LIMIT_MB = 56

NT = (((1,), (1,)), ((), ()))
TN = (((0,), (0,)), ((), ()))

ANY = pl.BlockSpec(memory_space=pl.ANY)
VMEM_SPEC = pl.BlockSpec(memory_space=pltpu.VMEM)


def _params(semantics=None):
    kw = dict(vmem_limit_bytes=VMEM_LIMIT_MB * 1024 * 1024)
    if semantics is not None:
        kw["dimension_semantics"] = semantics
    return pltpu.CompilerParams(**kw)


def _sigmoid(v):
    return 0.5 * (1.0 + jnp.tanh(0.5 * v))


def _gelu(v):
    return 0.5 * v * (1.0 + lax.erf(v * (2.0 ** -0.5)))


def _gelu_grad(v):
    return 0.5 * (1.0 + lax.erf(v * (2.0 ** -0.5))) + v * jnp.exp(-0.5 * v * v) * ((2.0 * jnp.pi) ** -0.5)


def _row(m):
    return lambda i: (i, 0)


def _fixed2(i):
    return (0, 0)


def _fixed3(i):
    return (0, 0, 0)


def _resident(shape):
    return pl.BlockSpec(shape, _fixed2 if len(shape) == 2 else _fixed3, pipeline_mode=pl.Buffered(1))


def mod_matmul(x, scl, shift, w, bias, out_dtype, name):
    S, D = x.shape
    NS, _, n = w.shape
    tm = min(MATMUL_ROW_TILE, S)
    has_bias = bias is not None

    def body(*refs):
        if has_bias:
            x_ref, scl_ref, sh_ref, w_ref, b_ref, o_ref, h_ref = refs
        else:
            x_ref, scl_ref, sh_ref, w_ref, o_ref, h_ref = refs
        h = (x_ref[...] * (1.0 + scl_ref[...]) + sh_ref[...]).astype(BF16)
        h_ref[...] = h
        for s in range(NS):
            acc = jnp.dot(h, w_ref[s], preferred_element_type=F32)
            if has_bias:
                acc = acc + b_ref[:, s * n:(s + 1) * n]
            o_ref[:, s * n:(s + 1) * n] = acc.astype(out_dtype)

    in_specs = [pl.BlockSpec((tm, D), _row(0)), pl.BlockSpec((1, D), _fixed2), pl.BlockSpec((1, D), _fixed2),
                _resident((NS, D, n))]
    args = [x, scl, shift, w]
    if has_bias:
        in_specs.append(pl.BlockSpec((1, NS * n), _fixed2))
        args.append(bias)
    return pl.pallas_call(
        body, name=name, grid=(S // tm,), in_specs=in_specs,
        out_specs=[pl.BlockSpec((tm, NS * n), _row(0)), pl.BlockSpec((tm, D), _row(0))],
        out_shape=[jax.ShapeDtypeStruct((S, NS * n), out_dtype), jax.ShapeDtypeStruct((S, D), BF16)],
        compiler_params=_params(("parallel",)),
    )(*args)


def matmul_res_ln(a, w, x, gw, lg, lb, alpha, swiglu, name):
    S, D = x.shape
    K = w.shape[0]
    tm = min(ROW_TILE, S)
    ka = a.shape[1]

    def body(a_ref, w_ref, x_ref, gw_ref, lg_ref, lb_ref, xn_ref, xh_ref, rs_ref, y_ref, *act_ref):
        if swiglu:
            g = a_ref[:, :K].astype(F32)
            u = a_ref[:, K:].astype(F32)
            act = (g * _sigmoid(g) * u).astype(BF16)
            act_ref[0][...] = act
        else:
            act = a_ref[...].astype(BF16)
        y = jnp.dot(act, w_ref[...], preferred_element_type=F32)
        z = alpha * x_ref[...] + gw_ref[...] * y
        mu = jnp.mean(z, axis=-1, keepdims=True)
        zc = z - mu
        var = jnp.mean(zc * zc, axis=-1, keepdims=True)
        rstd = lax.rsqrt(var + LN_EPS)
        xhat = zc * rstd
        xn_ref[...] = xhat * lg_ref[...] + lb_ref[...]
        xh_ref[...] = xhat
        rs_ref[...] = rstd
        y_ref[...] = y.astype(BF16)

    vec = pl.BlockSpec((1, D), _fixed2)
    out_specs = [pl.BlockSpec((tm, D), _row(0)), pl.BlockSpec((tm, D), _row(0)), pl.BlockSpec((tm, 1), _row(0)),
                 pl.BlockSpec((tm, D), _row(0))]
    out_shape = [jax.ShapeDtypeStruct((S, D), F32), jax.ShapeDtypeStruct((S, D), F32),
                 jax.ShapeDtypeStruct((S, 1), F32), jax.ShapeDtypeStruct((S, D), BF16)]
    if swiglu:
        out_specs.append(pl.BlockSpec((tm, K), _row(0)))
        out_shape.append(jax.ShapeDtypeStruct((S, K), BF16))
    return pl.pallas_call(
        body, name=name, grid=(S // tm,),
        in_specs=[pl.BlockSpec((tm, ka), _row(0)), _resident((K, D)), pl.BlockSpec((tm, D), _row(0)),
                  vec, vec, vec],
        out_specs=out_specs, out_shape=out_shape,
        compiler_params=_params(("parallel",)),
    )(a, w, x, gw, lg, lb)


def _ln_res_bwd_tile(d, xh_ref, rs_ref, lg_ref, y_ref, gw_ref, wres, alpha, dxa_ref, dy_ref, acc_ref, row0):
    xh = xh_ref[...]
    dxh = d * lg_ref[...]
    m1 = jnp.mean(dxh, axis=-1, keepdims=True)
    m2 = jnp.mean(dxh * xh, axis=-1, keepdims=True)
    dz = rs_ref[...] * (dxh - m1 - xh * m2)
    dxa_ref[...] = alpha * dz
    dy_ref[...] = (gw_ref[...] * dz).astype(BF16)
    acc_ref[row0:row0 + 1, :] += jnp.sum(d * xh, axis=0, keepdims=True)
    acc_ref[row0 + 1:row0 + 2, :] += jnp.sum(d, axis=0, keepdims=True)
    acc_ref[row0 + 2:row0 + 3, :] += jnp.sum((wres * dz) * y_ref[...].astype(F32), axis=0, keepdims=True)


def ln_res_bwd(dxn, xhat, rstd, lg, y, gw, wres, alpha, name):
    S, D = dxn.shape
    tm = min(MATMUL_ROW_TILE, S)

    def body(dxn_ref, xh_ref, rs_ref, lg_ref, y_ref, gw_ref, dxa_ref, dy_ref, acc_ref):
        @pl.when(pl.program_id(0) == 0)
        def _():
            acc_ref[...] = jnp.zeros_like(acc_ref)

        _ln_res_bwd_tile(dxn_ref[...], xh_ref, rs_ref, lg_ref, y_ref, gw_ref, wres, alpha, dxa_ref, dy_ref, acc_ref, 0)

    vec = pl.BlockSpec((1, D), _fixed2)
    tile = pl.BlockSpec((tm, D), _row(0))
    return pl.pallas_call(
        body, name=name, grid=(S // tm,),
        in_specs=[tile, tile, pl.BlockSpec((tm, 1), _row(0)), vec, tile, vec],
        out_specs=[tile, tile, pl.BlockSpec((8, D), _fixed2)],
        out_shape=[jax.ShapeDtypeStruct((S, D), F32), jax.ShapeDtypeStruct((S, D), BF16),
                   jax.ShapeDtypeStruct((8, D), F32)],
        compiler_params=_params(("arbitrary",)),
    )(dxn, xhat, rstd, lg, y, gw)


def ffn_act_bwd(dy, wd, gu, after, name):
    S, D = dy.shape
    K = wd.shape[0]
    tm = min(ROW_TILE, S)

    def body(dy_ref, wd_ref, gu_ref, after_ref, o_ref):
        da = lax.dot_general(dy_ref[...], wd_ref[...], NT, preferred_element_type=F32).astype(BF16)
        g = gu_ref[:, :K]
        u = gu_ref[:, K:]
        sg = _sigmoid(g)
        o_ref[:, :K] = da * u * (sg * (1.0 + g * (1.0 - sg)))
        o_ref[:, K:] = da * (g * sg)

    return pl.pallas_call(
        body, name=name, grid=(S // tm,),
        in_specs=[pl.BlockSpec((tm, D), _row(0)), _resident((K, D)), pl.BlockSpec((tm, 2 * K), _row(0)), ANY],
        out_specs=pl.BlockSpec((tm, 2 * K), _row(0)),
        out_shape=jax.ShapeDtypeStruct((S, 2 * K), BF16),
        compiler_params=_params(("parallel",)),
    )(dy, wd, gu, after)


def matmul_nt(a, w, after, name):
    S, D = a.shape
    K = w.shape[0]
    tm = min(MATMUL_ROW_TILE, S)

    def body(a_ref, w_ref, after_ref, o_ref):
        o_ref[...] = lax.dot_general(a_ref[...], w_ref[...], NT, preferred_element_type=F32).astype(BF16)

    return pl.pallas_call(
        body, name=name, grid=(S // tm,),
        in_specs=[pl.BlockSpec((tm, D), _row(0)), _resident((K, D)), ANY],
        out_specs=pl.BlockSpec((tm, K), _row(0)),
        out_shape=jax.ShapeDtypeStruct((S, K), BF16),
        compiler_params=_params(("parallel",)),
    )(a, w, after)


def dgrad_mod(dpre, w, dxa, xin, scl, prev, alpha, name):
    S, D = xin.shape
    NS, _, n = w.shape
    tm = min(ROW_TILE, S)
    wres = prev[5] if prev is not None else None

    def body(*refs):
        dp_ref, w_ref, dxa_ref, xin_ref, scl_ref = refs[:5]
        acc_ref = refs[-1]

        @pl.when(pl.program_id(0) == 0)
        def _():
            acc_ref[...] = jnp.zeros_like(acc_ref)

        dh = jnp.zeros((tm, D), F32)
        for s in range(NS):
            dh = dh + lax.dot_general(dp_ref[:, s * n:(s + 1) * n].astype(BF16), w_ref[s], NT,
                                      preferred_element_type=F32)
        dx = dxa_ref[...] + dh * (1.0 + scl_ref[...])
        acc_ref[0:1, :] += jnp.sum(dh * xin_ref[...], axis=0, keepdims=True)
        acc_ref[1:2, :] += jnp.sum(dh, axis=0, keepdims=True)
        if prev is None:
            refs[5][...] = dx
        else:
            xh_ref, rs_ref, lg_ref, y_ref, gw_ref, pdxa_ref, pdy_ref = refs[5:12]
            _ln_res_bwd_tile(dx, xh_ref, rs_ref, lg_ref, y_ref, gw_ref, wres, alpha, pdxa_ref, pdy_ref, acc_ref, 2)

    tile = pl.BlockSpec((tm, D), _row(0))
    vec = pl.BlockSpec((1, D), _fixed2)
    in_specs = [pl.BlockSpec((tm, NS * n), _row(0)), _resident((NS, D, n)), tile, tile, vec]
    args = [dpre, w, dxa, xin, scl]
    if prev is None:
        out_specs = [tile]
        out_shape = [jax.ShapeDtypeStruct((S, D), F32)]
    else:
        in_specs += [tile, pl.BlockSpec((tm, 1), _row(0)), vec, tile, vec]
        args += list(prev[:5])
        out_specs = [tile, tile]
        out_shape = [jax.ShapeDtypeStruct((S, D), F32), jax.ShapeDtypeStruct((S, D), BF16)]
    return pl.pallas_call(
        body, name=name, grid=(S // tm,), in_specs=in_specs,
        out_specs=out_specs + [pl.BlockSpec((8, D), _fixed2)],
        out_shape=out_shape + [jax.ShapeDtypeStruct((8, D), F32)],
        compiler_params=_params(("arbitrary",)),
    )(*args)


PAIR_COLLECTIVE_ID = 1


def wgrad_pair(a, b, J, kb, nb, a_block, b_block, half_idx, name):
    S = b.shape[0]
    ts = min(WGRAD_ROWS, S)
    nsteps = S // ts

    def body(h_ref, a_ref, b_ref, o_ref, acc_ref, send_buf, recv_buf, send_sems, recv_sems):
        jj, si = pl.program_id(0), pl.program_id(1)
        x, y, c = _coords()
        j = lax.rem(jj, J)
        last = si == nsteps - 1

        def copy(blk):
            return pltpu.make_async_remote_copy(
                src_ref=send_buf.at[blk], dst_ref=recv_buf.at[blk], send_sem=send_sems.at[blk],
                recv_sem=recv_sems.at[blk], device_id=(x, y, 1 - c), device_id_type=MESH)

        @pl.when(jnp.logical_and(jj == 0, si == 0))
        def _():
            barrier = pltpu.get_barrier_semaphore()
            pl.semaphore_signal(barrier, inc=1, device_id=(x, y, 1 - c), device_id_type=MESH)
            pl.semaphore_wait(barrier, 1)

        @pl.when(si == 0)
        def _():
            acc_ref[...] = jnp.zeros_like(acc_ref)

        acc_ref[...] += lax.dot_general(a_ref[...], b_ref[...].astype(BF16), TN, preferred_element_type=F32)

        @pl.when(jnp.logical_and(last, jj < J))
        def _():
            send_buf[j] = acc_ref[...].astype(BF16)
            copy(j).start()

        @pl.when(jnp.logical_and(last, jj >= J))
        def _():
            copy(j).wait_recv()
            o_ref[...] = (acc_ref[...] + recv_buf[j].astype(F32)).astype(BF16)

        @pl.when(jnp.logical_and(last, jj == 2 * J - 1))
        def _():
            for blk in range(J):
                copy(blk).wait_send()

    def half(jj, h):
        return jnp.where(jj < J, 1 - h[0], h[0])

    return pl.pallas_call(
        body, name=name,
        grid_spec=pltpu.PrefetchScalarGridSpec(
            num_scalar_prefetch=1, grid=(2 * J, nsteps),
            in_specs=[pl.BlockSpec((ts, kb), lambda jj, s, h: (s, a_block(lax.rem(jj, J), half(jj, h)))),
                      pl.BlockSpec((ts, nb), lambda jj, s, h: (s, b_block(lax.rem(jj, J), half(jj, h))))],
            out_specs=pl.BlockSpec((None, kb, nb), lambda jj, s, h: (jnp.maximum(jj - J, 0), 0, 0)),
            scratch_shapes=[pltpu.VMEM((kb, nb), F32), pltpu.VMEM((J, kb, nb), BF16), pltpu.VMEM((J, kb, nb), BF16),
                            pltpu.SemaphoreType.DMA((J,)), pltpu.SemaphoreType.DMA((J,))]),
        out_shape=jax.ShapeDtypeStruct((J, kb, nb), BF16),
        compiler_params=pltpu.CompilerParams(
            vmem_limit_bytes=VMEM_LIMIT_MB * 1024 * 1024, dimension_semantics=("arbitrary", "arbitrary"),
            collective_id=PAIR_COLLECTIVE_ID),
    )(half_idx, a, b)


def _window_mask():
    t = lax.broadcasted_iota(jnp.int32, (GMLP_WINDOW, GMLP_WINDOW), 0)
    s = lax.broadcasted_iota(jnp.int32, (GMLP_WINDOW, GMLP_WINDOW), 1)
    return ((s // CHUNK) <= (t // CHUNK)).astype(F32)


def sgu_fwd(pre, glg, glb, ws, bst, name):
    S, H2 = pre.shape
    H = H2 // 2
    W, G = GMLP_WINDOW, GMLP_GROUPS
    gd = H // G
    tm = min(ROW_TILE, S)

    def body(pre_ref, glg_ref, glb_ref, ws_ref, bst_ref, q_ref):
        u = _gelu(pre_ref[:, :H])
        v = _gelu(pre_ref[:, H:])
        mu = jnp.mean(v, axis=-1, keepdims=True)
        vc = v - mu
        var = jnp.mean(vc * vc, axis=-1, keepdims=True)
        vn = ((vc * lax.rsqrt(var + LN_EPS)) * glg_ref[...] + glb_ref[...]).astype(BF16)
        mask = _window_mask()
        for g in range(G):
            wsg = (ws_ref[g] * mask).astype(BF16)
            bcol = bst_ref[:, g:g + 1]
            for wi in range(tm // W):
                rows = slice(wi * W, (wi + 1) * W)
                cols = slice(g * gd, (g + 1) * gd)
                s = jnp.dot(wsg, vn[rows, cols], preferred_element_type=F32) + bcol
                q_ref[rows, cols] = (u[rows, cols] * s).astype(BF16)

    return pl.pallas_call(
        body, name=name, grid=(S // tm,),
        in_specs=[pl.BlockSpec((tm, H2), _row(0)), pl.BlockSpec((1, H), _fixed2), pl.BlockSpec((1, H), _fixed2),
                  pl.BlockSpec((G, W, W), _fixed3), pl.BlockSpec((W, G), _fixed2)],
        out_specs=pl.BlockSpec((tm, H), _row(0)),
        out_shape=jax.ShapeDtypeStruct((S, H), BF16),
        compiler_params=_params(("parallel",)),
    )(pre, glg, glb, ws, bst)


def sgu_bwd(dq, pre, glg, glb, ws, bst, name):
    S, H2 = pre.shape
    H = H2 // 2
    W, G = GMLP_WINDOW, GMLP_GROUPS
    gd = H // G
    tm = min(ROW_TILE, S)

    def body(dq_ref, pre_ref, glg_ref, glb_ref, ws_ref, bst_ref,
             dpre_ref, dws_ref, dss_ref, dgl_ref, dbin_ref, du_s, dvn_s):
        @pl.when(pl.program_id(0) == 0)
        def _():
            dws_ref[...] = jnp.zeros_like(dws_ref)
            dss_ref[...] = jnp.zeros_like(dss_ref)
            dgl_ref[...] = jnp.zeros_like(dgl_ref)
            dbin_ref[...] = jnp.zeros_like(dbin_ref)

        pu = pre_ref[:, :H]
        pv = pre_ref[:, H:]
        u = _gelu(pu)
        v = _gelu(pv)
        mu = jnp.mean(v, axis=-1, keepdims=True)
        vc = v - mu
        var = jnp.mean(vc * vc, axis=-1, keepdims=True)
        rstd = lax.rsqrt(var + LN_EPS)
        vhat = vc * rstd
        vn = (vhat * glg_ref[...] + glb_ref[...]).astype(BF16)
        mask = _window_mask()
        for g in range(G):
            wsg = (ws_ref[g] * mask).astype(BF16)
            bcol = bst_ref[:, g:g + 1]
            cols = slice(g * gd, (g + 1) * gd)
            for wi in range(tm // W):
                rows = slice(wi * W, (wi + 1) * W)
                vblk = vn[rows, cols]
                s = jnp.dot(wsg, vblk, preferred_element_type=F32) + bcol
                dqb = dq_ref[rows, cols].astype(F32)
                du_s[rows, cols] = dqb * s
                ds = dqb * u[rows, cols]
                dss_ref[:, cols] += ds
                dsb = ds.astype(BF16)
                dvn_s[rows, cols] = lax.dot_general(wsg, dsb, TN, preferred_element_type=F32)
                dws_ref[g] += lax.dot_general(dsb, vblk, NT, preferred_element_type=F32) * mask
        dvn = dvn_s[...]
        dgl_ref[0:1, :] += jnp.sum(dvn * vhat, axis=0, keepdims=True)
        dgl_ref[1:2, :] += jnp.sum(dvn, axis=0, keepdims=True)
        dvh = dvn * glg_ref[...]
        m1 = jnp.mean(dvh, axis=-1, keepdims=True)
        m2 = jnp.mean(dvh * vhat, axis=-1, keepdims=True)
        dv = rstd * (dvh - m1 - vhat * m2)
        dpu = du_s[...] * _gelu_grad(pu)
        dpv = dv * _gelu_grad(pv)
        dbin_ref[0:1, :H] += jnp.sum(dpu, axis=0, keepdims=True)
        dbin_ref[0:1, H:] += jnp.sum(dpv, axis=0, keepdims=True)
        dpre_ref[:, :H] = dpu.astype(BF16)
        dpre_ref[:, H:] = dpv.astype(BF16)

    return pl.pallas_call(
        body, name=name, grid=(S // tm,),
        in_specs=[pl.BlockSpec((tm, H), _row(0)), pl.BlockSpec((tm, H2), _row(0)), pl.BlockSpec((1, H), _fixed2),
                  pl.BlockSpec((1, H), _fixed2), pl.BlockSpec((G, W, W), _fixed3), pl.BlockSpec((W, G), _fixed2)],
        out_specs=[pl.BlockSpec((tm, H2), _row(0)), pl.BlockSpec((G, W, W), _fixed3), pl.BlockSpec((W, H), _fixed2),
                   pl.BlockSpec((8, H), _fixed2), pl.BlockSpec((8, H2), _fixed2)],
        out_shape=[jax.ShapeDtypeStruct((S, H2), BF16), jax.ShapeDtypeStruct((G, W, W), F32),
                   jax.ShapeDtypeStruct((W, H), F32), jax.ShapeDtypeStruct((8, H), F32),
                   jax.ShapeDtypeStruct((8, H2), F32)],
        scratch_shapes=[pltpu.VMEM((tm, H), F32), pltpu.VMEM((tm, H), F32)],
        compiler_params=_params(("arbitrary",)),
    )(dq, pre, glg, glb, ws, bst)


def group_lane_sum(dss, name):
    W, H = dss.shape
    gd = H // GMLP_GROUPS

    def body(d_ref, o_ref):
        j = lax.broadcasted_iota(jnp.int32, (H, LANES), 0)
        g = lax.broadcasted_iota(jnp.int32, (H, LANES), 1)
        ind = ((j // gd) == g).astype(F32)
        o_ref[...] = jnp.dot(d_ref[...], ind, preferred_element_type=F32, precision=HIGHEST)

    return pl.pallas_call(
        body, name=name, in_specs=[VMEM_SPEC], out_specs=VMEM_SPEC,
        out_shape=jax.ShapeDtypeStruct((W, LANES), F32), compiler_params=_params(),
    )(dss)


def _attn_load(j, cps, q_ref, k_ref, v_ref):
    r = lax.broadcasted_iota(jnp.int32, (CHUNK, BAND), 1)
    chunks = []
    for cc in range(cps):
        start = pl.multiple_of((j * cps + cc) * CHUNK, CHUNK)
        chunks.append((q_ref[cc * CHUNK:(cc + 1) * CHUNK, :], k_ref[pl.ds(start, BAND), :],
                       v_ref[pl.ds(start, BAND), :], (r + start) >= LEFT_PAD))
    return chunks


def _attn_probs(chunks, b_ref, sels, scale):
    qms = [[jnp.where(sel, q2, jnp.zeros_like(q2)) for sel in sels] for q2, _, _, _ in chunks]
    raw = [[lax.dot_general(qm, k2, NT, preferred_element_type=F32) for qm in qms[cc]]
           for cc, (_, k2, _, _) in enumerate(chunks)]
    probs = []
    for cc, (_, _, _, valid) in enumerate(chunks):
        row = []
        for sub in range(2):
            s = jnp.where(valid, raw[cc][sub] * scale + b_ref[sub], -jnp.inf)
            e = jnp.exp(s - jnp.max(s, axis=-1, keepdims=True))
            row.append(e / jnp.sum(e, axis=-1, keepdims=True))
        probs.append(row)
    return qms, probs


def attn_fwd(q, kpad, vpad, bias, name):
    S, D = q.shape
    HP = D // LANES
    cps = min(ATTN_CHUNKS_PER_STEP, S // CHUNK)
    tq = cps * CHUNK
    scale = HEAD_DIM ** -0.5

    def body(q_ref, k_ref, v_ref, b_ref, o_ref):
        sel0 = lax.broadcasted_iota(jnp.int32, (CHUNK, LANES), 1) < HEAD_DIM
        chunks = _attn_load(pl.program_id(1), cps, q_ref, k_ref, v_ref)
        _, probs = _attn_probs(chunks, b_ref, (sel0, jnp.logical_not(sel0)), scale)
        outs = [[jnp.dot(probs[cc][sub].astype(BF16), v2, preferred_element_type=F32) for sub in range(2)]
                for cc, (_, _, v2, _) in enumerate(chunks)]
        o_ref[...] = jnp.concatenate([jnp.where(sel0, o[0], o[1]) for o in outs], axis=0).astype(BF16)

    kv_spec = pl.BlockSpec((S + LEFT_PAD, LANES), lambda h, j: (0, h))
    return pl.pallas_call(
        body, name=name, grid=(HP, S // tq),
        in_specs=[pl.BlockSpec((tq, LANES), lambda h, j: (j, h)), kv_spec, kv_spec,
                  pl.BlockSpec((2, CHUNK, BAND), lambda h, j: (h, 0, 0))],
        out_specs=pl.BlockSpec((tq, LANES), lambda h, j: (j, h)),
        out_shape=jax.ShapeDtypeStruct((S, D), BF16),
        compiler_params=_params(("parallel", "parallel")),
    )(q, kpad, vpad, bias)


def attn_bwd(q, do, kpad, vpad, bias, dk_in, dv_in, name):
    S, D = q.shape
    HP = D // LANES
    NH = 2 * HP
    cps = min(ATTN_CHUNKS_PER_STEP, S // CHUNK)
    tq = cps * CHUNK
    nj = S // tq
    scale = HEAD_DIM ** -0.5

    def body(q_ref, do_ref, k_ref, v_ref, b_ref, dki_ref, dvi_ref, dq_ref, dk_ref, dv_ref, db_ref, dk_acc, dv_acc):
        j = pl.program_id(1)

        @pl.when(j == 0)
        def _():
            dk_acc[:LEFT_PAD, :] = jnp.zeros((LEFT_PAD, LANES), F32)
            dv_acc[:LEFT_PAD, :] = jnp.zeros((LEFT_PAD, LANES), F32)
            dk_acc[LEFT_PAD:, :] = dki_ref[...]
            dv_acc[LEFT_PAD:, :] = dvi_ref[...]
            db_ref[...] = jnp.zeros_like(db_ref)

        sel0 = lax.broadcasted_iota(jnp.int32, (CHUNK, LANES), 1) < HEAD_DIM
        sels = (sel0, jnp.logical_not(sel0))
        chunks = _attn_load(j, cps, q_ref, k_ref, v_ref)
        pairs = [(cc, sub) for cc in range(cps) for sub in range(2)]
        qms, probs = _attn_probs(chunks, b_ref, sels, scale)
        doms = [[jnp.where(sel, do_ref[cc * CHUNK:(cc + 1) * CHUNK, :], jnp.zeros((CHUNK, LANES), BF16))
                 for sel in sels] for cc in range(cps)]
        dps = {(cc, sub): lax.dot_general(doms[cc][sub], chunks[cc][2], NT, preferred_element_type=F32)
               for cc, sub in pairs}
        dss = {}
        for cc, sub in pairs:
            p = probs[cc][sub]
            dss[cc, sub] = p * (dps[cc, sub] - jnp.sum(dps[cc, sub] * p, axis=-1, keepdims=True))
        dsb = {key: ds.astype(BF16) for key, ds in dss.items()}
        dqs = {(cc, sub): jnp.dot(dsb[cc, sub], chunks[cc][1], preferred_element_type=F32) * scale
               for cc, sub in pairs}
        dks = {(cc, sub): lax.dot_general(dsb[cc, sub], qms[cc][sub], TN, preferred_element_type=F32) * scale
               for cc, sub in pairs}
        dvs = {(cc, sub): lax.dot_general(probs[cc][sub].astype(BF16), doms[cc][sub], TN,
                                          preferred_element_type=F32) for cc, sub in pairs}
        dq_ref[...] = jnp.concatenate([jnp.where(sel0, dqs[cc, 0], dqs[cc, 1]) for cc in range(cps)],
                                      axis=0).astype(BF16)
        for sub in range(2):
            total = dss[0, sub]
            for cc in range(1, cps):
                total = total + dss[cc, sub]
            db_ref[sub] += total
        dk_parts = [dks[cc, 0] + dks[cc, 1] for cc in range(cps)]
        dv_parts = [dvs[cc, 0] + dvs[cc, 1] for cc in range(cps)]

        def window(parts):
            blocks = []
            for rb in range(cps - 1 + BAND // CHUNK):
                acc = None
                for cc in range(cps):
                    b = rb - cc
                    if 0 <= b < BAND // CHUNK:
                        piece = parts[cc][b * CHUNK:(b + 1) * CHUNK, :]
                        acc = piece if acc is None else acc + piece
                blocks.append(acc)
            return jnp.concatenate(blocks, axis=0)

        span = pl.ds(pl.multiple_of(j * cps * CHUNK, CHUNK), (cps - 1) * CHUNK + BAND)
        dk_acc[span, :] += window(dk_parts)
        dv_acc[span, :] += window(dv_parts)

        @pl.when(j == nj - 1)
        def _():
            dk_ref[...] = dk_acc[LEFT_PAD:, :]
            dv_ref[...] = dv_acc[LEFT_PAD:, :]

    q_spec = pl.BlockSpec((tq, LANES), lambda h, j: (j, h))
    kv_spec = pl.BlockSpec((S + LEFT_PAD, LANES), lambda h, j: (0, h))
    col_spec = pl.BlockSpec((S, LANES), lambda h, j: (0, h))
    b_spec = pl.BlockSpec((2, CHUNK, BAND), lambda h, j: (h, 0, 0))
    return pl.pallas_call(
        body, name=name, grid=(HP, nj),
        in_specs=[q_spec, q_spec, kv_spec, kv_spec, b_spec, col_spec, col_spec],
        out_specs=[q_spec, col_spec, col_spec, b_spec],
        out_shape=[jax.ShapeDtypeStruct((S, D), BF16), jax.ShapeDtypeStruct((S, D), F32),
                   jax.ShapeDtypeStruct((S, D), F32), jax.ShapeDtypeStruct((NH, CHUNK, BAND), F32)],
        scratch_shapes=[pltpu.VMEM((S + LEFT_PAD, LANES), F32), pltpu.VMEM((S + LEFT_PAD, LANES), F32)],
        compiler_params=_params(("parallel", "arbitrary")),
    )(q, do, kpad, vpad, bias, dk_in, dv_in)


def _rel_onehot(t):
    r = lax.broadcasted_iota(jnp.int32, (BAND, N_REL), 0)
    i = lax.broadcasted_iota(jnp.int32, (BAND, N_REL), 1)
    idx = jnp.clip(t + LEFT_PAD - r, -(CHUNK - 1), MAX_REL) + (CHUNK - 1)
    return (idx == i).astype(BF16)


def _split3(v):
    hi = v.astype(BF16)
    rest = v - hi.astype(F32)
    mid = rest.astype(BF16)
    return hi, mid, (rest - mid.astype(F32)).astype(BF16)


def _dot_onehot(parts, onehot, dims):
    hi, mid, lo = [lax.dot_general(p, onehot, dims, preferred_element_type=F32) for p in parts]
    return (hi + mid) + lo


def bias_expand(rb, name):
    NH = rb.shape[0]

    def body(rb_ref, o_ref):
        parts = _split3(rb_ref[...])

        def step(t, carry):
            o_ref[t] = _dot_onehot(parts, _rel_onehot(t), NT)
            return carry

        lax.fori_loop(0, CHUNK, step, 0)

    return pl.pallas_call(
        body, name=name, in_specs=[VMEM_SPEC], out_specs=VMEM_SPEC,
        out_shape=jax.ShapeDtypeStruct((CHUNK, NH, BAND), F32), compiler_params=_params(),
    )(rb)


def bias_grad(dsum, name):
    NH = dsum.shape[1]

    def body(d_ref, o_ref):
        def step(t, acc):
            return acc + _dot_onehot(_split3(d_ref[t]), _rel_onehot(t), (((1,), (0,)), ((), ())))

        o_ref[...] = lax.fori_loop(0, CHUNK, step, jnp.zeros((NH, N_REL), F32))

    return pl.pallas_call(
        body, name=name, in_specs=[VMEM_SPEC], out_specs=VMEM_SPEC,
        out_shape=jax.ShapeDtypeStruct((NH, N_REL), F32), compiler_params=_params(),
    )(dsum)


def loss_grad(y, tgt, name):
    S, D = y.shape
    tm = min(MATMUL_ROW_TILE, S)

    def body(y_ref, t_ref, d_ref, acc_ref):
        @pl.when(pl.program_id(0) == 0)
        def _():
            acc_ref[...] = jnp.zeros_like(acc_ref)

        err = y_ref[...] - t_ref[...]
        d_ref[...] = err * (1.0 / D)
        acc_ref[0:1, :] += jnp.sum(err * err, axis=0, keepdims=True)

    tile = pl.BlockSpec((tm, D), _row(0))
    return pl.pallas_call(
        body, name=name, grid=(S // tm,), in_specs=[tile, tile],
        out_specs=[tile, pl.BlockSpec((8, D), _fixed2)],
        out_shape=[jax.ShapeDtypeStruct((S, D), F32), jax.ShapeDtypeStruct((8, D), F32)],
        compiler_params=_params(("arbitrary",)),
    )(y, tgt)


def _col_tile(n):
    for t in (768, 512, 256, 128):
        if n % t == 0:
            return t
    return n


def ada_fwd(c_all, w, b, name):
    L, D, n = w.shape
    tn = _col_tile(n)

    def body(c_ref, w_ref, b_ref, o_ref):
        cv = c_ref[...]
        ca = cv * _sigmoid(cv)
        o_ref[...] = jnp.dot(ca, w_ref[...], preferred_element_type=F32, precision=HIGHEST) + b_ref[...]

    return pl.pallas_call(
        body, name=name, grid=(L, n // tn),
        in_specs=[pl.BlockSpec((N_DEV, D), lambda l, j: (0, 0)), pl.BlockSpec((None, D, tn), lambda l, j: (l, 0, j)),
                  pl.BlockSpec((None, 1, tn), lambda l, j: (l, 0, j))],
        out_specs=pl.BlockSpec((None, N_DEV, tn), lambda l, j: (l, 0, j)),
        out_shape=jax.ShapeDtypeStruct((L, N_DEV, n), F32),
        compiler_params=_params(("parallel", "parallel")),
    )(c_all, w, b)


def ada_wgrad(c_all_t, dmod, name):
    L, _, n = dmod.shape
    D = c_all_t.shape[0]
    tn = _col_tile(n)

    def body(c_ref, d_ref, o_ref):
        cv = c_ref[...]
        ca = cv * _sigmoid(cv)
        o_ref[...] = jnp.dot(ca, d_ref[...], preferred_element_type=F32, precision=HIGHEST)

    return pl.pallas_call(
        body, name=name, grid=(L, n // tn),
        in_specs=[pl.BlockSpec((D, N_DEV), lambda l, j: (0, 0)), pl.BlockSpec((None, N_DEV, tn), lambda l, j: (l, 0, j))],
        out_specs=pl.BlockSpec((None, D, tn), lambda l, j: (l, 0, j)),
        out_shape=jax.ShapeDtypeStruct((L, D, n), F32),
        compiler_params=_params(("parallel", "parallel")),
    )(c_all_t, dmod)


ELEMENTWISE_BLOCK_BYTES = 3 * 1024 * 1024


def _elementwise_rows(rows, row_bytes):
    for t in (4096, 2048, 1024, 512, 256, 128, 64, 32, 16):
        if rows % t == 0 and t * row_bytes <= ELEMENTWISE_BLOCK_BYTES:
            return t
    return rows


def sum_leading(a, name):
    n, M, N = a.shape
    tr = _elementwise_rows(M, n * N * 4)

    def body(a_ref, o_ref):
        acc = a_ref[0]
        for i in range(1, n):
            acc = acc + a_ref[i]
        o_ref[...] = acc

    return pl.pallas_call(
        body, name=name, grid=(M // tr,),
        in_specs=[pl.BlockSpec((n, tr, N), lambda i: (0, i, 0))],
        out_specs=pl.BlockSpec((tr, N), _row(0)),
        out_shape=jax.ShapeDtypeStruct((M, N), F32),
        compiler_params=_params(("parallel",)),
    )(a)


def chip_sum(psum, land, chip_idx, transposed, name):
    n, M, N = psum.shape
    tr = M if transposed else _elementwise_rows(M, N * 8)

    def body(s_ref, p_ref, a_ref, b_ref, c_ref, o_ref):
        total = ((p_ref[...].astype(F32) + a_ref[...].astype(F32)) + b_ref[...].astype(F32)) + c_ref[...].astype(F32)
        o_ref[...] = jnp.transpose(total) if transposed else total

    def entry(j):
        return pl.BlockSpec((None, tr, N), lambda i, s: ((s[0] + j) % n, i, 0))

    out_block, out_dims = ((N, tr), (N, M)) if transposed else ((tr, N), (M, N))
    return pl.pallas_call(
        body, name=name,
        grid_spec=pltpu.PrefetchScalarGridSpec(
            num_scalar_prefetch=1, grid=(M // tr,),
            in_specs=[entry(0), entry(1), entry(2), entry(3)],
            out_specs=pl.BlockSpec(out_block, lambda i, s: (0, 0) if transposed else (i, 0))),
        out_shape=jax.ShapeDtypeStruct(out_dims, F32),
        compiler_params=_params(("parallel",)),
    )(chip_idx, psum, land, land, land)


def adamw(w, g, m, v, name):
    M, N = w.shape
    tr = _elementwise_rows(M, N * 4)
    c1 = 1.0 - ADAM_B1 ** ADAM_STEP
    c2 = 1.0 - ADAM_B2 ** ADAM_STEP

    def body(w_ref, g_ref, m_ref, v_ref, d_ref, nm_ref, nv_ref):
        g = g_ref[...]
        nm = ADAM_B1 * m_ref[...] + (1.0 - ADAM_B1) * g
        nv = ADAM_B2 * v_ref[...] + (1.0 - ADAM_B2) * (g * g)
        d_ref[...] = -ADAM_LR * ((nm / c1) / (jnp.sqrt(nv / c2) + ADAM_EPS) + ADAM_WD * w_ref[...])
        nm_ref[...] = nm
        nv_ref[...] = nv

    spec = pl.BlockSpec((tr, N), _row(0))
    shp = jax.ShapeDtypeStruct((M, N), F32)
    return pl.pallas_call(
        body, name=name, grid=(M // tr,), in_specs=[spec] * 4, out_specs=[spec] * 3, out_shape=[shp] * 3,
        compiler_params=_params(("parallel",)),
    )(w, g, m, v)


def _coords():
    return lax.axis_index("x"), lax.axis_index("y"), lax.axis_index("c")


def all_gather8(block, name):
    m_per, n = block.shape

    def body(x_ref, out_ref, send_sems, recv_sems, local_sem):
        x, y, c = _coords()
        me, sibling = (x, y, c), (x, y, 1 - c)
        chips = [(1 - x, y), (x, 1 - y), (1 - x, 1 - y)]

        def rows(px, py, pc):
            return out_ref.at[pl.ds((4 * px + 2 * py + pc) * m_per, m_per), :]

        def copy(k, blk, to, src=None):
            return pltpu.make_async_remote_copy(
                src_ref=rows(*blk) if src is None else src, dst_ref=rows(*blk),
                send_sem=send_sems.at[k], recv_sem=recv_sems.at[k], device_id=to, device_id_type=MESH)

        mine = pltpu.make_async_copy(x_ref, rows(*me), local_sem)
        mine.start()
        first = [copy(0, me, sibling, src=x_ref)]
        first += [copy(1 + j, me, (*chip, c), src=x_ref) for j, chip in enumerate(chips)]
        for cp in first:
            cp.start()
        passed = [copy(4 + j, (*chip, c), sibling) for j, chip in enumerate(chips)]
        for j, chip in enumerate(chips):
            copy(1 + j, (*chip, c), me).wait_recv()
            passed[j].start()
        copy(0, sibling, me).wait_recv()
        for j, chip in enumerate(chips):
            copy(4 + j, (*chip, 1 - c), me).wait_recv()
        for cp in first + passed:
            cp.wait_send()
        mine.wait()

    return pl.pallas_call(
        body, name=name, in_specs=[VMEM_SPEC], out_specs=VMEM_SPEC,
        out_shape=jax.ShapeDtypeStruct((N_DEV * m_per, n), block.dtype),
        scratch_shapes=[pltpu.SemaphoreType.DMA((7,)), pltpu.SemaphoreType.DMA((7,)), pltpu.SemaphoreType.DMA],
        compiler_params=_params(),
    )(block)


def _other_chips(x, y):
    return [(1 - x, y), (x, 1 - y), (1 - x, 1 - y)]


HBM_SPEC = pl.BlockSpec(memory_space=pltpu.HBM)
SEM_SPEC = pl.BlockSpec(memory_space=pltpu.SEMAPHORE)
DATAFLOW = pltpu.SideEffectType.DATAFLOW_SIDE_EFFECTING


def _chip_peers(x, y, c):
    return [(px, py, c) for px, py in _other_chips(x, y)]


def _sibling_peer(x, y, c):
    return [(x, y, 1 - c)]


def _weight_desc(_, k, land_ref, peer, me):
    h = land_ref.shape[1] // 2
    rows = pl.ds(me[2] * h, h)
    mine = land_ref.at[2 * me[0] + me[1], rows, :]
    return mine, mine, land_ref.at[2 * peer[0] + peer[1], rows, :]


def _grad_desc(psum_ref, k, land_ref, peer, me):
    return psum_ref.at[2 * peer[0] + peer[1]], land_ref.at[2 * me[0] + me[1]], land_ref.at[2 * peer[0] + peer[1]]


def _pair_desc(grad_ref, k, land_ref, peer, me):
    h = land_ref.shape[1]
    return grad_ref.at[:, pl.ds(peer[2] * h, h), :], land_ref, land_ref


def _whole_desc(src_ref, k, land_ref, peer, me):
    return src_ref, land_ref, land_ref


def exchange_start(srcs, lands, units, groups, desc, peers, after, name):
    n_s, n_l, n_g = len(srcs), len(lands), len(groups)
    n_p = len(peers(0, 0, 0))

    def body(*refs):
        s_refs, l_refs = refs[:n_s], refs[n_s:n_s + n_l]
        outs = refs[n_s + n_l + 1:]
        sems, token = outs[:2 * n_g], outs[-1]
        me = _coords()
        for g, ids in enumerate(groups):
            for i, u in enumerate(ids):
                si, k = units[u]
                for j, peer in enumerate(peers(*me)):
                    src, dst, _ = desc(s_refs[si] if s_refs else None, k, l_refs[u], peer, me)
                    pltpu.make_async_remote_copy(
                        src_ref=src, dst_ref=dst, send_sem=sems[2 * g].at[n_p * i + j],
                        recv_sem=sems[2 * g + 1].at[n_p * i + j], device_id=peer, device_id_type=MESH).start()
        token[...] = jnp.zeros_like(token)

    arrs = list(srcs) + list(lands)
    sem_shapes = [pltpu.SemaphoreType.DMA((n_p * len(ids),)) for ids in groups for _ in range(2)]
    outs = pl.pallas_call(
        body, name=name,
        in_specs=[HBM_SPEC] * len(arrs) + [ANY],
        out_specs=[SEM_SPEC] * (2 * n_g) + [HBM_SPEC] * len(arrs) + [VMEM_SPEC],
        out_shape=sem_shapes + [pltpu.HBM(a.shape, a.dtype) for a in arrs] + [jax.ShapeDtypeStruct((8, LANES), F32)],
        input_output_aliases={i: 2 * n_g + i for i in range(len(arrs))},
        compiler_params=pltpu.CompilerParams(has_side_effects=DATAFLOW),
    )(*[pltpu.with_memory_space_constraint(a, pltpu.HBM) for a in arrs], after)
    sems = outs[:2 * n_g]
    thru = outs[2 * n_g:2 * n_g + len(arrs)]
    return sems, list(thru[:n_s]), list(thru[n_s:]), outs[-1]


def exchange_wait(srcs, lands, units, send_sem, recv_sem, desc, peers, after, name):
    n_s, n_l = len(srcs), len(lands)
    n_p = len(peers(0, 0, 0))

    def body(*refs):
        s_refs, l_refs = refs[:n_s], refs[n_s:n_s + n_l]
        send_sems, recv_sems = refs[n_s + n_l], refs[n_s + n_l + 1]
        me = _coords()
        for i, (si, k) in enumerate(units):
            for j, peer in enumerate(peers(*me)):
                src, _, mine = desc(s_refs[si] if s_refs else None, k, l_refs[i], peer, me)
                cp = pltpu.make_async_remote_copy(
                    src_ref=src, dst_ref=mine, send_sem=send_sems.at[n_p * i + j], recv_sem=recv_sems.at[n_p * i + j],
                    device_id=peer, device_id_type=MESH)
                cp.wait_send()
                cp.wait_recv()

    arrs = list(srcs) + list(lands)
    outs = pl.pallas_call(
        body, name=name,
        in_specs=[HBM_SPEC] * len(arrs) + [SEM_SPEC, SEM_SPEC, ANY],
        out_specs=[HBM_SPEC] * len(arrs),
        out_shape=[pltpu.HBM(a.shape, a.dtype) for a in arrs],
        input_output_aliases={i: i for i in range(len(arrs))},
        compiler_params=pltpu.CompilerParams(has_side_effects=DATAFLOW),
    )(*arrs, send_sem, recv_sem, after)
    return list(outs[:n_s]), list(outs[n_s:])


def sibling_fill(lands, name):
    n_u = len(lands)

    def body(*refs):
        ins, outs = refs[:n_u], refs[n_u:2 * n_u]
        send_sems, recv_sems = refs[2 * n_u:]
        x, y, c = _coords()
        sends = []
        for u in range(n_u):
            h = ins[u].shape[1] // 2
            for j, (px, py) in enumerate(_other_chips(x, y)):
                part = (2 * px + py, pl.ds(c * h, h), slice(None))
                cp = pltpu.make_async_remote_copy(
                    src_ref=ins[u].at[part], dst_ref=outs[u].at[part], send_sem=send_sems.at[3 * u + j],
                    recv_sem=recv_sems.at[3 * u + j], device_id=(x, y, 1 - c), device_id_type=MESH)
                cp.start()
                sends.append(cp)
        for u in range(n_u):
            h = ins[u].shape[1] // 2
            for j, (px, py) in enumerate(_other_chips(x, y)):
                theirs = (2 * px + py, pl.ds((1 - c) * h, h), slice(None))
                pltpu.make_async_remote_copy(
                    src_ref=ins[u].at[theirs], dst_ref=outs[u].at[theirs], send_sem=send_sems.at[3 * u + j],
                    recv_sem=recv_sems.at[3 * u + j], device_id=(x, y, 1 - c), device_id_type=MESH).wait_recv()
        for cp in sends:
            cp.wait_send()

    return pl.pallas_call(
        body, name=name, in_specs=[ANY] * n_u, out_specs=[ANY] * n_u,
        out_shape=[jax.ShapeDtypeStruct(a.shape, a.dtype) for a in lands],
        input_output_aliases={i: i for i in range(n_u)},
        scratch_shapes=[pltpu.SemaphoreType.DMA((3 * n_u,)), pltpu.SemaphoreType.DMA((3 * n_u,))],
        compiler_params=_params(),
    )(*lands)


def _pack_rows(parts):
    flat = jnp.concatenate([p.reshape(-1).astype(F32) for p in parts])
    n = flat.shape[0]
    padded = -(-n // (8 * LANES)) * (8 * LANES)
    return jnp.pad(flat, (0, padded - n)).reshape(-1, LANES)


def _unpack_rows(packed, shapes):
    flat = packed.reshape(-1)
    out, off = [], 0
    for s in shapes:
        size = 1
        for d in s:
            size *= d
        out.append(flat[off:off + size].reshape(s))
        off += size
    return out


def _shard_last(full, s_me):
    n = full.shape[-1] // N_CHIP
    return lax.dynamic_slice_in_dim(full, s_me * n, n, axis=full.ndim - 1)


def _unshard_last(g):
    moved = jnp.moveaxis(g, 0, -2)
    return moved.reshape(moved.shape[:-2] + (moved.shape[-2] * moved.shape[-1],))


def kernel(x, c, w_ada, b_ada, ln_g, ln_b, ffn_gu, ffn_down, gmlp_w_in, gmlp_b_in, gmlp_ln_g, gmlp_ln_b, gmlp_w_s, gmlp_b_s, gmlp_w_out, w_ada_kv, b_ada_kv, w_kv, attn_w_q, attn_rel_bias, attn_w_o, loss_target, m_w_ada, m_b_ada, m_ln_g, m_ln_b, m_ffn_gu, m_ffn_down, m_gmlp_w_in, m_gmlp_b_in, m_gmlp_ln_g, m_gmlp_ln_b, m_gmlp_w_s, m_gmlp_b_s, m_gmlp_w_out, m_w_ada_kv, m_b_ada_kv, m_w_kv, m_attn_w_q, m_attn_rel_bias, m_attn_w_o, v_w_ada, v_b_ada, v_ln_g, v_ln_b, v_ffn_gu, v_ffn_down, v_gmlp_w_in, v_gmlp_b_in, v_gmlp_ln_g, v_gmlp_ln_b, v_gmlp_w_s, v_gmlp_b_s, v_gmlp_w_out, v_w_ada_kv, v_b_ada_kv, v_w_kv, v_attn_w_q, v_attn_rel_bias, v_attn_w_o):
    xi, yi, ci = _coords()
    s_me = 2 * xi + yi
    dev = 4 * xi + 2 * yi + ci

    x0 = x[0]
    tgt = loss_target[0]
    S, D = x0.shape
    L = w_ada.shape[0]
    NA = gmlp_w_in.shape[0]
    NB = attn_w_q.shape[0]
    NH = D // HEAD_DIM
    alpha = (2.0 * L) ** 0.25
    n_ada = w_ada.shape[2]
    n_kv = w_ada_kv.shape[1]

    stack_names = ["ffn_gu", "ffn_down", "gmlp_w_in", "gmlp_w_out", "w_kv", "attn_w_q", "attn_w_o"]
    stack_src = dict(ffn_gu=ffn_gu, ffn_down=ffn_down, gmlp_w_in=gmlp_w_in, gmlp_w_out=gmlp_w_out, w_kv=w_kv[None],
                     attn_w_q=attn_w_q, attn_w_o=attn_w_o)
    stacks = [stack_src[nm].reshape((-1,) + stack_src[nm].shape[-2:]) for nm in stack_names]
    units = [(si, k) for si, st in enumerate(stacks) for k in range(st.shape[0])]
    unit_of = {(stack_names[si], k): u for u, (si, k) in enumerate(units)}
    weight_groups = [[("ffn_gu", 0)], [("ffn_down", 0)]]
    for l in range(L):
        mixer = [("gmlp_w_in", l), ("gmlp_w_out", l)] if l < NA else [("attn_w_q", l - NA), ("attn_w_o", l - NA)]
        first, last = [("ffn_gu", 2 * l), ("ffn_down", 2 * l)], [("ffn_gu", 2 * l + 1), ("ffn_down", 2 * l + 1)]
        if l == 0:
            weight_groups += [mixer, last]
        elif l < NA:
            weight_groups += [first, mixer, last]
        else:
            weight_groups.append(([("w_kv", 0)] if l == NA else []) + first + mixer + last)
    weight_groups = [[unit_of[n] for n in names] for names in weight_groups]
    group_of = {u: g for g, ids in enumerate(weight_groups) for u in ids}

    c_all = all_gather8(jnp.broadcast_to(c, (8, D)), "ag_c").reshape(N_DEV, 8, D)[:, 0]
    b_ada_sh = lax.dynamic_slice_in_dim(b_ada, s_me * n_ada, n_ada, axis=1)
    b_kv_sh = lax.dynamic_slice_in_dim(b_ada_kv, s_me * n_kv, n_kv, axis=0)
    mod_part = ada_fwd(c_all, w_ada, b_ada_sh[:, None, :], "ada_fwd")
    mkv_part = ada_fwd(c_all, w_ada_kv[None], b_kv_sh[None, None, :], "ada_kv_fwd")
    part = jnp.concatenate([jnp.transpose(mod_part, (1, 0, 2)).reshape(N_DEV, L * n_ada), mkv_part[0]], axis=1)
    width = part.shape[1]
    pad_w = -(-width // LANES) * LANES - width
    all_part = all_gather8(jnp.pad(part, ((0, 0), (0, pad_w))), "ag_mod").reshape(N_DEV, N_DEV, width + pad_w)
    mine = lax.dynamic_index_in_dim(all_part[0::2], dev, axis=1, keepdims=False)
    mod = jnp.transpose(mine[:, :L * n_ada].reshape(N_CHIP, L, n_ada), (1, 0, 2)).reshape(L, N_MOD, D)
    mkv = mine[:, L * n_ada:width].reshape(2, D)

    def mrow(l, k):
        return mod[l, k][None, :]

    small_shapes = [ln_g.shape, ln_b.shape, gmlp_b_in.shape, gmlp_ln_g.shape, gmlp_ln_b.shape, attn_rel_bias.shape]
    small_pack = _pack_rows([ln_g, ln_b, gmlp_b_in, gmlp_ln_g, gmlp_ln_b, attn_rel_bias])
    small_all = all_gather8(small_pack, "ag_small_params").reshape((N_DEV,) + small_pack.shape)[0::2]
    sm = [_unpack_rows(small_all[s], small_shapes) for s in range(N_CHIP)]
    ln_g_f, ln_b_f, b_in_f, gln_g_f, gln_b_f, rel_f = [
        _unshard_last(jnp.stack([sm[s][i] for s in range(N_CHIP)])) for i in range(len(small_shapes))]

    def landing(u):
        si, k = units[u]
        shard = stacks[si][k]
        if group_of[u] < 2:
            shard = lax.optimization_barrier(shard)
        shard = shard.astype(BF16)
        return lax.dynamic_update_slice(lax.empty((N_CHIP,) + shard.shape, BF16), shard[None], (s_me, 0, 0))

    gathers_done = jnp.concatenate([mod.reshape(-1)[:LANES], small_all.reshape(-1)[:LANES]])
    w_sems, lands_t = {}, {}
    for part, groups, name in ((0, weight_groups[:2], "weight_send_start_first"),
                               (1, weight_groups[2:], "weight_send_start_rest")):
        ids = [u for grp in groups for u in grp]
        local = [[ids.index(u) for u in grp] for grp in groups]
        sems, _, lands, gathers_done = exchange_start([], [landing(u) for u in ids], [units[u] for u in ids], local,
                                                      _weight_desc, _chip_peers, gathers_done, name)
        for i, grp in enumerate(groups):
            w_sems[2 * part + i] = (sems[2 * i], sems[2 * i + 1])
        lands_t.update(zip(ids, lands))
    wg = {}
    latest = [gathers_done]

    def W(nm, k):
        u = unit_of[(nm, k)]
        if u not in wg:
            g = group_of[u]
            ids = weight_groups[g]
            _, got = exchange_wait([], [lands_t[v] for v in ids], [units[v] for v in ids], w_sems[g][0],
                                   w_sems[g][1], _weight_desc, _chip_peers, latest[0], "weight_send_wait_%d" % g)
            wg.update(zip(ids, sibling_fill(got, "weight_sibling_fill")))
        return wg[u]

    def Wrows(nm, k):
        w4 = W(nm, k)
        return w4.reshape(w4.shape[0] * w4.shape[1], w4.shape[2])

    bst = [jnp.transpose(gmlp_b_s[j]) for j in range(NA)]
    biases = {}

    def make_bias(j, dep):
        rel, _ = lax.optimization_barrier((rel_f[j], dep))
        biases[j] = jnp.transpose(bias_expand(rel, "bias_expand"), (1, 0, 2))
        return biases[j]

    saved = []
    xc = x0
    kpad = vpad = xkv = None
    for l in range(L):
        if l == 1 and NB > 1:
            latest[0] = make_bias(1, xc)
        if l == NA:
            xkv = xc
            kv, hkv = mod_matmul(xc, mkv[1][None], mkv[0][None], W("w_kv", 0), None, BF16, "kv_proj")
            kpad = jnp.pad(kv[:, :D], ((LEFT_PAD, 0), (0, 0)))
            vpad = jnp.pad(kv[:, D:], ((LEFT_PAD, 0), (0, 0)))
        sv = {}
        for i in (0, 2):
            k = 2 * l + i // 2
            gu, hv = mod_matmul(xc, mrow(l, 3 * i + 1), mrow(l, 3 * i), W("ffn_gu", k), None, BF16, "ffn_up")
            latest[0] = hv
            gw = 0.5 * (1.0 + mrow(l, 3 * i + 2))
            xn, xh, rs, yv, av = matmul_res_ln(gu, Wrows("ffn_down", k), xc, gw, ln_g_f[l, i][None],
                                               ln_b_f[l, i][None], alpha, True, "ffn_down")
            sv[i] = dict(x=xc, h=hv, gu=gu, a=av, xh=xh, rs=rs, y=yv, gw=gw)
            xc = latest[0] = xn
            if i == 0:
                if l == 0 and NB > 0:
                    latest[0] = make_bias(0, xc)
                gw = 1.0 + mrow(l, 5)
                if l < NA:
                    pre, hv = mod_matmul(xc, mrow(l, 4), mrow(l, 3), W("gmlp_w_in", l), b_in_f[l][None], F32,
                                         "gmlp_in")
                    qv = sgu_fwd(pre, gln_g_f[l][None], gln_b_f[l][None], gmlp_w_s[l], bst[l], "sgu_fwd")
                    xn, xh, rs, yv = matmul_res_ln(qv, Wrows("gmlp_w_out", l), xc, gw, ln_g_f[l, 1][None],
                                                   ln_b_f[l, 1][None], alpha, False, "gmlp_out")
                    sv[1] = dict(x=xc, h=hv, pre=pre, a=qv, xh=xh, rs=rs, y=yv, gw=gw)
                else:
                    j = l - NA
                    if j not in biases:
                        make_bias(j, xc)
                    qh, hv = mod_matmul(xc, mrow(l, 4), mrow(l, 3), Wrows("attn_w_q", j)[None], None, BF16, "attn_q")
                    ov = attn_fwd(qh, kpad, vpad, biases[j], "attn_fwd")
                    xn, xh, rs, yv = matmul_res_ln(ov, Wrows("attn_w_o", j), xc, gw, ln_g_f[l, 1][None],
                                                   ln_b_f[l, 1][None], alpha, False, "attn_out")
                    sv[1] = dict(x=xc, h=hv, q=qh, a=ov, xh=xh, rs=rs, y=yv, gw=gw)
                xc = latest[0] = xn
        saved.append(sv)

    dx, lacc = loss_grad(xc, tgt, "loss_grad")
    loss = lax.psum((0.5 / D) * jnp.sum(lacc[0]), ("x", "y", "c"))

    gpair = [None] * len(units)
    col_split = {u for u, (si, _) in enumerate(units) if stack_names[si] != "ffn_gu"}
    dmod = [[None] * N_MOD for _ in range(L)]
    d_ln_g = [[None] * 3 for _ in range(L)]
    d_ln_b = [[None] * 3 for _ in range(L)]
    d_b_in, d_gln_g, d_gln_b, d_ws, d_bs, d_rel = ([None] * NA, [None] * NA, [None] * NA, [None] * NA, [None] * NA,
                                                  [None] * NB)
    dk = jnp.zeros((S, D), F32)
    dv = jnp.zeros((S, D), F32)
    dmkv = None

    made = []

    core_idx = ci.astype(jnp.int32).reshape(1)
    chip_idx = s_me.astype(jnp.int32).reshape(1)

    def put(nm, k, a, b, name):
        u = unit_of[(nm, k)]
        rows, cols = stacks[units[u][0]].shape[1:]
        if nm == "ffn_gu":
            g = wgrad_pair(b, a, N_CHIP, cols, rows // 2, lambda j, p: j, lambda j, p: p, core_idx, name)
        elif nm in ("gmlp_w_in", "w_kv"):
            g = wgrad_pair(a, b, N_CHIP, rows, cols // 2, lambda j, p: 0, lambda j, p: 2 * j + p, core_idx, name)
        else:
            g = wgrad_pair(a, b, 1, N_CHIP * rows, cols // 2, lambda j, p: 0, lambda j, p: p, core_idx, name)
        gpair[u] = g.reshape((N_CHIP, -1, g.shape[-1]))
        made.append(u)

    own_half, sib_half = {}, {}
    n_started = [0]

    def start_grad_exchange(ids, after):
        psums = [gpair[u] for u in ids]
        n = len(ids)
        tag = n_started[0]
        n_started[0] += 1
        sems, ps_t, q_t, token = exchange_start(psums, [lax.empty(p.shape, p.dtype) for p in psums],
                                                [(i, 0) for i in range(n)], [list(range(n))], _grad_desc, _chip_peers,
                                                after, "grad_send_start_%d" % tag)
        return dict(ids=ids, tag=tag, sems=sems, ps=ps_t, q=q_t), token

    def finish_grad_exchange(pend, after):
        n = len(pend["ids"])
        ps_t, q = exchange_wait(pend["ps"], pend["q"], [(i, 0) for i in range(n)], pend["sems"][0], pend["sems"][1],
                                _grad_desc, _chip_peers, after, "grad_send_wait_%d" % pend["tag"])
        halves = [chip_sum(ps_t[i], q[i], chip_idx, u not in col_split, "grad_chip_sum")
                  for i, u in enumerate(pend["ids"])]
        sems, h_t, land_t, token = exchange_start(halves, [lax.empty(h.shape, h.dtype) for h in halves],
                                              [(i, 0) for i in range(n)], [list(range(n))], _whole_desc,
                                              _sibling_peer, halves[0], "half_send_start_%d" % pend["tag"])
        swaps.append(dict(ids=pend["ids"], tag=pend["tag"], sems=sems, h=h_t, land=land_t))
        return token

    def collect_halves(after):
        for sw in swaps:
            n = len(sw["ids"])
            h, land = exchange_wait(sw["h"], sw["land"], [(i, 0) for i in range(n)], sw["sems"][0], sw["sems"][1],
                                    _whole_desc, _sibling_peer, after, "half_send_wait_%d" % sw["tag"])
            for u, mine, theirs in zip(sw["ids"], h, land):
                own_half[u], sib_half[u] = mine, theirs
        swaps.clear()

    swaps = []
    pending = None
    started_before = jnp.zeros((8, LANES), F32)

    def ln_inputs(l, i):
        t = saved[l][i]
        return (t["xh"], t["rs"], ln_g_f[l, i][None], t["y"], t["gw"], 1.0 if i == 1 else 0.5)

    def record_ln(l, i, acc, row0):
        d_ln_g[l][i], d_ln_b[l][i], dmod[l][3 * i + 2] = acc[row0], acc[row0 + 1], acc[row0 + 2]

    ln_done = None
    for l in reversed(range(L)):
        if l == NA - 1:
            dkv = jnp.concatenate([dk, dv], axis=1)
            put("w_kv", 0, hkv, dkv, "kv_wgrad")
            pdxa, pdy, acc = dgrad_mod(dkv, W("w_kv", 0), dx, xkv, mkv[1][None], ln_inputs(l, 2), alpha, "kv_dgrad")
            dmkv = jnp.stack([acc[1], acc[0]])
            record_ln(l, 2, acc, 2)
            ln_done = (pdxa, pdy)
        sv = saved[l]
        for i in (2, 1, 0):
            t = sv[i]
            if ln_done is None:
                dxa, dy, acc1 = ln_res_bwd(dx, *ln_inputs(l, i), alpha, "ln_res_bwd")
                record_ln(l, i, acc1, 0)
            else:
                dxa, dy = ln_done
                ln_done = None
            before = (l, i - 1) if i > 0 else ((l - 1, 2) if l > 0 and l != NA else None)
            prev = ln_inputs(*before) if before is not None else None
            scl = mrow(l, 3 * i + 1)
            if i != 1:
                k = 2 * l + i // 2
                F = t["gu"].shape[1] // 2
                dgu = ffn_act_bwd(dy, Wrows("ffn_down", k), t["gu"], started_before, "ffn_act_bwd")
                put("ffn_down", k, t["a"], dy, "ffn_down_wgrad")
                put("ffn_gu", k, t["h"], dgu, "ffn_up_wgrad")
                res = dgrad_mod(dgu, W("ffn_gu", k), dxa, t["x"], scl, prev, alpha, "ffn_up_dgrad")
            elif l < NA:
                dq = matmul_nt(dy, Wrows("gmlp_w_out", l), started_before, "gmlp_out_dgrad")
                put("gmlp_w_out", l, t["a"], dy, "gmlp_out_wgrad")
                dpre, dws_l, dss, dgl, dbin = sgu_bwd(dq, t["pre"], gln_g_f[l][None], gln_b_f[l][None], gmlp_w_s[l],
                                                      bst[l], "sgu_bwd")
                d_ws[l] = dws_l
                d_bs[l] = jnp.transpose(group_lane_sum(dss, "sgu_bias_grad")[:, :GMLP_GROUPS])
                d_gln_g[l], d_gln_b[l], d_b_in[l] = dgl[0], dgl[1], dbin[0]
                put("gmlp_w_in", l, t["h"], dpre, "gmlp_in_wgrad")
                res = dgrad_mod(dpre, W("gmlp_w_in", l), dxa, t["x"], scl, prev, alpha, "gmlp_in_dgrad")
            else:
                j = l - NA
                do = matmul_nt(dy, Wrows("attn_w_o", j), started_before, "attn_out_dgrad")
                put("attn_w_o", j, t["a"], dy, "attn_out_wgrad")
                dqh, dk, dv, dbias = attn_bwd(t["q"], do, kpad, vpad, biases[j], dk, dv, "attn_bwd")
                d_rel[j] = bias_grad(jnp.transpose(dbias, (1, 0, 2)), "bias_grad")
                put("attn_w_q", j, t["h"], dqh, "attn_q_wgrad")
                res = dgrad_mod(dqh, Wrows("attn_w_q", j)[None], dxa, t["x"], scl, prev, alpha, "attn_q_dgrad")
            acc2 = res[-1]
            dmod[l][3 * i + 1], dmod[l][3 * i] = acc2[0], acc2[1]
            if before is None:
                dx = res[0]
            else:
                record_ln(*before, acc2, 2)
                ln_done = (res[0], res[1])
            if (i == 0 and l > 0) or (i == 1 and l == 0):
                started, started_before = start_grad_exchange(list(made), acc2)
                made.clear()
                if pending is not None:
                    started_before = started_before + finish_grad_exchange(pending, acc2)
                pending = started
    grad_x = dx[None]

    dvec = _pack_rows([jnp.stack([jnp.stack(r) for r in dmod]), dmkv])
    dvec = lax.optimization_barrier((dvec, [gpair[u] for u in made]))[0]
    n_dvec = L * N_MOD * D + 2 * D
    dall = all_gather8(dvec, "ag_dmod").reshape(N_DEV, -1, LANES)
    db_all = sum_leading(dall, "ada_bias_grad").reshape(-1)[:n_dvec]
    g_b_ada = db_all[:L * N_MOD * D].reshape(L, N_MOD * D)
    g_b_ada_kv = db_all[L * N_MOD * D:]
    dall2 = dall.reshape(N_DEV, -1)[:, :n_dvec]
    dmod_all = dall2[:, :L * N_MOD * D].reshape(N_DEV, L, N_MOD * D)
    dmod_sh = jnp.transpose(lax.dynamic_slice_in_dim(dmod_all, s_me * n_ada, n_ada, axis=2), (1, 0, 2))
    dmkv_sh = lax.dynamic_slice_in_dim(dall2[:, L * N_MOD * D:], s_me * n_kv, n_kv, axis=1)[None]
    c_all_t = jnp.transpose(c_all)
    g_w_ada = ada_wgrad(c_all_t, dmod_sh, "ada_wgrad")
    g_w_ada_kv = ada_wgrad(c_all_t, dmkv_sh, "ada_kv_wgrad")[0]

    small_g = [jnp.stack([jnp.stack(r) for r in d_ln_g]), jnp.stack([jnp.stack(r) for r in d_ln_b]),
               jnp.stack(d_b_in), jnp.stack(d_gln_g), jnp.stack(d_gln_b), jnp.stack(d_rel), jnp.stack(d_ws),
               jnp.stack(d_bs)]
    sg_shapes = [a.shape for a in small_g]
    sg_pack = _pack_rows(small_g)
    sg_all = all_gather8(sg_pack, "ag_small_grads").reshape(N_DEV, -1, LANES)
    sg_sum = _unpack_rows(sum_leading(sg_all, "small_grad_sum"), sg_shapes)
    g_ln_g, g_ln_b, g_b_in, g_gln_g, g_gln_b, g_rel = [_shard_last(a, s_me) for a in sg_sum[:6]]
    g_ws, g_bs = sg_sum[6], sg_sum[7]

    last, _ = start_grad_exchange(list(made), sg_all)

    grads = dict(w_ada=g_w_ada, b_ada=g_b_ada, ln_g=g_ln_g, ln_b=g_ln_b, gmlp_b_in=g_b_in, gmlp_ln_g=g_gln_g,
                 gmlp_ln_b=g_gln_b, gmlp_w_s=g_ws, gmlp_b_s=g_bs, w_ada_kv=g_w_ada_kv, b_ada_kv=g_b_ada_kv,
                 attn_rel_bias=g_rel)
    weights = dict(w_ada=w_ada, b_ada=b_ada, ln_g=ln_g, ln_b=ln_b, ffn_gu=ffn_gu, ffn_down=ffn_down,
                   gmlp_w_in=gmlp_w_in, gmlp_b_in=gmlp_b_in, gmlp_ln_g=gmlp_ln_g, gmlp_ln_b=gmlp_ln_b,
                   gmlp_w_s=gmlp_w_s, gmlp_b_s=gmlp_b_s, gmlp_w_out=gmlp_w_out, w_ada_kv=w_ada_kv,
                   b_ada_kv=b_ada_kv, w_kv=w_kv, attn_w_q=attn_w_q, attn_rel_bias=attn_rel_bias, attn_w_o=attn_w_o)
    ms = dict(w_ada=m_w_ada, b_ada=m_b_ada, ln_g=m_ln_g, ln_b=m_ln_b, ffn_gu=m_ffn_gu, ffn_down=m_ffn_down,
              gmlp_w_in=m_gmlp_w_in, gmlp_b_in=m_gmlp_b_in, gmlp_ln_g=m_gmlp_ln_g, gmlp_ln_b=m_gmlp_ln_b,
              gmlp_w_s=m_gmlp_w_s, gmlp_b_s=m_gmlp_b_s, gmlp_w_out=m_gmlp_w_out, w_ada_kv=m_w_ada_kv,
              b_ada_kv=m_b_ada_kv, w_kv=m_w_kv, attn_w_q=m_attn_w_q, attn_rel_bias=m_attn_rel_bias,
              attn_w_o=m_attn_w_o)
    vs = dict(w_ada=v_w_ada, b_ada=v_b_ada, ln_g=v_ln_g, ln_b=v_ln_b, ffn_gu=v_ffn_gu, ffn_down=v_ffn_down,
              gmlp_w_in=v_gmlp_w_in, gmlp_b_in=v_gmlp_b_in, gmlp_ln_g=v_gmlp_ln_g, gmlp_ln_b=v_gmlp_ln_b,
              gmlp_w_s=v_gmlp_w_s, gmlp_b_s=v_gmlp_b_s, gmlp_w_out=v_gmlp_w_out, w_ada_kv=v_w_ada_kv,
              b_ada_kv=v_b_ada_kv, w_kv=v_w_kv, attn_w_q=v_attn_w_q, attn_rel_bias=v_attn_rel_bias,
              attn_w_o=v_attn_w_o)
    order = ["w_ada", "b_ada", "ln_g", "ln_b", "ffn_gu", "ffn_down", "gmlp_w_in", "gmlp_b_in", "gmlp_ln_g",
             "gmlp_ln_b", "gmlp_w_s", "gmlp_b_s", "gmlp_w_out", "w_ada_kv", "b_ada_kv", "w_kv", "attn_w_q",
             "attn_rel_bias", "attn_w_o"]
    big_names = ["w_ada", "w_ada_kv"] + stack_names
    small_names = [nm for nm in order if nm not in big_names]
    delta, new_m, new_v = {}, {}, {}

    def adamw_big(nm):
        shp = weights[nm].shape
        two_d = (-1, shp[-1])
        d, a, b = adamw(weights[nm].reshape(two_d), grads[nm].reshape(two_d), ms[nm].reshape(two_d),
                        vs[nm].reshape(two_d), "adamw")
        delta[nm], new_m[nm], new_v[nm] = d.reshape(shp), a.reshape(shp), b.reshape(shp)

    adamw_big("w_ada")
    adamw_big("w_ada_kv")
    shapes = [weights[nm].shape for nm in small_names]
    d, a, b = adamw(_pack_rows([weights[nm] for nm in small_names]), _pack_rows([grads[nm] for nm in small_names]),
                    _pack_rows([ms[nm] for nm in small_names]), _pack_rows([vs[nm] for nm in small_names]),
                    "adamw_small")
    for nm, dd, aa, bb in zip(small_names, _unpack_rows(d, shapes), _unpack_rows(a, shapes), _unpack_rows(b, shapes)):
        delta[nm], new_m[nm], new_v[nm] = dd, aa, bb

    def full_grad(u):
        lo = jnp.where(ci == 0, own_half[u], sib_half[u])
        hi = jnp.where(ci == 0, sib_half[u], own_half[u])
        return jnp.concatenate([lo, hi], axis=1 if u in col_split else 0)

    def adamw_stack(nm):
        si = stack_names.index(nm)
        g = jnp.stack([full_grad(unit_of[(nm, k)]) for k in range(stacks[si].shape[0])])
        grads[nm] = g.reshape(weights[nm].shape)
        adamw_big(nm)

    late = [stack_names[units[u][0]] for u in last["ids"]]
    early = [nm for nm in stack_names if nm not in late]
    finish_grad_exchange(pending, delta["w_ada"])
    collect_halves(delta["w_ada"])
    for nm in early:
        adamw_stack(nm)
    finish_grad_exchange(last, delta[early[-1]])
    collect_halves(delta[early[-1]])
    for nm in stack_names:
        if nm in late:
            adamw_stack(nm)

    return (loss, grad_x, *[grads[nm] for nm in order], *[delta[nm] for nm in order],
            *[new_m[nm] for nm in order], *[new_v[nm] for nm in order])
```

```python
import functools

import jax
import jax.numpy as jnp
from jax import lax
from jax.experimental import pallas as pl
from jax.experimental.pallas import tpu as pltpu

F32 = jnp.float32
BF16 = jnp.bfloat16
MESH = pl.DeviceIdType.MESH
HIGHEST = lax.Precision.HIGHEST

CHUNK = 64
GMLP_WINDOW = 128
GMLP_GROUPS = 8
HEAD_DIM = 64
LEFT_CHUNKS = 8
BAND = (LEFT_CHUNKS + 1) * CHUNK
LEFT_PAD = LEFT_CHUNKS * CHUNK
MAX_REL = 4 * CHUNK
N_REL = (CHUNK - 1) + MAX_REL + 1
LN_EPS = 1e-5
N_MOD = 9
N_DEV = 8
N_CHIP = 4

ADAM_LR = 0.001
ADAM_B1 = 0.9
ADAM_B2 = 0.999
ADAM_EPS = 1e-08
ADAM_WD = 0.01
ADAM_STEP = 10

LANES = 128
ROW_TILE = 256
MATMUL_ROW_TILE = 512
WGRAD_ROWS = 2048
ATTN_CHUNKS_PER_STEP = 8
VMEM_LIMIT_MB = 56

NT = (((1,), (1,)), ((), ()))
TN = (((0,), (0,)), ((), ()))

ANY = pl.BlockSpec(memory_space=pl.ANY)
VMEM_SPEC = pl.BlockSpec(memory_space=pltpu.VMEM)


def _params(semantics=None):
    kw = dict(vmem_limit_bytes=VMEM_LIMIT_MB * 1024 * 1024)
    if semantics is not None:
        kw["dimension_semantics"] = semantics
    return pltpu.CompilerParams(**kw)


def _sigmoid(v):
    return 0.5 * (1.0 + jnp.tanh(0.5 * v))


def _gelu(v):
    return 0.5 * v * (1.0 + lax.erf(v * (2.0 ** -0.5)))


def _gelu_grad(v):
    return 0.5 * (1.0 + lax.erf(v * (2.0 ** -0.5))) + v * jnp.exp(-0.5 * v * v) * ((2.0 * jnp.pi) ** -0.5)


def _row(m):
    return lambda i: (i, 0)


def _fixed2(i):
    return (0, 0)


def _fixed3(i):
    return (0, 0, 0)


def _resident(shape):
    return pl.BlockSpec(shape, _fixed2 if len(shape) == 2 else _fixed3, pipeline_mode=pl.Buffered(1))


def mod_matmul(x, scl, shift, w, bias, out_dtype, name):
    S, D = x.shape
    NS, _, n = w.shape
    tm = min(MATMUL_ROW_TILE, S)
    has_bias = bias is not None

    def body(*refs):
        if has_bias:
            x_ref, scl_ref, sh_ref, w_ref, b_ref, o_ref, h_ref = refs
        else:
            x_ref, scl_ref, sh_ref, w_ref, o_ref, h_ref = refs
        h = (x_ref[...] * (1.0 + scl_ref[...]) + sh_ref[...]).astype(BF16)
        h_ref[...] = h
        for s in range(NS):
            acc = jnp.dot(h, w_ref[s], preferred_element_type=F32)
            if has_bias:
                acc = acc + b_ref[:, s * n:(s + 1) * n]
            o_ref[:, s * n:(s + 1) * n] = acc.astype(out_dtype)

    in_specs = [pl.BlockSpec((tm, D), _row(0)), pl.BlockSpec((1, D), _fixed2), pl.BlockSpec((1, D), _fixed2),
                _resident((NS, D, n))]
    args = [x, scl, shift, w]
    if has_bias:
        in_specs.append(pl.BlockSpec((1, NS * n), _fixed2))
        args.append(bias)
    return pl.pallas_call(
        body, name=name, grid=(S // tm,), in_specs=in_specs,
        out_specs=[pl.BlockSpec((tm, NS * n), _row(0)), pl.BlockSpec((tm, D), _row(0))],
        out_shape=[jax.ShapeDtypeStruct((S, NS * n), out_dtype), jax.ShapeDtypeStruct((S, D), BF16)],
        compiler_params=_params(("parallel",)),
    )(*args)


def matmul_res_ln(a, w, x, gw, lg, lb, alpha, swiglu, name):
    S, D = x.shape
    K = w.shape[0]
    tm = min(ROW_TILE, S)
    ka = a.shape[1]

    def body(a_ref, w_ref, x_ref, gw_ref, lg_ref, lb_ref, xn_ref, xh_ref, rs_ref, y_ref, *act_ref):
        if swiglu:
            g = a_ref[:, :K].astype(F32)
            u = a_ref[:, K:].astype(F32)
            act = (g * _sigmoid(g) * u).astype(BF16)
            act_ref[0][...] = act
        else:
            act = a_ref[...].astype(BF16)
        y = jnp.dot(act, w_ref[...], preferred_element_type=F32)
        z = alpha * x_ref[...] + gw_ref[...] * y
        mu = jnp.mean(z, axis=-1, keepdims=True)
        zc = z - mu
        var = jnp.mean(zc * zc, axis=-1, keepdims=True)
        rstd = lax.rsqrt(var + LN_EPS)
        xhat = zc * rstd
        xn_ref[...] = xhat * lg_ref[...] + lb_ref[...]
        xh_ref[...] = xhat
        rs_ref[...] = rstd
        y_ref[...] = y.astype(BF16)

    vec = pl.BlockSpec((1, D), _fixed2)
    out_specs = [pl.BlockSpec((tm, D), _row(0)), pl.BlockSpec((tm, D), _row(0)), pl.BlockSpec((tm, 1), _row(0)),
                 pl.BlockSpec((tm, D), _row(0))]
    out_shape = [jax.ShapeDtypeStruct((S, D), F32), jax.ShapeDtypeStruct((S, D), F32),
                 jax.ShapeDtypeStruct((S, 1), F32), jax.ShapeDtypeStruct((S, D), BF16)]
    if swiglu:
        out_specs.append(pl.BlockSpec((tm, K), _row(0)))
        out_shape.append(jax.ShapeDtypeStruct((S, K), BF16))
    return pl.pallas_call(
        body, name=name, grid=(S // tm,),
        in_specs=[pl.BlockSpec((tm, ka), _row(0)), _resident((K, D)), pl.BlockSpec((tm, D), _row(0)),
                  vec, vec, vec],
        out_specs=out_specs, out_shape=out_shape,
        compiler_params=_params(("parallel",)),
    )(a, w, x, gw, lg, lb)


def _ln_res_bwd_tile(d, xh_ref, rs_ref, lg_ref, y_ref, gw_ref, wres, alpha, dxa_ref, dy_ref, acc_ref, row0):
    xh = xh_ref[...]
    dxh = d * lg_ref[...]
    m1 = jnp.mean(dxh, axis=-1, keepdims=True)
    m2 = jnp.mean(dxh * xh, axis=-1, keepdims=True)
    dz = rs_ref[...] * (dxh - m1 - xh * m2)
    dxa_ref[...] = alpha * dz
    dy_ref[...] = (gw_ref[...] * dz).astype(BF16)
    acc_ref[row0:row0 + 1, :] += jnp.sum(d * xh, axis=0, keepdims=True)
    acc_ref[row0 + 1:row0 + 2, :] += jnp.sum(d, axis=0, keepdims=True)
    acc_ref[row0 + 2:row0 + 3, :] += jnp.sum((wres * dz) * y_ref[...].astype(F32), axis=0, keepdims=True)


def ln_res_bwd(dxn, xhat, rstd, lg, y, gw, wres, alpha, name):
    S, D = dxn.shape
    tm = min(MATMUL_ROW_TILE, S)

    def body(dxn_ref, xh_ref, rs_ref, lg_ref, y_ref, gw_ref, dxa_ref, dy_ref, acc_ref):
        @pl.when(pl.program_id(0) == 0)
        def _():
            acc_ref[...] = jnp.zeros_like(acc_ref)

        _ln_res_bwd_tile(dxn_ref[...], xh_ref, rs_ref, lg_ref, y_ref, gw_ref, wres, alpha, dxa_ref, dy_ref, acc_ref, 0)

    vec = pl.BlockSpec((1, D), _fixed2)
    tile = pl.BlockSpec((tm, D), _row(0))
    return pl.pallas_call(
        body, name=name, grid=(S // tm,),
        in_specs=[tile, tile, pl.BlockSpec((tm, 1), _row(0)), vec, tile, vec],
        out_specs=[tile, tile, pl.BlockSpec((8, D), _fixed2)],
        out_shape=[jax.ShapeDtypeStruct((S, D), F32), jax.ShapeDtypeStruct((S, D), BF16),
                   jax.ShapeDtypeStruct((8, D), F32)],
        compiler_params=_params(("arbitrary",)),
    )(dxn, xhat, rstd, lg, y, gw)


def ffn_act_bwd(dy, wd, gu, after, name):
    S, D = dy.shape
    K = wd.shape[0]
    tm = min(ROW_TILE, S)

    def body(dy_ref, wd_ref, gu_ref, after_ref, o_ref):
        da = lax.dot_general(dy_ref[...], wd_ref[...], NT, preferred_element_type=F32).astype(BF16)
        g = gu_ref[:, :K]
        u = gu_ref[:, K:]
        sg = _sigmoid(g)
        o_ref[:, :K] = da * u * (sg * (1.0 + g * (1.0 - sg)))
        o_ref[:, K:] = da * (g * sg)

    return pl.pallas_call(
        body, name=name, grid=(S // tm,),
        in_specs=[pl.BlockSpec((tm, D), _row(0)), _resident((K, D)), pl.BlockSpec((tm, 2 * K), _row(0)), ANY],
        out_specs=pl.BlockSpec((tm, 2 * K), _row(0)),
        out_shape=jax.ShapeDtypeStruct((S, 2 * K), BF16),
        compiler_params=_params(("parallel",)),
    )(dy, wd, gu, after)


def matmul_nt(a, w, after, name):
    S, D = a.shape
    K = w.shape[0]
    tm = min(MATMUL_ROW_TILE, S)

    def body(a_ref, w_ref, after_ref, o_ref):
        o_ref[...] = lax.dot_general(a_ref[...], w_ref[...], NT, preferred_element_type=F32).astype(BF16)

    return pl.pallas_call(
        body, name=name, grid=(S // tm,),
        in_specs=[pl.BlockSpec((tm, D), _row(0)), _resident((K, D)), ANY],
        out_specs=pl.BlockSpec((tm, K), _row(0)),
        out_shape=jax.ShapeDtypeStruct((S, K), BF16),
        compiler_params=_params(("parallel",)),
    )(a, w, after)


def dgrad_mod(dpre, w, dxa, xin, scl, prev, alpha, name):
    S, D = xin.shape
    NS, _, n = w.shape
    tm = min(ROW_TILE, S)
    wres = prev[5] if prev is not None else None

    def body(*refs):
        dp_ref, w_ref, dxa_ref, xin_ref, scl_ref = refs[:5]
        acc_ref = refs[-1]

        @pl.when(pl.program_id(0) == 0)
        def _():
            acc_ref[...] = jnp.zeros_like(acc_ref)

        dh = jnp.zeros((tm, D), F32)
        for s in range(NS):
            dh = dh + lax.dot_general(dp_ref[:, s * n:(s + 1) * n].astype(BF16), w_ref[s], NT,
                                      preferred_element_type=F32)
        dx = dxa_ref[...] + dh * (1.0 + scl_ref[...])
        acc_ref[0:1, :] += jnp.sum(dh * xin_ref[...], axis=0, keepdims=True)
        acc_ref[1:2, :] += jnp.sum(dh, axis=0, keepdims=True)
        if prev is None:
            refs[5][...] = dx
        else:
            xh_ref, rs_ref, lg_ref, y_ref, gw_ref, pdxa_ref, pdy_ref = refs[5:12]
            _ln_res_bwd_tile(dx, xh_ref, rs_ref, lg_ref, y_ref, gw_ref, wres, alpha, pdxa_ref, pdy_ref, acc_ref, 2)

    tile = pl.BlockSpec((tm, D), _row(0))
    vec = pl.BlockSpec((1, D), _fixed2)
    in_specs = [pl.BlockSpec((tm, NS * n), _row(0)), _resident((NS, D, n)), tile, tile, vec]
    args = [dpre, w, dxa, xin, scl]
    if prev is None:
        out_specs = [tile]
        out_shape = [jax.ShapeDtypeStruct((S, D), F32)]
    else:
        in_specs += [tile, pl.BlockSpec((tm, 1), _row(0)), vec, tile, vec]
        args += list(prev[:5])
        out_specs = [tile, tile]
        out_shape = [jax.ShapeDtypeStruct((S, D), F32), jax.ShapeDtypeStruct((S, D), BF16)]
    return pl.pallas_call(
        body, name=name, grid=(S // tm,), in_specs=in_specs,
        out_specs=out_specs + [pl.BlockSpec((8, D), _fixed2)],
        out_shape=out_shape + [jax.ShapeDtypeStruct((8, D), F32)],
        compiler_params=_params(("arbitrary",)),
    )(*args)


PAIR_COLLECTIVE_ID = 1


def wgrad_pair(a, b, J, kb, nb, a_block, b_block, half_idx, name):
    S = b.shape[0]
    ts = min(WGRAD_ROWS, S)
    nsteps = S // ts

    def body(h_ref, a_ref, b_ref, o_ref, acc_ref, send_buf, recv_buf, send_sems, recv_sems):
        jj, si = pl.program_id(0), pl.program_id(1)
        x, y, c = _coords()
        j = lax.rem(jj, J)
        last = si == nsteps - 1

        def copy(blk):
            return pltpu.make_async_remote_copy(
                src_ref=send_buf.at[blk], dst_ref=recv_buf.at[blk], send_sem=send_sems.at[blk],
                recv_sem=recv_sems.at[blk], device_id=(x, y, 1 - c), device_id_type=MESH)

        @pl.when(jnp.logical_and(jj == 0, si == 0))
        def _():
            barrier = pltpu.get_barrier_semaphore()
            pl.semaphore_signal(barrier, inc=1, device_id=(x, y, 1 - c), device_id_type=MESH)
            pl.semaphore_wait(barrier, 1)

        @pl.when(si == 0)
        def _():
            acc_ref[...] = jnp.zeros_like(acc_ref)

        acc_ref[...] += lax.dot_general(a_ref[...], b_ref[...].astype(BF16), TN, preferred_element_type=F32)

        @pl.when(jnp.logical_and(last, jj < J))
        def _():
            send_buf[j] = acc_ref[...].astype(BF16)
            copy(j).start()

        @pl.when(jnp.logical_and(last, jj >= J))
        def _():
            copy(j).wait_recv()
            o_ref[...] = (acc_ref[...] + recv_buf[j].astype(F32)).astype(BF16)

        @pl.when(jnp.logical_and(last, jj == 2 * J - 1))
        def _():
            for blk in range(J):
                copy(blk).wait_send()

    def half(jj, h):
        return jnp.where(jj < J, 1 - h[0], h[0])

    return pl.pallas_call(
        body, name=name,
        grid_spec=pltpu.PrefetchScalarGridSpec(
            num_scalar_prefetch=1, grid=(2 * J, nsteps),
            in_specs=[pl.BlockSpec((ts, kb), lambda jj, s, h: (s, a_block(lax.rem(jj, J), half(jj, h)))),
                      pl.BlockSpec((ts, nb), lambda jj, s, h: (s, b_block(lax.rem(jj, J), half(jj, h))))],
            out_specs=pl.BlockSpec((None, kb, nb), lambda jj, s, h: (jnp.maximum(jj - J, 0), 0, 0)),
            scratch_shapes=[pltpu.VMEM((kb, nb), F32), pltpu.VMEM((J, kb, nb), BF16), pltpu.VMEM((J, kb, nb), BF16),
                            pltpu.SemaphoreType.DMA((J,)), pltpu.SemaphoreType.DMA((J,))]),
        out_shape=jax.ShapeDtypeStruct((J, kb, nb), BF16),
        compiler_params=pltpu.CompilerParams(
            vmem_limit_bytes=VMEM_LIMIT_MB * 1024 * 1024, dimension_semantics=("arbitrary", "arbitrary"),
            collective_id=PAIR_COLLECTIVE_ID),
    )(half_idx, a, b)


def _window_mask():
    t = lax.broadcasted_iota(jnp.int32, (GMLP_WINDOW, GMLP_WINDOW), 0)
    s = lax.broadcasted_iota(jnp.int32, (GMLP_WINDOW, GMLP_WINDOW), 1)
    return ((s // CHUNK) <= (t // CHUNK)).astype(F32)


def sgu_fwd(pre, glg, glb, ws, bst, name):
    S, H2 = pre.shape
    H = H2 // 2
    W, G = GMLP_WINDOW, GMLP_GROUPS
    gd = H // G
    tm = min(ROW_TILE, S)

    def body(pre_ref, glg_ref, glb_ref, ws_ref, bst_ref, q_ref):
        u = _gelu(pre_ref[:, :H])
        v = _gelu(pre_ref[:, H:])
        mu = jnp.mean(v, axis=-1, keepdims=True)
        vc = v - mu
        var = jnp.mean(vc * vc, axis=-1, keepdims=True)
        vn = ((vc * lax.rsqrt(var + LN_EPS)) * glg_ref[...] + glb_ref[...]).astype(BF16)
        mask = _window_mask()
        for g in range(G):
            wsg = (ws_ref[g] * mask).astype(BF16)
            bcol = bst_ref[:, g:g + 1]
            for wi in range(tm // W):
                rows = slice(wi * W, (wi + 1) * W)
                cols = slice(g * gd, (g + 1) * gd)
                s = jnp.dot(wsg, vn[rows, cols], preferred_element_type=F32) + bcol
                q_ref[rows, cols] = (u[rows, cols] * s).astype(BF16)

    return pl.pallas_call(
        body, name=name, grid=(S // tm,),
        in_specs=[pl.BlockSpec((tm, H2), _row(0)), pl.BlockSpec((1, H), _fixed2), pl.BlockSpec((1, H), _fixed2),
                  pl.BlockSpec((G, W, W), _fixed3), pl.BlockSpec((W, G), _fixed2)],
        out_specs=pl.BlockSpec((tm, H), _row(0)),
        out_shape=jax.ShapeDtypeStruct((S, H), BF16),
        compiler_params=_params(("parallel",)),
    )(pre, glg, glb, ws, bst)


def sgu_bwd(dq, pre, glg, glb, ws, bst, name):
    S, H2 = pre.shape
    H = H2 // 2
    W, G = GMLP_WINDOW, GMLP_GROUPS
    gd = H // G
    tm = min(ROW_TILE, S)

    def body(dq_ref, pre_ref, glg_ref, glb_ref, ws_ref, bst_ref,
             dpre_ref, dws_ref, dss_ref, dgl_ref, dbin_ref, du_s, dvn_s):
        @pl.when(pl.program_id(0) == 0)
        def _():
            dws_ref[...] = jnp.zeros_like(dws_ref)
            dss_ref[...] = jnp.zeros_like(dss_ref)
            dgl_ref[...] = jnp.zeros_like(dgl_ref)
            dbin_ref[...] = jnp.zeros_like(dbin_ref)

        pu = pre_ref[:, :H]
        pv = pre_ref[:, H:]
        u = _gelu(pu)
        v = _gelu(pv)
        mu = jnp.mean(v, axis=-1, keepdims=True)
        vc = v - mu
        var = jnp.mean(vc * vc, axis=-1, keepdims=True)
        rstd = lax.rsqrt(var + LN_EPS)
        vhat = vc * rstd
        vn = (vhat * glg_ref[...] + glb_ref[...]).astype(BF16)
        mask = _window_mask()
        for g in range(G):
            wsg = (ws_ref[g] * mask).astype(BF16)
            bcol = bst_ref[:, g:g + 1]
            cols = slice(g * gd, (g + 1) * gd)
            for wi in range(tm // W):
                rows = slice(wi * W, (wi + 1) * W)
                vblk = vn[rows, cols]
                s = jnp.dot(wsg, vblk, preferred_element_type=F32) + bcol
                dqb = dq_ref[rows, cols].astype(F32)
                du_s[rows, cols] = dqb * s
                ds = dqb * u[rows, cols]
                dss_ref[:, cols] += ds
                dsb = ds.astype(BF16)
                dvn_s[rows, cols] = lax.dot_general(wsg, dsb, TN, preferred_element_type=F32)
                dws_ref[g] += lax.dot_general(dsb, vblk, NT, preferred_element_type=F32) * mask
        dvn = dvn_s[...]
        dgl_ref[0:1, :] += jnp.sum(dvn * vhat, axis=0, keepdims=True)
        dgl_ref[1:2, :] += jnp.sum(dvn, axis=0, keepdims=True)
        dvh = dvn * glg_ref[...]
        m1 = jnp.mean(dvh, axis=-1, keepdims=True)
        m2 = jnp.mean(dvh * vhat, axis=-1, keepdims=True)
        dv = rstd * (dvh - m1 - vhat * m2)
        dpu = du_s[...] * _gelu_grad(pu)
        dpv = dv * _gelu_grad(pv)
        dbin_ref[0:1, :H] += jnp.sum(dpu, axis=0, keepdims=True)
        dbin_ref[0:1, H:] += jnp.sum(dpv, axis=0, keepdims=True)
        dpre_ref[:, :H] = dpu.astype(BF16)
        dpre_ref[:, H:] = dpv.astype(BF16)

    return pl.pallas_call(
        body, name=name, grid=(S // tm,),
        in_specs=[pl.BlockSpec((tm, H), _row(0)), pl.BlockSpec((tm, H2), _row(0)), pl.BlockSpec((1, H), _fixed2),
                  pl.BlockSpec((1, H), _fixed2), pl.BlockSpec((G, W, W), _fixed3), pl.BlockSpec((W, G), _fixed2)],
        out_specs=[pl.BlockSpec((tm, H2), _row(0)), pl.BlockSpec((G, W, W), _fixed3), pl.BlockSpec((W, H), _fixed2),
                   pl.BlockSpec((8, H), _fixed2), pl.BlockSpec((8, H2), _fixed2)],
        out_shape=[jax.ShapeDtypeStruct((S, H2), BF16), jax.ShapeDtypeStruct((G, W, W), F32),
                   jax.ShapeDtypeStruct((W, H), F32), jax.ShapeDtypeStruct((8, H), F32),
                   jax.ShapeDtypeStruct((8, H2), F32)],
        scratch_shapes=[pltpu.VMEM((tm, H), F32), pltpu.VMEM((tm, H), F32)],
        compiler_params=_params(("arbitrary",)),
    )(dq, pre, glg, glb, ws, bst)


def group_lane_sum(dss, name):
    W, H = dss.shape
    gd = H // GMLP_GROUPS

    def body(d_ref, o_ref):
        j = lax.broadcasted_iota(jnp.int32, (H, LANES), 0)
        g = lax.broadcasted_iota(jnp.int32, (H, LANES), 1)
        ind = ((j // gd) == g).astype(F32)
        o_ref[...] = jnp.dot(d_ref[...], ind, preferred_element_type=F32, precision=HIGHEST)

    return pl.pallas_call(
        body, name=name, in_specs=[VMEM_SPEC], out_specs=VMEM_SPEC,
        out_shape=jax.ShapeDtypeStruct((W, LANES), F32), compiler_params=_params(),
    )(dss)


def _attn_load(j, cps, q_ref, k_ref, v_ref):
    r = lax.broadcasted_iota(jnp.int32, (CHUNK, BAND), 1)
    chunks = []
    for cc in range(cps):
        start = pl.multiple_of((j * cps + cc) * CHUNK, CHUNK)
        chunks.append((q_ref[cc * CHUNK:(cc + 1) * CHUNK, :], k_ref[pl.ds(start, BAND), :],
                       v_ref[pl.ds(start, BAND), :], (r + start) >= LEFT_PAD))
    return chunks


def _attn_probs(chunks, b_ref, sels, scale):
    qms = [[jnp.where(sel, q2, jnp.zeros_like(q2)) for sel in sels] for q2, _, _, _ in chunks]
    raw = [[lax.dot_general(qm, k2, NT, preferred_element_type=F32) for qm in qms[cc]]
           for cc, (_, k2, _, _) in enumerate(chunks)]
    probs = []
    for cc, (_, _, _, valid) in enumerate(chunks):
        row = []
        for sub in range(2):
            s = jnp.where(valid, raw[cc][sub] * scale + b_ref[sub], -jnp.inf)
            e = jnp.exp(s - jnp.max(s, axis=-1, keepdims=True))
            row.append(e / jnp.sum(e, axis=-1, keepdims=True))
        probs.append(row)
    return qms, probs


def attn_fwd(q, kpad, vpad, bias, name):
    S, D = q.shape
    HP = D // LANES
    cps = min(ATTN_CHUNKS_PER_STEP, S // CHUNK)
    tq = cps * CHUNK
    scale = HEAD_DIM ** -0.5

    def body(q_ref, k_ref, v_ref, b_ref, o_ref):
        sel0 = lax.broadcasted_iota(jnp.int32, (CHUNK, LANES), 1) < HEAD_DIM
        chunks = _attn_load(pl.program_id(1), cps, q_ref, k_ref, v_ref)
        _, probs = _attn_probs(chunks, b_ref, (sel0, jnp.logical_not(sel0)), scale)
        outs = [[jnp.dot(probs[cc][sub].astype(BF16), v2, preferred_element_type=F32) for sub in range(2)]
                for cc, (_, _, v2, _) in enumerate(chunks)]
        o_ref[...] = jnp.concatenate([jnp.where(sel0, o[0], o[1]) for o in outs], axis=0).astype(BF16)

    kv_spec = pl.BlockSpec((S + LEFT_PAD, LANES), lambda h, j: (0, h))
    return pl.pallas_call(
        body, name=name, grid=(HP, S // tq),
        in_specs=[pl.BlockSpec((tq, LANES), lambda h, j: (j, h)), kv_spec, kv_spec,
                  pl.BlockSpec((2, CHUNK, BAND), lambda h, j: (h, 0, 0))],
        out_specs=pl.BlockSpec((tq, LANES), lambda h, j: (j, h)),
        out_shape=jax.ShapeDtypeStruct((S, D), BF16),
        compiler_params=_params(("parallel", "parallel")),
    )(q, kpad, vpad, bias)


def attn_bwd(q, do, kpad, vpad, bias, dk_in, dv_in, name):
    S, D = q.shape
    HP = D // LANES
    NH = 2 * HP
    cps = min(ATTN_CHUNKS_PER_STEP, S // CHUNK)
    tq = cps * CHUNK
    nj = S // tq
    scale = HEAD_DIM ** -0.5

    def body(q_ref, do_ref, k_ref, v_ref, b_ref, dki_ref, dvi_ref, dq_ref, dk_ref, dv_ref, db_ref, dk_acc, dv_acc):
        j = pl.program_id(1)

        @pl.when(j == 0)
        def _():
            dk_acc[:LEFT_PAD, :] = jnp.zeros((LEFT_PAD, LANES), F32)
            dv_acc[:LEFT_PAD, :] = jnp.zeros((LEFT_PAD, LANES), F32)
            dk_acc[LEFT_PAD:, :] = dki_ref[...]
            dv_acc[LEFT_PAD:, :] = dvi_ref[...]
            db_ref[...] = jnp.zeros_like(db_ref)

        sel0 = lax.broadcasted_iota(jnp.int32, (CHUNK, LANES), 1) < HEAD_DIM
        sels = (sel0, jnp.logical_not(sel0))
        chunks = _attn_load(j, cps, q_ref, k_ref, v_ref)
        pairs = [(cc, sub) for cc in range(cps) for sub in range(2)]
        qms, probs = _attn_probs(chunks, b_ref, sels, scale)
        doms = [[jnp.where(sel, do_ref[cc * CHUNK:(cc + 1) * CHUNK, :], jnp.zeros((CHUNK, LANES), BF16))
                 for sel in sels] for cc in range(cps)]
        dps = {(cc, sub): lax.dot_general(doms[cc][sub], chunks[cc][2], NT, preferred_element_type=F32)
               for cc, sub in pairs}
        dss = {}
        for cc, sub in pairs:
            p = probs[cc][sub]
            dss[cc, sub] = p * (dps[cc, sub] - jnp.sum(dps[cc, sub] * p, axis=-1, keepdims=True))
        dsb = {key: ds.astype(BF16) for key, ds in dss.items()}
        dqs = {(cc, sub): jnp.dot(dsb[cc, sub], chunks[cc][1], preferred_element_type=F32) * scale
               for cc, sub in pairs}
        dks = {(cc, sub): lax.dot_general(dsb[cc, sub], qms[cc][sub], TN, preferred_element_type=F32) * scale
               for cc, sub in pairs}
        dvs = {(cc, sub): lax.dot_general(probs[cc][sub].astype(BF16), doms[cc][sub], TN,
                                          preferred_element_type=F32) for cc, sub in pairs}
        dq_ref[...] = jnp.concatenate([jnp.where(sel0, dqs[cc, 0], dqs[cc, 1]) for cc in range(cps)],
                                      axis=0).astype(BF16)
        for sub in range(2):
            total = dss[0, sub]
            for cc in range(1, cps):
                total = total + dss[cc, sub]
            db_ref[sub] += total
        dk_parts = [dks[cc, 0] + dks[cc, 1] for cc in range(cps)]
        dv_parts = [dvs[cc, 0] + dvs[cc, 1] for cc in range(cps)]

        def window(parts):
            blocks = []
            for rb in range(cps - 1 + BAND // CHUNK):
                acc = None
                for cc in range(cps):
                    b = rb - cc
                    if 0 <= b < BAND // CHUNK:
                        piece = parts[cc][b * CHUNK:(b + 1) * CHUNK, :]
                        acc = piece if acc is None else acc + piece
                blocks.append(acc)
            return jnp.concatenate(blocks, axis=0)

        span = pl.ds(pl.multiple_of(j * cps * CHUNK, CHUNK), (cps - 1) * CHUNK + BAND)
        dk_acc[span, :] += window(dk_parts)
        dv_acc[span, :] += window(dv_parts)

        @pl.when(j == nj - 1)
        def _():
            dk_ref[...] = dk_acc[LEFT_PAD:, :]
            dv_ref[...] = dv_acc[LEFT_PAD:, :]

    q_spec = pl.BlockSpec((tq, LANES), lambda h, j: (j, h))
    kv_spec = pl.BlockSpec((S + LEFT_PAD, LANES), lambda h, j: (0, h))
    col_spec = pl.BlockSpec((S, LANES), lambda h, j: (0, h))
    b_spec = pl.BlockSpec((2, CHUNK, BAND), lambda h, j: (h, 0, 0))
    return pl.pallas_call(
        body, name=name, grid=(HP, nj),
        in_specs=[q_spec, q_spec, kv_spec, kv_spec, b_spec, col_spec, col_spec],
        out_specs=[q_spec, col_spec, col_spec, b_spec],
        out_shape=[jax.ShapeDtypeStruct((S, D), BF16), jax.ShapeDtypeStruct((S, D), F32),
                   jax.ShapeDtypeStruct((S, D), F32), jax.ShapeDtypeStruct((NH, CHUNK, BAND), F32)],
        scratch_shapes=[pltpu.VMEM((S + LEFT_PAD, LANES), F32), pltpu.VMEM((S + LEFT_PAD, LANES), F32)],
        compiler_params=_params(("parallel", "arbitrary")),
    )(q, do, kpad, vpad, bias, dk_in, dv_in)


def _rel_onehot(t):
    r = lax.broadcasted_iota(jnp.int32, (BAND, N_REL), 0)
    i = lax.broadcasted_iota(jnp.int32, (BAND, N_REL), 1)
    idx = jnp.clip(t + LEFT_PAD - r, -(CHUNK - 1), MAX_REL) + (CHUNK - 1)
    return (idx == i).astype(BF16)


def _split3(v):
    hi = v.astype(BF16)
    rest = v - hi.astype(F32)
    mid = rest.astype(BF16)
    return hi, mid, (rest - mid.astype(F32)).astype(BF16)


def _dot_onehot(parts, onehot, dims):
    hi, mid, lo = [lax.dot_general(p, onehot, dims, preferred_element_type=F32) for p in parts]
    return (hi + mid) + lo


def bias_expand(rb, name):
    NH = rb.shape[0]

    def body(rb_ref, o_ref):
        parts = _split3(rb_ref[...])

        def step(t, carry):
            o_ref[t] = _dot_onehot(parts, _rel_onehot(t), NT)
            return carry

        lax.fori_loop(0, CHUNK, step, 0)

    return pl.pallas_call(
        body, name=name, in_specs=[VMEM_SPEC], out_specs=VMEM_SPEC,
        out_shape=jax.ShapeDtypeStruct((CHUNK, NH, BAND), F32), compiler_params=_params(),
    )(rb)


def bias_grad(dsum, name):
    NH = dsum.shape[1]

    def body(d_ref, o_ref):
        def step(t, acc):
            return acc + _dot_onehot(_split3(d_ref[t]), _rel_onehot(t), (((1,), (0,)), ((), ())))

        o_ref[...] = lax.fori_loop(0, CHUNK, step, jnp.zeros((NH, N_REL), F32))

    return pl.pallas_call(
        body, name=name, in_specs=[VMEM_SPEC], out_specs=VMEM_SPEC,
        out_shape=jax.ShapeDtypeStruct((NH, N_REL), F32), compiler_params=_params(),
    )(dsum)


def loss_grad(y, tgt, name):
    S, D = y.shape
    tm = min(MATMUL_ROW_TILE, S)

    def body(y_ref, t_ref, d_ref, acc_ref):
        @pl.when(pl.program_id(0) == 0)
        def _():
            acc_ref[...] = jnp.zeros_like(acc_ref)

        err = y_ref[...] - t_ref[...]
        d_ref[...] = err * (1.0 / D)
        acc_ref[0:1, :] += jnp.sum(err * err, axis=0, keepdims=True)

    tile = pl.BlockSpec((tm, D), _row(0))
    return pl.pallas_call(
        body, name=name, grid=(S // tm,), in_specs=[tile, tile],
        out_specs=[tile, pl.BlockSpec((8, D), _fixed2)],
        out_shape=[jax.ShapeDtypeStruct((S, D), F32), jax.ShapeDtypeStruct((8, D), F32)],
        compiler_params=_params(("arbitrary",)),
    )(y, tgt)


def _col_tile(n):
    for t in (768, 512, 256, 128):
        if n % t == 0:
            return t
    return n


def ada_fwd(c_all, w, b, name):
    L, D, n = w.shape
    tn = _col_tile(n)

    def body(c_ref, w_ref, b_ref, o_ref):
        cv = c_ref[...]
        ca = cv * _sigmoid(cv)
        o_ref[...] = jnp.dot(ca, w_ref[...], preferred_element_type=F32, precision=HIGHEST) + b_ref[...]

    return pl.pallas_call(
        body, name=name, grid=(L, n // tn),
        in_specs=[pl.BlockSpec((N_DEV, D), lambda l, j: (0, 0)), pl.BlockSpec((None, D, tn), lambda l, j: (l, 0, j)),
                  pl.BlockSpec((None, 1, tn), lambda l, j: (l, 0, j))],
        out_specs=pl.BlockSpec((None, N_DEV, tn), lambda l, j: (l, 0, j)),
        out_shape=jax.ShapeDtypeStruct((L, N_DEV, n), F32),
        compiler_params=_params(("parallel", "parallel")),
    )(c_all, w, b)


def ada_wgrad(c_all_t, dmod, name):
    L, _, n = dmod.shape
    D = c_all_t.shape[0]
    tn = _col_tile(n)

    def body(c_ref, d_ref, o_ref):
        cv = c_ref[...]
        ca = cv * _sigmoid(cv)
        o_ref[...] = jnp.dot(ca, d_ref[...], preferred_element_type=F32, precision=HIGHEST)

    return pl.pallas_call(
        body, name=name, grid=(L, n // tn),
        in_specs=[pl.BlockSpec((D, N_DEV), lambda l, j: (0, 0)), pl.BlockSpec((None, N_DEV, tn), lambda l, j: (l, 0, j))],
        out_specs=pl.BlockSpec((None, D, tn), lambda l, j: (l, 0, j)),
        out_shape=jax.ShapeDtypeStruct((L, D, n), F32),
        compiler_params=_params(("parallel", "parallel")),
    )(c_all_t, dmod)


ELEMENTWISE_BLOCK_BYTES = 3 * 1024 * 1024


def _elementwise_rows(rows, row_bytes):
    for t in (4096, 2048, 1024, 512, 256, 128, 64, 32, 16):
        if rows % t == 0 and t * row_bytes <= ELEMENTWISE_BLOCK_BYTES:
            return t
    return rows


def sum_leading(a, name):
    n, M, N = a.shape
    tr = _elementwise_rows(M, n * N * 4)

    def body(a_ref, o_ref):
        acc = a_ref[0]
        for i in range(1, n):
            acc = acc + a_ref[i]
        o_ref[...] = acc

    return pl.pallas_call(
        body, name=name, grid=(M // tr,),
        in_specs=[pl.BlockSpec((n, tr, N), lambda i: (0, i, 0))],
        out_specs=pl.BlockSpec((tr, N), _row(0)),
        out_shape=jax.ShapeDtypeStruct((M, N), F32),
        compiler_params=_params(("parallel",)),
    )(a)


def chip_sum(psum, land, chip_idx, transposed, name):
    n, M, N = psum.shape
    tr = M if transposed else _elementwise_rows(M, N * 8)

    def body(s_ref, p_ref, a_ref, b_ref, c_ref, o_ref):
        total = ((p_ref[...].astype(F32) + a_ref[...].astype(F32)) + b_ref[...].astype(F32)) + c_ref[...].astype(F32)
        o_ref[...] = jnp.transpose(total) if transposed else total

    def entry(j):
        return pl.BlockSpec((None, tr, N), lambda i, s: ((s[0] + j) % n, i, 0))

    out_block, out_dims = ((N, tr), (N, M)) if transposed else ((tr, N), (M, N))
    return pl.pallas_call(
        body, name=name,
        grid_spec=pltpu.PrefetchScalarGridSpec(
            num_scalar_prefetch=1, grid=(M // tr,),
            in_specs=[entry(0), entry(1), entry(2), entry(3)],
            out_specs=pl.BlockSpec(out_block, lambda i, s: (0, 0) if transposed else (i, 0))),
        out_shape=jax.ShapeDtypeStruct(out_dims, F32),
        compiler_params=_params(("parallel",)),
    )(chip_idx, psum, land, land, land)


def adamw(w, g, m, v, name):
    M, N = w.shape
    tr = _elementwise_rows(M, N * 4)
    c1 = 1.0 - ADAM_B1 ** ADAM_STEP
    c2 = 1.0 - ADAM_B2 ** ADAM_STEP

    def body(w_ref, g_ref, m_ref, v_ref, d_ref, nm_ref, nv_ref):
        g = g_ref[...]
        nm = ADAM_B1 * m_ref[...] + (1.0 - ADAM_B1) * g
        nv = ADAM_B2 * v_ref[...] + (1.0 - ADAM_B2) * (g * g)
        d_ref[...] = -ADAM_LR * ((nm / c1) / (jnp.sqrt(nv / c2) + ADAM_EPS) + ADAM_WD * w_ref[...])
        nm_ref[...] = nm
        nv_ref[...] = nv

    spec = pl.BlockSpec((tr, N), _row(0))
    shp = jax.ShapeDtypeStruct((M, N), F32)
    return pl.pallas_call(
        body, name=name, grid=(M // tr,), in_specs=[spec] * 4, out_specs=[spec] * 3, out_shape=[shp] * 3,
        compiler_params=_params(("parallel",)),
    )(w, g, m, v)


def _coords():
    return lax.axis_index("x"), lax.axis_index("y"), lax.axis_index("c")


def all_gather8(block, name):
    m_per, n = block.shape

    def body(x_ref, out_ref, send_sems, recv_sems, local_sem):
        x, y, c = _coords()
        me, sibling = (x, y, c), (x, y, 1 - c)
        chips = [(1 - x, y), (x, 1 - y), (1 - x, 1 - y)]

        def rows(px, py, pc):
            return out_ref.at[pl.ds((4 * px + 2 * py + pc) * m_per, m_per), :]

        def copy(k, blk, to, src=None):
            return pltpu.make_async_remote_copy(
                src_ref=rows(*blk) if src is None else src, dst_ref=rows(*blk),
                send_sem=send_sems.at[k], recv_sem=recv_sems.at[k], device_id=to, device_id_type=MESH)

        mine = pltpu.make_async_copy(x_ref, rows(*me), local_sem)
        mine.start()
        first = [copy(0, me, sibling, src=x_ref)]
        first += [copy(1 + j, me, (*chip, c), src=x_ref) for j, chip in enumerate(chips)]
        for cp in first:
            cp.start()
        passed = [copy(4 + j, (*chip, c), sibling) for j, chip in enumerate(chips)]
        for j, chip in enumerate(chips):
            copy(1 + j, (*chip, c), me).wait_recv()
            passed[j].start()
        copy(0, sibling, me).wait_recv()
        for j, chip in enumerate(chips):
            copy(4 + j, (*chip, 1 - c), me).wait_recv()
        for cp in first + passed:
            cp.wait_send()
        mine.wait()

    return pl.pallas_call(
        body, name=name, in_specs=[VMEM_SPEC], out_specs=VMEM_SPEC,
        out_shape=jax.ShapeDtypeStruct((N_DEV * m_per, n), block.dtype),
        scratch_shapes=[pltpu.SemaphoreType.DMA((7,)), pltpu.SemaphoreType.DMA((7,)), pltpu.SemaphoreType.DMA],
        compiler_params=_params(),
    )(block)


def _other_chips(x, y):
    return [(1 - x, y), (x, 1 - y), (1 - x, 1 - y)]


HBM_SPEC = pl.BlockSpec(memory_space=pltpu.HBM)
SEM_SPEC = pl.BlockSpec(memory_space=pltpu.SEMAPHORE)
DATAFLOW = pltpu.SideEffectType.DATAFLOW_SIDE_EFFECTING


def _chip_peers(x, y, c):
    return [(px, py, c) for px, py in _other_chips(x, y)]


def _sibling_peer(x, y, c):
    return [(x, y, 1 - c)]


def _weight_desc(_, k, land_ref, peer, me):
    h = land_ref.shape[1] // 2
    rows = pl.ds(me[2] * h, h)
    mine = land_ref.at[2 * me[0] + me[1], rows, :]
    return mine, mine, land_ref.at[2 * peer[0] + peer[1], rows, :]


def _grad_desc(psum_ref, k, land_ref, peer, me):
    return psum_ref.at[2 * peer[0] + peer[1]], land_ref.at[2 * me[0] + me[1]], land_ref.at[2 * peer[0] + peer[1]]


def _pair_desc(grad_ref, k, land_ref, peer, me):
    h = land_ref.shape[1]
    return grad_ref.at[:, pl.ds(peer[2] * h, h), :], land_ref, land_ref


def _whole_desc(src_ref, k, land_ref, peer, me):
    return src_ref, land_ref, land_ref


def exchange_start(srcs, lands, units, groups, desc, peers, after, name):
    n_s, n_l, n_g = len(srcs), len(lands), len(groups)
    n_p = len(peers(0, 0, 0))

    def body(*refs):
        s_refs, l_refs = refs[:n_s], refs[n_s:n_s + n_l]
        outs = refs[n_s + n_l + 1:]
        sems, token = outs[:2 * n_g], outs[-1]
        me = _coords()
        for g, ids in enumerate(groups):
            for i, u in enumerate(ids):
                si, k = units[u]
                for j, peer in enumerate(peers(*me)):
                    src, dst, _ = desc(s_refs[si] if s_refs else None, k, l_refs[u], peer, me)
                    pltpu.make_async_remote_copy(
                        src_ref=src, dst_ref=dst, send_sem=sems[2 * g].at[n_p * i + j],
                        recv_sem=sems[2 * g + 1].at[n_p * i + j], device_id=peer, device_id_type=MESH).start()
        token[...] = jnp.zeros_like(token)

    arrs = list(srcs) + list(lands)
    sem_shapes = [pltpu.SemaphoreType.DMA((n_p * len(ids),)) for ids in groups for _ in range(2)]
    outs = pl.pallas_call(
        body, name=name,
        in_specs=[HBM_SPEC] * len(arrs) + [ANY],
        out_specs=[SEM_SPEC] * (2 * n_g) + [HBM_SPEC] * len(arrs) + [VMEM_SPEC],
        out_shape=sem_shapes + [pltpu.HBM(a.shape, a.dtype) for a in arrs] + [jax.ShapeDtypeStruct((8, LANES), F32)],
        input_output_aliases={i: 2 * n_g + i for i in range(len(arrs))},
        compiler_params=pltpu.CompilerParams(has_side_effects=DATAFLOW),
    )(*[pltpu.with_memory_space_constraint(a, pltpu.HBM) for a in arrs], after)
    sems = outs[:2 * n_g]
    thru = outs[2 * n_g:2 * n_g + len(arrs)]
    return sems, list(thru[:n_s]), list(thru[n_s:]), outs[-1]


def exchange_wait(srcs, lands, units, send_sem, recv_sem, desc, peers, after, name):
    n_s, n_l = len(srcs), len(lands)
    n_p = len(peers(0, 0, 0))

    def body(*refs):
        s_refs, l_refs = refs[:n_s], refs[n_s:n_s + n_l]
        send_sems, recv_sems = refs[n_s + n_l], refs[n_s + n_l + 1]
        me = _coords()
        for i, (si, k) in enumerate(units):
            for j, peer in enumerate(peers(*me)):
                src, _, mine = desc(s_refs[si] if s_refs else None, k, l_refs[i], peer, me)
                cp = pltpu.make_async_remote_copy(
                    src_ref=src, dst_ref=mine, send_sem=send_sems.at[n_p * i + j], recv_sem=recv_sems.at[n_p * i + j],
                    device_id=peer, device_id_type=MESH)
                cp.wait_send()
                cp.wait_recv()

    arrs = list(srcs) + list(lands)
    outs = pl.pallas_call(
        body, name=name,
        in_specs=[HBM_SPEC] * len(arrs) + [SEM_SPEC, SEM_SPEC, ANY],
        out_specs=[HBM_SPEC] * len(arrs),
        out_shape=[pltpu.HBM(a.shape, a.dtype) for a in arrs],
        input_output_aliases={i: i for i in range(len(arrs))},
        compiler_params=pltpu.CompilerParams(has_side_effects=DATAFLOW),
    )(*arrs, send_sem, recv_sem, after)
    return list(outs[:n_s]), list(outs[n_s:])


def sibling_fill(lands, name):
    n_u = len(lands)

    def body(*refs):
        ins, outs = refs[:n_u], refs[n_u:2 * n_u]
        send_sems, recv_sems = refs[2 * n_u:]
        x, y, c = _coords()
        sends = []
        for u in range(n_u):
            h = ins[u].shape[1] // 2
            for j, (px, py) in enumerate(_other_chips(x, y)):
                part = (2 * px + py, pl.ds(c * h, h), slice(None))
                cp = pltpu.make_async_remote_copy(
                    src_ref=ins[u].at[part], dst_ref=outs[u].at[part], send_sem=send_sems.at[3 * u + j],
                    recv_sem=recv_sems.at[3 * u + j], device_id=(x, y, 1 - c), device_id_type=MESH)
                cp.start()
                sends.append(cp)
        for u in range(n_u):
            h = ins[u].shape[1] // 2
            for j, (px, py) in enumerate(_other_chips(x, y)):
                theirs = (2 * px + py, pl.ds((1 - c) * h, h), slice(None))
                pltpu.make_async_remote_copy(
                    src_ref=ins[u].at[theirs], dst_ref=outs[u].at[theirs], send_sem=send_sems.at[3 * u + j],
                    recv_sem=recv_sems.at[3 * u + j], device_id=(x, y, 1 - c), device_id_type=MESH).wait_recv()
        for cp in sends:
            cp.wait_send()

    return pl.pallas_call(
        body, name=name, in_specs=[ANY] * n_u, out_specs=[ANY] * n_u,
        out_shape=[jax.ShapeDtypeStruct(a.shape, a.dtype) for a in lands],
        input_output_aliases={i: i for i in range(n_u)},
        scratch_shapes=[pltpu.SemaphoreType.DMA((3 * n_u,)), pltpu.SemaphoreType.DMA((3 * n_u,))],
        compiler_params=_params(),
    )(*lands)


def _pack_rows(parts):
    flat = jnp.concatenate([p.reshape(-1).astype(F32) for p in parts])
    n = flat.shape[0]
    padded = -(-n // (8 * LANES)) * (8 * LANES)
    return jnp.pad(flat, (0, padded - n)).reshape(-1, LANES)


def _unpack_rows(packed, shapes):
    flat = packed.reshape(-1)
    out, off = [], 0
    for s in shapes:
        size = 1
        for d in s:
            size *= d
        out.append(flat[off:off + size].reshape(s))
        off += size
    return out


def _shard_last(full, s_me):
    n = full.shape[-1] // N_CHIP
    return lax.dynamic_slice_in_dim(full, s_me * n, n, axis=full.ndim - 1)


def _unshard_last(g):
    moved = jnp.moveaxis(g, 0, -2)
    return moved.reshape(moved.shape[:-2] + (moved.shape[-2] * moved.shape[-1],))


def kernel(x, c, w_ada, b_ada, ln_g, ln_b, ffn_gu, ffn_down, gmlp_w_in, gmlp_b_in, gmlp_ln_g, gmlp_ln_b, gmlp_w_s, gmlp_b_s, gmlp_w_out, w_ada_kv, b_ada_kv, w_kv, attn_w_q, attn_rel_bias, attn_w_o, loss_target, m_w_ada, m_b_ada, m_ln_g, m_ln_b, m_ffn_gu, m_ffn_down, m_gmlp_w_in, m_gmlp_b_in, m_gmlp_ln_g, m_gmlp_ln_b, m_gmlp_w_s, m_gmlp_b_s, m_gmlp_w_out, m_w_ada_kv, m_b_ada_kv, m_w_kv, m_attn_w_q, m_attn_rel_bias, m_attn_w_o, v_w_ada, v_b_ada, v_ln_g, v_ln_b, v_ffn_gu, v_ffn_down, v_gmlp_w_in, v_gmlp_b_in, v_gmlp_ln_g, v_gmlp_ln_b, v_gmlp_w_s, v_gmlp_b_s, v_gmlp_w_out, v_w_ada_kv, v_b_ada_kv, v_w_kv, v_attn_w_q, v_attn_rel_bias, v_attn_w_o):
    xi, yi, ci = _coords()
    s_me = 2 * xi + yi
    dev = 4 * xi + 2 * yi + ci

    x0 = x[0]
    tgt = loss_target[0]
    S, D = x0.shape
    L = w_ada.shape[0]
    NA = gmlp_w_in.shape[0]
    NB = attn_w_q.shape[0]
    NH = D // HEAD_DIM
    alpha = (2.0 * L) ** 0.25
    n_ada = w_ada.shape[2]
    n_kv = w_ada_kv.shape[1]

    stack_names = ["ffn_gu", "ffn_down", "gmlp_w_in", "gmlp_w_out", "w_kv", "attn_w_q", "attn_w_o"]
    stack_src = dict(ffn_gu=ffn_gu, ffn_down=ffn_down, gmlp_w_in=gmlp_w_in, gmlp_w_out=gmlp_w_out, w_kv=w_kv[None],
                     attn_w_q=attn_w_q, attn_w_o=attn_w_o)
    stacks = [stack_src[nm].reshape((-1,) + stack_src[nm].shape[-2:]) for nm in stack_names]
    units = [(si, k) for si, st in enumerate(stacks) for k in range(st.shape[0])]
    unit_of = {(stack_names[si], k): u for u, (si, k) in enumerate(units)}
    weight_groups = [[("ffn_gu", 0)], [("ffn_down", 0)]]
    for l in range(L):
        mixer = [("gmlp_w_in", l), ("gmlp_w_out", l)] if l < NA else [("attn_w_q", l - NA), ("attn_w_o", l - NA)]
        first, last = [("ffn_gu", 2 * l), ("ffn_down", 2 * l)], [("ffn_gu", 2 * l + 1), ("ffn_down", 2 * l + 1)]
        if l == 0:
            weight_groups += [mixer, last]
        else:
            weight_groups += [([("w_kv", 0)] if l == NA else []) + first, mixer, last]
    weight_groups = [[unit_of[n] for n in names] for names in weight_groups]
    group_of = {u: g for g, ids in enumerate(weight_groups) for u in ids}

    c_all = all_gather8(jnp.broadcast_to(c, (8, D)), "ag_c").reshape(N_DEV, 8, D)[:, 0]
    b_ada_sh = lax.dynamic_slice_in_dim(b_ada, s_me * n_ada, n_ada, axis=1)
    b_kv_sh = lax.dynamic_slice_in_dim(b_ada_kv, s_me * n_kv, n_kv, axis=0)
    mod_part = ada_fwd(c_all, w_ada, b_ada_sh[:, None, :], "ada_fwd")
    mkv_part = ada_fwd(c_all, w_ada_kv[None], b_kv_sh[None, None, :], "ada_kv_fwd")
    part = jnp.concatenate([jnp.transpose(mod_part, (1, 0, 2)).reshape(N_DEV, L * n_ada), mkv_part[0]], axis=1)
    width = part.shape[1]
    pad_w = -(-width // LANES) * LANES - width
    all_part = all_gather8(jnp.pad(part, ((0, 0), (0, pad_w))), "ag_mod").reshape(N_DEV, N_DEV, width + pad_w)
    mine = lax.dynamic_index_in_dim(all_part[0::2], dev, axis=1, keepdims=False)
    mod = jnp.transpose(mine[:, :L * n_ada].reshape(N_CHIP, L, n_ada), (1, 0, 2)).reshape(L, N_MOD, D)
    mkv = mine[:, L * n_ada:width].reshape(2, D)

    def mrow(l, k):
        return mod[l, k][None, :]

    small_shapes = [ln_g.shape, ln_b.shape, gmlp_b_in.shape, gmlp_ln_g.shape, gmlp_ln_b.shape, attn_rel_bias.shape]
    small_pack = _pack_rows([ln_g, ln_b, gmlp_b_in, gmlp_ln_g, gmlp_ln_b, attn_rel_bias])
    small_all = all_gather8(small_pack, "ag_small_params").reshape((N_DEV,) + small_pack.shape)[0::2]
    sm = [_unpack_rows(small_all[s], small_shapes) for s in range(N_CHIP)]
    ln_g_f, ln_b_f, b_in_f, gln_g_f, gln_b_f, rel_f = [
        _unshard_last(jnp.stack([sm[s][i] for s in range(N_CHIP)])) for i in range(len(small_shapes))]

    def landing(u):
        si, k = units[u]
        shard = stacks[si][k]
        if group_of[u] < 2:
            shard = lax.optimization_barrier(shard)
        shard = shard.astype(BF16)
        return lax.dynamic_update_slice(lax.empty((N_CHIP,) + shard.shape, BF16), shard[None], (s_me, 0, 0))

    gathers_done = jnp.concatenate([mod.reshape(-1)[:LANES], small_all.reshape(-1)[:LANES]])
    w_sems, lands_t = {}, {}
    for part, groups, name in ((0, weight_groups[:2], "weight_send_start_first"),
                               (1, weight_groups[2:], "weight_send_start_rest")):
        ids = [u for grp in groups for u in grp]
        local = [[ids.index(u) for u in grp] for grp in groups]
        sems, _, lands, gathers_done = exchange_start([], [landing(u) for u in ids], [units[u] for u in ids], local,
                                                      _weight_desc, _chip_peers, gathers_done, name)
        for i, grp in enumerate(groups):
            w_sems[2 * part + i] = (sems[2 * i], sems[2 * i + 1])
        lands_t.update(zip(ids, lands))
    wg = {}
    latest = [gathers_done]

    def W(nm, k):
        u = unit_of[(nm, k)]
        if u not in wg:
            g = group_of[u]
            ids = weight_groups[g]
            _, got = exchange_wait([], [lands_t[v] for v in ids], [units[v] for v in ids], w_sems[g][0],
                                   w_sems[g][1], _weight_desc, _chip_peers, latest[0], "weight_send_wait_%d" % g)
            wg.update(zip(ids, sibling_fill(got, "weight_sibling_fill")))
        return wg[u]

    def Wrows(nm, k):
        w4 = W(nm, k)
        return w4.reshape(w4.shape[0] * w4.shape[1], w4.shape[2])

    bst = [jnp.transpose(gmlp_b_s[j]) for j in range(NA)]
    biases = {}

    def make_bias(j, dep):
        rel, _ = lax.optimization_barrier((rel_f[j], dep))
        biases[j] = jnp.transpose(bias_expand(rel, "bias_expand"), (1, 0, 2))
        return biases[j]

    saved = []
    xc = x0
    kpad = vpad = xkv = None
    for l in range(L):
        if l == 1 and NB > 1:
            latest[0] = make_bias(1, xc)
        if l == NA:
            xkv = xc
            kv, hkv = mod_matmul(xc, mkv[1][None], mkv[0][None], W("w_kv", 0), None, BF16, "kv_proj")
            kpad = jnp.pad(kv[:, :D], ((LEFT_PAD, 0), (0, 0)))
            vpad = jnp.pad(kv[:, D:], ((LEFT_PAD, 0), (0, 0)))
        sv = {}
        for i in (0, 2):
            k = 2 * l + i // 2
            gu, hv = mod_matmul(xc, mrow(l, 3 * i + 1), mrow(l, 3 * i), W("ffn_gu", k), None, BF16, "ffn_up")
            latest[0] = hv
            gw = 0.5 * (1.0 + mrow(l, 3 * i + 2))
            xn, xh, rs, yv, av = matmul_res_ln(gu, Wrows("ffn_down", k), xc, gw, ln_g_f[l, i][None],
                                               ln_b_f[l, i][None], alpha, True, "ffn_down")
            sv[i] = dict(x=xc, h=hv, gu=gu, a=av, xh=xh, rs=rs, y=yv, gw=gw)
            xc = latest[0] = xn
            if i == 0:
                if l == 0 and NB > 0:
                    latest[0] = make_bias(0, xc)
                gw = 1.0 + mrow(l, 5)
                if l < NA:
                    pre, hv = mod_matmul(xc, mrow(l, 4), mrow(l, 3), W("gmlp_w_in", l), b_in_f[l][None], F32,
                                         "gmlp_in")
                    qv = sgu_fwd(pre, gln_g_f[l][None], gln_b_f[l][None], gmlp_w_s[l], bst[l], "sgu_fwd")
                    xn, xh, rs, yv = matmul_res_ln(qv, Wrows("gmlp_w_out", l), xc, gw, ln_g_f[l, 1][None],
                                                   ln_b_f[l, 1][None], alpha, False, "gmlp_out")
                    sv[1] = dict(x=xc, h=hv, pre=pre, a=qv, xh=xh, rs=rs, y=yv, gw=gw)
                else:
                    j = l - NA
                    if j not in biases:
                        make_bias(j, xc)
                    qh, hv = mod_matmul(xc, mrow(l, 4), mrow(l, 3), Wrows("attn_w_q", j)[None], None, BF16, "attn_q")
                    ov = attn_fwd(qh, kpad, vpad, biases[j], "attn_fwd")
                    xn, xh, rs, yv = matmul_res_ln(ov, Wrows("attn_w_o", j), xc, gw, ln_g_f[l, 1][None],
                                                   ln_b_f[l, 1][None], alpha, False, "attn_out")
                    sv[1] = dict(x=xc, h=hv, q=qh, a=ov, xh=xh, rs=rs, y=yv, gw=gw)
                xc = latest[0] = xn
        saved.append(sv)

    dx, lacc = loss_grad(xc, tgt, "loss_grad")
    loss = lax.psum((0.5 / D) * jnp.sum(lacc[0]), ("x", "y", "c"))

    gpair = [None] * len(units)
    col_split = {u for u, (si, _) in enumerate(units) if stack_names[si] != "ffn_gu"}
    dmod = [[None] * N_MOD for _ in range(L)]
    d_ln_g = [[None] * 3 for _ in range(L)]
    d_ln_b = [[None] * 3 for _ in range(L)]
    d_b_in, d_gln_g, d_gln_b, d_ws, d_bs, d_rel = ([None] * NA, [None] * NA, [None] * NA, [None] * NA, [None] * NA,
                                                  [None] * NB)
    dk = jnp.zeros((S, D), F32)
    dv = jnp.zeros((S, D), F32)
    dmkv = None

    made = []

    core_idx = ci.astype(jnp.int32).reshape(1)
    chip_idx = s_me.astype(jnp.int32).reshape(1)

    def put(nm, k, a, b, name):
        u = unit_of[(nm, k)]
        rows, cols = stacks[units[u][0]].shape[1:]
        if nm == "ffn_gu":
            g = wgrad_pair(b, a, N_CHIP, cols, rows // 2, lambda j, p: j, lambda j, p: p, core_idx, name)
        elif nm in ("gmlp_w_in", "w_kv"):
            g = wgrad_pair(a, b, N_CHIP, rows, cols // 2, lambda j, p: 0, lambda j, p: 2 * j + p, core_idx, name)
        else:
            g = wgrad_pair(a, b, 1, N_CHIP * rows, cols // 2, lambda j, p: 0, lambda j, p: p, core_idx, name)
        gpair[u] = g.reshape((N_CHIP, -1, g.shape[-1]))
        made.append(u)

    own_half, sib_half = {}, {}
    n_started = [0]

    def start_grad_exchange(ids, after):
        psums = [gpair[u] for u in ids]
        n = len(ids)
        tag = n_started[0]
        n_started[0] += 1
        sems, ps_t, q_t, token = exchange_start(psums, [lax.empty(p.shape, p.dtype) for p in psums],
                                                [(i, 0) for i in range(n)], [list(range(n))], _grad_desc, _chip_peers,
                                                after, "grad_send_start_%d" % tag)
        return dict(ids=ids, tag=tag, sems=sems, ps=ps_t, q=q_t), token

    def finish_grad_exchange(pend, after):
        n = len(pend["ids"])
        ps_t, q = exchange_wait(pend["ps"], pend["q"], [(i, 0) for i in range(n)], pend["sems"][0], pend["sems"][1],
                                _grad_desc, _chip_peers, after, "grad_send_wait_%d" % pend["tag"])
        halves = [chip_sum(ps_t[i], q[i], chip_idx, u not in col_split, "grad_chip_sum")
                  for i, u in enumerate(pend["ids"])]
        sems, h_t, land_t, token = exchange_start(halves, [lax.empty(h.shape, h.dtype) for h in halves],
                                              [(i, 0) for i in range(n)], [list(range(n))], _whole_desc,
                                              _sibling_peer, halves[0], "half_send_start_%d" % pend["tag"])
        swaps.append(dict(ids=pend["ids"], tag=pend["tag"], sems=sems, h=h_t, land=land_t))
        return token

    def collect_halves(after):
        for sw in swaps:
            n = len(sw["ids"])
            h, land = exchange_wait(sw["h"], sw["land"], [(i, 0) for i in range(n)], sw["sems"][0], sw["sems"][1],
                                    _whole_desc, _sibling_peer, after, "half_send_wait_%d" % sw["tag"])
            for u, mine, theirs in zip(sw["ids"], h, land):
                own_half[u], sib_half[u] = mine, theirs
        swaps.clear()

    swaps = []
    pending = None
    started_before = jnp.zeros((8, LANES), F32)

    def ln_inputs(l, i):
        t = saved[l][i]
        return (t["xh"], t["rs"], ln_g_f[l, i][None], t["y"], t["gw"], 1.0 if i == 1 else 0.5)

    def record_ln(l, i, acc, row0):
        d_ln_g[l][i], d_ln_b[l][i], dmod[l][3 * i + 2] = acc[row0], acc[row0 + 1], acc[row0 + 2]

    ln_done = None
    for l in reversed(range(L)):
        if l == NA - 1:
            dkv = jnp.concatenate([dk, dv], axis=1)
            put("w_kv", 0, hkv, dkv, "kv_wgrad")
            pdxa, pdy, acc = dgrad_mod(dkv, W("w_kv", 0), dx, xkv, mkv[1][None], ln_inputs(l, 2), alpha, "kv_dgrad")
            dmkv = jnp.stack([acc[1], acc[0]])
            record_ln(l, 2, acc, 2)
            ln_done = (pdxa, pdy)
        sv = saved[l]
        for i in (2, 1, 0):
            t = sv[i]
            if ln_done is None:
                dxa, dy, acc1 = ln_res_bwd(dx, *ln_inputs(l, i), alpha, "ln_res_bwd")
                record_ln(l, i, acc1, 0)
            else:
                dxa, dy = ln_done
                ln_done = None
            before = (l, i - 1) if i > 0 else ((l - 1, 2) if l > 0 and l != NA else None)
            prev = ln_inputs(*before) if before is not None else None
            scl = mrow(l, 3 * i + 1)
            if i != 1:
                k = 2 * l + i // 2
                F = t["gu"].shape[1] // 2
                dgu = ffn_act_bwd(dy, Wrows("ffn_down", k), t["gu"], started_before, "ffn_act_bwd")
                put("ffn_down", k, t["a"], dy, "ffn_down_wgrad")
                put("ffn_gu", k, t["h"], dgu, "ffn_up_wgrad")
                res = dgrad_mod(dgu, W("ffn_gu", k), dxa, t["x"], scl, prev, alpha, "ffn_up_dgrad")
            elif l < NA:
                dq = matmul_nt(dy, Wrows("gmlp_w_out", l), started_before, "gmlp_out_dgrad")
                put("gmlp_w_out", l, t["a"], dy, "gmlp_out_wgrad")
                dpre, dws_l, dss, dgl, dbin = sgu_bwd(dq, t["pre"], gln_g_f[l][None], gln_b_f[l][None], gmlp_w_s[l],
                                                      bst[l], "sgu_bwd")
                d_ws[l] = dws_l
                d_bs[l] = jnp.transpose(group_lane_sum(dss, "sgu_bias_grad")[:, :GMLP_GROUPS])
                d_gln_g[l], d_gln_b[l], d_b_in[l] = dgl[0], dgl[1], dbin[0]
                put("gmlp_w_in", l, t["h"], dpre, "gmlp_in_wgrad")
                res = dgrad_mod(dpre, W("gmlp_w_in", l), dxa, t["x"], scl, prev, alpha, "gmlp_in_dgrad")
            else:
                j = l - NA
                do = matmul_nt(dy, Wrows("attn_w_o", j), started_before, "attn_out_dgrad")
                put("attn_w_o", j, t["a"], dy, "attn_out_wgrad")
                dqh, dk, dv, dbias = attn_bwd(t["q"], do, kpad, vpad, biases[j], dk, dv, "attn_bwd")
                d_rel[j] = bias_grad(jnp.transpose(dbias, (1, 0, 2)), "bias_grad")
                put("attn_w_q", j, t["h"], dqh, "attn_q_wgrad")
                res = dgrad_mod(dqh, Wrows("attn_w_q", j)[None], dxa, t["x"], scl, prev, alpha, "attn_q_dgrad")
            acc2 = res[-1]
            dmod[l][3 * i + 1], dmod[l][3 * i] = acc2[0], acc2[1]
            if before is None:
                dx = res[0]
            else:
                record_ln(*before, acc2, 2)
                ln_done = (res[0], res[1])
            if (i == 0 and l > 0) or (i == 1 and l == 0):
                started, started_before = start_grad_exchange(list(made), acc2)
                made.clear()
                if pending is not None:
                    started_before = started_before + finish_grad_exchange(pending, acc2)
                pending = started
    grad_x = dx[None]

    dvec = _pack_rows([jnp.stack([jnp.stack(r) for r in dmod]), dmkv])
    dvec = lax.optimization_barrier((dvec, [gpair[u] for u in made]))[0]
    n_dvec = L * N_MOD * D + 2 * D
    dall = all_gather8(dvec, "ag_dmod").reshape(N_DEV, -1, LANES)
    db_all = sum_leading(dall, "ada_bias_grad").reshape(-1)[:n_dvec]
    g_b_ada = db_all[:L * N_MOD * D].reshape(L, N_MOD * D)
    g_b_ada_kv = db_all[L * N_MOD * D:]
    dall2 = dall.reshape(N_DEV, -1)[:, :n_dvec]
    dmod_all = dall2[:, :L * N_MOD * D].reshape(N_DEV, L, N_MOD * D)
    dmod_sh = jnp.transpose(lax.dynamic_slice_in_dim(dmod_all, s_me * n_ada, n_ada, axis=2), (1, 0, 2))
    dmkv_sh = lax.dynamic_slice_in_dim(dall2[:, L * N_MOD * D:], s_me * n_kv, n_kv, axis=1)[None]
    c_all_t = jnp.transpose(c_all)
    g_w_ada = ada_wgrad(c_all_t, dmod_sh, "ada_wgrad")
    g_w_ada_kv = ada_wgrad(c_all_t, dmkv_sh, "ada_kv_wgrad")[0]

    small_g = [jnp.stack([jnp.stack(r) for r in d_ln_g]), jnp.stack([jnp.stack(r) for r in d_ln_b]),
               jnp.stack(d_b_in), jnp.stack(d_gln_g), jnp.stack(d_gln_b), jnp.stack(d_rel), jnp.stack(d_ws),
               jnp.stack(d_bs)]
    sg_shapes = [a.shape for a in small_g]
    sg_pack = _pack_rows(small_g)
    sg_all = all_gather8(sg_pack, "ag_small_grads").reshape(N_DEV, -1, LANES)
    sg_sum = _unpack_rows(sum_leading(sg_all, "small_grad_sum"), sg_shapes)
    g_ln_g, g_ln_b, g_b_in, g_gln_g, g_gln_b, g_rel = [_shard_last(a, s_me) for a in sg_sum[:6]]
    g_ws, g_bs = sg_sum[6], sg_sum[7]

    last, _ = start_grad_exchange(list(made), sg_all)

    grads = dict(w_ada=g_w_ada, b_ada=g_b_ada, ln_g=g_ln_g, ln_b=g_ln_b, gmlp_b_in=g_b_in, gmlp_ln_g=g_gln_g,
                 gmlp_ln_b=g_gln_b, gmlp_w_s=g_ws, gmlp_b_s=g_bs, w_ada_kv=g_w_ada_kv, b_ada_kv=g_b_ada_kv,
                 attn_rel_bias=g_rel)
    weights = dict(w_ada=w_ada, b_ada=b_ada, ln_g=ln_g, ln_b=ln_b, ffn_gu=ffn_gu, ffn_down=ffn_down,
                   gmlp_w_in=gmlp_w_in, gmlp_b_in=gmlp_b_in, gmlp_ln_g=gmlp_ln_g, gmlp_ln_b=gmlp_ln_b,
                   gmlp_w_s=gmlp_w_s, gmlp_b_s=gmlp_b_s, gmlp_w_out=gmlp_w_out, w_ada_kv=w_ada_kv,
                   b_ada_kv=b_ada_kv, w_kv=w_kv, attn_w_q=attn_w_q, attn_rel_bias=attn_rel_bias, attn_w_o=attn_w_o)
    ms = dict(w_ada=m_w_ada, b_ada=m_b_ada, ln_g=m_ln_g, ln_b=m_ln_b, ffn_gu=m_ffn_gu, ffn_down=m_ffn_down,
              gmlp_w_in=m_gmlp_w_in, gmlp_b_in=m_gmlp_b_in, gmlp_ln_g=m_gmlp_ln_g, gmlp_ln_b=m_gmlp_ln_b,
              gmlp_w_s=m_gmlp_w_s, gmlp_b_s=m_gmlp_b_s, gmlp_w_out=m_gmlp_w_out, w_ada_kv=m_w_ada_kv,
              b_ada_kv=m_b_ada_kv, w_kv=m_w_kv, attn_w_q=m_attn_w_q, attn_rel_bias=m_attn_rel_bias,
              attn_w_o=m_attn_w_o)
    vs = dict(w_ada=v_w_ada, b_ada=v_b_ada, ln_g=v_ln_g, ln_b=v_ln_b, ffn_gu=v_ffn_gu, ffn_down=v_ffn_down,
              gmlp_w_in=v_gmlp_w_in, gmlp_b_in=v_gmlp_b_in, gmlp_ln_g=v_gmlp_ln_g, gmlp_ln_b=v_gmlp_ln_b,
              gmlp_w_s=v_gmlp_w_s, gmlp_b_s=v_gmlp_b_s, gmlp_w_out=v_gmlp_w_out, w_ada_kv=v_w_ada_kv,
              b_ada_kv=v_b_ada_kv, w_kv=v_w_kv, attn_w_q=v_attn_w_q, attn_rel_bias=v_attn_rel_bias,
              attn_w_o=v_attn_w_o)
    order = ["w_ada", "b_ada", "ln_g", "ln_b", "ffn_gu", "ffn_down", "gmlp_w_in", "gmlp_b_in", "gmlp_ln_g",
             "gmlp_ln_b", "gmlp_w_s", "gmlp_b_s", "gmlp_w_out", "w_ada_kv", "b_ada_kv", "w_kv", "attn_w_q",
             "attn_rel_bias", "attn_w_o"]
    big_names = ["w_ada", "w_ada_kv"] + stack_names
    small_names = [nm for nm in order if nm not in big_names]
    delta, new_m, new_v = {}, {}, {}

    def adamw_big(nm):
        shp = weights[nm].shape
        two_d = (-1, shp[-1])
        d, a, b = adamw(weights[nm].reshape(two_d), grads[nm].reshape(two_d), ms[nm].reshape(two_d),
                        vs[nm].reshape(two_d), "adamw")
        delta[nm], new_m[nm], new_v[nm] = d.reshape(shp), a.reshape(shp), b.reshape(shp)

    adamw_big("w_ada")
    adamw_big("w_ada_kv")
    shapes = [weights[nm].shape for nm in small_names]
    d, a, b = adamw(_pack_rows([weights[nm] for nm in small_names]), _pack_rows([grads[nm] for nm in small_names]),
                    _pack_rows([ms[nm] for nm in small_names]), _pack_rows([vs[nm] for nm in small_names]),
                    "adamw_small")
    for nm, dd, aa, bb in zip(small_names, _unpack_rows(d, shapes), _unpack_rows(a, shapes), _unpack_rows(b, shapes)):
        delta[nm], new_m[nm], new_v[nm] = dd, aa, bb

    def full_grad(u):
        lo = jnp.where(ci == 0, own_half[u], sib_half[u])
        hi = jnp.where(ci == 0, sib_half[u], own_half[u])
        return jnp.concatenate([lo, hi], axis=1 if u in col_split else 0)

    def adamw_stack(nm):
        si = stack_names.index(nm)
        g = jnp.stack([full_grad(unit_of[(nm, k)]) for k in range(stacks[si].shape[0])])
        grads[nm] = g.reshape(weights[nm].shape)
        adamw_big(nm)

    late = [stack_names[units[u][0]] for u in last["ids"]]
    early = [nm for nm in stack_names if nm not in late]
    finish_grad_exchange(pending, delta["w_ada"])
    collect_halves(delta["w_ada"])
    for nm in early:
        adamw_stack(nm)
    finish_grad_exchange(last, delta[early[-1]])
    collect_halves(delta[early[-1]])
    for nm in stack_names:
        if nm in late:
            adamw_stack(nm)

    return (loss, grad_x, *[grads[nm] for nm in order], *[delta[nm] for nm in order],
            *[new_m[nm] for nm in order], *[new_v[nm] for nm in order])
```

```python
import functools

import jax
import jax.numpy as jnp
from jax import lax
from jax.experimental import pallas as pl
from jax.experimental.pallas import tpu as pltpu

F32 = jnp.float32
BF16 = jnp.bfloat16
MESH = pl.DeviceIdType.MESH
HIGHEST = lax.Precision.HIGHEST

CHUNK = 64
GMLP_WINDOW = 128
GMLP_GROUPS = 8
HEAD_DIM = 64
LEFT_CHUNKS = 8
BAND = (LEFT_CHUNKS + 1) * CHUNK
LEFT_PAD = LEFT_CHUNKS * CHUNK
MAX_REL = 4 * CHUNK
N_REL = (CHUNK - 1) + MAX_REL + 1
LN_EPS = 1e-5
N_MOD = 9
N_DEV = 8
N_CHIP = 4

ADAM_LR = 0.001
ADAM_B1 = 0.9
ADAM_B2 = 0.999
ADAM_EPS = 1e-08
ADAM_WD = 0.01
ADAM_STEP = 10

LANES = 128
ROW_TILE = 256
MATMUL_ROW_TILE = 512
WGRAD_ROWS = 2048
ATTN_CHUNKS_PER_STEP = 16
VMEM_LIMIT_MB = 56

NT = (((1,), (1,)), ((), ()))
TN = (((0,), (0,)), ((), ()))

ANY = pl.BlockSpec(memory_space=pl.ANY)
VMEM_SPEC = pl.BlockSpec(memory_space=pltpu.VMEM)


def _params(semantics=None):
    kw = dict(vmem_limit_bytes=VMEM_LIMIT_MB * 1024 * 1024)
    if semantics is not None:
        kw["dimension_semantics"] = semantics
    return pltpu.CompilerParams(**kw)


def _sigmoid(v):
    return 0.5 * (1.0 + jnp.tanh(0.5 * v))


def _gelu(v):
    return 0.5 * v * (1.0 + lax.erf(v * (2.0 ** -0.5)))


def _gelu_grad(v):
    return 0.5 * (1.0 + lax.erf(v * (2.0 ** -0.5))) + v * jnp.exp(-0.5 * v * v) * ((2.0 * jnp.pi) ** -0.5)


def _row(m):
    return lambda i: (i, 0)


def _fixed2(i):
    return (0, 0)


def _fixed3(i):
    return (0, 0, 0)


def _resident(shape):
    return pl.BlockSpec(shape, _fixed2 if len(shape) == 2 else _fixed3, pipeline_mode=pl.Buffered(1))


def mod_matmul(x, scl, shift, w, bias, out_dtype, name):
    S, D = x.shape
    NS, _, n = w.shape
    tm = min(MATMUL_ROW_TILE, S)
    has_bias = bias is not None

    def body(*refs):
        if has_bias:
            x_ref, scl_ref, sh_ref, w_ref, b_ref, o_ref, h_ref = refs
        else:
            x_ref, scl_ref, sh_ref, w_ref, o_ref, h_ref = refs
        h = (x_ref[...] * (1.0 + scl_ref[...]) + sh_ref[...]).astype(BF16)
        h_ref[...] = h
        for s in range(NS):
            acc = jnp.dot(h, w_ref[s], preferred_element_type=F32)
            if has_bias:
                acc = acc + b_ref[:, s * n:(s + 1) * n]
            o_ref[:, s * n:(s + 1) * n] = acc.astype(out_dtype)

    in_specs = [pl.BlockSpec((tm, D), _row(0)), pl.BlockSpec((1, D), _fixed2), pl.BlockSpec((1, D), _fixed2),
                _resident((NS, D, n))]
    args = [x, scl, shift, w]
    if has_bias:
        in_specs.append(pl.BlockSpec((1, NS * n), _fixed2))
        args.append(bias)
    return pl.pallas_call(
        body, name=name, grid=(S // tm,), in_specs=in_specs,
        out_specs=[pl.BlockSpec((tm, NS * n), _row(0)), pl.BlockSpec((tm, D), _row(0))],
        out_shape=[jax.ShapeDtypeStruct((S, NS * n), out_dtype), jax.ShapeDtypeStruct((S, D), BF16)],
        compiler_params=_params(("parallel",)),
    )(*args)


def matmul_res_ln(a, w, x, gw, lg, lb, alpha, swiglu, name):
    S, D = x.shape
    K = w.shape[0]
    tm = min(ROW_TILE, S)
    ka = a.shape[1]

    def body(a_ref, w_ref, x_ref, gw_ref, lg_ref, lb_ref, xn_ref, xh_ref, rs_ref, y_ref, *act_ref):
        if swiglu:
            g = a_ref[:, :K].astype(F32)
            u = a_ref[:, K:].astype(F32)
            act = (g * _sigmoid(g) * u).astype(BF16)
            act_ref[0][...] = act
        else:
            act = a_ref[...].astype(BF16)
        y = jnp.dot(act, w_ref[...], preferred_element_type=F32)
        z = alpha * x_ref[...] + gw_ref[...] * y
        mu = jnp.mean(z, axis=-1, keepdims=True)
        zc = z - mu
        var = jnp.mean(zc * zc, axis=-1, keepdims=True)
        rstd = lax.rsqrt(var + LN_EPS)
        xhat = zc * rstd
        xn_ref[...] = xhat * lg_ref[...] + lb_ref[...]
        xh_ref[...] = xhat
        rs_ref[...] = rstd
        y_ref[...] = y.astype(BF16)

    vec = pl.BlockSpec((1, D), _fixed2)
    out_specs = [pl.BlockSpec((tm, D), _row(0)), pl.BlockSpec((tm, D), _row(0)), pl.BlockSpec((tm, 1), _row(0)),
                 pl.BlockSpec((tm, D), _row(0))]
    out_shape = [jax.ShapeDtypeStruct((S, D), F32), jax.ShapeDtypeStruct((S, D), F32),
                 jax.ShapeDtypeStruct((S, 1), F32), jax.ShapeDtypeStruct((S, D), BF16)]
    if swiglu:
        out_specs.append(pl.BlockSpec((tm, K), _row(0)))
        out_shape.append(jax.ShapeDtypeStruct((S, K), BF16))
    return pl.pallas_call(
        body, name=name, grid=(S // tm,),
        in_specs=[pl.BlockSpec((tm, ka), _row(0)), _resident((K, D)), pl.BlockSpec((tm, D), _row(0)),
                  vec, vec, vec],
        out_specs=out_specs, out_shape=out_shape,
        compiler_params=_params(("parallel",)),
    )(a, w, x, gw, lg, lb)


def _ln_res_bwd_tile(d, xh_ref, rs_ref, lg_ref, y_ref, gw_ref, wres, alpha, dxa_ref, dy_ref, acc_ref, row0):
    xh = xh_ref[...]
    dxh = d * lg_ref[...]
    m1 = jnp.mean(dxh, axis=-1, keepdims=True)
    m2 = jnp.mean(dxh * xh, axis=-1, keepdims=True)
    dz = rs_ref[...] * (dxh - m1 - xh * m2)
    dxa_ref[...] = alpha * dz
    dy_ref[...] = (gw_ref[...] * dz).astype(BF16)
    acc_ref[row0:row0 + 1, :] += jnp.sum(d * xh, axis=0, keepdims=True)
    acc_ref[row0 + 1:row0 + 2, :] += jnp.sum(d, axis=0, keepdims=True)
    acc_ref[row0 + 2:row0 + 3, :] += jnp.sum((wres * dz) * y_ref[...].astype(F32), axis=0, keepdims=True)


def ln_res_bwd(dxn, xhat, rstd, lg, y, gw, wres, alpha, name):
    S, D = dxn.shape
    tm = min(MATMUL_ROW_TILE, S)

    def body(dxn_ref, xh_ref, rs_ref, lg_ref, y_ref, gw_ref, dxa_ref, dy_ref, acc_ref):
        @pl.when(pl.program_id(0) == 0)
        def _():
            acc_ref[...] = jnp.zeros_like(acc_ref)

        _ln_res_bwd_tile(dxn_ref[...], xh_ref, rs_ref, lg_ref, y_ref, gw_ref, wres, alpha, dxa_ref, dy_ref, acc_ref, 0)

    vec = pl.BlockSpec((1, D), _fixed2)
    tile = pl.BlockSpec((tm, D), _row(0))
    return pl.pallas_call(
        body, name=name, grid=(S // tm,),
        in_specs=[tile, tile, pl.BlockSpec((tm, 1), _row(0)), vec, tile, vec],
        out_specs=[tile, tile, pl.BlockSpec((8, D), _fixed2)],
        out_shape=[jax.ShapeDtypeStruct((S, D), F32), jax.ShapeDtypeStruct((S, D), BF16),
                   jax.ShapeDtypeStruct((8, D), F32)],
        compiler_params=_params(("arbitrary",)),
    )(dxn, xhat, rstd, lg, y, gw)


def ffn_act_bwd(dy, wd, gu, after, name):
    S, D = dy.shape
    K = wd.shape[0]
    tm = min(ROW_TILE, S)

    def body(dy_ref, wd_ref, gu_ref, after_ref, o_ref):
        da = lax.dot_general(dy_ref[...], wd_ref[...], NT, preferred_element_type=F32).astype(BF16)
        g = gu_ref[:, :K]
        u = gu_ref[:, K:]
        sg = _sigmoid(g)
        o_ref[:, :K] = da * u * (sg * (1.0 + g * (1.0 - sg)))
        o_ref[:, K:] = da * (g * sg)

    return pl.pallas_call(
        body, name=name, grid=(S // tm,),
        in_specs=[pl.BlockSpec((tm, D), _row(0)), _resident((K, D)), pl.BlockSpec((tm, 2 * K), _row(0)), ANY],
        out_specs=pl.BlockSpec((tm, 2 * K), _row(0)),
        out_shape=jax.ShapeDtypeStruct((S, 2 * K), BF16),
        compiler_params=_params(("parallel",)),
    )(dy, wd, gu, after)


def matmul_nt(a, w, after, name):
    S, D = a.shape
    K = w.shape[0]
    tm = min(MATMUL_ROW_TILE, S)

    def body(a_ref, w_ref, after_ref, o_ref):
        o_ref[...] = lax.dot_general(a_ref[...], w_ref[...], NT, preferred_element_type=F32).astype(BF16)

    return pl.pallas_call(
        body, name=name, grid=(S // tm,),
        in_specs=[pl.BlockSpec((tm, D), _row(0)), _resident((K, D)), ANY],
        out_specs=pl.BlockSpec((tm, K), _row(0)),
        out_shape=jax.ShapeDtypeStruct((S, K), BF16),
        compiler_params=_params(("parallel",)),
    )(a, w, after)


def dgrad_mod(dpre, w, dxa, xin, scl, prev, alpha, name):
    S, D = xin.shape
    NS, _, n = w.shape
    tm = min(ROW_TILE, S)
    wres = prev[5] if prev is not None else None

    def body(*refs):
        dp_ref, w_ref, dxa_ref, xin_ref, scl_ref = refs[:5]
        acc_ref = refs[-1]

        @pl.when(pl.program_id(0) == 0)
        def _():
            acc_ref[...] = jnp.zeros_like(acc_ref)

        dh = jnp.zeros((tm, D), F32)
        for s in range(NS):
            dh = dh + lax.dot_general(dp_ref[:, s * n:(s + 1) * n].astype(BF16), w_ref[s], NT,
                                      preferred_element_type=F32)
        dx = dxa_ref[...] + dh * (1.0 + scl_ref[...])
        acc_ref[0:1, :] += jnp.sum(dh * xin_ref[...], axis=0, keepdims=True)
        acc_ref[1:2, :] += jnp.sum(dh, axis=0, keepdims=True)
        if prev is None:
            refs[5][...] = dx
        else:
            xh_ref, rs_ref, lg_ref, y_ref, gw_ref, pdxa_ref, pdy_ref = refs[5:12]
            _ln_res_bwd_tile(dx, xh_ref, rs_ref, lg_ref, y_ref, gw_ref, wres, alpha, pdxa_ref, pdy_ref, acc_ref, 2)

    tile = pl.BlockSpec((tm, D), _row(0))
    vec = pl.BlockSpec((1, D), _fixed2)
    in_specs = [pl.BlockSpec((tm, NS * n), _row(0)), _resident((NS, D, n)), tile, tile, vec]
    args = [dpre, w, dxa, xin, scl]
    if prev is None:
        out_specs = [tile]
        out_shape = [jax.ShapeDtypeStruct((S, D), F32)]
    else:
        in_specs += [tile, pl.BlockSpec((tm, 1), _row(0)), vec, tile, vec]
        args += list(prev[:5])
        out_specs = [tile, tile]
        out_shape = [jax.ShapeDtypeStruct((S, D), F32), jax.ShapeDtypeStruct((S, D), BF16)]
    return pl.pallas_call(
        body, name=name, grid=(S // tm,), in_specs=in_specs,
        out_specs=out_specs + [pl.BlockSpec((8, D), _fixed2)],
        out_shape=out_shape + [jax.ShapeDtypeStruct((8, D), F32)],
        compiler_params=_params(("arbitrary",)),
    )(*args)


PAIR_COLLECTIVE_ID = 1
FILL_COLLECTIVE_ID = 2


def wgrad_pair(a, b, J, kb, nb, a_block, b_block, half_idx, name):
    S = b.shape[0]
    ts = min(WGRAD_ROWS, S)
    nsteps = S // ts

    def body(h_ref, a_ref, b_ref, o_ref, acc_ref, send_buf, recv_buf, send_sems, recv_sems):
        jj, si = pl.program_id(0), pl.program_id(1)
        x, y, c = _coords()
        j = lax.rem(jj, J)
        last = si == nsteps - 1

        def copy(blk):
            return pltpu.make_async_remote_copy(
                src_ref=send_buf.at[blk], dst_ref=recv_buf.at[blk], send_sem=send_sems.at[blk],
                recv_sem=recv_sems.at[blk], device_id=(x, y, 1 - c), device_id_type=MESH)

        @pl.when(jnp.logical_and(jj == 0, si == 0))
        def _():
            barrier = pltpu.get_barrier_semaphore()
            pl.semaphore_signal(barrier, inc=1, device_id=(x, y, 1 - c), device_id_type=MESH)
            pl.semaphore_wait(barrier, 1)

        @pl.when(si == 0)
        def _():
            acc_ref[...] = jnp.zeros_like(acc_ref)

        acc_ref[...] += lax.dot_general(a_ref[...], b_ref[...].astype(BF16), TN, preferred_element_type=F32)

        @pl.when(jnp.logical_and(last, jj < J))
        def _():
            send_buf[j] = acc_ref[...].astype(BF16)
            copy(j).start()

        @pl.when(jnp.logical_and(last, jj >= J))
        def _():
            copy(j).wait_recv()
            o_ref[...] = (acc_ref[...] + recv_buf[j].astype(F32)).astype(BF16)

        @pl.when(jnp.logical_and(last, jj == 2 * J - 1))
        def _():
            for blk in range(J):
                copy(blk).wait_send()

    def half(jj, h):
        return jnp.where(jj < J, 1 - h[0], h[0])

    return pl.pallas_call(
        body, name=name,
        grid_spec=pltpu.PrefetchScalarGridSpec(
            num_scalar_prefetch=1, grid=(2 * J, nsteps),
            in_specs=[pl.BlockSpec((ts, kb), lambda jj, s, h: (s, a_block(lax.rem(jj, J), half(jj, h)))),
                      pl.BlockSpec((ts, nb), lambda jj, s, h: (s, b_block(lax.rem(jj, J), half(jj, h))))],
            out_specs=pl.BlockSpec((None, kb, nb), lambda jj, s, h: (jnp.maximum(jj - J, 0), 0, 0)),
            scratch_shapes=[pltpu.VMEM((kb, nb), F32), pltpu.VMEM((J, kb, nb), BF16), pltpu.VMEM((J, kb, nb), BF16),
                            pltpu.SemaphoreType.DMA((J,)), pltpu.SemaphoreType.DMA((J,))]),
        out_shape=jax.ShapeDtypeStruct((J, kb, nb), BF16),
        compiler_params=pltpu.CompilerParams(
            vmem_limit_bytes=VMEM_LIMIT_MB * 1024 * 1024, dimension_semantics=("arbitrary", "arbitrary"),
            collective_id=PAIR_COLLECTIVE_ID),
    )(half_idx, a, b)


def _window_mask():
    t = lax.broadcasted_iota(jnp.int32, (GMLP_WINDOW, GMLP_WINDOW), 0)
    s = lax.broadcasted_iota(jnp.int32, (GMLP_WINDOW, GMLP_WINDOW), 1)
    return ((s // CHUNK) <= (t // CHUNK)).astype(F32)


def sgu_fwd(pre, glg, glb, ws, bst, name):
    S, H2 = pre.shape
    H = H2 // 2
    W, G = GMLP_WINDOW, GMLP_GROUPS
    gd = H // G
    tm = min(ROW_TILE, S)

    def body(pre_ref, glg_ref, glb_ref, ws_ref, bst_ref, q_ref):
        u = _gelu(pre_ref[:, :H])
        v = _gelu(pre_ref[:, H:])
        mu = jnp.mean(v, axis=-1, keepdims=True)
        vc = v - mu
        var = jnp.mean(vc * vc, axis=-1, keepdims=True)
        vn = ((vc * lax.rsqrt(var + LN_EPS)) * glg_ref[...] + glb_ref[...]).astype(BF16)
        mask = _window_mask()
        for g in range(G):
            wsg = (ws_ref[g] * mask).astype(BF16)
            bcol = bst_ref[:, g:g + 1]
            for wi in range(tm // W):
                rows = slice(wi * W, (wi + 1) * W)
                cols = slice(g * gd, (g + 1) * gd)
                s = jnp.dot(wsg, vn[rows, cols], preferred_element_type=F32) + bcol
                q_ref[rows, cols] = (u[rows, cols] * s).astype(BF16)

    return pl.pallas_call(
        body, name=name, grid=(S // tm,),
        in_specs=[pl.BlockSpec((tm, H2), _row(0)), pl.BlockSpec((1, H), _fixed2), pl.BlockSpec((1, H), _fixed2),
                  pl.BlockSpec((G, W, W), _fixed3), pl.BlockSpec((W, G), _fixed2)],
        out_specs=pl.BlockSpec((tm, H), _row(0)),
        out_shape=jax.ShapeDtypeStruct((S, H), BF16),
        compiler_params=_params(("parallel",)),
    )(pre, glg, glb, ws, bst)


def sgu_bwd(dq, pre, glg, glb, ws, bst, name):
    S, H2 = pre.shape
    H = H2 // 2
    W, G = GMLP_WINDOW, GMLP_GROUPS
    gd = H // G
    tm = min(ROW_TILE, S)

    def body(dq_ref, pre_ref, glg_ref, glb_ref, ws_ref, bst_ref,
             dpre_ref, dws_ref, dss_ref, dgl_ref, dbin_ref, du_s, dvn_s):
        @pl.when(pl.program_id(0) == 0)
        def _():
            dws_ref[...] = jnp.zeros_like(dws_ref)
            dss_ref[...] = jnp.zeros_like(dss_ref)
            dgl_ref[...] = jnp.zeros_like(dgl_ref)
            dbin_ref[...] = jnp.zeros_like(dbin_ref)

        pu = pre_ref[:, :H]
        pv = pre_ref[:, H:]
        u = _gelu(pu)
        v = _gelu(pv)
        mu = jnp.mean(v, axis=-1, keepdims=True)
        vc = v - mu
        var = jnp.mean(vc * vc, axis=-1, keepdims=True)
        rstd = lax.rsqrt(var + LN_EPS)
        vhat = vc * rstd
        vn = (vhat * glg_ref[...] + glb_ref[...]).astype(BF16)
        mask = _window_mask()
        for g in range(G):
            wsg = (ws_ref[g] * mask).astype(BF16)
            bcol = bst_ref[:, g:g + 1]
            cols = slice(g * gd, (g + 1) * gd)
            for wi in range(tm // W):
                rows = slice(wi * W, (wi + 1) * W)
                vblk = vn[rows, cols]
                s = jnp.dot(wsg, vblk, preferred_element_type=F32) + bcol
                dqb = dq_ref[rows, cols].astype(F32)
                du_s[rows, cols] = dqb * s
                ds = dqb * u[rows, cols]
                dss_ref[:, cols] += ds
                dsb = ds.astype(BF16)
                dvn_s[rows, cols] = lax.dot_general(wsg, dsb, TN, preferred_element_type=F32)
                dws_ref[g] += lax.dot_general(dsb, vblk, NT, preferred_element_type=F32) * mask
        dvn = dvn_s[...]
        dgl_ref[0:1, :] += jnp.sum(dvn * vhat, axis=0, keepdims=True)
        dgl_ref[1:2, :] += jnp.sum(dvn, axis=0, keepdims=True)
        dvh = dvn * glg_ref[...]
        m1 = jnp.mean(dvh, axis=-1, keepdims=True)
        m2 = jnp.mean(dvh * vhat, axis=-1, keepdims=True)
        dv = rstd * (dvh - m1 - vhat * m2)
        dpu = du_s[...] * _gelu_grad(pu)
        dpv = dv * _gelu_grad(pv)
        dbin_ref[0:1, :H] += jnp.sum(dpu, axis=0, keepdims=True)
        dbin_ref[0:1, H:] += jnp.sum(dpv, axis=0, keepdims=True)
        dpre_ref[:, :H] = dpu.astype(BF16)
        dpre_ref[:, H:] = dpv.astype(BF16)

    return pl.pallas_call(
        body, name=name, grid=(S // tm,),
        in_specs=[pl.BlockSpec((tm, H), _row(0)), pl.BlockSpec((tm, H2), _row(0)), pl.BlockSpec((1, H), _fixed2),
                  pl.BlockSpec((1, H), _fixed2), pl.BlockSpec((G, W, W), _fixed3), pl.BlockSpec((W, G), _fixed2)],
        out_specs=[pl.BlockSpec((tm, H2), _row(0)), pl.BlockSpec((G, W, W), _fixed3), pl.BlockSpec((W, H), _fixed2),
                   pl.BlockSpec((8, H), _fixed2), pl.BlockSpec((8, H2), _fixed2)],
        out_shape=[jax.ShapeDtypeStruct((S, H2), BF16), jax.ShapeDtypeStruct((G, W, W), F32),
                   jax.ShapeDtypeStruct((W, H), F32), jax.ShapeDtypeStruct((8, H), F32),
                   jax.ShapeDtypeStruct((8, H2), F32)],
        scratch_shapes=[pltpu.VMEM((tm, H), F32), pltpu.VMEM((tm, H), F32)],
        compiler_params=_params(("arbitrary",)),
    )(dq, pre, glg, glb, ws, bst)


def group_lane_sum(dss, name):
    W, H = dss.shape
    gd = H // GMLP_GROUPS

    def body(d_ref, o_ref):
        j = lax.broadcasted_iota(jnp.int32, (H, LANES), 0)
        g = lax.broadcasted_iota(jnp.int32, (H, LANES), 1)
        ind = ((j // gd) == g).astype(F32)
        o_ref[...] = jnp.dot(d_ref[...], ind, preferred_element_type=F32, precision=HIGHEST)

    return pl.pallas_call(
        body, name=name, in_specs=[VMEM_SPEC], out_specs=VMEM_SPEC,
        out_shape=jax.ShapeDtypeStruct((W, LANES), F32), compiler_params=_params(),
    )(dss)


def _attn_load(j, cps, q_ref, k_ref, v_ref):
    r = lax.broadcasted_iota(jnp.int32, (CHUNK, BAND), 1)
    chunks = []
    for cc in range(cps):
        start = pl.multiple_of((j * cps + cc) * CHUNK, CHUNK)
        chunks.append((q_ref[cc * CHUNK:(cc + 1) * CHUNK, :], k_ref[pl.ds(start, BAND), :],
                       v_ref[pl.ds(start, BAND), :], (r + start) >= LEFT_PAD))
    return chunks


def _attn_probs(chunks, b_ref, sels, scale):
    qms = [[jnp.where(sel, q2, jnp.zeros_like(q2)) for sel in sels] for q2, _, _, _ in chunks]
    raw = [[lax.dot_general(qm, k2, NT, preferred_element_type=F32) for qm in qms[cc]]
           for cc, (_, k2, _, _) in enumerate(chunks)]
    probs = []
    for cc, (_, _, _, valid) in enumerate(chunks):
        row = []
        for sub in range(2):
            s = jnp.where(valid, raw[cc][sub] * scale + b_ref[sub], -jnp.inf)
            e = jnp.exp(s - jnp.max(s, axis=-1, keepdims=True))
            row.append(e / jnp.sum(e, axis=-1, keepdims=True))
        probs.append(row)
    return qms, probs


def attn_fwd(q, kpad, vpad, bias, name):
    S, D = q.shape
    HP = D // LANES
    cps = min(ATTN_CHUNKS_PER_STEP, S // CHUNK)
    tq = cps * CHUNK
    scale = HEAD_DIM ** -0.5

    def body(q_ref, k_ref, v_ref, b_ref, o_ref):
        sel0 = lax.broadcasted_iota(jnp.int32, (CHUNK, LANES), 1) < HEAD_DIM
        chunks = _attn_load(pl.program_id(1), cps, q_ref, k_ref, v_ref)
        _, probs = _attn_probs(chunks, b_ref, (sel0, jnp.logical_not(sel0)), scale)
        outs = [[jnp.dot(probs[cc][sub].astype(BF16), v2, preferred_element_type=F32) for sub in range(2)]
                for cc, (_, _, v2, _) in enumerate(chunks)]
        o_ref[...] = jnp.concatenate([jnp.where(sel0, o[0], o[1]) for o in outs], axis=0).astype(BF16)

    kv_spec = pl.BlockSpec((S + LEFT_PAD, LANES), lambda h, j: (0, h))
    return pl.pallas_call(
        body, name=name, grid=(HP, S // tq),
        in_specs=[pl.BlockSpec((tq, LANES), lambda h, j: (j, h)), kv_spec, kv_spec,
                  pl.BlockSpec((2, CHUNK, BAND), lambda h, j: (h, 0, 0))],
        out_specs=pl.BlockSpec((tq, LANES), lambda h, j: (j, h)),
        out_shape=jax.ShapeDtypeStruct((S, D), BF16),
        compiler_params=_params(("parallel", "parallel")),
    )(q, kpad, vpad, bias)


def attn_bwd(q, do, kpad, vpad, bias, dk_in, dv_in, name):
    S, D = q.shape
    HP = D // LANES
    NH = 2 * HP
    cps = min(ATTN_CHUNKS_PER_STEP, S // CHUNK)
    tq = cps * CHUNK
    nj = S // tq
    scale = HEAD_DIM ** -0.5

    def body(q_ref, do_ref, k_ref, v_ref, b_ref, dki_ref, dvi_ref, dq_ref, dk_ref, dv_ref, db_ref, dk_acc, dv_acc):
        j = pl.program_id(1)

        @pl.when(j == 0)
        def _():
            dk_acc[:LEFT_PAD, :] = jnp.zeros((LEFT_PAD, LANES), F32)
            dv_acc[:LEFT_PAD, :] = jnp.zeros((LEFT_PAD, LANES), F32)
            dk_acc[LEFT_PAD:, :] = dki_ref[...]
            dv_acc[LEFT_PAD:, :] = dvi_ref[...]
            db_ref[...] = jnp.zeros_like(db_ref)

        sel0 = lax.broadcasted_iota(jnp.int32, (CHUNK, LANES), 1) < HEAD_DIM
        sels = (sel0, jnp.logical_not(sel0))
        chunks = _attn_load(j, cps, q_ref, k_ref, v_ref)
        pairs = [(cc, sub) for cc in range(cps) for sub in range(2)]
        qms, probs = _attn_probs(chunks, b_ref, sels, scale)
        doms = [[jnp.where(sel, do_ref[cc * CHUNK:(cc + 1) * CHUNK, :], jnp.zeros((CHUNK, LANES), BF16))
                 for sel in sels] for cc in range(cps)]
        dps = {(cc, sub): lax.dot_general(doms[cc][sub], chunks[cc][2], NT, preferred_element_type=F32)
               for cc, sub in pairs}
        dss = {}
        for cc, sub in pairs:
            p = probs[cc][sub]
            dss[cc, sub] = p * (dps[cc, sub] - jnp.sum(dps[cc, sub] * p, axis=-1, keepdims=True))
        dsb = {key: ds.astype(BF16) for key, ds in dss.items()}
        dqs = {(cc, sub): jnp.dot(dsb[cc, sub], chunks[cc][1], preferred_element_type=F32) * scale
               for cc, sub in pairs}
        dks = {(cc, sub): lax.dot_general(dsb[cc, sub], qms[cc][sub], TN, preferred_element_type=F32) * scale
               for cc, sub in pairs}
        dvs = {(cc, sub): lax.dot_general(probs[cc][sub].astype(BF16), doms[cc][sub], TN,
                                          preferred_element_type=F32) for cc, sub in pairs}
        dq_ref[...] = jnp.concatenate([jnp.where(sel0, dqs[cc, 0], dqs[cc, 1]) for cc in range(cps)],
                                      axis=0).astype(BF16)
        for sub in range(2):
            total = dss[0, sub]
            for cc in range(1, cps):
                total = total + dss[cc, sub]
            db_ref[sub] += total
        dk_parts = [dks[cc, 0] + dks[cc, 1] for cc in range(cps)]
        dv_parts = [dvs[cc, 0] + dvs[cc, 1] for cc in range(cps)]

        def window(parts):
            blocks = []
            for rb in range(cps - 1 + BAND // CHUNK):
                acc = None
                for cc in range(cps):
                    b = rb - cc
                    if 0 <= b < BAND // CHUNK:
                        piece = parts[cc][b * CHUNK:(b + 1) * CHUNK, :]
                        acc = piece if acc is None else acc + piece
                blocks.append(acc)
            return jnp.concatenate(blocks, axis=0)

        span = pl.ds(pl.multiple_of(j * cps * CHUNK, CHUNK), (cps - 1) * CHUNK + BAND)
        dk_acc[span, :] += window(dk_parts)
        dv_acc[span, :] += window(dv_parts)

        @pl.when(j == nj - 1)
        def _():
            dk_ref[...] = dk_acc[LEFT_PAD:, :]
            dv_ref[...] = dv_acc[LEFT_PAD:, :]

    q_spec = pl.BlockSpec((tq, LANES), lambda h, j: (j, h))
    kv_spec = pl.BlockSpec((S + LEFT_PAD, LANES), lambda h, j: (0, h))
    col_spec = pl.BlockSpec((S, LANES), lambda h, j: (0, h))
    b_spec = pl.BlockSpec((2, CHUNK, BAND), lambda h, j: (h, 0, 0))
    return pl.pallas_call(
        body, name=name, grid=(HP, nj),
        in_specs=[q_spec, q_spec, kv_spec, kv_spec, b_spec, col_spec, col_spec],
        out_specs=[q_spec, col_spec, col_spec, b_spec],
        out_shape=[jax.ShapeDtypeStruct((S, D), BF16), jax.ShapeDtypeStruct((S, D), F32),
                   jax.ShapeDtypeStruct((S, D), F32), jax.ShapeDtypeStruct((NH, CHUNK, BAND), F32)],
        scratch_shapes=[pltpu.VMEM((S + LEFT_PAD, LANES), F32), pltpu.VMEM((S + LEFT_PAD, LANES), F32)],
        compiler_params=_params(("parallel", "arbitrary")),
    )(q, do, kpad, vpad, bias, dk_in, dv_in)


def _rel_onehot(t):
    r = lax.broadcasted_iota(jnp.int32, (BAND, N_REL), 0)
    i = lax.broadcasted_iota(jnp.int32, (BAND, N_REL), 1)
    idx = jnp.clip(t + LEFT_PAD - r, -(CHUNK - 1), MAX_REL) + (CHUNK - 1)
    return (idx == i).astype(BF16)


def _split3(v):
    hi = v.astype(BF16)
    rest = v - hi.astype(F32)
    mid = rest.astype(BF16)
    return hi, mid, (rest - mid.astype(F32)).astype(BF16)


def _dot_onehot(parts, onehot, dims):
    hi, mid, lo = [lax.dot_general(p, onehot, dims, preferred_element_type=F32) for p in parts]
    return (hi + mid) + lo


def bias_expand(rb, name):
    NH = rb.shape[0]

    def body(rb_ref, o_ref):
        parts = _split3(rb_ref[...])

        def step(t, carry):
            o_ref[t] = _dot_onehot(parts, _rel_onehot(t), NT)
            return carry

        lax.fori_loop(0, CHUNK, step, 0)

    return pl.pallas_call(
        body, name=name, in_specs=[VMEM_SPEC], out_specs=VMEM_SPEC,
        out_shape=jax.ShapeDtypeStruct((CHUNK, NH, BAND), F32), compiler_params=_params(),
    )(rb)


def bias_grad(dsum, name):
    NH = dsum.shape[1]

    def body(d_ref, o_ref):
        def step(t, acc):
            return acc + _dot_onehot(_split3(d_ref[t]), _rel_onehot(t), (((1,), (0,)), ((), ())))

        o_ref[...] = lax.fori_loop(0, CHUNK, step, jnp.zeros((NH, N_REL), F32))

    return pl.pallas_call(
        body, name=name, in_specs=[VMEM_SPEC], out_specs=VMEM_SPEC,
        out_shape=jax.ShapeDtypeStruct((NH, N_REL), F32), compiler_params=_params(),
    )(dsum)


def loss_grad(y, tgt, name):
    S, D = y.shape
    tm = min(MATMUL_ROW_TILE, S)

    def body(y_ref, t_ref, d_ref, acc_ref):
        @pl.when(pl.program_id(0) == 0)
        def _():
            acc_ref[...] = jnp.zeros_like(acc_ref)

        err = y_ref[...] - t_ref[...]
        d_ref[...] = err * (1.0 / D)
        acc_ref[0:1, :] += jnp.sum(err * err, axis=0, keepdims=True)

    tile = pl.BlockSpec((tm, D), _row(0))
    return pl.pallas_call(
        body, name=name, grid=(S // tm,), in_specs=[tile, tile],
        out_specs=[tile, pl.BlockSpec((8, D), _fixed2)],
        out_shape=[jax.ShapeDtypeStruct((S, D), F32), jax.ShapeDtypeStruct((8, D), F32)],
        compiler_params=_params(("arbitrary",)),
    )(y, tgt)


def _col_tile(n):
    for t in (768, 512, 256, 128):
        if n % t == 0:
            return t
    return n


def ada_fwd(c_all, w, b, name):
    L, D, n = w.shape
    tn = _col_tile(n)

    def body(c_ref, w_ref, b_ref, o_ref):
        cv = c_ref[...]
        ca = cv * _sigmoid(cv)
        o_ref[...] = jnp.dot(ca, w_ref[...], preferred_element_type=F32, precision=HIGHEST) + b_ref[...]

    return pl.pallas_call(
        body, name=name, grid=(L, n // tn),
        in_specs=[pl.BlockSpec((N_DEV, D), lambda l, j: (0, 0)), pl.BlockSpec((None, D, tn), lambda l, j: (l, 0, j)),
                  pl.BlockSpec((None, 1, tn), lambda l, j: (l, 0, j))],
        out_specs=pl.BlockSpec((None, N_DEV, tn), lambda l, j: (l, 0, j)),
        out_shape=jax.ShapeDtypeStruct((L, N_DEV, n), F32),
        compiler_params=_params(("parallel", "parallel")),
    )(c_all, w, b)


def ada_wgrad(c_all_t, dmod, name):
    L, _, n = dmod.shape
    D = c_all_t.shape[0]
    tn = _col_tile(n)

    def body(c_ref, d_ref, o_ref):
        cv = c_ref[...]
        ca = cv * _sigmoid(cv)
        o_ref[...] = jnp.dot(ca, d_ref[...], preferred_element_type=F32, precision=HIGHEST)

    return pl.pallas_call(
        body, name=name, grid=(L, n // tn),
        in_specs=[pl.BlockSpec((D, N_DEV), lambda l, j: (0, 0)), pl.BlockSpec((None, N_DEV, tn), lambda l, j: (l, 0, j))],
        out_specs=pl.BlockSpec((None, D, tn), lambda l, j: (l, 0, j)),
        out_shape=jax.ShapeDtypeStruct((L, D, n), F32),
        compiler_params=_params(("parallel", "parallel")),
    )(c_all_t, dmod)


ELEMENTWISE_BLOCK_BYTES = 3 * 1024 * 1024


def _elementwise_rows(rows, row_bytes):
    for t in (4096, 2048, 1024, 512, 256, 128, 64, 32, 16):
        if rows % t == 0 and t * row_bytes <= ELEMENTWISE_BLOCK_BYTES:
            return t
    return rows


def sum_leading(a, name):
    n, M, N = a.shape
    tr = _elementwise_rows(M, n * N * 4)

    def body(a_ref, o_ref):
        acc = a_ref[0]
        for i in range(1, n):
            acc = acc + a_ref[i]
        o_ref[...] = acc

    return pl.pallas_call(
        body, name=name, grid=(M // tr,),
        in_specs=[pl.BlockSpec((n, tr, N), lambda i: (0, i, 0))],
        out_specs=pl.BlockSpec((tr, N), _row(0)),
        out_shape=jax.ShapeDtypeStruct((M, N), F32),
        compiler_params=_params(("parallel",)),
    )(a)


def chip_sum(psum, land, chip_idx, transposed, name):
    n, M, N = psum.shape
    tr = M if transposed else _elementwise_rows(M, N * 8)

    def body(s_ref, p_ref, a_ref, b_ref, c_ref, o_ref):
        total = ((p_ref[...].astype(F32) + a_ref[...].astype(F32)) + b_ref[...].astype(F32)) + c_ref[...].astype(F32)
        o_ref[...] = jnp.transpose(total) if transposed else total

    def entry(j):
        return pl.BlockSpec((None, tr, N), lambda i, s: ((s[0] + j) % n, i, 0))

    out_block, out_dims = ((N, tr), (N, M)) if transposed else ((tr, N), (M, N))
    return pl.pallas_call(
        body, name=name,
        grid_spec=pltpu.PrefetchScalarGridSpec(
            num_scalar_prefetch=1, grid=(M // tr,),
            in_specs=[entry(0), entry(1), entry(2), entry(3)],
            out_specs=pl.BlockSpec(out_block, lambda i, s: (0, 0) if transposed else (i, 0))),
        out_shape=jax.ShapeDtypeStruct(out_dims, F32),
        compiler_params=_params(("parallel",)),
    )(chip_idx, psum, land, land, land)


def adamw(w, g, m, v, name):
    M, N = w.shape
    tr = _elementwise_rows(M, N * 4)
    c1 = 1.0 - ADAM_B1 ** ADAM_STEP
    c2 = 1.0 - ADAM_B2 ** ADAM_STEP

    def body(w_ref, g_ref, m_ref, v_ref, d_ref, nm_ref, nv_ref):
        g = g_ref[...]
        nm = ADAM_B1 * m_ref[...] + (1.0 - ADAM_B1) * g
        nv = ADAM_B2 * v_ref[...] + (1.0 - ADAM_B2) * (g * g)
        d_ref[...] = -ADAM_LR * ((nm / c1) / (jnp.sqrt(nv / c2) + ADAM_EPS) + ADAM_WD * w_ref[...])
        nm_ref[...] = nm
        nv_ref[...] = nv

    spec = pl.BlockSpec((tr, N), _row(0))
    shp = jax.ShapeDtypeStruct((M, N), F32)
    return pl.pallas_call(
        body, name=name, grid=(M // tr,), in_specs=[spec] * 4, out_specs=[spec] * 3, out_shape=[shp] * 3,
        compiler_params=_params(("parallel",)),
    )(w, g, m, v)


def _coords():
    return lax.axis_index("x"), lax.axis_index("y"), lax.axis_index("c")


def all_gather8(block, name):
    m_per, n = block.shape

    def body(x_ref, out_ref, send_sems, recv_sems, local_sem):
        x, y, c = _coords()
        me, sibling = (x, y, c), (x, y, 1 - c)
        chips = [(1 - x, y), (x, 1 - y), (1 - x, 1 - y)]

        def rows(px, py, pc):
            return out_ref.at[pl.ds((4 * px + 2 * py + pc) * m_per, m_per), :]

        def copy(k, blk, to, src=None):
            return pltpu.make_async_remote_copy(
                src_ref=rows(*blk) if src is None else src, dst_ref=rows(*blk),
                send_sem=send_sems.at[k], recv_sem=recv_sems.at[k], device_id=to, device_id_type=MESH)

        mine = pltpu.make_async_copy(x_ref, rows(*me), local_sem)
        mine.start()
        first = [copy(0, me, sibling, src=x_ref)]
        first += [copy(1 + j, me, (*chip, c), src=x_ref) for j, chip in enumerate(chips)]
        for cp in first:
            cp.start()
        passed = [copy(4 + j, (*chip, c), sibling) for j, chip in enumerate(chips)]
        for j, chip in enumerate(chips):
            copy(1 + j, (*chip, c), me).wait_recv()
            passed[j].start()
        copy(0, sibling, me).wait_recv()
        for j, chip in enumerate(chips):
            copy(4 + j, (*chip, 1 - c), me).wait_recv()
        for cp in first + passed:
            cp.wait_send()
        mine.wait()

    return pl.pallas_call(
        body, name=name, in_specs=[VMEM_SPEC], out_specs=VMEM_SPEC,
        out_shape=jax.ShapeDtypeStruct((N_DEV * m_per, n), block.dtype),
        scratch_shapes=[pltpu.SemaphoreType.DMA((7,)), pltpu.SemaphoreType.DMA((7,)), pltpu.SemaphoreType.DMA],
        compiler_params=_params(),
    )(block)


def _other_chips(x, y):
    return [(1 - x, y), (x, 1 - y), (1 - x, 1 - y)]


HBM_SPEC = pl.BlockSpec(memory_space=pltpu.HBM)
SEM_SPEC = pl.BlockSpec(memory_space=pltpu.SEMAPHORE)
DATAFLOW = pltpu.SideEffectType.DATAFLOW_SIDE_EFFECTING


def _chip_peers(x, y, c):
    return [(px, py, c) for px, py in _other_chips(x, y)]


def _sibling_peer(x, y, c):
    return [(x, y, 1 - c)]


def _weight_desc(_, k, land_ref, peer, me):
    h = land_ref.shape[1] // 2
    rows = pl.ds(me[2] * h, h)
    mine = land_ref.at[2 * me[0] + me[1], rows, :]
    return mine, mine, land_ref.at[2 * peer[0] + peer[1], rows, :]


def _grad_desc(psum_ref, k, land_ref, peer, me):
    return psum_ref.at[2 * peer[0] + peer[1]], land_ref.at[2 * me[0] + me[1]], land_ref.at[2 * peer[0] + peer[1]]


def _pair_desc(grad_ref, k, land_ref, peer, me):
    h = land_ref.shape[1]
    return grad_ref.at[:, pl.ds(peer[2] * h, h), :], land_ref, land_ref


def _whole_desc(src_ref, k, land_ref, peer, me):
    return src_ref, land_ref, land_ref


def exchange_start(srcs, lands, units, groups, desc, peers, after, name):
    n_s, n_l, n_g = len(srcs), len(lands), len(groups)
    n_p = len(peers(0, 0, 0))

    def body(*refs):
        s_refs, l_refs = refs[:n_s], refs[n_s:n_s + n_l]
        outs = refs[n_s + n_l + 1:]
        sems, token = outs[:2 * n_g], outs[-1]
        me = _coords()
        for g, ids in enumerate(groups):
            for i, u in enumerate(ids):
                si, k = units[u]
                for j, peer in enumerate(peers(*me)):
                    src, dst, _ = desc(s_refs[si] if s_refs else None, k, l_refs[u], peer, me)
                    pltpu.make_async_remote_copy(
                        src_ref=src, dst_ref=dst, send_sem=sems[2 * g].at[n_p * i + j],
                        recv_sem=sems[2 * g + 1].at[n_p * i + j], device_id=peer, device_id_type=MESH).start()
        token[...] = jnp.zeros_like(token)

    arrs = list(srcs) + list(lands)
    sem_shapes = [pltpu.SemaphoreType.DMA((n_p * len(ids),)) for ids in groups for _ in range(2)]
    outs = pl.pallas_call(
        body, name=name,
        in_specs=[HBM_SPEC] * len(arrs) + [ANY],
        out_specs=[SEM_SPEC] * (2 * n_g) + [HBM_SPEC] * len(arrs) + [VMEM_SPEC],
        out_shape=sem_shapes + [pltpu.HBM(a.shape, a.dtype) for a in arrs] + [jax.ShapeDtypeStruct((8, LANES), F32)],
        input_output_aliases={i: 2 * n_g + i for i in range(len(arrs))},
        compiler_params=pltpu.CompilerParams(has_side_effects=DATAFLOW),
    )(*[pltpu.with_memory_space_constraint(a, pltpu.HBM) for a in arrs], after)
    sems = outs[:2 * n_g]
    thru = outs[2 * n_g:2 * n_g + len(arrs)]
    return sems, list(thru[:n_s]), list(thru[n_s:]), outs[-1]


def exchange_wait(srcs, lands, units, send_sem, recv_sem, desc, peers, after, name):
    n_s, n_l = len(srcs), len(lands)
    n_p = len(peers(0, 0, 0))

    def body(*refs):
        s_refs, l_refs = refs[:n_s], refs[n_s:n_s + n_l]
        send_sems, recv_sems = refs[n_s + n_l], refs[n_s + n_l + 1]
        me = _coords()
        for i, (si, k) in enumerate(units):
            for j, peer in enumerate(peers(*me)):
                src, _, mine = desc(s_refs[si] if s_refs else None, k, l_refs[i], peer, me)
                cp = pltpu.make_async_remote_copy(
                    src_ref=src, dst_ref=mine, send_sem=send_sems.at[n_p * i + j], recv_sem=recv_sems.at[n_p * i + j],
                    device_id=peer, device_id_type=MESH)
                cp.wait_send()
                cp.wait_recv()

    arrs = list(srcs) + list(lands)
    outs = pl.pallas_call(
        body, name=name,
        in_specs=[HBM_SPEC] * len(arrs) + [SEM_SPEC, SEM_SPEC, ANY],
        out_specs=[HBM_SPEC] * len(arrs),
        out_shape=[pltpu.HBM(a.shape, a.dtype) for a in arrs],
        input_output_aliases={i: i for i in range(len(arrs))},
        compiler_params=pltpu.CompilerParams(has_side_effects=DATAFLOW),
    )(*arrs, send_sem, recv_sem, after)
    return list(outs[:n_s]), list(outs[n_s:])


def sibling_fill(lands, name):
    n_u = len(lands)

    def body(*refs):
        ins, outs = refs[:n_u], refs[n_u:2 * n_u]
        send_sems, recv_sems = refs[2 * n_u:]
        x, y, c = _coords()
        barrier = pltpu.get_barrier_semaphore()
        pl.semaphore_signal(barrier, inc=1, device_id=(x, y, 1 - c), device_id_type=MESH)
        pl.semaphore_wait(barrier, 1)
        sends = []
        for u in range(n_u):
            h = ins[u].shape[1] // 2
            for j, (px, py) in enumerate(_other_chips(x, y)):
                part = (2 * px + py, pl.ds(c * h, h), slice(None))
                cp = pltpu.make_async_remote_copy(
                    src_ref=ins[u].at[part], dst_ref=outs[u].at[part], send_sem=send_sems.at[3 * u + j],
                    recv_sem=recv_sems.at[3 * u + j], device_id=(x, y, 1 - c), device_id_type=MESH)
                cp.start()
                sends.append(cp)
        for u in range(n_u):
            h = ins[u].shape[1] // 2
            for j, (px, py) in enumerate(_other_chips(x, y)):
                theirs = (2 * px + py, pl.ds((1 - c) * h, h), slice(None))
                pltpu.make_async_remote_copy(
                    src_ref=ins[u].at[theirs], dst_ref=outs[u].at[theirs], send_sem=send_sems.at[3 * u + j],
                    recv_sem=recv_sems.at[3 * u + j], device_id=(x, y, 1 - c), device_id_type=MESH).wait_recv()
        for cp in sends:
            cp.wait_send()

    return pl.pallas_call(
        body, name=name, in_specs=[ANY] * n_u, out_specs=[ANY] * n_u,
        out_shape=[jax.ShapeDtypeStruct(a.shape, a.dtype) for a in lands],
        input_output_aliases={i: i for i in range(n_u)},
        scratch_shapes=[pltpu.SemaphoreType.DMA((3 * n_u,)), pltpu.SemaphoreType.DMA((3 * n_u,))],
        compiler_params=pltpu.CompilerParams(vmem_limit_bytes=VMEM_LIMIT_MB * 1024 * 1024,
                                             collective_id=FILL_COLLECTIVE_ID),
    )(*lands)


def _pack_rows(parts):
    flat = jnp.concatenate([p.reshape(-1).astype(F32) for p in parts])
    n = flat.shape[0]
    padded = -(-n // (8 * LANES)) * (8 * LANES)
    return jnp.pad(flat, (0, padded - n)).reshape(-1, LANES)


def _unpack_rows(packed, shapes):
    flat = packed.reshape(-1)
    out, off = [], 0
    for s in shapes:
        size = 1
        for d in s:
            size *= d
        out.append(flat[off:off + size].reshape(s))
        off += size
    return out


def _shard_last(full, s_me):
    n = full.shape[-1] // N_CHIP
    return lax.dynamic_slice_in_dim(full, s_me * n, n, axis=full.ndim - 1)


def _unshard_last(g):
    moved = jnp.moveaxis(g, 0, -2)
    return moved.reshape(moved.shape[:-2] + (moved.shape[-2] * moved.shape[-1],))


def kernel(x, c, w_ada, b_ada, ln_g, ln_b, ffn_gu, ffn_down, gmlp_w_in, gmlp_b_in, gmlp_ln_g, gmlp_ln_b, gmlp_w_s, gmlp_b_s, gmlp_w_out, w_ada_kv, b_ada_kv, w_kv, attn_w_q, attn_rel_bias, attn_w_o, loss_target, m_w_ada, m_b_ada, m_ln_g, m_ln_b, m_ffn_gu, m_ffn_down, m_gmlp_w_in, m_gmlp_b_in, m_gmlp_ln_g, m_gmlp_ln_b, m_gmlp_w_s, m_gmlp_b_s, m_gmlp_w_out, m_w_ada_kv, m_b_ada_kv, m_w_kv, m_attn_w_q, m_attn_rel_bias, m_attn_w_o, v_w_ada, v_b_ada, v_ln_g, v_ln_b, v_ffn_gu, v_ffn_down, v_gmlp_w_in, v_gmlp_b_in, v_gmlp_ln_g, v_gmlp_ln_b, v_gmlp_w_s, v_gmlp_b_s, v_gmlp_w_out, v_w_ada_kv, v_b_ada_kv, v_w_kv, v_attn_w_q, v_attn_rel_bias, v_attn_w_o):
    xi, yi, ci = _coords()
    s_me = 2 * xi + yi
    dev = 4 * xi + 2 * yi + ci

    x0 = x[0]
    tgt = loss_target[0]
    S, D = x0.shape
    L = w_ada.shape[0]
    NA = gmlp_w_in.shape[0]
    NB = attn_w_q.shape[0]
    NH = D // HEAD_DIM
    alpha = (2.0 * L) ** 0.25
    n_ada = w_ada.shape[2]
    n_kv = w_ada_kv.shape[1]

    stack_names = ["ffn_gu", "ffn_down", "gmlp_w_in", "gmlp_w_out", "w_kv", "attn_w_q", "attn_w_o"]
    stack_src = dict(ffn_gu=ffn_gu, ffn_down=ffn_down, gmlp_w_in=gmlp_w_in, gmlp_w_out=gmlp_w_out, w_kv=w_kv[None],
                     attn_w_q=attn_w_q, attn_w_o=attn_w_o)
    stacks = [stack_src[nm].reshape((-1,) + stack_src[nm].shape[-2:]) for nm in stack_names]
    units = [(si, k) for si, st in enumerate(stacks) for k in range(st.shape[0])]
    unit_of = {(stack_names[si], k): u for u, (si, k) in enumerate(units)}
    weight_groups = [[("ffn_gu", 0)], [("ffn_down", 0)]]
    for l in range(L):
        mixer = [("gmlp_w_in", l), ("gmlp_w_out", l)] if l < NA else [("attn_w_q", l - NA), ("attn_w_o", l - NA)]
        first, last = [("ffn_gu", 2 * l), ("ffn_down", 2 * l)], [("ffn_gu", 2 * l + 1), ("ffn_down", 2 * l + 1)]
        if l == 0:
            weight_groups += [mixer, last]
        else:
            weight_groups += [([("w_kv", 0)] if l == NA else []) + first, mixer, last]
    weight_groups = [[unit_of[n] for n in names] for names in weight_groups]
    group_of = {u: g for g, ids in enumerate(weight_groups) for u in ids}

    c_all = all_gather8(jnp.broadcast_to(c, (8, D)), "ag_c").reshape(N_DEV, 8, D)[:, 0]
    b_ada_sh = lax.dynamic_slice_in_dim(b_ada, s_me * n_ada, n_ada, axis=1)
    b_kv_sh = lax.dynamic_slice_in_dim(b_ada_kv, s_me * n_kv, n_kv, axis=0)
    mod_part = ada_fwd(c_all, w_ada, b_ada_sh[:, None, :], "ada_fwd")
    mkv_part = ada_fwd(c_all, w_ada_kv[None], b_kv_sh[None, None, :], "ada_kv_fwd")
    part = jnp.concatenate([jnp.transpose(mod_part, (1, 0, 2)).reshape(N_DEV, L * n_ada), mkv_part[0]], axis=1)
    width = part.shape[1]
    pad_w = -(-width // LANES) * LANES - width
    all_part = all_gather8(jnp.pad(part, ((0, 0), (0, pad_w))), "ag_mod").reshape(N_DEV, N_DEV, width + pad_w)
    mine = lax.dynamic_index_in_dim(all_part[0::2], dev, axis=1, keepdims=False)
    mod = jnp.transpose(mine[:, :L * n_ada].reshape(N_CHIP, L, n_ada), (1, 0, 2)).reshape(L, N_MOD, D)
    mkv = mine[:, L * n_ada:width].reshape(2, D)

    def mrow(l, k):
        return mod[l, k][None, :]

    small_shapes = [ln_g.shape, ln_b.shape, gmlp_b_in.shape, gmlp_ln_g.shape, gmlp_ln_b.shape, attn_rel_bias.shape]
    small_pack = _pack_rows([ln_g, ln_b, gmlp_b_in, gmlp_ln_g, gmlp_ln_b, attn_rel_bias])
    small_all = all_gather8(small_pack, "ag_small_params").reshape((N_DEV,) + small_pack.shape)[0::2]
    sm = [_unpack_rows(small_all[s], small_shapes) for s in range(N_CHIP)]
    ln_g_f, ln_b_f, b_in_f, gln_g_f, gln_b_f, rel_f = [
        _unshard_last(jnp.stack([sm[s][i] for s in range(N_CHIP)])) for i in range(len(small_shapes))]

    def landing(u):
        si, k = units[u]
        shard = stacks[si][k]
        if group_of[u] < 2:
            shard = lax.optimization_barrier(shard)
        shard = shard.astype(BF16)
        return lax.dynamic_update_slice(lax.empty((N_CHIP,) + shard.shape, BF16), shard[None], (s_me, 0, 0))

    gathers_done = jnp.concatenate([mod.reshape(-1)[:LANES], small_all.reshape(-1)[:LANES]])
    w_sems, lands_t = {}, {}
    for part, groups, name in ((0, weight_groups[:2], "weight_send_start_first"),
                               (1, weight_groups[2:], "weight_send_start_rest")):
        ids = [u for grp in groups for u in grp]
        local = [[ids.index(u) for u in grp] for grp in groups]
        sems, _, lands, gathers_done = exchange_start([], [landing(u) for u in ids], [units[u] for u in ids], local,
                                                      _weight_desc, _chip_peers, gathers_done, name)
        for i, grp in enumerate(groups):
            w_sems[2 * part + i] = (sems[2 * i], sems[2 * i + 1])
        lands_t.update(zip(ids, lands))
    wg = {}
    latest = [gathers_done]

    def W(nm, k):
        u = unit_of[(nm, k)]
        if u not in wg:
            g = group_of[u]
            ids = weight_groups[g]
            _, got = exchange_wait([], [lands_t[v] for v in ids], [units[v] for v in ids], w_sems[g][0],
                                   w_sems[g][1], _weight_desc, _chip_peers, latest[0], "weight_send_wait_%d" % g)
            wg.update(zip(ids, sibling_fill(got, "weight_sibling_fill")))
        return wg[u]

    def Wrows(nm, k):
        w4 = W(nm, k)
        return w4.reshape(w4.shape[0] * w4.shape[1], w4.shape[2])

    bst = [jnp.transpose(gmlp_b_s[j]) for j in range(NA)]
    biases = {}

    def make_bias(j, dep):
        rel, _ = lax.optimization_barrier((rel_f[j], dep))
        biases[j] = jnp.transpose(bias_expand(rel, "bias_expand"), (1, 0, 2))
        return biases[j]

    saved = []
    xc = x0
    kpad = vpad = xkv = None
    for l in range(L):
        if l == 1 and NB > 1:
            latest[0] = make_bias(1, xc)
        if l == NA:
            xkv = xc
            kv, hkv = mod_matmul(xc, mkv[1][None], mkv[0][None], W("w_kv", 0), None, BF16, "kv_proj")
            kpad = jnp.pad(kv[:, :D], ((LEFT_PAD, 0), (0, 0)))
            vpad = jnp.pad(kv[:, D:], ((LEFT_PAD, 0), (0, 0)))
        sv = {}
        for i in (0, 2):
            k = 2 * l + i // 2
            gu, hv = mod_matmul(xc, mrow(l, 3 * i + 1), mrow(l, 3 * i), W("ffn_gu", k), None, BF16, "ffn_up")
            latest[0] = hv
            gw = 0.5 * (1.0 + mrow(l, 3 * i + 2))
            xn, xh, rs, yv, av = matmul_res_ln(gu, Wrows("ffn_down", k), xc, gw, ln_g_f[l, i][None],
                                               ln_b_f[l, i][None], alpha, True, "ffn_down")
            sv[i] = dict(x=xc, h=hv, gu=gu, a=av, xh=xh, rs=rs, y=yv, gw=gw)
            xc = latest[0] = xn
            if i == 0:
                if l == 0 and NB > 0:
                    latest[0] = make_bias(0, xc)
                gw = 1.0 + mrow(l, 5)
                if l < NA:
                    pre, hv = mod_matmul(xc, mrow(l, 4), mrow(l, 3), W("gmlp_w_in", l), b_in_f[l][None], F32,
                                         "gmlp_in")
                    qv = sgu_fwd(pre, gln_g_f[l][None], gln_b_f[l][None], gmlp_w_s[l], bst[l], "sgu_fwd")
                    xn, xh, rs, yv = matmul_res_ln(qv, Wrows("gmlp_w_out", l), xc, gw, ln_g_f[l, 1][None],
                                                   ln_b_f[l, 1][None], alpha, False, "gmlp_out")
                    sv[1] = dict(x=xc, h=hv, pre=pre, a=qv, xh=xh, rs=rs, y=yv, gw=gw)
                else:
                    j = l - NA
                    if j not in biases:
                        make_bias(j, xc)
                    qh, hv = mod_matmul(xc, mrow(l, 4), mrow(l, 3), Wrows("attn_w_q", j)[None], None, BF16, "attn_q")
                    ov = attn_fwd(qh, kpad, vpad, biases[j], "attn_fwd")
                    xn, xh, rs, yv = matmul_res_ln(ov, Wrows("attn_w_o", j), xc, gw, ln_g_f[l, 1][None],
                                                   ln_b_f[l, 1][None], alpha, False, "attn_out")
                    sv[1] = dict(x=xc, h=hv, q=qh, a=ov, xh=xh, rs=rs, y=yv, gw=gw)
                xc = latest[0] = xn
        saved.append(sv)

    dx, lacc = loss_grad(xc, tgt, "loss_grad")
    loss = lax.psum((0.5 / D) * jnp.sum(lacc[0]), ("x", "y", "c"))

    gpair = [None] * len(units)
    col_split = {u for u, (si, _) in enumerate(units) if stack_names[si] != "ffn_gu"}
    dmod = [[None] * N_MOD for _ in range(L)]
    d_ln_g = [[None] * 3 for _ in range(L)]
    d_ln_b = [[None] * 3 for _ in range(L)]
    d_b_in, d_gln_g, d_gln_b, d_ws, d_bs, d_rel = ([None] * NA, [None] * NA, [None] * NA, [None] * NA, [None] * NA,
                                                  [None] * NB)
    dk = jnp.zeros((S, D), F32)
    dv = jnp.zeros((S, D), F32)
    dmkv = None

    made = []

    core_idx = ci.astype(jnp.int32).reshape(1)
    chip_idx = s_me.astype(jnp.int32).reshape(1)

    def put(nm, k, a, b, name):
        u = unit_of[(nm, k)]
        rows, cols = stacks[units[u][0]].shape[1:]
        if nm == "ffn_gu":
            g = wgrad_pair(b, a, N_CHIP, cols, rows // 2, lambda j, p: j, lambda j, p: p, core_idx, name)
        elif nm in ("gmlp_w_in", "w_kv"):
            g = wgrad_pair(a, b, N_CHIP, rows, cols // 2, lambda j, p: 0, lambda j, p: 2 * j + p, core_idx, name)
        else:
            g = wgrad_pair(a, b, 1, N_CHIP * rows, cols // 2, lambda j, p: 0, lambda j, p: p, core_idx, name)
        gpair[u] = g.reshape((N_CHIP, -1, g.shape[-1]))
        made.append(u)

    own_half, sib_half = {}, {}
    n_started = [0]

    def start_grad_exchange(ids, after):
        psums = [gpair[u] for u in ids]
        n = len(ids)
        tag = n_started[0]
        n_started[0] += 1
        sems, ps_t, q_t, token = exchange_start(psums, [lax.empty(p.shape, p.dtype) for p in psums],
                                                [(i, 0) for i in range(n)], [list(range(n))], _grad_desc, _chip_peers,
                                                after, "grad_send_start_%d" % tag)
        return dict(ids=ids, tag=tag, sems=sems, ps=ps_t, q=q_t), token

    def finish_grad_exchange(pend, after):
        n = len(pend["ids"])
        ps_t, q = exchange_wait(pend["ps"], pend["q"], [(i, 0) for i in range(n)], pend["sems"][0], pend["sems"][1],
                                _grad_desc, _chip_peers, after, "grad_send_wait_%d" % pend["tag"])
        halves = [chip_sum(ps_t[i], q[i], chip_idx, u not in col_split, "grad_chip_sum")
                  for i, u in enumerate(pend["ids"])]
        sems, h_t, land_t, token = exchange_start(halves, [lax.empty(h.shape, h.dtype) for h in halves],
                                              [(i, 0) for i in range(n)], [list(range(n))], _whole_desc,
                                              _sibling_peer, halves[0], "half_send_start_%d" % pend["tag"])
        swaps.append(dict(ids=pend["ids"], tag=pend["tag"], sems=sems, h=h_t, land=land_t))
        return token

    def collect_halves(after):
        for sw in swaps:
            n = len(sw["ids"])
            h, land = exchange_wait(sw["h"], sw["land"], [(i, 0) for i in range(n)], sw["sems"][0], sw["sems"][1],
                                    _whole_desc, _sibling_peer, after, "half_send_wait_%d" % sw["tag"])
            for u, mine, theirs in zip(sw["ids"], h, land):
                own_half[u], sib_half[u] = mine, theirs
        swaps.clear()

    swaps = []
    pending = None
    started_before = jnp.zeros((8, LANES), F32)

    def ln_inputs(l, i):
        t = saved[l][i]
        return (t["xh"], t["rs"], ln_g_f[l, i][None], t["y"], t["gw"], 1.0 if i == 1 else 0.5)

    def record_ln(l, i, acc, row0):
        d_ln_g[l][i], d_ln_b[l][i], dmod[l][3 * i + 2] = acc[row0], acc[row0 + 1], acc[row0 + 2]

    ln_done = None
    for l in reversed(range(L)):
        if l == NA - 1:
            dkv = jnp.concatenate([dk, dv], axis=1)
            put("w_kv", 0, hkv, dkv, "kv_wgrad")
            pdxa, pdy, acc = dgrad_mod(dkv, W("w_kv", 0), dx, xkv, mkv[1][None], ln_inputs(l, 2), alpha, "kv_dgrad")
            dmkv = jnp.stack([acc[1], acc[0]])
            record_ln(l, 2, acc, 2)
            ln_done = (pdxa, pdy)
        sv = saved[l]
        for i in (2, 1, 0):
            t = sv[i]
            if ln_done is None:
                dxa, dy, acc1 = ln_res_bwd(dx, *ln_inputs(l, i), alpha, "ln_res_bwd")
                record_ln(l, i, acc1, 0)
            else:
                dxa, dy = ln_done
                ln_done = None
            before = (l, i - 1) if i > 0 else ((l - 1, 2) if l > 0 and l != NA else None)
            prev = ln_inputs(*before) if before is not None else None
            scl = mrow(l, 3 * i + 1)
            if i != 1:
                k = 2 * l + i // 2
                F = t["gu"].shape[1] // 2
                dgu = ffn_act_bwd(dy, Wrows("ffn_down", k), t["gu"], started_before, "ffn_act_bwd")
                put("ffn_down", k, t["a"], dy, "ffn_down_wgrad")
                put("ffn_gu", k, t["h"], dgu, "ffn_up_wgrad")
                res = dgrad_mod(dgu, W("ffn_gu", k), dxa, t["x"], scl, prev, alpha, "ffn_up_dgrad")
            elif l < NA:
                dq = matmul_nt(dy, Wrows("gmlp_w_out", l), started_before, "gmlp_out_dgrad")
                put("gmlp_w_out", l, t["a"], dy, "gmlp_out_wgrad")
                dpre, dws_l, dss, dgl, dbin = sgu_bwd(dq, t["pre"], gln_g_f[l][None], gln_b_f[l][None], gmlp_w_s[l],
                                                      bst[l], "sgu_bwd")
                d_ws[l] = dws_l
                d_bs[l] = jnp.transpose(group_lane_sum(dss, "sgu_bias_grad")[:, :GMLP_GROUPS])
                d_gln_g[l], d_gln_b[l], d_b_in[l] = dgl[0], dgl[1], dbin[0]
                put("gmlp_w_in", l, t["h"], dpre, "gmlp_in_wgrad")
                res = dgrad_mod(dpre, W("gmlp_w_in", l), dxa, t["x"], scl, prev, alpha, "gmlp_in_dgrad")
            else:
                j = l - NA
                do = matmul_nt(dy, Wrows("attn_w_o", j), started_before, "attn_out_dgrad")
                put("attn_w_o", j, t["a"], dy, "attn_out_wgrad")
                dqh, dk, dv, dbias = attn_bwd(t["q"], do, kpad, vpad, biases[j], dk, dv, "attn_bwd")
                d_rel[j] = bias_grad(jnp.transpose(dbias, (1, 0, 2)), "bias_grad")
                put("attn_w_q", j, t["h"], dqh, "attn_q_wgrad")
                res = dgrad_mod(dqh, Wrows("attn_w_q", j)[None], dxa, t["x"], scl, prev, alpha, "attn_q_dgrad")
            acc2 = res[-1]
            dmod[l][3 * i + 1], dmod[l][3 * i] = acc2[0], acc2[1]
            if before is None:
                dx = res[0]
            else:
                record_ln(*before, acc2, 2)
                ln_done = (res[0], res[1])
            if (i == 0 and l > 0) or (i == 1 and l == 0):
                started, started_before = start_grad_exchange(list(made), acc2)
                made.clear()
                if pending is not None:
                    started_before = started_before + finish_grad_exchange(pending, acc2)
                pending = started
    grad_x = dx[None]

    dvec = _pack_rows([jnp.stack([jnp.stack(r) for r in dmod]), dmkv])
    dvec = lax.optimization_barrier((dvec, [gpair[u] for u in made]))[0]
    n_dvec = L * N_MOD * D + 2 * D
    dall = all_gather8(dvec, "ag_dmod").reshape(N_DEV, -1, LANES)
    db_all = sum_leading(dall, "ada_bias_grad").reshape(-1)[:n_dvec]
    g_b_ada = db_all[:L * N_MOD * D].reshape(L, N_MOD * D)
    g_b_ada_kv = db_all[L * N_MOD * D:]
    dall2 = dall.reshape(N_DEV, -1)[:, :n_dvec]
    dmod_all = dall2[:, :L * N_MOD * D].reshape(N_DEV, L, N_MOD * D)
    dmod_sh = jnp.transpose(lax.dynamic_slice_in_dim(dmod_all, s_me * n_ada, n_ada, axis=2), (1, 0, 2))
    dmkv_sh = lax.dynamic_slice_in_dim(dall2[:, L * N_MOD * D:], s_me * n_kv, n_kv, axis=1)[None]
    c_all_t = jnp.transpose(c_all)
    g_w_ada = ada_wgrad(c_all_t, dmod_sh, "ada_wgrad")
    g_w_ada_kv = ada_wgrad(c_all_t, dmkv_sh, "ada_kv_wgrad")[0]

    small_g = [jnp.stack([jnp.stack(r) for r in d_ln_g]), jnp.stack([jnp.stack(r) for r in d_ln_b]),
               jnp.stack(d_b_in), jnp.stack(d_gln_g), jnp.stack(d_gln_b), jnp.stack(d_rel), jnp.stack(d_ws),
               jnp.stack(d_bs)]
    sg_shapes = [a.shape for a in small_g]
    sg_pack = _pack_rows(small_g)
    sg_all = all_gather8(sg_pack, "ag_small_grads").reshape(N_DEV, -1, LANES)
    sg_sum = _unpack_rows(sum_leading(sg_all, "small_grad_sum"), sg_shapes)
    g_ln_g, g_ln_b, g_b_in, g_gln_g, g_gln_b, g_rel = [_shard_last(a, s_me) for a in sg_sum[:6]]
    g_ws, g_bs = sg_sum[6], sg_sum[7]

    last, _ = start_grad_exchange(list(made), sg_all)

    grads = dict(w_ada=g_w_ada, b_ada=g_b_ada, ln_g=g_ln_g, ln_b=g_ln_b, gmlp_b_in=g_b_in, gmlp_ln_g=g_gln_g,
                 gmlp_ln_b=g_gln_b, gmlp_w_s=g_ws, gmlp_b_s=g_bs, w_ada_kv=g_w_ada_kv, b_ada_kv=g_b_ada_kv,
                 attn_rel_bias=g_rel)
    weights = dict(w_ada=w_ada, b_ada=b_ada, ln_g=ln_g, ln_b=ln_b, ffn_gu=ffn_gu, ffn_down=ffn_down,
                   gmlp_w_in=gmlp_w_in, gmlp_b_in=gmlp_b_in, gmlp_ln_g=gmlp_ln_g, gmlp_ln_b=gmlp_ln_b,
                   gmlp_w_s=gmlp_w_s, gmlp_b_s=gmlp_b_s, gmlp_w_out=gmlp_w_out, w_ada_kv=w_ada_kv,
                   b_ada_kv=b_ada_kv, w_kv=w_kv, attn_w_q=attn_w_q, attn_rel_bias=attn_rel_bias, attn_w_o=attn_w_o)
    ms = dict(w_ada=m_w_ada, b_ada=m_b_ada, ln_g=m_ln_g, ln_b=m_ln_b, ffn_gu=m_ffn_gu, ffn_down=m_ffn_down,
              gmlp_w_in=m_gmlp_w_in, gmlp_b_in=m_gmlp_b_in, gmlp_ln_g=m_gmlp_ln_g, gmlp_ln_b=m_gmlp_ln_b,
              gmlp_w_s=m_gmlp_w_s, gmlp_b_s=m_gmlp_b_s, gmlp_w_out=m_gmlp_w_out, w_ada_kv=m_w_ada_kv,
              b_ada_kv=m_b_ada_kv, w_kv=m_w_kv, attn_w_q=m_attn_w_q, attn_rel_bias=m_attn_rel_bias,
              attn_w_o=m_attn_w_o)
    vs = dict(w_ada=v_w_ada, b_ada=v_b_ada, ln_g=v_ln_g, ln_b=v_ln_b, ffn_gu=v_ffn_gu, ffn_down=v_ffn_down,
              gmlp_w_in=v_gmlp_w_in, gmlp_b_in=v_gmlp_b_in, gmlp_ln_g=v_gmlp_ln_g, gmlp_ln_b=v_gmlp_ln_b,
              gmlp_w_s=v_gmlp_w_s, gmlp_b_s=v_gmlp_b_s, gmlp_w_out=v_gmlp_w_out, w_ada_kv=v_w_ada_kv,
              b_ada_kv=v_b_ada_kv, w_kv=v_w_kv, attn_w_q=v_attn_w_q, attn_rel_bias=v_attn_rel_bias,
              attn_w_o=v_attn_w_o)
    order = ["w_ada", "b_ada", "ln_g", "ln_b", "ffn_gu", "ffn_down", "gmlp_w_in", "gmlp_b_in", "gmlp_ln_g",
             "gmlp_ln_b", "gmlp_w_s", "gmlp_b_s", "gmlp_w_out", "w_ada_kv", "b_ada_kv", "w_kv", "attn_w_q",
             "attn_rel_bias", "attn_w_o"]
    big_names = ["w_ada", "w_ada_kv"] + stack_names
    small_names = [nm for nm in order if nm not in big_names]
    delta, new_m, new_v = {}, {}, {}

    def adamw_big(nm):
        shp = weights[nm].shape
        two_d = (-1, shp[-1])
        d, a, b = adamw(weights[nm].reshape(two_d), grads[nm].reshape(two_d), ms[nm].reshape(two_d),
                        vs[nm].reshape(two_d), "adamw")
        delta[nm], new_m[nm], new_v[nm] = d.reshape(shp), a.reshape(shp), b.reshape(shp)

    adamw_big("w_ada")
    adamw_big("w_ada_kv")
    shapes = [weights[nm].shape for nm in small_names]
    d, a, b = adamw(_pack_rows([weights[nm] for nm in small_names]), _pack_rows([grads[nm] for nm in small_names]),
                    _pack_rows([ms[nm] for nm in small_names]), _pack_rows([vs[nm] for nm in small_names]),
                    "adamw_small")
    for nm, dd, aa, bb in zip(small_names, _unpack_rows(d, shapes), _unpack_rows(a, shapes), _unpack_rows(b, shapes)):
        delta[nm], new_m[nm], new_v[nm] = dd, aa, bb

    def full_grad(u):
        lo = jnp.where(ci == 0, own_half[u], sib_half[u])
        hi = jnp.where(ci == 0, sib_half[u], own_half[u])
        return jnp.concatenate([lo, hi], axis=1 if u in col_split else 0)

    def adamw_stack(nm):
        si = stack_names.index(nm)
        g = jnp.stack([full_grad(unit_of[(nm, k)]) for k in range(stacks[si].shape[0])])
        grads[nm] = g.reshape(weights[nm].shape)
        adamw_big(nm)

    late = [stack_names[units[u][0]] for u in last["ids"]]
    early = [nm for nm in stack_names if nm not in late]
    finish_grad_exchange(pending, delta["w_ada"])
    collect_halves(delta["w_ada"])
    for nm in early:
        adamw_stack(nm)
    finish_grad_exchange(last, delta[early[-1]])
    collect_halves(delta[early[-1]])
    for nm in stack_names:
        if nm in late:
            adamw_stack(nm)

    return (loss, grad_x, *[grads[nm] for nm in order], *[delta[nm] for nm in order],
            *[new_m[nm] for nm in order], *[new_v[nm] for nm in order])
```

```python
import functools

import jax
import jax.numpy as jnp
from jax import lax
from jax.experimental import pallas as pl
from jax.experimental.pallas import tpu as pltpu

F32 = jnp.float32
BF16 = jnp.bfloat16
MESH = pl.DeviceIdType.MESH
HIGHEST = lax.Precision.HIGHEST

CHUNK = 64
GMLP_WINDOW = 128
GMLP_GROUPS = 8
HEAD_DIM = 64
LEFT_CHUNKS = 8
BAND = (LEFT_CHUNKS + 1) * CHUNK
LEFT_PAD = LEFT_CHUNKS * CHUNK
MAX_REL = 4 * CHUNK
N_REL = (CHUNK - 1) + MAX_REL + 1
LN_EPS = 1e-5
N_MOD = 9
N_DEV = 8
N_CHIP = 4

ADAM_LR = 0.001
ADAM_B1 = 0.9
ADAM_B2 = 0.999
ADAM_EPS = 1e-08
ADAM_WD = 0.01
ADAM_STEP = 10

LANES = 128
ROW_TILE = 256
MATMUL_ROW_TILE = 512
WGRAD_ROWS = 2048
ATTN_CHUNKS_PER_STEP = 16
VMEM_LIMIT_MB = 56

NT = (((1,), (1,)), ((), ()))
TN = (((0,), (0,)), ((), ()))

ANY = pl.BlockSpec(memory_space=pl.ANY)
VMEM_SPEC = pl.BlockSpec(memory_space=pltpu.VMEM)


def _params(semantics=None):
    kw = dict(vmem_limit_bytes=VMEM_LIMIT_MB * 1024 * 1024)
    if semantics is not None:
        kw["dimension_semantics"] = semantics
    return pltpu.CompilerParams(**kw)


def _sigmoid(v):
    return 0.5 * (1.0 + jnp.tanh(0.5 * v))


def _gelu(v):
    return 0.5 * v * (1.0 + lax.erf(v * (2.0 ** -0.5)))


def _gelu_grad(v):
    return 0.5 * (1.0 + lax.erf(v * (2.0 ** -0.5))) + v * jnp.exp(-0.5 * v * v) * ((2.0 * jnp.pi) ** -0.5)


def _row(m):
    return lambda i: (i, 0)


def _fixed2(i):
    return (0, 0)


def _fixed3(i):
    return (0, 0, 0)


def _resident(shape):
    return pl.BlockSpec(shape, _fixed2 if len(shape) == 2 else _fixed3, pipeline_mode=pl.Buffered(1))


def _fetch_blocks(w_hbm, w_vmem, sems, first):
    @pl.when(first)
    def _():
        for s in range(w_vmem.shape[0]):
            pltpu.make_async_copy(w_hbm.at[s], w_vmem.at[s], sems.at[s]).start()


def _await_block(w_hbm, w_vmem, sems, s, first):
    @pl.when(first)
    def _():
        pltpu.make_async_copy(w_hbm.at[s], w_vmem.at[s], sems.at[s]).wait()


def mod_matmul(x, scl, shift, w, bias, out_dtype, name):
    S, D = x.shape
    NS, _, n = w.shape
    tm = min(MATMUL_ROW_TILE, S)
    has_bias = bias is not None

    def body(*refs):
        if has_bias:
            x_ref, scl_ref, sh_ref, w_hbm, b_ref, o_ref, h_ref, w_ref, w_sems = refs
        else:
            x_ref, scl_ref, sh_ref, w_hbm, o_ref, h_ref, w_ref, w_sems = refs
        first = pl.program_id(0) == 0
        _fetch_blocks(w_hbm, w_ref, w_sems, first)
        h = (x_ref[...] * (1.0 + scl_ref[...]) + sh_ref[...]).astype(BF16)
        h_ref[...] = h
        for s in range(NS):
            _await_block(w_hbm, w_ref, w_sems, s, first)
            acc = jnp.dot(h, w_ref[s], preferred_element_type=F32)
            if has_bias:
                acc = acc + b_ref[:, s * n:(s + 1) * n]
            o_ref[:, s * n:(s + 1) * n] = acc.astype(out_dtype)

    in_specs = [pl.BlockSpec((tm, D), _row(0)), pl.BlockSpec((1, D), _fixed2), pl.BlockSpec((1, D), _fixed2), ANY]
    args = [x, scl, shift, w]
    if has_bias:
        in_specs.append(pl.BlockSpec((1, NS * n), _fixed2))
        args.append(bias)
    return pl.pallas_call(
        body, name=name, grid=(S // tm,), in_specs=in_specs,
        out_specs=[pl.BlockSpec((tm, NS * n), _row(0)), pl.BlockSpec((tm, D), _row(0))],
        out_shape=[jax.ShapeDtypeStruct((S, NS * n), out_dtype), jax.ShapeDtypeStruct((S, D), BF16)],
        scratch_shapes=[pltpu.VMEM((NS, D, n), w.dtype), pltpu.SemaphoreType.DMA((NS,))],
        compiler_params=_params(("arbitrary",)),
    )(*args)


def matmul_res_ln(a, w, x, gw, lg, lb, alpha, swiglu, name):
    S, D = x.shape
    K = w.shape[0]
    tm = min(ROW_TILE, S)
    ka = a.shape[1]

    def body(a_ref, w_ref, x_ref, gw_ref, lg_ref, lb_ref, xn_ref, xh_ref, rs_ref, y_ref, *act_ref):
        if swiglu:
            g = a_ref[:, :K].astype(F32)
            u = a_ref[:, K:].astype(F32)
            act = (g * _sigmoid(g) * u).astype(BF16)
            act_ref[0][...] = act
        else:
            act = a_ref[...].astype(BF16)
        y = jnp.dot(act, w_ref[...], preferred_element_type=F32)
        z = alpha * x_ref[...] + gw_ref[...] * y
        mu = jnp.mean(z, axis=-1, keepdims=True)
        zc = z - mu
        var = jnp.mean(zc * zc, axis=-1, keepdims=True)
        rstd = lax.rsqrt(var + LN_EPS)
        xhat = zc * rstd
        xn_ref[...] = xhat * lg_ref[...] + lb_ref[...]
        xh_ref[...] = xhat
        rs_ref[...] = rstd
        y_ref[...] = y.astype(BF16)

    vec = pl.BlockSpec((1, D), _fixed2)
    out_specs = [pl.BlockSpec((tm, D), _row(0)), pl.BlockSpec((tm, D), _row(0)), pl.BlockSpec((tm, 1), _row(0)),
                 pl.BlockSpec((tm, D), _row(0))]
    out_shape = [jax.ShapeDtypeStruct((S, D), F32), jax.ShapeDtypeStruct((S, D), F32),
                 jax.ShapeDtypeStruct((S, 1), F32), jax.ShapeDtypeStruct((S, D), BF16)]
    if swiglu:
        out_specs.append(pl.BlockSpec((tm, K), _row(0)))
        out_shape.append(jax.ShapeDtypeStruct((S, K), BF16))
    return pl.pallas_call(
        body, name=name, grid=(S // tm,),
        in_specs=[pl.BlockSpec((tm, ka), _row(0)), _resident((K, D)), pl.BlockSpec((tm, D), _row(0)),
                  vec, vec, vec],
        out_specs=out_specs, out_shape=out_shape,
        compiler_params=_params(("parallel",)),
    )(a, w, x, gw, lg, lb)


def _ln_res_bwd_tile(d, xh_ref, rs_ref, lg_ref, y_ref, gw_ref, wres, alpha, dxa_ref, dy_ref, acc_ref, row0):
    xh = xh_ref[...]
    dxh = d * lg_ref[...]
    m1 = jnp.mean(dxh, axis=-1, keepdims=True)
    m2 = jnp.mean(dxh * xh, axis=-1, keepdims=True)
    dz = rs_ref[...] * (dxh - m1 - xh * m2)
    dxa_ref[...] = alpha * dz
    dy_ref[...] = (gw_ref[...] * dz).astype(BF16)
    acc_ref[row0:row0 + 1, :] += jnp.sum(d * xh, axis=0, keepdims=True)
    acc_ref[row0 + 1:row0 + 2, :] += jnp.sum(d, axis=0, keepdims=True)
    acc_ref[row0 + 2:row0 + 3, :] += jnp.sum((wres * dz) * y_ref[...].astype(F32), axis=0, keepdims=True)


def ln_res_bwd(dxn, xhat, rstd, lg, y, gw, wres, alpha, name):
    S, D = dxn.shape
    tm = min(MATMUL_ROW_TILE, S)

    def body(dxn_ref, xh_ref, rs_ref, lg_ref, y_ref, gw_ref, dxa_ref, dy_ref, acc_ref):
        @pl.when(pl.program_id(0) == 0)
        def _():
            acc_ref[...] = jnp.zeros_like(acc_ref)

        _ln_res_bwd_tile(dxn_ref[...], xh_ref, rs_ref, lg_ref, y_ref, gw_ref, wres, alpha, dxa_ref, dy_ref, acc_ref, 0)

    vec = pl.BlockSpec((1, D), _fixed2)
    tile = pl.BlockSpec((tm, D), _row(0))
    return pl.pallas_call(
        body, name=name, grid=(S // tm,),
        in_specs=[tile, tile, pl.BlockSpec((tm, 1), _row(0)), vec, tile, vec],
        out_specs=[tile, tile, pl.BlockSpec((8, D), _fixed2)],
        out_shape=[jax.ShapeDtypeStruct((S, D), F32), jax.ShapeDtypeStruct((S, D), BF16),
                   jax.ShapeDtypeStruct((8, D), F32)],
        compiler_params=_params(("arbitrary",)),
    )(dxn, xhat, rstd, lg, y, gw)


def ffn_act_bwd(dy, wd, gu, after, name):
    S, D = dy.shape
    K = wd.shape[0]
    tm = min(ROW_TILE, S)

    def body(dy_ref, wd_ref, gu_ref, after_ref, o_ref):
        da = lax.dot_general(dy_ref[...], wd_ref[...], NT, preferred_element_type=F32).astype(BF16)
        g = gu_ref[:, :K]
        u = gu_ref[:, K:]
        sg = _sigmoid(g)
        o_ref[:, :K] = da * u * (sg * (1.0 + g * (1.0 - sg)))
        o_ref[:, K:] = da * (g * sg)

    return pl.pallas_call(
        body, name=name, grid=(S // tm,),
        in_specs=[pl.BlockSpec((tm, D), _row(0)), _resident((K, D)), pl.BlockSpec((tm, 2 * K), _row(0)), ANY],
        out_specs=pl.BlockSpec((tm, 2 * K), _row(0)),
        out_shape=jax.ShapeDtypeStruct((S, 2 * K), BF16),
        compiler_params=_params(("parallel",)),
    )(dy, wd, gu, after)


def matmul_nt(a, w, after, name):
    S, D = a.shape
    K = w.shape[0]
    tm = min(MATMUL_ROW_TILE, S)

    def body(a_ref, w_ref, after_ref, o_ref):
        o_ref[...] = lax.dot_general(a_ref[...], w_ref[...], NT, preferred_element_type=F32).astype(BF16)

    return pl.pallas_call(
        body, name=name, grid=(S // tm,),
        in_specs=[pl.BlockSpec((tm, D), _row(0)), _resident((K, D)), ANY],
        out_specs=pl.BlockSpec((tm, K), _row(0)),
        out_shape=jax.ShapeDtypeStruct((S, K), BF16),
        compiler_params=_params(("parallel",)),
    )(a, w, after)


def dgrad_mod(dpre, w, dxa, xin, scl, prev, alpha, name):
    S, D = xin.shape
    NS, _, n = w.shape
    tm = min(ROW_TILE, S)
    wres = prev[5] if prev is not None else None

    def body(*refs):
        dp_ref, w_hbm, dxa_ref, xin_ref, scl_ref = refs[:5]
        acc_ref, w_ref, w_sems = refs[-3:]
        first = pl.program_id(0) == 0
        _fetch_blocks(w_hbm, w_ref, w_sems, first)

        @pl.when(first)
        def _():
            acc_ref[...] = jnp.zeros_like(acc_ref)

        dh = jnp.zeros((tm, D), F32)
        for s in range(NS):
            _await_block(w_hbm, w_ref, w_sems, s, first)
            dh = dh + lax.dot_general(dp_ref[:, s * n:(s + 1) * n].astype(BF16), w_ref[s], NT,
                                      preferred_element_type=F32)
        dx = dxa_ref[...] + dh * (1.0 + scl_ref[...])
        acc_ref[0:1, :] += jnp.sum(dh * xin_ref[...], axis=0, keepdims=True)
        acc_ref[1:2, :] += jnp.sum(dh, axis=0, keepdims=True)
        if prev is None:
            refs[5][...] = dx
        else:
            xh_ref, rs_ref, lg_ref, y_ref, gw_ref, pdxa_ref, pdy_ref = refs[5:12]
            _ln_res_bwd_tile(dx, xh_ref, rs_ref, lg_ref, y_ref, gw_ref, wres, alpha, pdxa_ref, pdy_ref, acc_ref, 2)

    tile = pl.BlockSpec((tm, D), _row(0))
    vec = pl.BlockSpec((1, D), _fixed2)
    in_specs = [pl.BlockSpec((tm, NS * n), _row(0)), ANY, tile, tile, vec]
    args = [dpre, w, dxa, xin, scl]
    if prev is None:
        out_specs = [tile]
        out_shape = [jax.ShapeDtypeStruct((S, D), F32)]
    else:
        in_specs += [tile, pl.BlockSpec((tm, 1), _row(0)), vec, tile, vec]
        args += list(prev[:5])
        out_specs = [tile, tile]
        out_shape = [jax.ShapeDtypeStruct((S, D), F32), jax.ShapeDtypeStruct((S, D), BF16)]
    return pl.pallas_call(
        body, name=name, grid=(S // tm,), in_specs=in_specs,
        out_specs=out_specs + [pl.BlockSpec((8, D), _fixed2)],
        out_shape=out_shape + [jax.ShapeDtypeStruct((8, D), F32)],
        scratch_shapes=[pltpu.VMEM((NS, D, n), w.dtype), pltpu.SemaphoreType.DMA((NS,))],
        compiler_params=_params(("arbitrary",)),
    )(*args)


PAIR_COLLECTIVE_ID = 1
FILL_COLLECTIVE_ID = 2


def wgrad_pair(a, b, J, kb, nb, a_block, b_block, half_idx, name):
    S = b.shape[0]
    ts = min(WGRAD_ROWS, S)
    nsteps = S // ts

    def body(h_ref, a_ref, b_ref, o_ref, acc_ref, send_buf, recv_buf, send_sems, recv_sems):
        jj, si = pl.program_id(0), pl.program_id(1)
        x, y, c = _coords()
        j = lax.rem(jj, J)
        last = si == nsteps - 1

        def copy(blk):
            return pltpu.make_async_remote_copy(
                src_ref=send_buf.at[blk], dst_ref=recv_buf.at[blk], send_sem=send_sems.at[blk],
                recv_sem=recv_sems.at[blk], device_id=(x, y, 1 - c), device_id_type=MESH)

        @pl.when(jnp.logical_and(jj == 0, si == 0))
        def _():
            barrier = pltpu.get_barrier_semaphore()
            pl.semaphore_signal(barrier, inc=1, device_id=(x, y, 1 - c), device_id_type=MESH)
            pl.semaphore_wait(barrier, 1)

        @pl.when(si == 0)
        def _():
            acc_ref[...] = jnp.zeros_like(acc_ref)

        acc_ref[...] += lax.dot_general(a_ref[...], b_ref[...].astype(BF16), TN, preferred_element_type=F32)

        @pl.when(jnp.logical_and(last, jj < J))
        def _():
            send_buf[j] = acc_ref[...].astype(BF16)
            copy(j).start()

        @pl.when(jnp.logical_and(last, jj >= J))
        def _():
            copy(j).wait_recv()
            o_ref[...] = (acc_ref[...] + recv_buf[j].astype(F32)).astype(BF16)

        @pl.when(jnp.logical_and(last, jj == 2 * J - 1))
        def _():
            for blk in range(J):
                copy(blk).wait_send()

    def half(jj, h):
        return jnp.where(jj < J, 1 - h[0], h[0])

    return pl.pallas_call(
        body, name=name,
        grid_spec=pltpu.PrefetchScalarGridSpec(
            num_scalar_prefetch=1, grid=(2 * J, nsteps),
            in_specs=[pl.BlockSpec((ts, kb), lambda jj, s, h: (s, a_block(lax.rem(jj, J), half(jj, h)))),
                      pl.BlockSpec((ts, nb), lambda jj, s, h: (s, b_block(lax.rem(jj, J), half(jj, h))))],
            out_specs=pl.BlockSpec((None, kb, nb), lambda jj, s, h: (jnp.maximum(jj - J, 0), 0, 0)),
            scratch_shapes=[pltpu.VMEM((kb, nb), F32), pltpu.VMEM((J, kb, nb), BF16), pltpu.VMEM((J, kb, nb), BF16),
                            pltpu.SemaphoreType.DMA((J,)), pltpu.SemaphoreType.DMA((J,))]),
        out_shape=jax.ShapeDtypeStruct((J, kb, nb), BF16),
        compiler_params=pltpu.CompilerParams(
            vmem_limit_bytes=VMEM_LIMIT_MB * 1024 * 1024, dimension_semantics=("arbitrary", "arbitrary"),
            collective_id=PAIR_COLLECTIVE_ID),
    )(half_idx, a, b)


def _window_mask():
    t = lax.broadcasted_iota(jnp.int32, (GMLP_WINDOW, GMLP_WINDOW), 0)
    s = lax.broadcasted_iota(jnp.int32, (GMLP_WINDOW, GMLP_WINDOW), 1)
    return ((s // CHUNK) <= (t // CHUNK)).astype(F32)


def sgu_fwd(pre, glg, glb, ws, bst, name):
    S, H2 = pre.shape
    H = H2 // 2
    W, G = GMLP_WINDOW, GMLP_GROUPS
    gd = H // G
    tm = min(ROW_TILE, S)

    def body(pre_ref, glg_ref, glb_ref, ws_ref, bst_ref, q_ref):
        u = _gelu(pre_ref[:, :H])
        v = _gelu(pre_ref[:, H:])
        mu = jnp.mean(v, axis=-1, keepdims=True)
        vc = v - mu
        var = jnp.mean(vc * vc, axis=-1, keepdims=True)
        vn = ((vc * lax.rsqrt(var + LN_EPS)) * glg_ref[...] + glb_ref[...]).astype(BF16)
        mask = _window_mask()
        for g in range(G):
            wsg = (ws_ref[g] * mask).astype(BF16)
            bcol = bst_ref[:, g:g + 1]
            for wi in range(tm // W):
                rows = slice(wi * W, (wi + 1) * W)
                cols = slice(g * gd, (g + 1) * gd)
                s = jnp.dot(wsg, vn[rows, cols], preferred_element_type=F32) + bcol
                q_ref[rows, cols] = (u[rows, cols] * s).astype(BF16)

    return pl.pallas_call(
        body, name=name, grid=(S // tm,),
        in_specs=[pl.BlockSpec((tm, H2), _row(0)), pl.BlockSpec((1, H), _fixed2), pl.BlockSpec((1, H), _fixed2),
                  pl.BlockSpec((G, W, W), _fixed3), pl.BlockSpec((W, G), _fixed2)],
        out_specs=pl.BlockSpec((tm, H), _row(0)),
        out_shape=jax.ShapeDtypeStruct((S, H), BF16),
        compiler_params=_params(("parallel",)),
    )(pre, glg, glb, ws, bst)


def sgu_bwd(dq, pre, glg, glb, ws, bst, name):
    S, H2 = pre.shape
    H = H2 // 2
    W, G = GMLP_WINDOW, GMLP_GROUPS
    gd = H // G
    tm = min(ROW_TILE, S)

    def body(dq_ref, pre_ref, glg_ref, glb_ref, ws_ref, bst_ref,
             dpre_ref, dws_ref, dss_ref, dgl_ref, dbin_ref, du_s, dvn_s):
        @pl.when(pl.program_id(0) == 0)
        def _():
            dws_ref[...] = jnp.zeros_like(dws_ref)
            dss_ref[...] = jnp.zeros_like(dss_ref)
            dgl_ref[...] = jnp.zeros_like(dgl_ref)
            dbin_ref[...] = jnp.zeros_like(dbin_ref)

        pu = pre_ref[:, :H]
        pv = pre_ref[:, H:]
        u = _gelu(pu)
        v = _gelu(pv)
        mu = jnp.mean(v, axis=-1, keepdims=True)
        vc = v - mu
        var = jnp.mean(vc * vc, axis=-1, keepdims=True)
        rstd = lax.rsqrt(var + LN_EPS)
        vhat = vc * rstd
        vn = (vhat * glg_ref[...] + glb_ref[...]).astype(BF16)
        mask = _window_mask()
        for g in range(G):
            wsg = (ws_ref[g] * mask).astype(BF16)
            bcol = bst_ref[:, g:g + 1]
            cols = slice(g * gd, (g + 1) * gd)
            for wi in range(tm // W):
                rows = slice(wi * W, (wi + 1) * W)
                vblk = vn[rows, cols]
                s = jnp.dot(wsg, vblk, preferred_element_type=F32) + bcol
                dqb = dq_ref[rows, cols].astype(F32)
                du_s[rows, cols] = dqb * s
                ds = dqb * u[rows, cols]
                dss_ref[:, cols] += ds
                dsb = ds.astype(BF16)
                dvn_s[rows, cols] = lax.dot_general(wsg, dsb, TN, preferred_element_type=F32)
                dws_ref[g] += lax.dot_general(dsb, vblk, NT, preferred_element_type=F32) * mask
        dvn = dvn_s[...]
        dgl_ref[0:1, :] += jnp.sum(dvn * vhat, axis=0, keepdims=True)
        dgl_ref[1:2, :] += jnp.sum(dvn, axis=0, keepdims=True)
        dvh = dvn * glg_ref[...]
        m1 = jnp.mean(dvh, axis=-1, keepdims=True)
        m2 = jnp.mean(dvh * vhat, axis=-1, keepdims=True)
        dv = rstd * (dvh - m1 - vhat * m2)
        dpu = du_s[...] * _gelu_grad(pu)
        dpv = dv * _gelu_grad(pv)
        dbin_ref[0:1, :H] += jnp.sum(dpu, axis=0, keepdims=True)
        dbin_ref[0:1, H:] += jnp.sum(dpv, axis=0, keepdims=True)
        dpre_ref[:, :H] = dpu.astype(BF16)
        dpre_ref[:, H:] = dpv.astype(BF16)

    return pl.pallas_call(
        body, name=name, grid=(S // tm,),
        in_specs=[pl.BlockSpec((tm, H), _row(0)), pl.BlockSpec((tm, H2), _row(0)), pl.BlockSpec((1, H), _fixed2),
                  pl.BlockSpec((1, H), _fixed2), pl.BlockSpec((G, W, W), _fixed3), pl.BlockSpec((W, G), _fixed2)],
        out_specs=[pl.BlockSpec((tm, H2), _row(0)), pl.BlockSpec((G, W, W), _fixed3), pl.BlockSpec((W, H), _fixed2),
                   pl.BlockSpec((8, H), _fixed2), pl.BlockSpec((8, H2), _fixed2)],
        out_shape=[jax.ShapeDtypeStruct((S, H2), BF16), jax.ShapeDtypeStruct((G, W, W), F32),
                   jax.ShapeDtypeStruct((W, H), F32), jax.ShapeDtypeStruct((8, H), F32),
                   jax.ShapeDtypeStruct((8, H2), F32)],
        scratch_shapes=[pltpu.VMEM((tm, H), F32), pltpu.VMEM((tm, H), F32)],
        compiler_params=_params(("arbitrary",)),
    )(dq, pre, glg, glb, ws, bst)


def group_lane_sum(dss, name):
    W, H = dss.shape
    gd = H // GMLP_GROUPS

    def body(d_ref, o_ref):
        j = lax.broadcasted_iota(jnp.int32, (H, LANES), 0)
        g = lax.broadcasted_iota(jnp.int32, (H, LANES), 1)
        ind = ((j // gd) == g).astype(F32)
        o_ref[...] = jnp.dot(d_ref[...], ind, preferred_element_type=F32, precision=HIGHEST)

    return pl.pallas_call(
        body, name=name, in_specs=[VMEM_SPEC], out_specs=VMEM_SPEC,
        out_shape=jax.ShapeDtypeStruct((W, LANES), F32), compiler_params=_params(),
    )(dss)


def _attn_load(j, cps, q_ref, k_ref, v_ref):
    r = lax.broadcasted_iota(jnp.int32, (CHUNK, BAND), 1)
    chunks = []
    for cc in range(cps):
        start = pl.multiple_of((j * cps + cc) * CHUNK, CHUNK)
        chunks.append((q_ref[cc * CHUNK:(cc + 1) * CHUNK, :], k_ref[pl.ds(start, BAND), :],
                       v_ref[pl.ds(start, BAND), :], (r + start) >= LEFT_PAD))
    return chunks


def _attn_probs(chunks, b_ref, sels, scale):
    qms = [[jnp.where(sel, q2, jnp.zeros_like(q2)) for sel in sels] for q2, _, _, _ in chunks]
    raw = [[lax.dot_general(qm, k2, NT, preferred_element_type=F32) for qm in qms[cc]]
           for cc, (_, k2, _, _) in enumerate(chunks)]
    probs = []
    for cc, (_, _, _, valid) in enumerate(chunks):
        row = []
        for sub in range(2):
            s = jnp.where(valid, raw[cc][sub] * scale + b_ref[sub], -jnp.inf)
            e = jnp.exp(s - jnp.max(s, axis=-1, keepdims=True))
            row.append(e / jnp.sum(e, axis=-1, keepdims=True))
        probs.append(row)
    return qms, probs


def attn_fwd(q, kpad, vpad, bias, name):
    S, D = q.shape
    HP = D // LANES
    cps = min(ATTN_CHUNKS_PER_STEP, S // CHUNK)
    tq = cps * CHUNK
    scale = HEAD_DIM ** -0.5

    def body(q_ref, k_ref, v_ref, b_ref, o_ref):
        sel0 = lax.broadcasted_iota(jnp.int32, (CHUNK, LANES), 1) < HEAD_DIM
        chunks = _attn_load(pl.program_id(1), cps, q_ref, k_ref, v_ref)
        _, probs = _attn_probs(chunks, b_ref, (sel0, jnp.logical_not(sel0)), scale)
        outs = [[jnp.dot(probs[cc][sub].astype(BF16), v2, preferred_element_type=F32) for sub in range(2)]
                for cc, (_, _, v2, _) in enumerate(chunks)]
        o_ref[...] = jnp.concatenate([jnp.where(sel0, o[0], o[1]) for o in outs], axis=0).astype(BF16)

    kv_spec = pl.BlockSpec((S + LEFT_PAD, LANES), lambda h, j: (0, h))
    return pl.pallas_call(
        body, name=name, grid=(HP, S // tq),
        in_specs=[pl.BlockSpec((tq, LANES), lambda h, j: (j, h)), kv_spec, kv_spec,
                  pl.BlockSpec((2, CHUNK, BAND), lambda h, j: (h, 0, 0))],
        out_specs=pl.BlockSpec((tq, LANES), lambda h, j: (j, h)),
        out_shape=jax.ShapeDtypeStruct((S, D), BF16),
        compiler_params=_params(("parallel", "parallel")),
    )(q, kpad, vpad, bias)


def attn_bwd(q, do, kpad, vpad, bias, dk_in, dv_in, name):
    S, D = q.shape
    HP = D // LANES
    NH = 2 * HP
    cps = min(ATTN_CHUNKS_PER_STEP, S // CHUNK)
    tq = cps * CHUNK
    nj = S // tq
    scale = HEAD_DIM ** -0.5

    def body(q_ref, do_ref, k_ref, v_ref, b_ref, dki_ref, dvi_ref, dq_ref, dk_ref, dv_ref, db_ref, dk_acc, dv_acc):
        j = pl.program_id(1)

        @pl.when(j == 0)
        def _():
            dk_acc[:LEFT_PAD, :] = jnp.zeros((LEFT_PAD, LANES), F32)
            dv_acc[:LEFT_PAD, :] = jnp.zeros((LEFT_PAD, LANES), F32)
            dk_acc[LEFT_PAD:, :] = dki_ref[...]
            dv_acc[LEFT_PAD:, :] = dvi_ref[...]
            db_ref[...] = jnp.zeros_like(db_ref)

        sel0 = lax.broadcasted_iota(jnp.int32, (CHUNK, LANES), 1) < HEAD_DIM
        sels = (sel0, jnp.logical_not(sel0))
        chunks = _attn_load(j, cps, q_ref, k_ref, v_ref)
        pairs = [(cc, sub) for cc in range(cps) for sub in range(2)]
        qms, probs = _attn_probs(chunks, b_ref, sels, scale)
        doms = [[jnp.where(sel, do_ref[cc * CHUNK:(cc + 1) * CHUNK, :], jnp.zeros((CHUNK, LANES), BF16))
                 for sel in sels] for cc in range(cps)]
        dps = {(cc, sub): lax.dot_general(doms[cc][sub], chunks[cc][2], NT, preferred_element_type=F32)
               for cc, sub in pairs}
        dss = {}
        for cc, sub in pairs:
            p = probs[cc][sub]
            dss[cc, sub] = p * (dps[cc, sub] - jnp.sum(dps[cc, sub] * p, axis=-1, keepdims=True))
        dsb = {key: ds.astype(BF16) for key, ds in dss.items()}
        dqs = {(cc, sub): jnp.dot(dsb[cc, sub], chunks[cc][1], preferred_element_type=F32) * scale
               for cc, sub in pairs}
        dks = {(cc, sub): lax.dot_general(dsb[cc, sub], qms[cc][sub], TN, preferred_element_type=F32) * scale
               for cc, sub in pairs}
        dvs = {(cc, sub): lax.dot_general(probs[cc][sub].astype(BF16), doms[cc][sub], TN,
                                          preferred_element_type=F32) for cc, sub in pairs}
        dq_ref[...] = jnp.concatenate([jnp.where(sel0, dqs[cc, 0], dqs[cc, 1]) for cc in range(cps)],
                                      axis=0).astype(BF16)
        for sub in range(2):
            total = dss[0, sub]
            for cc in range(1, cps):
                total = total + dss[cc, sub]
            db_ref[sub] += total
        dk_parts = [dks[cc, 0] + dks[cc, 1] for cc in range(cps)]
        dv_parts = [dvs[cc, 0] + dvs[cc, 1] for cc in range(cps)]

        def window(parts):
            blocks = []
            for rb in range(cps - 1 + BAND // CHUNK):
                acc = None
                for cc in range(cps):
                    b = rb - cc
                    if 0 <= b < BAND // CHUNK:
                        piece = parts[cc][b * CHUNK:(b + 1) * CHUNK, :]
                        acc = piece if acc is None else acc + piece
                blocks.append(acc)
            return jnp.concatenate(blocks, axis=0)

        span = pl.ds(pl.multiple_of(j * cps * CHUNK, CHUNK), (cps - 1) * CHUNK + BAND)
        dk_acc[span, :] += window(dk_parts)
        dv_acc[span, :] += window(dv_parts)

        @pl.when(j == nj - 1)
        def _():
            dk_ref[...] = dk_acc[LEFT_PAD:, :]
            dv_ref[...] = dv_acc[LEFT_PAD:, :]

    q_spec = pl.BlockSpec((tq, LANES), lambda h, j: (j, h))
    kv_spec = pl.BlockSpec((S + LEFT_PAD, LANES), lambda h, j: (0, h))
    col_spec = pl.BlockSpec((S, LANES), lambda h, j: (0, h))
    b_spec = pl.BlockSpec((2, CHUNK, BAND), lambda h, j: (h, 0, 0))
    return pl.pallas_call(
        body, name=name, grid=(HP, nj),
        in_specs=[q_spec, q_spec, kv_spec, kv_spec, b_spec, col_spec, col_spec],
        out_specs=[q_spec, col_spec, col_spec, b_spec],
        out_shape=[jax.ShapeDtypeStruct((S, D), BF16), jax.ShapeDtypeStruct((S, D), F32),
                   jax.ShapeDtypeStruct((S, D), F32), jax.ShapeDtypeStruct((NH, CHUNK, BAND), F32)],
        scratch_shapes=[pltpu.VMEM((S + LEFT_PAD, LANES), F32), pltpu.VMEM((S + LEFT_PAD, LANES), F32)],
        compiler_params=_params(("parallel", "arbitrary")),
    )(q, do, kpad, vpad, bias, dk_in, dv_in)


def _rel_onehot(t):
    r = lax.broadcasted_iota(jnp.int32, (BAND, N_REL), 0)
    i = lax.broadcasted_iota(jnp.int32, (BAND, N_REL), 1)
    idx = jnp.clip(t + LEFT_PAD - r, -(CHUNK - 1), MAX_REL) + (CHUNK - 1)
    return (idx == i).astype(BF16)


def _split3(v):
    hi = v.astype(BF16)
    rest = v - hi.astype(F32)
    mid = rest.astype(BF16)
    return hi, mid, (rest - mid.astype(F32)).astype(BF16)


def _dot_onehot(parts, onehot, dims):
    hi, mid, lo = [lax.dot_general(p, onehot, dims, preferred_element_type=F32) for p in parts]
    return (hi + mid) + lo


def bias_expand(rb, name):
    NH = rb.shape[0]

    def body(rb_ref, o_ref):
        parts = _split3(rb_ref[...])

        def step(t, carry):
            o_ref[t] = _dot_onehot(parts, _rel_onehot(t), NT)
            return carry

        lax.fori_loop(0, CHUNK, step, 0)

    return pl.pallas_call(
        body, name=name, in_specs=[VMEM_SPEC], out_specs=VMEM_SPEC,
        out_shape=jax.ShapeDtypeStruct((CHUNK, NH, BAND), F32), compiler_params=_params(),
    )(rb)


def bias_grad(dsum, name):
    NH = dsum.shape[1]

    def body(d_ref, o_ref):
        def step(t, acc):
            return acc + _dot_onehot(_split3(d_ref[t]), _rel_onehot(t), (((1,), (0,)), ((), ())))

        o_ref[...] = lax.fori_loop(0, CHUNK, step, jnp.zeros((NH, N_REL), F32))

    return pl.pallas_call(
        body, name=name, in_specs=[VMEM_SPEC], out_specs=VMEM_SPEC,
        out_shape=jax.ShapeDtypeStruct((NH, N_REL), F32), compiler_params=_params(),
    )(dsum)


def loss_grad(y, tgt, name):
    S, D = y.shape
    tm = min(MATMUL_ROW_TILE, S)

    def body(y_ref, t_ref, d_ref, acc_ref):
        @pl.when(pl.program_id(0) == 0)
        def _():
            acc_ref[...] = jnp.zeros_like(acc_ref)

        err = y_ref[...] - t_ref[...]
        d_ref[...] = err * (1.0 / D)
        acc_ref[0:1, :] += jnp.sum(err * err, axis=0, keepdims=True)

    tile = pl.BlockSpec((tm, D), _row(0))
    return pl.pallas_call(
        body, name=name, grid=(S // tm,), in_specs=[tile, tile],
        out_specs=[tile, pl.BlockSpec((8, D), _fixed2)],
        out_shape=[jax.ShapeDtypeStruct((S, D), F32), jax.ShapeDtypeStruct((8, D), F32)],
        compiler_params=_params(("arbitrary",)),
    )(y, tgt)


def _col_tile(n):
    for t in (768, 512, 256, 128):
        if n % t == 0:
            return t
    return n


def ada_fwd(c_all, w, b, name):
    L, D, n = w.shape
    tn = _col_tile(n)

    def body(c_ref, w_ref, b_ref, o_ref):
        cv = c_ref[...]
        ca = cv * _sigmoid(cv)
        o_ref[...] = jnp.dot(ca, w_ref[...], preferred_element_type=F32, precision=HIGHEST) + b_ref[...]

    return pl.pallas_call(
        body, name=name, grid=(L, n // tn),
        in_specs=[pl.BlockSpec((N_DEV, D), lambda l, j: (0, 0)), pl.BlockSpec((None, D, tn), lambda l, j: (l, 0, j)),
                  pl.BlockSpec((None, 1, tn), lambda l, j: (l, 0, j))],
        out_specs=pl.BlockSpec((None, N_DEV, tn), lambda l, j: (l, 0, j)),
        out_shape=jax.ShapeDtypeStruct((L, N_DEV, n), F32),
        compiler_params=_params(("parallel", "parallel")),
    )(c_all, w, b)


def ada_wgrad(c_all_t, dmod, name):
    L, _, n = dmod.shape
    D = c_all_t.shape[0]
    tn = _col_tile(n)

    def body(c_ref, d_ref, o_ref):
        cv = c_ref[...]
        ca = cv * _sigmoid(cv)
        o_ref[...] = jnp.dot(ca, d_ref[...], preferred_element_type=F32, precision=HIGHEST)

    return pl.pallas_call(
        body, name=name, grid=(L, n // tn),
        in_specs=[pl.BlockSpec((D, N_DEV), lambda l, j: (0, 0)), pl.BlockSpec((None, N_DEV, tn), lambda l, j: (l, 0, j))],
        out_specs=pl.BlockSpec((None, D, tn), lambda l, j: (l, 0, j)),
        out_shape=jax.ShapeDtypeStruct((L, D, n), F32),
        compiler_params=_params(("parallel", "parallel")),
    )(c_all_t, dmod)


ELEMENTWISE_BLOCK_BYTES = 3 * 1024 * 1024


def _elementwise_rows(rows, row_bytes):
    for t in (4096, 2048, 1024, 512, 256, 128, 64, 32, 16):
        if rows % t == 0 and t * row_bytes <= ELEMENTWISE_BLOCK_BYTES:
            return t
    return rows


def sum_leading(a, name):
    n, M, N = a.shape
    tr = _elementwise_rows(M, n * N * 4)

    def body(a_ref, o_ref):
        acc = a_ref[0]
        for i in range(1, n):
            acc = acc + a_ref[i]
        o_ref[...] = acc

    return pl.pallas_call(
        body, name=name, grid=(M // tr,),
        in_specs=[pl.BlockSpec((n, tr, N), lambda i: (0, i, 0))],
        out_specs=pl.BlockSpec((tr, N), _row(0)),
        out_shape=jax.ShapeDtypeStruct((M, N), F32),
        compiler_params=_params(("parallel",)),
    )(a)


def chip_sum(psum, land, chip_idx, transposed, name):
    n, M, N = psum.shape
    tr = M if transposed else _elementwise_rows(M, N * 8)

    def body(s_ref, p_ref, a_ref, b_ref, c_ref, o_ref):
        total = ((p_ref[...].astype(F32) + a_ref[...].astype(F32)) + b_ref[...].astype(F32)) + c_ref[...].astype(F32)
        o_ref[...] = jnp.transpose(total) if transposed else total

    def entry(j):
        return pl.BlockSpec((None, tr, N), lambda i, s: ((s[0] + j) % n, i, 0))

    out_block, out_dims = ((N, tr), (N, M)) if transposed else ((tr, N), (M, N))
    return pl.pallas_call(
        body, name=name,
        grid_spec=pltpu.PrefetchScalarGridSpec(
            num_scalar_prefetch=1, grid=(M // tr,),
            in_specs=[entry(0), entry(1), entry(2), entry(3)],
            out_specs=pl.BlockSpec(out_block, lambda i, s: (0, 0) if transposed else (i, 0))),
        out_shape=jax.ShapeDtypeStruct(out_dims, F32),
        compiler_params=_params(("parallel",)),
    )(chip_idx, psum, land, land, land)


def adamw(w, g, m, v, name):
    M, N = w.shape
    tr = _elementwise_rows(M, N * 4)
    c1 = 1.0 - ADAM_B1 ** ADAM_STEP
    c2 = 1.0 - ADAM_B2 ** ADAM_STEP

    def body(w_ref, g_ref, m_ref, v_ref, d_ref, nm_ref, nv_ref):
        g = g_ref[...]
        nm = ADAM_B1 * m_ref[...] + (1.0 - ADAM_B1) * g
        nv = ADAM_B2 * v_ref[...] + (1.0 - ADAM_B2) * (g * g)
        d_ref[...] = -ADAM_LR * ((nm / c1) / (jnp.sqrt(nv / c2) + ADAM_EPS) + ADAM_WD * w_ref[...])
        nm_ref[...] = nm
        nv_ref[...] = nv

    spec = pl.BlockSpec((tr, N), _row(0))
    shp = jax.ShapeDtypeStruct((M, N), F32)
    return pl.pallas_call(
        body, name=name, grid=(M // tr,), in_specs=[spec] * 4, out_specs=[spec] * 3, out_shape=[shp] * 3,
        compiler_params=_params(("parallel",)),
    )(w, g, m, v)


def _coords():
    return lax.axis_index("x"), lax.axis_index("y"), lax.axis_index("c")


def all_gather8(block, name):
    m_per, n = block.shape

    def body(x_ref, out_ref, send_sems, recv_sems, local_sem):
        x, y, c = _coords()
        me, sibling = (x, y, c), (x, y, 1 - c)
        chips = [(1 - x, y), (x, 1 - y), (1 - x, 1 - y)]

        def rows(px, py, pc):
            return out_ref.at[pl.ds((4 * px + 2 * py + pc) * m_per, m_per), :]

        def copy(k, blk, to, src=None):
            return pltpu.make_async_remote_copy(
                src_ref=rows(*blk) if src is None else src, dst_ref=rows(*blk),
                send_sem=send_sems.at[k], recv_sem=recv_sems.at[k], device_id=to, device_id_type=MESH)

        mine = pltpu.make_async_copy(x_ref, rows(*me), local_sem)
        mine.start()
        first = [copy(0, me, sibling, src=x_ref)]
        first += [copy(1 + j, me, (*chip, c), src=x_ref) for j, chip in enumerate(chips)]
        for cp in first:
            cp.start()
        passed = [copy(4 + j, (*chip, c), sibling) for j, chip in enumerate(chips)]
        for j, chip in enumerate(chips):
            copy(1 + j, (*chip, c), me).wait_recv()
            passed[j].start()
        copy(0, sibling, me).wait_recv()
        for j, chip in enumerate(chips):
            copy(4 + j, (*chip, 1 - c), me).wait_recv()
        for cp in first + passed:
            cp.wait_send()
        mine.wait()

    return pl.pallas_call(
        body, name=name, in_specs=[VMEM_SPEC], out_specs=VMEM_SPEC,
        out_shape=jax.ShapeDtypeStruct((N_DEV * m_per, n), block.dtype),
        scratch_shapes=[pltpu.SemaphoreType.DMA((7,)), pltpu.SemaphoreType.DMA((7,)), pltpu.SemaphoreType.DMA],
        compiler_params=_params(),
    )(block)


def _other_chips(x, y):
    return [(1 - x, y), (x, 1 - y), (1 - x, 1 - y)]


HBM_SPEC = pl.BlockSpec(memory_space=pltpu.HBM)
SEM_SPEC = pl.BlockSpec(memory_space=pltpu.SEMAPHORE)
DATAFLOW = pltpu.SideEffectType.DATAFLOW_SIDE_EFFECTING


def _chip_peers(x, y, c):
    return [(px, py, c) for px, py in _other_chips(x, y)]


def _sibling_peer(x, y, c):
    return [(x, y, 1 - c)]


def _weight_desc(_, k, land_ref, peer, me):
    h = land_ref.shape[1] // 2
    rows = pl.ds(me[2] * h, h)
    mine = land_ref.at[2 * me[0] + me[1], rows, :]
    return mine, mine, land_ref.at[2 * peer[0] + peer[1], rows, :]


def _grad_desc(psum_ref, k, land_ref, peer, me):
    return psum_ref.at[2 * peer[0] + peer[1]], land_ref.at[2 * me[0] + me[1]], land_ref.at[2 * peer[0] + peer[1]]


def _pair_desc(grad_ref, k, land_ref, peer, me):
    h = land_ref.shape[1]
    return grad_ref.at[:, pl.ds(peer[2] * h, h), :], land_ref, land_ref


def _whole_desc(src_ref, k, land_ref, peer, me):
    return src_ref, land_ref, land_ref


def exchange_start(srcs, lands, units, groups, desc, peers, after, name):
    n_s, n_l, n_g = len(srcs), len(lands), len(groups)
    n_p = len(peers(0, 0, 0))

    def body(*refs):
        s_refs, l_refs = refs[:n_s], refs[n_s:n_s + n_l]
        outs = refs[n_s + n_l + 1:]
        sems, token = outs[:2 * n_g], outs[-1]
        me = _coords()
        for g, ids in enumerate(groups):
            for i, u in enumerate(ids):
                si, k = units[u]
                for j, peer in enumerate(peers(*me)):
                    src, dst, _ = desc(s_refs[si] if s_refs else None, k, l_refs[u], peer, me)
                    pltpu.make_async_remote_copy(
                        src_ref=src, dst_ref=dst, send_sem=sems[2 * g].at[n_p * i + j],
                        recv_sem=sems[2 * g + 1].at[n_p * i + j], device_id=peer, device_id_type=MESH).start()
        token[...] = jnp.zeros_like(token)

    arrs = list(srcs) + list(lands)
    sem_shapes = [pltpu.SemaphoreType.DMA((n_p * len(ids),)) for ids in groups for _ in range(2)]
    outs = pl.pallas_call(
        body, name=name,
        in_specs=[HBM_SPEC] * len(arrs) + [ANY],
        out_specs=[SEM_SPEC] * (2 * n_g) + [HBM_SPEC] * len(arrs) + [VMEM_SPEC],
        out_shape=sem_shapes + [pltpu.HBM(a.shape, a.dtype) for a in arrs] + [jax.ShapeDtypeStruct((8, LANES), F32)],
        input_output_aliases={i: 2 * n_g + i for i in range(len(arrs))},
        compiler_params=pltpu.CompilerParams(has_side_effects=DATAFLOW),
    )(*[pltpu.with_memory_space_constraint(a, pltpu.HBM) for a in arrs], after)
    sems = outs[:2 * n_g]
    thru = outs[2 * n_g:2 * n_g + len(arrs)]
    return sems, list(thru[:n_s]), list(thru[n_s:]), outs[-1]


def exchange_wait(srcs, lands, units, send_sem, recv_sem, desc, peers, after, name):
    n_s, n_l = len(srcs), len(lands)
    n_p = len(peers(0, 0, 0))

    def body(*refs):
        s_refs, l_refs = refs[:n_s], refs[n_s:n_s + n_l]
        send_sems, recv_sems = refs[n_s + n_l], refs[n_s + n_l + 1]
        me = _coords()
        for i, (si, k) in enumerate(units):
            for j, peer in enumerate(peers(*me)):
                src, _, mine = desc(s_refs[si] if s_refs else None, k, l_refs[i], peer, me)
                cp = pltpu.make_async_remote_copy(
                    src_ref=src, dst_ref=mine, send_sem=send_sems.at[n_p * i + j], recv_sem=recv_sems.at[n_p * i + j],
                    device_id=peer, device_id_type=MESH)
                cp.wait_send()
                cp.wait_recv()

    arrs = list(srcs) + list(lands)
    outs = pl.pallas_call(
        body, name=name,
        in_specs=[HBM_SPEC] * len(arrs) + [SEM_SPEC, SEM_SPEC, ANY],
        out_specs=[HBM_SPEC] * len(arrs),
        out_shape=[pltpu.HBM(a.shape, a.dtype) for a in arrs],
        input_output_aliases={i: i for i in range(len(arrs))},
        compiler_params=pltpu.CompilerParams(has_side_effects=DATAFLOW),
    )(*arrs, send_sem, recv_sem, after)
    return list(outs[:n_s]), list(outs[n_s:])


def sibling_fill(lands, name):
    n_u = len(lands)

    def body(*refs):
        ins, outs = refs[:n_u], refs[n_u:2 * n_u]
        send_sems, recv_sems = refs[2 * n_u:]
        x, y, c = _coords()
        barrier = pltpu.get_barrier_semaphore()
        pl.semaphore_signal(barrier, inc=1, device_id=(x, y, 1 - c), device_id_type=MESH)
        pl.semaphore_wait(barrier, 1)
        sends = []
        for u in range(n_u):
            h = ins[u].shape[1] // 2
            for j, (px, py) in enumerate(_other_chips(x, y)):
                part = (2 * px + py, pl.ds(c * h, h), slice(None))
                cp = pltpu.make_async_remote_copy(
                    src_ref=ins[u].at[part], dst_ref=outs[u].at[part], send_sem=send_sems.at[3 * u + j],
                    recv_sem=recv_sems.at[3 * u + j], device_id=(x, y, 1 - c), device_id_type=MESH)
                cp.start()
                sends.append(cp)
        for u in range(n_u):
            h = ins[u].shape[1] // 2
            for j, (px, py) in enumerate(_other_chips(x, y)):
                theirs = (2 * px + py, pl.ds((1 - c) * h, h), slice(None))
                pltpu.make_async_remote_copy(
                    src_ref=ins[u].at[theirs], dst_ref=outs[u].at[theirs], send_sem=send_sems.at[3 * u + j],
                    recv_sem=recv_sems.at[3 * u + j], device_id=(x, y, 1 - c), device_id_type=MESH).wait_recv()
        for cp in sends:
            cp.wait_send()

    return pl.pallas_call(
        body, name=name, in_specs=[ANY] * n_u, out_specs=[ANY] * n_u,
        out_shape=[jax.ShapeDtypeStruct(a.shape, a.dtype) for a in lands],
        input_output_aliases={i: i for i in range(n_u)},
        scratch_shapes=[pltpu.SemaphoreType.DMA((3 * n_u,)), pltpu.SemaphoreType.DMA((3 * n_u,))],
        compiler_params=pltpu.CompilerParams(vmem_limit_bytes=VMEM_LIMIT_MB * 1024 * 1024,
                                             collective_id=FILL_COLLECTIVE_ID),
    )(*lands)


def _pack_rows(parts):
    flat = jnp.concatenate([p.reshape(-1).astype(F32) for p in parts])
    n = flat.shape[0]
    padded = -(-n // (8 * LANES)) * (8 * LANES)
    return jnp.pad(flat, (0, padded - n)).reshape(-1, LANES)


def _unpack_rows(packed, shapes):
    flat = packed.reshape(-1)
    out, off = [], 0
    for s in shapes:
        size = 1
        for d in s:
            size *= d
        out.append(flat[off:off + size].reshape(s))
        off += size
    return out


def _shard_last(full, s_me):
    n = full.shape[-1] // N_CHIP
    return lax.dynamic_slice_in_dim(full, s_me * n, n, axis=full.ndim - 1)


def _unshard_last(g):
    moved = jnp.moveaxis(g, 0, -2)
    return moved.reshape(moved.shape[:-2] + (moved.shape[-2] * moved.shape[-1],))


def kernel(x, c, w_ada, b_ada, ln_g, ln_b, ffn_gu, ffn_down, gmlp_w_in, gmlp_b_in, gmlp_ln_g, gmlp_ln_b, gmlp_w_s, gmlp_b_s, gmlp_w_out, w_ada_kv, b_ada_kv, w_kv, attn_w_q, attn_rel_bias, attn_w_o, loss_target, m_w_ada, m_b_ada, m_ln_g, m_ln_b, m_ffn_gu, m_ffn_down, m_gmlp_w_in, m_gmlp_b_in, m_gmlp_ln_g, m_gmlp_ln_b, m_gmlp_w_s, m_gmlp_b_s, m_gmlp_w_out, m_w_ada_kv, m_b_ada_kv, m_w_kv, m_attn_w_q, m_attn_rel_bias, m_attn_w_o, v_w_ada, v_b_ada, v_ln_g, v_ln_b, v_ffn_gu, v_ffn_down, v_gmlp_w_in, v_gmlp_b_in, v_gmlp_ln_g, v_gmlp_ln_b, v_gmlp_w_s, v_gmlp_b_s, v_gmlp_w_out, v_w_ada_kv, v_b_ada_kv, v_w_kv, v_attn_w_q, v_attn_rel_bias, v_attn_w_o):
    xi, yi, ci = _coords()
    s_me = 2 * xi + yi
    dev = 4 * xi + 2 * yi + ci

    x0 = x[0]
    tgt = loss_target[0]
    S, D = x0.shape
    L = w_ada.shape[0]
    NA = gmlp_w_in.shape[0]
    NB = attn_w_q.shape[0]
    NH = D // HEAD_DIM
    alpha = (2.0 * L) ** 0.25
    n_ada = w_ada.shape[2]
    n_kv = w_ada_kv.shape[1]

    stack_names = ["ffn_gu", "ffn_down", "gmlp_w_in", "gmlp_w_out", "w_kv", "attn_w_q", "attn_w_o"]
    stack_src = dict(ffn_gu=ffn_gu, ffn_down=ffn_down, gmlp_w_in=gmlp_w_in, gmlp_w_out=gmlp_w_out, w_kv=w_kv[None],
                     attn_w_q=attn_w_q, attn_w_o=attn_w_o)
    stacks = [stack_src[nm].reshape((-1,) + stack_src[nm].shape[-2:]) for nm in stack_names]
    units = [(si, k) for si, st in enumerate(stacks) for k in range(st.shape[0])]
    unit_of = {(stack_names[si], k): u for u, (si, k) in enumerate(units)}
    weight_groups = [[("ffn_gu", 0)], [("ffn_down", 0)]]
    for l in range(L):
        mixer = [("gmlp_w_in", l), ("gmlp_w_out", l)] if l < NA else [("attn_w_q", l - NA), ("attn_w_o", l - NA)]
        first, last = [("ffn_gu", 2 * l), ("ffn_down", 2 * l)], [("ffn_gu", 2 * l + 1), ("ffn_down", 2 * l + 1)]
        if l == 0:
            weight_groups += [mixer, last]
        else:
            weight_groups += [([("w_kv", 0)] if l == NA else []) + first, mixer, last]
    weight_groups = [[unit_of[n] for n in names] for names in weight_groups]
    group_of = {u: g for g, ids in enumerate(weight_groups) for u in ids}

    c_all = all_gather8(jnp.broadcast_to(c, (8, D)), "ag_c").reshape(N_DEV, 8, D)[:, 0]
    b_ada_sh = lax.dynamic_slice_in_dim(b_ada, s_me * n_ada, n_ada, axis=1)
    b_kv_sh = lax.dynamic_slice_in_dim(b_ada_kv, s_me * n_kv, n_kv, axis=0)
    mod_part = ada_fwd(c_all, w_ada, b_ada_sh[:, None, :], "ada_fwd")
    mkv_part = ada_fwd(c_all, w_ada_kv[None], b_kv_sh[None, None, :], "ada_kv_fwd")
    part = jnp.concatenate([jnp.transpose(mod_part, (1, 0, 2)).reshape(N_DEV, L * n_ada), mkv_part[0]], axis=1)
    width = part.shape[1]
    pad_w = -(-width // LANES) * LANES - width
    all_part = all_gather8(jnp.pad(part, ((0, 0), (0, pad_w))), "ag_mod").reshape(N_DEV, N_DEV, width + pad_w)
    mine = lax.dynamic_index_in_dim(all_part[0::2], dev, axis=1, keepdims=False)
    mod = jnp.transpose(mine[:, :L * n_ada].reshape(N_CHIP, L, n_ada), (1, 0, 2)).reshape(L, N_MOD, D)
    mkv = mine[:, L * n_ada:width].reshape(2, D)

    def mrow(l, k):
        return mod[l, k][None, :]

    small_shapes = [ln_g.shape, ln_b.shape, gmlp_b_in.shape, gmlp_ln_g.shape, gmlp_ln_b.shape, attn_rel_bias.shape]
    small_pack = _pack_rows([ln_g, ln_b, gmlp_b_in, gmlp_ln_g, gmlp_ln_b, attn_rel_bias])
    small_all = all_gather8(small_pack, "ag_small_params").reshape((N_DEV,) + small_pack.shape)[0::2]
    sm = [_unpack_rows(small_all[s], small_shapes) for s in range(N_CHIP)]
    ln_g_f, ln_b_f, b_in_f, gln_g_f, gln_b_f, rel_f = [
        _unshard_last(jnp.stack([sm[s][i] for s in range(N_CHIP)])) for i in range(len(small_shapes))]

    def landing(u):
        si, k = units[u]
        shard = stacks[si][k]
        if group_of[u] < 2:
            shard = lax.optimization_barrier(shard)
        shard = shard.astype(BF16)
        return lax.dynamic_update_slice(lax.empty((N_CHIP,) + shard.shape, BF16), shard[None], (s_me, 0, 0))

    gathers_done = jnp.concatenate([mod.reshape(-1)[:LANES], small_all.reshape(-1)[:LANES]])
    w_sems, lands_t = {}, {}
    for part, groups, name in ((0, weight_groups[:2], "weight_send_start_first"),
                               (1, weight_groups[2:], "weight_send_start_rest")):
        ids = [u for grp in groups for u in grp]
        local = [[ids.index(u) for u in grp] for grp in groups]
        sems, _, lands, gathers_done = exchange_start([], [landing(u) for u in ids], [units[u] for u in ids], local,
                                                      _weight_desc, _chip_peers, gathers_done, name)
        for i, grp in enumerate(groups):
            w_sems[2 * part + i] = (sems[2 * i], sems[2 * i + 1])
        lands_t.update(zip(ids, lands))
    wg = {}
    latest = [gathers_done]

    def W(nm, k):
        u = unit_of[(nm, k)]
        if u not in wg:
            g = group_of[u]
            ids = weight_groups[g]
            _, got = exchange_wait([], [lands_t[v] for v in ids], [units[v] for v in ids], w_sems[g][0],
                                   w_sems[g][1], _weight_desc, _chip_peers, latest[0], "weight_send_wait_%d" % g)
            wg.update(zip(ids, sibling_fill(got, "weight_sibling_fill")))
        return wg[u]

    def Wrows(nm, k):
        w4 = W(nm, k)
        return w4.reshape(w4.shape[0] * w4.shape[1], w4.shape[2])

    bst = [jnp.transpose(gmlp_b_s[j]) for j in range(NA)]
    biases = {}

    def make_bias(j, dep):
        rel, _ = lax.optimization_barrier((rel_f[j], dep))
        biases[j] = jnp.transpose(bias_expand(rel, "bias_expand"), (1, 0, 2))
        return biases[j]

    saved = []
    xc = x0
    kpad = vpad = xkv = None
    for l in range(L):
        if l == 1 and NB > 1:
            latest[0] = make_bias(1, xc)
        if l == NA:
            xkv = xc
            kv, hkv = mod_matmul(xc, mkv[1][None], mkv[0][None], W("w_kv", 0), None, BF16, "kv_proj")
            kpad = jnp.pad(kv[:, :D], ((LEFT_PAD, 0), (0, 0)))
            vpad = jnp.pad(kv[:, D:], ((LEFT_PAD, 0), (0, 0)))
        sv = {}
        for i in (0, 2):
            k = 2 * l + i // 2
            gu, hv = mod_matmul(xc, mrow(l, 3 * i + 1), mrow(l, 3 * i), W("ffn_gu", k), None, BF16, "ffn_up")
            latest[0] = hv
            gw = 0.5 * (1.0 + mrow(l, 3 * i + 2))
            xn, xh, rs, yv, av = matmul_res_ln(gu, Wrows("ffn_down", k), xc, gw, ln_g_f[l, i][None],
                                               ln_b_f[l, i][None], alpha, True, "ffn_down")
            sv[i] = dict(x=xc, h=hv, gu=gu, a=av, xh=xh, rs=rs, y=yv, gw=gw)
            xc = latest[0] = xn
            if i == 0:
                if l == 0 and NB > 0:
                    latest[0] = make_bias(0, xc)
                gw = 1.0 + mrow(l, 5)
                if l < NA:
                    pre, hv = mod_matmul(xc, mrow(l, 4), mrow(l, 3), W("gmlp_w_in", l), b_in_f[l][None], F32,
                                         "gmlp_in")
                    qv = sgu_fwd(pre, gln_g_f[l][None], gln_b_f[l][None], gmlp_w_s[l], bst[l], "sgu_fwd")
                    xn, xh, rs, yv = matmul_res_ln(qv, Wrows("gmlp_w_out", l), xc, gw, ln_g_f[l, 1][None],
                                                   ln_b_f[l, 1][None], alpha, False, "gmlp_out")
                    sv[1] = dict(x=xc, h=hv, pre=pre, a=qv, xh=xh, rs=rs, y=yv, gw=gw)
                else:
                    j = l - NA
                    if j not in biases:
                        make_bias(j, xc)
                    qh, hv = mod_matmul(xc, mrow(l, 4), mrow(l, 3), Wrows("attn_w_q", j)[None], None, BF16, "attn_q")
                    ov = attn_fwd(qh, kpad, vpad, biases[j], "attn_fwd")
                    xn, xh, rs, yv = matmul_res_ln(ov, Wrows("attn_w_o", j), xc, gw, ln_g_f[l, 1][None],
                                                   ln_b_f[l, 1][None], alpha, False, "attn_out")
                    sv[1] = dict(x=xc, h=hv, q=qh, a=ov, xh=xh, rs=rs, y=yv, gw=gw)
                xc = latest[0] = xn
        saved.append(sv)

    dx, lacc = loss_grad(xc, tgt, "loss_grad")
    loss = lax.psum((0.5 / D) * jnp.sum(lacc[0]), ("x", "y", "c"))

    gpair = [None] * len(units)
    col_split = {u for u, (si, _) in enumerate(units) if stack_names[si] != "ffn_gu"}
    dmod = [[None] * N_MOD for _ in range(L)]
    d_ln_g = [[None] * 3 for _ in range(L)]
    d_ln_b = [[None] * 3 for _ in range(L)]
    d_b_in, d_gln_g, d_gln_b, d_ws, d_bs, d_rel = ([None] * NA, [None] * NA, [None] * NA, [None] * NA, [None] * NA,
                                                  [None] * NB)
    dk = jnp.zeros((S, D), F32)
    dv = jnp.zeros((S, D), F32)
    dmkv = None

    made = []

    core_idx = ci.astype(jnp.int32).reshape(1)
    chip_idx = s_me.astype(jnp.int32).reshape(1)

    def put(nm, k, a, b, name):
        u = unit_of[(nm, k)]
        rows, cols = stacks[units[u][0]].shape[1:]
        if nm == "ffn_gu":
            g = wgrad_pair(b, a, N_CHIP, cols, rows // 2, lambda j, p: j, lambda j, p: p, core_idx, name)
        elif nm in ("gmlp_w_in", "w_kv"):
            g = wgrad_pair(a, b, N_CHIP, rows, cols // 2, lambda j, p: 0, lambda j, p: 2 * j + p, core_idx, name)
        else:
            g = wgrad_pair(a, b, 1, N_CHIP * rows, cols // 2, lambda j, p: 0, lambda j, p: p, core_idx, name)
        gpair[u] = g.reshape((N_CHIP, -1, g.shape[-1]))
        made.append(u)

    own_half, sib_half = {}, {}
    n_started = [0]

    def start_grad_exchange(ids, after):
        psums = [gpair[u] for u in ids]
        n = len(ids)
        tag = n_started[0]
        n_started[0] += 1
        sems, ps_t, q_t, token = exchange_start(psums, [lax.empty(p.shape, p.dtype) for p in psums],
                                                [(i, 0) for i in range(n)], [list(range(n))], _grad_desc, _chip_peers,
                                                after, "grad_send_start_%d" % tag)
        return dict(ids=ids, tag=tag, sems=sems, ps=ps_t, q=q_t), token

    def finish_grad_exchange(pend, after):
        n = len(pend["ids"])
        ps_t, q = exchange_wait(pend["ps"], pend["q"], [(i, 0) for i in range(n)], pend["sems"][0], pend["sems"][1],
                                _grad_desc, _chip_peers, after, "grad_send_wait_%d" % pend["tag"])
        halves = [chip_sum(ps_t[i], q[i], chip_idx, u not in col_split, "grad_chip_sum")
                  for i, u in enumerate(pend["ids"])]
        sems, h_t, land_t, token = exchange_start(halves, [lax.empty(h.shape, h.dtype) for h in halves],
                                              [(i, 0) for i in range(n)], [list(range(n))], _whole_desc,
                                              _sibling_peer, halves[0], "half_send_start_%d" % pend["tag"])
        swaps.append(dict(ids=pend["ids"], tag=pend["tag"], sems=sems, h=h_t, land=land_t))
        return token

    def collect_halves(after):
        for sw in swaps:
            n = len(sw["ids"])
            h, land = exchange_wait(sw["h"], sw["land"], [(i, 0) for i in range(n)], sw["sems"][0], sw["sems"][1],
                                    _whole_desc, _sibling_peer, after, "half_send_wait_%d" % sw["tag"])
            for u, mine, theirs in zip(sw["ids"], h, land):
                own_half[u], sib_half[u] = mine, theirs
        swaps.clear()

    swaps = []
    pending = None
    started_before = jnp.zeros((8, LANES), F32)

    def ln_inputs(l, i):
        t = saved[l][i]
        return (t["xh"], t["rs"], ln_g_f[l, i][None], t["y"], t["gw"], 1.0 if i == 1 else 0.5)

    def record_ln(l, i, acc, row0):
        d_ln_g[l][i], d_ln_b[l][i], dmod[l][3 * i + 2] = acc[row0], acc[row0 + 1], acc[row0 + 2]

    ln_done = None
    for l in reversed(range(L)):
        if l == NA - 1:
            dkv = jnp.concatenate([dk, dv], axis=1)
            put("w_kv", 0, hkv, dkv, "kv_wgrad")
            pdxa, pdy, acc = dgrad_mod(dkv, W("w_kv", 0), dx, xkv, mkv[1][None], ln_inputs(l, 2), alpha, "kv_dgrad")
            dmkv = jnp.stack([acc[1], acc[0]])
            record_ln(l, 2, acc, 2)
            ln_done = (pdxa, pdy)
        sv = saved[l]
        for i in (2, 1, 0):
            t = sv[i]
            if ln_done is None:
                dxa, dy, acc1 = ln_res_bwd(dx, *ln_inputs(l, i), alpha, "ln_res_bwd")
                record_ln(l, i, acc1, 0)
            else:
                dxa, dy = ln_done
                ln_done = None
            before = (l, i - 1) if i > 0 else ((l - 1, 2) if l > 0 and l != NA else None)
            prev = ln_inputs(*before) if before is not None else None
            scl = mrow(l, 3 * i + 1)
            if i != 1:
                k = 2 * l + i // 2
                F = t["gu"].shape[1] // 2
                dgu = ffn_act_bwd(dy, Wrows("ffn_down", k), t["gu"], started_before, "ffn_act_bwd")
                put("ffn_down", k, t["a"], dy, "ffn_down_wgrad")
                put("ffn_gu", k, t["h"], dgu, "ffn_up_wgrad")
                res = dgrad_mod(dgu, W("ffn_gu", k), dxa, t["x"], scl, prev, alpha, "ffn_up_dgrad")
            elif l < NA:
                dq = matmul_nt(dy, Wrows("gmlp_w_out", l), started_before, "gmlp_out_dgrad")
                put("gmlp_w_out", l, t["a"], dy, "gmlp_out_wgrad")
                dpre, dws_l, dss, dgl, dbin = sgu_bwd(dq, t["pre"], gln_g_f[l][None], gln_b_f[l][None], gmlp_w_s[l],
                                                      bst[l], "sgu_bwd")
                d_ws[l] = dws_l
                d_bs[l] = jnp.transpose(group_lane_sum(dss, "sgu_bias_grad")[:, :GMLP_GROUPS])
                d_gln_g[l], d_gln_b[l], d_b_in[l] = dgl[0], dgl[1], dbin[0]
                put("gmlp_w_in", l, t["h"], dpre, "gmlp_in_wgrad")
                res = dgrad_mod(dpre, W("gmlp_w_in", l), dxa, t["x"], scl, prev, alpha, "gmlp_in_dgrad")
            else:
                j = l - NA
                do = matmul_nt(dy, Wrows("attn_w_o", j), started_before, "attn_out_dgrad")
                put("attn_w_o", j, t["a"], dy, "attn_out_wgrad")
                dqh, dk, dv, dbias = attn_bwd(t["q"], do, kpad, vpad, biases[j], dk, dv, "attn_bwd")
                d_rel[j] = bias_grad(jnp.transpose(dbias, (1, 0, 2)), "bias_grad")
                put("attn_w_q", j, t["h"], dqh, "attn_q_wgrad")
                res = dgrad_mod(dqh, Wrows("attn_w_q", j)[None], dxa, t["x"], scl, prev, alpha, "attn_q_dgrad")
            acc2 = res[-1]
            dmod[l][3 * i + 1], dmod[l][3 * i] = acc2[0], acc2[1]
            if before is None:
                dx = res[0]
            else:
                record_ln(*before, acc2, 2)
                ln_done = (res[0], res[1])
            if (i == 0 and l > 0) or (i == 1 and l == 0):
                started, started_before = start_grad_exchange(list(made), acc2)
                made.clear()
                if pending is not None:
                    started_before = started_before + finish_grad_exchange(pending, acc2)
                pending = started
    grad_x = dx[None]

    dvec = _pack_rows([jnp.stack([jnp.stack(r) for r in dmod]), dmkv])
    dvec = lax.optimization_barrier((dvec, [gpair[u] for u in made]))[0]
    n_dvec = L * N_MOD * D + 2 * D
    dall = all_gather8(dvec, "ag_dmod").reshape(N_DEV, -1, LANES)
    db_all = sum_leading(dall, "ada_bias_grad").reshape(-1)[:n_dvec]
    g_b_ada = db_all[:L * N_MOD * D].reshape(L, N_MOD * D)
    g_b_ada_kv = db_all[L * N_MOD * D:]
    dall2 = dall.reshape(N_DEV, -1)[:, :n_dvec]
    dmod_all = dall2[:, :L * N_MOD * D].reshape(N_DEV, L, N_MOD * D)
    dmod_sh = jnp.transpose(lax.dynamic_slice_in_dim(dmod_all, s_me * n_ada, n_ada, axis=2), (1, 0, 2))
    dmkv_sh = lax.dynamic_slice_in_dim(dall2[:, L * N_MOD * D:], s_me * n_kv, n_kv, axis=1)[None]
    c_all_t = jnp.transpose(c_all)
    g_w_ada = ada_wgrad(c_all_t, dmod_sh, "ada_wgrad")
    g_w_ada_kv = ada_wgrad(c_all_t, dmkv_sh, "ada_kv_wgrad")[0]

    small_g = [jnp.stack([jnp.stack(r) for r in d_ln_g]), jnp.stack([jnp.stack(r) for r in d_ln_b]),
               jnp.stack(d_b_in), jnp.stack(d_gln_g), jnp.stack(d_gln_b), jnp.stack(d_rel), jnp.stack(d_ws),
               jnp.stack(d_bs)]
    sg_shapes = [a.shape for a in small_g]
    sg_pack = _pack_rows(small_g)
    sg_all = all_gather8(sg_pack, "ag_small_grads").reshape(N_DEV, -1, LANES)
    sg_sum = _unpack_rows(sum_leading(sg_all, "small_grad_sum"), sg_shapes)
    g_ln_g, g_ln_b, g_b_in, g_gln_g, g_gln_b, g_rel = [_shard_last(a, s_me) for a in sg_sum[:6]]
    g_ws, g_bs = sg_sum[6], sg_sum[7]

    last, _ = start_grad_exchange(list(made), sg_all)

    grads = dict(w_ada=g_w_ada, b_ada=g_b_ada, ln_g=g_ln_g, ln_b=g_ln_b, gmlp_b_in=g_b_in, gmlp_ln_g=g_gln_g,
                 gmlp_ln_b=g_gln_b, gmlp_w_s=g_ws, gmlp_b_s=g_bs, w_ada_kv=g_w_ada_kv, b_ada_kv=g_b_ada_kv,
                 attn_rel_bias=g_rel)
    weights = dict(w_ada=w_ada, b_ada=b_ada, ln_g=ln_g, ln_b=ln_b, ffn_gu=ffn_gu, ffn_down=ffn_down,
                   gmlp_w_in=gmlp_w_in, gmlp_b_in=gmlp_b_in, gmlp_ln_g=gmlp_ln_g, gmlp_ln_b=gmlp_ln_b,
                   gmlp_w_s=gmlp_w_s, gmlp_b_s=gmlp_b_s, gmlp_w_out=gmlp_w_out, w_ada_kv=w_ada_kv,
                   b_ada_kv=b_ada_kv, w_kv=w_kv, attn_w_q=attn_w_q, attn_rel_bias=attn_rel_bias, attn_w_o=attn_w_o)
    ms = dict(w_ada=m_w_ada, b_ada=m_b_ada, ln_g=m_ln_g, ln_b=m_ln_b, ffn_gu=m_ffn_gu, ffn_down=m_ffn_down,
              gmlp_w_in=m_gmlp_w_in, gmlp_b_in=m_gmlp_b_in, gmlp_ln_g=m_gmlp_ln_g, gmlp_ln_b=m_gmlp_ln_b,
              gmlp_w_s=m_gmlp_w_s, gmlp_b_s=m_gmlp_b_s, gmlp_w_out=m_gmlp_w_out, w_ada_kv=m_w_ada_kv,
              b_ada_kv=m_b_ada_kv, w_kv=m_w_kv, attn_w_q=m_attn_w_q, attn_rel_bias=m_attn_rel_bias,
              attn_w_o=m_attn_w_o)
    vs = dict(w_ada=v_w_ada, b_ada=v_b_ada, ln_g=v_ln_g, ln_b=v_ln_b, ffn_gu=v_ffn_gu, ffn_down=v_ffn_down,
              gmlp_w_in=v_gmlp_w_in, gmlp_b_in=v_gmlp_b_in, gmlp_ln_g=v_gmlp_ln_g, gmlp_ln_b=v_gmlp_ln_b,
              gmlp_w_s=v_gmlp_w_s, gmlp_b_s=v_gmlp_b_s, gmlp_w_out=v_gmlp_w_out, w_ada_kv=v_w_ada_kv,
              b_ada_kv=v_b_ada_kv, w_kv=v_w_kv, attn_w_q=v_attn_w_q, attn_rel_bias=v_attn_rel_bias,
              attn_w_o=v_attn_w_o)
    order = ["w_ada", "b_ada", "ln_g", "ln_b", "ffn_gu", "ffn_down", "gmlp_w_in", "gmlp_b_in", "gmlp_ln_g",
             "gmlp_ln_b", "gmlp_w_s", "gmlp_b_s", "gmlp_w_out", "w_ada_kv", "b_ada_kv", "w_kv", "attn_w_q",
             "attn_rel_bias", "attn_w_o"]
    big_names = ["w_ada", "w_ada_kv"] + stack_names
    small_names = [nm for nm in order if nm not in big_names]
    delta, new_m, new_v = {}, {}, {}

    def adamw_big(nm):
        shp = weights[nm].shape
        two_d = (-1, shp[-1])
        d, a, b = adamw(weights[nm].reshape(two_d), grads[nm].reshape(two_d), ms[nm].reshape(two_d),
                        vs[nm].reshape(two_d), "adamw")
        delta[nm], new_m[nm], new_v[nm] = d.reshape(shp), a.reshape(shp), b.reshape(shp)

    adamw_big("w_ada")
    adamw_big("w_ada_kv")
    shapes = [weights[nm].shape for nm in small_names]
    d, a, b = adamw(_pack_rows([weights[nm] for nm in small_names]), _pack_rows([grads[nm] for nm in small_names]),
                    _pack_rows([ms[nm] for nm in small_names]), _pack_rows([vs[nm] for nm in small_names]),
                    "adamw_small")
    for nm, dd, aa, bb in zip(small_names, _unpack_rows(d, shapes), _unpack_rows(a, shapes), _unpack_rows(b, shapes)):
        delta[nm], new_m[nm], new_v[nm] = dd, aa, bb

    def full_grad(u):
        lo = jnp.where(ci == 0, own_half[u], sib_half[u])
        hi = jnp.where(ci == 0, sib_half[u], own_half[u])
        return jnp.concatenate([lo, hi], axis=1 if u in col_split else 0)

    def adamw_stack(nm):
        si = stack_names.index(nm)
        g = jnp.stack([full_grad(unit_of[(nm, k)]) for k in range(stacks[si].shape[0])])
        grads[nm] = g.reshape(weights[nm].shape)
        adamw_big(nm)

    late = [stack_names[units[u][0]] for u in last["ids"]]
    early = [nm for nm in stack_names if nm not in late]
    finish_grad_exchange(pending, delta["w_ada"])
    collect_halves(delta["w_ada"])
    for nm in early:
        adamw_stack(nm)
    finish_grad_exchange(last, delta[early[-1]])
    collect_halves(delta[early[-1]])
    for nm in stack_names:
        if nm in late:
            adamw_stack(nm)

    return (loss, grad_x, *[grads[nm] for nm in order], *[delta[nm] for nm in order],
            *[new_m[nm] for nm in order], *[new_v[nm] for nm in order])
```

```python
import functools

import jax
import jax.numpy as jnp
from jax import lax
from jax.experimental import pallas as pl
from jax.experimental.pallas import tpu as pltpu

F32 = jnp.float32
BF16 = jnp.bfloat16
MESH = pl.DeviceIdType.MESH
HIGHEST = lax.Precision.HIGHEST

CHUNK = 64
GMLP_WINDOW = 128
GMLP_GROUPS = 8
HEAD_DIM = 64
LEFT_CHUNKS = 8
BAND = (LEFT_CHUNKS + 1) * CHUNK
LEFT_PAD = LEFT_CHUNKS * CHUNK
MAX_REL = 4 * CHUNK
N_REL = (CHUNK - 1) + MAX_REL + 1
LN_EPS = 1e-5
N_MOD = 9
N_DEV = 8
N_CHIP = 4

ADAM_LR = 0.001
ADAM_B1 = 0.9
ADAM_B2 = 0.999
ADAM_EPS = 1e-08
ADAM_WD = 0.01
ADAM_STEP = 10

LANES = 128
ROW_TILE = 256
MATMUL_ROW_TILE = 512
WGRAD_ROWS = 2048
ATTN_CHUNKS_PER_STEP = 16
VMEM_LIMIT_MB = 56

NT = (((1,), (1,)), ((), ()))
TN = (((0,), (0,)), ((), ()))

ANY = pl.BlockSpec(memory_space=pl.ANY)
VMEM_SPEC = pl.BlockSpec(memory_space=pltpu.VMEM)


def _params(semantics=None):
    kw = dict(vmem_limit_bytes=VMEM_LIMIT_MB * 1024 * 1024)
    if semantics is not None:
        kw["dimension_semantics"] = semantics
    return pltpu.CompilerParams(**kw)


def _sigmoid(v):
    return 0.5 * (1.0 + jnp.tanh(0.5 * v))


def _gelu(v):
    return 0.5 * v * (1.0 + lax.erf(v * (2.0 ** -0.5)))


def _gelu_grad(v):
    return 0.5 * (1.0 + lax.erf(v * (2.0 ** -0.5))) + v * jnp.exp(-0.5 * v * v) * ((2.0 * jnp.pi) ** -0.5)


def _row(m):
    return lambda i: (i, 0)


def _fixed2(i):
    return (0, 0)


def _fixed3(i):
    return (0, 0, 0)


def _resident(shape):
    return pl.BlockSpec(shape, _fixed2 if len(shape) == 2 else _fixed3, pipeline_mode=pl.Buffered(1))


def mod_matmul(x, scl, shift, w, bias, out_dtype, name):
    S, D = x.shape
    NS, _, n = w.shape
    tm = min(MATMUL_ROW_TILE, S)
    has_bias = bias is not None

    def body(*refs):
        if has_bias:
            x_ref, scl_ref, sh_ref, w_ref, b_ref, o_ref, h_ref = refs
        else:
            x_ref, scl_ref, sh_ref, w_ref, o_ref, h_ref = refs
        h = (x_ref[...] * (1.0 + scl_ref[...]) + sh_ref[...]).astype(BF16)
        h_ref[...] = h
        for s in range(NS):
            acc = jnp.dot(h, w_ref[s], preferred_element_type=F32)
            if has_bias:
                acc = acc + b_ref[:, s * n:(s + 1) * n]
            o_ref[:, s * n:(s + 1) * n] = acc.astype(out_dtype)

    in_specs = [pl.BlockSpec((tm, D), _row(0)), pl.BlockSpec((1, D), _fixed2), pl.BlockSpec((1, D), _fixed2),
                _resident((NS, D, n))]
    args = [x, scl, shift, w]
    if has_bias:
        in_specs.append(pl.BlockSpec((1, NS * n), _fixed2))
        args.append(bias)
    return pl.pallas_call(
        body, name=name, grid=(S // tm,), in_specs=in_specs,
        out_specs=[pl.BlockSpec((tm, NS * n), _row(0)), pl.BlockSpec((tm, D), _row(0))],
        out_shape=[jax.ShapeDtypeStruct((S, NS * n), out_dtype), jax.ShapeDtypeStruct((S, D), BF16)],
        compiler_params=_params(("parallel",)),
    )(*args)


def matmul_res_ln(a, w, x, gw, lg, lb, alpha, swiglu, name):
    S, D = x.shape
    K = w.shape[0]
    tm = min(ROW_TILE, S)
    ka = a.shape[1]

    def body(a_ref, w_ref, x_ref, gw_ref, lg_ref, lb_ref, xn_ref, xh_ref, rs_ref, y_ref, *act_ref):
        if swiglu:
            g = a_ref[:, :K].astype(F32)
            u = a_ref[:, K:].astype(F32)
            act = (g * _sigmoid(g) * u).astype(BF16)
            act_ref[0][...] = act
        else:
            act = a_ref[...].astype(BF16)
        y = jnp.dot(act, w_ref[...], preferred_element_type=F32)
        z = alpha * x_ref[...] + gw_ref[...] * y
        mu = jnp.mean(z, axis=-1, keepdims=True)
        zc = z - mu
        var = jnp.mean(zc * zc, axis=-1, keepdims=True)
        rstd = lax.rsqrt(var + LN_EPS)
        xhat = zc * rstd
        xn_ref[...] = xhat * lg_ref[...] + lb_ref[...]
        xh_ref[...] = xhat
        rs_ref[...] = rstd
        y_ref[...] = y.astype(BF16)

    vec = pl.BlockSpec((1, D), _fixed2)
    out_specs = [pl.BlockSpec((tm, D), _row(0)), pl.BlockSpec((tm, D), _row(0)), pl.BlockSpec((tm, 1), _row(0)),
                 pl.BlockSpec((tm, D), _row(0))]
    out_shape = [jax.ShapeDtypeStruct((S, D), F32), jax.ShapeDtypeStruct((S, D), F32),
                 jax.ShapeDtypeStruct((S, 1), F32), jax.ShapeDtypeStruct((S, D), BF16)]
    if swiglu:
        out_specs.append(pl.BlockSpec((tm, K), _row(0)))
        out_shape.append(jax.ShapeDtypeStruct((S, K), BF16))
    return pl.pallas_call(
        body, name=name, grid=(S // tm,),
        in_specs=[pl.BlockSpec((tm, ka), _row(0)), _resident((K, D)), pl.BlockSpec((tm, D), _row(0)),
                  vec, vec, vec],
        out_specs=out_specs, out_shape=out_shape,
        compiler_params=_params(("parallel",)),
    )(a, w, x, gw, lg, lb)


def _ln_res_bwd_tile(d, xh_ref, rs_ref, lg_ref, y_ref, gw_ref, wres, alpha, dxa_ref, dy_ref, acc_ref, row0):
    xh = xh_ref[...]
    dxh = d * lg_ref[...]
    m1 = jnp.mean(dxh, axis=-1, keepdims=True)
    m2 = jnp.mean(dxh * xh, axis=-1, keepdims=True)
    dz = rs_ref[...] * (dxh - m1 - xh * m2)
    dxa_ref[...] = alpha * dz
    dy_ref[...] = (gw_ref[...] * dz).astype(BF16)
    acc_ref[row0:row0 + 1, :] += jnp.sum(d * xh, axis=0, keepdims=True)
    acc_ref[row0 + 1:row0 + 2, :] += jnp.sum(d, axis=0, keepdims=True)
    acc_ref[row0 + 2:row0 + 3, :] += jnp.sum((wres * dz) * y_ref[...].astype(F32), axis=0, keepdims=True)


def ln_res_bwd(dxn, xhat, rstd, lg, y, gw, wres, alpha, name):
    S, D = dxn.shape
    tm = min(MATMUL_ROW_TILE, S)

    def body(dxn_ref, xh_ref, rs_ref, lg_ref, y_ref, gw_ref, dxa_ref, dy_ref, acc_ref):
        @pl.when(pl.program_id(0) == 0)
        def _():
            acc_ref[...] = jnp.zeros_like(acc_ref)

        _ln_res_bwd_tile(dxn_ref[...], xh_ref, rs_ref, lg_ref, y_ref, gw_ref, wres, alpha, dxa_ref, dy_ref, acc_ref, 0)

    vec = pl.BlockSpec((1, D), _fixed2)
    tile = pl.BlockSpec((tm, D), _row(0))
    return pl.pallas_call(
        body, name=name, grid=(S // tm,),
        in_specs=[tile, tile, pl.BlockSpec((tm, 1), _row(0)), vec, tile, vec],
        out_specs=[tile, tile, pl.BlockSpec((8, D), _fixed2)],
        out_shape=[jax.ShapeDtypeStruct((S, D), F32), jax.ShapeDtypeStruct((S, D), BF16),
                   jax.ShapeDtypeStruct((8, D), F32)],
        compiler_params=_params(("arbitrary",)),
    )(dxn, xhat, rstd, lg, y, gw)


def ffn_act_bwd(dy, wd, gu, after, name):
    S, D = dy.shape
    K = wd.shape[0]
    tm = min(ROW_TILE, S)

    def body(dy_ref, wd_ref, gu_ref, after_ref, o_ref):
        da = lax.dot_general(dy_ref[...], wd_ref[...], NT, preferred_element_type=F32).astype(BF16)
        g = gu_ref[:, :K]
        u = gu_ref[:, K:]
        sg = _sigmoid(g)
        o_ref[:, :K] = da * u * (sg * (1.0 + g * (1.0 - sg)))
        o_ref[:, K:] = da * (g * sg)

    return pl.pallas_call(
        body, name=name, grid=(S // tm,),
        in_specs=[pl.BlockSpec((tm, D), _row(0)), _resident((K, D)), pl.BlockSpec((tm, 2 * K), _row(0)), ANY],
        out_specs=pl.BlockSpec((tm, 2 * K), _row(0)),
        out_shape=jax.ShapeDtypeStruct((S, 2 * K), BF16),
        compiler_params=_params(("parallel",)),
    )(dy, wd, gu, after)


def matmul_nt(a, w, after, name):
    S, D = a.shape
    K = w.shape[0]
    tm = min(MATMUL_ROW_TILE, S)

    def body(a_ref, w_ref, after_ref, o_ref):
        o_ref[...] = lax.dot_general(a_ref[...], w_ref[...], NT, preferred_element_type=F32).astype(BF16)

    return pl.pallas_call(
        body, name=name, grid=(S // tm,),
        in_specs=[pl.BlockSpec((tm, D), _row(0)), _resident((K, D)), ANY],
        out_specs=pl.BlockSpec((tm, K), _row(0)),
        out_shape=jax.ShapeDtypeStruct((S, K), BF16),
        compiler_params=_params(("parallel",)),
    )(a, w, after)


def dgrad_mod(dpre, w, dxa, xin, scl, prev, alpha, name):
    S, D = xin.shape
    NS, _, n = w.shape
    tm = min(ROW_TILE, S)
    wres = prev[5] if prev is not None else None

    def body(*refs):
        dp_ref, w_ref, dxa_ref, xin_ref, scl_ref = refs[:5]
        acc_ref = refs[-1]

        @pl.when(pl.program_id(0) == 0)
        def _():
            acc_ref[...] = jnp.zeros_like(acc_ref)

        dh = jnp.zeros((tm, D), F32)
        for s in range(NS):
            dh = dh + lax.dot_general(dp_ref[:, s * n:(s + 1) * n].astype(BF16), w_ref[s], NT,
                                      preferred_element_type=F32)
        dx = dxa_ref[...] + dh * (1.0 + scl_ref[...])
        acc_ref[0:1, :] += jnp.sum(dh * xin_ref[...], axis=0, keepdims=True)
        acc_ref[1:2, :] += jnp.sum(dh, axis=0, keepdims=True)
        if prev is None:
            refs[5][...] = dx
        else:
            xh_ref, rs_ref, lg_ref, y_ref, gw_ref, pdxa_ref, pdy_ref = refs[5:12]
            _ln_res_bwd_tile(dx, xh_ref, rs_ref, lg_ref, y_ref, gw_ref, wres, alpha, pdxa_ref, pdy_ref, acc_ref, 2)

    tile = pl.BlockSpec((tm, D), _row(0))
    vec = pl.BlockSpec((1, D), _fixed2)
    in_specs = [pl.BlockSpec((tm, NS * n), _row(0)), _resident((NS, D, n)), tile, tile, vec]
    args = [dpre, w, dxa, xin, scl]
    if prev is None:
        out_specs = [tile]
        out_shape = [jax.ShapeDtypeStruct((S, D), F32)]
    else:
        in_specs += [tile, pl.BlockSpec((tm, 1), _row(0)), vec, tile, vec]
        args += list(prev[:5])
        out_specs = [tile, tile]
        out_shape = [jax.ShapeDtypeStruct((S, D), F32), jax.ShapeDtypeStruct((S, D), BF16)]
    return pl.pallas_call(
        body, name=name, grid=(S // tm,), in_specs=in_specs,
        out_specs=out_specs + [pl.BlockSpec((8, D), _fixed2)],
        out_shape=out_shape + [jax.ShapeDtypeStruct((8, D), F32)],
        compiler_params=_params(("arbitrary",)),
    )(*args)


PAIR_COLLECTIVE_ID = 1
FILL_COLLECTIVE_ID = 2


def wgrad_pair(a, b, J, kb, nb, a_block, b_block, half_idx, name):
    S = b.shape[0]
    ts = min(WGRAD_ROWS, S)
    nsteps = S // ts

    def body(h_ref, a_ref, b_ref, o_ref, acc_ref, send_buf, recv_buf, send_sems, recv_sems):
        jj, si = pl.program_id(0), pl.program_id(1)
        x, y, c = _coords()
        j = lax.rem(jj, J)
        last = si == nsteps - 1

        def copy(blk):
            return pltpu.make_async_remote_copy(
                src_ref=send_buf.at[blk], dst_ref=recv_buf.at[blk], send_sem=send_sems.at[blk],
                recv_sem=recv_sems.at[blk], device_id=(x, y, 1 - c), device_id_type=MESH)

        @pl.when(jnp.logical_and(jj == 0, si == 0))
        def _():
            barrier = pltpu.get_barrier_semaphore()
            pl.semaphore_signal(barrier, inc=1, device_id=(x, y, 1 - c), device_id_type=MESH)
            pl.semaphore_wait(barrier, 1)

        @pl.when(si == 0)
        def _():
            acc_ref[...] = jnp.zeros_like(acc_ref)

        acc_ref[...] += lax.dot_general(a_ref[...], b_ref[...].astype(BF16), TN, preferred_element_type=F32)

        @pl.when(jnp.logical_and(last, jj < J))
        def _():
            send_buf[j] = acc_ref[...].astype(BF16)
            copy(j).start()

        @pl.when(jnp.logical_and(last, jj >= J))
        def _():
            copy(j).wait_recv()
            o_ref[...] = (acc_ref[...] + recv_buf[j].astype(F32)).astype(BF16)

        @pl.when(jnp.logical_and(last, jj == 2 * J - 1))
        def _():
            for blk in range(J):
                copy(blk).wait_send()

    def half(jj, h):
        return jnp.where(jj < J, 1 - h[0], h[0])

    return pl.pallas_call(
        body, name=name,
        grid_spec=pltpu.PrefetchScalarGridSpec(
            num_scalar_prefetch=1, grid=(2 * J, nsteps),
            in_specs=[pl.BlockSpec((ts, kb), lambda jj, s, h: (s, a_block(lax.rem(jj, J), half(jj, h)))),
                      pl.BlockSpec((ts, nb), lambda jj, s, h: (s, b_block(lax.rem(jj, J), half(jj, h))))],
            out_specs=pl.BlockSpec((None, kb, nb), lambda jj, s, h: (jnp.maximum(jj - J, 0), 0, 0)),
            scratch_shapes=[pltpu.VMEM((kb, nb), F32), pltpu.VMEM((J, kb, nb), BF16), pltpu.VMEM((J, kb, nb), BF16),
                            pltpu.SemaphoreType.DMA((J,)), pltpu.SemaphoreType.DMA((J,))]),
        out_shape=jax.ShapeDtypeStruct((J, kb, nb), BF16),
        compiler_params=pltpu.CompilerParams(
            vmem_limit_bytes=VMEM_LIMIT_MB * 1024 * 1024, dimension_semantics=("arbitrary", "arbitrary"),
            collective_id=PAIR_COLLECTIVE_ID),
    )(half_idx, a, b)


def _window_mask():
    t = lax.broadcasted_iota(jnp.int32, (GMLP_WINDOW, GMLP_WINDOW), 0)
    s = lax.broadcasted_iota(jnp.int32, (GMLP_WINDOW, GMLP_WINDOW), 1)
    return ((s // CHUNK) <= (t // CHUNK)).astype(F32)


def sgu_fwd(pre, glg, glb, ws, bst, name):
    S, H2 = pre.shape
    H = H2 // 2
    W, G = GMLP_WINDOW, GMLP_GROUPS
    gd = H // G
    tm = min(ROW_TILE, S)

    def body(pre_ref, glg_ref, glb_ref, ws_ref, bst_ref, q_ref):
        u = _gelu(pre_ref[:, :H])
        v = _gelu(pre_ref[:, H:])
        mu = jnp.mean(v, axis=-1, keepdims=True)
        vc = v - mu
        var = jnp.mean(vc * vc, axis=-1, keepdims=True)
        vn = ((vc * lax.rsqrt(var + LN_EPS)) * glg_ref[...] + glb_ref[...]).astype(BF16)
        mask = _window_mask()
        for g in range(G):
            wsg = (ws_ref[g] * mask).astype(BF16)
            bcol = bst_ref[:, g:g + 1]
            for wi in range(tm // W):
                rows = slice(wi * W, (wi + 1) * W)
                cols = slice(g * gd, (g + 1) * gd)
                s = jnp.dot(wsg, vn[rows, cols], preferred_element_type=F32) + bcol
                q_ref[rows, cols] = (u[rows, cols] * s).astype(BF16)

    return pl.pallas_call(
        body, name=name, grid=(S // tm,),
        in_specs=[pl.BlockSpec((tm, H2), _row(0)), pl.BlockSpec((1, H), _fixed2), pl.BlockSpec((1, H), _fixed2),
                  pl.BlockSpec((G, W, W), _fixed3), pl.BlockSpec((W, G), _fixed2)],
        out_specs=pl.BlockSpec((tm, H), _row(0)),
        out_shape=jax.ShapeDtypeStruct((S, H), BF16),
        compiler_params=_params(("parallel",)),
    )(pre, glg, glb, ws, bst)


def sgu_bwd(dq, pre, glg, glb, ws, bst, name):
    S, H2 = pre.shape
    H = H2 // 2
    W, G = GMLP_WINDOW, GMLP_GROUPS
    gd = H // G
    tm = min(ROW_TILE, S)

    def body(dq_ref, pre_ref, glg_ref, glb_ref, ws_ref, bst_ref,
             dpre_ref, dws_ref, dss_ref, dgl_ref, dbin_ref, du_s, dvn_s):
        @pl.when(pl.program_id(0) == 0)
        def _():
            dws_ref[...] = jnp.zeros_like(dws_ref)
            dss_ref[...] = jnp.zeros_like(dss_ref)
            dgl_ref[...] = jnp.zeros_like(dgl_ref)
            dbin_ref[...] = jnp.zeros_like(dbin_ref)

        pu = pre_ref[:, :H]
        pv = pre_ref[:, H:]
        u = _gelu(pu)
        v = _gelu(pv)
        mu = jnp.mean(v, axis=-1, keepdims=True)
        vc = v - mu
        var = jnp.mean(vc * vc, axis=-1, keepdims=True)
        rstd = lax.rsqrt(var + LN_EPS)
        vhat = vc * rstd
        vn = (vhat * glg_ref[...] + glb_ref[...]).astype(BF16)
        mask = _window_mask()
        for g in range(G):
            wsg = (ws_ref[g] * mask).astype(BF16)
            bcol = bst_ref[:, g:g + 1]
            cols = slice(g * gd, (g + 1) * gd)
            for wi in range(tm // W):
                rows = slice(wi * W, (wi + 1) * W)
                vblk = vn[rows, cols]
                s = jnp.dot(wsg, vblk, preferred_element_type=F32) + bcol
                dqb = dq_ref[rows, cols].astype(F32)
                du_s[rows, cols] = dqb * s
                ds = dqb * u[rows, cols]
                dss_ref[:, cols] += ds
                dsb = ds.astype(BF16)
                dvn_s[rows, cols] = lax.dot_general(wsg, dsb, TN, preferred_element_type=F32)
                dws_ref[g] += lax.dot_general(dsb, vblk, NT, preferred_element_type=F32) * mask
        dvn = dvn_s[...]
        dgl_ref[0:1, :] += jnp.sum(dvn * vhat, axis=0, keepdims=True)
        dgl_ref[1:2, :] += jnp.sum(dvn, axis=0, keepdims=True)
        dvh = dvn * glg_ref[...]
        m1 = jnp.mean(dvh, axis=-1, keepdims=True)
        m2 = jnp.mean(dvh * vhat, axis=-1, keepdims=True)
        dv = rstd * (dvh - m1 - vhat * m2)
        dpu = du_s[...] * _gelu_grad(pu)
        dpv = dv * _gelu_grad(pv)
        dbin_ref[0:1, :H] += jnp.sum(dpu, axis=0, keepdims=True)
        dbin_ref[0:1, H:] += jnp.sum(dpv, axis=0, keepdims=True)
        dpre_ref[:, :H] = dpu.astype(BF16)
        dpre_ref[:, H:] = dpv.astype(BF16)

    return pl.pallas_call(
        body, name=name, grid=(S // tm,),
        in_specs=[pl.BlockSpec((tm, H), _row(0)), pl.BlockSpec((tm, H2), _row(0)), pl.BlockSpec((1, H), _fixed2),
                  pl.BlockSpec((1, H), _fixed2), pl.BlockSpec((G, W, W), _fixed3), pl.BlockSpec((W, G), _fixed2)],
        out_specs=[pl.BlockSpec((tm, H2), _row(0)), pl.BlockSpec((G, W, W), _fixed3), pl.BlockSpec((W, H), _fixed2),
                   pl.BlockSpec((8, H), _fixed2), pl.BlockSpec((8, H2), _fixed2)],
        out_shape=[jax.ShapeDtypeStruct((S, H2), BF16), jax.ShapeDtypeStruct((G, W, W), F32),
                   jax.ShapeDtypeStruct((W, H), F32), jax.ShapeDtypeStruct((8, H), F32),
                   jax.ShapeDtypeStruct((8, H2), F32)],
        scratch_shapes=[pltpu.VMEM((tm, H), F32), pltpu.VMEM((tm, H), F32)],
        compiler_params=_params(("arbitrary",)),
    )(dq, pre, glg, glb, ws, bst)


def group_lane_sum(dss, name):
    W, H = dss.shape
    gd = H // GMLP_GROUPS

    def body(d_ref, o_ref):
        j = lax.broadcasted_iota(jnp.int32, (H, LANES), 0)
        g = lax.broadcasted_iota(jnp.int32, (H, LANES), 1)
        ind = ((j // gd) == g).astype(F32)
        o_ref[...] = jnp.dot(d_ref[...], ind, preferred_element_type=F32, precision=HIGHEST)

    return pl.pallas_call(
        body, name=name, in_specs=[VMEM_SPEC], out_specs=VMEM_SPEC,
        out_shape=jax.ShapeDtypeStruct((W, LANES), F32), compiler_params=_params(),
    )(dss)


def _attn_load(j, cps, q_ref, k_ref, v_ref):
    r = lax.broadcasted_iota(jnp.int32, (CHUNK, BAND), 1)
    chunks = []
    for cc in range(cps):
        start = pl.multiple_of((j * cps + cc) * CHUNK, CHUNK)
        chunks.append((q_ref[cc * CHUNK:(cc + 1) * CHUNK, :], k_ref[pl.ds(start, BAND), :],
                       v_ref[pl.ds(start, BAND), :], (r + start) >= LEFT_PAD))
    return chunks


def _attn_probs(chunks, b_ref, sels, scale):
    qms = [[jnp.where(sel, q2, jnp.zeros_like(q2)) for sel in sels] for q2, _, _, _ in chunks]
    raw = [[lax.dot_general(qm, k2, NT, preferred_element_type=F32) for qm in qms[cc]]
           for cc, (_, k2, _, _) in enumerate(chunks)]
    probs = []
    for cc, (_, _, _, valid) in enumerate(chunks):
        row = []
        for sub in range(2):
            s = jnp.where(valid, raw[cc][sub] * scale + b_ref[sub], -jnp.inf)
            e = jnp.exp(s - jnp.max(s, axis=-1, keepdims=True))
            row.append(e / jnp.sum(e, axis=-1, keepdims=True))
        probs.append(row)
    return qms, probs


def attn_fwd(q, kpad, vpad, bias, name):
    S, D = q.shape
    HP = D // LANES
    cps = min(ATTN_CHUNKS_PER_STEP, S // CHUNK)
    tq = cps * CHUNK
    scale = HEAD_DIM ** -0.5

    def body(q_ref, k_ref, v_ref, b_ref, o_ref):
        sel0 = lax.broadcasted_iota(jnp.int32, (CHUNK, LANES), 1) < HEAD_DIM
        chunks = _attn_load(pl.program_id(1), cps, q_ref, k_ref, v_ref)
        _, probs = _attn_probs(chunks, b_ref, (sel0, jnp.logical_not(sel0)), scale)
        outs = [[jnp.dot(probs[cc][sub].astype(BF16), v2, preferred_element_type=F32) for sub in range(2)]
                for cc, (_, _, v2, _) in enumerate(chunks)]
        o_ref[...] = jnp.concatenate([jnp.where(sel0, o[0], o[1]) for o in outs], axis=0).astype(BF16)

    kv_spec = pl.BlockSpec((S + LEFT_PAD, LANES), lambda h, j: (0, h))
    return pl.pallas_call(
        body, name=name, grid=(HP, S // tq),
        in_specs=[pl.BlockSpec((tq, LANES), lambda h, j: (j, h)), kv_spec, kv_spec,
                  pl.BlockSpec((2, CHUNK, BAND), lambda h, j: (h, 0, 0))],
        out_specs=pl.BlockSpec((tq, LANES), lambda h, j: (j, h)),
        out_shape=jax.ShapeDtypeStruct((S, D), BF16),
        compiler_params=_params(("parallel", "parallel")),
    )(q, kpad, vpad, bias)


def attn_bwd(q, do, kpad, vpad, bias, dk_in, dv_in, name):
    S, D = q.shape
    HP = D // LANES
    NH = 2 * HP
    cps = min(ATTN_CHUNKS_PER_STEP, S // CHUNK)
    tq = cps * CHUNK
    nj = S // tq
    scale = HEAD_DIM ** -0.5

    def body(q_ref, do_ref, k_ref, v_ref, b_ref, dki_ref, dvi_ref, dq_ref, dk_ref, dv_ref, db_ref, dk_acc, dv_acc):
        j = pl.program_id(1)

        @pl.when(j == 0)
        def _():
            dk_acc[:LEFT_PAD, :] = jnp.zeros((LEFT_PAD, LANES), F32)
            dv_acc[:LEFT_PAD, :] = jnp.zeros((LEFT_PAD, LANES), F32)
            dk_acc[LEFT_PAD:, :] = dki_ref[...]
            dv_acc[LEFT_PAD:, :] = dvi_ref[...]
            db_ref[...] = jnp.zeros_like(db_ref)

        sel0 = lax.broadcasted_iota(jnp.int32, (CHUNK, LANES), 1) < HEAD_DIM
        sels = (sel0, jnp.logical_not(sel0))
        chunks = _attn_load(j, cps, q_ref, k_ref, v_ref)
        pairs = [(cc, sub) for cc in range(cps) for sub in range(2)]
        qms, probs = _attn_probs(chunks, b_ref, sels, scale)
        doms = [[jnp.where(sel, do_ref[cc * CHUNK:(cc + 1) * CHUNK, :], jnp.zeros((CHUNK, LANES), BF16))
                 for sel in sels] for cc in range(cps)]
        dps = {(cc, sub): lax.dot_general(doms[cc][sub], chunks[cc][2], NT, preferred_element_type=F32)
               for cc, sub in pairs}
        dss = {}
        for cc, sub in pairs:
            p = probs[cc][sub]
            dss[cc, sub] = p * (dps[cc, sub] - jnp.sum(dps[cc, sub] * p, axis=-1, keepdims=True))
        dsb = {key: ds.astype(BF16) for key, ds in dss.items()}
        dqs = {(cc, sub): jnp.dot(dsb[cc, sub], chunks[cc][1], preferred_element_type=F32) * scale
               for cc, sub in pairs}
        dks = {(cc, sub): lax.dot_general(dsb[cc, sub], qms[cc][sub], TN, preferred_element_type=F32) * scale
               for cc, sub in pairs}
        dvs = {(cc, sub): lax.dot_general(probs[cc][sub].astype(BF16), doms[cc][sub], TN,
                                          preferred_element_type=F32) for cc, sub in pairs}
        dq_ref[...] = jnp.concatenate([jnp.where(sel0, dqs[cc, 0], dqs[cc, 1]) for cc in range(cps)],
                                      axis=0).astype(BF16)
        for sub in range(2):
            total = dss[0, sub]
            for cc in range(1, cps):
                total = total + dss[cc, sub]
            db_ref[sub] += total
        dk_parts = [dks[cc, 0] + dks[cc, 1] for cc in range(cps)]
        dv_parts = [dvs[cc, 0] + dvs[cc, 1] for cc in range(cps)]

        def window(parts):
            blocks = []
            for rb in range(cps - 1 + BAND // CHUNK):
                acc = None
                for cc in range(cps):
                    b = rb - cc
                    if 0 <= b < BAND // CHUNK:
                        piece = parts[cc][b * CHUNK:(b + 1) * CHUNK, :]
                        acc = piece if acc is None else acc + piece
                blocks.append(acc)
            return jnp.concatenate(blocks, axis=0)

        span = pl.ds(pl.multiple_of(j * cps * CHUNK, CHUNK), (cps - 1) * CHUNK + BAND)
        dk_acc[span, :] += window(dk_parts)
        dv_acc[span, :] += window(dv_parts)

        @pl.when(j == nj - 1)
        def _():
            dk_ref[...] = dk_acc[LEFT_PAD:, :]
            dv_ref[...] = dv_acc[LEFT_PAD:, :]

    q_spec = pl.BlockSpec((tq, LANES), lambda h, j: (j, h))
    kv_spec = pl.BlockSpec((S + LEFT_PAD, LANES), lambda h, j: (0, h))
    col_spec = pl.BlockSpec((S, LANES), lambda h, j: (0, h))
    b_spec = pl.BlockSpec((2, CHUNK, BAND), lambda h, j: (h, 0, 0))
    return pl.pallas_call(
        body, name=name, grid=(HP, nj),
        in_specs=[q_spec, q_spec, kv_spec, kv_spec, b_spec, col_spec, col_spec],
        out_specs=[q_spec, col_spec, col_spec, b_spec],
        out_shape=[jax.ShapeDtypeStruct((S, D), BF16), jax.ShapeDtypeStruct((S, D), F32),
                   jax.ShapeDtypeStruct((S, D), F32), jax.ShapeDtypeStruct((NH, CHUNK, BAND), F32)],
        scratch_shapes=[pltpu.VMEM((S + LEFT_PAD, LANES), F32), pltpu.VMEM((S + LEFT_PAD, LANES), F32)],
        compiler_params=_params(("parallel", "arbitrary")),
    )(q, do, kpad, vpad, bias, dk_in, dv_in)


def _rel_onehot(t):
    r = lax.broadcasted_iota(jnp.int32, (BAND, N_REL), 0)
    i = lax.broadcasted_iota(jnp.int32, (BAND, N_REL), 1)
    idx = jnp.clip(t + LEFT_PAD - r, -(CHUNK - 1), MAX_REL) + (CHUNK - 1)
    return (idx == i).astype(BF16)


def _split3(v):
    hi = v.astype(BF16)
    rest = v - hi.astype(F32)
    mid = rest.astype(BF16)
    return hi, mid, (rest - mid.astype(F32)).astype(BF16)


def _dot_onehot(parts, onehot, dims):
    hi, mid, lo = [lax.dot_general(p, onehot, dims, preferred_element_type=F32) for p in parts]
    return (hi + mid) + lo


def bias_expand(rb, name):
    NH = rb.shape[0]

    def body(rb_ref, o_ref):
        parts = _split3(rb_ref[...])

        def step(t, carry):
            o_ref[t] = _dot_onehot(parts, _rel_onehot(t), NT)
            return carry

        lax.fori_loop(0, CHUNK, step, 0)

    return pl.pallas_call(
        body, name=name, in_specs=[VMEM_SPEC], out_specs=VMEM_SPEC,
        out_shape=jax.ShapeDtypeStruct((CHUNK, NH, BAND), F32), compiler_params=_params(),
    )(rb)


def bias_grad(dsum, name):
    NH = dsum.shape[1]

    def body(d_ref, o_ref):
        def step(t, acc):
            return acc + _dot_onehot(_split3(d_ref[t]), _rel_onehot(t), (((1,), (0,)), ((), ())))

        o_ref[...] = lax.fori_loop(0, CHUNK, step, jnp.zeros((NH, N_REL), F32))

    return pl.pallas_call(
        body, name=name, in_specs=[VMEM_SPEC], out_specs=VMEM_SPEC,
        out_shape=jax.ShapeDtypeStruct((NH, N_REL), F32), compiler_params=_params(),
    )(dsum)


def loss_grad(y, tgt, name):
    S, D = y.shape
    tm = min(MATMUL_ROW_TILE, S)

    def body(y_ref, t_ref, d_ref, acc_ref):
        @pl.when(pl.program_id(0) == 0)
        def _():
            acc_ref[...] = jnp.zeros_like(acc_ref)

        err = y_ref[...] - t_ref[...]
        d_ref[...] = err * (1.0 / D)
        acc_ref[0:1, :] += jnp.sum(err * err, axis=0, keepdims=True)

    tile = pl.BlockSpec((tm, D), _row(0))
    return pl.pallas_call(
        body, name=name, grid=(S // tm,), in_specs=[tile, tile],
        out_specs=[tile, pl.BlockSpec((8, D), _fixed2)],
        out_shape=[jax.ShapeDtypeStruct((S, D), F32), jax.ShapeDtypeStruct((8, D), F32)],
        compiler_params=_params(("arbitrary",)),
    )(y, tgt)


def _col_tile(n):
    for t in (768, 512, 256, 128):
        if n % t == 0:
            return t
    return n


def ada_fwd(c_all, w, b, name):
    L, D, n = w.shape
    tn = _col_tile(n)

    def body(c_ref, w_ref, b_ref, o_ref):
        cv = c_ref[...]
        ca = cv * _sigmoid(cv)
        o_ref[...] = jnp.dot(ca, w_ref[...], preferred_element_type=F32, precision=HIGHEST) + b_ref[...]

    return pl.pallas_call(
        body, name=name, grid=(L, n // tn),
        in_specs=[pl.BlockSpec((N_DEV, D), lambda l, j: (0, 0)), pl.BlockSpec((None, D, tn), lambda l, j: (l, 0, j)),
                  pl.BlockSpec((None, 1, tn), lambda l, j: (l, 0, j))],
        out_specs=pl.BlockSpec((None, N_DEV, tn), lambda l, j: (l, 0, j)),
        out_shape=jax.ShapeDtypeStruct((L, N_DEV, n), F32),
        compiler_params=_params(("parallel", "parallel")),
    )(c_all, w, b)


def ada_wgrad(c_all_t, dmod, name):
    L, _, n = dmod.shape
    D = c_all_t.shape[0]
    tn = _col_tile(n)

    def body(c_ref, d_ref, o_ref):
        cv = c_ref[...]
        ca = cv * _sigmoid(cv)
        o_ref[...] = jnp.dot(ca, d_ref[...], preferred_element_type=F32, precision=HIGHEST)

    return pl.pallas_call(
        body, name=name, grid=(L, n // tn),
        in_specs=[pl.BlockSpec((D, N_DEV), lambda l, j: (0, 0)), pl.BlockSpec((None, N_DEV, tn), lambda l, j: (l, 0, j))],
        out_specs=pl.BlockSpec((None, D, tn), lambda l, j: (l, 0, j)),
        out_shape=jax.ShapeDtypeStruct((L, D, n), F32),
        compiler_params=_params(("parallel", "parallel")),
    )(c_all_t, dmod)


ELEMENTWISE_BLOCK_BYTES = 3 * 1024 * 1024


def _elementwise_rows(rows, row_bytes):
    for t in (4096, 2048, 1024, 512, 256, 128, 64, 32, 16):
        if rows % t == 0 and t * row_bytes <= ELEMENTWISE_BLOCK_BYTES:
            return t
    return rows


def sum_leading(a, name):
    n, M, N = a.shape
    tr = _elementwise_rows(M, n * N * 4)

    def body(a_ref, o_ref):
        acc = a_ref[0].astype(F32)
        for i in range(1, n):
            acc = acc + a_ref[i].astype(F32)
        o_ref[...] = acc

    return pl.pallas_call(
        body, name=name, grid=(M // tr,),
        in_specs=[pl.BlockSpec((n, tr, N), lambda i: (0, i, 0))],
        out_specs=pl.BlockSpec((tr, N), _row(0)),
        out_shape=jax.ShapeDtypeStruct((M, N), F32),
        compiler_params=_params(("parallel",)),
    )(a)


def chip_sum(psum, land, chip_idx, transposed, name):
    n, M, N = psum.shape
    tr = M if transposed else _elementwise_rows(M, N * 8)

    def body(s_ref, p_ref, a_ref, b_ref, c_ref, o_ref):
        total = ((p_ref[...].astype(F32) + a_ref[...].astype(F32)) + b_ref[...].astype(F32)) + c_ref[...].astype(F32)
        o_ref[...] = jnp.transpose(total) if transposed else total

    def entry(j):
        return pl.BlockSpec((None, tr, N), lambda i, s: ((s[0] + j) % n, i, 0))

    out_block, out_dims = ((N, tr), (N, M)) if transposed else ((tr, N), (M, N))
    return pl.pallas_call(
        body, name=name,
        grid_spec=pltpu.PrefetchScalarGridSpec(
            num_scalar_prefetch=1, grid=(M // tr,),
            in_specs=[entry(0), entry(1), entry(2), entry(3)],
            out_specs=pl.BlockSpec(out_block, lambda i, s: (0, 0) if transposed else (i, 0))),
        out_shape=jax.ShapeDtypeStruct(out_dims, F32),
        compiler_params=_params(("parallel",)),
    )(chip_idx, psum, land, land, land)


def adamw(w, g, m, v, name):
    M, N = w.shape
    tr = _elementwise_rows(M, N * 4)
    c1 = 1.0 - ADAM_B1 ** ADAM_STEP
    c2 = 1.0 - ADAM_B2 ** ADAM_STEP

    def body(w_ref, g_ref, m_ref, v_ref, d_ref, nm_ref, nv_ref):
        g = g_ref[...]
        nm = ADAM_B1 * m_ref[...] + (1.0 - ADAM_B1) * g
        nv = ADAM_B2 * v_ref[...] + (1.0 - ADAM_B2) * (g * g)
        d_ref[...] = -ADAM_LR * ((nm / c1) / (jnp.sqrt(nv / c2) + ADAM_EPS) + ADAM_WD * w_ref[...])
        nm_ref[...] = nm
        nv_ref[...] = nv

    spec = pl.BlockSpec((tr, N), _row(0))
    shp = jax.ShapeDtypeStruct((M, N), F32)
    return pl.pallas_call(
        body, name=name, grid=(M // tr,), in_specs=[spec] * 4, out_specs=[spec] * 3, out_shape=[shp] * 3,
        compiler_params=_params(("parallel",)),
    )(w, g, m, v)


def _coords():
    return lax.axis_index("x"), lax.axis_index("y"), lax.axis_index("c")


def all_gather8(block, name):
    m_per, n = block.shape

    def body(x_ref, out_ref, send_sems, recv_sems, local_sem):
        x, y, c = _coords()
        me, sibling = (x, y, c), (x, y, 1 - c)
        chips = [(1 - x, y), (x, 1 - y), (1 - x, 1 - y)]

        def rows(px, py, pc):
            return out_ref.at[pl.ds((4 * px + 2 * py + pc) * m_per, m_per), :]

        def copy(k, blk, to, src=None):
            return pltpu.make_async_remote_copy(
                src_ref=rows(*blk) if src is None else src, dst_ref=rows(*blk),
                send_sem=send_sems.at[k], recv_sem=recv_sems.at[k], device_id=to, device_id_type=MESH)

        mine = pltpu.make_async_copy(x_ref, rows(*me), local_sem)
        mine.start()
        first = [copy(0, me, sibling, src=x_ref)]
        first += [copy(1 + j, me, (*chip, c), src=x_ref) for j, chip in enumerate(chips)]
        for cp in first:
            cp.start()
        passed = [copy(4 + j, (*chip, c), sibling) for j, chip in enumerate(chips)]
        for j, chip in enumerate(chips):
            copy(1 + j, (*chip, c), me).wait_recv()
            passed[j].start()
        copy(0, sibling, me).wait_recv()
        for j, chip in enumerate(chips):
            copy(4 + j, (*chip, 1 - c), me).wait_recv()
        for cp in first + passed:
            cp.wait_send()
        mine.wait()

    return pl.pallas_call(
        body, name=name, in_specs=[VMEM_SPEC], out_specs=VMEM_SPEC,
        out_shape=jax.ShapeDtypeStruct((N_DEV * m_per, n), block.dtype),
        scratch_shapes=[pltpu.SemaphoreType.DMA((7,)), pltpu.SemaphoreType.DMA((7,)), pltpu.SemaphoreType.DMA],
        compiler_params=_params(),
    )(block)


def _other_chips(x, y):
    return [(1 - x, y), (x, 1 - y), (1 - x, 1 - y)]


HBM_SPEC = pl.BlockSpec(memory_space=pltpu.HBM)
SEM_SPEC = pl.BlockSpec(memory_space=pltpu.SEMAPHORE)
DATAFLOW = pltpu.SideEffectType.DATAFLOW_SIDE_EFFECTING


def _chip_peers(x, y, c):
    return [(px, py, c) for px, py in _other_chips(x, y)]


def _sibling_peer(x, y, c):
    return [(x, y, 1 - c)]


def _weight_desc(_, k, land_ref, peer, me):
    h = land_ref.shape[1] // 2
    rows = pl.ds(me[2] * h, h)
    mine = land_ref.at[2 * me[0] + me[1], rows, :]
    return mine, mine, land_ref.at[2 * peer[0] + peer[1], rows, :]


def _grad_desc(psum_ref, k, land_ref, peer, me):
    return psum_ref.at[2 * peer[0] + peer[1]], land_ref.at[2 * me[0] + me[1]], land_ref.at[2 * peer[0] + peer[1]]


def _pair_desc(grad_ref, k, land_ref, peer, me):
    h = land_ref.shape[1]
    return grad_ref.at[:, pl.ds(peer[2] * h, h), :], land_ref, land_ref


def _whole_desc(src_ref, k, land_ref, peer, me):
    return src_ref, land_ref, land_ref


def exchange_start(srcs, lands, units, groups, desc, peers, after, name):
    n_s, n_l, n_g = len(srcs), len(lands), len(groups)
    n_p = len(peers(0, 0, 0))

    def body(*refs):
        s_refs, l_refs = refs[:n_s], refs[n_s:n_s + n_l]
        outs = refs[n_s + n_l + 1:]
        sems, token = outs[:2 * n_g], outs[-1]
        me = _coords()
        for g, ids in enumerate(groups):
            for i, u in enumerate(ids):
                si, k = units[u]
                for j, peer in enumerate(peers(*me)):
                    src, dst, _ = desc(s_refs[si] if s_refs else None, k, l_refs[u], peer, me)
                    pltpu.make_async_remote_copy(
                        src_ref=src, dst_ref=dst, send_sem=sems[2 * g].at[n_p * i + j],
                        recv_sem=sems[2 * g + 1].at[n_p * i + j], device_id=peer, device_id_type=MESH).start()
        token[...] = jnp.zeros_like(token)

    arrs = list(srcs) + list(lands)
    sem_shapes = [pltpu.SemaphoreType.DMA((n_p * len(ids),)) for ids in groups for _ in range(2)]
    outs = pl.pallas_call(
        body, name=name,
        in_specs=[HBM_SPEC] * len(arrs) + [ANY],
        out_specs=[SEM_SPEC] * (2 * n_g) + [HBM_SPEC] * len(arrs) + [VMEM_SPEC],
        out_shape=sem_shapes + [pltpu.HBM(a.shape, a.dtype) for a in arrs] + [jax.ShapeDtypeStruct((8, LANES), F32)],
        input_output_aliases={i: 2 * n_g + i for i in range(len(arrs))},
        compiler_params=pltpu.CompilerParams(has_side_effects=DATAFLOW),
    )(*[pltpu.with_memory_space_constraint(a, pltpu.HBM) for a in arrs], after)
    sems = outs[:2 * n_g]
    thru = outs[2 * n_g:2 * n_g + len(arrs)]
    return sems, list(thru[:n_s]), list(thru[n_s:]), outs[-1]


def exchange_wait(srcs, lands, units, send_sem, recv_sem, desc, peers, after, name):
    n_s, n_l = len(srcs), len(lands)
    n_p = len(peers(0, 0, 0))

    def body(*refs):
        s_refs, l_refs = refs[:n_s], refs[n_s:n_s + n_l]
        send_sems, recv_sems = refs[n_s + n_l], refs[n_s + n_l + 1]
        me = _coords()
        for i, (si, k) in enumerate(units):
            for j, peer in enumerate(peers(*me)):
                src, _, mine = desc(s_refs[si] if s_refs else None, k, l_refs[i], peer, me)
                cp = pltpu.make_async_remote_copy(
                    src_ref=src, dst_ref=mine, send_sem=send_sems.at[n_p * i + j], recv_sem=recv_sems.at[n_p * i + j],
                    device_id=peer, device_id_type=MESH)
                cp.wait_send()
                cp.wait_recv()

    arrs = list(srcs) + list(lands)
    outs = pl.pallas_call(
        body, name=name,
        in_specs=[HBM_SPEC] * len(arrs) + [SEM_SPEC, SEM_SPEC, ANY],
        out_specs=[HBM_SPEC] * len(arrs),
        out_shape=[pltpu.HBM(a.shape, a.dtype) for a in arrs],
        input_output_aliases={i: i for i in range(len(arrs))},
        compiler_params=pltpu.CompilerParams(has_side_effects=DATAFLOW),
    )(*arrs, send_sem, recv_sem, after)
    return list(outs[:n_s]), list(outs[n_s:])


def sibling_fill(lands, name):
    n_u = len(lands)

    def body(*refs):
        ins, outs = refs[:n_u], refs[n_u:2 * n_u]
        send_sems, recv_sems = refs[2 * n_u:]
        x, y, c = _coords()
        barrier = pltpu.get_barrier_semaphore()
        pl.semaphore_signal(barrier, inc=1, device_id=(x, y, 1 - c), device_id_type=MESH)
        pl.semaphore_wait(barrier, 1)
        sends = []
        for u in range(n_u):
            h = ins[u].shape[1] // 2
            for j, (px, py) in enumerate(_other_chips(x, y)):
                part = (2 * px + py, pl.ds(c * h, h), slice(None))
                cp = pltpu.make_async_remote_copy(
                    src_ref=ins[u].at[part], dst_ref=outs[u].at[part], send_sem=send_sems.at[3 * u + j],
                    recv_sem=recv_sems.at[3 * u + j], device_id=(x, y, 1 - c), device_id_type=MESH)
                cp.start()
                sends.append(cp)
        for u in range(n_u):
            h = ins[u].shape[1] // 2
            for j, (px, py) in enumerate(_other_chips(x, y)):
                theirs = (2 * px + py, pl.ds((1 - c) * h, h), slice(None))
                pltpu.make_async_remote_copy(
                    src_ref=ins[u].at[theirs], dst_ref=outs[u].at[theirs], send_sem=send_sems.at[3 * u + j],
                    recv_sem=recv_sems.at[3 * u + j], device_id=(x, y, 1 - c), device_id_type=MESH).wait_recv()
        for cp in sends:
            cp.wait_send()

    return pl.pallas_call(
        body, name=name, in_specs=[ANY] * n_u, out_specs=[ANY] * n_u,
        out_shape=[jax.ShapeDtypeStruct(a.shape, a.dtype) for a in lands],
        input_output_aliases={i: i for i in range(n_u)},
        scratch_shapes=[pltpu.SemaphoreType.DMA((3 * n_u,)), pltpu.SemaphoreType.DMA((3 * n_u,))],
        compiler_params=pltpu.CompilerParams(vmem_limit_bytes=VMEM_LIMIT_MB * 1024 * 1024,
                                             collective_id=FILL_COLLECTIVE_ID),
    )(*lands)


def _pack_rows(parts):
    flat = jnp.concatenate([p.reshape(-1).astype(F32) for p in parts])
    n = flat.shape[0]
    padded = -(-n // (8 * LANES)) * (8 * LANES)
    return jnp.pad(flat, (0, padded - n)).reshape(-1, LANES)


def _unpack_rows(packed, shapes):
    flat = packed.reshape(-1)
    out, off = [], 0
    for s in shapes:
        size = 1
        for d in s:
            size *= d
        out.append(flat[off:off + size].reshape(s))
        off += size
    return out


def _shard_last(full, s_me):
    n = full.shape[-1] // N_CHIP
    return lax.dynamic_slice_in_dim(full, s_me * n, n, axis=full.ndim - 1)


def _unshard_last(g):
    moved = jnp.moveaxis(g, 0, -2)
    return moved.reshape(moved.shape[:-2] + (moved.shape[-2] * moved.shape[-1],))


def kernel(x, c, w_ada, b_ada, ln_g, ln_b, ffn_gu, ffn_down, gmlp_w_in, gmlp_b_in, gmlp_ln_g, gmlp_ln_b, gmlp_w_s, gmlp_b_s, gmlp_w_out, w_ada_kv, b_ada_kv, w_kv, attn_w_q, attn_rel_bias, attn_w_o, loss_target, m_w_ada, m_b_ada, m_ln_g, m_ln_b, m_ffn_gu, m_ffn_down, m_gmlp_w_in, m_gmlp_b_in, m_gmlp_ln_g, m_gmlp_ln_b, m_gmlp_w_s, m_gmlp_b_s, m_gmlp_w_out, m_w_ada_kv, m_b_ada_kv, m_w_kv, m_attn_w_q, m_attn_rel_bias, m_attn_w_o, v_w_ada, v_b_ada, v_ln_g, v_ln_b, v_ffn_gu, v_ffn_down, v_gmlp_w_in, v_gmlp_b_in, v_gmlp_ln_g, v_gmlp_ln_b, v_gmlp_w_s, v_gmlp_b_s, v_gmlp_w_out, v_w_ada_kv, v_b_ada_kv, v_w_kv, v_attn_w_q, v_attn_rel_bias, v_attn_w_o):
    xi, yi, ci = _coords()
    s_me = 2 * xi + yi
    dev = 4 * xi + 2 * yi + ci

    x0 = x[0]
    tgt = loss_target[0]
    S, D = x0.shape
    L = w_ada.shape[0]
    NA = gmlp_w_in.shape[0]
    NB = attn_w_q.shape[0]
    NH = D // HEAD_DIM
    alpha = (2.0 * L) ** 0.25
    n_ada = w_ada.shape[2]
    n_kv = w_ada_kv.shape[1]

    stack_names = ["ffn_gu", "ffn_down", "gmlp_w_in", "gmlp_w_out", "w_kv", "attn_w_q", "attn_w_o"]
    stack_src = dict(ffn_gu=ffn_gu, ffn_down=ffn_down, gmlp_w_in=gmlp_w_in, gmlp_w_out=gmlp_w_out, w_kv=w_kv[None],
                     attn_w_q=attn_w_q, attn_w_o=attn_w_o)
    stacks = [stack_src[nm].reshape((-1,) + stack_src[nm].shape[-2:]) for nm in stack_names]
    units = [(si, k) for si, st in enumerate(stacks) for k in range(st.shape[0])]
    unit_of = {(stack_names[si], k): u for u, (si, k) in enumerate(units)}
    weight_groups = [[("ffn_gu", 0)], [("ffn_down", 0)]]
    for l in range(L):
        mixer = [("gmlp_w_in", l), ("gmlp_w_out", l)] if l < NA else [("attn_w_q", l - NA), ("attn_w_o", l - NA)]
        first, last = [("ffn_gu", 2 * l), ("ffn_down", 2 * l)], [("ffn_gu", 2 * l + 1), ("ffn_down", 2 * l + 1)]
        if l == 0:
            weight_groups += [mixer, last]
        else:
            weight_groups += [([("w_kv", 0)] if l == NA else []) + first, mixer, last]
    weight_groups = [[unit_of[n] for n in names] for names in weight_groups]
    group_of = {u: g for g, ids in enumerate(weight_groups) for u in ids}

    c_all = all_gather8(jnp.broadcast_to(c, (8, D)), "ag_c").reshape(N_DEV, 8, D)[:, 0]
    b_ada_sh = lax.dynamic_slice_in_dim(b_ada, s_me * n_ada, n_ada, axis=1)
    b_kv_sh = lax.dynamic_slice_in_dim(b_ada_kv, s_me * n_kv, n_kv, axis=0)
    mod_part = ada_fwd(c_all, w_ada, b_ada_sh[:, None, :], "ada_fwd")
    mkv_part = ada_fwd(c_all, w_ada_kv[None], b_kv_sh[None, None, :], "ada_kv_fwd")
    part = jnp.concatenate([jnp.transpose(mod_part, (1, 0, 2)).reshape(N_DEV, L * n_ada), mkv_part[0]], axis=1)
    width = part.shape[1]
    pad_w = -(-width // LANES) * LANES - width
    all_part = all_gather8(jnp.pad(part, ((0, 0), (0, pad_w))), "ag_mod").reshape(N_DEV, N_DEV, width + pad_w)
    mine = lax.dynamic_index_in_dim(all_part[0::2], dev, axis=1, keepdims=False)
    mod = jnp.transpose(mine[:, :L * n_ada].reshape(N_CHIP, L, n_ada), (1, 0, 2)).reshape(L, N_MOD, D)
    mkv = mine[:, L * n_ada:width].reshape(2, D)

    def mrow(l, k):
        return mod[l, k][None, :]

    small_shapes = [ln_g.shape, ln_b.shape, gmlp_b_in.shape, gmlp_ln_g.shape, gmlp_ln_b.shape, attn_rel_bias.shape]
    small_pack = _pack_rows([ln_g, ln_b, gmlp_b_in, gmlp_ln_g, gmlp_ln_b, attn_rel_bias])
    small_all = all_gather8(small_pack, "ag_small_params").reshape((N_DEV,) + small_pack.shape)[0::2]
    sm = [_unpack_rows(small_all[s], small_shapes) for s in range(N_CHIP)]
    ln_g_f, ln_b_f, b_in_f, gln_g_f, gln_b_f, rel_f = [
        _unshard_last(jnp.stack([sm[s][i] for s in range(N_CHIP)])) for i in range(len(small_shapes))]

    def landing(u):
        si, k = units[u]
        shard = stacks[si][k]
        if group_of[u] < 2:
            shard = lax.optimization_barrier(shard)
        shard = shard.astype(BF16)
        return lax.dynamic_update_slice(lax.empty((N_CHIP,) + shard.shape, BF16), shard[None], (s_me, 0, 0))

    gathers_done = jnp.concatenate([mod.reshape(-1)[:LANES], small_all.reshape(-1)[:LANES]])
    w_sems, lands_t = {}, {}
    for part, groups, name in ((0, weight_groups[:2], "weight_send_start_first"),
                               (1, weight_groups[2:], "weight_send_start_rest")):
        ids = [u for grp in groups for u in grp]
        local = [[ids.index(u) for u in grp] for grp in groups]
        sems, _, lands, gathers_done = exchange_start([], [landing(u) for u in ids], [units[u] for u in ids], local,
                                                      _weight_desc, _chip_peers, gathers_done, name)
        for i, grp in enumerate(groups):
            w_sems[2 * part + i] = (sems[2 * i], sems[2 * i + 1])
        lands_t.update(zip(ids, lands))
    wg = {}
    latest = [gathers_done]

    def W(nm, k):
        u = unit_of[(nm, k)]
        if u not in wg:
            g = group_of[u]
            ids = weight_groups[g]
            _, got = exchange_wait([], [lands_t[v] for v in ids], [units[v] for v in ids], w_sems[g][0],
                                   w_sems[g][1], _weight_desc, _chip_peers, latest[0], "weight_send_wait_%d" % g)
            wg.update(zip(ids, sibling_fill(got, "weight_sibling_fill")))
        return wg[u]

    def Wrows(nm, k):
        w4 = W(nm, k)
        return w4.reshape(w4.shape[0] * w4.shape[1], w4.shape[2])

    bst = [jnp.transpose(gmlp_b_s[j]) for j in range(NA)]
    biases = {}

    def make_bias(j, dep):
        rel, _ = lax.optimization_barrier((rel_f[j], dep))
        biases[j] = jnp.transpose(bias_expand(rel, "bias_expand"), (1, 0, 2))
        return biases[j]

    saved = []
    xc = x0
    kpad = vpad = xkv = None
    for l in range(L):
        if l == 1 and NB > 1:
            latest[0] = make_bias(1, xc)
        if l == NA:
            xkv = xc
            kv, hkv = mod_matmul(xc, mkv[1][None], mkv[0][None], W("w_kv", 0), None, BF16, "kv_proj")
            kpad = jnp.pad(kv[:, :D], ((LEFT_PAD, 0), (0, 0)))
            vpad = jnp.pad(kv[:, D:], ((LEFT_PAD, 0), (0, 0)))
        sv = {}
        for i in (0, 2):
            k = 2 * l + i // 2
            gu, hv = mod_matmul(xc, mrow(l, 3 * i + 1), mrow(l, 3 * i), W("ffn_gu", k), None, BF16, "ffn_up")
            latest[0] = hv
            gw = 0.5 * (1.0 + mrow(l, 3 * i + 2))
            xn, xh, rs, yv, av = matmul_res_ln(gu, Wrows("ffn_down", k), xc, gw, ln_g_f[l, i][None],
                                               ln_b_f[l, i][None], alpha, True, "ffn_down")
            sv[i] = dict(x=xc, h=hv, gu=gu, a=av, xh=xh, rs=rs, y=yv, gw=gw)
            xc = latest[0] = xn
            if i == 0:
                if l == 0 and NB > 0:
                    latest[0] = make_bias(0, xc)
                gw = 1.0 + mrow(l, 5)
                if l < NA:
                    pre, hv = mod_matmul(xc, mrow(l, 4), mrow(l, 3), W("gmlp_w_in", l), b_in_f[l][None], F32,
                                         "gmlp_in")
                    qv = sgu_fwd(pre, gln_g_f[l][None], gln_b_f[l][None], gmlp_w_s[l], bst[l], "sgu_fwd")
                    xn, xh, rs, yv = matmul_res_ln(qv, Wrows("gmlp_w_out", l), xc, gw, ln_g_f[l, 1][None],
                                                   ln_b_f[l, 1][None], alpha, False, "gmlp_out")
                    sv[1] = dict(x=xc, h=hv, pre=pre, a=qv, xh=xh, rs=rs, y=yv, gw=gw)
                else:
                    j = l - NA
                    if j not in biases:
                        make_bias(j, xc)
                    qh, hv = mod_matmul(xc, mrow(l, 4), mrow(l, 3), Wrows("attn_w_q", j)[None], None, BF16, "attn_q")
                    ov = attn_fwd(qh, kpad, vpad, biases[j], "attn_fwd")
                    xn, xh, rs, yv = matmul_res_ln(ov, Wrows("attn_w_o", j), xc, gw, ln_g_f[l, 1][None],
                                                   ln_b_f[l, 1][None], alpha, False, "attn_out")
                    sv[1] = dict(x=xc, h=hv, q=qh, a=ov, xh=xh, rs=rs, y=yv, gw=gw)
                xc = latest[0] = xn
        saved.append(sv)

    dx, lacc = loss_grad(xc, tgt, "loss_grad")
    loss = lax.psum((0.5 / D) * jnp.sum(lacc[0]), ("x", "y", "c"))

    gpair = [None] * len(units)
    col_split = {u for u, (si, _) in enumerate(units) if stack_names[si] != "ffn_gu"}
    dmod = [[None] * N_MOD for _ in range(L)]
    d_ln_g = [[None] * 3 for _ in range(L)]
    d_ln_b = [[None] * 3 for _ in range(L)]
    d_b_in, d_gln_g, d_gln_b, d_ws, d_bs, d_rel = ([None] * NA, [None] * NA, [None] * NA, [None] * NA, [None] * NA,
                                                  [None] * NB)
    dk = jnp.zeros((S, D), F32)
    dv = jnp.zeros((S, D), F32)
    dmkv = None

    made = []

    core_idx = ci.astype(jnp.int32).reshape(1)
    chip_idx = s_me.astype(jnp.int32).reshape(1)

    def put(nm, k, a, b, name):
        u = unit_of[(nm, k)]
        rows, cols = stacks[units[u][0]].shape[1:]
        if nm == "ffn_gu":
            g = wgrad_pair(b, a, N_CHIP, cols, rows // 2, lambda j, p: j, lambda j, p: p, core_idx, name)
        elif nm in ("gmlp_w_in", "w_kv"):
            g = wgrad_pair(a, b, N_CHIP, rows, cols // 2, lambda j, p: 0, lambda j, p: 2 * j + p, core_idx, name)
        else:
            g = wgrad_pair(a, b, 1, N_CHIP * rows, cols // 2, lambda j, p: 0, lambda j, p: p, core_idx, name)
        gpair[u] = g.reshape((N_CHIP, -1, g.shape[-1]))
        made.append(u)

    own_half, sib_half = {}, {}
    n_started = [0]

    def start_grad_exchange(ids, after):
        psums = [gpair[u] for u in ids]
        n = len(ids)
        tag = n_started[0]
        n_started[0] += 1
        sems, ps_t, q_t, token = exchange_start(psums, [lax.empty(p.shape, p.dtype) for p in psums],
                                                [(i, 0) for i in range(n)], [list(range(n))], _grad_desc, _chip_peers,
                                                after, "grad_send_start_%d" % tag)
        return dict(ids=ids, tag=tag, sems=sems, ps=ps_t, q=q_t), token

    def finish_grad_exchange(pend, after):
        n = len(pend["ids"])
        ps_t, q = exchange_wait(pend["ps"], pend["q"], [(i, 0) for i in range(n)], pend["sems"][0], pend["sems"][1],
                                _grad_desc, _chip_peers, after, "grad_send_wait_%d" % pend["tag"])
        halves = [chip_sum(ps_t[i], q[i], chip_idx, u not in col_split, "grad_chip_sum")
                  for i, u in enumerate(pend["ids"])]
        sems, h_t, land_t, token = exchange_start(halves, [lax.empty(h.shape, h.dtype) for h in halves],
                                              [(i, 0) for i in range(n)], [list(range(n))], _whole_desc,
                                              _sibling_peer, halves[0], "half_send_start_%d" % pend["tag"])
        swaps.append(dict(ids=pend["ids"], tag=pend["tag"], sems=sems, h=h_t, land=land_t))
        return token

    def collect_halves(after):
        for sw in swaps:
            n = len(sw["ids"])
            h, land = exchange_wait(sw["h"], sw["land"], [(i, 0) for i in range(n)], sw["sems"][0], sw["sems"][1],
                                    _whole_desc, _sibling_peer, after, "half_send_wait_%d" % sw["tag"])
            for u, mine, theirs in zip(sw["ids"], h, land):
                own_half[u], sib_half[u] = mine, theirs
        swaps.clear()

    swaps = []
    pending = None
    started_before = jnp.zeros((8, LANES), F32)

    def ln_inputs(l, i):
        t = saved[l][i]
        return (t["xh"], t["rs"], ln_g_f[l, i][None], t["y"], t["gw"], 1.0 if i == 1 else 0.5)

    def record_ln(l, i, acc, row0):
        d_ln_g[l][i], d_ln_b[l][i], dmod[l][3 * i + 2] = acc[row0], acc[row0 + 1], acc[row0 + 2]

    ln_done = None
    for l in reversed(range(L)):
        if l == NA - 1:
            dkv = jnp.concatenate([dk, dv], axis=1)
            put("w_kv", 0, hkv, dkv, "kv_wgrad")
            pdxa, pdy, acc = dgrad_mod(dkv, W("w_kv", 0), dx, xkv, mkv[1][None], ln_inputs(l, 2), alpha, "kv_dgrad")
            dmkv = jnp.stack([acc[1], acc[0]])
            record_ln(l, 2, acc, 2)
            ln_done = (pdxa, pdy)
        sv = saved[l]
        for i in (2, 1, 0):
            t = sv[i]
            if ln_done is None:
                dxa, dy, acc1 = ln_res_bwd(dx, *ln_inputs(l, i), alpha, "ln_res_bwd")
                record_ln(l, i, acc1, 0)
            else:
                dxa, dy = ln_done
                ln_done = None
            before = (l, i - 1) if i > 0 else ((l - 1, 2) if l > 0 and l != NA else None)
            prev = ln_inputs(*before) if before is not None else None
            scl = mrow(l, 3 * i + 1)
            if i != 1:
                k = 2 * l + i // 2
                F = t["gu"].shape[1] // 2
                dgu = ffn_act_bwd(dy, Wrows("ffn_down", k), t["gu"], started_before, "ffn_act_bwd")
                put("ffn_down", k, t["a"], dy, "ffn_down_wgrad")
                put("ffn_gu", k, t["h"], dgu, "ffn_up_wgrad")
                res = dgrad_mod(dgu, W("ffn_gu", k), dxa, t["x"], scl, prev, alpha, "ffn_up_dgrad")
            elif l < NA:
                dq = matmul_nt(dy, Wrows("gmlp_w_out", l), started_before, "gmlp_out_dgrad")
                put("gmlp_w_out", l, t["a"], dy, "gmlp_out_wgrad")
                dpre, dws_l, dss, dgl, dbin = sgu_bwd(dq, t["pre"], gln_g_f[l][None], gln_b_f[l][None], gmlp_w_s[l],
                                                      bst[l], "sgu_bwd")
                d_ws[l] = dws_l
                d_bs[l] = jnp.transpose(group_lane_sum(dss, "sgu_bias_grad")[:, :GMLP_GROUPS])
                d_gln_g[l], d_gln_b[l], d_b_in[l] = dgl[0], dgl[1], dbin[0]
                put("gmlp_w_in", l, t["h"], dpre, "gmlp_in_wgrad")
                res = dgrad_mod(dpre, W("gmlp_w_in", l), dxa, t["x"], scl, prev, alpha, "gmlp_in_dgrad")
            else:
                j = l - NA
                do = matmul_nt(dy, Wrows("attn_w_o", j), started_before, "attn_out_dgrad")
                put("attn_w_o", j, t["a"], dy, "attn_out_wgrad")
                dqh, dk, dv, dbias = attn_bwd(t["q"], do, kpad, vpad, biases[j], dk, dv, "attn_bwd")
                d_rel[j] = bias_grad(jnp.transpose(dbias, (1, 0, 2)), "bias_grad")
                put("attn_w_q", j, t["h"], dqh, "attn_q_wgrad")
                res = dgrad_mod(dqh, Wrows("attn_w_q", j)[None], dxa, t["x"], scl, prev, alpha, "attn_q_dgrad")
            acc2 = res[-1]
            dmod[l][3 * i + 1], dmod[l][3 * i] = acc2[0], acc2[1]
            if before is None:
                dx = res[0]
            else:
                record_ln(*before, acc2, 2)
                ln_done = (res[0], res[1])
            if (i == 0 and l > 0) or (i == 1 and l == 0):
                started, started_before = start_grad_exchange(list(made), acc2)
                made.clear()
                if pending is not None:
                    started_before = started_before + finish_grad_exchange(pending, acc2)
                pending = started
    grad_x = dx[None]

    dvec = _pack_rows([jnp.stack([jnp.stack(r) for r in dmod]), dmkv])
    dvec = lax.optimization_barrier((dvec, [gpair[u] for u in made]))[0]
    n_dvec = L * N_MOD * D + 2 * D
    dall = all_gather8(dvec, "ag_dmod").reshape(N_DEV, -1, LANES)
    db_all = sum_leading(dall, "ada_bias_grad").reshape(-1)[:n_dvec]
    g_b_ada = db_all[:L * N_MOD * D].reshape(L, N_MOD * D)
    g_b_ada_kv = db_all[L * N_MOD * D:]
    dall2 = dall.reshape(N_DEV, -1)[:, :n_dvec]
    dmod_all = dall2[:, :L * N_MOD * D].reshape(N_DEV, L, N_MOD * D)
    dmod_sh = jnp.transpose(lax.dynamic_slice_in_dim(dmod_all, s_me * n_ada, n_ada, axis=2), (1, 0, 2))
    dmkv_sh = lax.dynamic_slice_in_dim(dall2[:, L * N_MOD * D:], s_me * n_kv, n_kv, axis=1)[None]
    c_all_t = jnp.transpose(c_all)
    g_w_ada = ada_wgrad(c_all_t, dmod_sh, "ada_wgrad")
    g_w_ada_kv = ada_wgrad(c_all_t, dmkv_sh, "ada_kv_wgrad")[0]

    small_g = [jnp.stack([jnp.stack(r) for r in d_ln_g]), jnp.stack([jnp.stack(r) for r in d_ln_b]),
               jnp.stack(d_b_in), jnp.stack(d_gln_g), jnp.stack(d_gln_b), jnp.stack(d_rel), jnp.stack(d_bs)]
    sg_shapes = [a.shape for a in small_g]
    sg_pack = _pack_rows(small_g)
    sg_all = all_gather8(sg_pack, "ag_small_grads").reshape(N_DEV, -1, LANES)
    sg_sum = _unpack_rows(sum_leading(sg_all, "small_grad_sum"), sg_shapes)
    g_ln_g, g_ln_b, g_b_in, g_gln_g, g_gln_b, g_rel = [_shard_last(a, s_me) for a in sg_sum[:6]]
    g_bs = sg_sum[6]
    ws_pack = jnp.stack(d_ws).astype(BF16).reshape(-1, LANES)
    ws_all = all_gather8(ws_pack, "ag_spatial_grads").reshape(N_DEV, -1, LANES)
    g_ws = sum_leading(ws_all, "spatial_grad_sum").reshape(gmlp_w_s.shape)

    last, _ = start_grad_exchange(list(made), sg_all)

    grads = dict(w_ada=g_w_ada, b_ada=g_b_ada, ln_g=g_ln_g, ln_b=g_ln_b, gmlp_b_in=g_b_in, gmlp_ln_g=g_gln_g,
                 gmlp_ln_b=g_gln_b, gmlp_w_s=g_ws, gmlp_b_s=g_bs, w_ada_kv=g_w_ada_kv, b_ada_kv=g_b_ada_kv,
                 attn_rel_bias=g_rel)
    weights = dict(w_ada=w_ada, b_ada=b_ada, ln_g=ln_g, ln_b=ln_b, ffn_gu=ffn_gu, ffn_down=ffn_down,
                   gmlp_w_in=gmlp_w_in, gmlp_b_in=gmlp_b_in, gmlp_ln_g=gmlp_ln_g, gmlp_ln_b=gmlp_ln_b,
                   gmlp_w_s=gmlp_w_s, gmlp_b_s=gmlp_b_s, gmlp_w_out=gmlp_w_out, w_ada_kv=w_ada_kv,
                   b_ada_kv=b_ada_kv, w_kv=w_kv, attn_w_q=attn_w_q, attn_rel_bias=attn_rel_bias, attn_w_o=attn_w_o)
    ms = dict(w_ada=m_w_ada, b_ada=m_b_ada, ln_g=m_ln_g, ln_b=m_ln_b, ffn_gu=m_ffn_gu, ffn_down=m_ffn_down,
              gmlp_w_in=m_gmlp_w_in, gmlp_b_in=m_gmlp_b_in, gmlp_ln_g=m_gmlp_ln_g, gmlp_ln_b=m_gmlp_ln_b,
              gmlp_w_s=m_gmlp_w_s, gmlp_b_s=m_gmlp_b_s, gmlp_w_out=m_gmlp_w_out, w_ada_kv=m_w_ada_kv,
              b_ada_kv=m_b_ada_kv, w_kv=m_w_kv, attn_w_q=m_attn_w_q, attn_rel_bias=m_attn_rel_bias,
              attn_w_o=m_attn_w_o)
    vs = dict(w_ada=v_w_ada, b_ada=v_b_ada, ln_g=v_ln_g, ln_b=v_ln_b, ffn_gu=v_ffn_gu, ffn_down=v_ffn_down,
              gmlp_w_in=v_gmlp_w_in, gmlp_b_in=v_gmlp_b_in, gmlp_ln_g=v_gmlp_ln_g, gmlp_ln_b=v_gmlp_ln_b,
              gmlp_w_s=v_gmlp_w_s, gmlp_b_s=v_gmlp_b_s, gmlp_w_out=v_gmlp_w_out, w_ada_kv=v_w_ada_kv,
              b_ada_kv=v_b_ada_kv, w_kv=v_w_kv, attn_w_q=v_attn_w_q, attn_rel_bias=v_attn_rel_bias,
              attn_w_o=v_attn_w_o)
    order = ["w_ada", "b_ada", "ln_g", "ln_b", "ffn_gu", "ffn_down", "gmlp_w_in", "gmlp_b_in", "gmlp_ln_g",
             "gmlp_ln_b", "gmlp_w_s", "gmlp_b_s", "gmlp_w_out", "w_ada_kv", "b_ada_kv", "w_kv", "attn_w_q",
             "attn_rel_bias", "attn_w_o"]
    big_names = ["w_ada", "w_ada_kv"] + stack_names
    small_names = [nm for nm in order if nm not in big_names]
    delta, new_m, new_v = {}, {}, {}

    def adamw_big(nm):
        shp = weights[nm].shape
        two_d = (-1, shp[-1])
        d, a, b = adamw(weights[nm].reshape(two_d), grads[nm].reshape(two_d), ms[nm].reshape(two_d),
                        vs[nm].reshape(two_d), "adamw")
        delta[nm], new_m[nm], new_v[nm] = d.reshape(shp), a.reshape(shp), b.reshape(shp)

    adamw_big("w_ada")
    adamw_big("w_ada_kv")
    shapes = [weights[nm].shape for nm in small_names]
    d, a, b = adamw(_pack_rows([weights[nm] for nm in small_names]), _pack_rows([grads[nm] for nm in small_names]),
                    _pack_rows([ms[nm] for nm in small_names]), _pack_rows([vs[nm] for nm in small_names]),
                    "adamw_small")
    for nm, dd, aa, bb in zip(small_names, _unpack_rows(d, shapes), _unpack_rows(a, shapes), _unpack_rows(b, shapes)):
        delta[nm], new_m[nm], new_v[nm] = dd, aa, bb

    def full_grad(u):
        lo = jnp.where(ci == 0, own_half[u], sib_half[u])
        hi = jnp.where(ci == 0, sib_half[u], own_half[u])
        return jnp.concatenate([lo, hi], axis=1 if u in col_split else 0)

    def adamw_stack(nm):
        si = stack_names.index(nm)
        g = jnp.stack([full_grad(unit_of[(nm, k)]) for k in range(stacks[si].shape[0])])
        grads[nm] = g.reshape(weights[nm].shape)
        adamw_big(nm)

    late = [stack_names[units[u][0]] for u in last["ids"]]
    early = [nm for nm in stack_names if nm not in late]
    finish_grad_exchange(pending, delta["w_ada"])
    collect_halves(delta["w_ada"])
    for nm in early:
        adamw_stack(nm)
    finish_grad_exchange(last, delta[early[-1]])
    collect_halves(delta[early[-1]])
    for nm in stack_names:
        if nm in late:
            adamw_stack(nm)

    return (loss, grad_x, *[grads[nm] for nm in order], *[delta[nm] for nm in order],
            *[new_m[nm] for nm in order], *[new_v[nm] for nm in order])
```

```python
import functools

import jax
import jax.numpy as jnp
from jax import lax
from jax.experimental import pallas as pl
from jax.experimental.pallas import tpu as pltpu

F32 = jnp.float32
BF16 = jnp.bfloat16
MESH = pl.DeviceIdType.MESH
HIGHEST = lax.Precision.HIGHEST

CHUNK = 64
GMLP_WINDOW = 128
GMLP_GROUPS = 8
HEAD_DIM = 64
LEFT_CHUNKS = 8
BAND = (LEFT_CHUNKS + 1) * CHUNK
LEFT_PAD = LEFT_CHUNKS * CHUNK
MAX_REL = 4 * CHUNK
N_REL = (CHUNK - 1) + MAX_REL + 1
LN_EPS = 1e-5
N_MOD = 9
N_DEV = 8
N_CHIP = 4

ADAM_LR = 0.001
ADAM_B1 = 0.9
ADAM_B2 = 0.999
ADAM_EPS = 1e-08
ADAM_WD = 0.01
ADAM_STEP = 10

LANES = 128
ROW_TILE = 256
MATMUL_ROW_TILE = 512
WGRAD_ROWS = 2048
ATTN_CHUNKS_PER_STEP = 16
VMEM_LIMIT_MB = 56

NT = (((1,), (1,)), ((), ()))
TN = (((0,), (0,)), ((), ()))

ANY = pl.BlockSpec(memory_space=pl.ANY)
VMEM_SPEC = pl.BlockSpec(memory_space=pltpu.VMEM)


def _params(semantics=None):
    kw = dict(vmem_limit_bytes=VMEM_LIMIT_MB * 1024 * 1024)
    if semantics is not None:
        kw["dimension_semantics"] = semantics
    return pltpu.CompilerParams(**kw)


def _sigmoid(v):
    return 0.5 * (1.0 + jnp.tanh(0.5 * v))


def _gelu(v):
    return 0.5 * v * (1.0 + lax.erf(v * (2.0 ** -0.5)))


def _gelu_grad(v):
    return 0.5 * (1.0 + lax.erf(v * (2.0 ** -0.5))) + v * jnp.exp(-0.5 * v * v) * ((2.0 * jnp.pi) ** -0.5)


def _row(m):
    return lambda i: (i, 0)


def _fixed2(i):
    return (0, 0)


def _fixed3(i):
    return (0, 0, 0)


def _resident(shape):
    return pl.BlockSpec(shape, _fixed2 if len(shape) == 2 else _fixed3, pipeline_mode=pl.Buffered(1))


def mod_matmul(x, scl, shift, w, bias, out_dtype, name):
    S, D = x.shape
    NS, _, n = w.shape
    tm = min(MATMUL_ROW_TILE, S)
    has_bias = bias is not None

    def body(*refs):
        if has_bias:
            x_ref, scl_ref, sh_ref, w_ref, b_ref, o_ref, h_ref = refs
        else:
            x_ref, scl_ref, sh_ref, w_ref, o_ref, h_ref = refs
        h = (x_ref[...] * (1.0 + scl_ref[...]) + sh_ref[...]).astype(BF16)
        h_ref[...] = h
        for s in range(NS):
            acc = jnp.dot(h, w_ref[s], preferred_element_type=F32)
            if has_bias:
                acc = acc + b_ref[:, s * n:(s + 1) * n]
            o_ref[:, s * n:(s + 1) * n] = acc.astype(out_dtype)

    in_specs = [pl.BlockSpec((tm, D), _row(0)), pl.BlockSpec((1, D), _fixed2), pl.BlockSpec((1, D), _fixed2),
                _resident((NS, D, n))]
    args = [x, scl, shift, w]
    if has_bias:
        in_specs.append(pl.BlockSpec((1, NS * n), _fixed2))
        args.append(bias)
    return pl.pallas_call(
        body, name=name, grid=(S // tm,), in_specs=in_specs,
        out_specs=[pl.BlockSpec((tm, NS * n), _row(0)), pl.BlockSpec((tm, D), _row(0))],
        out_shape=[jax.ShapeDtypeStruct((S, NS * n), out_dtype), jax.ShapeDtypeStruct((S, D), BF16)],
        compiler_params=_params(("parallel",)),
    )(*args)


def matmul_res_ln(a, w, x, gw, lg, lb, alpha, swiglu, name):
    S, D = x.shape
    K = w.shape[0]
    tm = min(ROW_TILE, S)
    ka = a.shape[1]

    def body(a_ref, w_ref, x_ref, gw_ref, lg_ref, lb_ref, xn_ref, xh_ref, rs_ref, y_ref, *act_ref):
        if swiglu:
            g = a_ref[:, :K]
            u = a_ref[:, K:]
            act = g * _sigmoid(g) * u
            act_ref[0][...] = act
        else:
            act = a_ref[...].astype(BF16)
        y = jnp.dot(act, w_ref[...], preferred_element_type=F32)
        z = alpha * x_ref[...] + gw_ref[...] * y
        mu = jnp.mean(z, axis=-1, keepdims=True)
        zc = z - mu
        var = jnp.mean(zc * zc, axis=-1, keepdims=True)
        rstd = lax.rsqrt(var + LN_EPS)
        xhat = zc * rstd
        xn_ref[...] = xhat * lg_ref[...] + lb_ref[...]
        xh_ref[...] = xhat
        rs_ref[...] = rstd
        y_ref[...] = y.astype(BF16)

    vec = pl.BlockSpec((1, D), _fixed2)
    out_specs = [pl.BlockSpec((tm, D), _row(0)), pl.BlockSpec((tm, D), _row(0)), pl.BlockSpec((tm, 1), _row(0)),
                 pl.BlockSpec((tm, D), _row(0))]
    out_shape = [jax.ShapeDtypeStruct((S, D), F32), jax.ShapeDtypeStruct((S, D), F32),
                 jax.ShapeDtypeStruct((S, 1), F32), jax.ShapeDtypeStruct((S, D), BF16)]
    if swiglu:
        out_specs.append(pl.BlockSpec((tm, K), _row(0)))
        out_shape.append(jax.ShapeDtypeStruct((S, K), BF16))
    return pl.pallas_call(
        body, name=name, grid=(S // tm,),
        in_specs=[pl.BlockSpec((tm, ka), _row(0)), _resident((K, D)), pl.BlockSpec((tm, D), _row(0)),
                  vec, vec, vec],
        out_specs=out_specs, out_shape=out_shape,
        compiler_params=_params(("parallel",)),
    )(a, w, x, gw, lg, lb)


def _ln_res_bwd_tile(d, xh_ref, rs_ref, lg_ref, y_ref, gw_ref, wres, alpha, dxa_ref, dy_ref, acc_ref, row0):
    xh = xh_ref[...]
    dxh = d * lg_ref[...]
    m1 = jnp.mean(dxh, axis=-1, keepdims=True)
    m2 = jnp.mean(dxh * xh, axis=-1, keepdims=True)
    dz = rs_ref[...] * (dxh - m1 - xh * m2)
    dxa_ref[...] = alpha * dz
    dy_ref[...] = (gw_ref[...] * dz).astype(BF16)
    acc_ref[row0:row0 + 1, :] += jnp.sum(d * xh, axis=0, keepdims=True)
    acc_ref[row0 + 1:row0 + 2, :] += jnp.sum(d, axis=0, keepdims=True)
    acc_ref[row0 + 2:row0 + 3, :] += jnp.sum((wres * dz) * y_ref[...].astype(F32), axis=0, keepdims=True)


def ln_res_bwd(dxn, xhat, rstd, lg, y, gw, wres, alpha, name):
    S, D = dxn.shape
    tm = min(MATMUL_ROW_TILE, S)

    def body(dxn_ref, xh_ref, rs_ref, lg_ref, y_ref, gw_ref, dxa_ref, dy_ref, acc_ref):
        @pl.when(pl.program_id(0) == 0)
        def _():
            acc_ref[...] = jnp.zeros_like(acc_ref)

        _ln_res_bwd_tile(dxn_ref[...], xh_ref, rs_ref, lg_ref, y_ref, gw_ref, wres, alpha, dxa_ref, dy_ref, acc_ref, 0)

    vec = pl.BlockSpec((1, D), _fixed2)
    tile = pl.BlockSpec((tm, D), _row(0))
    return pl.pallas_call(
        body, name=name, grid=(S // tm,),
        in_specs=[tile, tile, pl.BlockSpec((tm, 1), _row(0)), vec, tile, vec],
        out_specs=[tile, tile, pl.BlockSpec((8, D), _fixed2)],
        out_shape=[jax.ShapeDtypeStruct((S, D), F32), jax.ShapeDtypeStruct((S, D), BF16),
                   jax.ShapeDtypeStruct((8, D), F32)],
        compiler_params=_params(("arbitrary",)),
    )(dxn, xhat, rstd, lg, y, gw)


def ffn_act_bwd(dy, wd, gu, after, name):
    S, D = dy.shape
    K = wd.shape[0]
    tm = min(ROW_TILE, S)

    def body(dy_ref, wd_ref, gu_ref, after_ref, o_ref):
        da = lax.dot_general(dy_ref[...], wd_ref[...], NT, preferred_element_type=F32).astype(BF16)
        g = gu_ref[:, :K]
        u = gu_ref[:, K:]
        sg = _sigmoid(g)
        o_ref[:, :K] = da * u * (sg * (1.0 + g * (1.0 - sg)))
        o_ref[:, K:] = da * (g * sg)

    return pl.pallas_call(
        body, name=name, grid=(S // tm,),
        in_specs=[pl.BlockSpec((tm, D), _row(0)), _resident((K, D)), pl.BlockSpec((tm, 2 * K), _row(0)), ANY],
        out_specs=pl.BlockSpec((tm, 2 * K), _row(0)),
        out_shape=jax.ShapeDtypeStruct((S, 2 * K), BF16),
        compiler_params=_params(("parallel",)),
    )(dy, wd, gu, after)


def matmul_nt(a, w, after, name):
    S, D = a.shape
    K = w.shape[0]
    tm = min(MATMUL_ROW_TILE, S)

    def body(a_ref, w_ref, after_ref, o_ref):
        o_ref[...] = lax.dot_general(a_ref[...], w_ref[...], NT, preferred_element_type=F32).astype(BF16)

    return pl.pallas_call(
        body, name=name, grid=(S // tm,),
        in_specs=[pl.BlockSpec((tm, D), _row(0)), _resident((K, D)), ANY],
        out_specs=pl.BlockSpec((tm, K), _row(0)),
        out_shape=jax.ShapeDtypeStruct((S, K), BF16),
        compiler_params=_params(("parallel",)),
    )(a, w, after)


def dgrad_mod(dpre, w, dxa, xin, scl, prev, alpha, name):
    S, D = xin.shape
    NS, _, n = w.shape
    tm = min(ROW_TILE, S)
    wres = prev[5] if prev is not None else None

    def body(*refs):
        dp_ref, w_ref, dxa_ref, xin_ref, scl_ref = refs[:5]
        acc_ref = refs[-1]

        @pl.when(pl.program_id(0) == 0)
        def _():
            acc_ref[...] = jnp.zeros_like(acc_ref)

        dh = jnp.zeros((tm, D), F32)
        for s in range(NS):
            dh = dh + lax.dot_general(dp_ref[:, s * n:(s + 1) * n].astype(BF16), w_ref[s], NT,
                                      preferred_element_type=F32)
        dx = dxa_ref[...] + dh * (1.0 + scl_ref[...])
        acc_ref[0:1, :] += jnp.sum(dh * xin_ref[...], axis=0, keepdims=True)
        acc_ref[1:2, :] += jnp.sum(dh, axis=0, keepdims=True)
        if prev is None:
            refs[5][...] = dx
        else:
            xh_ref, rs_ref, lg_ref, y_ref, gw_ref, pdxa_ref, pdy_ref = refs[5:12]
            _ln_res_bwd_tile(dx, xh_ref, rs_ref, lg_ref, y_ref, gw_ref, wres, alpha, pdxa_ref, pdy_ref, acc_ref, 2)

    tile = pl.BlockSpec((tm, D), _row(0))
    vec = pl.BlockSpec((1, D), _fixed2)
    in_specs = [pl.BlockSpec((tm, NS * n), _row(0)), _resident((NS, D, n)), tile, tile, vec]
    args = [dpre, w, dxa, xin, scl]
    if prev is None:
        out_specs = [tile]
        out_shape = [jax.ShapeDtypeStruct((S, D), F32)]
    else:
        in_specs += [tile, pl.BlockSpec((tm, 1), _row(0)), vec, tile, vec]
        args += list(prev[:5])
        out_specs = [tile, tile]
        out_shape = [jax.ShapeDtypeStruct((S, D), F32), jax.ShapeDtypeStruct((S, D), BF16)]
    return pl.pallas_call(
        body, name=name, grid=(S // tm,), in_specs=in_specs,
        out_specs=out_specs + [pl.BlockSpec((8, D), _fixed2)],
        out_shape=out_shape + [jax.ShapeDtypeStruct((8, D), F32)],
        compiler_params=_params(("arbitrary",)),
    )(*args)


PAIR_COLLECTIVE_ID = 1
FILL_COLLECTIVE_ID = 2


def wgrad_pair(a, b, J, kb, nb, a_block, b_block, half_idx, name):
    S = b.shape[0]
    ts = min(WGRAD_ROWS, S)
    nsteps = S // ts

    def body(h_ref, a_ref, b_ref, o_ref, acc_ref, send_buf, recv_buf, send_sems, recv_sems):
        jj, si = pl.program_id(0), pl.program_id(1)
        x, y, c = _coords()
        j = lax.rem(jj, J)
        last = si == nsteps - 1

        def copy(blk):
            return pltpu.make_async_remote_copy(
                src_ref=send_buf.at[blk], dst_ref=recv_buf.at[blk], send_sem=send_sems.at[blk],
                recv_sem=recv_sems.at[blk], device_id=(x, y, 1 - c), device_id_type=MESH)

        @pl.when(jnp.logical_and(jj == 0, si == 0))
        def _():
            barrier = pltpu.get_barrier_semaphore()
            pl.semaphore_signal(barrier, inc=1, device_id=(x, y, 1 - c), device_id_type=MESH)
            pl.semaphore_wait(barrier, 1)

        @pl.when(si == 0)
        def _():
            acc_ref[...] = jnp.zeros_like(acc_ref)

        acc_ref[...] += lax.dot_general(a_ref[...], b_ref[...].astype(BF16), TN, preferred_element_type=F32)

        @pl.when(jnp.logical_and(last, jj < J))
        def _():
            send_buf[j] = acc_ref[...].astype(BF16)
            copy(j).start()

        @pl.when(jnp.logical_and(last, jj >= J))
        def _():
            copy(j).wait_recv()
            o_ref[...] = (acc_ref[...] + recv_buf[j].astype(F32)).astype(BF16)

        @pl.when(jnp.logical_and(last, jj == 2 * J - 1))
        def _():
            for blk in range(J):
                copy(blk).wait_send()

    def half(jj, h):
        return jnp.where(jj < J, 1 - h[0], h[0])

    return pl.pallas_call(
        body, name=name,
        grid_spec=pltpu.PrefetchScalarGridSpec(
            num_scalar_prefetch=1, grid=(2 * J, nsteps),
            in_specs=[pl.BlockSpec((ts, kb), lambda jj, s, h: (s, a_block(lax.rem(jj, J), half(jj, h)))),
                      pl.BlockSpec((ts, nb), lambda jj, s, h: (s, b_block(lax.rem(jj, J), half(jj, h))))],
            out_specs=pl.BlockSpec((None, kb, nb), lambda jj, s, h: (jnp.maximum(jj - J, 0), 0, 0)),
            scratch_shapes=[pltpu.VMEM((kb, nb), F32), pltpu.VMEM((J, kb, nb), BF16), pltpu.VMEM((J, kb, nb), BF16),
                            pltpu.SemaphoreType.DMA((J,)), pltpu.SemaphoreType.DMA((J,))]),
        out_shape=jax.ShapeDtypeStruct((J, kb, nb), BF16),
        compiler_params=pltpu.CompilerParams(
            vmem_limit_bytes=VMEM_LIMIT_MB * 1024 * 1024, dimension_semantics=("arbitrary", "arbitrary"),
            collective_id=PAIR_COLLECTIVE_ID),
    )(half_idx, a, b)


def _window_mask():
    t = lax.broadcasted_iota(jnp.int32, (GMLP_WINDOW, GMLP_WINDOW), 0)
    s = lax.broadcasted_iota(jnp.int32, (GMLP_WINDOW, GMLP_WINDOW), 1)
    return ((s // CHUNK) <= (t // CHUNK)).astype(F32)


def sgu_fwd(pre, glg, glb, ws, bst, name):
    S, H2 = pre.shape
    H = H2 // 2
    W, G = GMLP_WINDOW, GMLP_GROUPS
    gd = H // G
    tm = min(ROW_TILE, S)

    def body(pre_ref, glg_ref, glb_ref, ws_ref, bst_ref, q_ref):
        u = _gelu(pre_ref[:, :H])
        v = _gelu(pre_ref[:, H:])
        mu = jnp.mean(v, axis=-1, keepdims=True)
        vc = v - mu
        var = jnp.mean(vc * vc, axis=-1, keepdims=True)
        vn = ((vc * lax.rsqrt(var + LN_EPS)) * glg_ref[...] + glb_ref[...]).astype(BF16)
        mask = _window_mask()
        for g in range(G):
            wsg = (ws_ref[g] * mask).astype(BF16)
            bcol = bst_ref[:, g:g + 1]
            for wi in range(tm // W):
                rows = slice(wi * W, (wi + 1) * W)
                cols = slice(g * gd, (g + 1) * gd)
                s = jnp.dot(wsg, vn[rows, cols], preferred_element_type=F32) + bcol
                q_ref[rows, cols] = (u[rows, cols] * s).astype(BF16)

    return pl.pallas_call(
        body, name=name, grid=(S // tm,),
        in_specs=[pl.BlockSpec((tm, H2), _row(0)), pl.BlockSpec((1, H), _fixed2), pl.BlockSpec((1, H), _fixed2),
                  pl.BlockSpec((G, W, W), _fixed3), pl.BlockSpec((W, G), _fixed2)],
        out_specs=pl.BlockSpec((tm, H), _row(0)),
        out_shape=jax.ShapeDtypeStruct((S, H), BF16),
        compiler_params=_params(("parallel",)),
    )(pre, glg, glb, ws, bst)


def sgu_bwd(dq, pre, glg, glb, ws, bst, name):
    S, H2 = pre.shape
    H = H2 // 2
    W, G = GMLP_WINDOW, GMLP_GROUPS
    gd = H // G
    tm = min(ROW_TILE, S)

    def body(dq_ref, pre_ref, glg_ref, glb_ref, ws_ref, bst_ref,
             dpre_ref, dws_ref, dss_ref, dgl_ref, dbin_ref, du_s, dvn_s):
        @pl.when(pl.program_id(0) == 0)
        def _():
            dws_ref[...] = jnp.zeros_like(dws_ref)
            dss_ref[...] = jnp.zeros_like(dss_ref)
            dgl_ref[...] = jnp.zeros_like(dgl_ref)
            dbin_ref[...] = jnp.zeros_like(dbin_ref)

        pu = pre_ref[:, :H]
        pv = pre_ref[:, H:]
        u = _gelu(pu)
        v = _gelu(pv)
        mu = jnp.mean(v, axis=-1, keepdims=True)
        vc = v - mu
        var = jnp.mean(vc * vc, axis=-1, keepdims=True)
        rstd = lax.rsqrt(var + LN_EPS)
        vhat = vc * rstd
        vn = (vhat * glg_ref[...] + glb_ref[...]).astype(BF16)
        mask = _window_mask()
        for g in range(G):
            wsg = (ws_ref[g] * mask).astype(BF16)
            bcol = bst_ref[:, g:g + 1]
            cols = slice(g * gd, (g + 1) * gd)
            for wi in range(tm // W):
                rows = slice(wi * W, (wi + 1) * W)
                vblk = vn[rows, cols]
                s = jnp.dot(wsg, vblk, preferred_element_type=F32) + bcol
                dqb = dq_ref[rows, cols].astype(F32)
                du_s[rows, cols] = dqb * s
                ds = dqb * u[rows, cols]
                dss_ref[:, cols] += ds
                dsb = ds.astype(BF16)
                dvn_s[rows, cols] = lax.dot_general(wsg, dsb, TN, preferred_element_type=F32)
                dws_ref[g] += lax.dot_general(dsb, vblk, NT, preferred_element_type=F32) * mask
        dvn = dvn_s[...]
        dgl_ref[0:1, :] += jnp.sum(dvn * vhat, axis=0, keepdims=True)
        dgl_ref[1:2, :] += jnp.sum(dvn, axis=0, keepdims=True)
        dvh = dvn * glg_ref[...]
        m1 = jnp.mean(dvh, axis=-1, keepdims=True)
        m2 = jnp.mean(dvh * vhat, axis=-1, keepdims=True)
        dv = rstd * (dvh - m1 - vhat * m2)
        dpu = du_s[...] * _gelu_grad(pu)
        dpv = dv * _gelu_grad(pv)
        dbin_ref[0:1, :H] += jnp.sum(dpu, axis=0, keepdims=True)
        dbin_ref[0:1, H:] += jnp.sum(dpv, axis=0, keepdims=True)
        dpre_ref[:, :H] = dpu.astype(BF16)
        dpre_ref[:, H:] = dpv.astype(BF16)

    return pl.pallas_call(
        body, name=name, grid=(S // tm,),
        in_specs=[pl.BlockSpec((tm, H), _row(0)), pl.BlockSpec((tm, H2), _row(0)), pl.BlockSpec((1, H), _fixed2),
                  pl.BlockSpec((1, H), _fixed2), pl.BlockSpec((G, W, W), _fixed3), pl.BlockSpec((W, G), _fixed2)],
        out_specs=[pl.BlockSpec((tm, H2), _row(0)), pl.BlockSpec((G, W, W), _fixed3), pl.BlockSpec((W, H), _fixed2),
                   pl.BlockSpec((8, H), _fixed2), pl.BlockSpec((8, H2), _fixed2)],
        out_shape=[jax.ShapeDtypeStruct((S, H2), BF16), jax.ShapeDtypeStruct((G, W, W), F32),
                   jax.ShapeDtypeStruct((W, H), F32), jax.ShapeDtypeStruct((8, H), F32),
                   jax.ShapeDtypeStruct((8, H2), F32)],
        scratch_shapes=[pltpu.VMEM((tm, H), F32), pltpu.VMEM((tm, H), F32)],
        compiler_params=_params(("arbitrary",)),
    )(dq, pre, glg, glb, ws, bst)


def group_lane_sum(dss, name):
    W, H = dss.shape
    gd = H // GMLP_GROUPS

    def body(d_ref, o_ref):
        j = lax.broadcasted_iota(jnp.int32, (H, LANES), 0)
        g = lax.broadcasted_iota(jnp.int32, (H, LANES), 1)
        ind = ((j // gd) == g).astype(F32)
        o_ref[...] = jnp.dot(d_ref[...], ind, preferred_element_type=F32, precision=HIGHEST)

    return pl.pallas_call(
        body, name=name, in_specs=[VMEM_SPEC], out_specs=VMEM_SPEC,
        out_shape=jax.ShapeDtypeStruct((W, LANES), F32), compiler_params=_params(),
    )(dss)


def _attn_load(j, cps, q_ref, k_ref, v_ref):
    r = lax.broadcasted_iota(jnp.int32, (CHUNK, BAND), 1)
    chunks = []
    for cc in range(cps):
        start = pl.multiple_of((j * cps + cc) * CHUNK, CHUNK)
        chunks.append((q_ref[cc * CHUNK:(cc + 1) * CHUNK, :], k_ref[pl.ds(start, BAND), :],
                       v_ref[pl.ds(start, BAND), :], (r + start) >= LEFT_PAD))
    return chunks


def _attn_probs(chunks, b_ref, sels, scale):
    qms = [[jnp.where(sel, q2, jnp.zeros_like(q2)) for sel in sels] for q2, _, _, _ in chunks]
    raw = [[lax.dot_general(qm, k2, NT, preferred_element_type=F32) for qm in qms[cc]]
           for cc, (_, k2, _, _) in enumerate(chunks)]
    probs = []
    for cc, (_, _, _, valid) in enumerate(chunks):
        row = []
        for sub in range(2):
            s = jnp.where(valid, raw[cc][sub] * scale + b_ref[sub], -jnp.inf)
            e = jnp.exp(s - jnp.max(s, axis=-1, keepdims=True))
            row.append(e / jnp.sum(e, axis=-1, keepdims=True))
        probs.append(row)
    return qms, probs


def attn_fwd(q, kpad, vpad, bias, name):
    S, D = q.shape
    HP = D // LANES
    cps = min(ATTN_CHUNKS_PER_STEP, S // CHUNK)
    tq = cps * CHUNK
    scale = HEAD_DIM ** -0.5

    def body(q_ref, k_ref, v_ref, b_ref, o_ref):
        sel0 = lax.broadcasted_iota(jnp.int32, (CHUNK, LANES), 1) < HEAD_DIM
        chunks = _attn_load(pl.program_id(1), cps, q_ref, k_ref, v_ref)
        _, probs = _attn_probs(chunks, b_ref, (sel0, jnp.logical_not(sel0)), scale)
        outs = [[jnp.dot(probs[cc][sub].astype(BF16), v2, preferred_element_type=F32) for sub in range(2)]
                for cc, (_, _, v2, _) in enumerate(chunks)]
        o_ref[...] = jnp.concatenate([jnp.where(sel0, o[0], o[1]) for o in outs], axis=0).astype(BF16)

    kv_spec = pl.BlockSpec((S + LEFT_PAD, LANES), lambda h, j: (0, h))
    return pl.pallas_call(
        body, name=name, grid=(HP, S // tq),
        in_specs=[pl.BlockSpec((tq, LANES), lambda h, j: (j, h)), kv_spec, kv_spec,
                  pl.BlockSpec((2, CHUNK, BAND), lambda h, j: (h, 0, 0))],
        out_specs=pl.BlockSpec((tq, LANES), lambda h, j: (j, h)),
        out_shape=jax.ShapeDtypeStruct((S, D), BF16),
        compiler_params=_params(("parallel", "parallel")),
    )(q, kpad, vpad, bias)


def attn_bwd(q, do, kpad, vpad, bias, dk_in, dv_in, name):
    S, D = q.shape
    HP = D // LANES
    NH = 2 * HP
    cps = min(ATTN_CHUNKS_PER_STEP, S // CHUNK)
    tq = cps * CHUNK
    nj = S // tq
    scale = HEAD_DIM ** -0.5

    def body(q_ref, do_ref, k_ref, v_ref, b_ref, dki_ref, dvi_ref, dq_ref, dk_ref, dv_ref, db_ref, dk_acc, dv_acc):
        j = pl.program_id(1)

        @pl.when(j == 0)
        def _():
            dk_acc[:LEFT_PAD, :] = jnp.zeros((LEFT_PAD, LANES), F32)
            dv_acc[:LEFT_PAD, :] = jnp.zeros((LEFT_PAD, LANES), F32)
            dk_acc[LEFT_PAD:, :] = dki_ref[...]
            dv_acc[LEFT_PAD:, :] = dvi_ref[...]
            db_ref[...] = jnp.zeros_like(db_ref)

        sel0 = lax.broadcasted_iota(jnp.int32, (CHUNK, LANES), 1) < HEAD_DIM
        sels = (sel0, jnp.logical_not(sel0))
        chunks = _attn_load(j, cps, q_ref, k_ref, v_ref)
        pairs = [(cc, sub) for cc in range(cps) for sub in range(2)]
        qms, probs = _attn_probs(chunks, b_ref, sels, scale)
        doms = [[jnp.where(sel, do_ref[cc * CHUNK:(cc + 1) * CHUNK, :], jnp.zeros((CHUNK, LANES), BF16))
                 for sel in sels] for cc in range(cps)]
        dps = {(cc, sub): lax.dot_general(doms[cc][sub], chunks[cc][2], NT, preferred_element_type=F32)
               for cc, sub in pairs}
        dss = {}
        for cc, sub in pairs:
            p = probs[cc][sub]
            dss[cc, sub] = p * (dps[cc, sub] - jnp.sum(dps[cc, sub] * p, axis=-1, keepdims=True))
        dsb = {key: ds.astype(BF16) for key, ds in dss.items()}
        dqs = {(cc, sub): jnp.dot(dsb[cc, sub], chunks[cc][1], preferred_element_type=F32) * scale
               for cc, sub in pairs}
        dks = {(cc, sub): lax.dot_general(dsb[cc, sub], qms[cc][sub], TN, preferred_element_type=F32) * scale
               for cc, sub in pairs}
        dvs = {(cc, sub): lax.dot_general(probs[cc][sub].astype(BF16), doms[cc][sub], TN,
                                          preferred_element_type=F32) for cc, sub in pairs}
        dq_ref[...] = jnp.concatenate([jnp.where(sel0, dqs[cc, 0], dqs[cc, 1]) for cc in range(cps)],
                                      axis=0).astype(BF16)
        for sub in range(2):
            total = dss[0, sub]
            for cc in range(1, cps):
                total = total + dss[cc, sub]
            db_ref[sub] += total
        dk_parts = [dks[cc, 0] + dks[cc, 1] for cc in range(cps)]
        dv_parts = [dvs[cc, 0] + dvs[cc, 1] for cc in range(cps)]

        def window(parts):
            blocks = []
            for rb in range(cps - 1 + BAND // CHUNK):
                acc = None
                for cc in range(cps):
                    b = rb - cc
                    if 0 <= b < BAND // CHUNK:
                        piece = parts[cc][b * CHUNK:(b + 1) * CHUNK, :]
                        acc = piece if acc is None else acc + piece
                blocks.append(acc)
            return jnp.concatenate(blocks, axis=0)

        span = pl.ds(pl.multiple_of(j * cps * CHUNK, CHUNK), (cps - 1) * CHUNK + BAND)
        dk_acc[span, :] += window(dk_parts)
        dv_acc[span, :] += window(dv_parts)

        @pl.when(j == nj - 1)
        def _():
            dk_ref[...] = dk_acc[LEFT_PAD:, :]
            dv_ref[...] = dv_acc[LEFT_PAD:, :]

    q_spec = pl.BlockSpec((tq, LANES), lambda h, j: (j, h))
    kv_spec = pl.BlockSpec((S + LEFT_PAD, LANES), lambda h, j: (0, h))
    col_spec = pl.BlockSpec((S, LANES), lambda h, j: (0, h))
    b_spec = pl.BlockSpec((2, CHUNK, BAND), lambda h, j: (h, 0, 0))
    return pl.pallas_call(
        body, name=name, grid=(HP, nj),
        in_specs=[q_spec, q_spec, kv_spec, kv_spec, b_spec, col_spec, col_spec],
        out_specs=[q_spec, col_spec, col_spec, b_spec],
        out_shape=[jax.ShapeDtypeStruct((S, D), BF16), jax.ShapeDtypeStruct((S, D), F32),
                   jax.ShapeDtypeStruct((S, D), F32), jax.ShapeDtypeStruct((NH, CHUNK, BAND), F32)],
        scratch_shapes=[pltpu.VMEM((S + LEFT_PAD, LANES), F32), pltpu.VMEM((S + LEFT_PAD, LANES), F32)],
        compiler_params=_params(("parallel", "arbitrary")),
    )(q, do, kpad, vpad, bias, dk_in, dv_in)


def _rel_onehot(t):
    r = lax.broadcasted_iota(jnp.int32, (BAND, N_REL), 0)
    i = lax.broadcasted_iota(jnp.int32, (BAND, N_REL), 1)
    idx = jnp.clip(t + LEFT_PAD - r, -(CHUNK - 1), MAX_REL) + (CHUNK - 1)
    return (idx == i).astype(BF16)


def _split3(v):
    hi = v.astype(BF16)
    rest = v - hi.astype(F32)
    mid = rest.astype(BF16)
    return hi, mid, (rest - mid.astype(F32)).astype(BF16)


def _dot_onehot(parts, onehot, dims):
    hi, mid, lo = [lax.dot_general(p, onehot, dims, preferred_element_type=F32) for p in parts]
    return (hi + mid) + lo


def bias_expand(rb, name):
    NH = rb.shape[0]

    def body(rb_ref, o_ref):
        parts = _split3(rb_ref[...])

        def step(t, carry):
            o_ref[t] = _dot_onehot(parts, _rel_onehot(t), NT)
            return carry

        lax.fori_loop(0, CHUNK, step, 0)

    return pl.pallas_call(
        body, name=name, in_specs=[VMEM_SPEC], out_specs=VMEM_SPEC,
        out_shape=jax.ShapeDtypeStruct((CHUNK, NH, BAND), F32), compiler_params=_params(),
    )(rb)


def bias_grad(dsum, name):
    NH = dsum.shape[1]

    def body(d_ref, o_ref):
        def step(t, acc):
            return acc + _dot_onehot(_split3(d_ref[t]), _rel_onehot(t), (((1,), (0,)), ((), ())))

        o_ref[...] = lax.fori_loop(0, CHUNK, step, jnp.zeros((NH, N_REL), F32))

    return pl.pallas_call(
        body, name=name, in_specs=[VMEM_SPEC], out_specs=VMEM_SPEC,
        out_shape=jax.ShapeDtypeStruct((NH, N_REL), F32), compiler_params=_params(),
    )(dsum)


def loss_grad(y, tgt, name):
    S, D = y.shape
    tm = min(MATMUL_ROW_TILE, S)

    def body(y_ref, t_ref, d_ref, acc_ref):
        @pl.when(pl.program_id(0) == 0)
        def _():
            acc_ref[...] = jnp.zeros_like(acc_ref)

        err = y_ref[...] - t_ref[...]
        d_ref[...] = err * (1.0 / D)
        acc_ref[0:1, :] += jnp.sum(err * err, axis=0, keepdims=True)

    tile = pl.BlockSpec((tm, D), _row(0))
    return pl.pallas_call(
        body, name=name, grid=(S // tm,), in_specs=[tile, tile],
        out_specs=[tile, pl.BlockSpec((8, D), _fixed2)],
        out_shape=[jax.ShapeDtypeStruct((S, D), F32), jax.ShapeDtypeStruct((8, D), F32)],
        compiler_params=_params(("arbitrary",)),
    )(y, tgt)


def _col_tile(n):
    for t in (768, 512, 256, 128):
        if n % t == 0:
            return t
    return n


def ada_fwd(c_all, w, b, name):
    L, D, n = w.shape
    tn = _col_tile(n)

    def body(c_ref, w_ref, b_ref, o_ref):
        cv = c_ref[...]
        ca = cv * _sigmoid(cv)
        o_ref[...] = jnp.dot(ca, w_ref[...], preferred_element_type=F32, precision=HIGHEST) + b_ref[...]

    return pl.pallas_call(
        body, name=name, grid=(L, n // tn),
        in_specs=[pl.BlockSpec((N_DEV, D), lambda l, j: (0, 0)), pl.BlockSpec((None, D, tn), lambda l, j: (l, 0, j)),
                  pl.BlockSpec((None, 1, tn), lambda l, j: (l, 0, j))],
        out_specs=pl.BlockSpec((None, N_DEV, tn), lambda l, j: (l, 0, j)),
        out_shape=jax.ShapeDtypeStruct((L, N_DEV, n), F32),
        compiler_params=_params(("parallel", "parallel")),
    )(c_all, w, b)


def ada_wgrad(c_all_t, dmod, name):
    L, _, n = dmod.shape
    D = c_all_t.shape[0]
    tn = _col_tile(n)

    def body(c_ref, d_ref, o_ref):
        cv = c_ref[...]
        ca = cv * _sigmoid(cv)
        o_ref[...] = jnp.dot(ca, d_ref[...], preferred_element_type=F32, precision=HIGHEST)

    return pl.pallas_call(
        body, name=name, grid=(L, n // tn),
        in_specs=[pl.BlockSpec((D, N_DEV), lambda l, j: (0, 0)), pl.BlockSpec((None, N_DEV, tn), lambda l, j: (l, 0, j))],
        out_specs=pl.BlockSpec((None, D, tn), lambda l, j: (l, 0, j)),
        out_shape=jax.ShapeDtypeStruct((L, D, n), F32),
        compiler_params=_params(("parallel", "parallel")),
    )(c_all_t, dmod)


ELEMENTWISE_BLOCK_BYTES = 3 * 1024 * 1024


def _elementwise_rows(rows, row_bytes):
    for t in (4096, 2048, 1024, 512, 256, 128, 64, 32, 16):
        if rows % t == 0 and t * row_bytes <= ELEMENTWISE_BLOCK_BYTES:
            return t
    return rows


def sum_leading(a, name):
    n, M, N = a.shape
    tr = _elementwise_rows(M, n * N * 4)

    def body(a_ref, o_ref):
        acc = a_ref[0].astype(F32)
        for i in range(1, n):
            acc = acc + a_ref[i].astype(F32)
        o_ref[...] = acc

    return pl.pallas_call(
        body, name=name, grid=(M // tr,),
        in_specs=[pl.BlockSpec((n, tr, N), lambda i: (0, i, 0))],
        out_specs=pl.BlockSpec((tr, N), _row(0)),
        out_shape=jax.ShapeDtypeStruct((M, N), F32),
        compiler_params=_params(("parallel",)),
    )(a)


def chip_sum(psum, land, chip_idx, transposed, name):
    n, M, N = psum.shape
    tr = M if transposed else _elementwise_rows(M, N * 8)

    def body(s_ref, p_ref, a_ref, b_ref, c_ref, o_ref):
        total = ((p_ref[...].astype(F32) + a_ref[...].astype(F32)) + b_ref[...].astype(F32)) + c_ref[...].astype(F32)
        o_ref[...] = jnp.transpose(total) if transposed else total

    def entry(j):
        return pl.BlockSpec((None, tr, N), lambda i, s: ((s[0] + j) % n, i, 0))

    out_block, out_dims = ((N, tr), (N, M)) if transposed else ((tr, N), (M, N))
    return pl.pallas_call(
        body, name=name,
        grid_spec=pltpu.PrefetchScalarGridSpec(
            num_scalar_prefetch=1, grid=(M // tr,),
            in_specs=[entry(0), entry(1), entry(2), entry(3)],
            out_specs=pl.BlockSpec(out_block, lambda i, s: (0, 0) if transposed else (i, 0))),
        out_shape=jax.ShapeDtypeStruct(out_dims, F32),
        compiler_params=_params(("parallel",)),
    )(chip_idx, psum, land, land, land)


def adamw(w, g, m, v, name):
    M, N = w.shape
    tr = _elementwise_rows(M, N * 4)
    c1 = 1.0 - ADAM_B1 ** ADAM_STEP
    c2 = 1.0 - ADAM_B2 ** ADAM_STEP

    def body(w_ref, g_ref, m_ref, v_ref, d_ref, nm_ref, nv_ref):
        g = g_ref[...]
        nm = ADAM_B1 * m_ref[...] + (1.0 - ADAM_B1) * g
        nv = ADAM_B2 * v_ref[...] + (1.0 - ADAM_B2) * (g * g)
        d_ref[...] = -ADAM_LR * ((nm / c1) / (jnp.sqrt(nv / c2) + ADAM_EPS) + ADAM_WD * w_ref[...])
        nm_ref[...] = nm
        nv_ref[...] = nv

    spec = pl.BlockSpec((tr, N), _row(0))
    shp = jax.ShapeDtypeStruct((M, N), F32)
    return pl.pallas_call(
        body, name=name, grid=(M // tr,), in_specs=[spec] * 4, out_specs=[spec] * 3, out_shape=[shp] * 3,
        compiler_params=_params(("parallel",)),
    )(w, g, m, v)


def _coords():
    return lax.axis_index("x"), lax.axis_index("y"), lax.axis_index("c")


def all_gather8(block, name):
    m_per, n = block.shape

    def body(x_ref, out_ref, send_sems, recv_sems, local_sem):
        x, y, c = _coords()
        me, sibling = (x, y, c), (x, y, 1 - c)
        chips = [(1 - x, y), (x, 1 - y), (1 - x, 1 - y)]

        def rows(px, py, pc):
            return out_ref.at[pl.ds((4 * px + 2 * py + pc) * m_per, m_per), :]

        def copy(k, blk, to, src=None):
            return pltpu.make_async_remote_copy(
                src_ref=rows(*blk) if src is None else src, dst_ref=rows(*blk),
                send_sem=send_sems.at[k], recv_sem=recv_sems.at[k], device_id=to, device_id_type=MESH)

        mine = pltpu.make_async_copy(x_ref, rows(*me), local_sem)
        mine.start()
        first = [copy(0, me, sibling, src=x_ref)]
        first += [copy(1 + j, me, (*chip, c), src=x_ref) for j, chip in enumerate(chips)]
        for cp in first:
            cp.start()
        passed = [copy(4 + j, (*chip, c), sibling) for j, chip in enumerate(chips)]
        for j, chip in enumerate(chips):
            copy(1 + j, (*chip, c), me).wait_recv()
            passed[j].start()
        copy(0, sibling, me).wait_recv()
        for j, chip in enumerate(chips):
            copy(4 + j, (*chip, 1 - c), me).wait_recv()
        for cp in first + passed:
            cp.wait_send()
        mine.wait()

    return pl.pallas_call(
        body, name=name, in_specs=[VMEM_SPEC], out_specs=VMEM_SPEC,
        out_shape=jax.ShapeDtypeStruct((N_DEV * m_per, n), block.dtype),
        scratch_shapes=[pltpu.SemaphoreType.DMA((7,)), pltpu.SemaphoreType.DMA((7,)), pltpu.SemaphoreType.DMA],
        compiler_params=_params(),
    )(block)


def _other_chips(x, y):
    return [(1 - x, y), (x, 1 - y), (1 - x, 1 - y)]


HBM_SPEC = pl.BlockSpec(memory_space=pltpu.HBM)
SEM_SPEC = pl.BlockSpec(memory_space=pltpu.SEMAPHORE)
DATAFLOW = pltpu.SideEffectType.DATAFLOW_SIDE_EFFECTING


def _chip_peers(x, y, c):
    return [(px, py, c) for px, py in _other_chips(x, y)]


def _sibling_peer(x, y, c):
    return [(x, y, 1 - c)]


def _weight_desc(_, k, land_ref, peer, me):
    h = land_ref.shape[1] // 2
    rows = pl.ds(me[2] * h, h)
    mine = land_ref.at[2 * me[0] + me[1], rows, :]
    return mine, mine, land_ref.at[2 * peer[0] + peer[1], rows, :]


def _grad_desc(psum_ref, k, land_ref, peer, me):
    return psum_ref.at[2 * peer[0] + peer[1]], land_ref.at[2 * me[0] + me[1]], land_ref.at[2 * peer[0] + peer[1]]


def _pair_desc(grad_ref, k, land_ref, peer, me):
    h = land_ref.shape[1]
    return grad_ref.at[:, pl.ds(peer[2] * h, h), :], land_ref, land_ref


def _whole_desc(src_ref, k, land_ref, peer, me):
    return src_ref, land_ref, land_ref


def exchange_start(srcs, lands, units, groups, desc, peers, after, name):
    n_s, n_l, n_g = len(srcs), len(lands), len(groups)
    n_p = len(peers(0, 0, 0))

    def body(*refs):
        s_refs, l_refs = refs[:n_s], refs[n_s:n_s + n_l]
        outs = refs[n_s + n_l + 1:]
        sems, token = outs[:2 * n_g], outs[-1]
        me = _coords()
        for g, ids in enumerate(groups):
            for i, u in enumerate(ids):
                si, k = units[u]
                for j, peer in enumerate(peers(*me)):
                    src, dst, _ = desc(s_refs[si] if s_refs else None, k, l_refs[u], peer, me)
                    pltpu.make_async_remote_copy(
                        src_ref=src, dst_ref=dst, send_sem=sems[2 * g].at[n_p * i + j],
                        recv_sem=sems[2 * g + 1].at[n_p * i + j], device_id=peer, device_id_type=MESH).start()
        token[...] = jnp.zeros_like(token)

    arrs = list(srcs) + list(lands)
    sem_shapes = [pltpu.SemaphoreType.DMA((n_p * len(ids),)) for ids in groups for _ in range(2)]
    outs = pl.pallas_call(
        body, name=name,
        in_specs=[HBM_SPEC] * len(arrs) + [ANY],
        out_specs=[SEM_SPEC] * (2 * n_g) + [HBM_SPEC] * len(arrs) + [VMEM_SPEC],
        out_shape=sem_shapes + [pltpu.HBM(a.shape, a.dtype) for a in arrs] + [jax.ShapeDtypeStruct((8, LANES), F32)],
        input_output_aliases={i: 2 * n_g + i for i in range(len(arrs))},
        compiler_params=pltpu.CompilerParams(has_side_effects=DATAFLOW),
    )(*[pltpu.with_memory_space_constraint(a, pltpu.HBM) for a in arrs], after)
    sems = outs[:2 * n_g]
    thru = outs[2 * n_g:2 * n_g + len(arrs)]
    return sems, list(thru[:n_s]), list(thru[n_s:]), outs[-1]


def exchange_wait(srcs, lands, units, send_sem, recv_sem, desc, peers, after, name):
    n_s, n_l = len(srcs), len(lands)
    n_p = len(peers(0, 0, 0))

    def body(*refs):
        s_refs, l_refs = refs[:n_s], refs[n_s:n_s + n_l]
        send_sems, recv_sems = refs[n_s + n_l], refs[n_s + n_l + 1]
        me = _coords()
        for i, (si, k) in enumerate(units):
            for j, peer in enumerate(peers(*me)):
                src, _, mine = desc(s_refs[si] if s_refs else None, k, l_refs[i], peer, me)
                cp = pltpu.make_async_remote_copy(
                    src_ref=src, dst_ref=mine, send_sem=send_sems.at[n_p * i + j], recv_sem=recv_sems.at[n_p * i + j],
                    device_id=peer, device_id_type=MESH)
                cp.wait_send()
                cp.wait_recv()

    arrs = list(srcs) + list(lands)
    outs = pl.pallas_call(
        body, name=name,
        in_specs=[HBM_SPEC] * len(arrs) + [SEM_SPEC, SEM_SPEC, ANY],
        out_specs=[HBM_SPEC] * len(arrs),
        out_shape=[pltpu.HBM(a.shape, a.dtype) for a in arrs],
        input_output_aliases={i: i for i in range(len(arrs))},
        compiler_params=pltpu.CompilerParams(has_side_effects=DATAFLOW),
    )(*arrs, send_sem, recv_sem, after)
    return list(outs[:n_s]), list(outs[n_s:])


def sibling_fill(lands, name):
    n_u = len(lands)

    def body(*refs):
        ins, outs = refs[:n_u], refs[n_u:2 * n_u]
        send_sems, recv_sems = refs[2 * n_u:]
        x, y, c = _coords()
        barrier = pltpu.get_barrier_semaphore()
        pl.semaphore_signal(barrier, inc=1, device_id=(x, y, 1 - c), device_id_type=MESH)
        pl.semaphore_wait(barrier, 1)
        sends = []
        for u in range(n_u):
            h = ins[u].shape[1] // 2
            for j, (px, py) in enumerate(_other_chips(x, y)):
                part = (2 * px + py, pl.ds(c * h, h), slice(None))
                cp = pltpu.make_async_remote_copy(
                    src_ref=ins[u].at[part], dst_ref=outs[u].at[part], send_sem=send_sems.at[3 * u + j],
                    recv_sem=recv_sems.at[3 * u + j], device_id=(x, y, 1 - c), device_id_type=MESH)
                cp.start()
                sends.append(cp)
        for u in range(n_u):
            h = ins[u].shape[1] // 2
            for j, (px, py) in enumerate(_other_chips(x, y)):
                theirs = (2 * px + py, pl.ds((1 - c) * h, h), slice(None))
                pltpu.make_async_remote_copy(
                    src_ref=ins[u].at[theirs], dst_ref=outs[u].at[theirs], send_sem=send_sems.at[3 * u + j],
                    recv_sem=recv_sems.at[3 * u + j], device_id=(x, y, 1 - c), device_id_type=MESH).wait_recv()
        for cp in sends:
            cp.wait_send()

    return pl.pallas_call(
        body, name=name, in_specs=[ANY] * n_u, out_specs=[ANY] * n_u,
        out_shape=[jax.ShapeDtypeStruct(a.shape, a.dtype) for a in lands],
        input_output_aliases={i: i for i in range(n_u)},
        scratch_shapes=[pltpu.SemaphoreType.DMA((3 * n_u,)), pltpu.SemaphoreType.DMA((3 * n_u,))],
        compiler_params=pltpu.CompilerParams(vmem_limit_bytes=VMEM_LIMIT_MB * 1024 * 1024,
                                             collective_id=FILL_COLLECTIVE_ID),
    )(*lands)


def _pack_rows(parts):
    flat = jnp.concatenate([p.reshape(-1).astype(F32) for p in parts])
    n = flat.shape[0]
    padded = -(-n // (8 * LANES)) * (8 * LANES)
    return jnp.pad(flat, (0, padded - n)).reshape(-1, LANES)


def _unpack_rows(packed, shapes):
    flat = packed.reshape(-1)
    out, off = [], 0
    for s in shapes:
        size = 1
        for d in s:
            size *= d
        out.append(flat[off:off + size].reshape(s))
        off += size
    return out


def _shard_last(full, s_me):
    n = full.shape[-1] // N_CHIP
    return lax.dynamic_slice_in_dim(full, s_me * n, n, axis=full.ndim - 1)


def _unshard_last(g):
    moved = jnp.moveaxis(g, 0, -2)
    return moved.reshape(moved.shape[:-2] + (moved.shape[-2] * moved.shape[-1],))


def kernel(x, c, w_ada, b_ada, ln_g, ln_b, ffn_gu, ffn_down, gmlp_w_in, gmlp_b_in, gmlp_ln_g, gmlp_ln_b, gmlp_w_s, gmlp_b_s, gmlp_w_out, w_ada_kv, b_ada_kv, w_kv, attn_w_q, attn_rel_bias, attn_w_o, loss_target, m_w_ada, m_b_ada, m_ln_g, m_ln_b, m_ffn_gu, m_ffn_down, m_gmlp_w_in, m_gmlp_b_in, m_gmlp_ln_g, m_gmlp_ln_b, m_gmlp_w_s, m_gmlp_b_s, m_gmlp_w_out, m_w_ada_kv, m_b_ada_kv, m_w_kv, m_attn_w_q, m_attn_rel_bias, m_attn_w_o, v_w_ada, v_b_ada, v_ln_g, v_ln_b, v_ffn_gu, v_ffn_down, v_gmlp_w_in, v_gmlp_b_in, v_gmlp_ln_g, v_gmlp_ln_b, v_gmlp_w_s, v_gmlp_b_s, v_gmlp_w_out, v_w_ada_kv, v_b_ada_kv, v_w_kv, v_attn_w_q, v_attn_rel_bias, v_attn_w_o):
    xi, yi, ci = _coords()
    s_me = 2 * xi + yi
    dev = 4 * xi + 2 * yi + ci

    x0 = x[0]
    tgt = loss_target[0]
    S, D = x0.shape
    L = w_ada.shape[0]
    NA = gmlp_w_in.shape[0]
    NB = attn_w_q.shape[0]
    NH = D // HEAD_DIM
    alpha = (2.0 * L) ** 0.25
    n_ada = w_ada.shape[2]
    n_kv = w_ada_kv.shape[1]

    stack_names = ["ffn_gu", "ffn_down", "gmlp_w_in", "gmlp_w_out", "w_kv", "attn_w_q", "attn_w_o"]
    stack_src = dict(ffn_gu=ffn_gu, ffn_down=ffn_down, gmlp_w_in=gmlp_w_in, gmlp_w_out=gmlp_w_out, w_kv=w_kv[None],
                     attn_w_q=attn_w_q, attn_w_o=attn_w_o)
    stacks = [stack_src[nm].reshape((-1,) + stack_src[nm].shape[-2:]) for nm in stack_names]
    units = [(si, k) for si, st in enumerate(stacks) for k in range(st.shape[0])]
    unit_of = {(stack_names[si], k): u for u, (si, k) in enumerate(units)}
    weight_groups = [[("ffn_gu", 0)], [("ffn_down", 0)]]
    for l in range(L):
        mixer = [("gmlp_w_in", l), ("gmlp_w_out", l)] if l < NA else [("attn_w_q", l - NA), ("attn_w_o", l - NA)]
        first, last = [("ffn_gu", 2 * l), ("ffn_down", 2 * l)], [("ffn_gu", 2 * l + 1), ("ffn_down", 2 * l + 1)]
        if l == 0:
            weight_groups += [mixer, last]
        else:
            weight_groups += [([("w_kv", 0)] if l == NA else []) + first, mixer, last]
    weight_groups = [[unit_of[n] for n in names] for names in weight_groups]
    group_of = {u: g for g, ids in enumerate(weight_groups) for u in ids}

    c_all = all_gather8(jnp.broadcast_to(c, (8, D)), "ag_c").reshape(N_DEV, 8, D)[:, 0]
    b_ada_sh = lax.dynamic_slice_in_dim(b_ada, s_me * n_ada, n_ada, axis=1)
    b_kv_sh = lax.dynamic_slice_in_dim(b_ada_kv, s_me * n_kv, n_kv, axis=0)
    mod_part = ada_fwd(c_all, w_ada, b_ada_sh[:, None, :], "ada_fwd")
    mkv_part = ada_fwd(c_all, w_ada_kv[None], b_kv_sh[None, None, :], "ada_kv_fwd")
    part = jnp.concatenate([jnp.transpose(mod_part, (1, 0, 2)).reshape(N_DEV, L * n_ada), mkv_part[0]], axis=1)
    width = part.shape[1]
    pad_w = -(-width // LANES) * LANES - width
    all_part = all_gather8(jnp.pad(part, ((0, 0), (0, pad_w))), "ag_mod").reshape(N_DEV, N_DEV, width + pad_w)
    mine = lax.dynamic_index_in_dim(all_part[0::2], dev, axis=1, keepdims=False)
    mod = jnp.transpose(mine[:, :L * n_ada].reshape(N_CHIP, L, n_ada), (1, 0, 2)).reshape(L, N_MOD, D)
    mkv = mine[:, L * n_ada:width].reshape(2, D)

    def mrow(l, k):
        return mod[l, k][None, :]

    small_shapes = [ln_g.shape, ln_b.shape, gmlp_b_in.shape, gmlp_ln_g.shape, gmlp_ln_b.shape, attn_rel_bias.shape]
    small_pack = _pack_rows([ln_g, ln_b, gmlp_b_in, gmlp_ln_g, gmlp_ln_b, attn_rel_bias])
    small_all = all_gather8(small_pack, "ag_small_params").reshape((N_DEV,) + small_pack.shape)[0::2]
    sm = [_unpack_rows(small_all[s], small_shapes) for s in range(N_CHIP)]
    ln_g_f, ln_b_f, b_in_f, gln_g_f, gln_b_f, rel_f = [
        _unshard_last(jnp.stack([sm[s][i] for s in range(N_CHIP)])) for i in range(len(small_shapes))]

    def landing(u):
        si, k = units[u]
        shard = stacks[si][k]
        if group_of[u] < 2:
            shard = lax.optimization_barrier(shard)
        shard = shard.astype(BF16)
        return lax.dynamic_update_slice(lax.empty((N_CHIP,) + shard.shape, BF16), shard[None], (s_me, 0, 0))

    gathers_done = jnp.concatenate([mod.reshape(-1)[:LANES], small_all.reshape(-1)[:LANES]])
    w_sems, lands_t = {}, {}
    for part, groups, name in ((0, weight_groups[:2], "weight_send_start_first"),
                               (1, weight_groups[2:], "weight_send_start_rest")):
        ids = [u for grp in groups for u in grp]
        local = [[ids.index(u) for u in grp] for grp in groups]
        sems, _, lands, gathers_done = exchange_start([], [landing(u) for u in ids], [units[u] for u in ids], local,
                                                      _weight_desc, _chip_peers, gathers_done, name)
        for i, grp in enumerate(groups):
            w_sems[2 * part + i] = (sems[2 * i], sems[2 * i + 1])
        lands_t.update(zip(ids, lands))
    wg = {}
    latest = [gathers_done]

    def W(nm, k):
        u = unit_of[(nm, k)]
        if u not in wg:
            g = group_of[u]
            ids = weight_groups[g]
            _, got = exchange_wait([], [lands_t[v] for v in ids], [units[v] for v in ids], w_sems[g][0],
                                   w_sems[g][1], _weight_desc, _chip_peers, latest[0], "weight_send_wait_%d" % g)
            wg.update(zip(ids, sibling_fill(got, "weight_sibling_fill")))
        return wg[u]

    def Wrows(nm, k):
        w4 = W(nm, k)
        return w4.reshape(w4.shape[0] * w4.shape[1], w4.shape[2])

    bst = [jnp.transpose(gmlp_b_s[j]) for j in range(NA)]
    biases = {}

    def make_bias(j, dep):
        rel, _ = lax.optimization_barrier((rel_f[j], dep))
        biases[j] = jnp.transpose(bias_expand(rel, "bias_expand"), (1, 0, 2))
        return biases[j]

    saved = []
    xc = x0
    kpad = vpad = xkv = None
    for l in range(L):
        if l == 1 and NB > 1:
            latest[0] = make_bias(1, xc)
        if l == NA:
            xkv = xc
            kv, hkv = mod_matmul(xc, mkv[1][None], mkv[0][None], W("w_kv", 0), None, BF16, "kv_proj")
            kpad = jnp.pad(kv[:, :D], ((LEFT_PAD, 0), (0, 0)))
            vpad = jnp.pad(kv[:, D:], ((LEFT_PAD, 0), (0, 0)))
        sv = {}
        for i in (0, 2):
            k = 2 * l + i // 2
            gu, hv = mod_matmul(xc, mrow(l, 3 * i + 1), mrow(l, 3 * i), W("ffn_gu", k), None, BF16, "ffn_up")
            latest[0] = hv
            gw = 0.5 * (1.0 + mrow(l, 3 * i + 2))
            xn, xh, rs, yv, av = matmul_res_ln(gu, Wrows("ffn_down", k), xc, gw, ln_g_f[l, i][None],
                                               ln_b_f[l, i][None], alpha, True, "ffn_down")
            sv[i] = dict(x=xc, h=hv, gu=gu, a=av, xh=xh, rs=rs, y=yv, gw=gw)
            xc = latest[0] = xn
            if i == 0:
                if l == 0 and NB > 0:
                    latest[0] = make_bias(0, xc)
                gw = 1.0 + mrow(l, 5)
                if l < NA:
                    pre, hv = mod_matmul(xc, mrow(l, 4), mrow(l, 3), W("gmlp_w_in", l), b_in_f[l][None], F32,
                                         "gmlp_in")
                    qv = sgu_fwd(pre, gln_g_f[l][None], gln_b_f[l][None], gmlp_w_s[l], bst[l], "sgu_fwd")
                    xn, xh, rs, yv = matmul_res_ln(qv, Wrows("gmlp_w_out", l), xc, gw, ln_g_f[l, 1][None],
                                                   ln_b_f[l, 1][None], alpha, False, "gmlp_out")
                    sv[1] = dict(x=xc, h=hv, pre=pre, a=qv, xh=xh, rs=rs, y=yv, gw=gw)
                else:
                    j = l - NA
                    if j not in biases:
                        make_bias(j, xc)
                    qh, hv = mod_matmul(xc, mrow(l, 4), mrow(l, 3), Wrows("attn_w_q", j)[None], None, BF16, "attn_q")
                    ov = attn_fwd(qh, kpad, vpad, biases[j], "attn_fwd")
                    xn, xh, rs, yv = matmul_res_ln(ov, Wrows("attn_w_o", j), xc, gw, ln_g_f[l, 1][None],
                                                   ln_b_f[l, 1][None], alpha, False, "attn_out")
                    sv[1] = dict(x=xc, h=hv, q=qh, a=ov, xh=xh, rs=rs, y=yv, gw=gw)
                xc = latest[0] = xn
        saved.append(sv)

    dx, lacc = loss_grad(xc, tgt, "loss_grad")
    loss = lax.psum((0.5 / D) * jnp.sum(lacc[0]), ("x", "y", "c"))

    gpair = [None] * len(units)
    col_split = {u for u, (si, _) in enumerate(units) if stack_names[si] != "ffn_gu"}
    dmod = [[None] * N_MOD for _ in range(L)]
    d_ln_g = [[None] * 3 for _ in range(L)]
    d_ln_b = [[None] * 3 for _ in range(L)]
    d_b_in, d_gln_g, d_gln_b, d_ws, d_bs, d_rel = ([None] * NA, [None] * NA, [None] * NA, [None] * NA, [None] * NA,
                                                  [None] * NB)
    dk = jnp.zeros((S, D), F32)
    dv = jnp.zeros((S, D), F32)
    dmkv = None

    made = []

    core_idx = ci.astype(jnp.int32).reshape(1)
    chip_idx = s_me.astype(jnp.int32).reshape(1)

    def put(nm, k, a, b, name):
        u = unit_of[(nm, k)]
        rows, cols = stacks[units[u][0]].shape[1:]
        if nm == "ffn_gu":
            g = wgrad_pair(b, a, N_CHIP, cols, rows // 2, lambda j, p: j, lambda j, p: p, core_idx, name)
        elif nm in ("gmlp_w_in", "w_kv"):
            g = wgrad_pair(a, b, N_CHIP, rows, cols // 2, lambda j, p: 0, lambda j, p: 2 * j + p, core_idx, name)
        else:
            g = wgrad_pair(a, b, 1, N_CHIP * rows, cols // 2, lambda j, p: 0, lambda j, p: p, core_idx, name)
        gpair[u] = g.reshape((N_CHIP, -1, g.shape[-1]))
        made.append(u)

    own_half, sib_half = {}, {}
    n_started = [0]

    def start_grad_exchange(ids, after):
        psums = [gpair[u] for u in ids]
        n = len(ids)
        tag = n_started[0]
        n_started[0] += 1
        sems, ps_t, q_t, token = exchange_start(psums, [lax.empty(p.shape, p.dtype) for p in psums],
                                                [(i, 0) for i in range(n)], [list(range(n))], _grad_desc, _chip_peers,
                                                after, "grad_send_start_%d" % tag)
        return dict(ids=ids, tag=tag, sems=sems, ps=ps_t, q=q_t), token

    def finish_grad_exchange(pend, after):
        n = len(pend["ids"])
        ps_t, q = exchange_wait(pend["ps"], pend["q"], [(i, 0) for i in range(n)], pend["sems"][0], pend["sems"][1],
                                _grad_desc, _chip_peers, after, "grad_send_wait_%d" % pend["tag"])
        halves = [chip_sum(ps_t[i], q[i], chip_idx, u not in col_split, "grad_chip_sum")
                  for i, u in enumerate(pend["ids"])]
        sems, h_t, land_t, token = exchange_start(halves, [lax.empty(h.shape, h.dtype) for h in halves],
                                              [(i, 0) for i in range(n)], [list(range(n))], _whole_desc,
                                              _sibling_peer, halves[0], "half_send_start_%d" % pend["tag"])
        swaps.append(dict(ids=pend["ids"], tag=pend["tag"], sems=sems, h=h_t, land=land_t))
        return token

    def collect_halves(after):
        for sw in swaps:
            n = len(sw["ids"])
            h, land = exchange_wait(sw["h"], sw["land"], [(i, 0) for i in range(n)], sw["sems"][0], sw["sems"][1],
                                    _whole_desc, _sibling_peer, after, "half_send_wait_%d" % sw["tag"])
            for u, mine, theirs in zip(sw["ids"], h, land):
                own_half[u], sib_half[u] = mine, theirs
        swaps.clear()

    swaps = []
    pending = None
    started_before = jnp.zeros((8, LANES), F32)

    def ln_inputs(l, i):
        t = saved[l][i]
        return (t["xh"], t["rs"], ln_g_f[l, i][None], t["y"], t["gw"], 1.0 if i == 1 else 0.5)

    def record_ln(l, i, acc, row0):
        d_ln_g[l][i], d_ln_b[l][i], dmod[l][3 * i + 2] = acc[row0], acc[row0 + 1], acc[row0 + 2]

    ln_done = None
    for l in reversed(range(L)):
        if l == NA - 1:
            dkv = jnp.concatenate([dk, dv], axis=1)
            put("w_kv", 0, hkv, dkv, "kv_wgrad")
            pdxa, pdy, acc = dgrad_mod(dkv, W("w_kv", 0), dx, xkv, mkv[1][None], ln_inputs(l, 2), alpha, "kv_dgrad")
            dmkv = jnp.stack([acc[1], acc[0]])
            record_ln(l, 2, acc, 2)
            ln_done = (pdxa, pdy)
        sv = saved[l]
        for i in (2, 1, 0):
            t = sv[i]
            if ln_done is None:
                dxa, dy, acc1 = ln_res_bwd(dx, *ln_inputs(l, i), alpha, "ln_res_bwd")
                record_ln(l, i, acc1, 0)
            else:
                dxa, dy = ln_done
                ln_done = None
            before = (l, i - 1) if i > 0 else ((l - 1, 2) if l > 0 and l != NA else None)
            prev = ln_inputs(*before) if before is not None else None
            scl = mrow(l, 3 * i + 1)
            if i != 1:
                k = 2 * l + i // 2
                F = t["gu"].shape[1] // 2
                dgu = ffn_act_bwd(dy, Wrows("ffn_down", k), t["gu"], started_before, "ffn_act_bwd")
                put("ffn_down", k, t["a"], dy, "ffn_down_wgrad")
                put("ffn_gu", k, t["h"], dgu, "ffn_up_wgrad")
                res = dgrad_mod(dgu, W("ffn_gu", k), dxa, t["x"], scl, prev, alpha, "ffn_up_dgrad")
            elif l < NA:
                dq = matmul_nt(dy, Wrows("gmlp_w_out", l), started_before, "gmlp_out_dgrad")
                put("gmlp_w_out", l, t["a"], dy, "gmlp_out_wgrad")
                dpre, dws_l, dss, dgl, dbin = sgu_bwd(dq, t["pre"], gln_g_f[l][None], gln_b_f[l][None], gmlp_w_s[l],
                                                      bst[l], "sgu_bwd")
                d_ws[l] = dws_l
                d_bs[l] = jnp.transpose(group_lane_sum(dss, "sgu_bias_grad")[:, :GMLP_GROUPS])
                d_gln_g[l], d_gln_b[l], d_b_in[l] = dgl[0], dgl[1], dbin[0]
                put("gmlp_w_in", l, t["h"], dpre, "gmlp_in_wgrad")
                res = dgrad_mod(dpre, W("gmlp_w_in", l), dxa, t["x"], scl, prev, alpha, "gmlp_in_dgrad")
            else:
                j = l - NA
                do = matmul_nt(dy, Wrows("attn_w_o", j), started_before, "attn_out_dgrad")
                put("attn_w_o", j, t["a"], dy, "attn_out_wgrad")
                dqh, dk, dv, dbias = attn_bwd(t["q"], do, kpad, vpad, biases[j], dk, dv, "attn_bwd")
                d_rel[j] = bias_grad(jnp.transpose(dbias, (1, 0, 2)), "bias_grad")
                put("attn_w_q", j, t["h"], dqh, "attn_q_wgrad")
                res = dgrad_mod(dqh, Wrows("attn_w_q", j)[None], dxa, t["x"], scl, prev, alpha, "attn_q_dgrad")
            acc2 = res[-1]
            dmod[l][3 * i + 1], dmod[l][3 * i] = acc2[0], acc2[1]
            if before is None:
                dx = res[0]
            else:
                record_ln(*before, acc2, 2)
                ln_done = (res[0], res[1])
            if (i == 0 and l > 0) or (i == 1 and l == 0):
                started, started_before = start_grad_exchange(list(made), acc2)
                made.clear()
                if pending is not None:
                    started_before = started_before + finish_grad_exchange(pending, acc2)
                pending = started
    grad_x = dx[None]

    dvec = _pack_rows([jnp.stack([jnp.stack(r) for r in dmod]), dmkv])
    dvec = lax.optimization_barrier((dvec, [gpair[u] for u in made]))[0]
    n_dvec = L * N_MOD * D + 2 * D
    dall = all_gather8(dvec, "ag_dmod").reshape(N_DEV, -1, LANES)
    db_all = sum_leading(dall, "ada_bias_grad").reshape(-1)[:n_dvec]
    g_b_ada = db_all[:L * N_MOD * D].reshape(L, N_MOD * D)
    g_b_ada_kv = db_all[L * N_MOD * D:]
    dall2 = dall.reshape(N_DEV, -1)[:, :n_dvec]
    dmod_all = dall2[:, :L * N_MOD * D].reshape(N_DEV, L, N_MOD * D)
    dmod_sh = jnp.transpose(lax.dynamic_slice_in_dim(dmod_all, s_me * n_ada, n_ada, axis=2), (1, 0, 2))
    dmkv_sh = lax.dynamic_slice_in_dim(dall2[:, L * N_MOD * D:], s_me * n_kv, n_kv, axis=1)[None]
    c_all_t = jnp.transpose(c_all)
    g_w_ada = ada_wgrad(c_all_t, dmod_sh, "ada_wgrad")
    g_w_ada_kv = ada_wgrad(c_all_t, dmkv_sh, "ada_kv_wgrad")[0]

    small_g = [jnp.stack([jnp.stack(r) for r in d_ln_g]), jnp.stack([jnp.stack(r) for r in d_ln_b]),
               jnp.stack(d_b_in), jnp.stack(d_gln_g), jnp.stack(d_gln_b), jnp.stack(d_rel), jnp.stack(d_bs)]
    sg_shapes = [a.shape for a in small_g]
    sg_pack = _pack_rows(small_g)
    sg_all = all_gather8(sg_pack, "ag_small_grads").reshape(N_DEV, -1, LANES)
    sg_sum = _unpack_rows(sum_leading(sg_all, "small_grad_sum"), sg_shapes)
    g_ln_g, g_ln_b, g_b_in, g_gln_g, g_gln_b, g_rel = [_shard_last(a, s_me) for a in sg_sum[:6]]
    g_bs = sg_sum[6]
    ws_pack = jnp.stack(d_ws).astype(BF16).reshape(-1, LANES)
    ws_all = all_gather8(ws_pack, "ag_spatial_grads").reshape(N_DEV, -1, LANES)
    g_ws = sum_leading(ws_all, "spatial_grad_sum").reshape(gmlp_w_s.shape)

    last, _ = start_grad_exchange(list(made), sg_all)

    grads = dict(w_ada=g_w_ada, b_ada=g_b_ada, ln_g=g_ln_g, ln_b=g_ln_b, gmlp_b_in=g_b_in, gmlp_ln_g=g_gln_g,
                 gmlp_ln_b=g_gln_b, gmlp_w_s=g_ws, gmlp_b_s=g_bs, w_ada_kv=g_w_ada_kv, b_ada_kv=g_b_ada_kv,
                 attn_rel_bias=g_rel)
    weights = dict(w_ada=w_ada, b_ada=b_ada, ln_g=ln_g, ln_b=ln_b, ffn_gu=ffn_gu, ffn_down=ffn_down,
                   gmlp_w_in=gmlp_w_in, gmlp_b_in=gmlp_b_in, gmlp_ln_g=gmlp_ln_g, gmlp_ln_b=gmlp_ln_b,
                   gmlp_w_s=gmlp_w_s, gmlp_b_s=gmlp_b_s, gmlp_w_out=gmlp_w_out, w_ada_kv=w_ada_kv,
                   b_ada_kv=b_ada_kv, w_kv=w_kv, attn_w_q=attn_w_q, attn_rel_bias=attn_rel_bias, attn_w_o=attn_w_o)
    ms = dict(w_ada=m_w_ada, b_ada=m_b_ada, ln_g=m_ln_g, ln_b=m_ln_b, ffn_gu=m_ffn_gu, ffn_down=m_ffn_down,
              gmlp_w_in=m_gmlp_w_in, gmlp_b_in=m_gmlp_b_in, gmlp_ln_g=m_gmlp_ln_g, gmlp_ln_b=m_gmlp_ln_b,
              gmlp_w_s=m_gmlp_w_s, gmlp_b_s=m_gmlp_b_s, gmlp_w_out=m_gmlp_w_out, w_ada_kv=m_w_ada_kv,
              b_ada_kv=m_b_ada_kv, w_kv=m_w_kv, attn_w_q=m_attn_w_q, attn_rel_bias=m_attn_rel_bias,
              attn_w_o=m_attn_w_o)
    vs = dict(w_ada=v_w_ada, b_ada=v_b_ada, ln_g=v_ln_g, ln_b=v_ln_b, ffn_gu=v_ffn_gu, ffn_down=v_ffn_down,
              gmlp_w_in=v_gmlp_w_in, gmlp_b_in=v_gmlp_b_in, gmlp_ln_g=v_gmlp_ln_g, gmlp_ln_b=v_gmlp_ln_b,
              gmlp_w_s=v_gmlp_w_s, gmlp_b_s=v_gmlp_b_s, gmlp_w_out=v_gmlp_w_out, w_ada_kv=v_w_ada_kv,
              b_ada_kv=v_b_ada_kv, w_kv=v_w_kv, attn_w_q=v_attn_w_q, attn_rel_bias=v_attn_rel_bias,
              attn_w_o=v_attn_w_o)
    order = ["w_ada", "b_ada", "ln_g", "ln_b", "ffn_gu", "ffn_down", "gmlp_w_in", "gmlp_b_in", "gmlp_ln_g",
             "gmlp_ln_b", "gmlp_w_s", "gmlp_b_s", "gmlp_w_out", "w_ada_kv", "b_ada_kv", "w_kv", "attn_w_q",
             "attn_rel_bias", "attn_w_o"]
    big_names = ["w_ada", "w_ada_kv"] + stack_names
    small_names = [nm for nm in order if nm not in big_names]
    delta, new_m, new_v = {}, {}, {}

    def adamw_big(nm):
        shp = weights[nm].shape
        two_d = (-1, shp[-1])
        d, a, b = adamw(weights[nm].reshape(two_d), grads[nm].reshape(two_d), ms[nm].reshape(two_d),
                        vs[nm].reshape(two_d), "adamw")
        delta[nm], new_m[nm], new_v[nm] = d.reshape(shp), a.reshape(shp), b.reshape(shp)

    adamw_big("w_ada")
    adamw_big("w_ada_kv")
    shapes = [weights[nm].shape for nm in small_names]
    d, a, b = adamw(_pack_rows([weights[nm] for nm in small_names]), _pack_rows([grads[nm] for nm in small_names]),
                    _pack_rows([ms[nm] for nm in small_names]), _pack_rows([vs[nm] for nm in small_names]),
                    "adamw_small")
    for nm, dd, aa, bb in zip(small_names, _unpack_rows(d, shapes), _unpack_rows(a, shapes), _unpack_rows(b, shapes)):
        delta[nm], new_m[nm], new_v[nm] = dd, aa, bb

    def full_grad(u):
        lo = jnp.where(ci == 0, own_half[u], sib_half[u])
        hi = jnp.where(ci == 0, sib_half[u], own_half[u])
        return jnp.concatenate([lo, hi], axis=1 if u in col_split else 0)

    def adamw_stack(nm):
        si = stack_names.index(nm)
        g = jnp.stack([full_grad(unit_of[(nm, k)]) for k in range(stacks[si].shape[0])])
        grads[nm] = g.reshape(weights[nm].shape)
        adamw_big(nm)

    late = [stack_names[units[u][0]] for u in last["ids"]]
    early = [nm for nm in stack_names if nm not in late]
    finish_grad_exchange(pending, delta["w_ada"])
    collect_halves(delta["w_ada"])
    for nm in early:
        adamw_stack(nm)
    finish_grad_exchange(last, delta[early[-1]])
    collect_halves(delta[early[-1]])
    for nm in stack_names:
        if nm in late:
            adamw_stack(nm)

    return (loss, grad_x, *[grads[nm] for nm in order], *[delta[nm] for nm in order],
            *[new_m[nm] for nm in order], *[new_v[nm] for nm in order])
```

```python
import functools

import jax
import jax.numpy as jnp
from jax import lax
from jax.experimental import pallas as pl
from jax.experimental.pallas import tpu as pltpu

F32 = jnp.float32
BF16 = jnp.bfloat16
MESH = pl.DeviceIdType.MESH
HIGHEST = lax.Precision.HIGHEST

CHUNK = 64
GMLP_WINDOW = 128
GMLP_GROUPS = 8
HEAD_DIM = 64
LEFT_CHUNKS = 8
BAND = (LEFT_CHUNKS + 1) * CHUNK
LEFT_PAD = LEFT_CHUNKS * CHUNK
MAX_REL = 4 * CHUNK
N_REL = (CHUNK - 1) + MAX_REL + 1
LN_EPS = 1e-5
N_MOD = 9
N_DEV = 8
N_CHIP = 4

ADAM_LR = 0.001
ADAM_B1 = 0.9
ADAM_B2 = 0.999
ADAM_EPS = 1e-08
ADAM_WD = 0.01
ADAM_STEP = 10

LANES = 128
ROW_TILE = 256
MATMUL_ROW_TILE = 512
WGRAD_ROWS = 2048
ATTN_CHUNKS_PER_STEP = 16
VMEM_LIMIT_MB = 56

NT = (((1,), (1,)), ((), ()))
TN = (((0,), (0,)), ((), ()))

ANY = pl.BlockSpec(memory_space=pl.ANY)
VMEM_SPEC = pl.BlockSpec(memory_space=pltpu.VMEM)


def _params(semantics=None):
    kw = dict(vmem_limit_bytes=VMEM_LIMIT_MB * 1024 * 1024)
    if semantics is not None:
        kw["dimension_semantics"] = semantics
    return pltpu.CompilerParams(**kw)


def _sigmoid(v):
    return 0.5 * (1.0 + jnp.tanh(0.5 * v))


def _gelu(v):
    return 0.5 * v * (1.0 + lax.erf(v * (2.0 ** -0.5)))


def _gelu_grad(v):
    return 0.5 * (1.0 + lax.erf(v * (2.0 ** -0.5))) + v * jnp.exp(-0.5 * v * v) * ((2.0 * jnp.pi) ** -0.5)


def _row(m):
    return lambda i: (i, 0)


def _fixed2(i):
    return (0, 0)


def _fixed3(i):
    return (0, 0, 0)


def _resident(shape):
    return pl.BlockSpec(shape, _fixed2 if len(shape) == 2 else _fixed3, pipeline_mode=pl.Buffered(1))


def mod_matmul(x, scl, shift, w, bias, out_dtype, name):
    S, D = x.shape
    NS, _, n = w.shape
    tm = min(MATMUL_ROW_TILE, S)
    has_bias = bias is not None

    def body(*refs):
        if has_bias:
            x_ref, scl_ref, sh_ref, w_ref, b_ref, o_ref, h_ref = refs
        else:
            x_ref, scl_ref, sh_ref, w_ref, o_ref, h_ref = refs
        h = (x_ref[...] * (1.0 + scl_ref[...]) + sh_ref[...]).astype(BF16)
        h_ref[...] = h
        for s in range(NS):
            acc = jnp.dot(h, w_ref[s], preferred_element_type=F32)
            if has_bias:
                acc = acc + b_ref[:, s * n:(s + 1) * n]
            o_ref[:, s * n:(s + 1) * n] = acc.astype(out_dtype)

    in_specs = [pl.BlockSpec((tm, D), _row(0)), pl.BlockSpec((1, D), _fixed2), pl.BlockSpec((1, D), _fixed2),
                _resident((NS, D, n))]
    args = [x, scl, shift, w]
    if has_bias:
        in_specs.append(pl.BlockSpec((1, NS * n), _fixed2))
        args.append(bias)
    return pl.pallas_call(
        body, name=name, grid=(S // tm,), in_specs=in_specs,
        out_specs=[pl.BlockSpec((tm, NS * n), _row(0)), pl.BlockSpec((tm, D), _row(0))],
        out_shape=[jax.ShapeDtypeStruct((S, NS * n), out_dtype), jax.ShapeDtypeStruct((S, D), BF16)],
        compiler_params=_params(("parallel",)),
    )(*args)


def matmul_res_ln(a, w, x, gw, lg, lb, alpha, swiglu, name):
    S, D = x.shape
    K = w.shape[0]
    tm = min(ROW_TILE, S)
    ka = a.shape[1]

    def body(a_ref, w_ref, x_ref, gw_ref, lg_ref, lb_ref, xn_ref, xh_ref, rs_ref, y_ref, *act_ref):
        if swiglu:
            g = a_ref[:, :K].astype(F32)
            u = a_ref[:, K:].astype(F32)
            act = (g * _sigmoid(g) * u).astype(BF16)
            act_ref[0][...] = act
        else:
            act = a_ref[...].astype(BF16)
        y = jnp.dot(act, w_ref[...], preferred_element_type=F32)
        z = alpha * x_ref[...] + gw_ref[...] * y
        mu = jnp.mean(z, axis=-1, keepdims=True)
        zc = z - mu
        var = jnp.mean(zc * zc, axis=-1, keepdims=True)
        rstd = lax.rsqrt(var + LN_EPS)
        xhat = zc * rstd
        xn_ref[...] = xhat * lg_ref[...] + lb_ref[...]
        xh_ref[...] = xhat
        rs_ref[...] = rstd
        y_ref[...] = y.astype(BF16)

    vec = pl.BlockSpec((1, D), _fixed2)
    out_specs = [pl.BlockSpec((tm, D), _row(0)), pl.BlockSpec((tm, D), _row(0)), pl.BlockSpec((tm, 1), _row(0)),
                 pl.BlockSpec((tm, D), _row(0))]
    out_shape = [jax.ShapeDtypeStruct((S, D), F32), jax.ShapeDtypeStruct((S, D), F32),
                 jax.ShapeDtypeStruct((S, 1), F32), jax.ShapeDtypeStruct((S, D), BF16)]
    if swiglu:
        out_specs.append(pl.BlockSpec((tm, K), _row(0)))
        out_shape.append(jax.ShapeDtypeStruct((S, K), BF16))
    return pl.pallas_call(
        body, name=name, grid=(S // tm,),
        in_specs=[pl.BlockSpec((tm, ka), _row(0)), _resident((K, D)), pl.BlockSpec((tm, D), _row(0)),
                  vec, vec, vec],
        out_specs=out_specs, out_shape=out_shape,
        compiler_params=_params(("parallel",)),
    )(a, w, x, gw, lg, lb)


def _ln_res_bwd_tile(d, xh_ref, rs_ref, lg_ref, y_ref, gw_ref, wres, alpha, dxa_ref, dy_ref, acc_ref, row0):
    xh = xh_ref[...]
    dxh = d * lg_ref[...]
    m1 = jnp.mean(dxh, axis=-1, keepdims=True)
    m2 = jnp.mean(dxh * xh, axis=-1, keepdims=True)
    dz = rs_ref[...] * (dxh - m1 - xh * m2)
    dxa_ref[...] = alpha * dz
    dy_ref[...] = (gw_ref[...] * dz).astype(BF16)
    acc_ref[row0:row0 + 1, :] += jnp.sum(d * xh, axis=0, keepdims=True)
    acc_ref[row0 + 1:row0 + 2, :] += jnp.sum(d, axis=0, keepdims=True)
    acc_ref[row0 + 2:row0 + 3, :] += jnp.sum((wres * dz) * y_ref[...].astype(F32), axis=0, keepdims=True)


def ln_res_bwd(dxn, xhat, rstd, lg, y, gw, wres, alpha, name):
    S, D = dxn.shape
    tm = min(MATMUL_ROW_TILE, S)

    def body(dxn_ref, xh_ref, rs_ref, lg_ref, y_ref, gw_ref, dxa_ref, dy_ref, acc_ref):
        @pl.when(pl.program_id(0) == 0)
        def _():
            acc_ref[...] = jnp.zeros_like(acc_ref)

        _ln_res_bwd_tile(dxn_ref[...], xh_ref, rs_ref, lg_ref, y_ref, gw_ref, wres, alpha, dxa_ref, dy_ref, acc_ref, 0)

    vec = pl.BlockSpec((1, D), _fixed2)
    tile = pl.BlockSpec((tm, D), _row(0))
    return pl.pallas_call(
        body, name=name, grid=(S // tm,),
        in_specs=[tile, tile, pl.BlockSpec((tm, 1), _row(0)), vec, tile, vec],
        out_specs=[tile, tile, pl.BlockSpec((8, D), _fixed2)],
        out_shape=[jax.ShapeDtypeStruct((S, D), F32), jax.ShapeDtypeStruct((S, D), BF16),
                   jax.ShapeDtypeStruct((8, D), F32)],
        compiler_params=_params(("arbitrary",)),
    )(dxn, xhat, rstd, lg, y, gw)


def ffn_act_bwd(dy, wd, gu, after, name):
    S, D = dy.shape
    K = wd.shape[0]
    tm = min(ROW_TILE, S)

    def body(dy_ref, wd_ref, gu_ref, after_ref, o_ref):
        da = lax.dot_general(dy_ref[...], wd_ref[...], NT, preferred_element_type=F32).astype(BF16)
        g = gu_ref[:, :K]
        u = gu_ref[:, K:]
        sg = _sigmoid(g)
        o_ref[:, :K] = da * u * (sg * (1.0 + g * (1.0 - sg)))
        o_ref[:, K:] = da * (g * sg)

    return pl.pallas_call(
        body, name=name, grid=(S // tm,),
        in_specs=[pl.BlockSpec((tm, D), _row(0)), _resident((K, D)), pl.BlockSpec((tm, 2 * K), _row(0)), ANY],
        out_specs=pl.BlockSpec((tm, 2 * K), _row(0)),
        out_shape=jax.ShapeDtypeStruct((S, 2 * K), BF16),
        compiler_params=_params(("parallel",)),
    )(dy, wd, gu, after)


def matmul_nt(a, w, after, name):
    S, D = a.shape
    K = w.shape[0]
    tm = min(MATMUL_ROW_TILE, S)

    def body(a_ref, w_ref, after_ref, o_ref):
        o_ref[...] = lax.dot_general(a_ref[...], w_ref[...], NT, preferred_element_type=F32).astype(BF16)

    return pl.pallas_call(
        body, name=name, grid=(S // tm,),
        in_specs=[pl.BlockSpec((tm, D), _row(0)), _resident((K, D)), ANY],
        out_specs=pl.BlockSpec((tm, K), _row(0)),
        out_shape=jax.ShapeDtypeStruct((S, K), BF16),
        compiler_params=_params(("parallel",)),
    )(a, w, after)


def dgrad_mod(dpre, w, dxa, xin, scl, prev, alpha, name):
    S, D = xin.shape
    NS, _, n = w.shape
    tm = min(MATMUL_ROW_TILE, S)
    wres = prev[5] if prev is not None else None

    def body(*refs):
        dp_ref, w_ref, dxa_ref, xin_ref, scl_ref = refs[:5]
        acc_ref = refs[-1]

        @pl.when(pl.program_id(0) == 0)
        def _():
            acc_ref[...] = jnp.zeros_like(acc_ref)

        dh = jnp.zeros((tm, D), F32)
        for s in range(NS):
            dh = dh + lax.dot_general(dp_ref[:, s * n:(s + 1) * n].astype(BF16), w_ref[s], NT,
                                      preferred_element_type=F32)
        dx = dxa_ref[...] + dh * (1.0 + scl_ref[...])
        acc_ref[0:1, :] += jnp.sum(dh * xin_ref[...], axis=0, keepdims=True)
        acc_ref[1:2, :] += jnp.sum(dh, axis=0, keepdims=True)
        if prev is None:
            refs[5][...] = dx
        else:
            xh_ref, rs_ref, lg_ref, y_ref, gw_ref, pdxa_ref, pdy_ref = refs[5:12]
            _ln_res_bwd_tile(dx, xh_ref, rs_ref, lg_ref, y_ref, gw_ref, wres, alpha, pdxa_ref, pdy_ref, acc_ref, 2)

    tile = pl.BlockSpec((tm, D), _row(0))
    vec = pl.BlockSpec((1, D), _fixed2)
    in_specs = [pl.BlockSpec((tm, NS * n), _row(0)), _resident((NS, D, n)), tile, tile, vec]
    args = [dpre, w, dxa, xin, scl]
    if prev is None:
        out_specs = [tile]
        out_shape = [jax.ShapeDtypeStruct((S, D), F32)]
    else:
        in_specs += [tile, pl.BlockSpec((tm, 1), _row(0)), vec, tile, vec]
        args += list(prev[:5])
        out_specs = [tile, tile]
        out_shape = [jax.ShapeDtypeStruct((S, D), F32), jax.ShapeDtypeStruct((S, D), BF16)]
    return pl.pallas_call(
        body, name=name, grid=(S // tm,), in_specs=in_specs,
        out_specs=out_specs + [pl.BlockSpec((8, D), _fixed2)],
        out_shape=out_shape + [jax.ShapeDtypeStruct((8, D), F32)],
        compiler_params=_params(("arbitrary",)),
    )(*args)


PAIR_COLLECTIVE_ID = 1
FILL_COLLECTIVE_ID = 2


def wgrad_pair(a, b, J, kb, nb, a_block, b_block, half_idx, name):
    S = b.shape[0]
    ts = min(WGRAD_ROWS, S)
    nsteps = S // ts

    def body(h_ref, a_ref, b_ref, o_ref, acc_ref, send_buf, recv_buf, send_sems, recv_sems):
        jj, si = pl.program_id(0), pl.program_id(1)
        x, y, c = _coords()
        j = lax.rem(jj, J)
        last = si == nsteps - 1

        def copy(blk):
            return pltpu.make_async_remote_copy(
                src_ref=send_buf.at[blk], dst_ref=recv_buf.at[blk], send_sem=send_sems.at[blk],
                recv_sem=recv_sems.at[blk], device_id=(x, y, 1 - c), device_id_type=MESH)

        @pl.when(jnp.logical_and(jj == 0, si == 0))
        def _():
            barrier = pltpu.get_barrier_semaphore()
            pl.semaphore_signal(barrier, inc=1, device_id=(x, y, 1 - c), device_id_type=MESH)
            pl.semaphore_wait(barrier, 1)

        @pl.when(si == 0)
        def _():
            acc_ref[...] = jnp.zeros_like(acc_ref)

        acc_ref[...] += lax.dot_general(a_ref[...], b_ref[...].astype(BF16), TN, preferred_element_type=F32)

        @pl.when(jnp.logical_and(last, jj < J))
        def _():
            send_buf[j] = acc_ref[...].astype(BF16)
            copy(j).start()

        @pl.when(jnp.logical_and(last, jj >= J))
        def _():
            copy(j).wait_recv()
            o_ref[...] = (acc_ref[...] + recv_buf[j].astype(F32)).astype(BF16)

        @pl.when(jnp.logical_and(last, jj == 2 * J - 1))
        def _():
            for blk in range(J):
                copy(blk).wait_send()

    def half(jj, h):
        return jnp.where(jj < J, 1 - h[0], h[0])

    return pl.pallas_call(
        body, name=name,
        grid_spec=pltpu.PrefetchScalarGridSpec(
            num_scalar_prefetch=1, grid=(2 * J, nsteps),
            in_specs=[pl.BlockSpec((ts, kb), lambda jj, s, h: (s, a_block(lax.rem(jj, J), half(jj, h)))),
                      pl.BlockSpec((ts, nb), lambda jj, s, h: (s, b_block(lax.rem(jj, J), half(jj, h))))],
            out_specs=pl.BlockSpec((None, kb, nb), lambda jj, s, h: (jnp.maximum(jj - J, 0), 0, 0)),
            scratch_shapes=[pltpu.VMEM((kb, nb), F32), pltpu.VMEM((J, kb, nb), BF16), pltpu.VMEM((J, kb, nb), BF16),
                            pltpu.SemaphoreType.DMA((J,)), pltpu.SemaphoreType.DMA((J,))]),
        out_shape=jax.ShapeDtypeStruct((J, kb, nb), BF16),
        compiler_params=pltpu.CompilerParams(
            vmem_limit_bytes=VMEM_LIMIT_MB * 1024 * 1024, dimension_semantics=("arbitrary", "arbitrary"),
            collective_id=PAIR_COLLECTIVE_ID),
    )(half_idx, a, b)


def _window_mask():
    t = lax.broadcasted_iota(jnp.int32, (GMLP_WINDOW, GMLP_WINDOW), 0)
    s = lax.broadcasted_iota(jnp.int32, (GMLP_WINDOW, GMLP_WINDOW), 1)
    return ((s // CHUNK) <= (t // CHUNK)).astype(F32)


def sgu_fwd(pre, glg, glb, ws, bst, name):
    S, H2 = pre.shape
    H = H2 // 2
    W, G = GMLP_WINDOW, GMLP_GROUPS
    gd = H // G
    tm = min(ROW_TILE, S)

    def body(pre_ref, glg_ref, glb_ref, ws_ref, bst_ref, q_ref):
        u = _gelu(pre_ref[:, :H])
        v = _gelu(pre_ref[:, H:])
        mu = jnp.mean(v, axis=-1, keepdims=True)
        vc = v - mu
        var = jnp.mean(vc * vc, axis=-1, keepdims=True)
        vn = ((vc * lax.rsqrt(var + LN_EPS)) * glg_ref[...] + glb_ref[...]).astype(BF16)
        mask = _window_mask()
        for g in range(G):
            wsg = (ws_ref[g] * mask).astype(BF16)
            bcol = bst_ref[:, g:g + 1]
            for wi in range(tm // W):
                rows = slice(wi * W, (wi + 1) * W)
                cols = slice(g * gd, (g + 1) * gd)
                s = jnp.dot(wsg, vn[rows, cols], preferred_element_type=F32) + bcol
                q_ref[rows, cols] = (u[rows, cols] * s).astype(BF16)

    return pl.pallas_call(
        body, name=name, grid=(S // tm,),
        in_specs=[pl.BlockSpec((tm, H2), _row(0)), pl.BlockSpec((1, H), _fixed2), pl.BlockSpec((1, H), _fixed2),
                  pl.BlockSpec((G, W, W), _fixed3), pl.BlockSpec((W, G), _fixed2)],
        out_specs=pl.BlockSpec((tm, H), _row(0)),
        out_shape=jax.ShapeDtypeStruct((S, H), BF16),
        compiler_params=_params(("parallel",)),
    )(pre, glg, glb, ws, bst)


def sgu_bwd(dq, pre, glg, glb, ws, bst, name):
    S, H2 = pre.shape
    H = H2 // 2
    W, G = GMLP_WINDOW, GMLP_GROUPS
    gd = H // G
    tm = min(ROW_TILE, S)

    def body(dq_ref, pre_ref, glg_ref, glb_ref, ws_ref, bst_ref,
             dpre_ref, dws_ref, dss_ref, dgl_ref, dbin_ref, du_s, dvn_s):
        @pl.when(pl.program_id(0) == 0)
        def _():
            dws_ref[...] = jnp.zeros_like(dws_ref)
            dss_ref[...] = jnp.zeros_like(dss_ref)
            dgl_ref[...] = jnp.zeros_like(dgl_ref)
            dbin_ref[...] = jnp.zeros_like(dbin_ref)

        pu = pre_ref[:, :H]
        pv = pre_ref[:, H:]
        u = _gelu(pu)
        v = _gelu(pv)
        mu = jnp.mean(v, axis=-1, keepdims=True)
        vc = v - mu
        var = jnp.mean(vc * vc, axis=-1, keepdims=True)
        rstd = lax.rsqrt(var + LN_EPS)
        vhat = vc * rstd
        vn = (vhat * glg_ref[...] + glb_ref[...]).astype(BF16)
        mask = _window_mask()
        for g in range(G):
            wsg = (ws_ref[g] * mask).astype(BF16)
            bcol = bst_ref[:, g:g + 1]
            cols = slice(g * gd, (g + 1) * gd)
            for wi in range(tm // W):
                rows = slice(wi * W, (wi + 1) * W)
                vblk = vn[rows, cols]
                s = jnp.dot(wsg, vblk, preferred_element_type=F32) + bcol
                dqb = dq_ref[rows, cols].astype(F32)
                du_s[rows, cols] = dqb * s
                ds = dqb * u[rows, cols]
                dss_ref[:, cols] += ds
                dsb = ds.astype(BF16)
                dvn_s[rows, cols] = lax.dot_general(wsg, dsb, TN, preferred_element_type=F32)
                dws_ref[g] += lax.dot_general(dsb, vblk, NT, preferred_element_type=F32) * mask
        dvn = dvn_s[...]
        dgl_ref[0:1, :] += jnp.sum(dvn * vhat, axis=0, keepdims=True)
        dgl_ref[1:2, :] += jnp.sum(dvn, axis=0, keepdims=True)
        dvh = dvn * glg_ref[...]
        m1 = jnp.mean(dvh, axis=-1, keepdims=True)
        m2 = jnp.mean(dvh * vhat, axis=-1, keepdims=True)
        dv = rstd * (dvh - m1 - vhat * m2)
        dpu = du_s[...] * _gelu_grad(pu)
        dpv = dv * _gelu_grad(pv)
        dbin_ref[0:1, :H] += jnp.sum(dpu, axis=0, keepdims=True)
        dbin_ref[0:1, H:] += jnp.sum(dpv, axis=0, keepdims=True)
        dpre_ref[:, :H] = dpu.astype(BF16)
        dpre_ref[:, H:] = dpv.astype(BF16)

    return pl.pallas_call(
        body, name=name, grid=(S // tm,),
        in_specs=[pl.BlockSpec((tm, H), _row(0)), pl.BlockSpec((tm, H2), _row(0)), pl.BlockSpec((1, H), _fixed2),
                  pl.BlockSpec((1, H), _fixed2), pl.BlockSpec((G, W, W), _fixed3), pl.BlockSpec((W, G), _fixed2)],
        out_specs=[pl.BlockSpec((tm, H2), _row(0)), pl.BlockSpec((G, W, W), _fixed3), pl.BlockSpec((W, H), _fixed2),
                   pl.BlockSpec((8, H), _fixed2), pl.BlockSpec((8, H2), _fixed2)],
        out_shape=[jax.ShapeDtypeStruct((S, H2), BF16), jax.ShapeDtypeStruct((G, W, W), F32),
                   jax.ShapeDtypeStruct((W, H), F32), jax.ShapeDtypeStruct((8, H), F32),
                   jax.ShapeDtypeStruct((8, H2), F32)],
        scratch_shapes=[pltpu.VMEM((tm, H), F32), pltpu.VMEM((tm, H), F32)],
        compiler_params=_params(("arbitrary",)),
    )(dq, pre, glg, glb, ws, bst)


def group_lane_sum(dss, name):
    W, H = dss.shape
    gd = H // GMLP_GROUPS

    def body(d_ref, o_ref):
        j = lax.broadcasted_iota(jnp.int32, (H, LANES), 0)
        g = lax.broadcasted_iota(jnp.int32, (H, LANES), 1)
        ind = ((j // gd) == g).astype(F32)
        o_ref[...] = jnp.dot(d_ref[...], ind, preferred_element_type=F32, precision=HIGHEST)

    return pl.pallas_call(
        body, name=name, in_specs=[VMEM_SPEC], out_specs=VMEM_SPEC,
        out_shape=jax.ShapeDtypeStruct((W, LANES), F32), compiler_params=_params(),
    )(dss)


def _attn_load(j, cps, q_ref, k_ref, v_ref):
    r = lax.broadcasted_iota(jnp.int32, (CHUNK, BAND), 1)
    chunks = []
    for cc in range(cps):
        start = pl.multiple_of((j * cps + cc) * CHUNK, CHUNK)
        chunks.append((q_ref[cc * CHUNK:(cc + 1) * CHUNK, :], k_ref[pl.ds(start, BAND), :],
                       v_ref[pl.ds(start, BAND), :], (r + start) >= LEFT_PAD))
    return chunks


def _attn_probs(chunks, b_ref, sels, scale):
    qms = [[jnp.where(sel, q2, jnp.zeros_like(q2)) for sel in sels] for q2, _, _, _ in chunks]
    raw = [[lax.dot_general(qm, k2, NT, preferred_element_type=F32) for qm in qms[cc]]
           for cc, (_, k2, _, _) in enumerate(chunks)]
    probs = []
    for cc, (_, _, _, valid) in enumerate(chunks):
        row = []
        for sub in range(2):
            s = jnp.where(valid, raw[cc][sub] * scale + b_ref[sub], -jnp.inf)
            e = jnp.exp(s - jnp.max(s, axis=-1, keepdims=True))
            row.append(e / jnp.sum(e, axis=-1, keepdims=True))
        probs.append(row)
    return qms, probs


def attn_fwd(q, kpad, vpad, bias, name):
    S, D = q.shape
    HP = D // LANES
    cps = min(ATTN_CHUNKS_PER_STEP, S // CHUNK)
    tq = cps * CHUNK
    scale = HEAD_DIM ** -0.5

    def body(q_ref, k_ref, v_ref, b_ref, o_ref):
        sel0 = lax.broadcasted_iota(jnp.int32, (CHUNK, LANES), 1) < HEAD_DIM
        chunks = _attn_load(pl.program_id(1), cps, q_ref, k_ref, v_ref)
        _, probs = _attn_probs(chunks, b_ref, (sel0, jnp.logical_not(sel0)), scale)
        outs = [[jnp.dot(probs[cc][sub].astype(BF16), v2, preferred_element_type=F32) for sub in range(2)]
                for cc, (_, _, v2, _) in enumerate(chunks)]
        o_ref[...] = jnp.concatenate([jnp.where(sel0, o[0], o[1]) for o in outs], axis=0).astype(BF16)

    kv_spec = pl.BlockSpec((S + LEFT_PAD, LANES), lambda h, j: (0, h))
    return pl.pallas_call(
        body, name=name, grid=(HP, S // tq),
        in_specs=[pl.BlockSpec((tq, LANES), lambda h, j: (j, h)), kv_spec, kv_spec,
                  pl.BlockSpec((2, CHUNK, BAND), lambda h, j: (h, 0, 0))],
        out_specs=pl.BlockSpec((tq, LANES), lambda h, j: (j, h)),
        out_shape=jax.ShapeDtypeStruct((S, D), BF16),
        compiler_params=_params(("parallel", "parallel")),
    )(q, kpad, vpad, bias)


def attn_bwd(q, do, kpad, vpad, bias, dk_in, dv_in, name):
    S, D = q.shape
    HP = D // LANES
    NH = 2 * HP
    cps = min(ATTN_CHUNKS_PER_STEP, S // CHUNK)
    tq = cps * CHUNK
    nj = S // tq
    scale = HEAD_DIM ** -0.5

    def body(q_ref, do_ref, k_ref, v_ref, b_ref, dki_ref, dvi_ref, dq_ref, dk_ref, dv_ref, db_ref, dk_acc, dv_acc):
        j = pl.program_id(1)

        @pl.when(j == 0)
        def _():
            dk_acc[:LEFT_PAD, :] = jnp.zeros((LEFT_PAD, LANES), F32)
            dv_acc[:LEFT_PAD, :] = jnp.zeros((LEFT_PAD, LANES), F32)
            dk_acc[LEFT_PAD:, :] = dki_ref[...]
            dv_acc[LEFT_PAD:, :] = dvi_ref[...]
            db_ref[...] = jnp.zeros_like(db_ref)

        sel0 = lax.broadcasted_iota(jnp.int32, (CHUNK, LANES), 1) < HEAD_DIM
        sels = (sel0, jnp.logical_not(sel0))
        chunks = _attn_load(j, cps, q_ref, k_ref, v_ref)
        pairs = [(cc, sub) for cc in range(cps) for sub in range(2)]
        qms, probs = _attn_probs(chunks, b_ref, sels, scale)
        doms = [[jnp.where(sel, do_ref[cc * CHUNK:(cc + 1) * CHUNK, :], jnp.zeros((CHUNK, LANES), BF16))
                 for sel in sels] for cc in range(cps)]
        dps = {(cc, sub): lax.dot_general(doms[cc][sub], chunks[cc][2], NT, preferred_element_type=F32)
               for cc, sub in pairs}
        dss = {}
        for cc, sub in pairs:
            p = probs[cc][sub]
            dss[cc, sub] = p * (dps[cc, sub] - jnp.sum(dps[cc, sub] * p, axis=-1, keepdims=True))
        dsb = {key: ds.astype(BF16) for key, ds in dss.items()}
        dqs = {(cc, sub): jnp.dot(dsb[cc, sub], chunks[cc][1], preferred_element_type=F32) * scale
               for cc, sub in pairs}
        dks = {(cc, sub): lax.dot_general(dsb[cc, sub], qms[cc][sub], TN, preferred_element_type=F32) * scale
               for cc, sub in pairs}
        dvs = {(cc, sub): lax.dot_general(probs[cc][sub].astype(BF16), doms[cc][sub], TN,
                                          preferred_element_type=F32) for cc, sub in pairs}
        dq_ref[...] = jnp.concatenate([jnp.where(sel0, dqs[cc, 0], dqs[cc, 1]) for cc in range(cps)],
                                      axis=0).astype(BF16)
        for sub in range(2):
            total = dss[0, sub]
            for cc in range(1, cps):
                total = total + dss[cc, sub]
            db_ref[sub] += total
        dk_parts = [dks[cc, 0] + dks[cc, 1] for cc in range(cps)]
        dv_parts = [dvs[cc, 0] + dvs[cc, 1] for cc in range(cps)]

        def window(parts):
            blocks = []
            for rb in range(cps - 1 + BAND // CHUNK):
                acc = None
                for cc in range(cps):
                    b = rb - cc
                    if 0 <= b < BAND // CHUNK:
                        piece = parts[cc][b * CHUNK:(b + 1) * CHUNK, :]
                        acc = piece if acc is None else acc + piece
                blocks.append(acc)
            return jnp.concatenate(blocks, axis=0)

        span = pl.ds(pl.multiple_of(j * cps * CHUNK, CHUNK), (cps - 1) * CHUNK + BAND)
        dk_acc[span, :] += window(dk_parts)
        dv_acc[span, :] += window(dv_parts)

        @pl.when(j == nj - 1)
        def _():
            dk_ref[...] = dk_acc[LEFT_PAD:, :]
            dv_ref[...] = dv_acc[LEFT_PAD:, :]

    q_spec = pl.BlockSpec((tq, LANES), lambda h, j: (j, h))
    kv_spec = pl.BlockSpec((S + LEFT_PAD, LANES), lambda h, j: (0, h))
    col_spec = pl.BlockSpec((S, LANES), lambda h, j: (0, h))
    b_spec = pl.BlockSpec((2, CHUNK, BAND), lambda h, j: (h, 0, 0))
    return pl.pallas_call(
        body, name=name, grid=(HP, nj),
        in_specs=[q_spec, q_spec, kv_spec, kv_spec, b_spec, col_spec, col_spec],
        out_specs=[q_spec, col_spec, col_spec, b_spec],
        out_shape=[jax.ShapeDtypeStruct((S, D), BF16), jax.ShapeDtypeStruct((S, D), F32),
                   jax.ShapeDtypeStruct((S, D), F32), jax.ShapeDtypeStruct((NH, CHUNK, BAND), F32)],
        scratch_shapes=[pltpu.VMEM((S + LEFT_PAD, LANES), F32), pltpu.VMEM((S + LEFT_PAD, LANES), F32)],
        compiler_params=_params(("parallel", "arbitrary")),
    )(q, do, kpad, vpad, bias, dk_in, dv_in)


def _rel_onehot(t):
    r = lax.broadcasted_iota(jnp.int32, (BAND, N_REL), 0)
    i = lax.broadcasted_iota(jnp.int32, (BAND, N_REL), 1)
    idx = jnp.clip(t + LEFT_PAD - r, -(CHUNK - 1), MAX_REL) + (CHUNK - 1)
    return (idx == i).astype(BF16)


def _split3(v):
    hi = v.astype(BF16)
    rest = v - hi.astype(F32)
    mid = rest.astype(BF16)
    return hi, mid, (rest - mid.astype(F32)).astype(BF16)


def _dot_onehot(parts, onehot, dims):
    hi, mid, lo = [lax.dot_general(p, onehot, dims, preferred_element_type=F32) for p in parts]
    return (hi + mid) + lo


def bias_expand(rb, name):
    NH = rb.shape[0]

    def body(rb_ref, o_ref):
        parts = _split3(rb_ref[...])

        def step(t, carry):
            o_ref[t] = _dot_onehot(parts, _rel_onehot(t), NT)
            return carry

        lax.fori_loop(0, CHUNK, step, 0)

    return pl.pallas_call(
        body, name=name, in_specs=[VMEM_SPEC], out_specs=VMEM_SPEC,
        out_shape=jax.ShapeDtypeStruct((CHUNK, NH, BAND), F32), compiler_params=_params(),
    )(rb)


def bias_grad(dsum, name):
    NH = dsum.shape[1]

    def body(d_ref, o_ref):
        def step(t, acc):
            return acc + _dot_onehot(_split3(d_ref[t]), _rel_onehot(t), (((1,), (0,)), ((), ())))

        o_ref[...] = lax.fori_loop(0, CHUNK, step, jnp.zeros((NH, N_REL), F32))

    return pl.pallas_call(
        body, name=name, in_specs=[VMEM_SPEC], out_specs=VMEM_SPEC,
        out_shape=jax.ShapeDtypeStruct((NH, N_REL), F32), compiler_params=_params(),
    )(dsum)


def loss_grad(y, tgt, name):
    S, D = y.shape
    tm = min(MATMUL_ROW_TILE, S)

    def body(y_ref, t_ref, d_ref, acc_ref):
        @pl.when(pl.program_id(0) == 0)
        def _():
            acc_ref[...] = jnp.zeros_like(acc_ref)

        err = y_ref[...] - t_ref[...]
        d_ref[...] = err * (1.0 / D)
        acc_ref[0:1, :] += jnp.sum(err * err, axis=0, keepdims=True)

    tile = pl.BlockSpec((tm, D), _row(0))
    return pl.pallas_call(
        body, name=name, grid=(S // tm,), in_specs=[tile, tile],
        out_specs=[tile, pl.BlockSpec((8, D), _fixed2)],
        out_shape=[jax.ShapeDtypeStruct((S, D), F32), jax.ShapeDtypeStruct((8, D), F32)],
        compiler_params=_params(("arbitrary",)),
    )(y, tgt)


def _col_tile(n):
    for t in (768, 512, 256, 128):
        if n % t == 0:
            return t
    return n


def ada_fwd(c_all, w, b, name):
    L, D, n = w.shape
    tn = _col_tile(n)

    def body(c_ref, w_ref, b_ref, o_ref):
        cv = c_ref[...]
        ca = cv * _sigmoid(cv)
        o_ref[...] = jnp.dot(ca, w_ref[...], preferred_element_type=F32, precision=HIGHEST) + b_ref[...]

    return pl.pallas_call(
        body, name=name, grid=(L, n // tn),
        in_specs=[pl.BlockSpec((N_DEV, D), lambda l, j: (0, 0)), pl.BlockSpec((None, D, tn), lambda l, j: (l, 0, j)),
                  pl.BlockSpec((None, 1, tn), lambda l, j: (l, 0, j))],
        out_specs=pl.BlockSpec((None, N_DEV, tn), lambda l, j: (l, 0, j)),
        out_shape=jax.ShapeDtypeStruct((L, N_DEV, n), F32),
        compiler_params=_params(("parallel", "parallel")),
    )(c_all, w, b)


def ada_wgrad(c_all_t, dmod, name):
    L, _, n = dmod.shape
    D = c_all_t.shape[0]
    tn = _col_tile(n)

    def body(c_ref, d_ref, o_ref):
        cv = c_ref[...]
        ca = cv * _sigmoid(cv)
        o_ref[...] = jnp.dot(ca, d_ref[...], preferred_element_type=F32, precision=HIGHEST)

    return pl.pallas_call(
        body, name=name, grid=(L, n // tn),
        in_specs=[pl.BlockSpec((D, N_DEV), lambda l, j: (0, 0)), pl.BlockSpec((None, N_DEV, tn), lambda l, j: (l, 0, j))],
        out_specs=pl.BlockSpec((None, D, tn), lambda l, j: (l, 0, j)),
        out_shape=jax.ShapeDtypeStruct((L, D, n), F32),
        compiler_params=_params(("parallel", "parallel")),
    )(c_all_t, dmod)


ELEMENTWISE_BLOCK_BYTES = 3 * 1024 * 1024


def _elementwise_rows(rows, row_bytes):
    for t in (4096, 2048, 1024, 512, 256, 128, 64, 32, 16):
        if rows % t == 0 and t * row_bytes <= ELEMENTWISE_BLOCK_BYTES:
            return t
    return rows


def sum_leading(a, name):
    n, M, N = a.shape
    tr = _elementwise_rows(M, n * N * 4)

    def body(a_ref, o_ref):
        acc = a_ref[0].astype(F32)
        for i in range(1, n):
            acc = acc + a_ref[i].astype(F32)
        o_ref[...] = acc

    return pl.pallas_call(
        body, name=name, grid=(M // tr,),
        in_specs=[pl.BlockSpec((n, tr, N), lambda i: (0, i, 0))],
        out_specs=pl.BlockSpec((tr, N), _row(0)),
        out_shape=jax.ShapeDtypeStruct((M, N), F32),
        compiler_params=_params(("parallel",)),
    )(a)


def chip_sum(psum, land, chip_idx, transposed, name):
    n, M, N = psum.shape
    tr = M if transposed else _elementwise_rows(M, N * 8)

    def body(s_ref, p_ref, a_ref, b_ref, c_ref, o_ref):
        total = ((p_ref[...].astype(F32) + a_ref[...].astype(F32)) + b_ref[...].astype(F32)) + c_ref[...].astype(F32)
        o_ref[...] = jnp.transpose(total) if transposed else total

    def entry(j):
        return pl.BlockSpec((None, tr, N), lambda i, s: ((s[0] + j) % n, i, 0))

    out_block, out_dims = ((N, tr), (N, M)) if transposed else ((tr, N), (M, N))
    return pl.pallas_call(
        body, name=name,
        grid_spec=pltpu.PrefetchScalarGridSpec(
            num_scalar_prefetch=1, grid=(M // tr,),
            in_specs=[entry(0), entry(1), entry(2), entry(3)],
            out_specs=pl.BlockSpec(out_block, lambda i, s: (0, 0) if transposed else (i, 0))),
        out_shape=jax.ShapeDtypeStruct(out_dims, F32),
        compiler_params=_params(("parallel",)),
    )(chip_idx, psum, land, land, land)


def adamw(w, g, m, v, name):
    M, N = w.shape
    tr = _elementwise_rows(M, N * 4)
    c1 = 1.0 - ADAM_B1 ** ADAM_STEP
    c2 = 1.0 - ADAM_B2 ** ADAM_STEP

    def body(w_ref, g_ref, m_ref, v_ref, d_ref, nm_ref, nv_ref):
        g = g_ref[...]
        nm = ADAM_B1 * m_ref[...] + (1.0 - ADAM_B1) * g
        nv = ADAM_B2 * v_ref[...] + (1.0 - ADAM_B2) * (g * g)
        d_ref[...] = -ADAM_LR * ((nm / c1) / (jnp.sqrt(nv / c2) + ADAM_EPS) + ADAM_WD * w_ref[...])
        nm_ref[...] = nm
        nv_ref[...] = nv

    spec = pl.BlockSpec((tr, N), _row(0))
    shp = jax.ShapeDtypeStruct((M, N), F32)
    return pl.pallas_call(
        body, name=name, grid=(M // tr,), in_specs=[spec] * 4, out_specs=[spec] * 3, out_shape=[shp] * 3,
        compiler_params=_params(("parallel",)),
    )(w, g, m, v)


def _coords():
    return lax.axis_index("x"), lax.axis_index("y"), lax.axis_index("c")


def all_gather8(block, name):
    m_per, n = block.shape

    def body(x_ref, out_ref, send_sems, recv_sems, local_sem):
        x, y, c = _coords()
        me, sibling = (x, y, c), (x, y, 1 - c)
        chips = [(1 - x, y), (x, 1 - y), (1 - x, 1 - y)]

        def rows(px, py, pc):
            return out_ref.at[pl.ds((4 * px + 2 * py + pc) * m_per, m_per), :]

        def copy(k, blk, to, src=None):
            return pltpu.make_async_remote_copy(
                src_ref=rows(*blk) if src is None else src, dst_ref=rows(*blk),
                send_sem=send_sems.at[k], recv_sem=recv_sems.at[k], device_id=to, device_id_type=MESH)

        mine = pltpu.make_async_copy(x_ref, rows(*me), local_sem)
        mine.start()
        first = [copy(0, me, sibling, src=x_ref)]
        first += [copy(1 + j, me, (*chip, c), src=x_ref) for j, chip in enumerate(chips)]
        for cp in first:
            cp.start()
        passed = [copy(4 + j, (*chip, c), sibling) for j, chip in enumerate(chips)]
        for j, chip in enumerate(chips):
            copy(1 + j, (*chip, c), me).wait_recv()
            passed[j].start()
        copy(0, sibling, me).wait_recv()
        for j, chip in enumerate(chips):
            copy(4 + j, (*chip, 1 - c), me).wait_recv()
        for cp in first + passed:
            cp.wait_send()
        mine.wait()

    return pl.pallas_call(
        body, name=name, in_specs=[VMEM_SPEC], out_specs=VMEM_SPEC,
        out_shape=jax.ShapeDtypeStruct((N_DEV * m_per, n), block.dtype),
        scratch_shapes=[pltpu.SemaphoreType.DMA((7,)), pltpu.SemaphoreType.DMA((7,)), pltpu.SemaphoreType.DMA],
        compiler_params=_params(),
    )(block)


def _other_chips(x, y):
    return [(1 - x, y), (x, 1 - y), (1 - x, 1 - y)]


HBM_SPEC = pl.BlockSpec(memory_space=pltpu.HBM)
SEM_SPEC = pl.BlockSpec(memory_space=pltpu.SEMAPHORE)
DATAFLOW = pltpu.SideEffectType.DATAFLOW_SIDE_EFFECTING


def _chip_peers(x, y, c):
    return [(px, py, c) for px, py in _other_chips(x, y)]


def _sibling_peer(x, y, c):
    return [(x, y, 1 - c)]


def _weight_desc(_, k, land_ref, peer, me):
    h = land_ref.shape[1] // 2
    rows = pl.ds(me[2] * h, h)
    mine = land_ref.at[2 * me[0] + me[1], rows, :]
    return mine, mine, land_ref.at[2 * peer[0] + peer[1], rows, :]


def _grad_desc(psum_ref, k, land_ref, peer, me):
    return psum_ref.at[2 * peer[0] + peer[1]], land_ref.at[2 * me[0] + me[1]], land_ref.at[2 * peer[0] + peer[1]]


def _pair_desc(grad_ref, k, land_ref, peer, me):
    h = land_ref.shape[1]
    return grad_ref.at[:, pl.ds(peer[2] * h, h), :], land_ref, land_ref


def _whole_desc(src_ref, k, land_ref, peer, me):
    return src_ref, land_ref, land_ref


def exchange_start(srcs, lands, units, groups, desc, peers, after, name):
    n_s, n_l, n_g = len(srcs), len(lands), len(groups)
    n_p = len(peers(0, 0, 0))

    def body(*refs):
        s_refs, l_refs = refs[:n_s], refs[n_s:n_s + n_l]
        outs = refs[n_s + n_l + 1:]
        sems, token = outs[:2 * n_g], outs[-1]
        me = _coords()
        for g, ids in enumerate(groups):
            for i, u in enumerate(ids):
                si, k = units[u]
                for j, peer in enumerate(peers(*me)):
                    src, dst, _ = desc(s_refs[si] if s_refs else None, k, l_refs[u], peer, me)
                    pltpu.make_async_remote_copy(
                        src_ref=src, dst_ref=dst, send_sem=sems[2 * g].at[n_p * i + j],
                        recv_sem=sems[2 * g + 1].at[n_p * i + j], device_id=peer, device_id_type=MESH).start()
        token[...] = jnp.zeros_like(token)

    arrs = list(srcs) + list(lands)
    sem_shapes = [pltpu.SemaphoreType.DMA((n_p * len(ids),)) for ids in groups for _ in range(2)]
    outs = pl.pallas_call(
        body, name=name,
        in_specs=[HBM_SPEC] * len(arrs) + [ANY],
        out_specs=[SEM_SPEC] * (2 * n_g) + [HBM_SPEC] * len(arrs) + [VMEM_SPEC],
        out_shape=sem_shapes + [pltpu.HBM(a.shape, a.dtype) for a in arrs] + [jax.ShapeDtypeStruct((8, LANES), F32)],
        input_output_aliases={i: 2 * n_g + i for i in range(len(arrs))},
        compiler_params=pltpu.CompilerParams(has_side_effects=DATAFLOW),
    )(*[pltpu.with_memory_space_constraint(a, pltpu.HBM) for a in arrs], after)
    sems = outs[:2 * n_g]
    thru = outs[2 * n_g:2 * n_g + len(arrs)]
    return sems, list(thru[:n_s]), list(thru[n_s:]), outs[-1]


def exchange_wait(srcs, lands, units, send_sem, recv_sem, desc, peers, after, name):
    n_s, n_l = len(srcs), len(lands)
    n_p = len(peers(0, 0, 0))

    def body(*refs):
        s_refs, l_refs = refs[:n_s], refs[n_s:n_s + n_l]
        send_sems, recv_sems = refs[n_s + n_l], refs[n_s + n_l + 1]
        me = _coords()
        for i, (si, k) in enumerate(units):
            for j, peer in enumerate(peers(*me)):
                src, _, mine = desc(s_refs[si] if s_refs else None, k, l_refs[i], peer, me)
                cp = pltpu.make_async_remote_copy(
                    src_ref=src, dst_ref=mine, send_sem=send_sems.at[n_p * i + j], recv_sem=recv_sems.at[n_p * i + j],
                    device_id=peer, device_id_type=MESH)
                cp.wait_send()
                cp.wait_recv()

    arrs = list(srcs) + list(lands)
    outs = pl.pallas_call(
        body, name=name,
        in_specs=[HBM_SPEC] * len(arrs) + [SEM_SPEC, SEM_SPEC, ANY],
        out_specs=[HBM_SPEC] * len(arrs),
        out_shape=[pltpu.HBM(a.shape, a.dtype) for a in arrs],
        input_output_aliases={i: i for i in range(len(arrs))},
        compiler_params=pltpu.CompilerParams(has_side_effects=DATAFLOW),
    )(*arrs, send_sem, recv_sem, after)
    return list(outs[:n_s]), list(outs[n_s:])


def sibling_fill(lands, name):
    n_u = len(lands)

    def body(*refs):
        ins, outs = refs[:n_u], refs[n_u:2 * n_u]
        send_sems, recv_sems = refs[2 * n_u:]
        x, y, c = _coords()
        barrier = pltpu.get_barrier_semaphore()
        pl.semaphore_signal(barrier, inc=1, device_id=(x, y, 1 - c), device_id_type=MESH)
        pl.semaphore_wait(barrier, 1)
        sends = []
        for u in range(n_u):
            h = ins[u].shape[1] // 2
            for j, (px, py) in enumerate(_other_chips(x, y)):
                part = (2 * px + py, pl.ds(c * h, h), slice(None))
                cp = pltpu.make_async_remote_copy(
                    src_ref=ins[u].at[part], dst_ref=outs[u].at[part], send_sem=send_sems.at[3 * u + j],
                    recv_sem=recv_sems.at[3 * u + j], device_id=(x, y, 1 - c), device_id_type=MESH)
                cp.start()
                sends.append(cp)
        for u in range(n_u):
            h = ins[u].shape[1] // 2
            for j, (px, py) in enumerate(_other_chips(x, y)):
                theirs = (2 * px + py, pl.ds((1 - c) * h, h), slice(None))
                pltpu.make_async_remote_copy(
                    src_ref=ins[u].at[theirs], dst_ref=outs[u].at[theirs], send_sem=send_sems.at[3 * u + j],
                    recv_sem=recv_sems.at[3 * u + j], device_id=(x, y, 1 - c), device_id_type=MESH).wait_recv()
        for cp in sends:
            cp.wait_send()

    return pl.pallas_call(
        body, name=name, in_specs=[ANY] * n_u, out_specs=[ANY] * n_u,
        out_shape=[jax.ShapeDtypeStruct(a.shape, a.dtype) for a in lands],
        input_output_aliases={i: i for i in range(n_u)},
        scratch_shapes=[pltpu.SemaphoreType.DMA((3 * n_u,)), pltpu.SemaphoreType.DMA((3 * n_u,))],
        compiler_params=pltpu.CompilerParams(vmem_limit_bytes=VMEM_LIMIT_MB * 1024 * 1024,
                                             collective_id=FILL_COLLECTIVE_ID),
    )(*lands)


def _pack_rows(parts):
    flat = jnp.concatenate([p.reshape(-1).astype(F32) for p in parts])
    n = flat.shape[0]
    padded = -(-n // (8 * LANES)) * (8 * LANES)
    return jnp.pad(flat, (0, padded - n)).reshape(-1, LANES)


def _unpack_rows(packed, shapes):
    flat = packed.reshape(-1)
    out, off = [], 0
    for s in shapes:
        size = 1
        for d in s:
            size *= d
        out.append(flat[off:off + size].reshape(s))
        off += size
    return out


def _shard_last(full, s_me):
    n = full.shape[-1] // N_CHIP
    return lax.dynamic_slice_in_dim(full, s_me * n, n, axis=full.ndim - 1)


def _unshard_last(g):
    moved = jnp.moveaxis(g, 0, -2)
    return moved.reshape(moved.shape[:-2] + (moved.shape[-2] * moved.shape[-1],))


def kernel(x, c, w_ada, b_ada, ln_g, ln_b, ffn_gu, ffn_down, gmlp_w_in, gmlp_b_in, gmlp_ln_g, gmlp_ln_b, gmlp_w_s, gmlp_b_s, gmlp_w_out, w_ada_kv, b_ada_kv, w_kv, attn_w_q, attn_rel_bias, attn_w_o, loss_target, m_w_ada, m_b_ada, m_ln_g, m_ln_b, m_ffn_gu, m_ffn_down, m_gmlp_w_in, m_gmlp_b_in, m_gmlp_ln_g, m_gmlp_ln_b, m_gmlp_w_s, m_gmlp_b_s, m_gmlp_w_out, m_w_ada_kv, m_b_ada_kv, m_w_kv, m_attn_w_q, m_attn_rel_bias, m_attn_w_o, v_w_ada, v_b_ada, v_ln_g, v_ln_b, v_ffn_gu, v_ffn_down, v_gmlp_w_in, v_gmlp_b_in, v_gmlp_ln_g, v_gmlp_ln_b, v_gmlp_w_s, v_gmlp_b_s, v_gmlp_w_out, v_w_ada_kv, v_b_ada_kv, v_w_kv, v_attn_w_q, v_attn_rel_bias, v_attn_w_o):
    xi, yi, ci = _coords()
    s_me = 2 * xi + yi
    dev = 4 * xi + 2 * yi + ci

    x0 = x[0]
    tgt = loss_target[0]
    S, D = x0.shape
    L = w_ada.shape[0]
    NA = gmlp_w_in.shape[0]
    NB = attn_w_q.shape[0]
    NH = D // HEAD_DIM
    alpha = (2.0 * L) ** 0.25
    n_ada = w_ada.shape[2]
    n_kv = w_ada_kv.shape[1]

    stack_names = ["ffn_gu", "ffn_down", "gmlp_w_in", "gmlp_w_out", "w_kv", "attn_w_q", "attn_w_o"]
    stack_src = dict(ffn_gu=ffn_gu, ffn_down=ffn_down, gmlp_w_in=gmlp_w_in, gmlp_w_out=gmlp_w_out, w_kv=w_kv[None],
                     attn_w_q=attn_w_q, attn_w_o=attn_w_o)
    stacks = [stack_src[nm].reshape((-1,) + stack_src[nm].shape[-2:]) for nm in stack_names]
    units = [(si, k) for si, st in enumerate(stacks) for k in range(st.shape[0])]
    unit_of = {(stack_names[si], k): u for u, (si, k) in enumerate(units)}
    weight_groups = [[("ffn_gu", 0)], [("ffn_down", 0)]]
    for l in range(L):
        mixer = [("gmlp_w_in", l), ("gmlp_w_out", l)] if l < NA else [("attn_w_q", l - NA), ("attn_w_o", l - NA)]
        first, last = [("ffn_gu", 2 * l), ("ffn_down", 2 * l)], [("ffn_gu", 2 * l + 1), ("ffn_down", 2 * l + 1)]
        if l == 0:
            weight_groups += [mixer, last]
        else:
            weight_groups += [([("w_kv", 0)] if l == NA else []) + first, mixer, last]
    weight_groups = [[unit_of[n] for n in names] for names in weight_groups]
    group_of = {u: g for g, ids in enumerate(weight_groups) for u in ids}

    c_all = all_gather8(jnp.broadcast_to(c, (8, D)), "ag_c").reshape(N_DEV, 8, D)[:, 0]
    b_ada_sh = lax.dynamic_slice_in_dim(b_ada, s_me * n_ada, n_ada, axis=1)
    b_kv_sh = lax.dynamic_slice_in_dim(b_ada_kv, s_me * n_kv, n_kv, axis=0)
    mod_part = ada_fwd(c_all, w_ada, b_ada_sh[:, None, :], "ada_fwd")
    mkv_part = ada_fwd(c_all, w_ada_kv[None], b_kv_sh[None, None, :], "ada_kv_fwd")
    part = jnp.concatenate([jnp.transpose(mod_part, (1, 0, 2)).reshape(N_DEV, L * n_ada), mkv_part[0]], axis=1)
    width = part.shape[1]
    pad_w = -(-width // LANES) * LANES - width
    all_part = all_gather8(jnp.pad(part, ((0, 0), (0, pad_w))), "ag_mod").reshape(N_DEV, N_DEV, width + pad_w)
    mine = lax.dynamic_index_in_dim(all_part[0::2], dev, axis=1, keepdims=False)
    mod = jnp.transpose(mine[:, :L * n_ada].reshape(N_CHIP, L, n_ada), (1, 0, 2)).reshape(L, N_MOD, D)
    mkv = mine[:, L * n_ada:width].reshape(2, D)

    def mrow(l, k):
        return mod[l, k][None, :]

    small_shapes = [ln_g.shape, ln_b.shape, gmlp_b_in.shape, gmlp_ln_g.shape, gmlp_ln_b.shape, attn_rel_bias.shape]
    small_pack = _pack_rows([ln_g, ln_b, gmlp_b_in, gmlp_ln_g, gmlp_ln_b, attn_rel_bias])
    small_all = all_gather8(small_pack, "ag_small_params").reshape((N_DEV,) + small_pack.shape)[0::2]
    sm = [_unpack_rows(small_all[s], small_shapes) for s in range(N_CHIP)]
    ln_g_f, ln_b_f, b_in_f, gln_g_f, gln_b_f, rel_f = [
        _unshard_last(jnp.stack([sm[s][i] for s in range(N_CHIP)])) for i in range(len(small_shapes))]

    def landing(u):
        si, k = units[u]
        shard = stacks[si][k]
        if group_of[u] < 2:
            shard = lax.optimization_barrier(shard)
        shard = shard.astype(BF16)
        return lax.dynamic_update_slice(lax.empty((N_CHIP,) + shard.shape, BF16), shard[None], (s_me, 0, 0))

    gathers_done = jnp.concatenate([mod.reshape(-1)[:LANES], small_all.reshape(-1)[:LANES]])
    w_sems, lands_t = {}, {}
    for part, groups, name in ((0, weight_groups[:2], "weight_send_start_first"),
                               (1, weight_groups[2:], "weight_send_start_rest")):
        ids = [u for grp in groups for u in grp]
        local = [[ids.index(u) for u in grp] for grp in groups]
        sems, _, lands, gathers_done = exchange_start([], [landing(u) for u in ids], [units[u] for u in ids], local,
                                                      _weight_desc, _chip_peers, gathers_done, name)
        for i, grp in enumerate(groups):
            w_sems[2 * part + i] = (sems[2 * i], sems[2 * i + 1])
        lands_t.update(zip(ids, lands))
    wg = {}
    latest = [gathers_done]

    def W(nm, k):
        u = unit_of[(nm, k)]
        if u not in wg:
            g = group_of[u]
            ids = weight_groups[g]
            _, got = exchange_wait([], [lands_t[v] for v in ids], [units[v] for v in ids], w_sems[g][0],
                                   w_sems[g][1], _weight_desc, _chip_peers, latest[0], "weight_send_wait_%d" % g)
            wg.update(zip(ids, sibling_fill(got, "weight_sibling_fill")))
        return wg[u]

    def Wrows(nm, k):
        w4 = W(nm, k)
        return w4.reshape(w4.shape[0] * w4.shape[1], w4.shape[2])

    bst = [jnp.transpose(gmlp_b_s[j]) for j in range(NA)]
    biases = {}

    def make_bias(j, dep):
        rel, _ = lax.optimization_barrier((rel_f[j], dep))
        biases[j] = jnp.transpose(bias_expand(rel, "bias_expand"), (1, 0, 2))
        return biases[j]

    saved = []
    xc = x0
    kpad = vpad = xkv = None
    for l in range(L):
        if l == 1 and NB > 1:
            latest[0] = make_bias(1, xc)
        if l == NA:
            xkv = xc
            kv, hkv = mod_matmul(xc, mkv[1][None], mkv[0][None], W("w_kv", 0), None, BF16, "kv_proj")
            kpad = jnp.pad(kv[:, :D], ((LEFT_PAD, 0), (0, 0)))
            vpad = jnp.pad(kv[:, D:], ((LEFT_PAD, 0), (0, 0)))
        sv = {}
        for i in (0, 2):
            k = 2 * l + i // 2
            gu, hv = mod_matmul(xc, mrow(l, 3 * i + 1), mrow(l, 3 * i), W("ffn_gu", k), None, BF16, "ffn_up")
            latest[0] = hv
            gw = 0.5 * (1.0 + mrow(l, 3 * i + 2))
            xn, xh, rs, yv, av = matmul_res_ln(gu, Wrows("ffn_down", k), xc, gw, ln_g_f[l, i][None],
                                               ln_b_f[l, i][None], alpha, True, "ffn_down")
            sv[i] = dict(x=xc, h=hv, gu=gu, a=av, xh=xh, rs=rs, y=yv, gw=gw)
            xc = latest[0] = xn
            if i == 0:
                if l == 0 and NB > 0:
                    latest[0] = make_bias(0, xc)
                gw = 1.0 + mrow(l, 5)
                if l < NA:
                    pre, hv = mod_matmul(xc, mrow(l, 4), mrow(l, 3), W("gmlp_w_in", l), b_in_f[l][None], F32,
                                         "gmlp_in")
                    qv = sgu_fwd(pre, gln_g_f[l][None], gln_b_f[l][None], gmlp_w_s[l], bst[l], "sgu_fwd")
                    xn, xh, rs, yv = matmul_res_ln(qv, Wrows("gmlp_w_out", l), xc, gw, ln_g_f[l, 1][None],
                                                   ln_b_f[l, 1][None], alpha, False, "gmlp_out")
                    sv[1] = dict(x=xc, h=hv, pre=pre, a=qv, xh=xh, rs=rs, y=yv, gw=gw)
                else:
                    j = l - NA
                    if j not in biases:
                        make_bias(j, xc)
                    qh, hv = mod_matmul(xc, mrow(l, 4), mrow(l, 3), Wrows("attn_w_q", j)[None], None, BF16, "attn_q")
                    ov = attn_fwd(qh, kpad, vpad, biases[j], "attn_fwd")
                    xn, xh, rs, yv = matmul_res_ln(ov, Wrows("attn_w_o", j), xc, gw, ln_g_f[l, 1][None],
                                                   ln_b_f[l, 1][None], alpha, False, "attn_out")
                    sv[1] = dict(x=xc, h=hv, q=qh, a=ov, xh=xh, rs=rs, y=yv, gw=gw)
                xc = latest[0] = xn
        saved.append(sv)

    dx, lacc = loss_grad(xc, tgt, "loss_grad")
    loss = lax.psum((0.5 / D) * jnp.sum(lacc[0]), ("x", "y", "c"))

    gpair = [None] * len(units)
    col_split = {u for u, (si, _) in enumerate(units) if stack_names[si] != "ffn_gu"}
    dmod = [[None] * N_MOD for _ in range(L)]
    d_ln_g = [[None] * 3 for _ in range(L)]
    d_ln_b = [[None] * 3 for _ in range(L)]
    d_b_in, d_gln_g, d_gln_b, d_ws, d_bs, d_rel = ([None] * NA, [None] * NA, [None] * NA, [None] * NA, [None] * NA,
                                                  [None] * NB)
    dk = jnp.zeros((S, D), F32)
    dv = jnp.zeros((S, D), F32)
    dmkv = None

    made = []

    core_idx = ci.astype(jnp.int32).reshape(1)
    chip_idx = s_me.astype(jnp.int32).reshape(1)

    def put(nm, k, a, b, name):
        u = unit_of[(nm, k)]
        rows, cols = stacks[units[u][0]].shape[1:]
        if nm == "ffn_gu":
            g = wgrad_pair(b, a, N_CHIP, cols, rows // 2, lambda j, p: j, lambda j, p: p, core_idx, name)
        elif nm in ("gmlp_w_in", "w_kv"):
            g = wgrad_pair(a, b, N_CHIP, rows, cols // 2, lambda j, p: 0, lambda j, p: 2 * j + p, core_idx, name)
        else:
            g = wgrad_pair(a, b, 1, N_CHIP * rows, cols // 2, lambda j, p: 0, lambda j, p: p, core_idx, name)
        gpair[u] = g.reshape((N_CHIP, -1, g.shape[-1]))
        made.append(u)

    own_half, sib_half = {}, {}
    n_started = [0]

    def start_grad_exchange(ids, after):
        psums = [gpair[u] for u in ids]
        n = len(ids)
        tag = n_started[0]
        n_started[0] += 1
        sems, ps_t, q_t, token = exchange_start(psums, [lax.empty(p.shape, p.dtype) for p in psums],
                                                [(i, 0) for i in range(n)], [list(range(n))], _grad_desc, _chip_peers,
                                                after, "grad_send_start_%d" % tag)
        return dict(ids=ids, tag=tag, sems=sems, ps=ps_t, q=q_t), token

    def finish_grad_exchange(pend, after):
        n = len(pend["ids"])
        ps_t, q = exchange_wait(pend["ps"], pend["q"], [(i, 0) for i in range(n)], pend["sems"][0], pend["sems"][1],
                                _grad_desc, _chip_peers, after, "grad_send_wait_%d" % pend["tag"])
        halves = [chip_sum(ps_t[i], q[i], chip_idx, u not in col_split, "grad_chip_sum")
                  for i, u in enumerate(pend["ids"])]
        sems, h_t, land_t, token = exchange_start(halves, [lax.empty(h.shape, h.dtype) for h in halves],
                                              [(i, 0) for i in range(n)], [list(range(n))], _whole_desc,
                                              _sibling_peer, halves[0], "half_send_start_%d" % pend["tag"])
        swaps.append(dict(ids=pend["ids"], tag=pend["tag"], sems=sems, h=h_t, land=land_t))
        return token

    def collect_halves(after):
        for sw in swaps:
            n = len(sw["ids"])
            h, land = exchange_wait(sw["h"], sw["land"], [(i, 0) for i in range(n)], sw["sems"][0], sw["sems"][1],
                                    _whole_desc, _sibling_peer, after, "half_send_wait_%d" % sw["tag"])
            for u, mine, theirs in zip(sw["ids"], h, land):
                own_half[u], sib_half[u] = mine, theirs
        swaps.clear()

    swaps = []
    pending = None
    started_before = jnp.zeros((8, LANES), F32)

    def ln_inputs(l, i):
        t = saved[l][i]
        return (t["xh"], t["rs"], ln_g_f[l, i][None], t["y"], t["gw"], 1.0 if i == 1 else 0.5)

    def record_ln(l, i, acc, row0):
        d_ln_g[l][i], d_ln_b[l][i], dmod[l][3 * i + 2] = acc[row0], acc[row0 + 1], acc[row0 + 2]

    ln_done = None
    for l in reversed(range(L)):
        if l == NA - 1:
            dkv = jnp.concatenate([dk, dv], axis=1)
            put("w_kv", 0, hkv, dkv, "kv_wgrad")
            pdxa, pdy, acc = dgrad_mod(dkv, W("w_kv", 0), dx, xkv, mkv[1][None], ln_inputs(l, 2), alpha, "kv_dgrad")
            dmkv = jnp.stack([acc[1], acc[0]])
            record_ln(l, 2, acc, 2)
            ln_done = (pdxa, pdy)
        sv = saved[l]
        for i in (2, 1, 0):
            t = sv[i]
            if ln_done is None:
                dxa, dy, acc1 = ln_res_bwd(dx, *ln_inputs(l, i), alpha, "ln_res_bwd")
                record_ln(l, i, acc1, 0)
            else:
                dxa, dy = ln_done
                ln_done = None
            before = (l, i - 1) if i > 0 else ((l - 1, 2) if l > 0 and l != NA else None)
            prev = ln_inputs(*before) if before is not None else None
            scl = mrow(l, 3 * i + 1)
            if i != 1:
                k = 2 * l + i // 2
                F = t["gu"].shape[1] // 2
                dgu = ffn_act_bwd(dy, Wrows("ffn_down", k), t["gu"], started_before, "ffn_act_bwd")
                put("ffn_down", k, t["a"], dy, "ffn_down_wgrad")
                put("ffn_gu", k, t["h"], dgu, "ffn_up_wgrad")
                res = dgrad_mod(dgu, W("ffn_gu", k), dxa, t["x"], scl, prev, alpha, "ffn_up_dgrad")
            elif l < NA:
                dq = matmul_nt(dy, Wrows("gmlp_w_out", l), started_before, "gmlp_out_dgrad")
                put("gmlp_w_out", l, t["a"], dy, "gmlp_out_wgrad")
                dpre, dws_l, dss, dgl, dbin = sgu_bwd(dq, t["pre"], gln_g_f[l][None], gln_b_f[l][None], gmlp_w_s[l],
                                                      bst[l], "sgu_bwd")
                d_ws[l] = dws_l
                d_bs[l] = jnp.transpose(group_lane_sum(dss, "sgu_bias_grad")[:, :GMLP_GROUPS])
                d_gln_g[l], d_gln_b[l], d_b_in[l] = dgl[0], dgl[1], dbin[0]
                put("gmlp_w_in", l, t["h"], dpre, "gmlp_in_wgrad")
                res = dgrad_mod(dpre, W("gmlp_w_in", l), dxa, t["x"], scl, prev, alpha, "gmlp_in_dgrad")
            else:
                j = l - NA
                do = matmul_nt(dy, Wrows("attn_w_o", j), started_before, "attn_out_dgrad")
                put("attn_w_o", j, t["a"], dy, "attn_out_wgrad")
                dqh, dk, dv, dbias = attn_bwd(t["q"], do, kpad, vpad, biases[j], dk, dv, "attn_bwd")
                d_rel[j] = bias_grad(jnp.transpose(dbias, (1, 0, 2)), "bias_grad")
                put("attn_w_q", j, t["h"], dqh, "attn_q_wgrad")
                res = dgrad_mod(dqh, Wrows("attn_w_q", j)[None], dxa, t["x"], scl, prev, alpha, "attn_q_dgrad")
            acc2 = res[-1]
            dmod[l][3 * i + 1], dmod[l][3 * i] = acc2[0], acc2[1]
            if before is None:
                dx = res[0]
            else:
                record_ln(*before, acc2, 2)
                ln_done = (res[0], res[1])
            if (i == 0 and l > 0) or (i == 1 and l == 0):
                started, started_before = start_grad_exchange(list(made), acc2)
                made.clear()
                if pending is not None:
                    started_before = started_before + finish_grad_exchange(pending, acc2)
                pending = started
    grad_x = dx[None]

    dvec = _pack_rows([jnp.stack([jnp.stack(r) for r in dmod]), dmkv])
    dvec = lax.optimization_barrier((dvec, [gpair[u] for u in made]))[0]
    n_dvec = L * N_MOD * D + 2 * D
    dall = all_gather8(dvec, "ag_dmod").reshape(N_DEV, -1, LANES)
    db_all = sum_leading(dall, "ada_bias_grad").reshape(-1)[:n_dvec]
    g_b_ada = db_all[:L * N_MOD * D].reshape(L, N_MOD * D)
    g_b_ada_kv = db_all[L * N_MOD * D:]
    dall2 = dall.reshape(N_DEV, -1)[:, :n_dvec]
    dmod_all = dall2[:, :L * N_MOD * D].reshape(N_DEV, L, N_MOD * D)
    dmod_sh = jnp.transpose(lax.dynamic_slice_in_dim(dmod_all, s_me * n_ada, n_ada, axis=2), (1, 0, 2))
    dmkv_sh = lax.dynamic_slice_in_dim(dall2[:, L * N_MOD * D:], s_me * n_kv, n_kv, axis=1)[None]
    c_all_t = jnp.transpose(c_all)
    g_w_ada = ada_wgrad(c_all_t, dmod_sh, "ada_wgrad")
    g_w_ada_kv = ada_wgrad(c_all_t, dmkv_sh, "ada_kv_wgrad")[0]

    small_g = [jnp.stack([jnp.stack(r) for r in d_ln_g]), jnp.stack([jnp.stack(r) for r in d_ln_b]),
               jnp.stack(d_b_in), jnp.stack(d_gln_g), jnp.stack(d_gln_b), jnp.stack(d_rel), jnp.stack(d_bs)]
    sg_shapes = [a.shape for a in small_g]
    sg_pack = _pack_rows(small_g)
    sg_all = all_gather8(sg_pack, "ag_small_grads").reshape(N_DEV, -1, LANES)
    sg_sum = _unpack_rows(sum_leading(sg_all, "small_grad_sum"), sg_shapes)
    g_ln_g, g_ln_b, g_b_in, g_gln_g, g_gln_b, g_rel = [_shard_last(a, s_me) for a in sg_sum[:6]]
    g_bs = sg_sum[6]
    ws_pack = jnp.stack(d_ws).astype(BF16).reshape(-1, LANES)
    ws_all = all_gather8(ws_pack, "ag_spatial_grads").reshape(N_DEV, -1, LANES)
    g_ws = sum_leading(ws_all, "spatial_grad_sum").reshape(gmlp_w_s.shape)

    last, _ = start_grad_exchange(list(made), sg_all)

    grads = dict(w_ada=g_w_ada, b_ada=g_b_ada, ln_g=g_ln_g, ln_b=g_ln_b, gmlp_b_in=g_b_in, gmlp_ln_g=g_gln_g,
                 gmlp_ln_b=g_gln_b, gmlp_w_s=g_ws, gmlp_b_s=g_bs, w_ada_kv=g_w_ada_kv, b_ada_kv=g_b_ada_kv,
                 attn_rel_bias=g_rel)
    weights = dict(w_ada=w_ada, b_ada=b_ada, ln_g=ln_g, ln_b=ln_b, ffn_gu=ffn_gu, ffn_down=ffn_down,
                   gmlp_w_in=gmlp_w_in, gmlp_b_in=gmlp_b_in, gmlp_ln_g=gmlp_ln_g, gmlp_ln_b=gmlp_ln_b,
                   gmlp_w_s=gmlp_w_s, gmlp_b_s=gmlp_b_s, gmlp_w_out=gmlp_w_out, w_ada_kv=w_ada_kv,
                   b_ada_kv=b_ada_kv, w_kv=w_kv, attn_w_q=attn_w_q, attn_rel_bias=attn_rel_bias, attn_w_o=attn_w_o)
    ms = dict(w_ada=m_w_ada, b_ada=m_b_ada, ln_g=m_ln_g, ln_b=m_ln_b, ffn_gu=m_ffn_gu, ffn_down=m_ffn_down,
              gmlp_w_in=m_gmlp_w_in, gmlp_b_in=m_gmlp_b_in, gmlp_ln_g=m_gmlp_ln_g, gmlp_ln_b=m_gmlp_ln_b,
              gmlp_w_s=m_gmlp_w_s, gmlp_b_s=m_gmlp_b_s, gmlp_w_out=m_gmlp_w_out, w_ada_kv=m_w_ada_kv,
              b_ada_kv=m_b_ada_kv, w_kv=m_w_kv, attn_w_q=m_attn_w_q, attn_rel_bias=m_attn_rel_bias,
              attn_w_o=m_attn_w_o)
    vs = dict(w_ada=v_w_ada, b_ada=v_b_ada, ln_g=v_ln_g, ln_b=v_ln_b, ffn_gu=v_ffn_gu, ffn_down=v_ffn_down,
              gmlp_w_in=v_gmlp_w_in, gmlp_b_in=v_gmlp_b_in, gmlp_ln_g=v_gmlp_ln_g, gmlp_ln_b=v_gmlp_ln_b,
              gmlp_w_s=v_gmlp_w_s, gmlp_b_s=v_gmlp_b_s, gmlp_w_out=v_gmlp_w_out, w_ada_kv=v_w_ada_kv,
              b_ada_kv=v_b_ada_kv, w_kv=v_w_kv, attn_w_q=v_attn_w_q, attn_rel_bias=v_attn_rel_bias,
              attn_w_o=v_attn_w_o)
    order = ["w_ada", "b_ada", "ln_g", "ln_b", "ffn_gu", "ffn_down", "gmlp_w_in", "gmlp_b_in", "gmlp_ln_g",
             "gmlp_ln_b", "gmlp_w_s", "gmlp_b_s", "gmlp_w_out", "w_ada_kv", "b_ada_kv", "w_kv", "attn_w_q",
             "attn_rel_bias", "attn_w_o"]
    big_names = ["w_ada", "w_ada_kv"] + stack_names
    small_names = [nm for nm in order if nm not in big_names]
    delta, new_m, new_v = {}, {}, {}

    def adamw_big(nm):
        shp = weights[nm].shape
        two_d = (-1, shp[-1])
        d, a, b = adamw(weights[nm].reshape(two_d), grads[nm].reshape(two_d), ms[nm].reshape(two_d),
                        vs[nm].reshape(two_d), "adamw")
        delta[nm], new_m[nm], new_v[nm] = d.reshape(shp), a.reshape(shp), b.reshape(shp)

    adamw_big("w_ada")
    adamw_big("w_ada_kv")
    shapes = [weights[nm].shape for nm in small_names]
    d, a, b = adamw(_pack_rows([weights[nm] for nm in small_names]), _pack_rows([grads[nm] for nm in small_names]),
                    _pack_rows([ms[nm] for nm in small_names]), _pack_rows([vs[nm] for nm in small_names]),
                    "adamw_small")
    for nm, dd, aa, bb in zip(small_names, _unpack_rows(d, shapes), _unpack_rows(a, shapes), _unpack_rows(b, shapes)):
        delta[nm], new_m[nm], new_v[nm] = dd, aa, bb

    def full_grad(u):
        lo = jnp.where(ci == 0, own_half[u], sib_half[u])
        hi = jnp.where(ci == 0, sib_half[u], own_half[u])
        return jnp.concatenate([lo, hi], axis=1 if u in col_split else 0)

    def adamw_stack(nm):
        si = stack_names.index(nm)
        g = jnp.stack([full_grad(unit_of[(nm, k)]) for k in range(stacks[si].shape[0])])
        grads[nm] = g.reshape(weights[nm].shape)
        adamw_big(nm)

    late = [stack_names[units[u][0]] for u in last["ids"]]
    early = [nm for nm in stack_names if nm not in late]
    finish_grad_exchange(pending, delta["w_ada"])
    collect_halves(delta["w_ada"])
    for nm in early:
        adamw_stack(nm)
    finish_grad_exchange(last, delta[early[-1]])
    collect_halves(delta[early[-1]])
    for nm in stack_names:
        if nm in late:
            adamw_stack(nm)

    return (loss, grad_x, *[grads[nm] for nm in order], *[delta[nm] for nm in order],
            *[new_m[nm] for nm in order], *[new_v[nm] for nm in order])
```

```python
import functools

import jax
import jax.numpy as jnp
from jax import lax
from jax.experimental import pallas as pl
from jax.experimental.pallas import tpu as pltpu

F32 = jnp.float32
BF16 = jnp.bfloat16
MESH = pl.DeviceIdType.MESH
HIGHEST = lax.Precision.HIGHEST

CHUNK = 64
GMLP_WINDOW = 128
GMLP_GROUPS = 8
HEAD_DIM = 64
LEFT_CHUNKS = 8
BAND = (LEFT_CHUNKS + 1) * CHUNK
LEFT_PAD = LEFT_CHUNKS * CHUNK
MAX_REL = 4 * CHUNK
N_REL = (CHUNK - 1) + MAX_REL + 1
LN_EPS = 1e-5
N_MOD = 9
N_DEV = 8
N_CHIP = 4

ADAM_LR = 0.001
ADAM_B1 = 0.9
ADAM_B2 = 0.999
ADAM_EPS = 1e-08
ADAM_WD = 0.01
ADAM_STEP = 10

LANES = 128
ROW_TILE = 256
MATMUL_ROW_TILE = 512
WGRAD_ROWS = 2048
ATTN_CHUNKS_PER_STEP = 16
VMEM_LIMIT_MB = 56

NT = (((1,), (1,)), ((), ()))
TN = (((0,), (0,)), ((), ()))

ANY = pl.BlockSpec(memory_space=pl.ANY)
VMEM_SPEC = pl.BlockSpec(memory_space=pltpu.VMEM)


def _params(semantics=None):
    kw = dict(vmem_limit_bytes=VMEM_LIMIT_MB * 1024 * 1024)
    if semantics is not None:
        kw["dimension_semantics"] = semantics
    return pltpu.CompilerParams(**kw)


def _sigmoid(v):
    return 0.5 * (1.0 + jnp.tanh(0.5 * v))


def _gelu(v):
    return 0.5 * v * (1.0 + lax.erf(v * (2.0 ** -0.5)))


def _gelu_grad(v):
    return 0.5 * (1.0 + lax.erf(v * (2.0 ** -0.5))) + v * jnp.exp(-0.5 * v * v) * ((2.0 * jnp.pi) ** -0.5)


def _row(m):
    return lambda i: (i, 0)


def _fixed2(i):
    return (0, 0)


def _fixed3(i):
    return (0, 0, 0)


def _resident(shape):
    return pl.BlockSpec(shape, _fixed2 if len(shape) == 2 else _fixed3, pipeline_mode=pl.Buffered(1))


def mod_matmul(x, scl, shift, w, bias, out_dtype, name):
    S, D = x.shape
    NS, _, n = w.shape
    tm = min(MATMUL_ROW_TILE, S)
    has_bias = bias is not None

    def body(*refs):
        if has_bias:
            x_ref, scl_ref, sh_ref, w_ref, b_ref, o_ref, h_ref = refs
        else:
            x_ref, scl_ref, sh_ref, w_ref, o_ref, h_ref = refs
        h = (x_ref[...] * (1.0 + scl_ref[...]) + sh_ref[...]).astype(BF16)
        h_ref[...] = h
        for s in range(NS):
            acc = jnp.dot(h, w_ref[s], preferred_element_type=F32)
            if has_bias:
                acc = acc + b_ref[:, s * n:(s + 1) * n]
            o_ref[:, s * n:(s + 1) * n] = acc.astype(out_dtype)

    in_specs = [pl.BlockSpec((tm, D), _row(0)), pl.BlockSpec((1, D), _fixed2), pl.BlockSpec((1, D), _fixed2),
                _resident((NS, D, n))]
    args = [x, scl, shift, w]
    if has_bias:
        in_specs.append(pl.BlockSpec((1, NS * n), _fixed2))
        args.append(bias)
    return pl.pallas_call(
        body, name=name, grid=(S // tm,), in_specs=in_specs,
        out_specs=[pl.BlockSpec((tm, NS * n), _row(0)), pl.BlockSpec((tm, D), _row(0))],
        out_shape=[jax.ShapeDtypeStruct((S, NS * n), out_dtype), jax.ShapeDtypeStruct((S, D), BF16)],
        compiler_params=_params(("parallel",)),
    )(*args)


def matmul_res_ln(a, w, x, gw, lg, lb, alpha, swiglu, name):
    S, D = x.shape
    K = w.shape[0]
    tm = min(ROW_TILE, S)
    ka = a.shape[1]

    def body(a_ref, w_ref, x_ref, gw_ref, lg_ref, lb_ref, xn_ref, xh_ref, rs_ref, y_ref, *act_ref):
        if swiglu:
            g = a_ref[:, :K]
            u = a_ref[:, K:]
            act = g * _sigmoid(g) * u
            act_ref[0][...] = act
        else:
            act = a_ref[...].astype(BF16)
        y = jnp.dot(act, w_ref[...], preferred_element_type=F32)
        z = alpha * x_ref[...] + gw_ref[...] * y
        mu = jnp.mean(z, axis=-1, keepdims=True)
        zc = z - mu
        var = jnp.mean(zc * zc, axis=-1, keepdims=True)
        rstd = lax.rsqrt(var + LN_EPS)
        xhat = zc * rstd
        xn_ref[...] = xhat * lg_ref[...] + lb_ref[...]
        xh_ref[...] = xhat
        rs_ref[...] = rstd
        y_ref[...] = y.astype(BF16)

    vec = pl.BlockSpec((1, D), _fixed2)
    out_specs = [pl.BlockSpec((tm, D), _row(0)), pl.BlockSpec((tm, D), _row(0)), pl.BlockSpec((tm, 1), _row(0)),
                 pl.BlockSpec((tm, D), _row(0))]
    out_shape = [jax.ShapeDtypeStruct((S, D), F32), jax.ShapeDtypeStruct((S, D), F32),
                 jax.ShapeDtypeStruct((S, 1), F32), jax.ShapeDtypeStruct((S, D), BF16)]
    if swiglu:
        out_specs.append(pl.BlockSpec((tm, K), _row(0)))
        out_shape.append(jax.ShapeDtypeStruct((S, K), BF16))
    return pl.pallas_call(
        body, name=name, grid=(S // tm,),
        in_specs=[pl.BlockSpec((tm, ka), _row(0)), _resident((K, D)), pl.BlockSpec((tm, D), _row(0)),
                  vec, vec, vec],
        out_specs=out_specs, out_shape=out_shape,
        compiler_params=_params(("parallel",)),
    )(a, w, x, gw, lg, lb)


def _ln_res_bwd_tile(d, xh_ref, rs_ref, lg_ref, y_ref, gw_ref, wres, alpha, dxa_ref, dy_ref, acc_ref, row0):
    xh = xh_ref[...]
    dxh = d * lg_ref[...]
    m1 = jnp.mean(dxh, axis=-1, keepdims=True)
    m2 = jnp.mean(dxh * xh, axis=-1, keepdims=True)
    dz = rs_ref[...] * (dxh - m1 - xh * m2)
    dxa_ref[...] = alpha * dz
    dy_ref[...] = (gw_ref[...] * dz).astype(BF16)
    acc_ref[row0:row0 + 1, :] += jnp.sum(d * xh, axis=0, keepdims=True)
    acc_ref[row0 + 1:row0 + 2, :] += jnp.sum(d, axis=0, keepdims=True)
    acc_ref[row0 + 2:row0 + 3, :] += jnp.sum((wres * dz) * y_ref[...].astype(F32), axis=0, keepdims=True)


def loss_ln_res_bwd(out, tgt, xhat, rstd, lg, y, gw, wres, alpha, name):
    S, D = out.shape
    tm = min(MATMUL_ROW_TILE, S)

    def body(o_ref, t_ref, xh_ref, rs_ref, lg_ref, y_ref, gw_ref, dxa_ref, dy_ref, acc_ref):
        @pl.when(pl.program_id(0) == 0)
        def _():
            acc_ref[...] = jnp.zeros_like(acc_ref)

        err = o_ref[...] - t_ref[...]
        acc_ref[3:4, :] += jnp.sum(err * err, axis=0, keepdims=True)
        _ln_res_bwd_tile(err * (1.0 / D), xh_ref, rs_ref, lg_ref, y_ref, gw_ref, wres, alpha, dxa_ref, dy_ref,
                         acc_ref, 0)

    vec = pl.BlockSpec((1, D), _fixed2)
    tile = pl.BlockSpec((tm, D), _row(0))
    return pl.pallas_call(
        body, name=name, grid=(S // tm,),
        in_specs=[tile, tile, tile, pl.BlockSpec((tm, 1), _row(0)), vec, tile, vec],
        out_specs=[tile, tile, pl.BlockSpec((8, D), _fixed2)],
        out_shape=[jax.ShapeDtypeStruct((S, D), F32), jax.ShapeDtypeStruct((S, D), BF16),
                   jax.ShapeDtypeStruct((8, D), F32)],
        compiler_params=_params(("arbitrary",)),
    )(out, tgt, xhat, rstd, lg, y, gw)


def ffn_act_bwd(dy, wd, gu, after, name):
    S, D = dy.shape
    K = wd.shape[0]
    tm = min(ROW_TILE, S)

    def body(dy_ref, wd_ref, gu_ref, after_ref, o_ref):
        da = lax.dot_general(dy_ref[...], wd_ref[...], NT, preferred_element_type=F32).astype(BF16)
        g = gu_ref[:, :K]
        u = gu_ref[:, K:]
        sg = _sigmoid(g)
        o_ref[:, :K] = da * u * (sg * (1.0 + g * (1.0 - sg)))
        o_ref[:, K:] = da * (g * sg)

    return pl.pallas_call(
        body, name=name, grid=(S // tm,),
        in_specs=[pl.BlockSpec((tm, D), _row(0)), _resident((K, D)), pl.BlockSpec((tm, 2 * K), _row(0)), ANY],
        out_specs=pl.BlockSpec((tm, 2 * K), _row(0)),
        out_shape=jax.ShapeDtypeStruct((S, 2 * K), BF16),
        compiler_params=_params(("parallel",)),
    )(dy, wd, gu, after)


def matmul_nt(a, w, after, name):
    S, D = a.shape
    K = w.shape[0]
    tm = min(MATMUL_ROW_TILE, S)

    def body(a_ref, w_ref, after_ref, o_ref):
        o_ref[...] = lax.dot_general(a_ref[...], w_ref[...], NT, preferred_element_type=F32).astype(BF16)

    return pl.pallas_call(
        body, name=name, grid=(S // tm,),
        in_specs=[pl.BlockSpec((tm, D), _row(0)), _resident((K, D)), ANY],
        out_specs=pl.BlockSpec((tm, K), _row(0)),
        out_shape=jax.ShapeDtypeStruct((S, K), BF16),
        compiler_params=_params(("parallel",)),
    )(a, w, after)


def dgrad_mod(dpre, w, dxa, xin, scl, prev, alpha, name):
    S, D = xin.shape
    NS, _, n = w.shape
    tm = min(ROW_TILE, S)
    wres = prev[5] if prev is not None else None

    def body(*refs):
        dp_ref, w_ref, dxa_ref, xin_ref, scl_ref = refs[:5]
        acc_ref = refs[-1]

        @pl.when(pl.program_id(0) == 0)
        def _():
            acc_ref[...] = jnp.zeros_like(acc_ref)

        dh = jnp.zeros((tm, D), F32)
        for s in range(NS):
            dh = dh + lax.dot_general(dp_ref[:, s * n:(s + 1) * n].astype(BF16), w_ref[s], NT,
                                      preferred_element_type=F32)
        dx = dxa_ref[...] + dh * (1.0 + scl_ref[...])
        acc_ref[0:1, :] += jnp.sum(dh * xin_ref[...], axis=0, keepdims=True)
        acc_ref[1:2, :] += jnp.sum(dh, axis=0, keepdims=True)
        if prev is None:
            refs[5][...] = dx
        else:
            xh_ref, rs_ref, lg_ref, y_ref, gw_ref, pdxa_ref, pdy_ref = refs[5:12]
            _ln_res_bwd_tile(dx, xh_ref, rs_ref, lg_ref, y_ref, gw_ref, wres, alpha, pdxa_ref, pdy_ref, acc_ref, 2)

    tile = pl.BlockSpec((tm, D), _row(0))
    vec = pl.BlockSpec((1, D), _fixed2)
    in_specs = [pl.BlockSpec((tm, NS * n), _row(0)), _resident((NS, D, n)), tile, tile, vec]
    args = [dpre, w, dxa, xin, scl]
    if prev is None:
        out_specs = [tile]
        out_shape = [jax.ShapeDtypeStruct((S, D), F32)]
    else:
        in_specs += [tile, pl.BlockSpec((tm, 1), _row(0)), vec, tile, vec]
        args += list(prev[:5])
        out_specs = [tile, tile]
        out_shape = [jax.ShapeDtypeStruct((S, D), F32), jax.ShapeDtypeStruct((S, D), BF16)]
    return pl.pallas_call(
        body, name=name, grid=(S // tm,), in_specs=in_specs,
        out_specs=out_specs + [pl.BlockSpec((8, D), _fixed2)],
        out_shape=out_shape + [jax.ShapeDtypeStruct((8, D), F32)],
        compiler_params=_params(("arbitrary",)),
    )(*args)


PAIR_COLLECTIVE_ID = 1
FILL_COLLECTIVE_ID = 2


def wgrad_pair(a, b, J, kb, nb, a_block, b_block, half_idx, name):
    S = b.shape[0]
    ts = min(WGRAD_ROWS, S)
    nsteps = S // ts

    def body(h_ref, a_ref, b_ref, o_ref, acc_ref, send_buf, recv_buf, send_sems, recv_sems):
        jj, si = pl.program_id(0), pl.program_id(1)
        x, y, c = _coords()
        j = lax.rem(jj, J)
        last = si == nsteps - 1

        def copy(blk):
            return pltpu.make_async_remote_copy(
                src_ref=send_buf.at[blk], dst_ref=recv_buf.at[blk], send_sem=send_sems.at[blk],
                recv_sem=recv_sems.at[blk], device_id=(x, y, 1 - c), device_id_type=MESH)

        @pl.when(jnp.logical_and(jj == 0, si == 0))
        def _():
            barrier = pltpu.get_barrier_semaphore()
            pl.semaphore_signal(barrier, inc=1, device_id=(x, y, 1 - c), device_id_type=MESH)
            pl.semaphore_wait(barrier, 1)

        @pl.when(si == 0)
        def _():
            acc_ref[...] = jnp.zeros_like(acc_ref)

        acc_ref[...] += lax.dot_general(a_ref[...], b_ref[...].astype(BF16), TN, preferred_element_type=F32)

        @pl.when(jnp.logical_and(last, jj < J))
        def _():
            send_buf[j] = acc_ref[...].astype(BF16)
            copy(j).start()

        @pl.when(jnp.logical_and(last, jj >= J))
        def _():
            copy(j).wait_recv()
            o_ref[...] = (acc_ref[...] + recv_buf[j].astype(F32)).astype(BF16)

        @pl.when(jnp.logical_and(last, jj == 2 * J - 1))
        def _():
            for blk in range(J):
                copy(blk).wait_send()

    def half(jj, h):
        return jnp.where(jj < J, 1 - h[0], h[0])

    return pl.pallas_call(
        body, name=name,
        grid_spec=pltpu.PrefetchScalarGridSpec(
            num_scalar_prefetch=1, grid=(2 * J, nsteps),
            in_specs=[pl.BlockSpec((ts, kb), lambda jj, s, h: (s, a_block(lax.rem(jj, J), half(jj, h)))),
                      pl.BlockSpec((ts, nb), lambda jj, s, h: (s, b_block(lax.rem(jj, J), half(jj, h))))],
            out_specs=pl.BlockSpec((None, kb, nb), lambda jj, s, h: (jnp.maximum(jj - J, 0), 0, 0)),
            scratch_shapes=[pltpu.VMEM((kb, nb), F32), pltpu.VMEM((J, kb, nb), BF16), pltpu.VMEM((J, kb, nb), BF16),
                            pltpu.SemaphoreType.DMA((J,)), pltpu.SemaphoreType.DMA((J,))]),
        out_shape=jax.ShapeDtypeStruct((J, kb, nb), BF16),
        compiler_params=pltpu.CompilerParams(
            vmem_limit_bytes=VMEM_LIMIT_MB * 1024 * 1024, dimension_semantics=("arbitrary", "arbitrary"),
            collective_id=PAIR_COLLECTIVE_ID),
    )(half_idx, a, b)


def _window_mask():
    t = lax.broadcasted_iota(jnp.int32, (GMLP_WINDOW, GMLP_WINDOW), 0)
    s = lax.broadcasted_iota(jnp.int32, (GMLP_WINDOW, GMLP_WINDOW), 1)
    return ((s // CHUNK) <= (t // CHUNK)).astype(F32)


def sgu_fwd(pre, glg, glb, ws, bst, name):
    S, H2 = pre.shape
    H = H2 // 2
    W, G = GMLP_WINDOW, GMLP_GROUPS
    gd = H // G
    tm = min(ROW_TILE, S)

    def body(pre_ref, glg_ref, glb_ref, ws_ref, bst_ref, q_ref):
        u = _gelu(pre_ref[:, :H])
        v = _gelu(pre_ref[:, H:])
        mu = jnp.mean(v, axis=-1, keepdims=True)
        vc = v - mu
        var = jnp.mean(vc * vc, axis=-1, keepdims=True)
        vn = ((vc * lax.rsqrt(var + LN_EPS)) * glg_ref[...] + glb_ref[...]).astype(BF16)
        mask = _window_mask()
        for g in range(G):
            wsg = (ws_ref[g] * mask).astype(BF16)
            bcol = bst_ref[:, g:g + 1]
            for wi in range(tm // W):
                rows = slice(wi * W, (wi + 1) * W)
                cols = slice(g * gd, (g + 1) * gd)
                s = jnp.dot(wsg, vn[rows, cols], preferred_element_type=F32) + bcol
                q_ref[rows, cols] = (u[rows, cols] * s).astype(BF16)

    return pl.pallas_call(
        body, name=name, grid=(S // tm,),
        in_specs=[pl.BlockSpec((tm, H2), _row(0)), pl.BlockSpec((1, H), _fixed2), pl.BlockSpec((1, H), _fixed2),
                  pl.BlockSpec((G, W, W), _fixed3), pl.BlockSpec((W, G), _fixed2)],
        out_specs=pl.BlockSpec((tm, H), _row(0)),
        out_shape=jax.ShapeDtypeStruct((S, H), BF16),
        compiler_params=_params(("parallel",)),
    )(pre, glg, glb, ws, bst)


def sgu_bwd(dq, pre, glg, glb, ws, bst, name):
    S, H2 = pre.shape
    H = H2 // 2
    W, G = GMLP_WINDOW, GMLP_GROUPS
    gd = H // G
    tm = min(ROW_TILE, S)

    def body(dq_ref, pre_ref, glg_ref, glb_ref, ws_ref, bst_ref,
             dpre_ref, dws_ref, dss_ref, dgl_ref, dbin_ref, du_s, dvn_s):
        @pl.when(pl.program_id(0) == 0)
        def _():
            dws_ref[...] = jnp.zeros_like(dws_ref)
            dss_ref[...] = jnp.zeros_like(dss_ref)
            dgl_ref[...] = jnp.zeros_like(dgl_ref)
            dbin_ref[...] = jnp.zeros_like(dbin_ref)

        pu = pre_ref[:, :H]
        pv = pre_ref[:, H:]
        u = _gelu(pu)
        v = _gelu(pv)
        mu = jnp.mean(v, axis=-1, keepdims=True)
        vc = v - mu
        var = jnp.mean(vc * vc, axis=-1, keepdims=True)
        rstd = lax.rsqrt(var + LN_EPS)
        vhat = vc * rstd
        vn = (vhat * glg_ref[...] + glb_ref[...]).astype(BF16)
        mask = _window_mask()
        for g in range(G):
            wsg = (ws_ref[g] * mask).astype(BF16)
            bcol = bst_ref[:, g:g + 1]
            cols = slice(g * gd, (g + 1) * gd)
            for wi in range(tm // W):
                rows = slice(wi * W, (wi + 1) * W)
                vblk = vn[rows, cols]
                s = jnp.dot(wsg, vblk, preferred_element_type=F32) + bcol
                dqb = dq_ref[rows, cols].astype(F32)
                du_s[rows, cols] = dqb * s
                ds = dqb * u[rows, cols]
                dss_ref[:, cols] += ds
                dsb = ds.astype(BF16)
                dvn_s[rows, cols] = lax.dot_general(wsg, dsb, TN, preferred_element_type=F32)
                dws_ref[g] += lax.dot_general(dsb, vblk, NT, preferred_element_type=F32) * mask
        dvn = dvn_s[...]
        dgl_ref[0:1, :] += jnp.sum(dvn * vhat, axis=0, keepdims=True)
        dgl_ref[1:2, :] += jnp.sum(dvn, axis=0, keepdims=True)
        dvh = dvn * glg_ref[...]
        m1 = jnp.mean(dvh, axis=-1, keepdims=True)
        m2 = jnp.mean(dvh * vhat, axis=-1, keepdims=True)
        dv = rstd * (dvh - m1 - vhat * m2)
        dpu = du_s[...] * _gelu_grad(pu)
        dpv = dv * _gelu_grad(pv)
        dbin_ref[0:1, :H] += jnp.sum(dpu, axis=0, keepdims=True)
        dbin_ref[0:1, H:] += jnp.sum(dpv, axis=0, keepdims=True)
        dpre_ref[:, :H] = dpu.astype(BF16)
        dpre_ref[:, H:] = dpv.astype(BF16)

    return pl.pallas_call(
        body, name=name, grid=(S // tm,),
        in_specs=[pl.BlockSpec((tm, H), _row(0)), pl.BlockSpec((tm, H2), _row(0)), pl.BlockSpec((1, H), _fixed2),
                  pl.BlockSpec((1, H), _fixed2), pl.BlockSpec((G, W, W), _fixed3), pl.BlockSpec((W, G), _fixed2)],
        out_specs=[pl.BlockSpec((tm, H2), _row(0)), pl.BlockSpec((G, W, W), _fixed3), pl.BlockSpec((W, H), _fixed2),
                   pl.BlockSpec((8, H), _fixed2), pl.BlockSpec((8, H2), _fixed2)],
        out_shape=[jax.ShapeDtypeStruct((S, H2), BF16), jax.ShapeDtypeStruct((G, W, W), F32),
                   jax.ShapeDtypeStruct((W, H), F32), jax.ShapeDtypeStruct((8, H), F32),
                   jax.ShapeDtypeStruct((8, H2), F32)],
        scratch_shapes=[pltpu.VMEM((tm, H), F32), pltpu.VMEM((tm, H), F32)],
        compiler_params=_params(("arbitrary",)),
    )(dq, pre, glg, glb, ws, bst)


def group_lane_sum(dss, name):
    W, H = dss.shape
    gd = H // GMLP_GROUPS

    def body(d_ref, o_ref):
        j = lax.broadcasted_iota(jnp.int32, (H, LANES), 0)
        g = lax.broadcasted_iota(jnp.int32, (H, LANES), 1)
        ind = ((j // gd) == g).astype(F32)
        o_ref[...] = jnp.dot(d_ref[...], ind, preferred_element_type=F32, precision=HIGHEST)

    return pl.pallas_call(
        body, name=name, in_specs=[VMEM_SPEC], out_specs=VMEM_SPEC,
        out_shape=jax.ShapeDtypeStruct((W, LANES), F32), compiler_params=_params(),
    )(dss)


def _attn_load(j, cps, q_ref, k_ref, v_ref):
    r = lax.broadcasted_iota(jnp.int32, (CHUNK, BAND), 1)
    chunks = []
    for cc in range(cps):
        start = pl.multiple_of((j * cps + cc) * CHUNK, CHUNK)
        chunks.append((q_ref[cc * CHUNK:(cc + 1) * CHUNK, :], k_ref[pl.ds(start, BAND), :],
                       v_ref[pl.ds(start, BAND), :], (r + start) >= LEFT_PAD))
    return chunks


def _attn_probs(chunks, b_ref, sels, scale):
    qms = [[jnp.where(sel, q2, jnp.zeros_like(q2)) for sel in sels] for q2, _, _, _ in chunks]
    raw = [[lax.dot_general(qm, k2, NT, preferred_element_type=F32) for qm in qms[cc]]
           for cc, (_, k2, _, _) in enumerate(chunks)]
    probs = []
    for cc, (_, _, _, valid) in enumerate(chunks):
        row = []
        for sub in range(2):
            s = jnp.where(valid, raw[cc][sub] * scale + b_ref[sub], -jnp.inf)
            e = jnp.exp(s - jnp.max(s, axis=-1, keepdims=True))
            row.append(e / jnp.sum(e, axis=-1, keepdims=True))
        probs.append(row)
    return qms, probs


def attn_fwd(q, kpad, vpad, bias, name):
    S, D = q.shape
    HP = D // LANES
    cps = min(ATTN_CHUNKS_PER_STEP, S // CHUNK)
    tq = cps * CHUNK
    scale = HEAD_DIM ** -0.5

    def body(q_ref, k_ref, v_ref, b_ref, o_ref):
        sel0 = lax.broadcasted_iota(jnp.int32, (CHUNK, LANES), 1) < HEAD_DIM
        chunks = _attn_load(pl.program_id(1), cps, q_ref, k_ref, v_ref)
        _, probs = _attn_probs(chunks, b_ref, (sel0, jnp.logical_not(sel0)), scale)
        outs = [[jnp.dot(probs[cc][sub].astype(BF16), v2, preferred_element_type=F32) for sub in range(2)]
                for cc, (_, _, v2, _) in enumerate(chunks)]
        o_ref[...] = jnp.concatenate([jnp.where(sel0, o[0], o[1]) for o in outs], axis=0).astype(BF16)

    kv_spec = pl.BlockSpec((S + LEFT_PAD, LANES), lambda h, j: (0, h))
    return pl.pallas_call(
        body, name=name, grid=(HP, S // tq),
        in_specs=[pl.BlockSpec((tq, LANES), lambda h, j: (j, h)), kv_spec, kv_spec,
                  pl.BlockSpec((2, CHUNK, BAND), lambda h, j: (h, 0, 0))],
        out_specs=pl.BlockSpec((tq, LANES), lambda h, j: (j, h)),
        out_shape=jax.ShapeDtypeStruct((S, D), BF16),
        compiler_params=_params(("parallel", "parallel")),
    )(q, kpad, vpad, bias)


def attn_bwd(q, do, kpad, vpad, bias, dk_in, dv_in, name):
    S, D = q.shape
    HP = D // LANES
    NH = 2 * HP
    cps = min(ATTN_CHUNKS_PER_STEP, S // CHUNK)
    tq = cps * CHUNK
    nj = S // tq
    scale = HEAD_DIM ** -0.5

    def body(q_ref, do_ref, k_ref, v_ref, b_ref, dki_ref, dvi_ref, dq_ref, dk_ref, dv_ref, db_ref, dk_acc, dv_acc):
        j = pl.program_id(1)

        @pl.when(j == 0)
        def _():
            dk_acc[:LEFT_PAD, :] = jnp.zeros((LEFT_PAD, LANES), F32)
            dv_acc[:LEFT_PAD, :] = jnp.zeros((LEFT_PAD, LANES), F32)
            dk_acc[LEFT_PAD:, :] = dki_ref[...]
            dv_acc[LEFT_PAD:, :] = dvi_ref[...]
            db_ref[...] = jnp.zeros_like(db_ref)

        sel0 = lax.broadcasted_iota(jnp.int32, (CHUNK, LANES), 1) < HEAD_DIM
        sels = (sel0, jnp.logical_not(sel0))
        chunks = _attn_load(j, cps, q_ref, k_ref, v_ref)
        pairs = [(cc, sub) for cc in range(cps) for sub in range(2)]
        qms, probs = _attn_probs(chunks, b_ref, sels, scale)
        doms = [[jnp.where(sel, do_ref[cc * CHUNK:(cc + 1) * CHUNK, :], jnp.zeros((CHUNK, LANES), BF16))
                 for sel in sels] for cc in range(cps)]
        dps = {(cc, sub): lax.dot_general(doms[cc][sub], chunks[cc][2], NT, preferred_element_type=F32)
               for cc, sub in pairs}
        dss = {}
        for cc, sub in pairs:
            p = probs[cc][sub]
            dss[cc, sub] = p * (dps[cc, sub] - jnp.sum(dps[cc, sub] * p, axis=-1, keepdims=True))
        dsb = {key: ds.astype(BF16) for key, ds in dss.items()}
        dqs = {(cc, sub): jnp.dot(dsb[cc, sub], chunks[cc][1], preferred_element_type=F32) * scale
               for cc, sub in pairs}
        dks = {(cc, sub): lax.dot_general(dsb[cc, sub], qms[cc][sub], TN, preferred_element_type=F32) * scale
               for cc, sub in pairs}
        dvs = {(cc, sub): lax.dot_general(probs[cc][sub].astype(BF16), doms[cc][sub], TN,
                                          preferred_element_type=F32) for cc, sub in pairs}
        dq_ref[...] = jnp.concatenate([jnp.where(sel0, dqs[cc, 0], dqs[cc, 1]) for cc in range(cps)],
                                      axis=0).astype(BF16)
        for sub in range(2):
            total = dss[0, sub]
            for cc in range(1, cps):
                total = total + dss[cc, sub]
            db_ref[sub] += total
        dk_parts = [dks[cc, 0] + dks[cc, 1] for cc in range(cps)]
        dv_parts = [dvs[cc, 0] + dvs[cc, 1] for cc in range(cps)]

        def window(parts):
            blocks = []
            for rb in range(cps - 1 + BAND // CHUNK):
                acc = None
                for cc in range(cps):
                    b = rb - cc
                    if 0 <= b < BAND // CHUNK:
                        piece = parts[cc][b * CHUNK:(b + 1) * CHUNK, :]
                        acc = piece if acc is None else acc + piece
                blocks.append(acc)
            return jnp.concatenate(blocks, axis=0)

        span = pl.ds(pl.multiple_of(j * cps * CHUNK, CHUNK), (cps - 1) * CHUNK + BAND)
        dk_acc[span, :] += window(dk_parts)
        dv_acc[span, :] += window(dv_parts)

        @pl.when(j == nj - 1)
        def _():
            dk_ref[...] = dk_acc[LEFT_PAD:, :]
            dv_ref[...] = dv_acc[LEFT_PAD:, :]

    q_spec = pl.BlockSpec((tq, LANES), lambda h, j: (j, h))
    kv_spec = pl.BlockSpec((S + LEFT_PAD, LANES), lambda h, j: (0, h))
    col_spec = pl.BlockSpec((S, LANES), lambda h, j: (0, h))
    b_spec = pl.BlockSpec((2, CHUNK, BAND), lambda h, j: (h, 0, 0))
    return pl.pallas_call(
        body, name=name, grid=(HP, nj),
        in_specs=[q_spec, q_spec, kv_spec, kv_spec, b_spec, col_spec, col_spec],
        out_specs=[q_spec, col_spec, col_spec, b_spec],
        out_shape=[jax.ShapeDtypeStruct((S, D), BF16), jax.ShapeDtypeStruct((S, D), F32),
                   jax.ShapeDtypeStruct((S, D), F32), jax.ShapeDtypeStruct((NH, CHUNK, BAND), F32)],
        scratch_shapes=[pltpu.VMEM((S + LEFT_PAD, LANES), F32), pltpu.VMEM((S + LEFT_PAD, LANES), F32)],
        compiler_params=_params(("parallel", "arbitrary")),
    )(q, do, kpad, vpad, bias, dk_in, dv_in)


def _rel_onehot(t):
    r = lax.broadcasted_iota(jnp.int32, (BAND, N_REL), 0)
    i = lax.broadcasted_iota(jnp.int32, (BAND, N_REL), 1)
    idx = jnp.clip(t + LEFT_PAD - r, -(CHUNK - 1), MAX_REL) + (CHUNK - 1)
    return (idx == i).astype(BF16)


def _split3(v):
    hi = v.astype(BF16)
    rest = v - hi.astype(F32)
    mid = rest.astype(BF16)
    return hi, mid, (rest - mid.astype(F32)).astype(BF16)


def _dot_onehot(parts, onehot, dims):
    hi, mid, lo = [lax.dot_general(p, onehot, dims, preferred_element_type=F32) for p in parts]
    return (hi + mid) + lo


def bias_expand(rb, name):
    NH = rb.shape[0]

    def body(rb_ref, o_ref):
        parts = _split3(rb_ref[...])

        def step(t, carry):
            o_ref[t] = _dot_onehot(parts, _rel_onehot(t), NT)
            return carry

        lax.fori_loop(0, CHUNK, step, 0)

    return pl.pallas_call(
        body, name=name, in_specs=[VMEM_SPEC], out_specs=VMEM_SPEC,
        out_shape=jax.ShapeDtypeStruct((CHUNK, NH, BAND), F32), compiler_params=_params(),
    )(rb)


def bias_grad(dsum, name):
    NH = dsum.shape[1]

    def body(d_ref, o_ref):
        def step(t, acc):
            return acc + _dot_onehot(_split3(d_ref[t]), _rel_onehot(t), (((1,), (0,)), ((), ())))

        o_ref[...] = lax.fori_loop(0, CHUNK, step, jnp.zeros((NH, N_REL), F32))

    return pl.pallas_call(
        body, name=name, in_specs=[VMEM_SPEC], out_specs=VMEM_SPEC,
        out_shape=jax.ShapeDtypeStruct((NH, N_REL), F32), compiler_params=_params(),
    )(dsum)


def loss_grad(y, tgt, name):
    S, D = y.shape
    tm = min(MATMUL_ROW_TILE, S)

    def body(y_ref, t_ref, d_ref, acc_ref):
        @pl.when(pl.program_id(0) == 0)
        def _():
            acc_ref[...] = jnp.zeros_like(acc_ref)

        err = y_ref[...] - t_ref[...]
        d_ref[...] = err * (1.0 / D)
        acc_ref[0:1, :] += jnp.sum(err * err, axis=0, keepdims=True)

    tile = pl.BlockSpec((tm, D), _row(0))
    return pl.pallas_call(
        body, name=name, grid=(S // tm,), in_specs=[tile, tile],
        out_specs=[tile, pl.BlockSpec((8, D), _fixed2)],
        out_shape=[jax.ShapeDtypeStruct((S, D), F32), jax.ShapeDtypeStruct((8, D), F32)],
        compiler_params=_params(("arbitrary",)),
    )(y, tgt)


def _col_tile(n):
    for t in (768, 512, 256, 128):
        if n % t == 0:
            return t
    return n


def ada_fwd(c_all, w, b, name):
    L, D, n = w.shape
    tn = _col_tile(n)

    def body(c_ref, w_ref, b_ref, o_ref):
        cv = c_ref[...]
        ca = cv * _sigmoid(cv)
        o_ref[...] = jnp.dot(ca, w_ref[...], preferred_element_type=F32, precision=HIGHEST) + b_ref[...]

    return pl.pallas_call(
        body, name=name, grid=(L, n // tn),
        in_specs=[pl.BlockSpec((N_DEV, D), lambda l, j: (0, 0)), pl.BlockSpec((None, D, tn), lambda l, j: (l, 0, j)),
                  pl.BlockSpec((None, 1, tn), lambda l, j: (l, 0, j))],
        out_specs=pl.BlockSpec((None, N_DEV, tn), lambda l, j: (l, 0, j)),
        out_shape=jax.ShapeDtypeStruct((L, N_DEV, n), F32),
        compiler_params=_params(("parallel", "parallel")),
    )(c_all, w, b)


def ada_wgrad(c_all_t, dmod, name):
    L, _, n = dmod.shape
    D = c_all_t.shape[0]
    tn = _col_tile(n)

    def body(c_ref, d_ref, o_ref):
        cv = c_ref[...]
        ca = cv * _sigmoid(cv)
        o_ref[...] = jnp.dot(ca, d_ref[...], preferred_element_type=F32, precision=HIGHEST)

    return pl.pallas_call(
        body, name=name, grid=(L, n // tn),
        in_specs=[pl.BlockSpec((D, N_DEV), lambda l, j: (0, 0)), pl.BlockSpec((None, N_DEV, tn), lambda l, j: (l, 0, j))],
        out_specs=pl.BlockSpec((None, D, tn), lambda l, j: (l, 0, j)),
        out_shape=jax.ShapeDtypeStruct((L, D, n), F32),
        compiler_params=_params(("parallel", "parallel")),
    )(c_all_t, dmod)


ELEMENTWISE_BLOCK_BYTES = 3 * 1024 * 1024


def _elementwise_rows(rows, row_bytes):
    for t in (4096, 2048, 1024, 512, 256, 128, 64, 32, 16):
        if rows % t == 0 and t * row_bytes <= ELEMENTWISE_BLOCK_BYTES:
            return t
    return rows


def sum_leading(a, name):
    n, M, N = a.shape
    tr = _elementwise_rows(M, n * N * 4)

    def body(a_ref, o_ref):
        acc = a_ref[0].astype(F32)
        for i in range(1, n):
            acc = acc + a_ref[i].astype(F32)
        o_ref[...] = acc

    return pl.pallas_call(
        body, name=name, grid=(M // tr,),
        in_specs=[pl.BlockSpec((n, tr, N), lambda i: (0, i, 0))],
        out_specs=pl.BlockSpec((tr, N), _row(0)),
        out_shape=jax.ShapeDtypeStruct((M, N), F32),
        compiler_params=_params(("parallel",)),
    )(a)


def chip_sum(psum, land, chip_idx, transposed, name):
    n, M, N = psum.shape
    tr = M if transposed else _elementwise_rows(M, N * 8)

    def body(s_ref, p_ref, a_ref, b_ref, c_ref, o_ref):
        total = ((p_ref[...].astype(F32) + a_ref[...].astype(F32)) + b_ref[...].astype(F32)) + c_ref[...].astype(F32)
        o_ref[...] = jnp.transpose(total) if transposed else total

    def entry(j):
        return pl.BlockSpec((None, tr, N), lambda i, s: ((s[0] + j) % n, i, 0))

    out_block, out_dims = ((N, tr), (N, M)) if transposed else ((tr, N), (M, N))
    return pl.pallas_call(
        body, name=name,
        grid_spec=pltpu.PrefetchScalarGridSpec(
            num_scalar_prefetch=1, grid=(M // tr,),
            in_specs=[entry(0), entry(1), entry(2), entry(3)],
            out_specs=pl.BlockSpec(out_block, lambda i, s: (0, 0) if transposed else (i, 0))),
        out_shape=jax.ShapeDtypeStruct(out_dims, F32),
        compiler_params=_params(("parallel",)),
    )(chip_idx, psum, land, land, land)


def adamw(w, g, m, v, name):
    M, N = w.shape
    tr = _elementwise_rows(M, N * 4)
    c1 = 1.0 - ADAM_B1 ** ADAM_STEP
    c2 = 1.0 - ADAM_B2 ** ADAM_STEP

    def body(w_ref, g_ref, m_ref, v_ref, d_ref, nm_ref, nv_ref):
        g = g_ref[...]
        nm = ADAM_B1 * m_ref[...] + (1.0 - ADAM_B1) * g
        nv = ADAM_B2 * v_ref[...] + (1.0 - ADAM_B2) * (g * g)
        d_ref[...] = -ADAM_LR * ((nm / c1) / (jnp.sqrt(nv / c2) + ADAM_EPS) + ADAM_WD * w_ref[...])
        nm_ref[...] = nm
        nv_ref[...] = nv

    spec = pl.BlockSpec((tr, N), _row(0))
    shp = jax.ShapeDtypeStruct((M, N), F32)
    return pl.pallas_call(
        body, name=name, grid=(M // tr,), in_specs=[spec] * 4, out_specs=[spec] * 3, out_shape=[shp] * 3,
        compiler_params=_params(("parallel",)),
    )(w, g, m, v)


def _coords():
    return lax.axis_index("x"), lax.axis_index("y"), lax.axis_index("c")


def all_gather8(block, name):
    m_per, n = block.shape

    def body(x_ref, out_ref, send_sems, recv_sems, local_sem):
        x, y, c = _coords()
        me, sibling = (x, y, c), (x, y, 1 - c)
        chips = [(1 - x, y), (x, 1 - y), (1 - x, 1 - y)]

        def rows(px, py, pc):
            return out_ref.at[pl.ds((4 * px + 2 * py + pc) * m_per, m_per), :]

        def copy(k, blk, to, src=None):
            return pltpu.make_async_remote_copy(
                src_ref=rows(*blk) if src is None else src, dst_ref=rows(*blk),
                send_sem=send_sems.at[k], recv_sem=recv_sems.at[k], device_id=to, device_id_type=MESH)

        mine = pltpu.make_async_copy(x_ref, rows(*me), local_sem)
        mine.start()
        first = [copy(0, me, sibling, src=x_ref)]
        first += [copy(1 + j, me, (*chip, c), src=x_ref) for j, chip in enumerate(chips)]
        for cp in first:
            cp.start()
        passed = [copy(4 + j, (*chip, c), sibling) for j, chip in enumerate(chips)]
        for j, chip in enumerate(chips):
            copy(1 + j, (*chip, c), me).wait_recv()
            passed[j].start()
        copy(0, sibling, me).wait_recv()
        for j, chip in enumerate(chips):
            copy(4 + j, (*chip, 1 - c), me).wait_recv()
        for cp in first + passed:
            cp.wait_send()
        mine.wait()

    return pl.pallas_call(
        body, name=name, in_specs=[VMEM_SPEC], out_specs=VMEM_SPEC,
        out_shape=jax.ShapeDtypeStruct((N_DEV * m_per, n), block.dtype),
        scratch_shapes=[pltpu.SemaphoreType.DMA((7,)), pltpu.SemaphoreType.DMA((7,)), pltpu.SemaphoreType.DMA],
        compiler_params=_params(),
    )(block)


def _other_chips(x, y):
    return [(1 - x, y), (x, 1 - y), (1 - x, 1 - y)]


HBM_SPEC = pl.BlockSpec(memory_space=pltpu.HBM)
SEM_SPEC = pl.BlockSpec(memory_space=pltpu.SEMAPHORE)
DATAFLOW = pltpu.SideEffectType.DATAFLOW_SIDE_EFFECTING


def _chip_peers(x, y, c):
    return [(px, py, c) for px, py in _other_chips(x, y)]


def _sibling_peer(x, y, c):
    return [(x, y, 1 - c)]


def _weight_desc(_, k, land_ref, peer, me):
    h = land_ref.shape[1] // 2
    rows = pl.ds(me[2] * h, h)
    mine = land_ref.at[2 * me[0] + me[1], rows, :]
    return mine, mine, land_ref.at[2 * peer[0] + peer[1], rows, :]


def _grad_desc(psum_ref, k, land_ref, peer, me):
    return psum_ref.at[2 * peer[0] + peer[1]], land_ref.at[2 * me[0] + me[1]], land_ref.at[2 * peer[0] + peer[1]]


def _pair_desc(grad_ref, k, land_ref, peer, me):
    h = land_ref.shape[1]
    return grad_ref.at[:, pl.ds(peer[2] * h, h), :], land_ref, land_ref


def _whole_desc(src_ref, k, land_ref, peer, me):
    return src_ref, land_ref, land_ref


def exchange_start(srcs, lands, units, groups, desc, peers, after, name):
    n_s, n_l, n_g = len(srcs), len(lands), len(groups)
    n_p = len(peers(0, 0, 0))

    def body(*refs):
        s_refs, l_refs = refs[:n_s], refs[n_s:n_s + n_l]
        outs = refs[n_s + n_l + 1:]
        sems, token = outs[:2 * n_g], outs[-1]
        me = _coords()
        for g, ids in enumerate(groups):
            for i, u in enumerate(ids):
                si, k = units[u]
                for j, peer in enumerate(peers(*me)):
                    src, dst, _ = desc(s_refs[si] if s_refs else None, k, l_refs[u], peer, me)
                    pltpu.make_async_remote_copy(
                        src_ref=src, dst_ref=dst, send_sem=sems[2 * g].at[n_p * i + j],
                        recv_sem=sems[2 * g + 1].at[n_p * i + j], device_id=peer, device_id_type=MESH).start()
        token[...] = jnp.zeros_like(token)

    arrs = list(srcs) + list(lands)
    sem_shapes = [pltpu.SemaphoreType.DMA((n_p * len(ids),)) for ids in groups for _ in range(2)]
    outs = pl.pallas_call(
        body, name=name,
        in_specs=[HBM_SPEC] * len(arrs) + [ANY],
        out_specs=[SEM_SPEC] * (2 * n_g) + [HBM_SPEC] * len(arrs) + [VMEM_SPEC],
        out_shape=sem_shapes + [pltpu.HBM(a.shape, a.dtype) for a in arrs] + [jax.ShapeDtypeStruct((8, LANES), F32)],
        input_output_aliases={i: 2 * n_g + i for i in range(len(arrs))},
        compiler_params=pltpu.CompilerParams(has_side_effects=DATAFLOW),
    )(*[pltpu.with_memory_space_constraint(a, pltpu.HBM) for a in arrs], after)
    sems = outs[:2 * n_g]
    thru = outs[2 * n_g:2 * n_g + len(arrs)]
    return sems, list(thru[:n_s]), list(thru[n_s:]), outs[-1]


def exchange_wait(srcs, lands, units, send_sem, recv_sem, desc, peers, after, name):
    n_s, n_l = len(srcs), len(lands)
    n_p = len(peers(0, 0, 0))

    def body(*refs):
        s_refs, l_refs = refs[:n_s], refs[n_s:n_s + n_l]
        send_sems, recv_sems = refs[n_s + n_l], refs[n_s + n_l + 1]
        me = _coords()
        for i, (si, k) in enumerate(units):
            for j, peer in enumerate(peers(*me)):
                src, _, mine = desc(s_refs[si] if s_refs else None, k, l_refs[i], peer, me)
                cp = pltpu.make_async_remote_copy(
                    src_ref=src, dst_ref=mine, send_sem=send_sems.at[n_p * i + j], recv_sem=recv_sems.at[n_p * i + j],
                    device_id=peer, device_id_type=MESH)
                cp.wait_send()
                cp.wait_recv()

    arrs = list(srcs) + list(lands)
    outs = pl.pallas_call(
        body, name=name,
        in_specs=[HBM_SPEC] * len(arrs) + [SEM_SPEC, SEM_SPEC, ANY],
        out_specs=[HBM_SPEC] * len(arrs),
        out_shape=[pltpu.HBM(a.shape, a.dtype) for a in arrs],
        input_output_aliases={i: i for i in range(len(arrs))},
        compiler_params=pltpu.CompilerParams(has_side_effects=DATAFLOW),
    )(*arrs, send_sem, recv_sem, after)
    return list(outs[:n_s]), list(outs[n_s:])


def sibling_fill(lands, name):
    n_u = len(lands)

    def body(*refs):
        ins, outs = refs[:n_u], refs[n_u:2 * n_u]
        send_sems, recv_sems = refs[2 * n_u:]
        x, y, c = _coords()
        barrier = pltpu.get_barrier_semaphore()
        pl.semaphore_signal(barrier, inc=1, device_id=(x, y, 1 - c), device_id_type=MESH)
        pl.semaphore_wait(barrier, 1)
        sends = []
        for u in range(n_u):
            h = ins[u].shape[1] // 2
            for j, (px, py) in enumerate(_other_chips(x, y)):
                part = (2 * px + py, pl.ds(c * h, h), slice(None))
                cp = pltpu.make_async_remote_copy(
                    src_ref=ins[u].at[part], dst_ref=outs[u].at[part], send_sem=send_sems.at[3 * u + j],
                    recv_sem=recv_sems.at[3 * u + j], device_id=(x, y, 1 - c), device_id_type=MESH)
                cp.start()
                sends.append(cp)
        for u in range(n_u):
            h = ins[u].shape[1] // 2
            for j, (px, py) in enumerate(_other_chips(x, y)):
                theirs = (2 * px + py, pl.ds((1 - c) * h, h), slice(None))
                pltpu.make_async_remote_copy(
                    src_ref=ins[u].at[theirs], dst_ref=outs[u].at[theirs], send_sem=send_sems.at[3 * u + j],
                    recv_sem=recv_sems.at[3 * u + j], device_id=(x, y, 1 - c), device_id_type=MESH).wait_recv()
        for cp in sends:
            cp.wait_send()

    return pl.pallas_call(
        body, name=name, in_specs=[ANY] * n_u, out_specs=[ANY] * n_u,
        out_shape=[jax.ShapeDtypeStruct(a.shape, a.dtype) for a in lands],
        input_output_aliases={i: i for i in range(n_u)},
        scratch_shapes=[pltpu.SemaphoreType.DMA((3 * n_u,)), pltpu.SemaphoreType.DMA((3 * n_u,))],
        compiler_params=pltpu.CompilerParams(vmem_limit_bytes=VMEM_LIMIT_MB * 1024 * 1024,
                                             collective_id=FILL_COLLECTIVE_ID),
    )(*lands)


def _pack_rows(parts):
    flat = jnp.concatenate([p.reshape(-1).astype(F32) for p in parts])
    n = flat.shape[0]
    padded = -(-n // (8 * LANES)) * (8 * LANES)
    return jnp.pad(flat, (0, padded - n)).reshape(-1, LANES)


def _unpack_rows(packed, shapes):
    flat = packed.reshape(-1)
    out, off = [], 0
    for s in shapes:
        size = 1
        for d in s:
            size *= d
        out.append(flat[off:off + size].reshape(s))
        off += size
    return out


def _shard_last(full, s_me):
    n = full.shape[-1] // N_CHIP
    return lax.dynamic_slice_in_dim(full, s_me * n, n, axis=full.ndim - 1)


def _unshard_last(g):
    moved = jnp.moveaxis(g, 0, -2)
    return moved.reshape(moved.shape[:-2] + (moved.shape[-2] * moved.shape[-1],))


def kernel(x, c, w_ada, b_ada, ln_g, ln_b, ffn_gu, ffn_down, gmlp_w_in, gmlp_b_in, gmlp_ln_g, gmlp_ln_b, gmlp_w_s, gmlp_b_s, gmlp_w_out, w_ada_kv, b_ada_kv, w_kv, attn_w_q, attn_rel_bias, attn_w_o, loss_target, m_w_ada, m_b_ada, m_ln_g, m_ln_b, m_ffn_gu, m_ffn_down, m_gmlp_w_in, m_gmlp_b_in, m_gmlp_ln_g, m_gmlp_ln_b, m_gmlp_w_s, m_gmlp_b_s, m_gmlp_w_out, m_w_ada_kv, m_b_ada_kv, m_w_kv, m_attn_w_q, m_attn_rel_bias, m_attn_w_o, v_w_ada, v_b_ada, v_ln_g, v_ln_b, v_ffn_gu, v_ffn_down, v_gmlp_w_in, v_gmlp_b_in, v_gmlp_ln_g, v_gmlp_ln_b, v_gmlp_w_s, v_gmlp_b_s, v_gmlp_w_out, v_w_ada_kv, v_b_ada_kv, v_w_kv, v_attn_w_q, v_attn_rel_bias, v_attn_w_o):
    xi, yi, ci = _coords()
    s_me = 2 * xi + yi
    dev = 4 * xi + 2 * yi + ci

    x0 = x[0]
    tgt = loss_target[0]
    S, D = x0.shape
    L = w_ada.shape[0]
    NA = gmlp_w_in.shape[0]
    NB = attn_w_q.shape[0]
    NH = D // HEAD_DIM
    alpha = (2.0 * L) ** 0.25
    n_ada = w_ada.shape[2]
    n_kv = w_ada_kv.shape[1]

    stack_names = ["ffn_gu", "ffn_down", "gmlp_w_in", "gmlp_w_out", "w_kv", "attn_w_q", "attn_w_o"]
    stack_src = dict(ffn_gu=ffn_gu, ffn_down=ffn_down, gmlp_w_in=gmlp_w_in, gmlp_w_out=gmlp_w_out, w_kv=w_kv[None],
                     attn_w_q=attn_w_q, attn_w_o=attn_w_o)
    stacks = [stack_src[nm].reshape((-1,) + stack_src[nm].shape[-2:]) for nm in stack_names]
    units = [(si, k) for si, st in enumerate(stacks) for k in range(st.shape[0])]
    unit_of = {(stack_names[si], k): u for u, (si, k) in enumerate(units)}
    weight_groups = [[("ffn_gu", 0)], [("ffn_down", 0)]]
    for l in range(L):
        mixer = [("gmlp_w_in", l), ("gmlp_w_out", l)] if l < NA else [("attn_w_q", l - NA), ("attn_w_o", l - NA)]
        first, last = [("ffn_gu", 2 * l), ("ffn_down", 2 * l)], [("ffn_gu", 2 * l + 1), ("ffn_down", 2 * l + 1)]
        if l == 0:
            weight_groups += [mixer, last]
        else:
            weight_groups += [([("w_kv", 0)] if l == NA else []) + first, mixer, last]
    weight_groups = [[unit_of[n] for n in names] for names in weight_groups]
    group_of = {u: g for g, ids in enumerate(weight_groups) for u in ids}

    c_all = all_gather8(jnp.broadcast_to(c, (8, D)), "ag_c").reshape(N_DEV, 8, D)[:, 0]
    b_ada_sh = lax.dynamic_slice_in_dim(b_ada, s_me * n_ada, n_ada, axis=1)
    b_kv_sh = lax.dynamic_slice_in_dim(b_ada_kv, s_me * n_kv, n_kv, axis=0)
    mod_part = ada_fwd(c_all, w_ada, b_ada_sh[:, None, :], "ada_fwd")
    mkv_part = ada_fwd(c_all, w_ada_kv[None], b_kv_sh[None, None, :], "ada_kv_fwd")
    part = jnp.concatenate([jnp.transpose(mod_part, (1, 0, 2)).reshape(N_DEV, L * n_ada), mkv_part[0]], axis=1)
    width = part.shape[1]
    pad_w = -(-width // LANES) * LANES - width
    all_part = all_gather8(jnp.pad(part, ((0, 0), (0, pad_w))), "ag_mod").reshape(N_DEV, N_DEV, width + pad_w)
    mine = lax.dynamic_index_in_dim(all_part[0::2], dev, axis=1, keepdims=False)
    mod = jnp.transpose(mine[:, :L * n_ada].reshape(N_CHIP, L, n_ada), (1, 0, 2)).reshape(L, N_MOD, D)
    mkv = mine[:, L * n_ada:width].reshape(2, D)

    def mrow(l, k):
        return mod[l, k][None, :]

    small_shapes = [ln_g.shape, ln_b.shape, gmlp_b_in.shape, gmlp_ln_g.shape, gmlp_ln_b.shape, attn_rel_bias.shape]
    small_pack = _pack_rows([ln_g, ln_b, gmlp_b_in, gmlp_ln_g, gmlp_ln_b, attn_rel_bias])
    small_all = all_gather8(small_pack, "ag_small_params").reshape((N_DEV,) + small_pack.shape)[0::2]
    sm = [_unpack_rows(small_all[s], small_shapes) for s in range(N_CHIP)]
    ln_g_f, ln_b_f, b_in_f, gln_g_f, gln_b_f, rel_f = [
        _unshard_last(jnp.stack([sm[s][i] for s in range(N_CHIP)])) for i in range(len(small_shapes))]

    def landing(u):
        si, k = units[u]
        shard = stacks[si][k]
        if group_of[u] < 2:
            shard = lax.optimization_barrier(shard)
        shard = shard.astype(BF16)
        return lax.dynamic_update_slice(lax.empty((N_CHIP,) + shard.shape, BF16), shard[None], (s_me, 0, 0))

    gathers_done = jnp.concatenate([mod.reshape(-1)[:LANES], small_all.reshape(-1)[:LANES]])
    w_sems, lands_t = {}, {}
    for part, groups, name in ((0, weight_groups[:2], "weight_send_start_first"),
                               (1, weight_groups[2:], "weight_send_start_rest")):
        ids = [u for grp in groups for u in grp]
        local = [[ids.index(u) for u in grp] for grp in groups]
        sems, _, lands, gathers_done = exchange_start([], [landing(u) for u in ids], [units[u] for u in ids], local,
                                                      _weight_desc, _chip_peers, gathers_done, name)
        for i, grp in enumerate(groups):
            w_sems[2 * part + i] = (sems[2 * i], sems[2 * i + 1])
        lands_t.update(zip(ids, lands))
    wg = {}
    latest = [gathers_done]

    def W(nm, k):
        u = unit_of[(nm, k)]
        if u not in wg:
            g = group_of[u]
            ids = weight_groups[g]
            _, got = exchange_wait([], [lands_t[v] for v in ids], [units[v] for v in ids], w_sems[g][0],
                                   w_sems[g][1], _weight_desc, _chip_peers, latest[0], "weight_send_wait_%d" % g)
            wg.update(zip(ids, sibling_fill(got, "weight_sibling_fill")))
        return wg[u]

    def Wrows(nm, k):
        w4 = W(nm, k)
        return w4.reshape(w4.shape[0] * w4.shape[1], w4.shape[2])

    bst = [jnp.transpose(gmlp_b_s[j]) for j in range(NA)]
    biases = {}

    def make_bias(j, dep):
        rel, _ = lax.optimization_barrier((rel_f[j], dep))
        biases[j] = jnp.transpose(bias_expand(rel, "bias_expand"), (1, 0, 2))
        return biases[j]

    saved = []
    xc = x0
    kpad = vpad = xkv = None
    for l in range(L):
        if l == 1 and NB > 1:
            latest[0] = make_bias(1, xc)
        if l == NA:
            xkv = xc
            kv, hkv = mod_matmul(xc, mkv[1][None], mkv[0][None], W("w_kv", 0), None, BF16, "kv_proj")
            kpad = jnp.pad(kv[:, :D], ((LEFT_PAD, 0), (0, 0)))
            vpad = jnp.pad(kv[:, D:], ((LEFT_PAD, 0), (0, 0)))
        sv = {}
        for i in (0, 2):
            k = 2 * l + i // 2
            gu, hv = mod_matmul(xc, mrow(l, 3 * i + 1), mrow(l, 3 * i), W("ffn_gu", k), None, BF16, "ffn_up")
            latest[0] = hv
            gw = 0.5 * (1.0 + mrow(l, 3 * i + 2))
            xn, xh, rs, yv, av = matmul_res_ln(gu, Wrows("ffn_down", k), xc, gw, ln_g_f[l, i][None],
                                               ln_b_f[l, i][None], alpha, True, "ffn_down")
            sv[i] = dict(x=xc, h=hv, gu=gu, a=av, xh=xh, rs=rs, y=yv, gw=gw)
            xc = latest[0] = xn
            if i == 0:
                if l == 0 and NB > 0:
                    latest[0] = make_bias(0, xc)
                gw = 1.0 + mrow(l, 5)
                if l < NA:
                    pre, hv = mod_matmul(xc, mrow(l, 4), mrow(l, 3), W("gmlp_w_in", l), b_in_f[l][None], F32,
                                         "gmlp_in")
                    qv = sgu_fwd(pre, gln_g_f[l][None], gln_b_f[l][None], gmlp_w_s[l], bst[l], "sgu_fwd")
                    xn, xh, rs, yv = matmul_res_ln(qv, Wrows("gmlp_w_out", l), xc, gw, ln_g_f[l, 1][None],
                                                   ln_b_f[l, 1][None], alpha, False, "gmlp_out")
                    sv[1] = dict(x=xc, h=hv, pre=pre, a=qv, xh=xh, rs=rs, y=yv, gw=gw)
                else:
                    j = l - NA
                    if j not in biases:
                        make_bias(j, xc)
                    qh, hv = mod_matmul(xc, mrow(l, 4), mrow(l, 3), Wrows("attn_w_q", j)[None], None, BF16, "attn_q")
                    ov = attn_fwd(qh, kpad, vpad, biases[j], "attn_fwd")
                    xn, xh, rs, yv = matmul_res_ln(ov, Wrows("attn_w_o", j), xc, gw, ln_g_f[l, 1][None],
                                                   ln_b_f[l, 1][None], alpha, False, "attn_out")
                    sv[1] = dict(x=xc, h=hv, q=qh, a=ov, xh=xh, rs=rs, y=yv, gw=gw)
                xc = latest[0] = xn
        saved.append(sv)

    loss = dx = None

    gpair = [None] * len(units)
    col_split = {u for u, (si, _) in enumerate(units) if stack_names[si] != "ffn_gu"}
    dmod = [[None] * N_MOD for _ in range(L)]
    d_ln_g = [[None] * 3 for _ in range(L)]
    d_ln_b = [[None] * 3 for _ in range(L)]
    d_b_in, d_gln_g, d_gln_b, d_ws, d_bs, d_rel = ([None] * NA, [None] * NA, [None] * NA, [None] * NA, [None] * NA,
                                                  [None] * NB)
    dk = jnp.zeros((S, D), F32)
    dv = jnp.zeros((S, D), F32)
    dmkv = None

    made = []

    core_idx = ci.astype(jnp.int32).reshape(1)
    chip_idx = s_me.astype(jnp.int32).reshape(1)

    def put(nm, k, a, b, name):
        u = unit_of[(nm, k)]
        rows, cols = stacks[units[u][0]].shape[1:]
        if nm == "ffn_gu":
            g = wgrad_pair(b, a, N_CHIP, cols, rows // 2, lambda j, p: j, lambda j, p: p, core_idx, name)
        elif nm in ("gmlp_w_in", "w_kv"):
            g = wgrad_pair(a, b, N_CHIP, rows, cols // 2, lambda j, p: 0, lambda j, p: 2 * j + p, core_idx, name)
        else:
            g = wgrad_pair(a, b, 1, N_CHIP * rows, cols // 2, lambda j, p: 0, lambda j, p: p, core_idx, name)
        gpair[u] = g.reshape((N_CHIP, -1, g.shape[-1]))
        made.append(u)

    own_half, sib_half = {}, {}
    n_started = [0]

    def start_grad_exchange(ids, after):
        psums = [gpair[u] for u in ids]
        n = len(ids)
        tag = n_started[0]
        n_started[0] += 1
        sems, ps_t, q_t, token = exchange_start(psums, [lax.empty(p.shape, p.dtype) for p in psums],
                                                [(i, 0) for i in range(n)], [list(range(n))], _grad_desc, _chip_peers,
                                                after, "grad_send_start_%d" % tag)
        return dict(ids=ids, tag=tag, sems=sems, ps=ps_t, q=q_t), token

    def finish_grad_exchange(pend, after):
        n = len(pend["ids"])
        ps_t, q = exchange_wait(pend["ps"], pend["q"], [(i, 0) for i in range(n)], pend["sems"][0], pend["sems"][1],
                                _grad_desc, _chip_peers, after, "grad_send_wait_%d" % pend["tag"])
        halves = [chip_sum(ps_t[i], q[i], chip_idx, u not in col_split, "grad_chip_sum")
                  for i, u in enumerate(pend["ids"])]
        sems, h_t, land_t, token = exchange_start(halves, [lax.empty(h.shape, h.dtype) for h in halves],
                                              [(i, 0) for i in range(n)], [list(range(n))], _whole_desc,
                                              _sibling_peer, halves[0], "half_send_start_%d" % pend["tag"])
        swaps.append(dict(ids=pend["ids"], tag=pend["tag"], sems=sems, h=h_t, land=land_t))
        return token

    def collect_halves(after):
        for sw in swaps:
            n = len(sw["ids"])
            h, land = exchange_wait(sw["h"], sw["land"], [(i, 0) for i in range(n)], sw["sems"][0], sw["sems"][1],
                                    _whole_desc, _sibling_peer, after, "half_send_wait_%d" % sw["tag"])
            for u, mine, theirs in zip(sw["ids"], h, land):
                own_half[u], sib_half[u] = mine, theirs
        swaps.clear()

    swaps = []
    pending = None
    started_before = jnp.zeros((8, LANES), F32)

    def ln_inputs(l, i):
        t = saved[l][i]
        return (t["xh"], t["rs"], ln_g_f[l, i][None], t["y"], t["gw"], 1.0 if i == 1 else 0.5)

    def record_ln(l, i, acc, row0):
        d_ln_g[l][i], d_ln_b[l][i], dmod[l][3 * i + 2] = acc[row0], acc[row0 + 1], acc[row0 + 2]

    ln_done = None
    for l in reversed(range(L)):
        if l == NA - 1:
            dkv = jnp.concatenate([dk, dv], axis=1)
            put("w_kv", 0, hkv, dkv, "kv_wgrad")
            pdxa, pdy, acc = dgrad_mod(dkv, W("w_kv", 0), dx, xkv, mkv[1][None], ln_inputs(l, 2), alpha, "kv_dgrad")
            dmkv = jnp.stack([acc[1], acc[0]])
            record_ln(l, 2, acc, 2)
            ln_done = (pdxa, pdy)
        sv = saved[l]
        for i in (2, 1, 0):
            t = sv[i]
            if ln_done is None:
                assert loss is None
                dxa, dy, acc1 = loss_ln_res_bwd(xc, tgt, *ln_inputs(l, i), alpha, "loss_ln_res_bwd")
                loss = lax.psum((0.5 / D) * jnp.sum(acc1[3]), ("x", "y", "c"))
                record_ln(l, i, acc1, 0)
            else:
                dxa, dy = ln_done
                ln_done = None
            before = (l, i - 1) if i > 0 else ((l - 1, 2) if l > 0 and l != NA else None)
            prev = ln_inputs(*before) if before is not None else None
            scl = mrow(l, 3 * i + 1)
            if i != 1:
                k = 2 * l + i // 2
                F = t["gu"].shape[1] // 2
                dgu = ffn_act_bwd(dy, Wrows("ffn_down", k), t["gu"], started_before, "ffn_act_bwd")
                put("ffn_down", k, t["a"], dy, "ffn_down_wgrad")
                put("ffn_gu", k, t["h"], dgu, "ffn_up_wgrad")
                res = dgrad_mod(dgu, W("ffn_gu", k), dxa, t["x"], scl, prev, alpha, "ffn_up_dgrad")
            elif l < NA:
                dq = matmul_nt(dy, Wrows("gmlp_w_out", l), started_before, "gmlp_out_dgrad")
                put("gmlp_w_out", l, t["a"], dy, "gmlp_out_wgrad")
                dpre, dws_l, dss, dgl, dbin = sgu_bwd(dq, t["pre"], gln_g_f[l][None], gln_b_f[l][None], gmlp_w_s[l],
                                                      bst[l], "sgu_bwd")
                d_ws[l] = dws_l
                d_bs[l] = jnp.transpose(group_lane_sum(dss, "sgu_bias_grad")[:, :GMLP_GROUPS])
                d_gln_g[l], d_gln_b[l], d_b_in[l] = dgl[0], dgl[1], dbin[0]
                put("gmlp_w_in", l, t["h"], dpre, "gmlp_in_wgrad")
                res = dgrad_mod(dpre, W("gmlp_w_in", l), dxa, t["x"], scl, prev, alpha, "gmlp_in_dgrad")
            else:
                j = l - NA
                do = matmul_nt(dy, Wrows("attn_w_o", j), started_before, "attn_out_dgrad")
                put("attn_w_o", j, t["a"], dy, "attn_out_wgrad")
                dqh, dk, dv, dbias = attn_bwd(t["q"], do, kpad, vpad, biases[j], dk, dv, "attn_bwd")
                d_rel[j] = bias_grad(jnp.transpose(dbias, (1, 0, 2)), "bias_grad")
                put("attn_w_q", j, t["h"], dqh, "attn_q_wgrad")
                res = dgrad_mod(dqh, Wrows("attn_w_q", j)[None], dxa, t["x"], scl, prev, alpha, "attn_q_dgrad")
            acc2 = res[-1]
            dmod[l][3 * i + 1], dmod[l][3 * i] = acc2[0], acc2[1]
            if before is None:
                dx = res[0]
            else:
                record_ln(*before, acc2, 2)
                ln_done = (res[0], res[1])
            if (i == 0 and l > 0) or (i == 1 and l == 0):
                started, started_before = start_grad_exchange(list(made), acc2)
                made.clear()
                if pending is not None:
                    started_before = started_before + finish_grad_exchange(pending, acc2)
                pending = started
    grad_x = dx[None]

    dvec = _pack_rows([jnp.stack([jnp.stack(r) for r in dmod]), dmkv])
    dvec = lax.optimization_barrier((dvec, [gpair[u] for u in made]))[0]
    n_dvec = L * N_MOD * D + 2 * D
    dall = all_gather8(dvec, "ag_dmod").reshape(N_DEV, -1, LANES)
    db_all = sum_leading(dall, "ada_bias_grad").reshape(-1)[:n_dvec]
    g_b_ada = db_all[:L * N_MOD * D].reshape(L, N_MOD * D)
    g_b_ada_kv = db_all[L * N_MOD * D:]
    dall2 = dall.reshape(N_DEV, -1)[:, :n_dvec]
    dmod_all = dall2[:, :L * N_MOD * D].reshape(N_DEV, L, N_MOD * D)
    dmod_sh = jnp.transpose(lax.dynamic_slice_in_dim(dmod_all, s_me * n_ada, n_ada, axis=2), (1, 0, 2))
    dmkv_sh = lax.dynamic_slice_in_dim(dall2[:, L * N_MOD * D:], s_me * n_kv, n_kv, axis=1)[None]
    c_all_t = jnp.transpose(c_all)
    g_w_ada = ada_wgrad(c_all_t, dmod_sh, "ada_wgrad")
    g_w_ada_kv = ada_wgrad(c_all_t, dmkv_sh, "ada_kv_wgrad")[0]

    small_g = [jnp.stack([jnp.stack(r) for r in d_ln_g]), jnp.stack([jnp.stack(r) for r in d_ln_b]),
               jnp.stack(d_b_in), jnp.stack(d_gln_g), jnp.stack(d_gln_b), jnp.stack(d_rel), jnp.stack(d_bs)]
    sg_shapes = [a.shape for a in small_g]
    sg_pack = _pack_rows(small_g)
    sg_all = all_gather8(sg_pack, "ag_small_grads").reshape(N_DEV, -1, LANES)
    sg_sum = _unpack_rows(sum_leading(sg_all, "small_grad_sum"), sg_shapes)
    g_ln_g, g_ln_b, g_b_in, g_gln_g, g_gln_b, g_rel = [_shard_last(a, s_me) for a in sg_sum[:6]]
    g_bs = sg_sum[6]
    ws_pack = jnp.stack(d_ws).astype(BF16).reshape(-1, LANES)
    ws_all = all_gather8(ws_pack, "ag_spatial_grads").reshape(N_DEV, -1, LANES)
    g_ws = sum_leading(ws_all, "spatial_grad_sum").reshape(gmlp_w_s.shape)

    last, _ = start_grad_exchange(list(made), sg_all)

    grads = dict(w_ada=g_w_ada, b_ada=g_b_ada, ln_g=g_ln_g, ln_b=g_ln_b, gmlp_b_in=g_b_in, gmlp_ln_g=g_gln_g,
                 gmlp_ln_b=g_gln_b, gmlp_w_s=g_ws, gmlp_b_s=g_bs, w_ada_kv=g_w_ada_kv, b_ada_kv=g_b_ada_kv,
                 attn_rel_bias=g_rel)
    weights = dict(w_ada=w_ada, b_ada=b_ada, ln_g=ln_g, ln_b=ln_b, ffn_gu=ffn_gu, ffn_down=ffn_down,
                   gmlp_w_in=gmlp_w_in, gmlp_b_in=gmlp_b_in, gmlp_ln_g=gmlp_ln_g, gmlp_ln_b=gmlp_ln_b,
                   gmlp_w_s=gmlp_w_s, gmlp_b_s=gmlp_b_s, gmlp_w_out=gmlp_w_out, w_ada_kv=w_ada_kv,
                   b_ada_kv=b_ada_kv, w_kv=w_kv, attn_w_q=attn_w_q, attn_rel_bias=attn_rel_bias, attn_w_o=attn_w_o)
    ms = dict(w_ada=m_w_ada, b_ada=m_b_ada, ln_g=m_ln_g, ln_b=m_ln_b, ffn_gu=m_ffn_gu, ffn_down=m_ffn_down,
              gmlp_w_in=m_gmlp_w_in, gmlp_b_in=m_gmlp_b_in, gmlp_ln_g=m_gmlp_ln_g, gmlp_ln_b=m_gmlp_ln_b,
              gmlp_w_s=m_gmlp_w_s, gmlp_b_s=m_gmlp_b_s, gmlp_w_out=m_gmlp_w_out, w_ada_kv=m_w_ada_kv,
              b_ada_kv=m_b_ada_kv, w_kv=m_w_kv, attn_w_q=m_attn_w_q, attn_rel_bias=m_attn_rel_bias,
              attn_w_o=m_attn_w_o)
    vs = dict(w_ada=v_w_ada, b_ada=v_b_ada, ln_g=v_ln_g, ln_b=v_ln_b, ffn_gu=v_ffn_gu, ffn_down=v_ffn_down,
              gmlp_w_in=v_gmlp_w_in, gmlp_b_in=v_gmlp_b_in, gmlp_ln_g=v_gmlp_ln_g, gmlp_ln_b=v_gmlp_ln_b,
              gmlp_w_s=v_gmlp_w_s, gmlp_b_s=v_gmlp_b_s, gmlp_w_out=v_gmlp_w_out, w_ada_kv=v_w_ada_kv,
              b_ada_kv=v_b_ada_kv, w_kv=v_w_kv, attn_w_q=v_attn_w_q, attn_rel_bias=v_attn_rel_bias,
              attn_w_o=v_attn_w_o)
    order = ["w_ada", "b_ada", "ln_g", "ln_b", "ffn_gu", "ffn_down", "gmlp_w_in", "gmlp_b_in", "gmlp_ln_g",
             "gmlp_ln_b", "gmlp_w_s", "gmlp_b_s", "gmlp_w_out", "w_ada_kv", "b_ada_kv", "w_kv", "attn_w_q",
             "attn_rel_bias", "attn_w_o"]
    big_names = ["w_ada", "w_ada_kv"] + stack_names
    small_names = [nm for nm in order if nm not in big_names]
    delta, new_m, new_v = {}, {}, {}

    def adamw_big(nm):
        shp = weights[nm].shape
        two_d = (-1, shp[-1])
        d, a, b = adamw(weights[nm].reshape(two_d), grads[nm].reshape(two_d), ms[nm].reshape(two_d),
                        vs[nm].reshape(two_d), "adamw")
        delta[nm], new_m[nm], new_v[nm] = d.reshape(shp), a.reshape(shp), b.reshape(shp)

    adamw_big("w_ada")
    adamw_big("w_ada_kv")
    shapes = [weights[nm].shape for nm in small_names]
    d, a, b = adamw(_pack_rows([weights[nm] for nm in small_names]), _pack_rows([grads[nm] for nm in small_names]),
                    _pack_rows([ms[nm] for nm in small_names]), _pack_rows([vs[nm] for nm in small_names]),
                    "adamw_small")
    for nm, dd, aa, bb in zip(small_names, _unpack_rows(d, shapes), _unpack_rows(a, shapes), _unpack_rows(b, shapes)):
        delta[nm], new_m[nm], new_v[nm] = dd, aa, bb

    def full_grad(u):
        lo = jnp.where(ci == 0, own_half[u], sib_half[u])
        hi = jnp.where(ci == 0, sib_half[u], own_half[u])
        return jnp.concatenate([lo, hi], axis=1 if u in col_split else 0)

    def adamw_stack(nm):
        si = stack_names.index(nm)
        g = jnp.stack([full_grad(unit_of[(nm, k)]) for k in range(stacks[si].shape[0])])
        grads[nm] = g.reshape(weights[nm].shape)
        adamw_big(nm)

    late = [stack_names[units[u][0]] for u in last["ids"]]
    early = [nm for nm in stack_names if nm not in late]
    finish_grad_exchange(pending, delta["w_ada"])
    collect_halves(delta["w_ada"])
    for nm in early:
        adamw_stack(nm)
    finish_grad_exchange(last, delta[early[-1]])
    collect_halves(delta[early[-1]])
    for nm in stack_names:
        if nm in late:
            adamw_stack(nm)

    return (loss, grad_x, *[grads[nm] for nm in order], *[delta[nm] for nm in order],
            *[new_m[nm] for nm in order], *[new_v[nm] for nm in order])
```
